```python
import math
import jax, jax.numpy as jnp
from jax import lax
import numpy as np

D_MODEL = 1024
BATCH = 16
SEQ = 2048
DEPTH = 2

N_A_LAYERS = DEPTH // 2
N_B_LAYERS = DEPTH - N_A_LAYERS
S5_GROUP = 16
S5_GROUPS = D_MODEL // S5_GROUP
S5_STATE = 64
SB_HEAD_DIM = 64
SB_HEADS = D_MODEL // SB_HEAD_DIM
D_FF = 4 * D_MODEL
Q_BLOCK = 128
EPS = 1e-6
DT_MIN = 1e-3
DT_MAX = 1e-1

kernel_name = "yoco_s5_stickbreaking_hybrid"


def rms_norm(x, g):
    xf = x.astype(jnp.float32)
    y = xf * lax.rsqrt(jnp.mean(xf * xf, axis=-1, keepdims=True) + EPS)
    return (y * g.astype(jnp.float32)).astype(x.dtype)


def modulate(h, shift, scale):
    return h * (1 + scale[:, None, :]) + shift[:, None, :]


def ada_chunks(c, w, b, n):
    m = jnp.einsum('bd,de->be', jax.nn.silu(c), w) + b
    return jnp.split(m, n, axis=-1)


def _ssm_combine(left, right):
    a1r, a1i, b1r, b1i = left
    a2r, a2i, b2r, b2i = right
    ar = a2r * a1r - a2i * a1i
    ai = a2r * a1i + a2i * a1r
    br = a2r * b1r - a2i * b1i + b2r
    bi = a2r * b1i + a2i * b1r + b2i
    return ar, ai, br, bi


def s5_mixer(u, a_re, a_im, log_dt, b_re, b_im, c_re, c_im, d_skip):
    bsz, seq, _ = u.shape
    f32 = jnp.float32
    uf = u.astype(f32).reshape(bsz, seq, S5_GROUPS, S5_GROUP)
    lam_re = a_re.astype(f32)
    lam_im = a_im.astype(f32)
    dt = jnp.exp(log_dt.astype(f32))[:, None]
    mag = jnp.exp(lam_re * dt)
    ab_re = mag * jnp.cos(lam_im * dt)
    ab_im = mag * jnp.sin(lam_im * dt)
    den = lam_re * lam_re + lam_im * lam_im
    nr = ab_re - 1
    ni = ab_im
    f_re = (nr * lam_re + ni * lam_im) / den
    f_im = (ni * lam_re - nr * lam_im) / den
    br = b_re.astype(f32)
    bi = b_im.astype(f32)
    bb_re = f_re[..., None] * br - f_im[..., None] * bi
    bb_im = f_re[..., None] * bi + f_im[..., None] * br
    bu_re = jnp.einsum('bsgh,gph->bsgp', uf, bb_re)
    bu_im = jnp.einsum('bsgh,gph->bsgp', uf, bb_im)
    a_seq_re = jnp.broadcast_to(ab_re, (1, seq) + ab_re.shape)
    a_seq_im = jnp.broadcast_to(ab_im, (1, seq) + ab_im.shape)
    _, _, st_re, st_im = lax.associative_scan(
        _ssm_combine, (a_seq_re, a_seq_im, bu_re, bu_im), axis=1)
    y = (jnp.einsum('bsgp,ghp->bsgh', st_re, c_re.astype(f32))
         - jnp.einsum('bsgp,ghp->bsgh', st_im, c_im.astype(f32)))
    y = y.reshape(bsz, seq, D_MODEL) + d_skip.astype(f32) * u.astype(f32)
    return y.astype(u.dtype)


def stick_breaking_attention(q, k, v):
    seq = q.shape[2]
    scale = 1.0 / math.sqrt(SB_HEAD_DIM)
    outs = []
    for t0 in range(0, seq, Q_BLOCK):
        t1 = t0 + Q_BLOCK
        qb = q[:, :, t0:t1]
        kb = k[:, :, :t1]
        vb = v[:, :, :t1]
        z = jnp.einsum('bhtd,bhsd->bhts', qb, kb).astype(jnp.float32) * scale
        t_idx = jnp.arange(t0, t1)[:, None]
        s_idx = jnp.arange(t1)[None, :]
        strict = s_idx < t_idx
        log_fail = jnp.where(strict, jax.nn.log_sigmoid(-z), 0.0)
        rev = lax.cumsum(log_fail, axis=3, reverse=True)
        after = jnp.concatenate([rev[..., 1:], jnp.zeros_like(rev[..., :1])], axis=-1)
        w = jnp.where(strict, jnp.exp(jax.nn.log_sigmoid(z) + after), 0.0)
        outs.append(jnp.einsum('bhts,bhsd->bhtd', w.astype(v.dtype), vb))
    return jnp.concatenate(outs, axis=2)


def split_heads(t):
    bsz, seq, _ = t.shape
    return t.reshape(bsz, seq, SB_HEADS, SB_HEAD_DIM)


def _fwd_setup_inputs(seed: int = 0) -> dict:
    key = jax.random.key(seed)
    ks = jax.random.split(key, 32)
    f32 = jnp.float32
    D = D_MODEL
    G, P, H = S5_GROUPS, S5_STATE, S5_GROUP
    nrm = lambda k, shape, std: jax.random.normal(k, shape, f32) * std
    x = jax.random.normal(ks[0], (BATCH, SEQ, D), f32)
    c = jax.random.normal(ks[1], (BATCH, D), f32)
    ada_w = nrm(ks[2], (DEPTH, D, 6 * D), 0.5 * D ** -0.5)
    ada_b = nrm(ks[3], (DEPTH, 6 * D), 0.02)
    mix_norm_g = 1.0 + nrm(ks[4], (DEPTH, D), 0.02)
    mlp_norm_g = 1.0 + nrm(ks[5], (DEPTH, D), 0.02)
    mlp_w1 = nrm(ks[6], (DEPTH, D, D_FF), D ** -0.5)
    mlp_w2 = nrm(ks[7], (DEPTH, D_FF, D), D_FF ** -0.5)
    s5_a_re = -0.5 + nrm(ks[8], (N_A_LAYERS, G, P), 0.01)
    s5_a_im = (jnp.float32(math.pi) * jnp.arange(P, dtype=f32))[None, None, :] + nrm(ks[9], (N_A_LAYERS, G, P), 0.01)
    s5_log_dt = jax.random.uniform(ks[10], (N_A_LAYERS, G), f32, math.log(DT_MIN), math.log(DT_MAX))
    s5_b_re = nrm(ks[11], (N_A_LAYERS, G, P, H), (2 * H) ** -0.5)
    s5_b_im = nrm(ks[12], (N_A_LAYERS, G, P, H), (2 * H) ** -0.5)
    s5_c_re = nrm(ks[13], (N_A_LAYERS, G, H, P), P ** -0.5)
    s5_c_im = nrm(ks[14], (N_A_LAYERS, G, H, P), P ** -0.5)
    s5_d = nrm(ks[15], (N_A_LAYERS, D), 1.0)
    s5_w_glu = nrm(ks[16], (N_A_LAYERS, D, 2 * D), D ** -0.5)
    kv_ada_w = nrm(ks[17], (D, 2 * D), 0.5 * D ** -0.5)
    kv_ada_b = nrm(ks[18], (2 * D,), 0.02)
    kv_norm_g = 1.0 + nrm(ks[19], (D,), 0.02)
    w_kv = nrm(ks[20], (D, 2 * D), D ** -0.5)
    k_norm_g = 1.0 + nrm(ks[21], (SB_HEAD_DIM,), 0.02)
    sb_w_q = nrm(ks[22], (N_B_LAYERS, D, D), D ** -0.5)
    q_norm_g = 1.0 + nrm(ks[23], (N_B_LAYERS, SB_HEAD_DIM), 0.02)
    sb_w_o = nrm(ks[24], (N_B_LAYERS, D, D), D ** -0.5)
    return {"x": x, "c": c, "ada_w": ada_w, "ada_b": ada_b,
            "mix_norm_g": mix_norm_g, "mlp_norm_g": mlp_norm_g,
            "mlp_w1": mlp_w1, "mlp_w2": mlp_w2,
            "s5_a_re": s5_a_re, "s5_a_im": s5_a_im, "s5_log_dt": s5_log_dt,
            "s5_b_re": s5_b_re, "s5_b_im": s5_b_im, "s5_c_re": s5_c_re, "s5_c_im": s5_c_im,
            "s5_d": s5_d, "s5_w_glu": s5_w_glu,
            "kv_ada_w": kv_ada_w, "kv_ada_b": kv_ada_b, "kv_norm_g": kv_norm_g,
            "w_kv": w_kv, "k_norm_g": k_norm_g,
            "sb_w_q": sb_w_q, "q_norm_g": q_norm_g, "sb_w_o": sb_w_o}


def _fwd_reference(x, c, ada_w, ada_b, mix_norm_g, mlp_norm_g, mlp_w1, mlp_w2,
              s5_a_re, s5_a_im, s5_log_dt, s5_b_re, s5_b_im, s5_c_re, s5_c_im,
              s5_d, s5_w_glu, kv_ada_w, kv_ada_b, kv_norm_g, w_kv, k_norm_g,
              sb_w_q, q_norm_g, sb_w_o):
    bsz, seq, _ = x.shape
    k_sh = None
    v_sh = None
    for i in range(DEPTH):
        sh_a, sc_a, g_a, sh_m, sc_m, g_m = ada_chunks(c, ada_w[i], ada_b[i], 6)
        if i < N_A_LAYERS:
            j = i
            h = modulate(rms_norm(x, mix_norm_g[i]), sh_a, sc_a)
            y = s5_mixer(h, s5_a_re[j], s5_a_im[j], s5_log_dt[j], s5_b_re[j], s5_b_im[j],
                         s5_c_re[j], s5_c_im[j], s5_d[j])
            val, gate = jnp.split(jnp.einsum('bsd,de->bse', jax.nn.gelu(y), s5_w_glu[j]), 2, axis=-1)
            mix = val * jax.nn.sigmoid(gate)
        else:
            j = i - N_A_LAYERS
            if j == 0:
                kv_shift, kv_scale = ada_chunks(c, kv_ada_w, kv_ada_b, 2)
                hkv = modulate(rms_norm(x, kv_norm_g), kv_shift, kv_scale)
                k_flat, v_flat = jnp.split(jnp.einsum('bsd,de->bse', hkv, w_kv), 2, axis=-1)
                k_sh = rms_norm(split_heads(k_flat), k_norm_g).transpose(0, 2, 1, 3)
                v_sh = split_heads(v_flat).transpose(0, 2, 1, 3)
            h = modulate(rms_norm(x, mix_norm_g[i]), sh_a, sc_a)
            q = jnp.einsum('bsd,de->bse', h, sb_w_q[j])
            q = rms_norm(split_heads(q), q_norm_g[j]).transpose(0, 2, 1, 3)
            o = stick_breaking_attention(q, k_sh, v_sh)
            o = o.transpose(0, 2, 1, 3).reshape(bsz, seq, D_MODEL)
            mix = jnp.einsum('bsd,de->bse', o, sb_w_o[j])
        x = x + g_a[:, None, :] * mix
        h = modulate(rms_norm(x, mlp_norm_g[i]), sh_m, sc_m)
        ff = jnp.einsum('bsf,fd->bsd', jnp.square(jax.nn.relu(jnp.einsum('bsd,df->bsf', h, mlp_w1[i]))), mlp_w2[i])
        x = x + g_m[:, None, :] * ff
    return x


import jax as _jax
import jax.numpy as _jnp

TWIN_FORMAT = 'train_step'
FWD_PARAMS = ['x', 'c', 'ada_w', 'ada_b', 'mix_norm_g', 'mlp_norm_g', 'mlp_w1', 'mlp_w2', 's5_a_re', 's5_a_im', 's5_log_dt', 's5_b_re', 's5_b_im', 's5_c_re', 's5_c_im', 's5_d', 's5_w_glu', 'kv_ada_w', 'kv_ada_b', 'kv_norm_g', 'w_kv', 'k_norm_g', 'sb_w_q', 'q_norm_g', 'sb_w_o']
TWIN_WEIGHTS = ['ada_w', 'ada_b', 'mix_norm_g', 'mlp_norm_g', 'mlp_w1', 'mlp_w2', 's5_a_re', 's5_a_im', 's5_log_dt', 's5_b_re', 's5_b_im', 's5_c_re', 's5_c_im', 's5_d', 's5_w_glu', 'kv_ada_w', 'kv_ada_b', 'kv_norm_g', 'w_kv', 'k_norm_g', 'sb_w_q', 'q_norm_g', 'sb_w_o']
TWIN_DIFF_INPUT = 'x'
TWIN_INPUTS = ['x', 'c', 'ada_w', 'ada_b', 'mix_norm_g', 'mlp_norm_g', 'mlp_w1', 'mlp_w2', 's5_a_re', 's5_a_im', 's5_log_dt', 's5_b_re', 's5_b_im', 's5_c_re', 's5_c_im', 's5_d', 's5_w_glu', 'kv_ada_w', 'kv_ada_b', 'kv_norm_g', 'w_kv', 'k_norm_g', 'sb_w_q', 'q_norm_g', 'sb_w_o', 'loss_target', 'm_ada_w', 'm_ada_b', 'm_mix_norm_g', 'm_mlp_norm_g', 'm_mlp_w1', 'm_mlp_w2', 'm_s5_a_re', 'm_s5_a_im', 'm_s5_log_dt', 'm_s5_b_re', 'm_s5_b_im', 'm_s5_c_re', 'm_s5_c_im', 'm_s5_d', 'm_s5_w_glu', 'm_kv_ada_w', 'm_kv_ada_b', 'm_kv_norm_g', 'm_w_kv', 'm_k_norm_g', 'm_sb_w_q', 'm_q_norm_g', 'm_sb_w_o', 'v_ada_w', 'v_ada_b', 'v_mix_norm_g', 'v_mlp_norm_g', 'v_mlp_w1', 'v_mlp_w2', 'v_s5_a_re', 'v_s5_a_im', 'v_s5_log_dt', 'v_s5_b_re', 'v_s5_b_im', 'v_s5_c_re', 'v_s5_c_im', 'v_s5_d', 'v_s5_w_glu', 'v_kv_ada_w', 'v_kv_ada_b', 'v_kv_norm_g', 'v_w_kv', 'v_k_norm_g', 'v_sb_w_q', 'v_q_norm_g', 'v_sb_w_o']
TWIN_OUTPUTS = ['loss', 'grad_x', 'grad_ada_w', 'grad_ada_b', 'grad_mix_norm_g', 'grad_mlp_norm_g', 'grad_mlp_w1', 'grad_mlp_w2', 'grad_s5_a_re', 'grad_s5_a_im', 'grad_s5_log_dt', 'grad_s5_b_re', 'grad_s5_b_im', 'grad_s5_c_re', 'grad_s5_c_im', 'grad_s5_d', 'grad_s5_w_glu', 'grad_kv_ada_w', 'grad_kv_ada_b', 'grad_kv_norm_g', 'grad_w_kv', 'grad_k_norm_g', 'grad_sb_w_q', 'grad_q_norm_g', 'grad_sb_w_o', 'delta_ada_w', 'delta_ada_b', 'delta_mix_norm_g', 'delta_mlp_norm_g', 'delta_mlp_w1', 'delta_mlp_w2', 'delta_s5_a_re', 'delta_s5_a_im', 'delta_s5_log_dt', 'delta_s5_b_re', 'delta_s5_b_im', 'delta_s5_c_re', 'delta_s5_c_im', 'delta_s5_d', 'delta_s5_w_glu', 'delta_kv_ada_w', 'delta_kv_ada_b', 'delta_kv_norm_g', 'delta_w_kv', 'delta_k_norm_g', 'delta_sb_w_q', 'delta_q_norm_g', 'delta_sb_w_o', 'new_m_ada_w', 'new_m_ada_b', 'new_m_mix_norm_g', 'new_m_mlp_norm_g', 'new_m_mlp_w1', 'new_m_mlp_w2', 'new_m_s5_a_re', 'new_m_s5_a_im', 'new_m_s5_log_dt', 'new_m_s5_b_re', 'new_m_s5_b_im', 'new_m_s5_c_re', 'new_m_s5_c_im', 'new_m_s5_d', 'new_m_s5_w_glu', 'new_m_kv_ada_w', 'new_m_kv_ada_b', 'new_m_kv_norm_g', 'new_m_w_kv', 'new_m_k_norm_g', 'new_m_sb_w_q', 'new_m_q_norm_g', 'new_m_sb_w_o', 'new_v_ada_w', 'new_v_ada_b', 'new_v_mix_norm_g', 'new_v_mlp_norm_g', 'new_v_mlp_w1', 'new_v_mlp_w2', 'new_v_s5_a_re', 'new_v_s5_a_im', 'new_v_s5_log_dt', 'new_v_s5_b_re', 'new_v_s5_b_im', 'new_v_s5_c_re', 'new_v_s5_c_im', 'new_v_s5_d', 'new_v_s5_w_glu', 'new_v_kv_ada_w', 'new_v_kv_ada_b', 'new_v_kv_norm_g', 'new_v_w_kv', 'new_v_k_norm_g', 'new_v_sb_w_q', 'new_v_q_norm_g', 'new_v_sb_w_o']
TWIN_LEAF_KINDS = {'loss': 'loss', 'grad_x': 'grad_x', 'grad_ada_w': 'grad_w', 'grad_ada_b': 'grad_w', 'grad_mix_norm_g': 'grad_w', 'grad_mlp_norm_g': 'grad_w', 'grad_mlp_w1': 'grad_w', 'grad_mlp_w2': 'grad_w', 'grad_s5_a_re': 'grad_w', 'grad_s5_a_im': 'grad_w', 'grad_s5_log_dt': 'grad_w', 'grad_s5_b_re': 'grad_w', 'grad_s5_b_im': 'grad_w', 'grad_s5_c_re': 'grad_w', 'grad_s5_c_im': 'grad_w', 'grad_s5_d': 'grad_w', 'grad_s5_w_glu': 'grad_w', 'grad_kv_ada_w': 'grad_w', 'grad_kv_ada_b': 'grad_w', 'grad_kv_norm_g': 'grad_w', 'grad_w_kv': 'grad_w', 'grad_k_norm_g': 'grad_w', 'grad_sb_w_q': 'grad_w', 'grad_q_norm_g': 'grad_w', 'grad_sb_w_o': 'grad_w', 'delta_ada_w': 'delta_w', 'delta_ada_b': 'delta_w', 'delta_mix_norm_g': 'delta_w', 'delta_mlp_norm_g': 'delta_w', 'delta_mlp_w1': 'delta_w', 'delta_mlp_w2': 'delta_w', 'delta_s5_a_re': 'delta_w', 'delta_s5_a_im': 'delta_w', 'delta_s5_log_dt': 'delta_w', 'delta_s5_b_re': 'delta_w', 'delta_s5_b_im': 'delta_w', 'delta_s5_c_re': 'delta_w', 'delta_s5_c_im': 'delta_w', 'delta_s5_d': 'delta_w', 'delta_s5_w_glu': 'delta_w', 'delta_kv_ada_w': 'delta_w', 'delta_kv_ada_b': 'delta_w', 'delta_kv_norm_g': 'delta_w', 'delta_w_kv': 'delta_w', 'delta_k_norm_g': 'delta_w', 'delta_sb_w_q': 'delta_w', 'delta_q_norm_g': 'delta_w', 'delta_sb_w_o': 'delta_w', 'new_m_ada_w': 'new_m', 'new_m_ada_b': 'new_m', 'new_m_mix_norm_g': 'new_m', 'new_m_mlp_norm_g': 'new_m', 'new_m_mlp_w1': 'new_m', 'new_m_mlp_w2': 'new_m', 'new_m_s5_a_re': 'new_m', 'new_m_s5_a_im': 'new_m', 'new_m_s5_log_dt': 'new_m', 'new_m_s5_b_re': 'new_m', 'new_m_s5_b_im': 'new_m', 'new_m_s5_c_re': 'new_m', 'new_m_s5_c_im': 'new_m', 'new_m_s5_d': 'new_m', 'new_m_s5_w_glu': 'new_m', 'new_m_kv_ada_w': 'new_m', 'new_m_kv_ada_b': 'new_m', 'new_m_kv_norm_g': 'new_m', 'new_m_w_kv': 'new_m', 'new_m_k_norm_g': 'new_m', 'new_m_sb_w_q': 'new_m', 'new_m_q_norm_g': 'new_m', 'new_m_sb_w_o': 'new_m', 'new_v_ada_w': 'new_v', 'new_v_ada_b': 'new_v', 'new_v_mix_norm_g': 'new_v', 'new_v_mlp_norm_g': 'new_v', 'new_v_mlp_w1': 'new_v', 'new_v_mlp_w2': 'new_v', 'new_v_s5_a_re': 'new_v', 'new_v_s5_a_im': 'new_v', 'new_v_s5_log_dt': 'new_v', 'new_v_s5_b_re': 'new_v', 'new_v_s5_b_im': 'new_v', 'new_v_s5_c_re': 'new_v', 'new_v_s5_c_im': 'new_v', 'new_v_s5_d': 'new_v', 'new_v_s5_w_glu': 'new_v', 'new_v_kv_ada_w': 'new_v', 'new_v_kv_ada_b': 'new_v', 'new_v_kv_norm_g': 'new_v', 'new_v_w_kv': 'new_v', 'new_v_k_norm_g': 'new_v', 'new_v_sb_w_q': 'new_v', 'new_v_q_norm_g': 'new_v', 'new_v_sb_w_o': 'new_v'}


def _forward(args):
    return _fwd_reference(*[args[k] for k in FWD_PARAMS])


def _output_shape():
    out = _jax.eval_shape(lambda: _forward(_fwd_setup_inputs(0)))
    return out.shape, out.dtype

N_MICROBATCH = 1
ADAM_LR = 0.001
ADAM_B1 = 0.9
ADAM_B2 = 0.999
ADAM_EPS = 1e-08
ADAM_WD = 0.01
ADAM_STEP = 10
PER_EXAMPLE_BATCH_AXIS = {'x': 0, 'c': 0, 'loss_target': 0}
SHARED_INPUTS = []
_WEIGHT_DTYPES = {'ada_w': _jnp.float32, 'ada_b': _jnp.float32, 'mix_norm_g': _jnp.float32, 'mlp_norm_g': _jnp.float32, 'mlp_w1': _jnp.float32, 'mlp_w2': _jnp.float32, 's5_a_re': _jnp.float32, 's5_a_im': _jnp.float32, 's5_log_dt': _jnp.float32, 's5_b_re': _jnp.float32, 's5_b_im': _jnp.float32, 's5_c_re': _jnp.float32, 's5_c_im': _jnp.float32, 's5_d': _jnp.float32, 's5_w_glu': _jnp.float32, 'kv_ada_w': _jnp.float32, 'kv_ada_b': _jnp.float32, 'kv_norm_g': _jnp.float32, 'w_kv': _jnp.float32, 'k_norm_g': _jnp.float32, 'sb_w_q': _jnp.float32, 'q_norm_g': _jnp.float32, 'sb_w_o': _jnp.float32}
MOMENT_SCALE = {'ada_w': 3.593635e+00, 'ada_b': 7.465662e+00, 'mix_norm_g': 7.386112e-01, 'mlp_norm_g': 1.255814e+01, 'mlp_w1': 4.714002e-01, 'mlp_w2': 1.818513e+00, 's5_a_re': 2.267814e-02, 's5_a_im': 3.047989e-02, 's5_log_dt': 2.491780e+00, 's5_b_re': 2.104153e-02, 's5_b_im': 1.989458e-02, 's5_c_re': 3.115152e-02, 's5_c_im': 3.034652e-02, 's5_d': 7.773720e-01, 's5_w_glu': 3.626978e-01, 'kv_ada_w': 1.089051e+00, 'kv_ada_b': 1.947804e+00, 'kv_norm_g': 1.662929e+00, 'w_kv': 5.356601e-01, 'k_norm_g': 3.114319e+00, 'sb_w_q': 9.513783e-02, 'q_norm_g': 3.098308e+00, 'sb_w_o': 7.052357e-01}


def _to_microbatches(a, axis):
    t = _jnp.moveaxis(a, axis, 0)
    t = t.reshape((N_MICROBATCH, t.shape[0] // N_MICROBATCH) + t.shape[1:])
    return _jnp.moveaxis(t, 1, axis + 1)


def setup_inputs(seed: int = 0) -> dict:
    inp = _fwd_setup_inputs(seed)
    key = _jax.random.fold_in(_jax.random.key(seed), 7919)
    shape, _ = _output_shape()
    out = dict(inp)
    out["loss_target"] = _jax.random.normal(_jax.random.fold_in(key, 0), shape, _jnp.float32)
    for i, name in enumerate(TWIN_WEIGHTS):
        w = inp[name].astype(_jnp.float32)
        if MOMENT_SCALE is None:
            s = _jnp.sqrt(_jnp.mean(_jnp.square(w)) + 1e-30)
        else:
            s = MOMENT_SCALE[name]
        km, kv = _jax.random.split(_jax.random.fold_in(key, i + 1))
        out[name] = w
        out["m_" + name] = s * _jax.random.normal(km, w.shape, _jnp.float32)
        out["v_" + name] = (s * s) * _jax.random.uniform(kv, w.shape, _jnp.float32, 0.5, 1.5)
    if N_MICROBATCH > 1:
        for name, axis in PER_EXAMPLE_BATCH_AXIS.items():
            out[name] = _to_microbatches(out[name], axis)
    return {'x': out['x'], 'c': out['c'], 'ada_w': out['ada_w'], 'ada_b': out['ada_b'], 'mix_norm_g': out['mix_norm_g'], 'mlp_norm_g': out['mlp_norm_g'], 'mlp_w1': out['mlp_w1'], 'mlp_w2': out['mlp_w2'], 's5_a_re': out['s5_a_re'], 's5_a_im': out['s5_a_im'], 's5_log_dt': out['s5_log_dt'], 's5_b_re': out['s5_b_re'], 's5_b_im': out['s5_b_im'], 's5_c_re': out['s5_c_re'], 's5_c_im': out['s5_c_im'], 's5_d': out['s5_d'], 's5_w_glu': out['s5_w_glu'], 'kv_ada_w': out['kv_ada_w'], 'kv_ada_b': out['kv_ada_b'], 'kv_norm_g': out['kv_norm_g'], 'w_kv': out['w_kv'], 'k_norm_g': out['k_norm_g'], 'sb_w_q': out['sb_w_q'], 'q_norm_g': out['q_norm_g'], 'sb_w_o': out['sb_w_o'], 'loss_target': out['loss_target'], 'm_ada_w': out['m_ada_w'], 'm_ada_b': out['m_ada_b'], 'm_mix_norm_g': out['m_mix_norm_g'], 'm_mlp_norm_g': out['m_mlp_norm_g'], 'm_mlp_w1': out['m_mlp_w1'], 'm_mlp_w2': out['m_mlp_w2'], 'm_s5_a_re': out['m_s5_a_re'], 'm_s5_a_im': out['m_s5_a_im'], 'm_s5_log_dt': out['m_s5_log_dt'], 'm_s5_b_re': out['m_s5_b_re'], 'm_s5_b_im': out['m_s5_b_im'], 'm_s5_c_re': out['m_s5_c_re'], 'm_s5_c_im': out['m_s5_c_im'], 'm_s5_d': out['m_s5_d'], 'm_s5_w_glu': out['m_s5_w_glu'], 'm_kv_ada_w': out['m_kv_ada_w'], 'm_kv_ada_b': out['m_kv_ada_b'], 'm_kv_norm_g': out['m_kv_norm_g'], 'm_w_kv': out['m_w_kv'], 'm_k_norm_g': out['m_k_norm_g'], 'm_sb_w_q': out['m_sb_w_q'], 'm_q_norm_g': out['m_q_norm_g'], 'm_sb_w_o': out['m_sb_w_o'], 'v_ada_w': out['v_ada_w'], 'v_ada_b': out['v_ada_b'], 'v_mix_norm_g': out['v_mix_norm_g'], 'v_mlp_norm_g': out['v_mlp_norm_g'], 'v_mlp_w1': out['v_mlp_w1'], 'v_mlp_w2': out['v_mlp_w2'], 'v_s5_a_re': out['v_s5_a_re'], 'v_s5_a_im': out['v_s5_a_im'], 'v_s5_log_dt': out['v_s5_log_dt'], 'v_s5_b_re': out['v_s5_b_re'], 'v_s5_b_im': out['v_s5_b_im'], 'v_s5_c_re': out['v_s5_c_re'], 'v_s5_c_im': out['v_s5_c_im'], 'v_s5_d': out['v_s5_d'], 'v_s5_w_glu': out['v_s5_w_glu'], 'v_kv_ada_w': out['v_kv_ada_w'], 'v_kv_ada_b': out['v_kv_ada_b'], 'v_kv_norm_g': out['v_kv_norm_g'], 'v_w_kv': out['v_w_kv'], 'v_k_norm_g': out['v_k_norm_g'], 'v_sb_w_q': out['v_sb_w_q'], 'v_q_norm_g': out['v_q_norm_g'], 'v_sb_w_o': out['v_sb_w_o']}


def _loss(weights, diff, rest, loss_target):
    with _jax.named_scope("forward"):
        args = {**rest, TWIN_DIFF_INPUT: diff, **{k: w.astype(_WEIGHT_DTYPES[k]) for k, w in weights.items()}}
        y = _forward(args)
    with _jax.named_scope("loss_head"):
        err = _jnp.square(y.astype(_jnp.float32) - loss_target)
        return 0.5 * _jnp.sum(_jnp.mean(err, axis=-1)) if err.ndim else 0.5 * err


def _adamw(w, g, m, v):
    m = ADAM_B1 * m + (1.0 - ADAM_B1) * g
    v = ADAM_B2 * v + (1.0 - ADAM_B2) * _jnp.square(g)
    m_hat = m / (1.0 - ADAM_B1 ** ADAM_STEP)
    v_hat = v / (1.0 - ADAM_B2 ** ADAM_STEP)
    delta = -ADAM_LR * (m_hat / (_jnp.sqrt(v_hat) + ADAM_EPS) + ADAM_WD * w)
    return delta, m, v


def reference(x, c, ada_w, ada_b, mix_norm_g, mlp_norm_g, mlp_w1, mlp_w2, s5_a_re, s5_a_im, s5_log_dt, s5_b_re, s5_b_im, s5_c_re, s5_c_im, s5_d, s5_w_glu, kv_ada_w, kv_ada_b, kv_norm_g, w_kv, k_norm_g, sb_w_q, q_norm_g, sb_w_o, loss_target, m_ada_w, m_ada_b, m_mix_norm_g, m_mlp_norm_g, m_mlp_w1, m_mlp_w2, m_s5_a_re, m_s5_a_im, m_s5_log_dt, m_s5_b_re, m_s5_b_im, m_s5_c_re, m_s5_c_im, m_s5_d, m_s5_w_glu, m_kv_ada_w, m_kv_ada_b, m_kv_norm_g, m_w_kv, m_k_norm_g, m_sb_w_q, m_q_norm_g, m_sb_w_o, v_ada_w, v_ada_b, v_mix_norm_g, v_mlp_norm_g, v_mlp_w1, v_mlp_w2, v_s5_a_re, v_s5_a_im, v_s5_log_dt, v_s5_b_re, v_s5_b_im, v_s5_c_re, v_s5_c_im, v_s5_d, v_s5_w_glu, v_kv_ada_w, v_kv_ada_b, v_kv_norm_g, v_w_kv, v_k_norm_g, v_sb_w_q, v_q_norm_g, v_sb_w_o):
    given = dict(x=x, c=c, ada_w=ada_w, ada_b=ada_b, mix_norm_g=mix_norm_g, mlp_norm_g=mlp_norm_g, mlp_w1=mlp_w1, mlp_w2=mlp_w2, s5_a_re=s5_a_re, s5_a_im=s5_a_im, s5_log_dt=s5_log_dt, s5_b_re=s5_b_re, s5_b_im=s5_b_im, s5_c_re=s5_c_re, s5_c_im=s5_c_im, s5_d=s5_d, s5_w_glu=s5_w_glu, kv_ada_w=kv_ada_w, kv_ada_b=kv_ada_b, kv_norm_g=kv_norm_g, w_kv=w_kv, k_norm_g=k_norm_g, sb_w_q=sb_w_q, q_norm_g=q_norm_g, sb_w_o=sb_w_o, loss_target=loss_target, m_ada_w=m_ada_w, m_ada_b=m_ada_b, m_mix_norm_g=m_mix_norm_g, m_mlp_norm_g=m_mlp_norm_g, m_mlp_w1=m_mlp_w1, m_mlp_w2=m_mlp_w2, m_s5_a_re=m_s5_a_re, m_s5_a_im=m_s5_a_im, m_s5_log_dt=m_s5_log_dt, m_s5_b_re=m_s5_b_re, m_s5_b_im=m_s5_b_im, m_s5_c_re=m_s5_c_re, m_s5_c_im=m_s5_c_im, m_s5_d=m_s5_d, m_s5_w_glu=m_s5_w_glu, m_kv_ada_w=m_kv_ada_w, m_kv_ada_b=m_kv_ada_b, m_kv_norm_g=m_kv_norm_g, m_w_kv=m_w_kv, m_k_norm_g=m_k_norm_g, m_sb_w_q=m_sb_w_q, m_q_norm_g=m_q_norm_g, m_sb_w_o=m_sb_w_o, v_ada_w=v_ada_w, v_ada_b=v_ada_b, v_mix_norm_g=v_mix_norm_g, v_mlp_norm_g=v_mlp_norm_g, v_mlp_w1=v_mlp_w1, v_mlp_w2=v_mlp_w2, v_s5_a_re=v_s5_a_re, v_s5_a_im=v_s5_a_im, v_s5_log_dt=v_s5_log_dt, v_s5_b_re=v_s5_b_re, v_s5_b_im=v_s5_b_im, v_s5_c_re=v_s5_c_re, v_s5_c_im=v_s5_c_im, v_s5_d=v_s5_d, v_s5_w_glu=v_s5_w_glu, v_kv_ada_w=v_kv_ada_w, v_kv_ada_b=v_kv_ada_b, v_kv_norm_g=v_kv_norm_g, v_w_kv=v_w_kv, v_k_norm_g=v_k_norm_g, v_sb_w_q=v_sb_w_q, v_q_norm_g=v_q_norm_g, v_sb_w_o=v_sb_w_o)
    weights = {n: given[n] for n in TWIN_WEIGHTS}
    shared = {n: given[n] for n in SHARED_INPUTS}
    per_example = {n: given[n] for n in ['x', 'c']}
    grad_fn = _jax.value_and_grad(_loss, argnums=(0, 1))

    def one_microbatch(ex, loss_target):
        ex = dict(ex)
        diff = ex.pop(TWIN_DIFF_INPUT)
        return grad_fn(weights, diff, {**shared, **ex}, loss_target)

    if N_MICROBATCH == 1:
        loss, (grad_w, grad_x) = one_microbatch(per_example, given["loss_target"])
    else:
        def body(carry, xs):
            loss_sum, grad_sum = carry
            l_k, (gw_k, gx_k) = one_microbatch(xs[0], xs[1])
            with _jax.named_scope("update"):
                return (loss_sum + l_k, _jax.tree.map(_jnp.add, grad_sum, gw_k)), gx_k

        init = (_jnp.zeros((), _jnp.float32), _jax.tree.map(_jnp.zeros_like, weights))
        (loss, grad_w), grad_x = _jax.lax.scan(body, init, (per_example, given["loss_target"]))
    with _jax.named_scope("update"):
        delta_w, new_m, new_v = {}, {}, {}
        for n in TWIN_WEIGHTS:
            delta_w[n], new_m[n], new_v[n] = _adamw(weights[n], grad_w[n], given["m_" + n], given["v_" + n])
    return (loss, grad_x, *[grad_w[n] for n in TWIN_WEIGHTS], *[delta_w[n] for n in TWIN_WEIGHTS],
            *[new_m[n] for n in TWIN_WEIGHTS], *[new_v[n] for n in TWIN_WEIGHTS])
```

```python
import functools
import math

import jax
import jax.numpy as jnp
from jax import lax
from jax.experimental import pallas as pl
from jax.experimental.pallas import tpu as pltpu

F32 = jnp.float32
BF16 = jnp.bfloat16

N_DEV = 8
N_CHIPS = 4
MESH = pl.DeviceIdType.MESH
ANY = pl.BlockSpec(memory_space=pl.ANY)

LANES = 128
VMEM_LIMIT_BYTES = 48 * 2 ** 20
TILE_BUDGET_BYTES = 4 * 2 ** 20

S5_GROUP = 16
S5_STATE = 64
S5_BLOCK_GROUPS = 16
HEAD_DIM = 64
ATT_BLOCK = 128
EPS = 1e-6

ADAM_LR = 0.001
ADAM_B1 = 0.9
ADAM_B2 = 0.999
ADAM_EPS = 1e-08
ADAM_WD = 0.01
ADAM_STEP = 10


def _cparams(*sem):
    return pltpu.CompilerParams(dimension_semantics=sem, vmem_limit_bytes=VMEM_LIMIT_BYTES)


def _divisor_tile(n, limit, mult):
    best = None
    for t in range(mult, min(n, limit) + 1, mult):
        if n % t == 0:
            best = t
    return best if best is not None else n


def _tile_m(m):
    return _divisor_tile(m, 512 if m >= 1024 else 256, 16)


def all_gather(x, name):
    def body(x_ref, out_ref, send_sems, recv_sems, local_sem):
        ax, ay, ac = lax.axis_index("x"), lax.axis_index("y"), lax.axis_index("c")
        me, sibling = (ax, ay, ac), (ax, ay, 1 - ac)
        chips = [(1 - ax, ay), (ax, 1 - ay), (1 - ax, 1 - ay)]

        def slot(px, py, pc):
            return out_ref.at[4 * px + 2 * py + pc]

        def copy(k, block, to, src=None):
            return pltpu.make_async_remote_copy(
                src_ref=slot(*block) if src is None else src, dst_ref=slot(*block),
                send_sem=send_sems.at[k], recv_sem=recv_sems.at[k], device_id=to, device_id_type=MESH)

        mine = pltpu.make_async_copy(x_ref, slot(*me), local_sem)
        mine.start()
        first = [copy(0, me, sibling, src=x_ref)]
        first += [copy(1 + j, me, (*chip, ac), src=x_ref) for j, chip in enumerate(chips)]
        for cp in first:
            cp.start()
        passed = [copy(4 + j, (*chip, ac), sibling) for j, chip in enumerate(chips)]
        for j, chip in enumerate(chips):
            copy(1 + j, (*chip, ac), me).wait_recv()
            passed[j].start()
        copy(0, sibling, me).wait_recv()
        for j, chip in enumerate(chips):
            copy(4 + j, (*chip, 1 - ac), me).wait_recv()
        for cp in first + passed:
            cp.wait_send()
        mine.wait()

    return pl.pallas_call(
        body, name=name,
        out_shape=jax.ShapeDtypeStruct((N_DEV,) + x.shape, x.dtype),
        in_specs=[ANY], out_specs=ANY,
        scratch_shapes=[pltpu.SemaphoreType.DMA((7,)), pltpu.SemaphoreType.DMA((7,)), pltpu.SemaphoreType.DMA],
    )(x)


def pair_exchange(part, name):
    def body(p_ref, out_ref, send_sems, recv_sems):
        ax, ay, ac = lax.axis_index("x"), lax.axis_index("y"), lax.axis_index("c")
        copies = [pltpu.make_async_remote_copy(
            src_ref=p_ref.at[2 * q + (1 - ac)], dst_ref=out_ref.at[q],
            send_sem=send_sems.at[q], recv_sem=recv_sems.at[q],
            device_id=(ax, ay, 1 - ac), device_id_type=MESH) for q in range(N_CHIPS)]
        for cp in copies:
            cp.start()
        for cp in copies:
            cp.wait_recv()
        for cp in copies:
            cp.wait_send()

    return pl.pallas_call(
        body, name=name,
        out_shape=jax.ShapeDtypeStruct((N_CHIPS,) + part.shape[1:], part.dtype),
        in_specs=[ANY], out_specs=ANY,
        scratch_shapes=[pltpu.SemaphoreType.DMA((N_CHIPS,)), pltpu.SemaphoreType.DMA((N_CHIPS,))],
    )(part)


def chip_exchange(p, name):
    def body(p_ref, out_ref, send_sems, recv_sems, local_sem):
        ax, ay, ac = lax.axis_index("x"), lax.axis_index("y"), lax.axis_index("c")
        my_chip = 2 * ax + ay
        chips = [(1 - ax, ay), (ax, 1 - ay), (1 - ax, 1 - ay)]
        mine = pltpu.make_async_copy(p_ref.at[my_chip], out_ref.at[my_chip], local_sem)
        mine.start()
        sends = [pltpu.make_async_remote_copy(
            src_ref=p_ref.at[2 * px + py], dst_ref=out_ref.at[my_chip],
            send_sem=send_sems.at[k], recv_sem=recv_sems.at[k],
            device_id=(px, py, ac), device_id_type=MESH) for k, (px, py) in enumerate(chips)]
        for cp in sends:
            cp.start()
        for k, (px, py) in enumerate(chips):
            pltpu.make_async_remote_copy(
                src_ref=p_ref.at[my_chip], dst_ref=out_ref.at[2 * px + py],
                send_sem=send_sems.at[k], recv_sem=recv_sems.at[k],
                device_id=(px, py, ac), device_id_type=MESH).wait_recv()
        for cp in sends:
            cp.wait_send()
        mine.wait()

    return pl.pallas_call(
        body, name=name,
        out_shape=jax.ShapeDtypeStruct(p.shape, p.dtype),
        in_specs=[ANY], out_specs=ANY,
        scratch_shapes=[pltpu.SemaphoreType.DMA((3,)), pltpu.SemaphoreType.DMA((3,)), pltpu.SemaphoreType.DMA],
    )(p)


def rows_call(fn, ins, outs, name):
    rows = ins[0].shape[1]
    per_row = sum(a.shape[0] * a.shape[2] * a.dtype.itemsize for a in ins)
    per_row += sum(l * c * jnp.dtype(dt).itemsize for l, c, dt in outs)
    tr = _divisor_tile(rows, max(16, TILE_BUDGET_BYTES // per_row), 16)
    n_in = len(ins)

    def body(*refs):
        vals = fn(*[r[...] for r in refs[:n_in]])
        if not isinstance(vals, (tuple, list)):
            vals = (vals,)
        for r, v in zip(refs[n_in:], vals):
            r[...] = v.astype(r.dtype)

    def spec(l, c):
        return pl.BlockSpec((l, tr, c), lambda i: (0, i, 0))

    res = pl.pallas_call(
        body, name=name, grid=(rows // tr,),
        in_specs=[spec(a.shape[0], a.shape[2]) for a in ins],
        out_specs=[spec(l, c) for l, c, _ in outs],
        out_shape=[jax.ShapeDtypeStruct((l, rows, c), dt) for l, c, dt in outs],
        compiler_params=_cparams("arbitrary"),
    )(*ins)
    return res


def _as_rows(a, lead=0):
    shape = a.shape
    l = int(math.prod(shape[:lead])) if lead else 1
    rest = shape[lead:]
    c = rest[-1] if rest else 1
    r = int(math.prod(rest[:-1])) if len(rest) > 1 else 1
    return a.reshape(l, r, c)


def act_call(fn, ins, outs, name):
    bsz, seq = ins[0].shape[0], ins[0].shape[1]
    per_row = sum(a.shape[2] * a.dtype.itemsize for a in ins if a.shape[1] == seq)
    per_row += sum(c * jnp.dtype(dt).itemsize for c, dt, kind in outs if kind == "tile")
    ts = _divisor_tile(seq, max(16, TILE_BUDGET_BYTES // per_row), 16)
    n_in = len(ins)

    def in_spec(a):
        c = a.shape[2]
        if a.shape[1] == seq:
            return pl.BlockSpec((None, ts, c), lambda b, s: (b, s, 0))
        if a.shape[0] == bsz:
            return pl.BlockSpec((None, 1, c), lambda b, s: (b, 0, 0))
        return pl.BlockSpec((None, 1, c), lambda b, s: (0, 0, 0))

    def out_spec(c, kind):
        if kind == "tile":
            return pl.BlockSpec((None, ts, c), lambda b, s: (b, s, 0))
        if kind == "seq":
            return pl.BlockSpec((None, 1, c), lambda b, s: (b, 0, 0))
        return pl.BlockSpec((None, 1, c), lambda b, s: (0, 0, 0))

    def out_shape(c, dt, kind):
        if kind == "tile":
            return jax.ShapeDtypeStruct((bsz, seq, c), dt)
        return jax.ShapeDtypeStruct((bsz if kind == "seq" else 1, 1, c), dt)

    def accumulate(ref, v, first):
        @pl.when(first)
        def _():
            ref[...] = jnp.zeros_like(ref)

        ref[...] += v.astype(ref.dtype)

    def body(*refs):
        b, s = pl.program_id(0), pl.program_id(1)
        vals = fn(*[r[...] for r in refs[:n_in]])
        if not isinstance(vals, (tuple, list)):
            vals = (vals,)
        for ref, v, (_, _, kind) in zip(refs[n_in:], vals, outs):
            if kind == "tile":
                ref[...] = v.astype(ref.dtype)
            elif kind == "seq":
                accumulate(ref, v, s == 0)
            else:
                accumulate(ref, v, jnp.logical_and(b == 0, s == 0))

    return pl.pallas_call(
        body, name=name, grid=(bsz, seq // ts),
        in_specs=[in_spec(a) for a in ins],
        out_specs=[out_spec(c, kind) for c, _, kind in outs],
        out_shape=[out_shape(*o) for o in outs],
        compiler_params=_cparams("arbitrary", "arbitrary"),
    )(*ins)


def _mm(name, grid, a, a_spec, b, b_spec, dims, out_shape, out_spec, out_dtypes, acc_steps,
        epi=None, extras=(), extra_spec=None):
    n_ex, n_out = len(extras), len(out_dtypes)
    tile = tuple(d for d in out_spec.block_shape if d is not None)

    def body(*refs):
        a_ref, b_ref = refs[0], refs[1]
        ex_refs = refs[2:2 + n_ex]
        o_refs = refs[2 + n_ex:2 + n_ex + n_out]
        p = lax.dot_general(a_ref[...].astype(BF16), b_ref[...].astype(BF16), (dims, ((), ())),
                            preferred_element_type=F32)

        def finish(acc):
            vals = epi(acc, *[r[...] for r in ex_refs]) if epi is not None else (acc,) * n_out
            for r, v in zip(o_refs, vals):
                r[...] = v.astype(r.dtype)

        if not acc_steps:
            finish(p)
        else:
            acc_ref = refs[-1]
            s = pl.program_id(1)

            @pl.when(s == 0)
            def _():
                acc_ref[...] = p

            @pl.when(s > 0)
            def _():
                acc_ref[...] += p

            @pl.when(s == acc_steps - 1)
            def _():
                finish(acc_ref[...])

    res = pl.pallas_call(
        body, name=name, grid=grid,
        in_specs=[a_spec, b_spec] + [extra_spec] * n_ex,
        out_specs=[out_spec] * n_out,
        out_shape=[jax.ShapeDtypeStruct(out_shape, dt) for dt in out_dtypes],
        scratch_shapes=[pltpu.VMEM(tile, F32)] if acc_steps else [],
        compiler_params=_cparams("arbitrary", "arbitrary"),
    )(a, b, *extras)
    return res if n_out > 1 else res[0]


def mm_nn_col(a, w, name, out_dtypes=(F32,), epi=None):
    m, k = a.shape
    ns, _, nb = w.shape
    tm = _tile_m(m)
    return _mm(name, (m // tm, ns), a, pl.BlockSpec((tm, k), lambda i, j: (i, 0)),
               w, pl.BlockSpec((None, k, nb), lambda i, j: (j, 0, 0)), ((1,), (0,)),
               (m, ns * nb), pl.BlockSpec((tm, nb), lambda i, j: (i, j)), out_dtypes, 0, epi)


def mm_nn_row(a, w, name, out_dtypes=(F32,)):
    m = a.shape[0]
    ns, kb, n = w.shape
    tm = _tile_m(m)
    return _mm(name, (m // tm, ns), a, pl.BlockSpec((tm, kb), lambda i, s: (i, s)),
               w, pl.BlockSpec((None, kb, n), lambda i, s: (s, 0, 0)), ((1,), (0,)),
               (m, n), pl.BlockSpec((tm, n), lambda i, s: (i, 0)), out_dtypes, ns)


def mm_nt_col(dc, w, name, out_dtypes=(F32,), epi=None, extras=()):
    m = dc.shape[0]
    ns, k, nb = w.shape
    tm = _tile_m(m)
    spec = pl.BlockSpec((tm, k), lambda i, s: (i, 0))
    return _mm(name, (m // tm, ns), dc, pl.BlockSpec((tm, nb), lambda i, s: (i, s)),
               w, pl.BlockSpec((None, k, nb), lambda i, s: (s, 0, 0)), ((1,), (1,)),
               (m, k), spec, out_dtypes, ns, epi, extras, spec)


def mm_nt_row(dc, w, name, out_dtypes=(F32,), epi=None, extras=()):
    m, n = dc.shape
    ns, kb, _ = w.shape
    tm = _tile_m(m)
    spec = pl.BlockSpec((tm, kb), lambda i, s: (i, s))
    return _mm(name, (m // tm, ns), dc, pl.BlockSpec((tm, n), lambda i, s: (i, 0)),
               w, pl.BlockSpec((None, kb, n), lambda i, s: (s, 0, 0)), ((1,), (1,)),
               (m, ns * kb), spec, out_dtypes, 0, epi, extras, spec)


def mm_tn(a, c, slab, ns, name, out_dtype=BF16):
    m, ka_all = a.shape
    nc_all = c.shape[1]
    ka = ka_all // ns if slab == "a" else ka_all
    nc = nc_all // ns if slab == "c" else nc_all
    tt = _tile_m(m)
    a_spec = pl.BlockSpec((tt, ka), (lambda s, t: (t, s)) if slab == "a" else (lambda s, t: (t, 0)))
    c_spec = pl.BlockSpec((tt, nc), (lambda s, t: (t, s)) if slab == "c" else (lambda s, t: (t, 0)))
    return _mm(name, (ns, m // tt), a, a_spec, c, c_spec, ((0,), (0,)),
               (ns, ka, nc), pl.BlockSpec((None, ka, nc), lambda s, t: (s, 0, 0)), (out_dtype,), m // tt)


def reduce_scatter(part, name):
    shape = part.shape[1:]
    ac = lax.axis_index("c")
    from_sibling = pair_exchange(part, name + "_pair")
    own = lax.dynamic_index_in_dim(part.reshape((N_CHIPS, 2) + shape), ac, axis=1, keepdims=False)
    pair = rows_call(lambda a, b: a.astype(F32) + b.astype(F32),
                     [_as_rows(own, 1), _as_rows(from_sibling, 1)],
                     [(N_CHIPS, shape[-1], BF16)], name + "_pairsum")[0]
    got = chip_exchange(pair.reshape((N_CHIPS,) + shape), name + "_chips")
    total = rows_call(lambda g: jnp.sum(g.astype(F32), axis=0, keepdims=True),
                      [_as_rows(got, 1)], [(1, shape[-1], F32)], name + "_sum")[0]
    return total.reshape(shape)


def all_reduce_small(leaves, name):
    sizes = [int(a.size) for a in leaves]
    flat = jnp.concatenate([a.reshape(-1) for a in leaves])
    total = int(flat.size)
    padded = -(-total // (16 * LANES)) * (16 * LANES)
    flat = jnp.pad(flat, (0, padded - total)).reshape(padded // LANES, LANES)
    gathered = all_gather(flat, name + "_gather")
    summed = rows_call(lambda g: jnp.sum(g, axis=0, keepdims=True), [gathered],
                       [(1, LANES, F32)], name + "_sum")[0].reshape(-1)
    out, at = [], 0
    for a, n in zip(leaves, sizes):
        out.append(summed[at:at + n].reshape(a.shape))
        at += n
    return out


def adamw(w, g, m, v, name):
    c = w.shape[-1] if w.ndim else 1

    def fn(w_, g_, m_, v_):
        nm = ADAM_B1 * m_ + (1.0 - ADAM_B1) * g_
        nv = ADAM_B2 * v_ + (1.0 - ADAM_B2) * (g_ * g_)
        m_hat = nm / (1.0 - ADAM_B1 ** ADAM_STEP)
        v_hat = nv / (1.0 - ADAM_B2 ** ADAM_STEP)
        delta = -ADAM_LR * (m_hat / (jnp.sqrt(v_hat) + ADAM_EPS) + ADAM_WD * w_)
        return delta, nm, nv

    res = rows_call(fn, [_as_rows(t) for t in (w, g.astype(F32), m, v)], [(1, c, F32)] * 3, name)
    return tuple(r.reshape(w.shape) for r in res)


def _rowsum(v):
    return jnp.sum(v, axis=0, keepdims=True)


def _norm_mod(x, g, sh, sc):
    n = x * lax.rsqrt(jnp.mean(x * x, axis=-1, keepdims=True) + EPS)
    return (n * g) * (1.0 + sc) + sh


def _norm_mod_bwd(x, g, sc, dh, dres):
    r = lax.rsqrt(jnp.mean(x * x, axis=-1, keepdims=True) + EPS)
    n = x * r
    dy = dh * (1.0 + sc)
    dn = dy * g
    dx = r * (dn - n * jnp.mean(dn * n, axis=-1, keepdims=True))
    return dres + dx, _rowsum(dh), _rowsum(dh * (n * g)), _rowsum(dy * n)


def _head_mean(v):
    low = lax.broadcasted_iota(jnp.int32, (1, LANES), 1) < HEAD_DIM
    parts = []
    for p in range(v.shape[1] // LANES):
        blk = v[:, p * LANES:(p + 1) * LANES]
        s0 = jnp.sum(jnp.where(low, blk, 0.0), axis=-1, keepdims=True)
        s1 = jnp.sum(jnp.where(low, 0.0, blk), axis=-1, keepdims=True)
        parts.append(jnp.where(low, s0, s1))
    return jnp.concatenate(parts, axis=1) * (1.0 / HEAD_DIM)


def _head_norm(x, g):
    return x * lax.rsqrt(_head_mean(x * x) + EPS) * g


def _head_norm_bwd(x, g, dy):
    r = lax.rsqrt(_head_mean(x * x) + EPS)
    n = x * r
    dn = dy * g
    return r * (dn - n * _head_mean(dn * n)), _rowsum(dy * n)


GELU_C = math.sqrt(2.0 / math.pi)
GELU_A = 0.044715


def _gelu_grad(y):
    t = jnp.tanh(GELU_C * (y + GELU_A * y * y * y))
    return 0.5 * (1.0 + t) + 0.5 * y * (1.0 - t * t) * GELU_C * (1.0 + 3.0 * GELU_A * y * y)


def ada_fwd(c_all, w_cols, b_cols):
    def body(c_ref, w_ref, b_ref, o_ref):
        c = c_ref[...]
        s = (c * jax.nn.sigmoid(c)).astype(BF16)
        o_ref[...] = jnp.dot(s, w_ref[...].astype(BF16), preferred_element_type=F32) + b_ref[...]

    return pl.pallas_call(
        body, name="ada_fwd", out_shape=jax.ShapeDtypeStruct((c_all.shape[0], w_cols.shape[1]), F32),
        compiler_params=pltpu.CompilerParams(vmem_limit_bytes=VMEM_LIMIT_BYTES),
    )(c_all, w_cols, b_cols)


def ada_bwd(c_all, dm_cols, dm_all):
    def body(c_ref, d_ref, all_ref, dw_ref, db_ref):
        c = c_ref[...]
        s = (c * jax.nn.sigmoid(c)).astype(BF16)
        dw_ref[...] = lax.dot_general(s, d_ref[...].astype(BF16), (((0,), (0,)), ((), ())),
                                      preferred_element_type=F32)
        db_ref[...] = jnp.sum(all_ref[...], axis=0, keepdims=True)

    return pl.pallas_call(
        body, name="ada_bwd",
        out_shape=[jax.ShapeDtypeStruct((c_all.shape[1], dm_cols.shape[1]), F32),
                   jax.ShapeDtypeStruct((1, dm_all.shape[1]), F32)],
        compiler_params=pltpu.CompilerParams(vmem_limit_bytes=VMEM_LIMIT_BYTES),
    )(c_all, dm_cols, dm_all)


def _s5_discretise(lam_re, lam_im, log_dt, b_re, b_im):
    dt = jnp.exp(log_dt)
    mag = jnp.exp(lam_re * dt)
    ab_re = mag * jnp.cos(lam_im * dt)
    ab_im = mag * jnp.sin(lam_im * dt)
    den = lam_re * lam_re + lam_im * lam_im
    nr = ab_re - 1.0
    ni = ab_im
    f_re = (nr * lam_re + ni * lam_im) / den
    f_im = (ni * lam_re - nr * lam_im) / den
    bb_re = f_re * b_re - f_im * b_im
    bb_im = f_re * b_im + f_im * b_re
    return ab_re, ab_im, bb_re, bb_im


def s5_prep(lam_re, lam_im, log_dt, b_re, b_im):
    gp, h = b_re.shape

    def body(lr, li, ld, br, bi, o_ar, o_ai, o_br, o_bi):
        res = _s5_discretise(lr[...], li[...], ld[...], br[...], bi[...])
        for r, v in zip((o_ar, o_ai, o_br, o_bi), res):
            r[...] = v

    col, mat = jax.ShapeDtypeStruct((gp, 1), F32), jax.ShapeDtypeStruct((gp, h), F32)
    return pl.pallas_call(body, name="s5_prep", out_shape=[col, col, mat, mat])(lam_re, lam_im, log_dt, b_re, b_im)


def s5_prep_bwd(lam_re, lam_im, log_dt, b_re, b_im, d_ab_re, d_ab_im, d_bb_re, d_bb_im):
    gp, h = b_re.shape

    def body(lr, li, ld, br, bi, g_ar, g_ai, g_br, g_bi, o_lr, o_li, o_ld, o_br, o_bi):
        _, vjp = jax.vjp(_s5_discretise, lr[...], li[...], ld[...], br[...], bi[...])
        res = vjp((g_ar[...], g_ai[...], g_br[...], g_bi[...]))
        for r, v in zip((o_lr, o_li, o_ld, o_br, o_bi), res):
            r[...] = v

    col, mat = jax.ShapeDtypeStruct((gp, 1), F32), jax.ShapeDtypeStruct((gp, h), F32)
    return pl.pallas_call(body, name="s5_prep_bwd", out_shape=[col, col, col, mat, mat])(
        lam_re, lam_im, log_dt, b_re, b_im, d_ab_re, d_ab_im, d_bb_re, d_bb_im)


def _s5_chunk(seq):
    return _divisor_tile(seq, 256, 16)


def s5_fwd(u, bbd_re, bbd_im, cbd_re, cbd_im, ab_re, ab_im, dskip):
    bsz, seq, d = u.shape
    nb, cb, ns = bbd_re.shape
    lc = _s5_chunk(seq)

    def body(u_ref, bre_ref, bim_ref, cre_ref, cim_ref, ar_ref, ai_ref, d_ref, y_ref, sre_ref, sim_ref,
             carry_re, carry_im):
        t = pl.program_id(2)

        @pl.when(t == 0)
        def _():
            carry_re[...] = jnp.zeros_like(carry_re)
            carry_im[...] = jnp.zeros_like(carry_im)

        uf = u_ref[...]
        ub = uf.astype(BF16)
        sre_ref[...] = jnp.dot(ub, bre_ref[...], preferred_element_type=F32)
        sim_ref[...] = jnp.dot(ub, bim_ref[...], preferred_element_type=F32)
        ar, ai = ar_ref[...], ai_ref[...]

        def step(i, c):
            cr, ci = c
            row = pl.ds(i, 1)
            nr = ar * cr - ai * ci + sre_ref[row, :]
            ni = ar * ci + ai * cr + sim_ref[row, :]
            sre_ref[row, :] = nr
            sim_ref[row, :] = ni
            return nr, ni

        cr, ci = lax.fori_loop(0, lc, step, (carry_re[...], carry_im[...]), unroll=8)
        carry_re[...] = cr
        carry_im[...] = ci
        y = jnp.dot(sre_ref[...].astype(BF16), cre_ref[...], preferred_element_type=F32)
        y -= jnp.dot(sim_ref[...].astype(BF16), cim_ref[...], preferred_element_type=F32)
        y_ref[...] = y + d_ref[...] * uf

    chan = pl.BlockSpec((None, lc, cb), lambda n, b, t: (b, t, n))
    state = pl.BlockSpec((None, lc, ns), lambda n, b, t: (b, t, n))
    par = lambda r, c: pl.BlockSpec((None, r, c), lambda n, b, t: (n, 0, 0))
    return pl.pallas_call(
        body, name="s5_fwd", grid=(nb, bsz, seq // lc),
        in_specs=[chan, par(cb, ns), par(cb, ns), par(ns, cb), par(ns, cb), par(1, ns), par(1, ns),
                  pl.BlockSpec((None, 1, cb), lambda n, b, t: (0, 0, n))],
        out_specs=[chan, state, state],
        out_shape=[jax.ShapeDtypeStruct((bsz, seq, d), F32),
                   jax.ShapeDtypeStruct((bsz, seq, nb * ns), F32),
                   jax.ShapeDtypeStruct((bsz, seq, nb * ns), F32)],
        scratch_shapes=[pltpu.VMEM((1, ns), F32), pltpu.VMEM((1, ns), F32)],
        compiler_params=_cparams("arbitrary", "arbitrary", "arbitrary"),
    )(u, bbd_re, bbd_im, cbd_re, cbd_im, ab_re, ab_im, dskip)


def s5_bwd(dy, u, st_re, st_im, bbd_re, bbd_im, cbd_re, cbd_im, ab_re, ab_im, dskip):
    bsz, seq, d = u.shape
    nb, cb, ns = bbd_re.shape
    lc = _s5_chunk(seq)
    nc = seq // lc

    def body(dy_ref, u_ref, sre_ref, sim_ref, bre_ref, bim_ref, cre_ref, cim_ref, ar_ref, ai_ref, d_ref,
             du_ref, dbre_ref, dbim_ref, dcre_ref, dcim_ref, dar_ref, dai_ref, dd_ref,
             g_re, g_im, gs_re, gs_im, carry_re, carry_im):
        b, t = pl.program_id(1), pl.program_id(2)

        @pl.when(jnp.logical_and(b == 0, t == 0))
        def _():
            for r in (dbre_ref, dbim_ref, dcre_ref, dcim_ref, dar_ref, dai_ref, dd_ref):
                r[...] = jnp.zeros_like(r)

        @pl.when(t == 0)
        def _():
            carry_re[...] = jnp.zeros_like(carry_re)
            carry_im[...] = jnp.zeros_like(carry_im)

        dyf, uf = dy_ref[...], u_ref[...]
        dyb, ub = dyf.astype(BF16), uf.astype(BF16)
        nt = (((1,), (1,)), ((), ()))
        tn = (((0,), (0,)), ((), ()))
        g_re[...] = lax.dot_general(dyb, cre_ref[...], nt, preferred_element_type=F32)
        g_im[...] = -lax.dot_general(dyb, cim_ref[...], nt, preferred_element_type=F32)
        ar, ai = ar_ref[...], ai_ref[...]

        def step(k, c):
            cr, ci = c
            row = pl.ds(lc - 1 - k, 1)
            gs_re[row, :] = cr
            gs_im[row, :] = ci
            nr = ar * cr + ai * ci + g_re[row, :]
            ni = ar * ci - ai * cr + g_im[row, :]
            g_re[row, :] = nr
            g_im[row, :] = ni
            return nr, ni

        cr, ci = lax.fori_loop(0, lc, step, (carry_re[...], carry_im[...]), unroll=8)
        carry_re[...] = cr
        carry_im[...] = ci

        sr, si = sre_ref[...], sim_ref[...]
        hr, hi = gs_re[...], gs_im[...]
        dar_ref[...] += _rowsum(hr * sr + hi * si)
        dai_ref[...] += _rowsum(hi * sr - hr * si)
        gr, gi = g_re[...].astype(BF16), g_im[...].astype(BF16)
        du = lax.dot_general(gr, bre_ref[...], nt, preferred_element_type=F32)
        du += lax.dot_general(gi, bim_ref[...], nt, preferred_element_type=F32)
        du_ref[...] = du + d_ref[...] * dyf
        dbre_ref[...] += lax.dot_general(ub, gr, tn, preferred_element_type=F32)
        dbim_ref[...] += lax.dot_general(ub, gi, tn, preferred_element_type=F32)
        dcre_ref[...] += lax.dot_general(sr.astype(BF16), dyb, tn, preferred_element_type=F32)
        dcim_ref[...] -= lax.dot_general(si.astype(BF16), dyb, tn, preferred_element_type=F32)
        dd_ref[...] += _rowsum(dyf * uf)

    chan = pl.BlockSpec((None, lc, cb), lambda n, b, t: (b, nc - 1 - t, n))
    state = pl.BlockSpec((None, lc, ns), lambda n, b, t: (b, nc - 1 - t, n))
    par = lambda r, c: pl.BlockSpec((None, r, c), lambda n, b, t: (n, 0, 0))
    return pl.pallas_call(
        body, name="s5_bwd", grid=(nb, bsz, nc),
        in_specs=[chan, chan, state, state, par(cb, ns), par(cb, ns), par(ns, cb), par(ns, cb),
                  par(1, ns), par(1, ns), pl.BlockSpec((None, 1, cb), lambda n, b, t: (0, 0, n))],
        out_specs=[chan, par(cb, ns), par(cb, ns), par(ns, cb), par(ns, cb), par(1, ns), par(1, ns), par(1, cb)],
        out_shape=[jax.ShapeDtypeStruct((bsz, seq, d), F32),
                   jax.ShapeDtypeStruct((nb, cb, ns), F32), jax.ShapeDtypeStruct((nb, cb, ns), F32),
                   jax.ShapeDtypeStruct((nb, ns, cb), F32), jax.ShapeDtypeStruct((nb, ns, cb), F32),
                   jax.ShapeDtypeStruct((nb, 1, ns), F32), jax.ShapeDtypeStruct((nb, 1, ns), F32),
                   jax.ShapeDtypeStruct((nb, 1, cb), F32)],
        scratch_shapes=[pltpu.VMEM((lc, ns), F32)] * 4 + [pltpu.VMEM((1, ns), F32)] * 2,
        compiler_params=_cparams("arbitrary", "arbitrary", "arbitrary"),
    )(dy, u, st_re, st_im, bbd_re, bbd_im, cbd_re, cbd_im, ab_re, ab_im, dskip)


def _split_sum(v, ones):
    hi = v.astype(BF16)
    lo = (v - hi.astype(F32)).astype(BF16)
    return (jnp.dot(hi, ones, preferred_element_type=F32) + jnp.dot(lo, ones, preferred_element_type=F32))


def _tri(after, inclusive):
    j = lax.broadcasted_iota(jnp.int32, (ATT_BLOCK, ATT_BLOCK), 0)
    s = lax.broadcasted_iota(jnp.int32, (ATT_BLOCK, ATT_BLOCK), 1)
    if after:
        hit = (j >= s) if inclusive else (j > s)
    else:
        hit = (j <= s) if inclusive else (j < s)
    return jnp.where(hit, 1.0, 0.0).astype(BF16)


def _att_weights(qh, kh, strict, run, tri_excl, scale):
    z = lax.dot_general(qh, kh, (((1,), (1,)), ((), ())), preferred_element_type=F32) * scale
    log_fail = jnp.where(strict, jax.nn.log_sigmoid(-z), 0.0)
    after = run + _split_sum(log_fail, tri_excl)
    w = jnp.where(strict, jnp.exp(jax.nn.log_sigmoid(z) + after), 0.0)
    return z, log_fail, w


def _strict_mask(i, kj):
    t = i * ATT_BLOCK + lax.broadcasted_iota(jnp.int32, (ATT_BLOCK, ATT_BLOCK), 0)
    s = kj * ATT_BLOCK + lax.broadcasted_iota(jnp.int32, (ATT_BLOCK, ATT_BLOCK), 1)
    return s < t


def attention_fwd(q, k, v):
    bsz, seq, d = q.shape
    scale = 1.0 / math.sqrt(HEAD_DIM)

    def body(q_ref, k_ref, v_ref, o_ref, tot_ref):
        i = pl.program_id(2)
        tri_excl = _tri(True, False)
        qs = [q_ref[:, h * HEAD_DIM:(h + 1) * HEAD_DIM] for h in range(2)]

        def step(it, carry):
            kj = i - it
            rows = pl.ds(pl.multiple_of(kj * ATT_BLOCK, ATT_BLOCK), ATT_BLOCK)
            kb, vb = k_ref[rows, :], v_ref[rows, :]
            strict = _strict_mask(i, kj)
            out = []
            for h in range(2):
                acc, run = carry[h]
                lanes = slice(h * HEAD_DIM, (h + 1) * HEAD_DIM)
                _, log_fail, w = _att_weights(qs[h], kb[:, lanes], strict, run, tri_excl, scale)
                acc = acc + jnp.dot(w.astype(BF16), vb[:, lanes], preferred_element_type=F32)
                out.append((acc, run + jnp.sum(log_fail, axis=-1, keepdims=True)))
            return tuple(out)

        init = tuple((jnp.zeros((ATT_BLOCK, HEAD_DIM), F32), jnp.zeros((ATT_BLOCK, 1), F32)) for _ in range(2))
        res = lax.fori_loop(0, i + 1, step, init)
        o_ref[...] = jnp.concatenate([res[0][0], res[1][0]], axis=1)
        tot_ref[...] = jnp.concatenate(
            [jnp.broadcast_to(res[h][1], (ATT_BLOCK, HEAD_DIM)) for h in range(2)], axis=1)

    blk = pl.BlockSpec((None, ATT_BLOCK, LANES), lambda b, p, i: (b, i, p))
    full = pl.BlockSpec((None, seq, LANES), lambda b, p, i: (b, 0, p))
    shape = jax.ShapeDtypeStruct((bsz, seq, d), F32)
    return pl.pallas_call(
        body, name="attention_fwd", grid=(bsz, d // LANES, seq // ATT_BLOCK),
        in_specs=[blk, full, full], out_specs=[blk, blk],
        out_shape=[shape, shape],
        compiler_params=_cparams("arbitrary", "arbitrary", "arbitrary"),
    )(q, k, v)


def attention_bwd(q, k, v, tot, do):
    bsz, seq, d = q.shape
    scale = 1.0 / math.sqrt(HEAD_DIM)

    def body(q_ref, k_ref, v_ref, tot_ref, do_ref, dq_ref, dk_ref, dv_ref):
        i = pl.program_id(2)

        @pl.when(i == 0)
        def _():
            dk_ref[...] = jnp.zeros_like(dk_ref)
            dv_ref[...] = jnp.zeros_like(dv_ref)

        upto_incl, upto_excl = _tri(False, True), _tri(False, False)
        nt = (((1,), (1,)), ((), ()))
        tn = (((0,), (0,)), ((), ()))
        qs, dos, tots = [], [], []
        for h in range(2):
            lanes = slice(h * HEAD_DIM, (h + 1) * HEAD_DIM)
            qs.append(q_ref[:, lanes])
            dos.append(do_ref[:, lanes].astype(BF16))
            tots.append(tot_ref[:, h * HEAD_DIM:h * HEAD_DIM + 1])

        def step(kj, carry):
            rows = pl.ds(pl.multiple_of(kj * ATT_BLOCK, ATT_BLOCK), ATT_BLOCK)
            kb, vb = k_ref[rows, :], v_ref[rows, :]
            strict = _strict_mask(i, kj)
            out, dks, dvs = [], [], []
            for h in range(2):
                dq, run, erun = carry[h]
                lanes = slice(h * HEAD_DIM, (h + 1) * HEAD_DIM)
                kh, vh = kb[:, lanes], vb[:, lanes]
                z = lax.dot_general(qs[h], kh, nt, preferred_element_type=F32) * scale
                log_fail = jnp.where(strict, jax.nn.log_sigmoid(-z), 0.0)
                after = tots[h] - (run + _split_sum(log_fail, upto_incl))
                w = jnp.where(strict, jnp.exp(jax.nn.log_sigmoid(z) + after), 0.0)
                beta = jax.nn.sigmoid(z)
                dw = lax.dot_general(dos[h], vh, nt, preferred_element_type=F32)
                e = dw * w
                before = erun + _split_sum(e, upto_excl)
                dz = (jnp.where(strict, e * (1.0 - beta) - beta * before, 0.0) * scale).astype(BF16)
                dq = dq + jnp.dot(dz, kh, preferred_element_type=F32)
                dks.append(lax.dot_general(dz, qs[h], tn, preferred_element_type=F32))
                dvs.append(lax.dot_general(w.astype(BF16), dos[h], tn, preferred_element_type=F32))
                out.append((dq, run + jnp.sum(log_fail, axis=-1, keepdims=True),
                            erun + jnp.sum(e, axis=-1, keepdims=True)))
            dk_ref[rows, :] += jnp.concatenate(dks, axis=1)
            dv_ref[rows, :] += jnp.concatenate(dvs, axis=1)
            return tuple(out)

        col = jnp.zeros((ATT_BLOCK, 1), F32)
        init = tuple((jnp.zeros((ATT_BLOCK, HEAD_DIM), F32), col, col) for _ in range(2))
        res = lax.fori_loop(0, i + 1, step, init)
        dq_ref[...] = jnp.concatenate([res[0][0], res[1][0]], axis=1)

    blk = pl.BlockSpec((None, ATT_BLOCK, LANES), lambda b, p, i: (b, i, p))
    full = pl.BlockSpec((None, seq, LANES), lambda b, p, i: (b, 0, p))
    shape = jax.ShapeDtypeStruct((bsz, seq, d), F32)
    return pl.pallas_call(
        body, name="attention_bwd", grid=(bsz, d // LANES, seq // ATT_BLOCK),
        in_specs=[blk, full, full, blk, blk], out_specs=[blk, full, full],
        out_shape=[shape, shape, shape],
        compiler_params=_cparams("arbitrary", "arbitrary", "arbitrary"),
    )(q, k, v, tot, do)


def mlp_fwd(x, g, sh, sc, gate, w1, w2, tag):
    bsz, seq, d = x.shape
    t = bsz * seq
    h = act_call(_norm_mod, [x, g, sh, sc], [(d, BF16, "tile")], tag + "_norm")[0]
    pre, act = mm_nn_col(h.reshape(t, d), w1, tag + "_up", (F32, BF16),
                         lambda acc: (acc, jnp.square(jnp.maximum(acc, 0.0))))
    ff = mm_nn_row(act, w2, tag + "_down").reshape(bsz, seq, d)
    out = act_call(lambda x_, f_, g_: x_ + g_ * f_, [x, ff, gate], [(d, F32, "tile")], tag + "_res")[0]
    return out, (h, pre, act, ff)


def mlp_bwd(dout, x, g, sc, gate, w1, w2, saved, tag):
    bsz, seq, d = x.shape
    t = bsz * seq
    ns = w1.shape[0]
    h, pre, act, ff = saved
    dff, dgate = act_call(lambda do_, f_, g_: (g_ * do_, _rowsum(do_ * f_)), [dout, ff, gate],
                          [(d, BF16, "tile"), (d, F32, "seq")], tag + "_dres")
    dff = dff.reshape(t, d)
    dpre = mm_nt_row(dff, w2, tag + "_dact", (BF16,),
                     lambda acc, p: (acc * (2.0 * jnp.maximum(p, 0.0)),), (pre,))
    dw2 = mm_tn(act, dff, "a", ns, tag + "_dw2")
    dw1 = mm_tn(h.reshape(t, d), dpre, "c", ns, tag + "_dw1")
    dh = mm_nt_col(dpre, w1, tag + "_dh").reshape(bsz, seq, d)
    dx, dsh, dsc, dg = act_call(_norm_mod_bwd, [x, g, sc, dh, dout],
                                [(d, F32, "tile"), (d, F32, "seq"), (d, F32, "seq"), (d, F32, "all")],
                                tag + "_dnorm")
    return dx, dw1, dw2, (dsh, dsc, dgate, dg)


def _block_diag(m, rows_first):
    nb, k, r, c = m.shape
    eye = jnp.eye(k, dtype=m.dtype)
    return jnp.einsum("nkrc,kl->nkrlc", m, eye).reshape(nb, k * r, k * c)


def _block_diag_part(m, r, c):
    nb = m.shape[0]
    k = m.shape[1] // r
    return jnp.einsum("nkrlc,kl->nkrc", m.reshape(nb, k, r, k, c), jnp.eye(k, dtype=m.dtype))


def kernel(x, c, ada_w, ada_b, mix_norm_g, mlp_norm_g, mlp_w1, mlp_w2, s5_a_re, s5_a_im, s5_log_dt, s5_b_re, s5_b_im, s5_c_re, s5_c_im, s5_d, s5_w_glu, kv_ada_w, kv_ada_b, kv_norm_g, w_kv, k_norm_g, sb_w_q, q_norm_g, sb_w_o, loss_target, m_ada_w, m_ada_b, m_mix_norm_g, m_mlp_norm_g, m_mlp_w1, m_mlp_w2, m_s5_a_re, m_s5_a_im, m_s5_log_dt, m_s5_b_re, m_s5_b_im, m_s5_c_re, m_s5_c_im, m_s5_d, m_s5_w_glu, m_kv_ada_w, m_kv_ada_b, m_kv_norm_g, m_w_kv, m_k_norm_g, m_sb_w_q, m_q_norm_g, m_sb_w_o, v_ada_w, v_ada_b, v_mix_norm_g, v_mlp_norm_g, v_mlp_w1, v_mlp_w2, v_s5_a_re, v_s5_a_im, v_s5_log_dt, v_s5_b_re, v_s5_b_im, v_s5_c_re, v_s5_c_im, v_s5_d, v_s5_w_glu, v_kv_ada_w, v_kv_ada_b, v_kv_norm_g, v_w_kv, v_k_norm_g, v_sb_w_q, v_q_norm_g, v_sb_w_o):
    bsz, seq, d = x.shape
    t = bsz * seq
    n_groups = d // S5_GROUP
    nb = n_groups // S5_BLOCK_GROUPS
    gp = n_groups * S5_STATE
    dev = 4 * lax.axis_index("x") + 2 * lax.axis_index("y") + lax.axis_index("c")
    e_ada, e_kv = 6 * d, 2 * d
    n_ada, n_kv = e_ada // N_DEV, e_kv // N_DEV

    w1_all = all_gather(mlp_w1.astype(BF16), "gather_w1")
    w2_all = all_gather(mlp_w2.astype(BF16), "gather_w2")
    w1 = [w1_all[:, i] for i in range(2)]
    w2 = [w2_all[:, i] for i in range(2)]
    w_glu = all_gather(s5_w_glu[0].astype(BF16), "gather_glu")
    wkv = all_gather(w_kv.astype(BF16), "gather_wkv")
    wq = all_gather(sb_w_q[0].astype(BF16), "gather_wq")
    wo = all_gather(sb_w_o[0].astype(BF16), "gather_wo")
    d_skip = all_gather(s5_d, "gather_skip").reshape(1, 1, d)
    c_all = all_gather(c, "gather_c").reshape(N_DEV * bsz, d)

    w_cols = jnp.concatenate([ada_w[0], ada_w[1], kv_ada_w], axis=1)
    b_cols = jnp.concatenate([
        lax.dynamic_slice_in_dim(ada_b[0], dev * n_ada, n_ada),
        lax.dynamic_slice_in_dim(ada_b[1], dev * n_ada, n_ada),
        lax.dynamic_slice_in_dim(kv_ada_b, dev * n_kv, n_kv)])[None, :]
    mod_cols = ada_fwd(c_all, w_cols, b_cols)
    mod_all = all_gather(mod_cols, "gather_mod")
    mod_mine = lax.dynamic_slice_in_dim(mod_all, dev * bsz, bsz, axis=1)
    mod_mine = jnp.transpose(mod_mine, (1, 0, 2))
    mods = []
    for i in range(2):
        full = mod_mine[:, :, i * n_ada:(i + 1) * n_ada].reshape(bsz, e_ada)
        mods.append([full[:, None, j * d:(j + 1) * d] for j in range(6)])
    kv_full = mod_mine[:, :, 2 * n_ada:].reshape(bsz, e_kv)
    kv_sh, kv_sc = kv_full[:, None, :d], kv_full[:, None, d:]

    par = lambda p: p.reshape(1, 1, -1)

    sh_a, sc_a, g_a, sh_m, sc_m, g_m = mods[0]
    lam_re, lam_im = s5_a_re.reshape(gp, 1), s5_a_im.reshape(gp, 1)
    log_dt = jnp.broadcast_to(s5_log_dt.reshape(n_groups, 1), (n_groups, S5_STATE)).reshape(gp, 1)
    b_re, b_im = s5_b_re.reshape(gp, S5_GROUP), s5_b_im.reshape(gp, S5_GROUP)
    ab_re, ab_im, bb_re, bb_im = s5_prep(lam_re, lam_im, log_dt, b_re, b_im)
    to_bbd = lambda m: _block_diag(jnp.swapaxes(m.reshape(nb, S5_BLOCK_GROUPS, S5_STATE, S5_GROUP), 2, 3), True)
    to_cbd = lambda m: _block_diag(jnp.swapaxes(m.reshape(nb, S5_BLOCK_GROUPS, S5_GROUP, S5_STATE), 2, 3), True)
    bbd_re, bbd_im = to_bbd(bb_re).astype(BF16), to_bbd(bb_im).astype(BF16)
    cbd_re, cbd_im = to_cbd(s5_c_re[0]).astype(BF16), to_cbd(s5_c_im[0]).astype(BF16)
    abr, abi = ab_re.reshape(nb, 1, -1), ab_im.reshape(nb, 1, -1)

    h0 = act_call(_norm_mod, [x, par(mix_norm_g[0]), sh_a, sc_a], [(d, F32, "tile")], "mix0_norm")[0]
    y, st_re, st_im = s5_fwd(h0, bbd_re, bbd_im, cbd_re, cbd_im, abr, abi, d_skip)
    ge = act_call(lambda y_: jax.nn.gelu(y_), [y], [(d, BF16, "tile")], "gelu")[0]
    z = mm_nn_col(ge.reshape(t, d), w_glu, "glu_up").reshape(bsz, seq, 2 * d)
    x1 = act_call(lambda x_, z_, g_: x_ + g_ * (z_[:, :d] * jax.nn.sigmoid(z_[:, d:])), [x, z, g_a],
                  [(d, F32, "tile")], "glu_res")[0]
    x2, mlp0_saved = mlp_fwd(x1, par(mlp_norm_g[0]), sh_m, sc_m, g_m, w1[0], w2[0], "mlp0")

    sh_a1, sc_a1, g_a1, sh_m1, sc_m1, g_m1 = mods[1]
    kg = par(jnp.tile(k_norm_g, d // HEAD_DIM))
    qg = par(jnp.tile(q_norm_g[0], d // HEAD_DIM))
    hkv = act_call(_norm_mod, [x2, par(kv_norm_g), kv_sh, kv_sc], [(d, BF16, "tile")], "kv_norm")[0]
    kvf = mm_nn_col(hkv.reshape(t, d), wkv, "kv_proj").reshape(bsz, seq, 2 * d)
    k_h, v_h = act_call(lambda kv_, g_: (_head_norm(kv_[:, :d], g_), kv_[:, d:]), [kvf, kg],
                        [(d, BF16, "tile"), (d, BF16, "tile")], "k_norm")
    h1 = act_call(_norm_mod, [x2, par(mix_norm_g[1]), sh_a1, sc_a1], [(d, BF16, "tile")], "mix1_norm")[0]
    q_raw = mm_nn_row(h1.reshape(t, d), wq, "q_proj").reshape(bsz, seq, d)
    q_h = act_call(_head_norm, [q_raw, qg], [(d, BF16, "tile")], "q_norm")[0]
    o, att_tot = attention_fwd(q_h, k_h, v_h)
    mix1 = mm_nn_row(o.reshape(t, d), wo, "o_proj").reshape(bsz, seq, d)
    x3 = act_call(lambda x_, f_, g_: x_ + g_ * f_, [x2, mix1, g_a1], [(d, F32, "tile")], "att_res")[0]
    x4, mlp1_saved = mlp_fwd(x3, par(mlp_norm_g[1]), sh_m1, sc_m1, g_m1, w1[1], w2[1], "mlp1")

    def loss_fn(y_, t_):
        diff = y_ - t_
        part = jnp.sum(0.5 * jnp.mean(diff * diff, axis=-1, keepdims=True), axis=0, keepdims=True)
        return jnp.broadcast_to(part, (1, LANES)), diff * (1.0 / d)

    loss_part, dx4 = act_call(loss_fn, [x4, loss_target], [(LANES, F32, "all"), (d, F32, "tile")], "loss")
    loss = lax.psum(loss_part[0, 0, 0], ("x", "y", "c"))

    dx3, dw1_1, dw2_1, (dsh_m1, dsc_m1, dg_m1, dgn_mlp1) = mlp_bwd(
        dx4, x3, par(mlp_norm_g[1]), sc_m1, g_m1, w1[1], w2[1], mlp1_saved, "mlp1")
    dmix1, dg_a1 = act_call(lambda do_, f_, g_: (g_ * do_, _rowsum(do_ * f_)), [dx3, mix1, g_a1],
                            [(d, BF16, "tile"), (d, F32, "seq")], "att_dres")
    dmix1 = dmix1.reshape(t, d)
    do = mm_nt_row(dmix1, wo, "o_dproj").reshape(bsz, seq, d)
    dwo = mm_tn(o.reshape(t, d), dmix1, "a", N_DEV, "o_dw")
    dq, dk, dv = attention_bwd(q_h, k_h, v_h, att_tot, do)
    dq_raw, dqg = act_call(_head_norm_bwd, [q_raw, qg, dq], [(d, BF16, "tile"), (d, F32, "all")], "q_dnorm")
    dq_raw = dq_raw.reshape(t, d)
    dh1 = mm_nt_row(dq_raw, wq, "q_dproj").reshape(bsz, seq, d)
    dwq = mm_tn(h1.reshape(t, d), dq_raw, "a", N_DEV, "q_dw")
    dx2, dsh_a1, dsc_a1, dgn_mix1 = act_call(
        _norm_mod_bwd, [x2, par(mix_norm_g[1]), sc_a1, dh1, dx3],
        [(d, F32, "tile"), (d, F32, "seq"), (d, F32, "seq"), (d, F32, "all")], "mix1_dnorm")

    def kv_bwd_fn(kv_, g_, dk_, dv_):
        dk_raw, dg_ = _head_norm_bwd(kv_[:, :d], g_, dk_)
        return jnp.concatenate([dk_raw, dv_], axis=1), dg_

    dkvf, dkg = act_call(kv_bwd_fn, [kvf, kg, dk, dv], [(2 * d, BF16, "tile"), (d, F32, "all")], "k_dnorm")
    dkvf = dkvf.reshape(t, 2 * d)
    dhkv = mm_nt_col(dkvf, wkv, "kv_dproj").reshape(bsz, seq, d)
    dwkv = mm_tn(hkv.reshape(t, d), dkvf, "c", N_DEV, "kv_dw")
    dx2, dkv_sh, dkv_sc, dgn_kv = act_call(
        _norm_mod_bwd, [x2, par(kv_norm_g), kv_sc, dhkv, dx2],
        [(d, F32, "tile"), (d, F32, "seq"), (d, F32, "seq"), (d, F32, "all")], "kv_dnorm")

    dx1, dw1_0, dw2_0, (dsh_m0, dsc_m0, dg_m0, dgn_mlp0) = mlp_bwd(
        dx2, x1, par(mlp_norm_g[0]), sc_m, g_m, w1[0], w2[0], mlp0_saved, "mlp0")

    def glu_bwd_fn(do_, z_, g_):
        val, sig = z_[:, :d], jax.nn.sigmoid(z_[:, d:])
        dmix = g_ * do_
        dz = jnp.concatenate([dmix * sig, dmix * val * sig * (1.0 - sig)], axis=1)
        return dz, _rowsum(do_ * (val * sig))

    dz, dg_a0 = act_call(glu_bwd_fn, [dx1, z, g_a], [(2 * d, BF16, "tile"), (d, F32, "seq")], "glu_dres")
    dz = dz.reshape(t, 2 * d)
    dy = mm_nt_col(dz, w_glu, "glu_dup", (F32,), lambda acc, y_: (acc * _gelu_grad(y_),),
                   (y.reshape(t, d),)).reshape(bsz, seq, d)
    dwglu = mm_tn(ge.reshape(t, d), dz, "c", N_DEV, "glu_dw")
    du, dbbd_re, dbbd_im, dcbd_re, dcbd_im, dab_re, dab_im, dd_skip = s5_bwd(
        dy, h0, st_re, st_im, bbd_re, bbd_im, cbd_re, cbd_im, abr, abi, d_skip)
    dx0, dsh_a0, dsc_a0, dgn_mix0 = act_call(
        _norm_mod_bwd, [x, par(mix_norm_g[0]), sc_a, du, dx1],
        [(d, F32, "tile"), (d, F32, "seq"), (d, F32, "seq"), (d, F32, "all")], "mix0_dnorm")

    from_bbd = lambda m: jnp.swapaxes(_block_diag_part(m, S5_GROUP, S5_STATE), 2, 3).reshape(gp, S5_GROUP)
    d_c = lambda m: jnp.swapaxes(_block_diag_part(m, S5_STATE, S5_GROUP), 2, 3).reshape(
        1, n_groups, S5_GROUP, S5_STATE)
    d_lam_re, d_lam_im, d_log_dt, d_b_re, d_b_im = s5_prep_bwd(
        lam_re, lam_im, log_dt, b_re, b_im, dab_re.reshape(gp, 1), dab_im.reshape(gp, 1),
        from_bbd(dbbd_re), from_bbd(dbbd_im))

    small = all_reduce_small([
        jnp.stack([dgn_mix0.reshape(d), dgn_mix1.reshape(d)]),
        jnp.stack([dgn_mlp0.reshape(d), dgn_mlp1.reshape(d)]),
        d_lam_re.reshape(1, n_groups, S5_STATE), d_lam_im.reshape(1, n_groups, S5_STATE),
        d_log_dt.reshape(1, n_groups, S5_STATE).sum(axis=-1),
        d_b_re.reshape(s5_b_re.shape), d_b_im.reshape(s5_b_im.shape),
        d_c(dcbd_re), d_c(dcbd_im),
        dd_skip.reshape(1, d),
        dgn_kv.reshape(d),
        dkg.reshape(d // HEAD_DIM, HEAD_DIM).sum(axis=0),
        dqg.reshape(d // HEAD_DIM, HEAD_DIM).sum(axis=0)[None, :],
    ], "small_grads")
    (g_mix_norm, g_mlp_norm, g_a_re, g_a_im, g_log_dt, g_b_re, g_b_im, g_c_re, g_c_im,
     g_skip_full, g_kv_norm, g_k_norm, g_q_norm) = small
    g_s5_d = lax.dynamic_slice_in_dim(g_skip_full, dev * (d // N_DEV), d // N_DEV, axis=1)

    dm_mine = jnp.concatenate([
        dsh_a0, dsc_a0, dg_a0, dsh_m0, dsc_m0, dg_m0,
        dsh_a1, dsc_a1, dg_a1, dsh_m1, dsc_m1, dg_m1, dkv_sh, dkv_sc], axis=2).reshape(bsz, 2 * e_ada + e_kv)
    dm_all = all_gather(dm_mine, "gather_dmod").reshape(N_DEV * bsz, 2 * e_ada + e_kv)
    dm_cols = jnp.concatenate([
        lax.dynamic_slice_in_dim(dm_all, dev * n_ada, n_ada, axis=1),
        lax.dynamic_slice_in_dim(dm_all, e_ada + dev * n_ada, n_ada, axis=1),
        lax.dynamic_slice_in_dim(dm_all, 2 * e_ada + dev * n_kv, n_kv, axis=1)], axis=1)
    dw_cols, db_all = ada_bwd(c_all, dm_cols, dm_all)
    g_ada_w = jnp.stack([dw_cols[:, :n_ada], dw_cols[:, n_ada:2 * n_ada]])
    g_kv_ada_w = dw_cols[:, 2 * n_ada:]
    g_ada_b = db_all[0, :2 * e_ada].reshape(2, e_ada)
    g_kv_ada_b = db_all[0, 2 * e_ada:]

    g_w1 = reduce_scatter(jnp.stack([dw1_0, dw1_1], axis=1), "rs_w1")
    g_w2 = reduce_scatter(jnp.stack([dw2_0, dw2_1], axis=1), "rs_w2")
    g_glu = reduce_scatter(dwglu, "rs_glu")[None]
    g_wkv = reduce_scatter(dwkv, "rs_wkv")
    g_wq = reduce_scatter(dwq, "rs_wq")[None]
    g_wo = reduce_scatter(dwo, "rs_wo")[None]

    weights = [ada_w, ada_b, mix_norm_g, mlp_norm_g, mlp_w1, mlp_w2, s5_a_re, s5_a_im, s5_log_dt, s5_b_re,
               s5_b_im, s5_c_re, s5_c_im, s5_d, s5_w_glu, kv_ada_w, kv_ada_b, kv_norm_g, w_kv, k_norm_g,
               sb_w_q, q_norm_g, sb_w_o]
    grads = [g_ada_w, g_ada_b, g_mix_norm, g_mlp_norm, g_w1, g_w2, g_a_re, g_a_im, g_log_dt, g_b_re,
             g_b_im, g_c_re, g_c_im, g_s5_d, g_glu, g_kv_ada_w, g_kv_ada_b, g_kv_norm, g_wkv, g_k_norm,
             g_wq, g_q_norm, g_wo]
    ms = [m_ada_w, m_ada_b, m_mix_norm_g, m_mlp_norm_g, m_mlp_w1, m_mlp_w2, m_s5_a_re, m_s5_a_im, m_s5_log_dt,
          m_s5_b_re, m_s5_b_im, m_s5_c_re, m_s5_c_im, m_s5_d, m_s5_w_glu, m_kv_ada_w, m_kv_ada_b, m_kv_norm_g,
          m_w_kv, m_k_norm_g, m_sb_w_q, m_q_norm_g, m_sb_w_o]
    vs = [v_ada_w, v_ada_b, v_mix_norm_g, v_mlp_norm_g, v_mlp_w1, v_mlp_w2, v_s5_a_re, v_s5_a_im, v_s5_log_dt,
          v_s5_b_re, v_s5_b_im, v_s5_c_re, v_s5_c_im, v_s5_d, v_s5_w_glu, v_kv_ada_w, v_kv_ada_b, v_kv_norm_g,
          v_w_kv, v_k_norm_g, v_sb_w_q, v_q_norm_g, v_sb_w_o]
    grads = [g.reshape(w.shape) for g, w in zip(grads, weights)]
    deltas, new_ms, new_vs = [], [], []
    for i, (w, g, m, v) in enumerate(zip(weights, grads, ms, vs)):
        dl, nm, nv = adamw(w, g, m, v, f"adamw_{i}")
        deltas.append(dl)
        new_ms.append(nm)
        new_vs.append(nv)
    return (loss, dx0, *grads, *deltas, *new_ms, *new_vs)
```

```python
import functools
import math

import jax
import jax.numpy as jnp
from jax import lax
from jax.experimental import pallas as pl
from jax.experimental.pallas import tpu as pltpu

F32 = jnp.float32
BF16 = jnp.bfloat16

N_DEV = 8
N_CHIPS = 4
MESH = pl.DeviceIdType.MESH
ANY = pl.BlockSpec(memory_space=pl.ANY)

LANES = 128
VMEM_LIMIT_BYTES = 48 * 2 ** 20
TILE_BUDGET_BYTES = 4 * 2 ** 20

S5_GROUP = 16
S5_STATE = 64
S5_BLOCK_GROUPS = 16
HEAD_DIM = 64
ATT_BLOCK = 128
EPS = 1e-6

ADAM_LR = 0.001
ADAM_B1 = 0.9
ADAM_B2 = 0.999
ADAM_EPS = 1e-08
ADAM_WD = 0.01
ADAM_STEP = 10


def _cparams(*sem):
    return pltpu.CompilerParams(dimension_semantics=sem, vmem_limit_bytes=VMEM_LIMIT_BYTES)


def _divisor_tile(n, limit, mult):
    best = None
    for t in range(mult, min(n, limit) + 1, mult):
        if n % t == 0:
            best = t
    return best if best is not None else n


def _tile_m(m):
    return _divisor_tile(m, 512 if m >= 1024 else 256, 16)


def all_gather(x, name):
    def body(x_ref, out_ref, send_sems, recv_sems, local_sem):
        ax, ay, ac = lax.axis_index("x"), lax.axis_index("y"), lax.axis_index("c")
        me, sibling = (ax, ay, ac), (ax, ay, 1 - ac)
        chips = [(1 - ax, ay), (ax, 1 - ay), (1 - ax, 1 - ay)]

        def slot(px, py, pc):
            return out_ref.at[4 * px + 2 * py + pc]

        def copy(k, block, to, src=None):
            return pltpu.make_async_remote_copy(
                src_ref=slot(*block) if src is None else src, dst_ref=slot(*block),
                send_sem=send_sems.at[k], recv_sem=recv_sems.at[k], device_id=to, device_id_type=MESH)

        mine = pltpu.make_async_copy(x_ref, slot(*me), local_sem)
        mine.start()
        first = [copy(0, me, sibling, src=x_ref)]
        first += [copy(1 + j, me, (*chip, ac), src=x_ref) for j, chip in enumerate(chips)]
        for cp in first:
            cp.start()
        passed = [copy(4 + j, (*chip, ac), sibling) for j, chip in enumerate(chips)]
        for j, chip in enumerate(chips):
            copy(1 + j, (*chip, ac), me).wait_recv()
            passed[j].start()
        copy(0, sibling, me).wait_recv()
        for j, chip in enumerate(chips):
            copy(4 + j, (*chip, 1 - ac), me).wait_recv()
        for cp in first + passed:
            cp.wait_send()
        mine.wait()

    return pl.pallas_call(
        body, name=name,
        out_shape=jax.ShapeDtypeStruct((N_DEV,) + x.shape, x.dtype),
        in_specs=[ANY], out_specs=ANY,
        scratch_shapes=[pltpu.SemaphoreType.DMA((7,)), pltpu.SemaphoreType.DMA((7,)), pltpu.SemaphoreType.DMA],
    )(x)


def pair_exchange(part, name):
    def body(p_ref, out_ref, send_sems, recv_sems):
        ax, ay, ac = lax.axis_index("x"), lax.axis_index("y"), lax.axis_index("c")
        copies = [pltpu.make_async_remote_copy(
            src_ref=p_ref.at[2 * q + (1 - ac)], dst_ref=out_ref.at[q],
            send_sem=send_sems.at[q], recv_sem=recv_sems.at[q],
            device_id=(ax, ay, 1 - ac), device_id_type=MESH) for q in range(N_CHIPS)]
        for cp in copies:
            cp.start()
        for cp in copies:
            cp.wait_recv()
        for cp in copies:
            cp.wait_send()

    return pl.pallas_call(
        body, name=name,
        out_shape=jax.ShapeDtypeStruct((N_CHIPS,) + part.shape[1:], part.dtype),
        in_specs=[ANY], out_specs=ANY,
        scratch_shapes=[pltpu.SemaphoreType.DMA((N_CHIPS,)), pltpu.SemaphoreType.DMA((N_CHIPS,))],
    )(part)


def chip_exchange(p, name):
    def body(p_ref, out_ref, send_sems, recv_sems, local_sem):
        ax, ay, ac = lax.axis_index("x"), lax.axis_index("y"), lax.axis_index("c")
        my_chip = 2 * ax + ay
        chips = [(1 - ax, ay), (ax, 1 - ay), (1 - ax, 1 - ay)]
        mine = pltpu.make_async_copy(p_ref.at[my_chip], out_ref.at[my_chip], local_sem)
        mine.start()
        sends = [pltpu.make_async_remote_copy(
            src_ref=p_ref.at[2 * px + py], dst_ref=out_ref.at[my_chip],
            send_sem=send_sems.at[k], recv_sem=recv_sems.at[k],
            device_id=(px, py, ac), device_id_type=MESH) for k, (px, py) in enumerate(chips)]
        for cp in sends:
            cp.start()
        for k, (px, py) in enumerate(chips):
            pltpu.make_async_remote_copy(
                src_ref=p_ref.at[my_chip], dst_ref=out_ref.at[2 * px + py],
                send_sem=send_sems.at[k], recv_sem=recv_sems.at[k],
                device_id=(px, py, ac), device_id_type=MESH).wait_recv()
        for cp in sends:
            cp.wait_send()
        mine.wait()

    return pl.pallas_call(
        body, name=name,
        out_shape=jax.ShapeDtypeStruct(p.shape, p.dtype),
        in_specs=[ANY], out_specs=ANY,
        scratch_shapes=[pltpu.SemaphoreType.DMA((3,)), pltpu.SemaphoreType.DMA((3,)), pltpu.SemaphoreType.DMA],
    )(p)


def rows_call(fn, ins, outs, name):
    rows = ins[0].shape[1]
    per_row = sum(a.shape[0] * a.shape[2] * a.dtype.itemsize for a in ins)
    per_row += sum(l * c * jnp.dtype(dt).itemsize for l, c, dt in outs)
    tr = _divisor_tile(rows, max(16, TILE_BUDGET_BYTES // per_row), 16)
    n_in = len(ins)

    def body(*refs):
        vals = fn(*[r[...] for r in refs[:n_in]])
        if not isinstance(vals, (tuple, list)):
            vals = (vals,)
        for r, v in zip(refs[n_in:], vals):
            r[...] = v.astype(r.dtype)

    def spec(l, c):
        return pl.BlockSpec((l, tr, c), lambda i: (0, i, 0))

    res = pl.pallas_call(
        body, name=name, grid=(rows // tr,),
        in_specs=[spec(a.shape[0], a.shape[2]) for a in ins],
        out_specs=[spec(l, c) for l, c, _ in outs],
        out_shape=[jax.ShapeDtypeStruct((l, rows, c), dt) for l, c, dt in outs],
        compiler_params=_cparams("arbitrary"),
    )(*ins)
    return res


def _as_rows(a, lead=0):
    shape = a.shape
    l = int(math.prod(shape[:lead])) if lead else 1
    rest = shape[lead:]
    c = rest[-1] if rest else 1
    r = int(math.prod(rest[:-1])) if len(rest) > 1 else 1
    return a.reshape(l, r, c)


def act_call(fn, ins, outs, name):
    bsz, seq = ins[0].shape[0], ins[0].shape[1]
    per_row = sum(a.shape[2] * a.dtype.itemsize for a in ins if a.shape[1] == seq)
    per_row += sum(c * jnp.dtype(dt).itemsize for c, dt, kind in outs if kind == "tile")
    ts = _divisor_tile(seq, max(16, TILE_BUDGET_BYTES // per_row), 16)
    n_in = len(ins)

    def in_spec(a):
        c = a.shape[2]
        if a.shape[1] == seq:
            return pl.BlockSpec((None, ts, c), lambda b, s: (b, s, 0))
        if a.shape[0] == bsz:
            return pl.BlockSpec((None, 1, c), lambda b, s: (b, 0, 0))
        return pl.BlockSpec((None, 1, c), lambda b, s: (0, 0, 0))

    def out_spec(c, kind):
        if kind == "tile":
            return pl.BlockSpec((None, ts, c), lambda b, s: (b, s, 0))
        if kind == "seq":
            return pl.BlockSpec((None, 1, c), lambda b, s: (b, 0, 0))
        return pl.BlockSpec((None, 1, c), lambda b, s: (0, 0, 0))

    def out_shape(c, dt, kind):
        if kind == "tile":
            return jax.ShapeDtypeStruct((bsz, seq, c), dt)
        return jax.ShapeDtypeStruct((bsz if kind == "seq" else 1, 1, c), dt)

    def accumulate(ref, v, first):
        @pl.when(first)
        def _():
            ref[...] = jnp.zeros_like(ref)

        ref[...] += v.astype(ref.dtype)

    def body(*refs):
        b, s = pl.program_id(0), pl.program_id(1)
        vals = fn(*[r[...] for r in refs[:n_in]])
        if not isinstance(vals, (tuple, list)):
            vals = (vals,)
        for ref, v, (_, _, kind) in zip(refs[n_in:], vals, outs):
            if kind == "tile":
                ref[...] = v.astype(ref.dtype)
            elif kind == "seq":
                accumulate(ref, v, s == 0)
            else:
                accumulate(ref, v, jnp.logical_and(b == 0, s == 0))

    return pl.pallas_call(
        body, name=name, grid=(bsz, seq // ts),
        in_specs=[in_spec(a) for a in ins],
        out_specs=[out_spec(c, kind) for c, _, kind in outs],
        out_shape=[out_shape(*o) for o in outs],
        compiler_params=_cparams("arbitrary", "arbitrary"),
    )(*ins)


def _mm(name, grid, a, a_spec, b, b_spec, dims, out_shape, out_spec, out_dtypes, acc_steps,
        epi=None, extras=(), extra_spec=None):
    n_ex, n_out = len(extras), len(out_dtypes)
    tile = tuple(d for d in out_spec.block_shape if d is not None)

    def body(*refs):
        a_ref, b_ref = refs[0], refs[1]
        ex_refs = refs[2:2 + n_ex]
        o_refs = refs[2 + n_ex:2 + n_ex + n_out]
        p = lax.dot_general(a_ref[...].astype(BF16), b_ref[...].astype(BF16), (dims, ((), ())),
                            preferred_element_type=F32)

        def finish(acc):
            vals = epi(acc, *[r[...] for r in ex_refs]) if epi is not None else (acc,) * n_out
            for r, v in zip(o_refs, vals):
                r[...] = v.astype(r.dtype)

        if not acc_steps:
            finish(p)
        else:
            acc_ref = refs[-1]
            s = pl.program_id(1)

            @pl.when(s == 0)
            def _():
                acc_ref[...] = p

            @pl.when(s > 0)
            def _():
                acc_ref[...] += p

            @pl.when(s == acc_steps - 1)
            def _():
                finish(acc_ref[...])

    res = pl.pallas_call(
        body, name=name, grid=grid,
        in_specs=[a_spec, b_spec] + [extra_spec] * n_ex,
        out_specs=[out_spec] * n_out,
        out_shape=[jax.ShapeDtypeStruct(out_shape, dt) for dt in out_dtypes],
        scratch_shapes=[pltpu.VMEM(tile, F32)] if acc_steps else [],
        compiler_params=_cparams("arbitrary", "arbitrary"),
    )(a, b, *extras)
    return res if n_out > 1 else res[0]


def mm_nn_col(a, w, name, out_dtypes=(F32,), epi=None):
    m, k = a.shape
    ns, _, nb = w.shape
    tm = _tile_m(m)
    return _mm(name, (m // tm, ns), a, pl.BlockSpec((tm, k), lambda i, j: (i, 0)),
               w, pl.BlockSpec((None, k, nb), lambda i, j: (j, 0, 0)), ((1,), (0,)),
               (m, ns * nb), pl.BlockSpec((tm, nb), lambda i, j: (i, j)), out_dtypes, 0, epi)


def mm_nn_row(a, w, name, out_dtypes=(F32,)):
    m = a.shape[0]
    ns, kb, n = w.shape
    tm = _tile_m(m)
    return _mm(name, (m // tm, ns), a, pl.BlockSpec((tm, kb), lambda i, s: (i, s)),
               w, pl.BlockSpec((None, kb, n), lambda i, s: (s, 0, 0)), ((1,), (0,)),
               (m, n), pl.BlockSpec((tm, n), lambda i, s: (i, 0)), out_dtypes, ns)


def mm_nt_col(dc, w, name, out_dtypes=(F32,), epi=None, extras=()):
    m = dc.shape[0]
    ns, k, nb = w.shape
    tm = _tile_m(m)
    spec = pl.BlockSpec((tm, k), lambda i, s: (i, 0))
    return _mm(name, (m // tm, ns), dc, pl.BlockSpec((tm, nb), lambda i, s: (i, s)),
               w, pl.BlockSpec((None, k, nb), lambda i, s: (s, 0, 0)), ((1,), (1,)),
               (m, k), spec, out_dtypes, ns, epi, extras, spec)


def mm_nt_row(dc, w, name, out_dtypes=(F32,), epi=None, extras=()):
    m, n = dc.shape
    ns, kb, _ = w.shape
    tm = _tile_m(m)
    spec = pl.BlockSpec((tm, kb), lambda i, s: (i, s))
    return _mm(name, (m // tm, ns), dc, pl.BlockSpec((tm, n), lambda i, s: (i, 0)),
               w, pl.BlockSpec((None, kb, n), lambda i, s: (s, 0, 0)), ((1,), (1,)),
               (m, ns * kb), spec, out_dtypes, 0, epi, extras, spec)


def mm_tn(a, c, slab, ns, name, out_dtype=BF16):
    m, ka_all = a.shape
    nc_all = c.shape[1]
    ka = ka_all // ns if slab == "a" else ka_all
    nc = nc_all // ns if slab == "c" else nc_all
    tt = _tile_m(m)
    a_spec = pl.BlockSpec((tt, ka), (lambda s, t: (t, s)) if slab == "a" else (lambda s, t: (t, 0)))
    c_spec = pl.BlockSpec((tt, nc), (lambda s, t: (t, s)) if slab == "c" else (lambda s, t: (t, 0)))
    return _mm(name, (ns, m // tt), a, a_spec, c, c_spec, ((0,), (0,)),
               (ns, ka, nc), pl.BlockSpec((None, ka, nc), lambda s, t: (s, 0, 0)), (out_dtype,), m // tt)


def reduce_scatter(part, name):
    shape = part.shape[1:]
    ac = lax.axis_index("c")
    from_sibling = pair_exchange(part, name + "_pair")
    own = lax.dynamic_index_in_dim(part.reshape((N_CHIPS, 2) + shape), ac, axis=1, keepdims=False)
    pair = rows_call(lambda a, b: a.astype(F32) + b.astype(F32),
                     [_as_rows(own, 1), _as_rows(from_sibling, 1)],
                     [(N_CHIPS, shape[-1], BF16)], name + "_pairsum")[0]
    got = chip_exchange(pair.reshape((N_CHIPS,) + shape), name + "_chips")
    total = rows_call(lambda g: jnp.sum(g.astype(F32), axis=0, keepdims=True),
                      [_as_rows(got, 1)], [(1, shape[-1], F32)], name + "_sum")[0]
    return total.reshape(shape)


def all_reduce_small(leaves, name):
    sizes = [int(a.size) for a in leaves]
    flat = jnp.concatenate([a.reshape(-1) for a in leaves])
    total = int(flat.size)
    padded = -(-total // (16 * LANES)) * (16 * LANES)
    flat = jnp.pad(flat, (0, padded - total)).reshape(padded // LANES, LANES)
    gathered = all_gather(flat, name + "_gather")
    summed = rows_call(lambda g: jnp.sum(g, axis=0, keepdims=True), [gathered],
                       [(1, LANES, F32)], name + "_sum")[0].reshape(-1)
    out, at = [], 0
    for a, n in zip(leaves, sizes):
        out.append(summed[at:at + n].reshape(a.shape))
        at += n
    return out


def adamw(w, g, m, v, name):
    c = w.shape[-1] if w.ndim else 1

    def fn(w_, g_, m_, v_):
        nm = ADAM_B1 * m_ + (1.0 - ADAM_B1) * g_
        nv = ADAM_B2 * v_ + (1.0 - ADAM_B2) * (g_ * g_)
        m_hat = nm / (1.0 - ADAM_B1 ** ADAM_STEP)
        v_hat = nv / (1.0 - ADAM_B2 ** ADAM_STEP)
        delta = -ADAM_LR * (m_hat / (jnp.sqrt(v_hat) + ADAM_EPS) + ADAM_WD * w_)
        return delta, nm, nv

    res = rows_call(fn, [_as_rows(t) for t in (w, g.astype(F32), m, v)], [(1, c, F32)] * 3, name)
    return tuple(r.reshape(w.shape) for r in res)


def _rowsum(v):
    return jnp.sum(v, axis=0, keepdims=True)


def _norm_mod(x, g, sh, sc):
    n = x * lax.rsqrt(jnp.mean(x * x, axis=-1, keepdims=True) + EPS)
    return (n * g) * (1.0 + sc) + sh


def _norm_mod_bwd(x, g, sc, dh, dres):
    r = lax.rsqrt(jnp.mean(x * x, axis=-1, keepdims=True) + EPS)
    n = x * r
    dy = dh * (1.0 + sc)
    dn = dy * g
    dx = r * (dn - n * jnp.mean(dn * n, axis=-1, keepdims=True))
    return dres + dx, _rowsum(dh), _rowsum(dh * (n * g)), _rowsum(dy * n)


def _head_mean(v):
    low = lax.broadcasted_iota(jnp.int32, (1, LANES), 1) < HEAD_DIM
    parts = []
    for p in range(v.shape[1] // LANES):
        blk = v[:, p * LANES:(p + 1) * LANES]
        s0 = jnp.sum(jnp.where(low, blk, 0.0), axis=-1, keepdims=True)
        s1 = jnp.sum(jnp.where(low, 0.0, blk), axis=-1, keepdims=True)
        parts.append(jnp.where(low, s0, s1))
    return jnp.concatenate(parts, axis=1) * (1.0 / HEAD_DIM)


def _head_norm(x, g):
    return x * lax.rsqrt(_head_mean(x * x) + EPS) * g


def _head_norm_bwd(x, g, dy):
    r = lax.rsqrt(_head_mean(x * x) + EPS)
    n = x * r
    dn = dy * g
    return r * (dn - n * _head_mean(dn * n)), _rowsum(dy * n)


GELU_C = math.sqrt(2.0 / math.pi)
GELU_A = 0.044715


def _gelu_grad(y):
    t = jnp.tanh(GELU_C * (y + GELU_A * y * y * y))
    return 0.5 * (1.0 + t) + 0.5 * y * (1.0 - t * t) * GELU_C * (1.0 + 3.0 * GELU_A * y * y)


def ada_fwd(c_all, w_cols, b_cols):
    def body(c_ref, w_ref, b_ref, o_ref):
        c = c_ref[...]
        s = (c * jax.nn.sigmoid(c)).astype(BF16)
        o_ref[...] = jnp.dot(s, w_ref[...].astype(BF16), preferred_element_type=F32) + b_ref[...]

    return pl.pallas_call(
        body, name="ada_fwd", out_shape=jax.ShapeDtypeStruct((c_all.shape[0], w_cols.shape[1]), F32),
        compiler_params=pltpu.CompilerParams(vmem_limit_bytes=VMEM_LIMIT_BYTES),
    )(c_all, w_cols, b_cols)


def ada_bwd(c_all, dm_cols, dm_all):
    def body(c_ref, d_ref, all_ref, dw_ref, db_ref):
        c = c_ref[...]
        s = (c * jax.nn.sigmoid(c)).astype(BF16)
        dw_ref[...] = lax.dot_general(s, d_ref[...].astype(BF16), (((0,), (0,)), ((), ())),
                                      preferred_element_type=F32)
        db_ref[...] = jnp.sum(all_ref[...], axis=0, keepdims=True)

    return pl.pallas_call(
        body, name="ada_bwd",
        out_shape=[jax.ShapeDtypeStruct((c_all.shape[1], dm_cols.shape[1]), F32),
                   jax.ShapeDtypeStruct((1, dm_all.shape[1]), F32)],
        compiler_params=pltpu.CompilerParams(vmem_limit_bytes=VMEM_LIMIT_BYTES),
    )(c_all, dm_cols, dm_all)


def _s5_discretise(lam_re, lam_im, log_dt, b_re, b_im):
    dt = jnp.exp(log_dt)
    mag = jnp.exp(lam_re * dt)
    ab_re = mag * jnp.cos(lam_im * dt)
    ab_im = mag * jnp.sin(lam_im * dt)
    den = lam_re * lam_re + lam_im * lam_im
    nr = ab_re - 1.0
    ni = ab_im
    f_re = (nr * lam_re + ni * lam_im) / den
    f_im = (ni * lam_re - nr * lam_im) / den
    bb_re = f_re * b_re - f_im * b_im
    bb_im = f_re * b_im + f_im * b_re
    return ab_re, ab_im, bb_re, bb_im


def s5_prep(lam_re, lam_im, log_dt, b_re, b_im):
    gp, h = b_re.shape

    def body(lr, li, ld, br, bi, o_ar, o_ai, o_br, o_bi):
        res = _s5_discretise(lr[...], li[...], ld[...], br[...], bi[...])
        for r, v in zip((o_ar, o_ai, o_br, o_bi), res):
            r[...] = v

    col, mat = jax.ShapeDtypeStruct((gp, 1), F32), jax.ShapeDtypeStruct((gp, h), F32)
    return pl.pallas_call(body, name="s5_prep", out_shape=[col, col, mat, mat])(lam_re, lam_im, log_dt, b_re, b_im)


def s5_prep_bwd(lam_re, lam_im, log_dt, b_re, b_im, d_ab_re, d_ab_im, d_bb_re, d_bb_im):
    gp, h = b_re.shape

    def body(lr, li, ld, br, bi, g_ar, g_ai, g_br, g_bi, o_lr, o_li, o_ld, o_br, o_bi):
        _, vjp = jax.vjp(_s5_discretise, lr[...], li[...], ld[...], br[...], bi[...])
        res = vjp((g_ar[...], g_ai[...], g_br[...], g_bi[...]))
        for r, v in zip((o_lr, o_li, o_ld, o_br, o_bi), res):
            r[...] = v

    col, mat = jax.ShapeDtypeStruct((gp, 1), F32), jax.ShapeDtypeStruct((gp, h), F32)
    return pl.pallas_call(body, name="s5_prep_bwd", out_shape=[col, col, col, mat, mat])(
        lam_re, lam_im, log_dt, b_re, b_im, d_ab_re, d_ab_im, d_bb_re, d_bb_im)


def _s5_chunk(seq):
    return _divisor_tile(seq, 256, 16)


def s5_fwd(u, bbd_re, bbd_im, cbd_re, cbd_im, ab_re, ab_im, dskip):
    bsz, seq, d = u.shape
    nb, cb, ns = bbd_re.shape
    lc = _s5_chunk(seq)

    def body(u_ref, bre_ref, bim_ref, cre_ref, cim_ref, ar_ref, ai_ref, d_ref, y_ref, sre_ref, sim_ref,
             carry_re, carry_im):
        t = pl.program_id(2)

        @pl.when(t == 0)
        def _():
            carry_re[...] = jnp.zeros_like(carry_re)
            carry_im[...] = jnp.zeros_like(carry_im)

        uf = u_ref[...]
        ub = uf.astype(BF16)
        sre_ref[...] = jnp.dot(ub, bre_ref[...], preferred_element_type=F32)
        sim_ref[...] = jnp.dot(ub, bim_ref[...], preferred_element_type=F32)
        ar, ai = ar_ref[...], ai_ref[...]

        def step(i, c):
            cr, ci = c
            row = pl.ds(i, 1)
            nr = ar * cr - ai * ci + sre_ref[row, :]
            ni = ar * ci + ai * cr + sim_ref[row, :]
            sre_ref[row, :] = nr
            sim_ref[row, :] = ni
            return nr, ni

        cr, ci = lax.fori_loop(0, lc, step, (carry_re[...], carry_im[...]), unroll=8)
        carry_re[...] = cr
        carry_im[...] = ci
        y = jnp.dot(sre_ref[...].astype(BF16), cre_ref[...], preferred_element_type=F32)
        y -= jnp.dot(sim_ref[...].astype(BF16), cim_ref[...], preferred_element_type=F32)
        y_ref[...] = y + d_ref[...] * uf

    chan = pl.BlockSpec((None, lc, cb), lambda n, b, t: (b, t, n))
    state = pl.BlockSpec((None, lc, ns), lambda n, b, t: (b, t, n))
    par = lambda r, c: pl.BlockSpec((None, r, c), lambda n, b, t: (n, 0, 0))
    return pl.pallas_call(
        body, name="s5_fwd", grid=(nb, bsz, seq // lc),
        in_specs=[chan, par(cb, ns), par(cb, ns), par(ns, cb), par(ns, cb), par(1, ns), par(1, ns),
                  pl.BlockSpec((None, 1, cb), lambda n, b, t: (0, 0, n))],
        out_specs=[chan, state, state],
        out_shape=[jax.ShapeDtypeStruct((bsz, seq, d), F32),
                   jax.ShapeDtypeStruct((bsz, seq, nb * ns), F32),
                   jax.ShapeDtypeStruct((bsz, seq, nb * ns), F32)],
        scratch_shapes=[pltpu.VMEM((1, ns), F32), pltpu.VMEM((1, ns), F32)],
        compiler_params=_cparams("arbitrary", "arbitrary", "arbitrary"),
    )(u, bbd_re, bbd_im, cbd_re, cbd_im, ab_re, ab_im, dskip)


def s5_bwd(dy, u, st_re, st_im, bbd_re, bbd_im, cbd_re, cbd_im, ab_re, ab_im, dskip):
    bsz, seq, d = u.shape
    nb, cb, ns = bbd_re.shape
    lc = _s5_chunk(seq)
    nc = seq // lc

    def body(dy_ref, u_ref, sre_ref, sim_ref, bre_ref, bim_ref, cre_ref, cim_ref, ar_ref, ai_ref, d_ref,
             du_ref, dbre_ref, dbim_ref, dcre_ref, dcim_ref, dar_ref, dai_ref, dd_ref,
             g_re, g_im, gs_re, gs_im, carry_re, carry_im):
        b, t = pl.program_id(1), pl.program_id(2)

        @pl.when(jnp.logical_and(b == 0, t == 0))
        def _():
            for r in (dbre_ref, dbim_ref, dcre_ref, dcim_ref, dar_ref, dai_ref, dd_ref):
                r[...] = jnp.zeros_like(r)

        @pl.when(t == 0)
        def _():
            carry_re[...] = jnp.zeros_like(carry_re)
            carry_im[...] = jnp.zeros_like(carry_im)

        dyf, uf = dy_ref[...], u_ref[...]
        dyb, ub = dyf.astype(BF16), uf.astype(BF16)
        nt = (((1,), (1,)), ((), ()))
        tn = (((0,), (0,)), ((), ()))
        g_re[...] = lax.dot_general(dyb, cre_ref[...], nt, preferred_element_type=F32)
        g_im[...] = -lax.dot_general(dyb, cim_ref[...], nt, preferred_element_type=F32)
        ar, ai = ar_ref[...], ai_ref[...]

        def step(k, c):
            cr, ci = c
            row = pl.ds(lc - 1 - k, 1)
            gs_re[row, :] = cr
            gs_im[row, :] = ci
            nr = ar * cr + ai * ci + g_re[row, :]
            ni = ar * ci - ai * cr + g_im[row, :]
            g_re[row, :] = nr
            g_im[row, :] = ni
            return nr, ni

        cr, ci = lax.fori_loop(0, lc, step, (carry_re[...], carry_im[...]), unroll=8)
        carry_re[...] = cr
        carry_im[...] = ci

        sr, si = sre_ref[...], sim_ref[...]
        hr, hi = gs_re[...], gs_im[...]
        dar_ref[...] += _rowsum(hr * sr + hi * si)
        dai_ref[...] += _rowsum(hi * sr - hr * si)
        gr, gi = g_re[...].astype(BF16), g_im[...].astype(BF16)
        du = lax.dot_general(gr, bre_ref[...], nt, preferred_element_type=F32)
        du += lax.dot_general(gi, bim_ref[...], nt, preferred_element_type=F32)
        du_ref[...] = du + d_ref[...] * dyf
        dbre_ref[...] += lax.dot_general(ub, gr, tn, preferred_element_type=F32)
        dbim_ref[...] += lax.dot_general(ub, gi, tn, preferred_element_type=F32)
        dcre_ref[...] += lax.dot_general(sr.astype(BF16), dyb, tn, preferred_element_type=F32)
        dcim_ref[...] -= lax.dot_general(si.astype(BF16), dyb, tn, preferred_element_type=F32)
        dd_ref[...] += _rowsum(dyf * uf)

    chan = pl.BlockSpec((None, lc, cb), lambda n, b, t: (b, nc - 1 - t, n))
    state = pl.BlockSpec((None, lc, ns), lambda n, b, t: (b, nc - 1 - t, n))
    par = lambda r, c: pl.BlockSpec((None, r, c), lambda n, b, t: (n, 0, 0))
    return pl.pallas_call(
        body, name="s5_bwd", grid=(nb, bsz, nc),
        in_specs=[chan, chan, state, state, par(cb, ns), par(cb, ns), par(ns, cb), par(ns, cb),
                  par(1, ns), par(1, ns), pl.BlockSpec((None, 1, cb), lambda n, b, t: (0, 0, n))],
        out_specs=[chan, par(cb, ns), par(cb, ns), par(ns, cb), par(ns, cb), par(1, ns), par(1, ns), par(1, cb)],
        out_shape=[jax.ShapeDtypeStruct((bsz, seq, d), F32),
                   jax.ShapeDtypeStruct((nb, cb, ns), F32), jax.ShapeDtypeStruct((nb, cb, ns), F32),
                   jax.ShapeDtypeStruct((nb, ns, cb), F32), jax.ShapeDtypeStruct((nb, ns, cb), F32),
                   jax.ShapeDtypeStruct((nb, 1, ns), F32), jax.ShapeDtypeStruct((nb, 1, ns), F32),
                   jax.ShapeDtypeStruct((nb, 1, cb), F32)],
        scratch_shapes=[pltpu.VMEM((lc, ns), F32)] * 4 + [pltpu.VMEM((1, ns), F32)] * 2,
        compiler_params=_cparams("arbitrary", "arbitrary", "arbitrary"),
    )(dy, u, st_re, st_im, bbd_re, bbd_im, cbd_re, cbd_im, ab_re, ab_im, dskip)


ATT_HEADS = 4
ATT_LANES = ATT_HEADS * HEAD_DIM
ATT_KEYS = 2 * ATT_BLOCK
ATT_SCALE = 1.0 / math.sqrt(HEAD_DIM)
_NT = (((1,), (1,)), ((), ()))
_TN = (((0,), (0,)), ((), ()))
_HEADS = [slice(h * HEAD_DIM, (h + 1) * HEAD_DIM) for h in range(ATT_HEADS)]
_HALF = [slice(0, ATT_BLOCK), slice(ATT_BLOCK, ATT_KEYS)]


def _log_sigmoids(z):
    sp = jnp.log(1.0 + jnp.exp(-jnp.abs(z)))
    ls = jnp.minimum(z, 0.0) - sp
    return ls, ls - z


def _sum_matrix(after, inclusive):
    j = lax.broadcasted_iota(jnp.int32, (ATT_KEYS, ATT_KEYS), 0) % ATT_BLOCK
    s = lax.broadcasted_iota(jnp.int32, (ATT_KEYS, ATT_KEYS), 1)
    if after:
        hit = (j >= s) if inclusive else (j > s)
    else:
        hit = (j <= s) if inclusive else (j < s)
    return jnp.where(jnp.logical_or(hit, s >= ATT_BLOCK), 1.0, 0.0).astype(BF16)


def _hi_lo(v):
    hi = v.astype(BF16)
    lo = (v - hi.astype(F32)).astype(BF16)
    return jnp.concatenate([hi, lo], axis=1)


def _strict_mask(i, j):
    t = i * ATT_BLOCK + lax.broadcasted_iota(jnp.int32, (ATT_BLOCK, ATT_KEYS), 0)
    s = j * ATT_KEYS + lax.broadcasted_iota(jnp.int32, (ATT_BLOCK, ATT_KEYS), 1)
    return s < t


def attention_fwd(q, k, v):
    bsz, seq, d = q.shape

    def body(q_ref, k_ref, v_ref, o_ref, tot_ref, z_buf, ls_buf, cs_buf, acc_buf, run_buf):
        i = pl.program_id(2)
        jd = i // 2
        sums = _sum_matrix(True, False)
        acc_buf[...] = jnp.zeros_like(acc_buf)
        run_buf[...] = jnp.zeros_like(run_buf)

        def block(j, masked):
            rows = pl.ds(pl.multiple_of(j * ATT_KEYS, ATT_KEYS), ATT_KEYS)
            strict = _strict_mask(i, j) if masked else None
            for h, ln in enumerate(_HEADS):
                z_buf[h] = lax.dot_general(q_ref[:, ln], k_ref[rows, ln], _NT, preferred_element_type=F32)
            for h in range(ATT_HEADS):
                for half, cols in enumerate(_HALF):
                    ls, lf = _log_sigmoids(z_buf[h, :, cols])
                    if masked:
                        lf = jnp.where(strict[:, cols], lf, 0.0)
                    ls_buf[h, :, cols] = ls
                    cs_buf[h, half] = jnp.dot(_hi_lo(lf), sums, preferred_element_type=F32)
            for h, ln in enumerate(_HEADS):
                run = run_buf[h]
                late, early = cs_buf[h, 1], cs_buf[h, 0]
                a1 = run + late[:, _HALF[0]]
                run = run + late[:, _HALF[1]]
                a0 = run + early[:, _HALF[0]]
                run_buf[h] = run + early[:, _HALF[1]]
                w = jnp.exp(ls_buf[h] + jnp.concatenate([a0, a1], axis=1))
                if masked:
                    w = jnp.where(strict, w, 0.0)
                acc_buf[h] += jnp.dot(w.astype(BF16), v_ref[rows, ln], preferred_element_type=F32)

        block(jd, True)

        def step(it, carry):
            block(jd - 1 - it, False)
            return carry

        lax.fori_loop(0, jd, step, 0)
        o_ref[...] = jnp.concatenate([acc_buf[h] for h in range(ATT_HEADS)], axis=1)
        tot_ref[...] = jnp.concatenate([run_buf[h, :, :HEAD_DIM] for h in range(ATT_HEADS)], axis=1)

    blk = pl.BlockSpec((None, ATT_BLOCK, ATT_LANES), lambda b, p, i: (b, i, p))
    full = pl.BlockSpec((None, seq, ATT_LANES), lambda b, p, i: (b, 0, p))
    shape = jax.ShapeDtypeStruct((bsz, seq, d), F32)
    tile = (ATT_HEADS, ATT_BLOCK, ATT_KEYS)
    return pl.pallas_call(
        body, name="attention_fwd", grid=(bsz, d // ATT_LANES, seq // ATT_BLOCK),
        in_specs=[blk, full, full], out_specs=[blk, blk], out_shape=[shape, shape],
        scratch_shapes=[pltpu.VMEM(tile, F32), pltpu.VMEM(tile, F32),
                        pltpu.VMEM((ATT_HEADS, 2, ATT_BLOCK, ATT_KEYS), F32),
                        pltpu.VMEM((ATT_HEADS, ATT_BLOCK, HEAD_DIM), F32),
                        pltpu.VMEM((ATT_HEADS, ATT_BLOCK, ATT_BLOCK), F32)],
        compiler_params=_cparams("arbitrary", "arbitrary", "arbitrary"),
    )(q, k, v)


def attention_bwd(q, k, v, tot, do):
    bsz, seq, d = q.shape

    def body(q_ref, k_ref, v_ref, tot_ref, do_ref, dq_ref, dk_ref, dv_ref,
             z_buf, dw_buf, ls_buf, e_buf, up_buf, bf_buf, w_buf, do_buf, dq_buf, tot_buf, run_buf, erun_buf):
        i = pl.program_id(2)
        jd = i // 2

        @pl.when(i == 0)
        def _():
            dk_ref[...] = jnp.zeros_like(dk_ref)
            dv_ref[...] = jnp.zeros_like(dv_ref)

        upto_incl, upto_excl = _sum_matrix(False, True), _sum_matrix(False, False)
        do_buf[...] = do_ref[...].astype(BF16)
        for h, ln in enumerate(_HEADS):
            tot_buf[h] = jnp.concatenate([tot_ref[:, ln], tot_ref[:, ln]], axis=1)
        dq_buf[...] = jnp.zeros_like(dq_buf)
        run_buf[...] = jnp.zeros_like(run_buf)
        erun_buf[...] = jnp.zeros_like(erun_buf)

        def block(j, masked):
            rows = pl.ds(pl.multiple_of(j * ATT_KEYS, ATT_KEYS), ATT_KEYS)
            strict = _strict_mask(i, j) if masked else None
            for h, ln in enumerate(_HEADS):
                z_buf[h] = lax.dot_general(q_ref[:, ln], k_ref[rows, ln], _NT, preferred_element_type=F32)
                dw_buf[h] = lax.dot_general(do_buf[:, ln], v_ref[rows, ln], _NT, preferred_element_type=F32)
            for h in range(ATT_HEADS):
                for half, cols in enumerate(_HALF):
                    ls, lf = _log_sigmoids(z_buf[h, :, cols])
                    if masked:
                        lf = jnp.where(strict[:, cols], lf, 0.0)
                    ls_buf[h, :, cols] = ls
                    up_buf[h, half] = jnp.dot(_hi_lo(lf), upto_incl, preferred_element_type=F32)
            for h in range(ATT_HEADS):
                run = run_buf[h]
                early, late = up_buf[h, 0], up_buf[h, 1]
                u0 = run + early[:, _HALF[0]]
                run = run + early[:, _HALF[1]]
                u1 = run + late[:, _HALF[0]]
                run_buf[h] = run + late[:, _HALF[1]]
                tot_h = tot_buf[h]
                after = jnp.concatenate([tot_h - u0, tot_h - u1], axis=1)
                w = jnp.exp(ls_buf[h] + after)
                if masked:
                    w = jnp.where(strict, w, 0.0)
                w_buf[h] = w.astype(BF16)
                e = dw_buf[h] * w
                e_buf[h] = e
                for half, cols in enumerate(_HALF):
                    bf_buf[h, half] = jnp.dot(_hi_lo(e[:, cols]), upto_excl, preferred_element_type=F32)
            dks, dvs = [], []
            for h, ln in enumerate(_HEADS):
                erun = erun_buf[h]
                early, late = bf_buf[h, 0], bf_buf[h, 1]
                b0 = erun + early[:, _HALF[0]]
                erun = erun + early[:, _HALF[1]]
                b1 = erun + late[:, _HALF[0]]
                erun_buf[h] = erun + late[:, _HALF[1]]
                e = e_buf[h]
                dz = e - jnp.exp(ls_buf[h]) * (e + jnp.concatenate([b0, b1], axis=1))
                if masked:
                    dz = jnp.where(strict, dz, 0.0)
                dz = dz.astype(BF16)
                dq_buf[h] += jnp.dot(dz, k_ref[rows, ln], preferred_element_type=F32)
                dks.append(lax.dot_general(dz, q_ref[:, ln], _TN, preferred_element_type=F32))
                dvs.append(lax.dot_general(w_buf[h], do_buf[:, ln], _TN, preferred_element_type=F32))
            dk_ref[rows, :] += jnp.concatenate(dks, axis=1)
            dv_ref[rows, :] += jnp.concatenate(dvs, axis=1)

        def step(j, carry):
            block(j, False)
            return carry

        lax.fori_loop(0, jd, step, 0)
        block(jd, True)
        dq_ref[...] = jnp.concatenate([dq_buf[h] for h in range(ATT_HEADS)], axis=1) * ATT_SCALE

    blk = pl.BlockSpec((None, ATT_BLOCK, ATT_LANES), lambda b, p, i: (b, i, p))
    full = pl.BlockSpec((None, seq, ATT_LANES), lambda b, p, i: (b, 0, p))
    shape = jax.ShapeDtypeStruct((bsz, seq, d), F32)
    tile = (ATT_HEADS, ATT_BLOCK, ATT_KEYS)
    pair = (ATT_HEADS, 2, ATT_BLOCK, ATT_KEYS)
    square = (ATT_HEADS, ATT_BLOCK, ATT_BLOCK)
    return pl.pallas_call(
        body, name="attention_bwd", grid=(bsz, d // ATT_LANES, seq // ATT_BLOCK),
        in_specs=[blk, full, full, blk, blk], out_specs=[blk, full, full], out_shape=[shape, shape, shape],
        scratch_shapes=[pltpu.VMEM(tile, F32), pltpu.VMEM(tile, F32), pltpu.VMEM(tile, F32), pltpu.VMEM(tile, F32),
                        pltpu.VMEM(pair, F32), pltpu.VMEM(pair, F32), pltpu.VMEM(tile, BF16),
                        pltpu.VMEM((ATT_BLOCK, ATT_LANES), BF16), pltpu.VMEM((ATT_HEADS, ATT_BLOCK, HEAD_DIM), F32),
                        pltpu.VMEM(square, F32), pltpu.VMEM(square, F32), pltpu.VMEM(square, F32)],
        compiler_params=_cparams("arbitrary", "arbitrary", "arbitrary"),
    )(q, k, v, tot, do)


def mlp_fwd(x, g, sh, sc, gate, w1, w2, tag):
    bsz, seq, d = x.shape
    t = bsz * seq
    h = act_call(_norm_mod, [x, g, sh, sc], [(d, BF16, "tile")], tag + "_norm")[0]
    pre, act = mm_nn_col(h.reshape(t, d), w1, tag + "_up", (F32, BF16),
                         lambda acc: (acc, jnp.square(jnp.maximum(acc, 0.0))))
    ff = mm_nn_row(act, w2, tag + "_down").reshape(bsz, seq, d)
    out = act_call(lambda x_, f_, g_: x_ + g_ * f_, [x, ff, gate], [(d, F32, "tile")], tag + "_res")[0]
    return out, (h, pre, act, ff)


def mlp_bwd(dout, x, g, sc, gate, w1, w2, saved, tag):
    bsz, seq, d = x.shape
    t = bsz * seq
    ns = w1.shape[0]
    h, pre, act, ff = saved
    dff, dgate = act_call(lambda do_, f_, g_: (g_ * do_, _rowsum(do_ * f_)), [dout, ff, gate],
                          [(d, BF16, "tile"), (d, F32, "seq")], tag + "_dres")
    dff = dff.reshape(t, d)
    dpre = mm_nt_row(dff, w2, tag + "_dact", (BF16,),
                     lambda acc, p: (acc * (2.0 * jnp.maximum(p, 0.0)),), (pre,))
    dw2 = mm_tn(act, dff, "a", ns, tag + "_dw2")
    dw1 = mm_tn(h.reshape(t, d), dpre, "c", ns, tag + "_dw1")
    dh = mm_nt_col(dpre, w1, tag + "_dh").reshape(bsz, seq, d)
    dx, dsh, dsc, dg = act_call(_norm_mod_bwd, [x, g, sc, dh, dout],
                                [(d, F32, "tile"), (d, F32, "seq"), (d, F32, "seq"), (d, F32, "all")],
                                tag + "_dnorm")
    return dx, dw1, dw2, (dsh, dsc, dgate, dg)


def _block_diag(m, rows_first):
    nb, k, r, c = m.shape
    eye = jnp.eye(k, dtype=m.dtype)
    return jnp.einsum("nkrc,kl->nkrlc", m, eye).reshape(nb, k * r, k * c)


def _block_diag_part(m, r, c):
    nb = m.shape[0]
    k = m.shape[1] // r
    return jnp.einsum("nkrlc,kl->nkrc", m.reshape(nb, k, r, k, c), jnp.eye(k, dtype=m.dtype))


def kernel(x, c, ada_w, ada_b, mix_norm_g, mlp_norm_g, mlp_w1, mlp_w2, s5_a_re, s5_a_im, s5_log_dt, s5_b_re, s5_b_im, s5_c_re, s5_c_im, s5_d, s5_w_glu, kv_ada_w, kv_ada_b, kv_norm_g, w_kv, k_norm_g, sb_w_q, q_norm_g, sb_w_o, loss_target, m_ada_w, m_ada_b, m_mix_norm_g, m_mlp_norm_g, m_mlp_w1, m_mlp_w2, m_s5_a_re, m_s5_a_im, m_s5_log_dt, m_s5_b_re, m_s5_b_im, m_s5_c_re, m_s5_c_im, m_s5_d, m_s5_w_glu, m_kv_ada_w, m_kv_ada_b, m_kv_norm_g, m_w_kv, m_k_norm_g, m_sb_w_q, m_q_norm_g, m_sb_w_o, v_ada_w, v_ada_b, v_mix_norm_g, v_mlp_norm_g, v_mlp_w1, v_mlp_w2, v_s5_a_re, v_s5_a_im, v_s5_log_dt, v_s5_b_re, v_s5_b_im, v_s5_c_re, v_s5_c_im, v_s5_d, v_s5_w_glu, v_kv_ada_w, v_kv_ada_b, v_kv_norm_g, v_w_kv, v_k_norm_g, v_sb_w_q, v_q_norm_g, v_sb_w_o):
    bsz, seq, d = x.shape
    t = bsz * seq
    n_groups = d // S5_GROUP
    nb = n_groups // S5_BLOCK_GROUPS
    gp = n_groups * S5_STATE
    dev = 4 * lax.axis_index("x") + 2 * lax.axis_index("y") + lax.axis_index("c")
    e_ada, e_kv = 6 * d, 2 * d
    n_ada, n_kv = e_ada // N_DEV, e_kv // N_DEV

    w1_all = all_gather(mlp_w1.astype(BF16), "gather_w1")
    w2_all = all_gather(mlp_w2.astype(BF16), "gather_w2")
    w1 = [w1_all[:, i] for i in range(2)]
    w2 = [w2_all[:, i] for i in range(2)]
    w_glu = all_gather(s5_w_glu[0].astype(BF16), "gather_glu")
    wkv = all_gather(w_kv.astype(BF16), "gather_wkv")
    wq = all_gather(sb_w_q[0].astype(BF16), "gather_wq")
    wo = all_gather(sb_w_o[0].astype(BF16), "gather_wo")
    d_skip = all_gather(s5_d, "gather_skip").reshape(1, 1, d)
    c_all = all_gather(c, "gather_c").reshape(N_DEV * bsz, d)

    w_cols = jnp.concatenate([ada_w[0], ada_w[1], kv_ada_w], axis=1)
    b_cols = jnp.concatenate([
        lax.dynamic_slice_in_dim(ada_b[0], dev * n_ada, n_ada),
        lax.dynamic_slice_in_dim(ada_b[1], dev * n_ada, n_ada),
        lax.dynamic_slice_in_dim(kv_ada_b, dev * n_kv, n_kv)])[None, :]
    mod_cols = ada_fwd(c_all, w_cols, b_cols)
    mod_all = all_gather(mod_cols, "gather_mod")
    mod_mine = lax.dynamic_slice_in_dim(mod_all, dev * bsz, bsz, axis=1)
    mod_mine = jnp.transpose(mod_mine, (1, 0, 2))
    mods = []
    for i in range(2):
        full = mod_mine[:, :, i * n_ada:(i + 1) * n_ada].reshape(bsz, e_ada)
        mods.append([full[:, None, j * d:(j + 1) * d] for j in range(6)])
    kv_full = mod_mine[:, :, 2 * n_ada:].reshape(bsz, e_kv)
    kv_sh, kv_sc = kv_full[:, None, :d], kv_full[:, None, d:]

    par = lambda p: p.reshape(1, 1, -1)

    sh_a, sc_a, g_a, sh_m, sc_m, g_m = mods[0]
    lam_re, lam_im = s5_a_re.reshape(gp, 1), s5_a_im.reshape(gp, 1)
    log_dt = jnp.broadcast_to(s5_log_dt.reshape(n_groups, 1), (n_groups, S5_STATE)).reshape(gp, 1)
    b_re, b_im = s5_b_re.reshape(gp, S5_GROUP), s5_b_im.reshape(gp, S5_GROUP)
    ab_re, ab_im, bb_re, bb_im = s5_prep(lam_re, lam_im, log_dt, b_re, b_im)
    to_bbd = lambda m: _block_diag(jnp.swapaxes(m.reshape(nb, S5_BLOCK_GROUPS, S5_STATE, S5_GROUP), 2, 3), True)
    to_cbd = lambda m: _block_diag(jnp.swapaxes(m.reshape(nb, S5_BLOCK_GROUPS, S5_GROUP, S5_STATE), 2, 3), True)
    bbd_re, bbd_im = to_bbd(bb_re).astype(BF16), to_bbd(bb_im).astype(BF16)
    cbd_re, cbd_im = to_cbd(s5_c_re[0]).astype(BF16), to_cbd(s5_c_im[0]).astype(BF16)
    abr, abi = ab_re.reshape(nb, 1, -1), ab_im.reshape(nb, 1, -1)

    h0 = act_call(_norm_mod, [x, par(mix_norm_g[0]), sh_a, sc_a], [(d, F32, "tile")], "mix0_norm")[0]
    y, st_re, st_im = s5_fwd(h0, bbd_re, bbd_im, cbd_re, cbd_im, abr, abi, d_skip)
    ge = act_call(lambda y_: jax.nn.gelu(y_), [y], [(d, BF16, "tile")], "gelu")[0]
    z = mm_nn_col(ge.reshape(t, d), w_glu, "glu_up").reshape(bsz, seq, 2 * d)
    x1 = act_call(lambda x_, z_, g_: x_ + g_ * (z_[:, :d] * jax.nn.sigmoid(z_[:, d:])), [x, z, g_a],
                  [(d, F32, "tile")], "glu_res")[0]
    x2, mlp0_saved = mlp_fwd(x1, par(mlp_norm_g[0]), sh_m, sc_m, g_m, w1[0], w2[0], "mlp0")

    sh_a1, sc_a1, g_a1, sh_m1, sc_m1, g_m1 = mods[1]
    kg = par(jnp.tile(k_norm_g, d // HEAD_DIM))
    qg = par(jnp.tile(q_norm_g[0], d // HEAD_DIM))
    hkv = act_call(_norm_mod, [x2, par(kv_norm_g), kv_sh, kv_sc], [(d, BF16, "tile")], "kv_norm")[0]
    kvf = mm_nn_col(hkv.reshape(t, d), wkv, "kv_proj").reshape(bsz, seq, 2 * d)
    k_h, v_h = act_call(lambda kv_, g_: (_head_norm(kv_[:, :d], g_), kv_[:, d:]), [kvf, kg],
                        [(d, BF16, "tile"), (d, BF16, "tile")], "k_norm")
    h1 = act_call(_norm_mod, [x2, par(mix_norm_g[1]), sh_a1, sc_a1], [(d, BF16, "tile")], "mix1_norm")[0]
    q_raw = mm_nn_row(h1.reshape(t, d), wq, "q_proj").reshape(bsz, seq, d)
    q_h = act_call(lambda x_, g_: _head_norm(x_, g_) * ATT_SCALE, [q_raw, qg], [(d, BF16, "tile")], "q_norm")[0]
    o, att_tot = attention_fwd(q_h, k_h, v_h)
    mix1 = mm_nn_row(o.reshape(t, d), wo, "o_proj").reshape(bsz, seq, d)
    x3 = act_call(lambda x_, f_, g_: x_ + g_ * f_, [x2, mix1, g_a1], [(d, F32, "tile")], "att_res")[0]
    x4, mlp1_saved = mlp_fwd(x3, par(mlp_norm_g[1]), sh_m1, sc_m1, g_m1, w1[1], w2[1], "mlp1")

    def loss_fn(y_, t_):
        diff = y_ - t_
        part = jnp.sum(0.5 * jnp.mean(diff * diff, axis=-1, keepdims=True), axis=0, keepdims=True)
        return jnp.broadcast_to(part, (1, LANES)), diff * (1.0 / d)

    loss_part, dx4 = act_call(loss_fn, [x4, loss_target], [(LANES, F32, "all"), (d, F32, "tile")], "loss")
    loss = lax.psum(loss_part[0, 0, 0], ("x", "y", "c"))

    dx3, dw1_1, dw2_1, (dsh_m1, dsc_m1, dg_m1, dgn_mlp1) = mlp_bwd(
        dx4, x3, par(mlp_norm_g[1]), sc_m1, g_m1, w1[1], w2[1], mlp1_saved, "mlp1")
    dmix1, dg_a1 = act_call(lambda do_, f_, g_: (g_ * do_, _rowsum(do_ * f_)), [dx3, mix1, g_a1],
                            [(d, BF16, "tile"), (d, F32, "seq")], "att_dres")
    dmix1 = dmix1.reshape(t, d)
    do = mm_nt_row(dmix1, wo, "o_dproj").reshape(bsz, seq, d)
    dwo = mm_tn(o.reshape(t, d), dmix1, "a", N_DEV, "o_dw")
    dq, dk, dv = attention_bwd(q_h, k_h, v_h, att_tot, do)
    dq_raw, dqg = act_call(_head_norm_bwd, [q_raw, qg, dq], [(d, BF16, "tile"), (d, F32, "all")], "q_dnorm")
    dq_raw = dq_raw.reshape(t, d)
    dh1 = mm_nt_row(dq_raw, wq, "q_dproj").reshape(bsz, seq, d)
    dwq = mm_tn(h1.reshape(t, d), dq_raw, "a", N_DEV, "q_dw")
    dx2, dsh_a1, dsc_a1, dgn_mix1 = act_call(
        _norm_mod_bwd, [x2, par(mix_norm_g[1]), sc_a1, dh1, dx3],
        [(d, F32, "tile"), (d, F32, "seq"), (d, F32, "seq"), (d, F32, "all")], "mix1_dnorm")

    def kv_bwd_fn(kv_, g_, dk_, dv_):
        dk_raw, dg_ = _head_norm_bwd(kv_[:, :d], g_, dk_)
        return jnp.concatenate([dk_raw, dv_], axis=1), dg_

    dkvf, dkg = act_call(kv_bwd_fn, [kvf, kg, dk, dv], [(2 * d, BF16, "tile"), (d, F32, "all")], "k_dnorm")
    dkvf = dkvf.reshape(t, 2 * d)
    dhkv = mm_nt_col(dkvf, wkv, "kv_dproj").reshape(bsz, seq, d)
    dwkv = mm_tn(hkv.reshape(t, d), dkvf, "c", N_DEV, "kv_dw")
    dx2, dkv_sh, dkv_sc, dgn_kv = act_call(
        _norm_mod_bwd, [x2, par(kv_norm_g), kv_sc, dhkv, dx2],
        [(d, F32, "tile"), (d, F32, "seq"), (d, F32, "seq"), (d, F32, "all")], "kv_dnorm")

    dx1, dw1_0, dw2_0, (dsh_m0, dsc_m0, dg_m0, dgn_mlp0) = mlp_bwd(
        dx2, x1, par(mlp_norm_g[0]), sc_m, g_m, w1[0], w2[0], mlp0_saved, "mlp0")

    def glu_bwd_fn(do_, z_, g_):
        val, sig = z_[:, :d], jax.nn.sigmoid(z_[:, d:])
        dmix = g_ * do_
        dz = jnp.concatenate([dmix * sig, dmix * val * sig * (1.0 - sig)], axis=1)
        return dz, _rowsum(do_ * (val * sig))

    dz, dg_a0 = act_call(glu_bwd_fn, [dx1, z, g_a], [(2 * d, BF16, "tile"), (d, F32, "seq")], "glu_dres")
    dz = dz.reshape(t, 2 * d)
    dy = mm_nt_col(dz, w_glu, "glu_dup", (F32,), lambda acc, y_: (acc * _gelu_grad(y_),),
                   (y.reshape(t, d),)).reshape(bsz, seq, d)
    dwglu = mm_tn(ge.reshape(t, d), dz, "c", N_DEV, "glu_dw")
    du, dbbd_re, dbbd_im, dcbd_re, dcbd_im, dab_re, dab_im, dd_skip = s5_bwd(
        dy, h0, st_re, st_im, bbd_re, bbd_im, cbd_re, cbd_im, abr, abi, d_skip)
    dx0, dsh_a0, dsc_a0, dgn_mix0 = act_call(
        _norm_mod_bwd, [x, par(mix_norm_g[0]), sc_a, du, dx1],
        [(d, F32, "tile"), (d, F32, "seq"), (d, F32, "seq"), (d, F32, "all")], "mix0_dnorm")

    from_bbd = lambda m: jnp.swapaxes(_block_diag_part(m, S5_GROUP, S5_STATE), 2, 3).reshape(gp, S5_GROUP)
    d_c = lambda m: jnp.swapaxes(_block_diag_part(m, S5_STATE, S5_GROUP), 2, 3).reshape(
        1, n_groups, S5_GROUP, S5_STATE)
    d_lam_re, d_lam_im, d_log_dt, d_b_re, d_b_im = s5_prep_bwd(
        lam_re, lam_im, log_dt, b_re, b_im, dab_re.reshape(gp, 1), dab_im.reshape(gp, 1),
        from_bbd(dbbd_re), from_bbd(dbbd_im))

    small = all_reduce_small([
        jnp.stack([dgn_mix0.reshape(d), dgn_mix1.reshape(d)]),
        jnp.stack([dgn_mlp0.reshape(d), dgn_mlp1.reshape(d)]),
        d_lam_re.reshape(1, n_groups, S5_STATE), d_lam_im.reshape(1, n_groups, S5_STATE),
        d_log_dt.reshape(1, n_groups, S5_STATE).sum(axis=-1),
        d_b_re.reshape(s5_b_re.shape), d_b_im.reshape(s5_b_im.shape),
        d_c(dcbd_re), d_c(dcbd_im),
        dd_skip.reshape(1, d),
        dgn_kv.reshape(d),
        dkg.reshape(d // HEAD_DIM, HEAD_DIM).sum(axis=0),
        dqg.reshape(d // HEAD_DIM, HEAD_DIM).sum(axis=0)[None, :],
    ], "small_grads")
    (g_mix_norm, g_mlp_norm, g_a_re, g_a_im, g_log_dt, g_b_re, g_b_im, g_c_re, g_c_im,
     g_skip_full, g_kv_norm, g_k_norm, g_q_norm) = small
    g_s5_d = lax.dynamic_slice_in_dim(g_skip_full, dev * (d // N_DEV), d // N_DEV, axis=1)

    dm_mine = jnp.concatenate([
        dsh_a0, dsc_a0, dg_a0, dsh_m0, dsc_m0, dg_m0,
        dsh_a1, dsc_a1, dg_a1, dsh_m1, dsc_m1, dg_m1, dkv_sh, dkv_sc], axis=2).reshape(bsz, 2 * e_ada + e_kv)
    dm_all = all_gather(dm_mine, "gather_dmod").reshape(N_DEV * bsz, 2 * e_ada + e_kv)
    dm_cols = jnp.concatenate([
        lax.dynamic_slice_in_dim(dm_all, dev * n_ada, n_ada, axis=1),
        lax.dynamic_slice_in_dim(dm_all, e_ada + dev * n_ada, n_ada, axis=1),
        lax.dynamic_slice_in_dim(dm_all, 2 * e_ada + dev * n_kv, n_kv, axis=1)], axis=1)
    dw_cols, db_all = ada_bwd(c_all, dm_cols, dm_all)
    g_ada_w = jnp.stack([dw_cols[:, :n_ada], dw_cols[:, n_ada:2 * n_ada]])
    g_kv_ada_w = dw_cols[:, 2 * n_ada:]
    g_ada_b = db_all[0, :2 * e_ada].reshape(2, e_ada)
    g_kv_ada_b = db_all[0, 2 * e_ada:]

    g_w1 = reduce_scatter(jnp.stack([dw1_0, dw1_1], axis=1), "rs_w1")
    g_w2 = reduce_scatter(jnp.stack([dw2_0, dw2_1], axis=1), "rs_w2")
    g_glu = reduce_scatter(dwglu, "rs_glu")[None]
    g_wkv = reduce_scatter(dwkv, "rs_wkv")
    g_wq = reduce_scatter(dwq, "rs_wq")[None]
    g_wo = reduce_scatter(dwo, "rs_wo")[None]

    weights = [ada_w, ada_b, mix_norm_g, mlp_norm_g, mlp_w1, mlp_w2, s5_a_re, s5_a_im, s5_log_dt, s5_b_re,
               s5_b_im, s5_c_re, s5_c_im, s5_d, s5_w_glu, kv_ada_w, kv_ada_b, kv_norm_g, w_kv, k_norm_g,
               sb_w_q, q_norm_g, sb_w_o]
    grads = [g_ada_w, g_ada_b, g_mix_norm, g_mlp_norm, g_w1, g_w2, g_a_re, g_a_im, g_log_dt, g_b_re,
             g_b_im, g_c_re, g_c_im, g_s5_d, g_glu, g_kv_ada_w, g_kv_ada_b, g_kv_norm, g_wkv, g_k_norm,
             g_wq, g_q_norm, g_wo]
    ms = [m_ada_w, m_ada_b, m_mix_norm_g, m_mlp_norm_g, m_mlp_w1, m_mlp_w2, m_s5_a_re, m_s5_a_im, m_s5_log_dt,
          m_s5_b_re, m_s5_b_im, m_s5_c_re, m_s5_c_im, m_s5_d, m_s5_w_glu, m_kv_ada_w, m_kv_ada_b, m_kv_norm_g,
          m_w_kv, m_k_norm_g, m_sb_w_q, m_q_norm_g, m_sb_w_o]
    vs = [v_ada_w, v_ada_b, v_mix_norm_g, v_mlp_norm_g, v_mlp_w1, v_mlp_w2, v_s5_a_re, v_s5_a_im, v_s5_log_dt,
          v_s5_b_re, v_s5_b_im, v_s5_c_re, v_s5_c_im, v_s5_d, v_s5_w_glu, v_kv_ada_w, v_kv_ada_b, v_kv_norm_g,
          v_w_kv, v_k_norm_g, v_sb_w_q, v_q_norm_g, v_sb_w_o]
    grads = [g.reshape(w.shape) for g, w in zip(grads, weights)]
    deltas, new_ms, new_vs = [], [], []
    for i, (w, g, m, v) in enumerate(zip(weights, grads, ms, vs)):
        dl, nm, nv = adamw(w, g, m, v, f"adamw_{i}")
        deltas.append(dl)
        new_ms.append(nm)
        new_vs.append(nv)
    return (loss, dx0, *grads, *deltas, *new_ms, *new_vs)
```

```python
import functools
import math

import jax
import jax.numpy as jnp
from jax import lax
from jax.experimental import pallas as pl
from jax.experimental.pallas import tpu as pltpu

F32 = jnp.float32
BF16 = jnp.bfloat16

N_DEV = 8
N_CHIPS = 4
MESH = pl.DeviceIdType.MESH
ANY = pl.BlockSpec(memory_space=pl.ANY)

LANES = 128
VMEM_LIMIT_BYTES = 48 * 2 ** 20
TILE_BUDGET_BYTES = 4 * 2 ** 20

S5_GROUP = 16
S5_STATE = 64
S5_BLOCK_GROUPS = 16
HEAD_DIM = 64
ATT_BLOCK = 128
EPS = 1e-6

ADAM_LR = 0.001
ADAM_B1 = 0.9
ADAM_B2 = 0.999
ADAM_EPS = 1e-08
ADAM_WD = 0.01
ADAM_STEP = 10


def _cparams(*sem):
    return pltpu.CompilerParams(dimension_semantics=sem, vmem_limit_bytes=VMEM_LIMIT_BYTES)


def _divisor_tile(n, limit, mult):
    best = None
    for t in range(mult, min(n, limit) + 1, mult):
        if n % t == 0:
            best = t
    return best if best is not None else n


def _tile_m(m):
    return _divisor_tile(m, 1024 if m >= 2048 else 256, 16)


def all_gather(x, name):
    def body(x_ref, out_ref, send_sems, recv_sems, local_sem):
        ax, ay, ac = lax.axis_index("x"), lax.axis_index("y"), lax.axis_index("c")
        me, sibling = (ax, ay, ac), (ax, ay, 1 - ac)
        chips = [(1 - ax, ay), (ax, 1 - ay), (1 - ax, 1 - ay)]

        def slot(px, py, pc):
            return out_ref.at[4 * px + 2 * py + pc]

        def copy(k, block, to, src=None):
            return pltpu.make_async_remote_copy(
                src_ref=slot(*block) if src is None else src, dst_ref=slot(*block),
                send_sem=send_sems.at[k], recv_sem=recv_sems.at[k], device_id=to, device_id_type=MESH)

        mine = pltpu.make_async_copy(x_ref, slot(*me), local_sem)
        mine.start()
        first = [copy(0, me, sibling, src=x_ref)]
        first += [copy(1 + j, me, (*chip, ac), src=x_ref) for j, chip in enumerate(chips)]
        for cp in first:
            cp.start()
        passed = [copy(4 + j, (*chip, ac), sibling) for j, chip in enumerate(chips)]
        for j, chip in enumerate(chips):
            copy(1 + j, (*chip, ac), me).wait_recv()
            passed[j].start()
        copy(0, sibling, me).wait_recv()
        for j, chip in enumerate(chips):
            copy(4 + j, (*chip, 1 - ac), me).wait_recv()
        for cp in first + passed:
            cp.wait_send()
        mine.wait()

    return pl.pallas_call(
        body, name=name,
        out_shape=jax.ShapeDtypeStruct((N_DEV,) + x.shape, x.dtype),
        in_specs=[ANY], out_specs=ANY,
        scratch_shapes=[pltpu.SemaphoreType.DMA((7,)), pltpu.SemaphoreType.DMA((7,)), pltpu.SemaphoreType.DMA],
    )(x)


def pair_exchange(part, name):
    def body(p_ref, out_ref, send_sems, recv_sems):
        ax, ay, ac = lax.axis_index("x"), lax.axis_index("y"), lax.axis_index("c")
        copies = [pltpu.make_async_remote_copy(
            src_ref=p_ref.at[2 * q + (1 - ac)], dst_ref=out_ref.at[q],
            send_sem=send_sems.at[q], recv_sem=recv_sems.at[q],
            device_id=(ax, ay, 1 - ac), device_id_type=MESH) for q in range(N_CHIPS)]
        for cp in copies:
            cp.start()
        for cp in copies:
            cp.wait_recv()
        for cp in copies:
            cp.wait_send()

    return pl.pallas_call(
        body, name=name,
        out_shape=jax.ShapeDtypeStruct((N_CHIPS,) + part.shape[1:], part.dtype),
        in_specs=[ANY], out_specs=ANY,
        scratch_shapes=[pltpu.SemaphoreType.DMA((N_CHIPS,)), pltpu.SemaphoreType.DMA((N_CHIPS,))],
    )(part)


def chip_exchange(p, name):
    def body(p_ref, out_ref, send_sems, recv_sems, local_sem):
        ax, ay, ac = lax.axis_index("x"), lax.axis_index("y"), lax.axis_index("c")
        my_chip = 2 * ax + ay
        chips = [(1 - ax, ay), (ax, 1 - ay), (1 - ax, 1 - ay)]
        mine = pltpu.make_async_copy(p_ref.at[my_chip], out_ref.at[my_chip], local_sem)
        mine.start()
        sends = [pltpu.make_async_remote_copy(
            src_ref=p_ref.at[2 * px + py], dst_ref=out_ref.at[my_chip],
            send_sem=send_sems.at[k], recv_sem=recv_sems.at[k],
            device_id=(px, py, ac), device_id_type=MESH) for k, (px, py) in enumerate(chips)]
        for cp in sends:
            cp.start()
        for k, (px, py) in enumerate(chips):
            pltpu.make_async_remote_copy(
                src_ref=p_ref.at[my_chip], dst_ref=out_ref.at[2 * px + py],
                send_sem=send_sems.at[k], recv_sem=recv_sems.at[k],
                device_id=(px, py, ac), device_id_type=MESH).wait_recv()
        for cp in sends:
            cp.wait_send()
        mine.wait()

    return pl.pallas_call(
        body, name=name,
        out_shape=jax.ShapeDtypeStruct(p.shape, p.dtype),
        in_specs=[ANY], out_specs=ANY,
        scratch_shapes=[pltpu.SemaphoreType.DMA((3,)), pltpu.SemaphoreType.DMA((3,)), pltpu.SemaphoreType.DMA],
    )(p)


def rows_call(fn, ins, outs, name):
    rows = ins[0].shape[1]
    per_row = sum(a.shape[0] * a.shape[2] * a.dtype.itemsize for a in ins)
    per_row += sum(l * c * jnp.dtype(dt).itemsize for l, c, dt in outs)
    tr = _divisor_tile(rows, max(16, TILE_BUDGET_BYTES // per_row), 16)
    n_in = len(ins)

    def body(*refs):
        vals = fn(*[r[...] for r in refs[:n_in]])
        if not isinstance(vals, (tuple, list)):
            vals = (vals,)
        for r, v in zip(refs[n_in:], vals):
            r[...] = v.astype(r.dtype)

    def spec(l, c):
        return pl.BlockSpec((l, tr, c), lambda i: (0, i, 0))

    res = pl.pallas_call(
        body, name=name, grid=(rows // tr,),
        in_specs=[spec(a.shape[0], a.shape[2]) for a in ins],
        out_specs=[spec(l, c) for l, c, _ in outs],
        out_shape=[jax.ShapeDtypeStruct((l, rows, c), dt) for l, c, dt in outs],
        compiler_params=_cparams("arbitrary"),
    )(*ins)
    return res


def _as_rows(a, lead=0):
    shape = a.shape
    l = int(math.prod(shape[:lead])) if lead else 1
    rest = shape[lead:]
    c = rest[-1] if rest else 1
    r = int(math.prod(rest[:-1])) if len(rest) > 1 else 1
    return a.reshape(l, r, c)


def act_call(fn, ins, outs, name):
    bsz, seq = ins[0].shape[0], ins[0].shape[1]
    per_row = sum(a.shape[2] * a.dtype.itemsize for a in ins if a.shape[1] == seq)
    per_row += sum(c * jnp.dtype(dt).itemsize for c, dt, kind in outs if kind == "tile")
    ts = _divisor_tile(seq, max(16, TILE_BUDGET_BYTES // per_row), 16)
    n_in = len(ins)

    def in_spec(a):
        c = a.shape[2]
        if a.shape[1] == seq:
            return pl.BlockSpec((None, ts, c), lambda b, s: (b, s, 0))
        if a.shape[0] == bsz:
            return pl.BlockSpec((None, 1, c), lambda b, s: (b, 0, 0))
        return pl.BlockSpec((None, 1, c), lambda b, s: (0, 0, 0))

    def out_spec(c, kind):
        if kind == "tile":
            return pl.BlockSpec((None, ts, c), lambda b, s: (b, s, 0))
        if kind == "seq":
            return pl.BlockSpec((None, 1, c), lambda b, s: (b, 0, 0))
        return pl.BlockSpec((None, 1, c), lambda b, s: (0, 0, 0))

    def out_shape(c, dt, kind):
        if kind == "tile":
            return jax.ShapeDtypeStruct((bsz, seq, c), dt)
        return jax.ShapeDtypeStruct((bsz if kind == "seq" else 1, 1, c), dt)

    def accumulate(ref, v, first):
        @pl.when(first)
        def _():
            ref[...] = jnp.zeros_like(ref)

        ref[...] += v.astype(ref.dtype)

    def body(*refs):
        b, s = pl.program_id(0), pl.program_id(1)
        vals = fn(*[r[...] for r in refs[:n_in]])
        if not isinstance(vals, (tuple, list)):
            vals = (vals,)
        for ref, v, (_, _, kind) in zip(refs[n_in:], vals, outs):
            if kind == "tile":
                ref[...] = v.astype(ref.dtype)
            elif kind == "seq":
                accumulate(ref, v, s == 0)
            else:
                accumulate(ref, v, jnp.logical_and(b == 0, s == 0))

    return pl.pallas_call(
        body, name=name, grid=(bsz, seq // ts),
        in_specs=[in_spec(a) for a in ins],
        out_specs=[out_spec(c, kind) for c, _, kind in outs],
        out_shape=[out_shape(*o) for o in outs],
        compiler_params=_cparams("arbitrary", "arbitrary"),
    )(*ins)


def _mm(name, grid, a, a_spec, b, b_spec, dims, out_shape, out_spec, out_dtypes, acc_steps,
        epi=None, extras=(), extra_spec=None):
    n_ex, n_out = len(extras), len(out_dtypes)
    tile = tuple(d for d in out_spec.block_shape if d is not None)

    def body(*refs):
        a_ref, b_ref = refs[0], refs[1]
        ex_refs = refs[2:2 + n_ex]
        o_refs = refs[2 + n_ex:2 + n_ex + n_out]
        p = lax.dot_general(a_ref[...].astype(BF16), b_ref[...].astype(BF16), (dims, ((), ())),
                            preferred_element_type=F32)

        def finish(acc):
            vals = epi(acc, *[r[...] for r in ex_refs]) if epi is not None else (acc,) * n_out
            for r, v in zip(o_refs, vals):
                r[...] = v.astype(r.dtype)

        if not acc_steps:
            finish(p)
        else:
            acc_ref = refs[-1]
            s = pl.program_id(1)

            @pl.when(s == 0)
            def _():
                acc_ref[...] = p

            @pl.when(s > 0)
            def _():
                acc_ref[...] += p

            @pl.when(s == acc_steps - 1)
            def _():
                finish(acc_ref[...])

    res = pl.pallas_call(
        body, name=name, grid=grid,
        in_specs=[a_spec, b_spec] + [extra_spec] * n_ex,
        out_specs=[out_spec] * n_out,
        out_shape=[jax.ShapeDtypeStruct(out_shape, dt) for dt in out_dtypes],
        scratch_shapes=[pltpu.VMEM(tile, F32)] if acc_steps else [],
        compiler_params=_cparams("arbitrary", "arbitrary"),
    )(a, b, *extras)
    return res if n_out > 1 else res[0]


def mm_nn_col(a, w, name, out_dtypes=(F32,), epi=None):
    m, k = a.shape
    ns, _, nb = w.shape
    tm = _tile_m(m)
    return _mm(name, (m // tm, ns), a, pl.BlockSpec((tm, k), lambda i, j: (i, 0)),
               w, pl.BlockSpec((None, k, nb), lambda i, j: (j, 0, 0)), ((1,), (0,)),
               (m, ns * nb), pl.BlockSpec((tm, nb), lambda i, j: (i, j)), out_dtypes, 0, epi)


def mm_nn_row(a, w, name, out_dtypes=(F32,)):
    m = a.shape[0]
    ns, kb, n = w.shape
    tm = _tile_m(m)
    return _mm(name, (m // tm, ns), a, pl.BlockSpec((tm, kb), lambda i, s: (i, s)),
               w, pl.BlockSpec((None, kb, n), lambda i, s: (s, 0, 0)), ((1,), (0,)),
               (m, n), pl.BlockSpec((tm, n), lambda i, s: (i, 0)), out_dtypes, ns)


def mm_nt_col(dc, w, name, out_dtypes=(F32,), epi=None, extras=()):
    m = dc.shape[0]
    ns, k, nb = w.shape
    tm = _tile_m(m)
    spec = pl.BlockSpec((tm, k), lambda i, s: (i, 0))
    return _mm(name, (m // tm, ns), dc, pl.BlockSpec((tm, nb), lambda i, s: (i, s)),
               w, pl.BlockSpec((None, k, nb), lambda i, s: (s, 0, 0)), ((1,), (1,)),
               (m, k), spec, out_dtypes, ns, epi, extras, spec)


def mm_nt_row(dc, w, name, out_dtypes=(F32,), epi=None, extras=()):
    m, n = dc.shape
    ns, kb, _ = w.shape
    tm = _tile_m(m)
    spec = pl.BlockSpec((tm, kb), lambda i, s: (i, s))
    return _mm(name, (m // tm, ns), dc, pl.BlockSpec((tm, n), lambda i, s: (i, 0)),
               w, pl.BlockSpec((None, kb, n), lambda i, s: (s, 0, 0)), ((1,), (1,)),
               (m, ns * kb), spec, out_dtypes, 0, epi, extras, spec)


def mm_tn(a, c, slab, ns, name, out_dtype=BF16):
    m, ka_all = a.shape
    nc_all = c.shape[1]
    ka = ka_all // ns if slab == "a" else ka_all
    nc = nc_all // ns if slab == "c" else nc_all
    tt = _divisor_tile(m, 256, 16)
    steps = m // tt

    def body(a_ref, c_ref, o_ref, acc_ref):
        t = pl.program_id(0)

        @pl.when(t == 0)
        def _():
            acc_ref[...] = jnp.zeros_like(acc_ref)

        for s in range(ns):
            a_s = a_ref[:, s * ka:(s + 1) * ka] if slab == "a" else a_ref[...]
            c_s = c_ref[:, s * nc:(s + 1) * nc] if slab == "c" else c_ref[...]
            acc_ref[s] += lax.dot_general(a_s.astype(BF16), c_s.astype(BF16), (((0,), (0,)), ((), ())),
                                          preferred_element_type=F32)

        @pl.when(t == steps - 1)
        def _():
            o_ref[...] = acc_ref[...].astype(o_ref.dtype)

    return pl.pallas_call(
        body, name=name, grid=(steps,),
        in_specs=[pl.BlockSpec((tt, ka_all), lambda t: (t, 0)), pl.BlockSpec((tt, nc_all), lambda t: (t, 0))],
        out_specs=pl.BlockSpec((ns, ka, nc), lambda t: (0, 0, 0)),
        out_shape=jax.ShapeDtypeStruct((ns, ka, nc), out_dtype),
        scratch_shapes=[pltpu.VMEM((ns, ka, nc), F32)],
        compiler_params=_cparams("arbitrary"),
    )(a, c)


def reduce_scatter(part, name):
    shape = part.shape[1:]
    ac = lax.axis_index("c")
    from_sibling = pair_exchange(part, name + "_pair")
    own = lax.dynamic_index_in_dim(part.reshape((N_CHIPS, 2) + shape), ac, axis=1, keepdims=False)
    pair = rows_call(lambda a, b: a.astype(F32) + b.astype(F32),
                     [_as_rows(own, 1), _as_rows(from_sibling, 1)],
                     [(N_CHIPS, shape[-1], BF16)], name + "_pairsum")[0]
    got = chip_exchange(pair.reshape((N_CHIPS,) + shape), name + "_chips")
    total = rows_call(lambda g: jnp.sum(g.astype(F32), axis=0, keepdims=True),
                      [_as_rows(got, 1)], [(1, shape[-1], F32)], name + "_sum")[0]
    return total.reshape(shape)


def all_reduce_small(leaves, name):
    sizes = [int(a.size) for a in leaves]
    flat = jnp.concatenate([a.reshape(-1) for a in leaves])
    total = int(flat.size)
    padded = -(-total // (16 * LANES)) * (16 * LANES)
    flat = jnp.pad(flat, (0, padded - total)).reshape(padded // LANES, LANES)
    gathered = all_gather(flat, name + "_gather")
    summed = rows_call(lambda g: jnp.sum(g, axis=0, keepdims=True), [gathered],
                       [(1, LANES, F32)], name + "_sum")[0].reshape(-1)
    out, at = [], 0
    for a, n in zip(leaves, sizes):
        out.append(summed[at:at + n].reshape(a.shape))
        at += n
    return out


def adamw(w, g, m, v, name):
    c = w.shape[-1] if w.ndim else 1

    def fn(w_, g_, m_, v_):
        nm = ADAM_B1 * m_ + (1.0 - ADAM_B1) * g_
        nv = ADAM_B2 * v_ + (1.0 - ADAM_B2) * (g_ * g_)
        m_hat = nm / (1.0 - ADAM_B1 ** ADAM_STEP)
        v_hat = nv / (1.0 - ADAM_B2 ** ADAM_STEP)
        delta = -ADAM_LR * (m_hat / (jnp.sqrt(v_hat) + ADAM_EPS) + ADAM_WD * w_)
        return delta, nm, nv

    res = rows_call(fn, [_as_rows(t) for t in (w, g.astype(F32), m, v)], [(1, c, F32)] * 3, name)
    return tuple(r.reshape(w.shape) for r in res)


def _rowsum(v):
    return jnp.sum(v, axis=0, keepdims=True)


def _norm_mod(x, g, sh, sc):
    n = x * lax.rsqrt(jnp.mean(x * x, axis=-1, keepdims=True) + EPS)
    return (n * g) * (1.0 + sc) + sh


def _norm_mod_bwd(x, g, sc, dh, dres):
    r = lax.rsqrt(jnp.mean(x * x, axis=-1, keepdims=True) + EPS)
    n = x * r
    dy = dh * (1.0 + sc)
    dn = dy * g
    dx = r * (dn - n * jnp.mean(dn * n, axis=-1, keepdims=True))
    return dres + dx, _rowsum(dh), _rowsum(dh * (n * g)), _rowsum(dy * n)


def _head_mean(v):
    low = lax.broadcasted_iota(jnp.int32, (1, LANES), 1) < HEAD_DIM
    parts = []
    for p in range(v.shape[1] // LANES):
        blk = v[:, p * LANES:(p + 1) * LANES]
        s0 = jnp.sum(jnp.where(low, blk, 0.0), axis=-1, keepdims=True)
        s1 = jnp.sum(jnp.where(low, 0.0, blk), axis=-1, keepdims=True)
        parts.append(jnp.where(low, s0, s1))
    return jnp.concatenate(parts, axis=1) * (1.0 / HEAD_DIM)


def _head_norm(x, g):
    return x * lax.rsqrt(_head_mean(x * x) + EPS) * g


def _head_norm_bwd(x, g, dy):
    r = lax.rsqrt(_head_mean(x * x) + EPS)
    n = x * r
    dn = dy * g
    return r * (dn - n * _head_mean(dn * n)), _rowsum(dy * n)


GELU_C = math.sqrt(2.0 / math.pi)
GELU_A = 0.044715


def _gelu_grad(y):
    t = jnp.tanh(GELU_C * (y + GELU_A * y * y * y))
    return 0.5 * (1.0 + t) + 0.5 * y * (1.0 - t * t) * GELU_C * (1.0 + 3.0 * GELU_A * y * y)


def ada_fwd(c_all, w_cols, b_cols):
    def body(c_ref, w_ref, b_ref, o_ref):
        c = c_ref[...]
        s = (c * jax.nn.sigmoid(c)).astype(BF16)
        o_ref[...] = jnp.dot(s, w_ref[...].astype(BF16), preferred_element_type=F32) + b_ref[...]

    return pl.pallas_call(
        body, name="ada_fwd", out_shape=jax.ShapeDtypeStruct((c_all.shape[0], w_cols.shape[1]), F32),
        compiler_params=pltpu.CompilerParams(vmem_limit_bytes=VMEM_LIMIT_BYTES),
    )(c_all, w_cols, b_cols)


def ada_bwd(c_all, dm_cols, dm_all):
    def body(c_ref, d_ref, all_ref, dw_ref, db_ref):
        c = c_ref[...]
        s = (c * jax.nn.sigmoid(c)).astype(BF16)
        dw_ref[...] = lax.dot_general(s, d_ref[...].astype(BF16), (((0,), (0,)), ((), ())),
                                      preferred_element_type=F32)
        db_ref[...] = jnp.sum(all_ref[...], axis=0, keepdims=True)

    return pl.pallas_call(
        body, name="ada_bwd",
        out_shape=[jax.ShapeDtypeStruct((c_all.shape[1], dm_cols.shape[1]), F32),
                   jax.ShapeDtypeStruct((1, dm_all.shape[1]), F32)],
        compiler_params=pltpu.CompilerParams(vmem_limit_bytes=VMEM_LIMIT_BYTES),
    )(c_all, dm_cols, dm_all)


def _s5_discretise(lam_re, lam_im, log_dt, b_re, b_im):
    dt = jnp.exp(log_dt)
    mag = jnp.exp(lam_re * dt)
    ab_re = mag * jnp.cos(lam_im * dt)
    ab_im = mag * jnp.sin(lam_im * dt)
    den = lam_re * lam_re + lam_im * lam_im
    nr = ab_re - 1.0
    ni = ab_im
    f_re = (nr * lam_re + ni * lam_im) / den
    f_im = (ni * lam_re - nr * lam_im) / den
    bb_re = f_re * b_re - f_im * b_im
    bb_im = f_re * b_im + f_im * b_re
    return ab_re, ab_im, bb_re, bb_im


def s5_prep(lam_re, lam_im, log_dt, b_re, b_im):
    gp, h = b_re.shape

    def body(lr, li, ld, br, bi, o_ar, o_ai, o_br, o_bi):
        res = _s5_discretise(lr[...], li[...], ld[...], br[...], bi[...])
        for r, v in zip((o_ar, o_ai, o_br, o_bi), res):
            r[...] = v

    col, mat = jax.ShapeDtypeStruct((gp, 1), F32), jax.ShapeDtypeStruct((gp, h), F32)
    return pl.pallas_call(body, name="s5_prep", out_shape=[col, col, mat, mat])(lam_re, lam_im, log_dt, b_re, b_im)


def s5_prep_bwd(lam_re, lam_im, log_dt, b_re, b_im, d_ab_re, d_ab_im, d_bb_re, d_bb_im):
    gp, h = b_re.shape

    def body(lr, li, ld, br, bi, g_ar, g_ai, g_br, g_bi, o_lr, o_li, o_ld, o_br, o_bi):
        _, vjp = jax.vjp(_s5_discretise, lr[...], li[...], ld[...], br[...], bi[...])
        res = vjp((g_ar[...], g_ai[...], g_br[...], g_bi[...]))
        for r, v in zip((o_lr, o_li, o_ld, o_br, o_bi), res):
            r[...] = v

    col, mat = jax.ShapeDtypeStruct((gp, 1), F32), jax.ShapeDtypeStruct((gp, h), F32)
    return pl.pallas_call(body, name="s5_prep_bwd", out_shape=[col, col, col, mat, mat])(
        lam_re, lam_im, log_dt, b_re, b_im, d_ab_re, d_ab_im, d_bb_re, d_bb_im)


def _s5_chunk(seq):
    return _divisor_tile(seq, 256, 16)


def s5_fwd(u, bbd_re, bbd_im, cbd_re, cbd_im, ab_re, ab_im, dskip):
    bsz, seq, d = u.shape
    nb, cb, ns = bbd_re.shape
    lc = _s5_chunk(seq)

    def body(u_ref, bre_ref, bim_ref, cre_ref, cim_ref, ar_ref, ai_ref, d_ref, y_ref, sre_ref, sim_ref,
             carry_re, carry_im):
        t = pl.program_id(2)

        @pl.when(t == 0)
        def _():
            carry_re[...] = jnp.zeros_like(carry_re)
            carry_im[...] = jnp.zeros_like(carry_im)

        uf = u_ref[...]
        ub = uf.astype(BF16)
        sre_ref[...] = jnp.dot(ub, bre_ref[...], preferred_element_type=F32)
        sim_ref[...] = jnp.dot(ub, bim_ref[...], preferred_element_type=F32)
        ar, ai = ar_ref[...], ai_ref[...]

        def step(i, c):
            cr, ci = c
            row = pl.ds(i, 1)
            nr = ar * cr - ai * ci + sre_ref[row, :]
            ni = ar * ci + ai * cr + sim_ref[row, :]
            sre_ref[row, :] = nr
            sim_ref[row, :] = ni
            return nr, ni

        cr, ci = lax.fori_loop(0, lc, step, (carry_re[...], carry_im[...]), unroll=8)
        carry_re[...] = cr
        carry_im[...] = ci
        y = jnp.dot(sre_ref[...].astype(BF16), cre_ref[...], preferred_element_type=F32)
        y -= jnp.dot(sim_ref[...].astype(BF16), cim_ref[...], preferred_element_type=F32)
        y_ref[...] = y + d_ref[...] * uf

    chan = pl.BlockSpec((None, lc, cb), lambda n, b, t: (b, t, n))
    state = pl.BlockSpec((None, lc, ns), lambda n, b, t: (b, t, n))
    par = lambda r, c: pl.BlockSpec((None, r, c), lambda n, b, t: (n, 0, 0))
    return pl.pallas_call(
        body, name="s5_fwd", grid=(nb, bsz, seq // lc),
        in_specs=[chan, par(cb, ns), par(cb, ns), par(ns, cb), par(ns, cb), par(1, ns), par(1, ns),
                  pl.BlockSpec((None, 1, cb), lambda n, b, t: (0, 0, n))],
        out_specs=[chan, state, state],
        out_shape=[jax.ShapeDtypeStruct((bsz, seq, d), F32),
                   jax.ShapeDtypeStruct((bsz, seq, nb * ns), F32),
                   jax.ShapeDtypeStruct((bsz, seq, nb * ns), F32)],
        scratch_shapes=[pltpu.VMEM((1, ns), F32), pltpu.VMEM((1, ns), F32)],
        compiler_params=_cparams("arbitrary", "arbitrary", "arbitrary"),
    )(u, bbd_re, bbd_im, cbd_re, cbd_im, ab_re, ab_im, dskip)


def s5_bwd(dy, u, st_re, st_im, bbd_re, bbd_im, cbd_re, cbd_im, ab_re, ab_im, dskip):
    bsz, seq, d = u.shape
    nb, cb, ns = bbd_re.shape
    lc = _s5_chunk(seq)
    nc = seq // lc

    def body(dy_ref, u_ref, sre_ref, sim_ref, bre_ref, bim_ref, cre_ref, cim_ref, ar_ref, ai_ref, d_ref,
             du_ref, dbre_ref, dbim_ref, dcre_ref, dcim_ref, dar_ref, dai_ref, dd_ref,
             g_re, g_im, gs_re, gs_im, carry_re, carry_im):
        b, t = pl.program_id(1), pl.program_id(2)

        @pl.when(jnp.logical_and(b == 0, t == 0))
        def _():
            for r in (dbre_ref, dbim_ref, dcre_ref, dcim_ref, dar_ref, dai_ref, dd_ref):
                r[...] = jnp.zeros_like(r)

        @pl.when(t == 0)
        def _():
            carry_re[...] = jnp.zeros_like(carry_re)
            carry_im[...] = jnp.zeros_like(carry_im)

        dyf, uf = dy_ref[...], u_ref[...]
        dyb, ub = dyf.astype(BF16), uf.astype(BF16)
        nt = (((1,), (1,)), ((), ()))
        tn = (((0,), (0,)), ((), ()))
        g_re[...] = lax.dot_general(dyb, cre_ref[...], nt, preferred_element_type=F32)
        g_im[...] = -lax.dot_general(dyb, cim_ref[...], nt, preferred_element_type=F32)
        ar, ai = ar_ref[...], ai_ref[...]

        def step(k, c):
            cr, ci = c
            row = pl.ds(lc - 1 - k, 1)
            gs_re[row, :] = cr
            gs_im[row, :] = ci
            nr = ar * cr + ai * ci + g_re[row, :]
            ni = ar * ci - ai * cr + g_im[row, :]
            g_re[row, :] = nr
            g_im[row, :] = ni
            return nr, ni

        cr, ci = lax.fori_loop(0, lc, step, (carry_re[...], carry_im[...]), unroll=8)
        carry_re[...] = cr
        carry_im[...] = ci

        sr, si = sre_ref[...], sim_ref[...]
        hr, hi = gs_re[...], gs_im[...]
        dar_ref[...] += _rowsum(hr * sr + hi * si)
        dai_ref[...] += _rowsum(hi * sr - hr * si)
        gr, gi = g_re[...].astype(BF16), g_im[...].astype(BF16)
        du = lax.dot_general(gr, bre_ref[...], nt, preferred_element_type=F32)
        du += lax.dot_general(gi, bim_ref[...], nt, preferred_element_type=F32)
        du_ref[...] = du + d_ref[...] * dyf
        dbre_ref[...] += lax.dot_general(ub, gr, tn, preferred_element_type=F32)
        dbim_ref[...] += lax.dot_general(ub, gi, tn, preferred_element_type=F32)
        dcre_ref[...] += lax.dot_general(sr.astype(BF16), dyb, tn, preferred_element_type=F32)
        dcim_ref[...] -= lax.dot_general(si.astype(BF16), dyb, tn, preferred_element_type=F32)
        dd_ref[...] += _rowsum(dyf * uf)

    chan = pl.BlockSpec((None, lc, cb), lambda n, b, t: (b, nc - 1 - t, n))
    state = pl.BlockSpec((None, lc, ns), lambda n, b, t: (b, nc - 1 - t, n))
    par = lambda r, c: pl.BlockSpec((None, r, c), lambda n, b, t: (n, 0, 0))
    return pl.pallas_call(
        body, name="s5_bwd", grid=(nb, bsz, nc),
        in_specs=[chan, chan, state, state, par(cb, ns), par(cb, ns), par(ns, cb), par(ns, cb),
                  par(1, ns), par(1, ns), pl.BlockSpec((None, 1, cb), lambda n, b, t: (0, 0, n))],
        out_specs=[chan, par(cb, ns), par(cb, ns), par(ns, cb), par(ns, cb), par(1, ns), par(1, ns), par(1, cb)],
        out_shape=[jax.ShapeDtypeStruct((bsz, seq, d), F32),
                   jax.ShapeDtypeStruct((nb, cb, ns), F32), jax.ShapeDtypeStruct((nb, cb, ns), F32),
                   jax.ShapeDtypeStruct((nb, ns, cb), F32), jax.ShapeDtypeStruct((nb, ns, cb), F32),
                   jax.ShapeDtypeStruct((nb, 1, ns), F32), jax.ShapeDtypeStruct((nb, 1, ns), F32),
                   jax.ShapeDtypeStruct((nb, 1, cb), F32)],
        scratch_shapes=[pltpu.VMEM((lc, ns), F32)] * 4 + [pltpu.VMEM((1, ns), F32)] * 2,
        compiler_params=_cparams("arbitrary", "arbitrary", "arbitrary"),
    )(dy, u, st_re, st_im, bbd_re, bbd_im, cbd_re, cbd_im, ab_re, ab_im, dskip)


ATT_HEADS = 4
ATT_LANES = ATT_HEADS * HEAD_DIM
ATT_KEYS = 2 * ATT_BLOCK
ATT_SCALE = 1.0 / math.sqrt(HEAD_DIM)
_NT = (((1,), (1,)), ((), ()))
_TN = (((0,), (0,)), ((), ()))
_HEADS = [slice(h * HEAD_DIM, (h + 1) * HEAD_DIM) for h in range(ATT_HEADS)]
_HALF = [slice(0, ATT_BLOCK), slice(ATT_BLOCK, ATT_KEYS)]


def _log_sigmoids(z):
    sp = jnp.log(1.0 + jnp.exp(-jnp.abs(z)))
    ls = jnp.minimum(z, 0.0) - sp
    return ls, ls - z


def _sum_matrix(after, inclusive):
    j = lax.broadcasted_iota(jnp.int32, (ATT_KEYS, ATT_KEYS), 0) % ATT_BLOCK
    s = lax.broadcasted_iota(jnp.int32, (ATT_KEYS, ATT_KEYS), 1)
    if after:
        hit = (j >= s) if inclusive else (j > s)
    else:
        hit = (j <= s) if inclusive else (j < s)
    return jnp.where(jnp.logical_or(hit, s >= ATT_BLOCK), 1.0, 0.0).astype(BF16)


def _hi_lo(v):
    hi = v.astype(BF16)
    lo = (v - hi.astype(F32)).astype(BF16)
    return jnp.concatenate([hi, lo], axis=1)


def _strict_mask(i, j):
    t = i * ATT_BLOCK + lax.broadcasted_iota(jnp.int32, (ATT_BLOCK, ATT_KEYS), 0)
    s = j * ATT_KEYS + lax.broadcasted_iota(jnp.int32, (ATT_BLOCK, ATT_KEYS), 1)
    return s < t


def attention_fwd(q, k, v):
    bsz, seq, d = q.shape

    def body(q_ref, k_ref, v_ref, o_ref, tot_ref, z_buf, ls_buf, cs_buf, acc_buf, run_buf):
        i = pl.program_id(2)
        jd = i // 2
        sums = _sum_matrix(True, False)
        acc_buf[...] = jnp.zeros_like(acc_buf)
        run_buf[...] = jnp.zeros_like(run_buf)

        def block(j, masked):
            rows = pl.ds(pl.multiple_of(j * ATT_KEYS, ATT_KEYS), ATT_KEYS)
            strict = _strict_mask(i, j) if masked else None
            for h, ln in enumerate(_HEADS):
                z_buf[h] = lax.dot_general(q_ref[:, ln], k_ref[rows, ln], _NT, preferred_element_type=F32)
            for h in range(ATT_HEADS):
                for half, cols in enumerate(_HALF):
                    ls, lf = _log_sigmoids(z_buf[h, :, cols])
                    if masked:
                        lf = jnp.where(strict[:, cols], lf, 0.0)
                    ls_buf[h, :, cols] = ls
                    cs_buf[h, half] = jnp.dot(_hi_lo(lf), sums, preferred_element_type=F32)
            for h, ln in enumerate(_HEADS):
                run = run_buf[h]
                late, early = cs_buf[h, 1], cs_buf[h, 0]
                a1 = run + late[:, _HALF[0]]
                run = run + late[:, _HALF[1]]
                a0 = run + early[:, _HALF[0]]
                run_buf[h] = run + early[:, _HALF[1]]
                w = jnp.exp(ls_buf[h] + jnp.concatenate([a0, a1], axis=1))
                if masked:
                    w = jnp.where(strict, w, 0.0)
                acc_buf[h] += jnp.dot(w.astype(BF16), v_ref[rows, ln], preferred_element_type=F32)

        block(jd, True)

        def step(it, carry):
            block(jd - 1 - it, False)
            return carry

        lax.fori_loop(0, jd, step, 0)
        o_ref[...] = jnp.concatenate([acc_buf[h] for h in range(ATT_HEADS)], axis=1).astype(o_ref.dtype)
        tot_ref[...] = jnp.concatenate([run_buf[h, :, :HEAD_DIM] for h in range(ATT_HEADS)], axis=1)

    blk = pl.BlockSpec((None, ATT_BLOCK, ATT_LANES), lambda b, p, i: (b, i, p))
    full = pl.BlockSpec((None, seq, ATT_LANES), lambda b, p, i: (b, 0, p))
    tile = (ATT_HEADS, ATT_BLOCK, ATT_KEYS)
    return pl.pallas_call(
        body, name="attention_fwd", grid=(bsz, d // ATT_LANES, seq // ATT_BLOCK),
        in_specs=[blk, full, full], out_specs=[blk, blk],
        out_shape=[jax.ShapeDtypeStruct((bsz, seq, d), BF16), jax.ShapeDtypeStruct((bsz, seq, d), F32)],
        scratch_shapes=[pltpu.VMEM(tile, F32), pltpu.VMEM(tile, F32),
                        pltpu.VMEM((ATT_HEADS, 2, ATT_BLOCK, ATT_KEYS), F32),
                        pltpu.VMEM((ATT_HEADS, ATT_BLOCK, HEAD_DIM), F32),
                        pltpu.VMEM((ATT_HEADS, ATT_BLOCK, ATT_BLOCK), F32)],
        compiler_params=_cparams("arbitrary", "arbitrary", "arbitrary"),
    )(q, k, v)


def attention_bwd(q, k, v, tot, do):
    bsz, seq, d = q.shape

    def body(q_ref, k_ref, v_ref, tot_ref, do_ref, dq_ref, dk_ref, dv_ref,
             z_buf, dw_buf, ls_buf, e_buf, up_buf, bf_buf, w_buf, do_buf, dq_buf, tot_buf, run_buf, erun_buf):
        i = pl.program_id(2)
        jd = i // 2

        @pl.when(i == 0)
        def _():
            dk_ref[...] = jnp.zeros_like(dk_ref)
            dv_ref[...] = jnp.zeros_like(dv_ref)

        upto_incl, upto_excl = _sum_matrix(False, True), _sum_matrix(False, False)
        do_buf[...] = do_ref[...].astype(BF16)
        for h, ln in enumerate(_HEADS):
            tot_buf[h] = jnp.concatenate([tot_ref[:, ln], tot_ref[:, ln]], axis=1)
        dq_buf[...] = jnp.zeros_like(dq_buf)
        run_buf[...] = jnp.zeros_like(run_buf)
        erun_buf[...] = jnp.zeros_like(erun_buf)

        def block(j, masked):
            rows = pl.ds(pl.multiple_of(j * ATT_KEYS, ATT_KEYS), ATT_KEYS)
            strict = _strict_mask(i, j) if masked else None
            for h, ln in enumerate(_HEADS):
                z_buf[h] = lax.dot_general(q_ref[:, ln], k_ref[rows, ln], _NT, preferred_element_type=F32)
                dw_buf[h] = lax.dot_general(do_buf[:, ln], v_ref[rows, ln], _NT, preferred_element_type=F32)
            for h in range(ATT_HEADS):
                for half, cols in enumerate(_HALF):
                    ls, lf = _log_sigmoids(z_buf[h, :, cols])
                    if masked:
                        lf = jnp.where(strict[:, cols], lf, 0.0)
                    ls_buf[h, :, cols] = ls
                    up_buf[h, half] = jnp.dot(_hi_lo(lf), upto_incl, preferred_element_type=F32)
            for h in range(ATT_HEADS):
                run = run_buf[h]
                early, late = up_buf[h, 0], up_buf[h, 1]
                u0 = run + early[:, _HALF[0]]
                run = run + early[:, _HALF[1]]
                u1 = run + late[:, _HALF[0]]
                run_buf[h] = run + late[:, _HALF[1]]
                tot_h = tot_buf[h]
                after = jnp.concatenate([tot_h - u0, tot_h - u1], axis=1)
                w = jnp.exp(ls_buf[h] + after)
                if masked:
                    w = jnp.where(strict, w, 0.0)
                w_buf[h] = w.astype(BF16)
                e = dw_buf[h] * w
                e_buf[h] = e
                for half, cols in enumerate(_HALF):
                    bf_buf[h, half] = jnp.dot(_hi_lo(e[:, cols]), upto_excl, preferred_element_type=F32)
            dks, dvs = [], []
            for h, ln in enumerate(_HEADS):
                erun = erun_buf[h]
                early, late = bf_buf[h, 0], bf_buf[h, 1]
                b0 = erun + early[:, _HALF[0]]
                erun = erun + early[:, _HALF[1]]
                b1 = erun + late[:, _HALF[0]]
                erun_buf[h] = erun + late[:, _HALF[1]]
                e = e_buf[h]
                dz = e - jnp.exp(ls_buf[h]) * (e + jnp.concatenate([b0, b1], axis=1))
                if masked:
                    dz = jnp.where(strict, dz, 0.0)
                dz = dz.astype(BF16)
                dq_buf[h] += jnp.dot(dz, k_ref[rows, ln], preferred_element_type=F32)
                dks.append(lax.dot_general(dz, q_ref[:, ln], _TN, preferred_element_type=F32))
                dvs.append(lax.dot_general(w_buf[h], do_buf[:, ln], _TN, preferred_element_type=F32))
            dk_ref[rows, :] += jnp.concatenate(dks, axis=1)
            dv_ref[rows, :] += jnp.concatenate(dvs, axis=1)

        def step(j, carry):
            block(j, False)
            return carry

        lax.fori_loop(0, jd, step, 0)
        block(jd, True)
        dq_ref[...] = jnp.concatenate([dq_buf[h] for h in range(ATT_HEADS)], axis=1) * ATT_SCALE

    blk = pl.BlockSpec((None, ATT_BLOCK, ATT_LANES), lambda b, p, i: (b, i, p))
    full = pl.BlockSpec((None, seq, ATT_LANES), lambda b, p, i: (b, 0, p))
    shape = jax.ShapeDtypeStruct((bsz, seq, d), F32)
    tile = (ATT_HEADS, ATT_BLOCK, ATT_KEYS)
    pair = (ATT_HEADS, 2, ATT_BLOCK, ATT_KEYS)
    square = (ATT_HEADS, ATT_BLOCK, ATT_BLOCK)
    return pl.pallas_call(
        body, name="attention_bwd", grid=(bsz, d // ATT_LANES, seq // ATT_BLOCK),
        in_specs=[blk, full, full, blk, blk], out_specs=[blk, full, full], out_shape=[shape, shape, shape],
        scratch_shapes=[pltpu.VMEM(tile, F32), pltpu.VMEM(tile, F32), pltpu.VMEM(tile, F32), pltpu.VMEM(tile, F32),
                        pltpu.VMEM(pair, F32), pltpu.VMEM(pair, F32), pltpu.VMEM(tile, BF16),
                        pltpu.VMEM((ATT_BLOCK, ATT_LANES), BF16), pltpu.VMEM((ATT_HEADS, ATT_BLOCK, HEAD_DIM), F32),
                        pltpu.VMEM(square, F32), pltpu.VMEM(square, F32), pltpu.VMEM(square, F32)],
        compiler_params=_cparams("arbitrary", "arbitrary", "arbitrary"),
    )(q, k, v, tot, do)


def mlp_fwd(x, g, sh, sc, gate, w1, w2, tag):
    bsz, seq, d = x.shape
    t = bsz * seq
    h = act_call(_norm_mod, [x, g, sh, sc], [(d, BF16, "tile")], tag + "_norm")[0]
    act = mm_nn_col(h.reshape(t, d), w1, tag + "_up", (BF16,),
                    lambda acc: (jnp.square(jnp.maximum(acc, 0.0)),))
    ff = mm_nn_row(act, w2, tag + "_down").reshape(bsz, seq, d)
    out = act_call(lambda x_, f_, g_: x_ + g_ * f_, [x, ff, gate], [(d, F32, "tile")], tag + "_res")[0]
    return out, (h, act, ff)


def mlp_bwd(dout, x, g, sc, gate, w1, w2, saved, tag):
    bsz, seq, d = x.shape
    t = bsz * seq
    ns = w1.shape[0]
    h, act, ff = saved
    dff, dgate = act_call(lambda do_, f_, g_: (g_ * do_, _rowsum(do_ * f_)), [dout, ff, gate],
                          [(d, BF16, "tile"), (d, F32, "seq")], tag + "_dres")
    dff = dff.reshape(t, d)
    dpre = mm_nt_row(dff, w2, tag + "_dact", (BF16,),
                     lambda acc, a_: (acc * (2.0 * jnp.sqrt(a_.astype(F32))),), (act,))
    dw2 = mm_tn(act, dff, "a", ns, tag + "_dw2")
    dw1 = mm_tn(h.reshape(t, d), dpre, "c", ns, tag + "_dw1")
    dh = mm_nt_col(dpre, w1, tag + "_dh").reshape(bsz, seq, d)
    dx, dsh, dsc, dg = act_call(_norm_mod_bwd, [x, g, sc, dh, dout],
                                [(d, F32, "tile"), (d, F32, "seq"), (d, F32, "seq"), (d, F32, "all")],
                                tag + "_dnorm")
    return dx, dw1, dw2, (dsh, dsc, dgate, dg)


def _block_diag(m, rows_first):
    nb, k, r, c = m.shape
    eye = jnp.eye(k, dtype=m.dtype)
    return jnp.einsum("nkrc,kl->nkrlc", m, eye).reshape(nb, k * r, k * c)


def _block_diag_part(m, r, c):
    nb = m.shape[0]
    k = m.shape[1] // r
    return jnp.einsum("nkrlc,kl->nkrc", m.reshape(nb, k, r, k, c), jnp.eye(k, dtype=m.dtype))


def kernel(x, c, ada_w, ada_b, mix_norm_g, mlp_norm_g, mlp_w1, mlp_w2, s5_a_re, s5_a_im, s5_log_dt, s5_b_re, s5_b_im, s5_c_re, s5_c_im, s5_d, s5_w_glu, kv_ada_w, kv_ada_b, kv_norm_g, w_kv, k_norm_g, sb_w_q, q_norm_g, sb_w_o, loss_target, m_ada_w, m_ada_b, m_mix_norm_g, m_mlp_norm_g, m_mlp_w1, m_mlp_w2, m_s5_a_re, m_s5_a_im, m_s5_log_dt, m_s5_b_re, m_s5_b_im, m_s5_c_re, m_s5_c_im, m_s5_d, m_s5_w_glu, m_kv_ada_w, m_kv_ada_b, m_kv_norm_g, m_w_kv, m_k_norm_g, m_sb_w_q, m_q_norm_g, m_sb_w_o, v_ada_w, v_ada_b, v_mix_norm_g, v_mlp_norm_g, v_mlp_w1, v_mlp_w2, v_s5_a_re, v_s5_a_im, v_s5_log_dt, v_s5_b_re, v_s5_b_im, v_s5_c_re, v_s5_c_im, v_s5_d, v_s5_w_glu, v_kv_ada_w, v_kv_ada_b, v_kv_norm_g, v_w_kv, v_k_norm_g, v_sb_w_q, v_q_norm_g, v_sb_w_o):
    bsz, seq, d = x.shape
    t = bsz * seq
    n_groups = d // S5_GROUP
    nb = n_groups // S5_BLOCK_GROUPS
    gp = n_groups * S5_STATE
    dev = 4 * lax.axis_index("x") + 2 * lax.axis_index("y") + lax.axis_index("c")
    e_ada, e_kv = 6 * d, 2 * d
    n_ada, n_kv = e_ada // N_DEV, e_kv // N_DEV

    w1_all = all_gather(mlp_w1.astype(BF16), "gather_w1")
    w2_all = all_gather(mlp_w2.astype(BF16), "gather_w2")
    w1 = [w1_all[:, i] for i in range(2)]
    w2 = [w2_all[:, i] for i in range(2)]
    w_glu = all_gather(s5_w_glu[0].astype(BF16), "gather_glu")
    wkv = all_gather(w_kv.astype(BF16), "gather_wkv")
    wq = all_gather(sb_w_q[0].astype(BF16), "gather_wq")
    wo = all_gather(sb_w_o[0].astype(BF16), "gather_wo")
    d_skip = all_gather(s5_d, "gather_skip").reshape(1, 1, d)
    c_all = all_gather(c, "gather_c").reshape(N_DEV * bsz, d)

    w_cols = jnp.concatenate([ada_w[0], ada_w[1], kv_ada_w], axis=1)
    b_cols = jnp.concatenate([
        lax.dynamic_slice_in_dim(ada_b[0], dev * n_ada, n_ada),
        lax.dynamic_slice_in_dim(ada_b[1], dev * n_ada, n_ada),
        lax.dynamic_slice_in_dim(kv_ada_b, dev * n_kv, n_kv)])[None, :]
    mod_cols = ada_fwd(c_all, w_cols, b_cols)
    mod_all = all_gather(mod_cols, "gather_mod")
    mod_mine = lax.dynamic_slice_in_dim(mod_all, dev * bsz, bsz, axis=1)
    mod_mine = jnp.transpose(mod_mine, (1, 0, 2))
    mods = []
    for i in range(2):
        full = mod_mine[:, :, i * n_ada:(i + 1) * n_ada].reshape(bsz, e_ada)
        mods.append([full[:, None, j * d:(j + 1) * d] for j in range(6)])
    kv_full = mod_mine[:, :, 2 * n_ada:].reshape(bsz, e_kv)
    kv_sh, kv_sc = kv_full[:, None, :d], kv_full[:, None, d:]

    par = lambda p: p.reshape(1, 1, -1)

    sh_a, sc_a, g_a, sh_m, sc_m, g_m = mods[0]
    lam_re, lam_im = s5_a_re.reshape(gp, 1), s5_a_im.reshape(gp, 1)
    log_dt = jnp.broadcast_to(s5_log_dt.reshape(n_groups, 1), (n_groups, S5_STATE)).reshape(gp, 1)
    b_re, b_im = s5_b_re.reshape(gp, S5_GROUP), s5_b_im.reshape(gp, S5_GROUP)
    ab_re, ab_im, bb_re, bb_im = s5_prep(lam_re, lam_im, log_dt, b_re, b_im)
    to_bbd = lambda m: _block_diag(jnp.swapaxes(m.reshape(nb, S5_BLOCK_GROUPS, S5_STATE, S5_GROUP), 2, 3), True)
    to_cbd = lambda m: _block_diag(jnp.swapaxes(m.reshape(nb, S5_BLOCK_GROUPS, S5_GROUP, S5_STATE), 2, 3), True)
    bbd_re, bbd_im = to_bbd(bb_re).astype(BF16), to_bbd(bb_im).astype(BF16)
    cbd_re, cbd_im = to_cbd(s5_c_re[0]).astype(BF16), to_cbd(s5_c_im[0]).astype(BF16)
    abr, abi = ab_re.reshape(nb, 1, -1), ab_im.reshape(nb, 1, -1)

    h0 = act_call(_norm_mod, [x, par(mix_norm_g[0]), sh_a, sc_a], [(d, F32, "tile")], "mix0_norm")[0]
    y, st_re, st_im = s5_fwd(h0, bbd_re, bbd_im, cbd_re, cbd_im, abr, abi, d_skip)
    ge = act_call(lambda y_: jax.nn.gelu(y_), [y], [(d, BF16, "tile")], "gelu")[0]
    z = mm_nn_col(ge.reshape(t, d), w_glu, "glu_up").reshape(bsz, seq, 2 * d)
    x1 = act_call(lambda x_, z_, g_: x_ + g_ * (z_[:, :d] * jax.nn.sigmoid(z_[:, d:])), [x, z, g_a],
                  [(d, F32, "tile")], "glu_res")[0]
    x2, mlp0_saved = mlp_fwd(x1, par(mlp_norm_g[0]), sh_m, sc_m, g_m, w1[0], w2[0], "mlp0")

    sh_a1, sc_a1, g_a1, sh_m1, sc_m1, g_m1 = mods[1]
    kg = par(jnp.tile(k_norm_g, d // HEAD_DIM))
    qg = par(jnp.tile(q_norm_g[0], d // HEAD_DIM))
    hkv = act_call(_norm_mod, [x2, par(kv_norm_g), kv_sh, kv_sc], [(d, BF16, "tile")], "kv_norm")[0]
    kvf = mm_nn_col(hkv.reshape(t, d), wkv, "kv_proj").reshape(bsz, seq, 2 * d)
    k_h, v_h = act_call(lambda kv_, g_: (_head_norm(kv_[:, :d], g_), kv_[:, d:]), [kvf, kg],
                        [(d, BF16, "tile"), (d, BF16, "tile")], "k_norm")
    h1 = act_call(_norm_mod, [x2, par(mix_norm_g[1]), sh_a1, sc_a1], [(d, BF16, "tile")], "mix1_norm")[0]
    q_raw = mm_nn_row(h1.reshape(t, d), wq, "q_proj").reshape(bsz, seq, d)
    q_h = act_call(lambda x_, g_: _head_norm(x_, g_) * ATT_SCALE, [q_raw, qg], [(d, BF16, "tile")], "q_norm")[0]
    o, att_tot = attention_fwd(q_h, k_h, v_h)
    mix1 = mm_nn_row(o.reshape(t, d), wo, "o_proj").reshape(bsz, seq, d)
    x3 = act_call(lambda x_, f_, g_: x_ + g_ * f_, [x2, mix1, g_a1], [(d, F32, "tile")], "att_res")[0]
    x4, mlp1_saved = mlp_fwd(x3, par(mlp_norm_g[1]), sh_m1, sc_m1, g_m1, w1[1], w2[1], "mlp1")

    def loss_fn(y_, t_):
        diff = y_ - t_
        part = jnp.sum(0.5 * jnp.mean(diff * diff, axis=-1, keepdims=True), axis=0, keepdims=True)
        return jnp.broadcast_to(part, (1, LANES)), diff * (1.0 / d)

    loss_part, dx4 = act_call(loss_fn, [x4, loss_target], [(LANES, F32, "all"), (d, F32, "tile")], "loss")
    loss = lax.psum(loss_part[0, 0, 0], ("x", "y", "c"))

    dx3, dw1_1, dw2_1, (dsh_m1, dsc_m1, dg_m1, dgn_mlp1) = mlp_bwd(
        dx4, x3, par(mlp_norm_g[1]), sc_m1, g_m1, w1[1], w2[1], mlp1_saved, "mlp1")
    dmix1, dg_a1 = act_call(lambda do_, f_, g_: (g_ * do_, _rowsum(do_ * f_)), [dx3, mix1, g_a1],
                            [(d, BF16, "tile"), (d, F32, "seq")], "att_dres")
    dmix1 = dmix1.reshape(t, d)
    do = mm_nt_row(dmix1, wo, "o_dproj").reshape(bsz, seq, d)
    dwo = mm_tn(o.reshape(t, d), dmix1, "a", N_DEV, "o_dw")
    dq, dk, dv = attention_bwd(q_h, k_h, v_h, att_tot, do)
    dq_raw, dqg = act_call(_head_norm_bwd, [q_raw, qg, dq], [(d, BF16, "tile"), (d, F32, "all")], "q_dnorm")
    dq_raw = dq_raw.reshape(t, d)
    dh1 = mm_nt_row(dq_raw, wq, "q_dproj").reshape(bsz, seq, d)
    dwq = mm_tn(h1.reshape(t, d), dq_raw, "a", N_DEV, "q_dw")
    dx2, dsh_a1, dsc_a1, dgn_mix1 = act_call(
        _norm_mod_bwd, [x2, par(mix_norm_g[1]), sc_a1, dh1, dx3],
        [(d, F32, "tile"), (d, F32, "seq"), (d, F32, "seq"), (d, F32, "all")], "mix1_dnorm")

    def kv_bwd_fn(kv_, g_, dk_, dv_):
        dk_raw, dg_ = _head_norm_bwd(kv_[:, :d], g_, dk_)
        return jnp.concatenate([dk_raw, dv_], axis=1), dg_

    dkvf, dkg = act_call(kv_bwd_fn, [kvf, kg, dk, dv], [(2 * d, BF16, "tile"), (d, F32, "all")], "k_dnorm")
    dkvf = dkvf.reshape(t, 2 * d)
    dhkv = mm_nt_col(dkvf, wkv, "kv_dproj").reshape(bsz, seq, d)
    dwkv = mm_tn(hkv.reshape(t, d), dkvf, "c", N_DEV, "kv_dw")
    dx2, dkv_sh, dkv_sc, dgn_kv = act_call(
        _norm_mod_bwd, [x2, par(kv_norm_g), kv_sc, dhkv, dx2],
        [(d, F32, "tile"), (d, F32, "seq"), (d, F32, "seq"), (d, F32, "all")], "kv_dnorm")

    dx1, dw1_0, dw2_0, (dsh_m0, dsc_m0, dg_m0, dgn_mlp0) = mlp_bwd(
        dx2, x1, par(mlp_norm_g[0]), sc_m, g_m, w1[0], w2[0], mlp0_saved, "mlp0")

    def glu_bwd_fn(do_, z_, g_):
        val, sig = z_[:, :d], jax.nn.sigmoid(z_[:, d:])
        dmix = g_ * do_
        dz = jnp.concatenate([dmix * sig, dmix * val * sig * (1.0 - sig)], axis=1)
        return dz, _rowsum(do_ * (val * sig))

    dz, dg_a0 = act_call(glu_bwd_fn, [dx1, z, g_a], [(2 * d, BF16, "tile"), (d, F32, "seq")], "glu_dres")
    dz = dz.reshape(t, 2 * d)
    dy = mm_nt_col(dz, w_glu, "glu_dup", (F32,), lambda acc, y_: (acc * _gelu_grad(y_),),
                   (y.reshape(t, d),)).reshape(bsz, seq, d)
    dwglu = mm_tn(ge.reshape(t, d), dz, "c", N_DEV, "glu_dw")
    du, dbbd_re, dbbd_im, dcbd_re, dcbd_im, dab_re, dab_im, dd_skip = s5_bwd(
        dy, h0, st_re, st_im, bbd_re, bbd_im, cbd_re, cbd_im, abr, abi, d_skip)
    dx0, dsh_a0, dsc_a0, dgn_mix0 = act_call(
        _norm_mod_bwd, [x, par(mix_norm_g[0]), sc_a, du, dx1],
        [(d, F32, "tile"), (d, F32, "seq"), (d, F32, "seq"), (d, F32, "all")], "mix0_dnorm")

    from_bbd = lambda m: jnp.swapaxes(_block_diag_part(m, S5_GROUP, S5_STATE), 2, 3).reshape(gp, S5_GROUP)
    d_c = lambda m: jnp.swapaxes(_block_diag_part(m, S5_STATE, S5_GROUP), 2, 3).reshape(
        1, n_groups, S5_GROUP, S5_STATE)
    d_lam_re, d_lam_im, d_log_dt, d_b_re, d_b_im = s5_prep_bwd(
        lam_re, lam_im, log_dt, b_re, b_im, dab_re.reshape(gp, 1), dab_im.reshape(gp, 1),
        from_bbd(dbbd_re), from_bbd(dbbd_im))

    small = all_reduce_small([
        jnp.stack([dgn_mix0.reshape(d), dgn_mix1.reshape(d)]),
        jnp.stack([dgn_mlp0.reshape(d), dgn_mlp1.reshape(d)]),
        d_lam_re.reshape(1, n_groups, S5_STATE), d_lam_im.reshape(1, n_groups, S5_STATE),
        d_log_dt.reshape(1, n_groups, S5_STATE).sum(axis=-1),
        d_b_re.reshape(s5_b_re.shape), d_b_im.reshape(s5_b_im.shape),
        d_c(dcbd_re), d_c(dcbd_im),
        dd_skip.reshape(1, d),
        dgn_kv.reshape(d),
        dkg.reshape(d // HEAD_DIM, HEAD_DIM).sum(axis=0),
        dqg.reshape(d // HEAD_DIM, HEAD_DIM).sum(axis=0)[None, :],
    ], "small_grads")
    (g_mix_norm, g_mlp_norm, g_a_re, g_a_im, g_log_dt, g_b_re, g_b_im, g_c_re, g_c_im,
     g_skip_full, g_kv_norm, g_k_norm, g_q_norm) = small
    g_s5_d = lax.dynamic_slice_in_dim(g_skip_full, dev * (d // N_DEV), d // N_DEV, axis=1)

    dm_mine = jnp.concatenate([
        dsh_a0, dsc_a0, dg_a0, dsh_m0, dsc_m0, dg_m0,
        dsh_a1, dsc_a1, dg_a1, dsh_m1, dsc_m1, dg_m1, dkv_sh, dkv_sc], axis=2).reshape(bsz, 2 * e_ada + e_kv)
    dm_all = all_gather(dm_mine, "gather_dmod").reshape(N_DEV * bsz, 2 * e_ada + e_kv)
    dm_cols = jnp.concatenate([
        lax.dynamic_slice_in_dim(dm_all, dev * n_ada, n_ada, axis=1),
        lax.dynamic_slice_in_dim(dm_all, e_ada + dev * n_ada, n_ada, axis=1),
        lax.dynamic_slice_in_dim(dm_all, 2 * e_ada + dev * n_kv, n_kv, axis=1)], axis=1)
    dw_cols, db_all = ada_bwd(c_all, dm_cols, dm_all)
    g_ada_w = jnp.stack([dw_cols[:, :n_ada], dw_cols[:, n_ada:2 * n_ada]])
    g_kv_ada_w = dw_cols[:, 2 * n_ada:]
    g_ada_b = db_all[0, :2 * e_ada].reshape(2, e_ada)
    g_kv_ada_b = db_all[0, 2 * e_ada:]

    g_w1 = reduce_scatter(jnp.stack([dw1_0, dw1_1], axis=1), "rs_w1")
    g_w2 = reduce_scatter(jnp.stack([dw2_0, dw2_1], axis=1), "rs_w2")
    g_glu = reduce_scatter(dwglu, "rs_glu")[None]
    g_wkv = reduce_scatter(dwkv, "rs_wkv")
    g_wq = reduce_scatter(dwq, "rs_wq")[None]
    g_wo = reduce_scatter(dwo, "rs_wo")[None]

    weights = [ada_w, ada_b, mix_norm_g, mlp_norm_g, mlp_w1, mlp_w2, s5_a_re, s5_a_im, s5_log_dt, s5_b_re,
               s5_b_im, s5_c_re, s5_c_im, s5_d, s5_w_glu, kv_ada_w, kv_ada_b, kv_norm_g, w_kv, k_norm_g,
               sb_w_q, q_norm_g, sb_w_o]
    grads = [g_ada_w, g_ada_b, g_mix_norm, g_mlp_norm, g_w1, g_w2, g_a_re, g_a_im, g_log_dt, g_b_re,
             g_b_im, g_c_re, g_c_im, g_s5_d, g_glu, g_kv_ada_w, g_kv_ada_b, g_kv_norm, g_wkv, g_k_norm,
             g_wq, g_q_norm, g_wo]
    ms = [m_ada_w, m_ada_b, m_mix_norm_g, m_mlp_norm_g, m_mlp_w1, m_mlp_w2, m_s5_a_re, m_s5_a_im, m_s5_log_dt,
          m_s5_b_re, m_s5_b_im, m_s5_c_re, m_s5_c_im, m_s5_d, m_s5_w_glu, m_kv_ada_w, m_kv_ada_b, m_kv_norm_g,
          m_w_kv, m_k_norm_g, m_sb_w_q, m_q_norm_g, m_sb_w_o]
    vs = [v_ada_w, v_ada_b, v_mix_norm_g, v_mlp_norm_g, v_mlp_w1, v_mlp_w2, v_s5_a_re, v_s5_a_im, v_s5_log_dt,
          v_s5_b_re, v_s5_b_im, v_s5_c_re, v_s5_c_im, v_s5_d, v_s5_w_glu, v_kv_ada_w, v_kv_ada_b, v_kv_norm_g,
          v_w_kv, v_k_norm_g, v_sb_w_q, v_q_norm_g, v_sb_w_o]
    grads = [g.reshape(w.shape) for g, w in zip(grads, weights)]
    deltas, new_ms, new_vs = [], [], []
    for i, (w, g, m, v) in enumerate(zip(weights, grads, ms, vs)):
        dl, nm, nv = adamw(w, g, m, v, f"adamw_{i}")
        deltas.append(dl)
        new_ms.append(nm)
        new_vs.append(nv)
    return (loss, dx0, *grads, *deltas, *new_ms, *new_vs)
```

```python
import functools
import math

import jax
import jax.numpy as jnp
from jax import lax
from jax.experimental import pallas as pl
from jax.experimental.pallas import tpu as pltpu

F32 = jnp.float32
BF16 = jnp.bfloat16

N_DEV = 8
N_CHIPS = 4
MESH = pl.DeviceIdType.MESH
ANY = pl.BlockSpec(memory_space=pl.ANY)

LANES = 128
VMEM_LIMIT_BYTES = 48 * 2 ** 20
TILE_BUDGET_BYTES = 4 * 2 ** 20

S5_GROUP = 16
S5_STATE = 64
S5_BLOCK_GROUPS = 16
HEAD_DIM = 64
ATT_BLOCK = 128
EPS = 1e-6

ADAM_LR = 0.001
ADAM_B1 = 0.9
ADAM_B2 = 0.999
ADAM_EPS = 1e-08
ADAM_WD = 0.01
ADAM_STEP = 10


def _cparams(*sem):
    return pltpu.CompilerParams(dimension_semantics=sem, vmem_limit_bytes=VMEM_LIMIT_BYTES)


def _divisor_tile(n, limit, mult):
    best = None
    for t in range(mult, min(n, limit) + 1, mult):
        if n % t == 0:
            best = t
    return best if best is not None else n


def _tile_m(m):
    return _divisor_tile(m, 1024 if m >= 2048 else 256, 16)


def all_gather(x, name):
    def body(x_ref, out_ref, send_sems, recv_sems, local_sem):
        ax, ay, ac = lax.axis_index("x"), lax.axis_index("y"), lax.axis_index("c")
        me, sibling = (ax, ay, ac), (ax, ay, 1 - ac)
        chips = [(1 - ax, ay), (ax, 1 - ay), (1 - ax, 1 - ay)]

        def slot(px, py, pc):
            return out_ref.at[4 * px + 2 * py + pc]

        def copy(k, block, to, src=None):
            return pltpu.make_async_remote_copy(
                src_ref=slot(*block) if src is None else src, dst_ref=slot(*block),
                send_sem=send_sems.at[k], recv_sem=recv_sems.at[k], device_id=to, device_id_type=MESH)

        mine = pltpu.make_async_copy(x_ref, slot(*me), local_sem)
        mine.start()
        first = [copy(0, me, sibling, src=x_ref)]
        first += [copy(1 + j, me, (*chip, ac), src=x_ref) for j, chip in enumerate(chips)]
        for cp in first:
            cp.start()
        passed = [copy(4 + j, (*chip, ac), sibling) for j, chip in enumerate(chips)]
        for j, chip in enumerate(chips):
            copy(1 + j, (*chip, ac), me).wait_recv()
            passed[j].start()
        copy(0, sibling, me).wait_recv()
        for j, chip in enumerate(chips):
            copy(4 + j, (*chip, 1 - ac), me).wait_recv()
        for cp in first + passed:
            cp.wait_send()
        mine.wait()

    return pl.pallas_call(
        body, name=name,
        out_shape=jax.ShapeDtypeStruct((N_DEV,) + x.shape, x.dtype),
        in_specs=[ANY], out_specs=ANY,
        scratch_shapes=[pltpu.SemaphoreType.DMA((7,)), pltpu.SemaphoreType.DMA((7,)), pltpu.SemaphoreType.DMA],
    )(x)


HBM = pl.BlockSpec(memory_space=pltpu.HBM)
SEM = pl.BlockSpec(memory_space=pltpu.SEMAPHORE)
N_PEERS = N_DEV - 1


def _peers():
    ax, ay, ac = lax.axis_index("x"), lax.axis_index("y"), lax.axis_index("c")
    flip = lambda v, bit: 1 - v if bit else v
    return [(flip(ax, k & 4), flip(ay, k & 2), flip(ac, k & 1)) for k in range(1, N_DEV)]


def _dev_index(pos):
    return 4 * pos[0] + 2 * pos[1] + pos[2]


def exchange_start(items, name, after=None):
    n = len(items)
    srcs = [a for a, _ in items]
    blocks = [a.shape[1:] if scatter else a.shape for a, scatter in items]
    extra = [] if after is None else [after]

    def body(*refs):
        src_refs, land_refs = refs[:n], refs[n:2 * n]
        outs = refs[2 * n + len(extra):]
        send_sems, recv_sems = outs[:n], outs[n:2 * n]
        token = outs[-1]
        me = _dev_index((lax.axis_index("x"), lax.axis_index("y"), lax.axis_index("c")))
        for w, (_, scatter) in enumerate(items):
            for k, peer in enumerate(_peers()):
                src = src_refs[w].at[_dev_index(peer)] if scatter else src_refs[w]
                pltpu.make_async_remote_copy(
                    src_ref=src, dst_ref=land_refs[w].at[me], send_sem=send_sems[w].at[k],
                    recv_sem=recv_sems[w].at[k], device_id=peer, device_id_type=MESH).start()
        token[...] = jnp.zeros_like(token)

    lands = [lax.empty((N_DEV,) + blk, a.dtype) for a, blk in zip(srcs, blocks)]
    res = pl.pallas_call(
        body, name=name,
        out_shape=([pltpu.SemaphoreType.DMA((N_PEERS,))] * (2 * n)
                   + [pltpu.HBM(a.shape, a.dtype) for a in srcs] + [pltpu.HBM(l.shape, l.dtype) for l in lands]
                   + [jax.ShapeDtypeStruct((8, LANES), F32)]),
        in_specs=[HBM] * (2 * n) + [ANY] * len(extra),
        out_specs=[SEM] * (2 * n) + [HBM] * (2 * n) + [pl.BlockSpec(memory_space=pltpu.VMEM)],
        input_output_aliases={i: 2 * n + i for i in range(2 * n)},
        compiler_params=pltpu.CompilerParams(has_side_effects=pltpu.SideEffectType.DATAFLOW_SIDE_EFFECTING),
    )(*[pltpu.with_memory_space_constraint(a, pltpu.HBM) for a in srcs + lands], *extra)
    handles = [(res[w], res[n + w], res[2 * n + w], res[3 * n + w], scatter) for w, (_, scatter) in enumerate(items)]
    return handles, res[-1]


def exchange_wait(handle, after, name):
    send_sem, recv_sem, src, land, scatter = handle

    def body(src_ref, land_ref, send_ref, recv_ref, after_ref, src_out, land_out):
        for k, peer in enumerate(_peers()):
            slot = _dev_index(peer)
            copy = pltpu.make_async_remote_copy(
                src_ref=src_ref.at[slot] if scatter else src_ref, dst_ref=land_ref.at[slot],
                send_sem=send_ref.at[k], recv_sem=recv_ref.at[k], device_id=peer, device_id_type=MESH)
            copy.wait_send()
            copy.wait_recv()

    src, landed = pl.pallas_call(
        body, name=name,
        out_shape=(pltpu.HBM(src.shape, src.dtype), pltpu.HBM(land.shape, land.dtype)),
        in_specs=[HBM, HBM, SEM, SEM, ANY], out_specs=(HBM, HBM), input_output_aliases={0: 0, 1: 1},
        compiler_params=pltpu.CompilerParams(has_side_effects=pltpu.SideEffectType.DATAFLOW_SIDE_EFFECTING),
    )(src, land, send_sem, recv_sem, after)
    dev = _dev_index((lax.axis_index("x"), lax.axis_index("y"), lax.axis_index("c")))
    own = lax.dynamic_index_in_dim(src, dev, axis=0, keepdims=True) if scatter else src[None]
    return lax.dynamic_update_slice_in_dim(landed, own, dev, axis=0)


def rows_call(fn, ins, outs, name):
    rows = ins[0].shape[1]
    per_row = sum(a.shape[0] * a.shape[2] * a.dtype.itemsize for a in ins)
    per_row += sum(l * c * jnp.dtype(dt).itemsize for l, c, dt in outs)
    tr = _divisor_tile(rows, max(16, TILE_BUDGET_BYTES // per_row), 16)
    n_in = len(ins)

    def body(*refs):
        vals = fn(*[r[...] for r in refs[:n_in]])
        if not isinstance(vals, (tuple, list)):
            vals = (vals,)
        for r, v in zip(refs[n_in:], vals):
            r[...] = v.astype(r.dtype)

    def spec(l, c):
        return pl.BlockSpec((l, tr, c), lambda i: (0, i, 0))

    res = pl.pallas_call(
        body, name=name, grid=(rows // tr,),
        in_specs=[spec(a.shape[0], a.shape[2]) for a in ins],
        out_specs=[spec(l, c) for l, c, _ in outs],
        out_shape=[jax.ShapeDtypeStruct((l, rows, c), dt) for l, c, dt in outs],
        compiler_params=_cparams("arbitrary"),
    )(*ins)
    return res


def _as_rows(a, lead=0):
    shape = a.shape
    l = int(math.prod(shape[:lead])) if lead else 1
    rest = shape[lead:]
    c = rest[-1] if rest else 1
    r = int(math.prod(rest[:-1])) if len(rest) > 1 else 1
    return a.reshape(l, r, c)


def act_call(fn, ins, outs, name):
    bsz, seq = ins[0].shape[0], ins[0].shape[1]
    per_row = sum(a.shape[2] * a.dtype.itemsize for a in ins if a.shape[1] == seq)
    per_row += sum(c * jnp.dtype(dt).itemsize for c, dt, kind in outs if kind == "tile")
    ts = _divisor_tile(seq, max(16, TILE_BUDGET_BYTES // per_row), 16)
    n_in = len(ins)

    def in_spec(a):
        c = a.shape[2]
        if a.shape[1] == seq:
            return pl.BlockSpec((None, ts, c), lambda b, s: (b, s, 0))
        if a.shape[0] == bsz:
            return pl.BlockSpec((None, 1, c), lambda b, s: (b, 0, 0))
        return pl.BlockSpec((None, 1, c), lambda b, s: (0, 0, 0))

    def out_spec(c, kind):
        if kind == "tile":
            return pl.BlockSpec((None, ts, c), lambda b, s: (b, s, 0))
        if kind == "seq":
            return pl.BlockSpec((None, 1, c), lambda b, s: (b, 0, 0))
        return pl.BlockSpec((None, 1, c), lambda b, s: (0, 0, 0))

    def out_shape(c, dt, kind):
        if kind == "tile":
            return jax.ShapeDtypeStruct((bsz, seq, c), dt)
        return jax.ShapeDtypeStruct((bsz if kind == "seq" else 1, 1, c), dt)

    def accumulate(ref, v, first):
        @pl.when(first)
        def _():
            ref[...] = jnp.zeros_like(ref)

        ref[...] += v.astype(ref.dtype)

    def body(*refs):
        b, s = pl.program_id(0), pl.program_id(1)
        vals = fn(*[r[...] for r in refs[:n_in]])
        if not isinstance(vals, (tuple, list)):
            vals = (vals,)
        for ref, v, (_, _, kind) in zip(refs[n_in:], vals, outs):
            if kind == "tile":
                ref[...] = v.astype(ref.dtype)
            elif kind == "seq":
                accumulate(ref, v, s == 0)
            else:
                accumulate(ref, v, jnp.logical_and(b == 0, s == 0))

    return pl.pallas_call(
        body, name=name, grid=(bsz, seq // ts),
        in_specs=[in_spec(a) for a in ins],
        out_specs=[out_spec(c, kind) for c, _, kind in outs],
        out_shape=[out_shape(*o) for o in outs],
        compiler_params=_cparams("arbitrary", "arbitrary"),
    )(*ins)


def _mm(name, grid, a, a_spec, b, b_spec, dims, out_shape, out_spec, out_dtypes, acc_steps,
        epi=None, extras=(), extra_spec=None):
    n_ex, n_out = len(extras), len(out_dtypes)
    tile = tuple(d for d in out_spec.block_shape if d is not None)

    def body(*refs):
        a_ref, b_ref = refs[0], refs[1]
        ex_refs = refs[2:2 + n_ex]
        o_refs = refs[2 + n_ex:2 + n_ex + n_out]
        p = lax.dot_general(a_ref[...].astype(BF16), b_ref[...].astype(BF16), (dims, ((), ())),
                            preferred_element_type=F32)

        def finish(acc):
            vals = epi(acc, *[r[...] for r in ex_refs]) if epi is not None else (acc,) * n_out
            for r, v in zip(o_refs, vals):
                r[...] = v.astype(r.dtype)

        if not acc_steps:
            finish(p)
        else:
            acc_ref = refs[-1]
            s = pl.program_id(1)

            @pl.when(s == 0)
            def _():
                acc_ref[...] = p

            @pl.when(s > 0)
            def _():
                acc_ref[...] += p

            @pl.when(s == acc_steps - 1)
            def _():
                finish(acc_ref[...])

    res = pl.pallas_call(
        body, name=name, grid=grid,
        in_specs=[a_spec, b_spec] + [extra_spec] * n_ex,
        out_specs=[out_spec] * n_out,
        out_shape=[jax.ShapeDtypeStruct(out_shape, dt) for dt in out_dtypes],
        scratch_shapes=[pltpu.VMEM(tile, F32)] if acc_steps else [],
        compiler_params=_cparams("arbitrary", "arbitrary"),
    )(a, b, *extras)
    return res if n_out > 1 else res[0]


def mm_nn_col(a, w, name, out_dtypes=(F32,), epi=None):
    m, k = a.shape
    ns, _, nb = w.shape
    tm = _tile_m(m)
    return _mm(name, (m // tm, ns), a, pl.BlockSpec((tm, k), lambda i, j: (i, 0)),
               w, pl.BlockSpec((None, k, nb), lambda i, j: (j, 0, 0)), ((1,), (0,)),
               (m, ns * nb), pl.BlockSpec((tm, nb), lambda i, j: (i, j)), out_dtypes, 0, epi)


def mm_nn_row(a, w, name, out_dtypes=(F32,)):
    m = a.shape[0]
    ns, kb, n = w.shape
    tm = _tile_m(m)
    return _mm(name, (m // tm, ns), a, pl.BlockSpec((tm, kb), lambda i, s: (i, s)),
               w, pl.BlockSpec((None, kb, n), lambda i, s: (s, 0, 0)), ((1,), (0,)),
               (m, n), pl.BlockSpec((tm, n), lambda i, s: (i, 0)), out_dtypes, ns)


def mm_nt_col(dc, w, name, out_dtypes=(F32,), epi=None, extras=()):
    m = dc.shape[0]
    ns, k, nb = w.shape
    tm = _tile_m(m)
    spec = pl.BlockSpec((tm, k), lambda i, s: (i, 0))
    return _mm(name, (m // tm, ns), dc, pl.BlockSpec((tm, nb), lambda i, s: (i, s)),
               w, pl.BlockSpec((None, k, nb), lambda i, s: (s, 0, 0)), ((1,), (1,)),
               (m, k), spec, out_dtypes, ns, epi, extras, spec)


def mm_nt_row(dc, w, name, out_dtypes=(F32,), epi=None, extras=()):
    m, n = dc.shape
    ns, kb, _ = w.shape
    tm = _tile_m(m)
    spec = pl.BlockSpec((tm, kb), lambda i, s: (i, s))
    return _mm(name, (m // tm, ns), dc, pl.BlockSpec((tm, n), lambda i, s: (i, 0)),
               w, pl.BlockSpec((None, kb, n), lambda i, s: (s, 0, 0)), ((1,), (1,)),
               (m, ns * kb), spec, out_dtypes, 0, epi, extras, spec)


def mm_tn(a, c, slab, ns, name, out_dtype=BF16):
    m, ka_all = a.shape
    nc_all = c.shape[1]
    ka = ka_all // ns if slab == "a" else ka_all
    nc = nc_all // ns if slab == "c" else nc_all
    tt = _divisor_tile(m, 256, 16)
    steps = m // tt

    def body(a_ref, c_ref, o_ref, acc_ref):
        t = pl.program_id(0)

        @pl.when(t == 0)
        def _():
            acc_ref[...] = jnp.zeros_like(acc_ref)

        for s in range(ns):
            a_s = a_ref[:, s * ka:(s + 1) * ka] if slab == "a" else a_ref[...]
            c_s = c_ref[:, s * nc:(s + 1) * nc] if slab == "c" else c_ref[...]
            acc_ref[s] += lax.dot_general(a_s.astype(BF16), c_s.astype(BF16), (((0,), (0,)), ((), ())),
                                          preferred_element_type=F32)

        @pl.when(t == steps - 1)
        def _():
            o_ref[...] = acc_ref[...].astype(o_ref.dtype)

    return pl.pallas_call(
        body, name=name, grid=(steps,),
        in_specs=[pl.BlockSpec((tt, ka_all), lambda t: (t, 0)), pl.BlockSpec((tt, nc_all), lambda t: (t, 0))],
        out_specs=pl.BlockSpec((ns, ka, nc), lambda t: (0, 0, 0)),
        out_shape=jax.ShapeDtypeStruct((ns, ka, nc), out_dtype),
        scratch_shapes=[pltpu.VMEM((ns, ka, nc), F32)],
        compiler_params=_cparams("arbitrary"),
    )(a, c)


def slab_sum(landed, name):
    shape = landed.shape[1:]
    total = rows_call(lambda g: jnp.sum(g.astype(F32), axis=0, keepdims=True),
                      [_as_rows(landed, 1)], [(1, shape[-1], F32)], name)[0]
    return total.reshape(shape)


def all_reduce_small(leaves, name):
    sizes = [int(a.size) for a in leaves]
    flat = jnp.concatenate([a.reshape(-1) for a in leaves])
    total = int(flat.size)
    padded = -(-total // (16 * LANES)) * (16 * LANES)
    flat = jnp.pad(flat, (0, padded - total)).reshape(padded // LANES, LANES)
    gathered = all_gather(flat, name + "_gather")
    summed = rows_call(lambda g: jnp.sum(g, axis=0, keepdims=True), [gathered],
                       [(1, LANES, F32)], name + "_sum")[0].reshape(-1)
    out, at = [], 0
    for a, n in zip(leaves, sizes):
        out.append(summed[at:at + n].reshape(a.shape))
        at += n
    return out


def adamw(w, g, m, v, name):
    c = w.shape[-1] if w.ndim else 1

    def fn(w_, g_, m_, v_):
        nm = ADAM_B1 * m_ + (1.0 - ADAM_B1) * g_
        nv = ADAM_B2 * v_ + (1.0 - ADAM_B2) * (g_ * g_)
        m_hat = nm / (1.0 - ADAM_B1 ** ADAM_STEP)
        v_hat = nv / (1.0 - ADAM_B2 ** ADAM_STEP)
        delta = -ADAM_LR * (m_hat / (jnp.sqrt(v_hat) + ADAM_EPS) + ADAM_WD * w_)
        return delta, nm, nv

    res = rows_call(fn, [_as_rows(t) for t in (w, g.astype(F32), m, v)], [(1, c, F32)] * 3, name)
    return tuple(r.reshape(w.shape) for r in res)


def _rowsum(v):
    return jnp.sum(v, axis=0, keepdims=True)


def _norm_mod(x, g, sh, sc):
    n = x * lax.rsqrt(jnp.mean(x * x, axis=-1, keepdims=True) + EPS)
    return (n * g) * (1.0 + sc) + sh


def _norm_mod_bwd(x, g, sc, dh, dres):
    r = lax.rsqrt(jnp.mean(x * x, axis=-1, keepdims=True) + EPS)
    n = x * r
    dy = dh * (1.0 + sc)
    dn = dy * g
    dx = r * (dn - n * jnp.mean(dn * n, axis=-1, keepdims=True))
    return dres + dx, _rowsum(dh), _rowsum(dh * (n * g)), _rowsum(dy * n)


def _head_mean(v):
    low = lax.broadcasted_iota(jnp.int32, (1, LANES), 1) < HEAD_DIM
    parts = []
    for p in range(v.shape[1] // LANES):
        blk = v[:, p * LANES:(p + 1) * LANES]
        s0 = jnp.sum(jnp.where(low, blk, 0.0), axis=-1, keepdims=True)
        s1 = jnp.sum(jnp.where(low, 0.0, blk), axis=-1, keepdims=True)
        parts.append(jnp.where(low, s0, s1))
    return jnp.concatenate(parts, axis=1) * (1.0 / HEAD_DIM)


def _head_norm(x, g):
    return x * lax.rsqrt(_head_mean(x * x) + EPS) * g


def _head_norm_bwd(x, g, dy):
    r = lax.rsqrt(_head_mean(x * x) + EPS)
    n = x * r
    dn = dy * g
    return r * (dn - n * _head_mean(dn * n)), _rowsum(dy * n)


GELU_C = math.sqrt(2.0 / math.pi)
GELU_A = 0.044715


def _gelu_grad(y):
    t = jnp.tanh(GELU_C * (y + GELU_A * y * y * y))
    return 0.5 * (1.0 + t) + 0.5 * y * (1.0 - t * t) * GELU_C * (1.0 + 3.0 * GELU_A * y * y)


def ada_fwd(c_all, w_cols, b_cols):
    def body(c_ref, w_ref, b_ref, o_ref):
        c = c_ref[...]
        s = (c * jax.nn.sigmoid(c)).astype(BF16)
        o_ref[...] = jnp.dot(s, w_ref[...].astype(BF16), preferred_element_type=F32) + b_ref[...]

    return pl.pallas_call(
        body, name="ada_fwd", out_shape=jax.ShapeDtypeStruct((c_all.shape[0], w_cols.shape[1]), F32),
        compiler_params=pltpu.CompilerParams(vmem_limit_bytes=VMEM_LIMIT_BYTES),
    )(c_all, w_cols, b_cols)


def ada_bwd(c_all, dm_cols, dm_all):
    def body(c_ref, d_ref, all_ref, dw_ref, db_ref):
        c = c_ref[...]
        s = (c * jax.nn.sigmoid(c)).astype(BF16)
        dw_ref[...] = lax.dot_general(s, d_ref[...].astype(BF16), (((0,), (0,)), ((), ())),
                                      preferred_element_type=F32)
        db_ref[...] = jnp.sum(all_ref[...], axis=0, keepdims=True)

    return pl.pallas_call(
        body, name="ada_bwd",
        out_shape=[jax.ShapeDtypeStruct((c_all.shape[1], dm_cols.shape[1]), F32),
                   jax.ShapeDtypeStruct((1, dm_all.shape[1]), F32)],
        compiler_params=pltpu.CompilerParams(vmem_limit_bytes=VMEM_LIMIT_BYTES),
    )(c_all, dm_cols, dm_all)


def _s5_discretise(lam_re, lam_im, log_dt, b_re, b_im):
    dt = jnp.exp(log_dt)
    mag = jnp.exp(lam_re * dt)
    ab_re = mag * jnp.cos(lam_im * dt)
    ab_im = mag * jnp.sin(lam_im * dt)
    den = lam_re * lam_re + lam_im * lam_im
    nr = ab_re - 1.0
    ni = ab_im
    f_re = (nr * lam_re + ni * lam_im) / den
    f_im = (ni * lam_re - nr * lam_im) / den
    bb_re = f_re * b_re - f_im * b_im
    bb_im = f_re * b_im + f_im * b_re
    return ab_re, ab_im, bb_re, bb_im


def s5_prep(lam_re, lam_im, log_dt, b_re, b_im):
    gp, h = b_re.shape

    def body(lr, li, ld, br, bi, o_ar, o_ai, o_br, o_bi):
        res = _s5_discretise(lr[...], li[...], ld[...], br[...], bi[...])
        for r, v in zip((o_ar, o_ai, o_br, o_bi), res):
            r[...] = v

    col, mat = jax.ShapeDtypeStruct((gp, 1), F32), jax.ShapeDtypeStruct((gp, h), F32)
    return pl.pallas_call(body, name="s5_prep", out_shape=[col, col, mat, mat])(lam_re, lam_im, log_dt, b_re, b_im)


def s5_prep_bwd(lam_re, lam_im, log_dt, b_re, b_im, d_ab_re, d_ab_im, d_bb_re, d_bb_im):
    gp, h = b_re.shape

    def body(lr, li, ld, br, bi, g_ar, g_ai, g_br, g_bi, o_lr, o_li, o_ld, o_br, o_bi):
        _, vjp = jax.vjp(_s5_discretise, lr[...], li[...], ld[...], br[...], bi[...])
        res = vjp((g_ar[...], g_ai[...], g_br[...], g_bi[...]))
        for r, v in zip((o_lr, o_li, o_ld, o_br, o_bi), res):
            r[...] = v

    col, mat = jax.ShapeDtypeStruct((gp, 1), F32), jax.ShapeDtypeStruct((gp, h), F32)
    return pl.pallas_call(body, name="s5_prep_bwd", out_shape=[col, col, col, mat, mat])(
        lam_re, lam_im, log_dt, b_re, b_im, d_ab_re, d_ab_im, d_bb_re, d_bb_im)


def _s5_chunk(seq):
    return _divisor_tile(seq, 256, 16)


def s5_fwd(u, bbd_re, bbd_im, cbd_re, cbd_im, ab_re, ab_im, dskip):
    bsz, seq, d = u.shape
    nb, cb, ns = bbd_re.shape
    lc = _s5_chunk(seq)

    def body(u_ref, bre_ref, bim_ref, cre_ref, cim_ref, ar_ref, ai_ref, d_ref, y_ref, sre_ref, sim_ref,
             carry_re, carry_im):
        t = pl.program_id(2)

        @pl.when(t == 0)
        def _():
            carry_re[...] = jnp.zeros_like(carry_re)
            carry_im[...] = jnp.zeros_like(carry_im)

        uf = u_ref[...]
        ub = uf.astype(BF16)
        sre_ref[...] = jnp.dot(ub, bre_ref[...], preferred_element_type=F32)
        sim_ref[...] = jnp.dot(ub, bim_ref[...], preferred_element_type=F32)
        ar, ai = ar_ref[...], ai_ref[...]

        def step(i, c):
            cr, ci = c
            row = pl.ds(i, 1)
            nr = ar * cr - ai * ci + sre_ref[row, :]
            ni = ar * ci + ai * cr + sim_ref[row, :]
            sre_ref[row, :] = nr
            sim_ref[row, :] = ni
            return nr, ni

        cr, ci = lax.fori_loop(0, lc, step, (carry_re[...], carry_im[...]), unroll=8)
        carry_re[...] = cr
        carry_im[...] = ci
        y = jnp.dot(sre_ref[...].astype(BF16), cre_ref[...], preferred_element_type=F32)
        y -= jnp.dot(sim_ref[...].astype(BF16), cim_ref[...], preferred_element_type=F32)
        y_ref[...] = y + d_ref[...] * uf

    chan = pl.BlockSpec((None, lc, cb), lambda n, b, t: (b, t, n))
    state = pl.BlockSpec((None, lc, ns), lambda n, b, t: (b, t, n))
    par = lambda r, c: pl.BlockSpec((None, r, c), lambda n, b, t: (n, 0, 0))
    return pl.pallas_call(
        body, name="s5_fwd", grid=(nb, bsz, seq // lc),
        in_specs=[chan, par(cb, ns), par(cb, ns), par(ns, cb), par(ns, cb), par(1, ns), par(1, ns),
                  pl.BlockSpec((None, 1, cb), lambda n, b, t: (0, 0, n))],
        out_specs=[chan, state, state],
        out_shape=[jax.ShapeDtypeStruct((bsz, seq, d), F32),
                   jax.ShapeDtypeStruct((bsz, seq, nb * ns), F32),
                   jax.ShapeDtypeStruct((bsz, seq, nb * ns), F32)],
        scratch_shapes=[pltpu.VMEM((1, ns), F32), pltpu.VMEM((1, ns), F32)],
        compiler_params=_cparams("arbitrary", "arbitrary", "arbitrary"),
    )(u, bbd_re, bbd_im, cbd_re, cbd_im, ab_re, ab_im, dskip)


def s5_bwd(dy, u, st_re, st_im, bbd_re, bbd_im, cbd_re, cbd_im, ab_re, ab_im, dskip):
    bsz, seq, d = u.shape
    nb, cb, ns = bbd_re.shape
    lc = _s5_chunk(seq)
    nc = seq // lc

    def body(dy_ref, u_ref, sre_ref, sim_ref, bre_ref, bim_ref, cre_ref, cim_ref, ar_ref, ai_ref, d_ref,
             du_ref, dbre_ref, dbim_ref, dcre_ref, dcim_ref, dar_ref, dai_ref, dd_ref,
             g_re, g_im, gs_re, gs_im, carry_re, carry_im):
        b, t = pl.program_id(1), pl.program_id(2)

        @pl.when(jnp.logical_and(b == 0, t == 0))
        def _():
            for r in (dbre_ref, dbim_ref, dcre_ref, dcim_ref, dar_ref, dai_ref, dd_ref):
                r[...] = jnp.zeros_like(r)

        @pl.when(t == 0)
        def _():
            carry_re[...] = jnp.zeros_like(carry_re)
            carry_im[...] = jnp.zeros_like(carry_im)

        dyf, uf = dy_ref[...], u_ref[...]
        dyb, ub = dyf.astype(BF16), uf.astype(BF16)
        nt = (((1,), (1,)), ((), ()))
        tn = (((0,), (0,)), ((), ()))
        g_re[...] = lax.dot_general(dyb, cre_ref[...], nt, preferred_element_type=F32)
        g_im[...] = -lax.dot_general(dyb, cim_ref[...], nt, preferred_element_type=F32)
        ar, ai = ar_ref[...], ai_ref[...]

        def step(k, c):
            cr, ci = c
            row = pl.ds(lc - 1 - k, 1)
            gs_re[row, :] = cr
            gs_im[row, :] = ci
            nr = ar * cr + ai * ci + g_re[row, :]
            ni = ar * ci - ai * cr + g_im[row, :]
            g_re[row, :] = nr
            g_im[row, :] = ni
            return nr, ni

        cr, ci = lax.fori_loop(0, lc, step, (carry_re[...], carry_im[...]), unroll=8)
        carry_re[...] = cr
        carry_im[...] = ci

        sr, si = sre_ref[...], sim_ref[...]
        hr, hi = gs_re[...], gs_im[...]
        dar_ref[...] += _rowsum(hr * sr + hi * si)
        dai_ref[...] += _rowsum(hi * sr - hr * si)
        gr, gi = g_re[...].astype(BF16), g_im[...].astype(BF16)
        du = lax.dot_general(gr, bre_ref[...], nt, preferred_element_type=F32)
        du += lax.dot_general(gi, bim_ref[...], nt, preferred_element_type=F32)
        du_ref[...] = du + d_ref[...] * dyf
        dbre_ref[...] += lax.dot_general(ub, gr, tn, preferred_element_type=F32)
        dbim_ref[...] += lax.dot_general(ub, gi, tn, preferred_element_type=F32)
        dcre_ref[...] += lax.dot_general(sr.astype(BF16), dyb, tn, preferred_element_type=F32)
        dcim_ref[...] -= lax.dot_general(si.astype(BF16), dyb, tn, preferred_element_type=F32)
        dd_ref[...] += _rowsum(dyf * uf)

    chan = pl.BlockSpec((None, lc, cb), lambda n, b, t: (b, nc - 1 - t, n))
    state = pl.BlockSpec((None, lc, ns), lambda n, b, t: (b, nc - 1 - t, n))
    par = lambda r, c: pl.BlockSpec((None, r, c), lambda n, b, t: (n, 0, 0))
    return pl.pallas_call(
        body, name="s5_bwd", grid=(nb, bsz, nc),
        in_specs=[chan, chan, state, state, par(cb, ns), par(cb, ns), par(ns, cb), par(ns, cb),
                  par(1, ns), par(1, ns), pl.BlockSpec((None, 1, cb), lambda n, b, t: (0, 0, n))],
        out_specs=[chan, par(cb, ns), par(cb, ns), par(ns, cb), par(ns, cb), par(1, ns), par(1, ns), par(1, cb)],
        out_shape=[jax.ShapeDtypeStruct((bsz, seq, d), F32),
                   jax.ShapeDtypeStruct((nb, cb, ns), F32), jax.ShapeDtypeStruct((nb, cb, ns), F32),
                   jax.ShapeDtypeStruct((nb, ns, cb), F32), jax.ShapeDtypeStruct((nb, ns, cb), F32),
                   jax.ShapeDtypeStruct((nb, 1, ns), F32), jax.ShapeDtypeStruct((nb, 1, ns), F32),
                   jax.ShapeDtypeStruct((nb, 1, cb), F32)],
        scratch_shapes=[pltpu.VMEM((lc, ns), F32)] * 4 + [pltpu.VMEM((1, ns), F32)] * 2,
        compiler_params=_cparams("arbitrary", "arbitrary", "arbitrary"),
    )(dy, u, st_re, st_im, bbd_re, bbd_im, cbd_re, cbd_im, ab_re, ab_im, dskip)


ATT_HEADS = 4
ATT_LANES = ATT_HEADS * HEAD_DIM
ATT_KEYS = 2 * ATT_BLOCK
ATT_SCALE = 1.0 / math.sqrt(HEAD_DIM)
_NT = (((1,), (1,)), ((), ()))
_TN = (((0,), (0,)), ((), ()))
_HEADS = [slice(h * HEAD_DIM, (h + 1) * HEAD_DIM) for h in range(ATT_HEADS)]
_HALF = [slice(0, ATT_BLOCK), slice(ATT_BLOCK, ATT_KEYS)]


def _log_sigmoids(z):
    sp = jnp.log(1.0 + jnp.exp(-jnp.abs(z)))
    ls = jnp.minimum(z, 0.0) - sp
    return ls, ls - z


def _sum_matrix(after, inclusive):
    j = lax.broadcasted_iota(jnp.int32, (ATT_KEYS, ATT_KEYS), 0) % ATT_BLOCK
    s = lax.broadcasted_iota(jnp.int32, (ATT_KEYS, ATT_KEYS), 1)
    if after:
        hit = (j >= s) if inclusive else (j > s)
    else:
        hit = (j <= s) if inclusive else (j < s)
    return jnp.where(jnp.logical_or(hit, s >= ATT_BLOCK), 1.0, 0.0).astype(BF16)


def _hi_lo(v):
    hi = v.astype(BF16)
    lo = (v - hi.astype(F32)).astype(BF16)
    return jnp.concatenate([hi, lo], axis=1)


def _strict_mask(i, j):
    t = i * ATT_BLOCK + lax.broadcasted_iota(jnp.int32, (ATT_BLOCK, ATT_KEYS), 0)
    s = j * ATT_KEYS + lax.broadcasted_iota(jnp.int32, (ATT_BLOCK, ATT_KEYS), 1)
    return s < t


def attention_fwd(q, k, v):
    bsz, seq, d = q.shape

    def body(q_ref, k_ref, v_ref, o_ref, tot_ref, z_buf, ls_buf, cs_buf, acc_buf, run_buf):
        i = pl.program_id(2)
        jd = i // 2
        sums = _sum_matrix(True, False)
        acc_buf[...] = jnp.zeros_like(acc_buf)
        run_buf[...] = jnp.zeros_like(run_buf)

        def block(j, masked):
            rows = pl.ds(pl.multiple_of(j * ATT_KEYS, ATT_KEYS), ATT_KEYS)
            strict = _strict_mask(i, j) if masked else None
            for h, ln in enumerate(_HEADS):
                z_buf[h] = lax.dot_general(q_ref[:, ln], k_ref[rows, ln], _NT, preferred_element_type=F32)
            for h in range(ATT_HEADS):
                for half, cols in enumerate(_HALF):
                    ls, lf = _log_sigmoids(z_buf[h, :, cols])
                    if masked:
                        lf = jnp.where(strict[:, cols], lf, 0.0)
                    ls_buf[h, :, cols] = ls
                    cs_buf[h, half] = jnp.dot(_hi_lo(lf), sums, preferred_element_type=F32)
            for h, ln in enumerate(_HEADS):
                run = run_buf[h]
                late, early = cs_buf[h, 1], cs_buf[h, 0]
                a1 = run + late[:, _HALF[0]]
                run = run + late[:, _HALF[1]]
                a0 = run + early[:, _HALF[0]]
                run_buf[h] = run + early[:, _HALF[1]]
                w = jnp.exp(ls_buf[h] + jnp.concatenate([a0, a1], axis=1))
                if masked:
                    w = jnp.where(strict, w, 0.0)
                acc_buf[h] += jnp.dot(w.astype(BF16), v_ref[rows, ln], preferred_element_type=F32)

        block(jd, True)

        def step(it, carry):
            block(jd - 1 - it, False)
            return carry

        lax.fori_loop(0, jd, step, 0)
        o_ref[...] = jnp.concatenate([acc_buf[h] for h in range(ATT_HEADS)], axis=1).astype(o_ref.dtype)
        tot_ref[...] = jnp.concatenate([run_buf[h, :, :HEAD_DIM] for h in range(ATT_HEADS)], axis=1)

    blk = pl.BlockSpec((None, ATT_BLOCK, ATT_LANES), lambda b, p, i: (b, i, p))
    full = pl.BlockSpec((None, seq, ATT_LANES), lambda b, p, i: (b, 0, p))
    tile = (ATT_HEADS, ATT_BLOCK, ATT_KEYS)
    return pl.pallas_call(
        body, name="attention_fwd", grid=(bsz, d // ATT_LANES, seq // ATT_BLOCK),
        in_specs=[blk, full, full], out_specs=[blk, blk],
        out_shape=[jax.ShapeDtypeStruct((bsz, seq, d), BF16), jax.ShapeDtypeStruct((bsz, seq, d), F32)],
        scratch_shapes=[pltpu.VMEM(tile, F32), pltpu.VMEM(tile, F32),
                        pltpu.VMEM((ATT_HEADS, 2, ATT_BLOCK, ATT_KEYS), F32),
                        pltpu.VMEM((ATT_HEADS, ATT_BLOCK, HEAD_DIM), F32),
                        pltpu.VMEM((ATT_HEADS, ATT_BLOCK, ATT_BLOCK), F32)],
        compiler_params=_cparams("arbitrary", "arbitrary", "arbitrary"),
    )(q, k, v)


def attention_bwd(q, k, v, tot, do):
    bsz, seq, d = q.shape

    def body(q_ref, k_ref, v_ref, tot_ref, do_ref, dq_ref, dk_ref, dv_ref,
             z_buf, dw_buf, ls_buf, e_buf, up_buf, bf_buf, w_buf, do_buf, dq_buf, tot_buf, run_buf, erun_buf):
        i = pl.program_id(2)
        jd = i // 2

        @pl.when(i == 0)
        def _():
            dk_ref[...] = jnp.zeros_like(dk_ref)
            dv_ref[...] = jnp.zeros_like(dv_ref)

        upto_incl, upto_excl = _sum_matrix(False, True), _sum_matrix(False, False)
        do_buf[...] = do_ref[...].astype(BF16)
        for h, ln in enumerate(_HEADS):
            tot_buf[h] = jnp.concatenate([tot_ref[:, ln], tot_ref[:, ln]], axis=1)
        dq_buf[...] = jnp.zeros_like(dq_buf)
        run_buf[...] = jnp.zeros_like(run_buf)
        erun_buf[...] = jnp.zeros_like(erun_buf)

        def block(j, masked):
            rows = pl.ds(pl.multiple_of(j * ATT_KEYS, ATT_KEYS), ATT_KEYS)
            strict = _strict_mask(i, j) if masked else None
            for h, ln in enumerate(_HEADS):
                z_buf[h] = lax.dot_general(q_ref[:, ln], k_ref[rows, ln], _NT, preferred_element_type=F32)
                dw_buf[h] = lax.dot_general(do_buf[:, ln], v_ref[rows, ln], _NT, preferred_element_type=F32)
            for h in range(ATT_HEADS):
                for half, cols in enumerate(_HALF):
                    ls, lf = _log_sigmoids(z_buf[h, :, cols])
                    if masked:
                        lf = jnp.where(strict[:, cols], lf, 0.0)
                    ls_buf[h, :, cols] = ls
                    up_buf[h, half] = jnp.dot(_hi_lo(lf), upto_incl, preferred_element_type=F32)
            for h in range(ATT_HEADS):
                run = run_buf[h]
                early, late = up_buf[h, 0], up_buf[h, 1]
                u0 = run + early[:, _HALF[0]]
                run = run + early[:, _HALF[1]]
                u1 = run + late[:, _HALF[0]]
                run_buf[h] = run + late[:, _HALF[1]]
                tot_h = tot_buf[h]
                after = jnp.concatenate([tot_h - u0, tot_h - u1], axis=1)
                w = jnp.exp(ls_buf[h] + after)
                if masked:
                    w = jnp.where(strict, w, 0.0)
                w_buf[h] = w.astype(BF16)
                e = dw_buf[h] * w
                e_buf[h] = e
                for half, cols in enumerate(_HALF):
                    bf_buf[h, half] = jnp.dot(_hi_lo(e[:, cols]), upto_excl, preferred_element_type=F32)
            dks, dvs = [], []
            for h, ln in enumerate(_HEADS):
                erun = erun_buf[h]
                early, late = bf_buf[h, 0], bf_buf[h, 1]
                b0 = erun + early[:, _HALF[0]]
                erun = erun + early[:, _HALF[1]]
                b1 = erun + late[:, _HALF[0]]
                erun_buf[h] = erun + late[:, _HALF[1]]
                e = e_buf[h]
                dz = e - jnp.exp(ls_buf[h]) * (e + jnp.concatenate([b0, b1], axis=1))
                if masked:
                    dz = jnp.where(strict, dz, 0.0)
                dz = dz.astype(BF16)
                dq_buf[h] += jnp.dot(dz, k_ref[rows, ln], preferred_element_type=F32)
                dks.append(lax.dot_general(dz, q_ref[:, ln], _TN, preferred_element_type=F32))
                dvs.append(lax.dot_general(w_buf[h], do_buf[:, ln], _TN, preferred_element_type=F32))
            dk_ref[rows, :] += jnp.concatenate(dks, axis=1)
            dv_ref[rows, :] += jnp.concatenate(dvs, axis=1)

        def step(j, carry):
            block(j, False)
            return carry

        lax.fori_loop(0, jd, step, 0)
        block(jd, True)
        dq_ref[...] = jnp.concatenate([dq_buf[h] for h in range(ATT_HEADS)], axis=1) * ATT_SCALE

    blk = pl.BlockSpec((None, ATT_BLOCK, ATT_LANES), lambda b, p, i: (b, i, p))
    full = pl.BlockSpec((None, seq, ATT_LANES), lambda b, p, i: (b, 0, p))
    shape = jax.ShapeDtypeStruct((bsz, seq, d), F32)
    tile = (ATT_HEADS, ATT_BLOCK, ATT_KEYS)
    pair = (ATT_HEADS, 2, ATT_BLOCK, ATT_KEYS)
    square = (ATT_HEADS, ATT_BLOCK, ATT_BLOCK)
    return pl.pallas_call(
        body, name="attention_bwd", grid=(bsz, d // ATT_LANES, seq // ATT_BLOCK),
        in_specs=[blk, full, full, blk, blk], out_specs=[blk, full, full], out_shape=[shape, shape, shape],
        scratch_shapes=[pltpu.VMEM(tile, F32), pltpu.VMEM(tile, F32), pltpu.VMEM(tile, F32), pltpu.VMEM(tile, F32),
                        pltpu.VMEM(pair, F32), pltpu.VMEM(pair, F32), pltpu.VMEM(tile, BF16),
                        pltpu.VMEM((ATT_BLOCK, ATT_LANES), BF16), pltpu.VMEM((ATT_HEADS, ATT_BLOCK, HEAD_DIM), F32),
                        pltpu.VMEM(square, F32), pltpu.VMEM(square, F32), pltpu.VMEM(square, F32)],
        compiler_params=_cparams("arbitrary", "arbitrary", "arbitrary"),
    )(q, k, v, tot, do)


def mlp_fwd(x, g, sh, sc, gate, w1_handle, w2_handle, tag):
    bsz, seq, d = x.shape
    t = bsz * seq
    h = act_call(_norm_mod, [x, g, sh, sc], [(d, BF16, "tile")], tag + "_norm")[0]
    w1 = exchange_wait(w1_handle, h, tag + "_w1_wait")
    act = mm_nn_col(h.reshape(t, d), w1, tag + "_up", (BF16,),
                    lambda acc: (jnp.square(jnp.maximum(acc, 0.0)),))
    w2 = exchange_wait(w2_handle, act, tag + "_w2_wait")
    ff = mm_nn_row(act, w2, tag + "_down").reshape(bsz, seq, d)
    out = act_call(lambda x_, f_, g_: x_ + g_ * f_, [x, ff, gate], [(d, F32, "tile")], tag + "_res")[0]
    return out, (h, act, ff), w1, w2


def mlp_bwd(dout, x, g, sc, gate, w1, w2, saved, tag):
    bsz, seq, d = x.shape
    t = bsz * seq
    ns = w1.shape[0]
    h, act, ff = saved
    dff, dgate = act_call(lambda do_, f_, g_: (g_ * do_, _rowsum(do_ * f_)), [dout, ff, gate],
                          [(d, BF16, "tile"), (d, F32, "seq")], tag + "_dres")
    dff = dff.reshape(t, d)
    dpre = mm_nt_row(dff, w2, tag + "_dact", (BF16,),
                     lambda acc, a_: (acc * (2.0 * jnp.sqrt(a_.astype(F32))),), (act,))
    dw2 = mm_tn(act, dff, "a", ns, tag + "_dw2")
    dw1 = mm_tn(h.reshape(t, d), dpre, "c", ns, tag + "_dw1")
    (dw1_handle, dw2_handle), token = exchange_start([(dw1, True), (dw2, True)], tag + "_dw_start")
    dh = mm_nt_col(dpre, w1, tag + "_dh").reshape(bsz, seq, d)
    dx, dsh, dsc, dg = act_call(_norm_mod_bwd, [x, g + token[0, 0], sc, dh, dout],
                                [(d, F32, "tile"), (d, F32, "seq"), (d, F32, "seq"), (d, F32, "all")],
                                tag + "_dnorm")
    return dx, dw1_handle, dw2_handle, (dsh, dsc, dgate, dg)


def _block_diag(m, rows_first):
    nb, k, r, c = m.shape
    eye = jnp.eye(k, dtype=m.dtype)
    return jnp.einsum("nkrc,kl->nkrlc", m, eye).reshape(nb, k * r, k * c)


def _block_diag_part(m, r, c):
    nb = m.shape[0]
    k = m.shape[1] // r
    return jnp.einsum("nkrlc,kl->nkrc", m.reshape(nb, k, r, k, c), jnp.eye(k, dtype=m.dtype))


def kernel(x, c, ada_w, ada_b, mix_norm_g, mlp_norm_g, mlp_w1, mlp_w2, s5_a_re, s5_a_im, s5_log_dt, s5_b_re, s5_b_im, s5_c_re, s5_c_im, s5_d, s5_w_glu, kv_ada_w, kv_ada_b, kv_norm_g, w_kv, k_norm_g, sb_w_q, q_norm_g, sb_w_o, loss_target, m_ada_w, m_ada_b, m_mix_norm_g, m_mlp_norm_g, m_mlp_w1, m_mlp_w2, m_s5_a_re, m_s5_a_im, m_s5_log_dt, m_s5_b_re, m_s5_b_im, m_s5_c_re, m_s5_c_im, m_s5_d, m_s5_w_glu, m_kv_ada_w, m_kv_ada_b, m_kv_norm_g, m_w_kv, m_k_norm_g, m_sb_w_q, m_q_norm_g, m_sb_w_o, v_ada_w, v_ada_b, v_mix_norm_g, v_mlp_norm_g, v_mlp_w1, v_mlp_w2, v_s5_a_re, v_s5_a_im, v_s5_log_dt, v_s5_b_re, v_s5_b_im, v_s5_c_re, v_s5_c_im, v_s5_d, v_s5_w_glu, v_kv_ada_w, v_kv_ada_b, v_kv_norm_g, v_w_kv, v_k_norm_g, v_sb_w_q, v_q_norm_g, v_sb_w_o):
    bsz, seq, d = x.shape
    t = bsz * seq
    n_groups = d // S5_GROUP
    nb = n_groups // S5_BLOCK_GROUPS
    gp = n_groups * S5_STATE
    dev = 4 * lax.axis_index("x") + 2 * lax.axis_index("y") + lax.axis_index("c")
    e_ada, e_kv = 6 * d, 2 * d
    n_ada, n_kv = e_ada // N_DEV, e_kv // N_DEV

    d_skip = all_gather(s5_d, "gather_skip").reshape(1, 1, d)
    c_all = all_gather(c, "gather_c").reshape(N_DEV * bsz, d)

    w_cols = jnp.concatenate([ada_w[0], ada_w[1], kv_ada_w], axis=1)
    b_cols = jnp.concatenate([
        lax.dynamic_slice_in_dim(ada_b[0], dev * n_ada, n_ada),
        lax.dynamic_slice_in_dim(ada_b[1], dev * n_ada, n_ada),
        lax.dynamic_slice_in_dim(kv_ada_b, dev * n_kv, n_kv)])[None, :]
    mod_cols = ada_fwd(c_all, w_cols, b_cols)
    mod_all = all_gather(mod_cols, "gather_mod")
    mod_mine = lax.dynamic_slice_in_dim(mod_all, dev * bsz, bsz, axis=1)
    mod_mine = jnp.transpose(mod_mine, (1, 0, 2))
    mods = []
    for i in range(2):
        full = mod_mine[:, :, i * n_ada:(i + 1) * n_ada].reshape(bsz, e_ada)
        mods.append([full[:, None, j * d:(j + 1) * d] for j in range(6)])
    kv_full = mod_mine[:, :, 2 * n_ada:].reshape(bsz, e_kv)
    kv_sh, kv_sc = kv_full[:, None, :d], kv_full[:, None, d:]

    par = lambda p: p.reshape(1, 1, -1)

    shards = [s5_w_glu[0], mlp_w1[0], mlp_w2[0], w_kv, sb_w_q[0], sb_w_o[0], mlp_w1[1], mlp_w2[1]]
    gathers, gather_token = exchange_start([(w.astype(BF16), False) for w in shards], "gather_start", after=mod_all)
    glu_handle, w1_0_handle, w2_0_handle, wkv_handle, wq_handle, wo_handle, w1_1_handle, w2_1_handle = gathers
    started = gather_token[0, 0]

    sh_a, sc_a, g_a, sh_m, sc_m, g_m = mods[0]
    lam_re, lam_im = s5_a_re.reshape(gp, 1), s5_a_im.reshape(gp, 1)
    log_dt = jnp.broadcast_to(s5_log_dt.reshape(n_groups, 1), (n_groups, S5_STATE)).reshape(gp, 1)
    b_re, b_im = s5_b_re.reshape(gp, S5_GROUP), s5_b_im.reshape(gp, S5_GROUP)
    ab_re, ab_im, bb_re, bb_im = s5_prep(lam_re, lam_im, log_dt, b_re, b_im)
    to_bbd = lambda m: _block_diag(jnp.swapaxes(m.reshape(nb, S5_BLOCK_GROUPS, S5_STATE, S5_GROUP), 2, 3), True)
    to_cbd = lambda m: _block_diag(jnp.swapaxes(m.reshape(nb, S5_BLOCK_GROUPS, S5_GROUP, S5_STATE), 2, 3), True)
    bbd_re, bbd_im = to_bbd(bb_re).astype(BF16), to_bbd(bb_im).astype(BF16)
    cbd_re, cbd_im = to_cbd(s5_c_re[0]).astype(BF16), to_cbd(s5_c_im[0]).astype(BF16)
    abr, abi = ab_re.reshape(nb, 1, -1), ab_im.reshape(nb, 1, -1)

    h0 = act_call(_norm_mod, [x, par(mix_norm_g[0]) + started, sh_a, sc_a], [(d, F32, "tile")], "mix0_norm")[0]
    y, st_re, st_im = s5_fwd(h0, bbd_re, bbd_im, cbd_re, cbd_im, abr, abi, d_skip)
    ge = act_call(lambda y_: jax.nn.gelu(y_), [y], [(d, BF16, "tile")], "gelu")[0]
    w_glu = exchange_wait(glu_handle, ge, "glu_w_wait")
    z = mm_nn_col(ge.reshape(t, d), w_glu, "glu_up").reshape(bsz, seq, 2 * d)
    x1 = act_call(lambda x_, z_, g_: x_ + g_ * (z_[:, :d] * jax.nn.sigmoid(z_[:, d:])), [x, z, g_a],
                  [(d, F32, "tile")], "glu_res")[0]
    x2, mlp0_saved, w1_0, w2_0 = mlp_fwd(x1, par(mlp_norm_g[0]), sh_m, sc_m, g_m, w1_0_handle, w2_0_handle, "mlp0")

    sh_a1, sc_a1, g_a1, sh_m1, sc_m1, g_m1 = mods[1]
    kg = par(jnp.tile(k_norm_g, d // HEAD_DIM))
    qg = par(jnp.tile(q_norm_g[0], d // HEAD_DIM))
    hkv = act_call(_norm_mod, [x2, par(kv_norm_g), kv_sh, kv_sc], [(d, BF16, "tile")], "kv_norm")[0]
    wkv = exchange_wait(wkv_handle, hkv, "kv_w_wait")
    kvf = mm_nn_col(hkv.reshape(t, d), wkv, "kv_proj").reshape(bsz, seq, 2 * d)
    k_h, v_h = act_call(lambda kv_, g_: (_head_norm(kv_[:, :d], g_), kv_[:, d:]), [kvf, kg],
                        [(d, BF16, "tile"), (d, BF16, "tile")], "k_norm")
    h1 = act_call(_norm_mod, [x2, par(mix_norm_g[1]), sh_a1, sc_a1], [(d, BF16, "tile")], "mix1_norm")[0]
    wq = exchange_wait(wq_handle, h1, "q_w_wait")
    q_raw = mm_nn_row(h1.reshape(t, d), wq, "q_proj").reshape(bsz, seq, d)
    q_h = act_call(lambda x_, g_: _head_norm(x_, g_) * ATT_SCALE, [q_raw, qg], [(d, BF16, "tile")], "q_norm")[0]
    o, att_tot = attention_fwd(q_h, k_h, v_h)
    wo = exchange_wait(wo_handle, o, "o_w_wait")
    mix1 = mm_nn_row(o.reshape(t, d), wo, "o_proj").reshape(bsz, seq, d)
    x3 = act_call(lambda x_, f_, g_: x_ + g_ * f_, [x2, mix1, g_a1], [(d, F32, "tile")], "att_res")[0]
    x4, mlp1_saved, w1_1, w2_1 = mlp_fwd(x3, par(mlp_norm_g[1]), sh_m1, sc_m1, g_m1, w1_1_handle, w2_1_handle, "mlp1")

    def loss_fn(y_, t_):
        diff = y_ - t_
        part = jnp.sum(0.5 * jnp.mean(diff * diff, axis=-1, keepdims=True), axis=0, keepdims=True)
        return jnp.broadcast_to(part, (1, LANES)), diff * (1.0 / d)

    loss_part, dx4 = act_call(loss_fn, [x4, loss_target], [(LANES, F32, "all"), (d, F32, "tile")], "loss")
    loss = lax.psum(loss_part[0, 0, 0], ("x", "y", "c"))

    dx3, dw1_1, dw2_1, (dsh_m1, dsc_m1, dg_m1, dgn_mlp1) = mlp_bwd(
        dx4, x3, par(mlp_norm_g[1]), sc_m1, g_m1, w1_1, w2_1, mlp1_saved, "mlp1")
    dmix1, dg_a1 = act_call(lambda do_, f_, g_: (g_ * do_, _rowsum(do_ * f_)), [dx3, mix1, g_a1],
                            [(d, BF16, "tile"), (d, F32, "seq")], "att_dres")
    dmix1 = dmix1.reshape(t, d)
    do = mm_nt_row(dmix1, wo, "o_dproj").reshape(bsz, seq, d)
    dwo = mm_tn(o.reshape(t, d), dmix1, "a", N_DEV, "o_dw")
    dq, dk, dv = attention_bwd(q_h, k_h, v_h, att_tot, do)
    dq_raw, dqg = act_call(_head_norm_bwd, [q_raw, qg, dq], [(d, BF16, "tile"), (d, F32, "all")], "q_dnorm")
    dq_raw = dq_raw.reshape(t, d)
    dh1 = mm_nt_row(dq_raw, wq, "q_dproj").reshape(bsz, seq, d)
    dwq = mm_tn(h1.reshape(t, d), dq_raw, "a", N_DEV, "q_dw")
    dx2, dsh_a1, dsc_a1, dgn_mix1 = act_call(
        _norm_mod_bwd, [x2, par(mix_norm_g[1]), sc_a1, dh1, dx3],
        [(d, F32, "tile"), (d, F32, "seq"), (d, F32, "seq"), (d, F32, "all")], "mix1_dnorm")

    def kv_bwd_fn(kv_, g_, dk_, dv_):
        dk_raw, dg_ = _head_norm_bwd(kv_[:, :d], g_, dk_)
        return jnp.concatenate([dk_raw, dv_], axis=1), dg_

    dkvf, dkg = act_call(kv_bwd_fn, [kvf, kg, dk, dv], [(2 * d, BF16, "tile"), (d, F32, "all")], "k_dnorm")
    dkvf = dkvf.reshape(t, 2 * d)
    dhkv = mm_nt_col(dkvf, wkv, "kv_dproj").reshape(bsz, seq, d)
    dwkv = mm_tn(hkv.reshape(t, d), dkvf, "c", N_DEV, "kv_dw")
    (dwo, dwq, dwkv), att_token = exchange_start([(dwo, True), (dwq, True), (dwkv, True)], "att_dw_start")
    dx2, dkv_sh, dkv_sc, dgn_kv = act_call(
        _norm_mod_bwd, [x2, par(kv_norm_g) + att_token[0, 0], kv_sc, dhkv, dx2],
        [(d, F32, "tile"), (d, F32, "seq"), (d, F32, "seq"), (d, F32, "all")], "kv_dnorm")

    dx1, dw1_0, dw2_0, (dsh_m0, dsc_m0, dg_m0, dgn_mlp0) = mlp_bwd(
        dx2, x1, par(mlp_norm_g[0]), sc_m, g_m, w1_0, w2_0, mlp0_saved, "mlp0")

    def glu_bwd_fn(do_, z_, g_):
        val, sig = z_[:, :d], jax.nn.sigmoid(z_[:, d:])
        dmix = g_ * do_
        dz = jnp.concatenate([dmix * sig, dmix * val * sig * (1.0 - sig)], axis=1)
        return dz, _rowsum(do_ * (val * sig))

    dz, dg_a0 = act_call(glu_bwd_fn, [dx1, z, g_a], [(2 * d, BF16, "tile"), (d, F32, "seq")], "glu_dres")
    dz = dz.reshape(t, 2 * d)
    dy = mm_nt_col(dz, w_glu, "glu_dup", (F32,), lambda acc, y_: (acc * _gelu_grad(y_),),
                   (y.reshape(t, d),)).reshape(bsz, seq, d)
    dwglu = mm_tn(ge.reshape(t, d), dz, "c", N_DEV, "glu_dw")
    (dwglu,), glu_token = exchange_start([(dwglu, True)], "glu_dw_start")
    du, dbbd_re, dbbd_im, dcbd_re, dcbd_im, dab_re, dab_im, dd_skip = s5_bwd(
        dy, h0, st_re, st_im, bbd_re, bbd_im, cbd_re, cbd_im, abr, abi, d_skip + glu_token[0, 0])
    dx0, dsh_a0, dsc_a0, dgn_mix0 = act_call(
        _norm_mod_bwd, [x, par(mix_norm_g[0]), sc_a, du, dx1],
        [(d, F32, "tile"), (d, F32, "seq"), (d, F32, "seq"), (d, F32, "all")], "mix0_dnorm")

    from_bbd = lambda m: jnp.swapaxes(_block_diag_part(m, S5_GROUP, S5_STATE), 2, 3).reshape(gp, S5_GROUP)
    d_c = lambda m: jnp.swapaxes(_block_diag_part(m, S5_STATE, S5_GROUP), 2, 3).reshape(
        1, n_groups, S5_GROUP, S5_STATE)
    d_lam_re, d_lam_im, d_log_dt, d_b_re, d_b_im = s5_prep_bwd(
        lam_re, lam_im, log_dt, b_re, b_im, dab_re.reshape(gp, 1), dab_im.reshape(gp, 1),
        from_bbd(dbbd_re), from_bbd(dbbd_im))

    small = all_reduce_small([
        jnp.stack([dgn_mix0.reshape(d), dgn_mix1.reshape(d)]),
        jnp.stack([dgn_mlp0.reshape(d), dgn_mlp1.reshape(d)]),
        d_lam_re.reshape(1, n_groups, S5_STATE), d_lam_im.reshape(1, n_groups, S5_STATE),
        d_log_dt.reshape(1, n_groups, S5_STATE).sum(axis=-1),
        d_b_re.reshape(s5_b_re.shape), d_b_im.reshape(s5_b_im.shape),
        d_c(dcbd_re), d_c(dcbd_im),
        dd_skip.reshape(1, d),
        dgn_kv.reshape(d),
        dkg.reshape(d // HEAD_DIM, HEAD_DIM).sum(axis=0),
        dqg.reshape(d // HEAD_DIM, HEAD_DIM).sum(axis=0)[None, :],
    ], "small_grads")
    (g_mix_norm, g_mlp_norm, g_a_re, g_a_im, g_log_dt, g_b_re, g_b_im, g_c_re, g_c_im,
     g_skip_full, g_kv_norm, g_k_norm, g_q_norm) = small
    g_s5_d = lax.dynamic_slice_in_dim(g_skip_full, dev * (d // N_DEV), d // N_DEV, axis=1)

    dm_mine = jnp.concatenate([
        dsh_a0, dsc_a0, dg_a0, dsh_m0, dsc_m0, dg_m0,
        dsh_a1, dsc_a1, dg_a1, dsh_m1, dsc_m1, dg_m1, dkv_sh, dkv_sc], axis=2).reshape(bsz, 2 * e_ada + e_kv)
    dm_all = all_gather(dm_mine, "gather_dmod").reshape(N_DEV * bsz, 2 * e_ada + e_kv)
    dm_cols = jnp.concatenate([
        lax.dynamic_slice_in_dim(dm_all, dev * n_ada, n_ada, axis=1),
        lax.dynamic_slice_in_dim(dm_all, e_ada + dev * n_ada, n_ada, axis=1),
        lax.dynamic_slice_in_dim(dm_all, 2 * e_ada + dev * n_kv, n_kv, axis=1)], axis=1)
    dw_cols, db_all = ada_bwd(c_all, dm_cols, dm_all)
    g_ada_w = jnp.stack([dw_cols[:, :n_ada], dw_cols[:, n_ada:2 * n_ada]])
    g_kv_ada_w = dw_cols[:, 2 * n_ada:]
    g_ada_b = db_all[0, :2 * e_ada].reshape(2, e_ada)
    g_kv_ada_b = db_all[0, 2 * e_ada:]

    landed = lambda handle, name: slab_sum(exchange_wait(handle, dx0, name + "_wait"), name + "_sum")
    g_w1 = jnp.stack([landed(dw1_0, "rs_w1_0"), landed(dw1_1, "rs_w1_1")])
    g_w2 = jnp.stack([landed(dw2_0, "rs_w2_0"), landed(dw2_1, "rs_w2_1")])
    g_glu = landed(dwglu, "rs_glu")[None]
    g_wkv = landed(dwkv, "rs_wkv")
    g_wq = landed(dwq, "rs_wq")[None]
    g_wo = landed(dwo, "rs_wo")[None]

    weights = [ada_w, ada_b, mix_norm_g, mlp_norm_g, mlp_w1, mlp_w2, s5_a_re, s5_a_im, s5_log_dt, s5_b_re,
               s5_b_im, s5_c_re, s5_c_im, s5_d, s5_w_glu, kv_ada_w, kv_ada_b, kv_norm_g, w_kv, k_norm_g,
               sb_w_q, q_norm_g, sb_w_o]
    grads = [g_ada_w, g_ada_b, g_mix_norm, g_mlp_norm, g_w1, g_w2, g_a_re, g_a_im, g_log_dt, g_b_re,
             g_b_im, g_c_re, g_c_im, g_s5_d, g_glu, g_kv_ada_w, g_kv_ada_b, g_kv_norm, g_wkv, g_k_norm,
             g_wq, g_q_norm, g_wo]
    ms = [m_ada_w, m_ada_b, m_mix_norm_g, m_mlp_norm_g, m_mlp_w1, m_mlp_w2, m_s5_a_re, m_s5_a_im, m_s5_log_dt,
          m_s5_b_re, m_s5_b_im, m_s5_c_re, m_s5_c_im, m_s5_d, m_s5_w_glu, m_kv_ada_w, m_kv_ada_b, m_kv_norm_g,
          m_w_kv, m_k_norm_g, m_sb_w_q, m_q_norm_g, m_sb_w_o]
    vs = [v_ada_w, v_ada_b, v_mix_norm_g, v_mlp_norm_g, v_mlp_w1, v_mlp_w2, v_s5_a_re, v_s5_a_im, v_s5_log_dt,
          v_s5_b_re, v_s5_b_im, v_s5_c_re, v_s5_c_im, v_s5_d, v_s5_w_glu, v_kv_ada_w, v_kv_ada_b, v_kv_norm_g,
          v_w_kv, v_k_norm_g, v_sb_w_q, v_q_norm_g, v_sb_w_o]
    grads = [g.reshape(w.shape) for g, w in zip(grads, weights)]
    deltas, new_ms, new_vs = [], [], []
    for i, (w, g, m, v) in enumerate(zip(weights, grads, ms, vs)):
        dl, nm, nv = adamw(w, g, m, v, f"adamw_{i}")
        deltas.append(dl)
        new_ms.append(nm)
        new_vs.append(nv)
    return (loss, dx0, *grads, *deltas, *new_ms, *new_vs)
```

```python
import functools
import math

import jax
import jax.numpy as jnp
from jax import lax
from jax.experimental import pallas as pl
from jax.experimental.pallas import tpu as pltpu

F32 = jnp.float32
BF16 = jnp.bfloat16

N_DEV = 8
N_CHIPS = 4
MESH = pl.DeviceIdType.MESH
ANY = pl.BlockSpec(memory_space=pl.ANY)

LANES = 128
VMEM_LIMIT_BYTES = 48 * 2 ** 20
TILE_BUDGET_BYTES = 4 * 2 ** 20

S5_GROUP = 16
S5_STATE = 64
S5_BLOCK_GROUPS = 16
HEAD_DIM = 64
ATT_BLOCK = 128
EPS = 1e-6

ADAM_LR = 0.001
ADAM_B1 = 0.9
ADAM_B2 = 0.999
ADAM_EPS = 1e-08
ADAM_WD = 0.01
ADAM_STEP = 10


def _cparams(*sem):
    return pltpu.CompilerParams(dimension_semantics=sem, vmem_limit_bytes=VMEM_LIMIT_BYTES)


def _divisor_tile(n, limit, mult):
    best = None
    for t in range(mult, min(n, limit) + 1, mult):
        if n % t == 0:
            best = t
    return best if best is not None else n


def _tile_m(m):
    return _divisor_tile(m, 2048 if m >= 4096 else 256, 16)


def all_gather(x, name):
    def body(x_ref, out_ref, send_sems, recv_sems, local_sem):
        ax, ay, ac = lax.axis_index("x"), lax.axis_index("y"), lax.axis_index("c")
        me, sibling = (ax, ay, ac), (ax, ay, 1 - ac)
        chips = [(1 - ax, ay), (ax, 1 - ay), (1 - ax, 1 - ay)]

        def slot(px, py, pc):
            return out_ref.at[4 * px + 2 * py + pc]

        def copy(k, block, to, src=None):
            return pltpu.make_async_remote_copy(
                src_ref=slot(*block) if src is None else src, dst_ref=slot(*block),
                send_sem=send_sems.at[k], recv_sem=recv_sems.at[k], device_id=to, device_id_type=MESH)

        mine = pltpu.make_async_copy(x_ref, slot(*me), local_sem)
        mine.start()
        first = [copy(0, me, sibling, src=x_ref)]
        first += [copy(1 + j, me, (*chip, ac), src=x_ref) for j, chip in enumerate(chips)]
        for cp in first:
            cp.start()
        passed = [copy(4 + j, (*chip, ac), sibling) for j, chip in enumerate(chips)]
        for j, chip in enumerate(chips):
            copy(1 + j, (*chip, ac), me).wait_recv()
            passed[j].start()
        copy(0, sibling, me).wait_recv()
        for j, chip in enumerate(chips):
            copy(4 + j, (*chip, 1 - ac), me).wait_recv()
        for cp in first + passed:
            cp.wait_send()
        mine.wait()

    return pl.pallas_call(
        body, name=name,
        out_shape=jax.ShapeDtypeStruct((N_DEV,) + x.shape, x.dtype),
        in_specs=[ANY], out_specs=ANY,
        scratch_shapes=[pltpu.SemaphoreType.DMA((7,)), pltpu.SemaphoreType.DMA((7,)), pltpu.SemaphoreType.DMA],
    )(x)


HBM = pl.BlockSpec(memory_space=pltpu.HBM)
SEM = pl.BlockSpec(memory_space=pltpu.SEMAPHORE)
N_PEERS = N_DEV - 1


def _peers():
    ax, ay, ac = lax.axis_index("x"), lax.axis_index("y"), lax.axis_index("c")
    flip = lambda v, bit: 1 - v if bit else v
    return [(flip(ax, k & 4), flip(ay, k & 2), flip(ac, k & 1)) for k in range(1, N_DEV)]


def _dev_index(pos):
    return 4 * pos[0] + 2 * pos[1] + pos[2]


def exchange_start(items, name, after=None):
    n = len(items)
    srcs = [a for a, _ in items]
    blocks = [a.shape[1:] if scatter else a.shape for a, scatter in items]
    extra = list(after or ())

    def body(*refs):
        src_refs, land_refs = refs[:n], refs[n:2 * n]
        outs = refs[2 * n + len(extra):]
        send_sems, recv_sems = outs[:n], outs[n:2 * n]
        token = outs[-1]
        me = _dev_index((lax.axis_index("x"), lax.axis_index("y"), lax.axis_index("c")))
        for w, (_, scatter) in enumerate(items):
            for k, peer in enumerate(_peers()):
                src = src_refs[w].at[_dev_index(peer)] if scatter else src_refs[w]
                pltpu.make_async_remote_copy(
                    src_ref=src, dst_ref=land_refs[w].at[me], send_sem=send_sems[w].at[k],
                    recv_sem=recv_sems[w].at[k], device_id=peer, device_id_type=MESH).start()
        token[...] = jnp.zeros_like(token)

    lands = [lax.empty((N_DEV,) + blk, a.dtype) for a, blk in zip(srcs, blocks)]
    res = pl.pallas_call(
        body, name=name,
        out_shape=([pltpu.SemaphoreType.DMA((N_PEERS,))] * (2 * n)
                   + [pltpu.HBM(a.shape, a.dtype) for a in srcs] + [pltpu.HBM(l.shape, l.dtype) for l in lands]
                   + [jax.ShapeDtypeStruct((8, LANES), F32)]),
        in_specs=[HBM] * (2 * n) + [ANY] * len(extra),
        out_specs=[SEM] * (2 * n) + [HBM] * (2 * n) + [pl.BlockSpec(memory_space=pltpu.VMEM)],
        input_output_aliases={i: 2 * n + i for i in range(2 * n)},
        compiler_params=pltpu.CompilerParams(has_side_effects=pltpu.SideEffectType.DATAFLOW_SIDE_EFFECTING),
    )(*[pltpu.with_memory_space_constraint(a, pltpu.HBM) for a in srcs + lands], *extra)
    handles = [(res[w], res[n + w], res[2 * n + w], res[3 * n + w], scatter) for w, (_, scatter) in enumerate(items)]
    return handles, res[-1]


def exchange_wait(handle, after, name):
    send_sem, recv_sem, src, land, scatter = handle

    def body(src_ref, land_ref, send_ref, recv_ref, after_ref, src_out, land_out):
        for k, peer in enumerate(_peers()):
            slot = _dev_index(peer)
            copy = pltpu.make_async_remote_copy(
                src_ref=src_ref.at[slot] if scatter else src_ref, dst_ref=land_ref.at[slot],
                send_sem=send_ref.at[k], recv_sem=recv_ref.at[k], device_id=peer, device_id_type=MESH)
            copy.wait_send()
            copy.wait_recv()

    src, landed = pl.pallas_call(
        body, name=name,
        out_shape=(pltpu.HBM(src.shape, src.dtype), pltpu.HBM(land.shape, land.dtype)),
        in_specs=[HBM, HBM, SEM, SEM, ANY], out_specs=(HBM, HBM), input_output_aliases={0: 0, 1: 1},
        compiler_params=pltpu.CompilerParams(has_side_effects=pltpu.SideEffectType.DATAFLOW_SIDE_EFFECTING),
    )(src, land, send_sem, recv_sem, after)
    dev = _dev_index((lax.axis_index("x"), lax.axis_index("y"), lax.axis_index("c")))
    own = lax.dynamic_index_in_dim(src, dev, axis=0, keepdims=True) if scatter else src[None]
    return lax.dynamic_update_slice_in_dim(landed, own, dev, axis=0)


def rows_call(fn, ins, outs, name):
    rows = ins[0].shape[1]
    per_row = sum(a.shape[0] * a.shape[2] * a.dtype.itemsize for a in ins)
    per_row += sum(l * c * jnp.dtype(dt).itemsize for l, c, dt in outs)
    tr = _divisor_tile(rows, max(16, TILE_BUDGET_BYTES // per_row), 16)
    n_in = len(ins)

    def body(*refs):
        vals = fn(*[r[...] for r in refs[:n_in]])
        if not isinstance(vals, (tuple, list)):
            vals = (vals,)
        for r, v in zip(refs[n_in:], vals):
            r[...] = v.astype(r.dtype)

    def spec(l, c):
        return pl.BlockSpec((l, tr, c), lambda i: (0, i, 0))

    res = pl.pallas_call(
        body, name=name, grid=(rows // tr,),
        in_specs=[spec(a.shape[0], a.shape[2]) for a in ins],
        out_specs=[spec(l, c) for l, c, _ in outs],
        out_shape=[jax.ShapeDtypeStruct((l, rows, c), dt) for l, c, dt in outs],
        compiler_params=_cparams("arbitrary"),
    )(*ins)
    return res


def _as_rows(a, lead=0):
    shape = a.shape
    l = int(math.prod(shape[:lead])) if lead else 1
    rest = shape[lead:]
    c = rest[-1] if rest else 1
    r = int(math.prod(rest[:-1])) if len(rest) > 1 else 1
    return a.reshape(l, r, c)


def act_call(fn, ins, outs, name):
    bsz, seq = ins[0].shape[0], ins[0].shape[1]
    per_row = sum(a.shape[2] * a.dtype.itemsize for a in ins if a.shape[1] == seq)
    per_row += sum(c * jnp.dtype(dt).itemsize for c, dt, kind in outs if kind == "tile")
    ts = _divisor_tile(seq, max(16, TILE_BUDGET_BYTES // per_row), 16)
    n_in = len(ins)

    def in_spec(a):
        c = a.shape[2]
        if a.shape[1] == seq:
            return pl.BlockSpec((None, ts, c), lambda b, s: (b, s, 0))
        if a.shape[0] == bsz:
            return pl.BlockSpec((None, 1, c), lambda b, s: (b, 0, 0))
        return pl.BlockSpec((None, 1, c), lambda b, s: (0, 0, 0))

    def out_spec(c, kind):
        if kind == "tile":
            return pl.BlockSpec((None, ts, c), lambda b, s: (b, s, 0))
        if kind == "seq":
            return pl.BlockSpec((None, 1, c), lambda b, s: (b, 0, 0))
        return pl.BlockSpec((None, 1, c), lambda b, s: (0, 0, 0))

    def out_shape(c, dt, kind):
        if kind == "tile":
            return jax.ShapeDtypeStruct((bsz, seq, c), dt)
        return jax.ShapeDtypeStruct((bsz if kind == "seq" else 1, 1, c), dt)

    def accumulate(ref, v, first):
        @pl.when(first)
        def _():
            ref[...] = jnp.zeros_like(ref)

        ref[...] += v.astype(ref.dtype)

    def body(*refs):
        b, s = pl.program_id(0), pl.program_id(1)
        vals = fn(*[r[...] for r in refs[:n_in]])
        if not isinstance(vals, (tuple, list)):
            vals = (vals,)
        for ref, v, (_, _, kind) in zip(refs[n_in:], vals, outs):
            if kind == "tile":
                ref[...] = v.astype(ref.dtype)
            elif kind == "seq":
                accumulate(ref, v, s == 0)
            else:
                accumulate(ref, v, jnp.logical_and(b == 0, s == 0))

    return pl.pallas_call(
        body, name=name, grid=(bsz, seq // ts),
        in_specs=[in_spec(a) for a in ins],
        out_specs=[out_spec(c, kind) for c, _, kind in outs],
        out_shape=[out_shape(*o) for o in outs],
        compiler_params=_cparams("arbitrary", "arbitrary"),
    )(*ins)


def _mm(name, grid, a, a_spec, b, b_spec, dims, out_shape, out_spec, out_dtypes, acc_steps,
        epi=None, extras=(), extra_spec=None):
    n_ex, n_out = len(extras), len(out_dtypes)
    tile = tuple(d for d in out_spec.block_shape if d is not None)

    def body(*refs):
        a_ref, b_ref = refs[0], refs[1]
        ex_refs = refs[2:2 + n_ex]
        o_refs = refs[2 + n_ex:2 + n_ex + n_out]
        p = lax.dot_general(a_ref[...].astype(BF16), b_ref[...].astype(BF16), (dims, ((), ())),
                            preferred_element_type=F32)

        def finish(acc):
            vals = epi(acc, *[r[...] for r in ex_refs]) if epi is not None else (acc,) * n_out
            for r, v in zip(o_refs, vals):
                r[...] = v.astype(r.dtype)

        if not acc_steps:
            finish(p)
        else:
            acc_ref = refs[-1]
            s = pl.program_id(1)

            @pl.when(s == 0)
            def _():
                acc_ref[...] = p

            @pl.when(s > 0)
            def _():
                acc_ref[...] += p

            @pl.when(s == acc_steps - 1)
            def _():
                finish(acc_ref[...])

    res = pl.pallas_call(
        body, name=name, grid=grid,
        in_specs=[a_spec, b_spec] + [extra_spec] * n_ex,
        out_specs=[out_spec] * n_out,
        out_shape=[jax.ShapeDtypeStruct(out_shape, dt) for dt in out_dtypes],
        scratch_shapes=[pltpu.VMEM(tile, F32)] if acc_steps else [],
        compiler_params=_cparams("arbitrary", "arbitrary"),
    )(a, b, *extras)
    return res if n_out > 1 else res[0]


def mm_nn_col(a, w, name, out_dtypes=(F32,), epi=None):
    m, k = a.shape
    ns, _, nb = w.shape
    tm = _tile_m(m)
    return _mm(name, (m // tm, ns), a, pl.BlockSpec((tm, k), lambda i, j: (i, 0)),
               w, pl.BlockSpec((None, k, nb), lambda i, j: (j, 0, 0)), ((1,), (0,)),
               (m, ns * nb), pl.BlockSpec((tm, nb), lambda i, j: (i, j)), out_dtypes, 0, epi)


def mm_nn_row(a, w, name, out_dtypes=(F32,)):
    m = a.shape[0]
    ns, kb, n = w.shape
    tm = _tile_m(m)
    return _mm(name, (m // tm, ns), a, pl.BlockSpec((tm, kb), lambda i, s: (i, s)),
               w, pl.BlockSpec((None, kb, n), lambda i, s: (s, 0, 0)), ((1,), (0,)),
               (m, n), pl.BlockSpec((tm, n), lambda i, s: (i, 0)), out_dtypes, ns)


def mm_nt_col(dc, w, name, out_dtypes=(F32,), epi=None, extras=()):
    m = dc.shape[0]
    ns, k, nb = w.shape
    tm = _tile_m(m) // 2 if extras else _tile_m(m)
    spec = pl.BlockSpec((tm, k), lambda i, s: (i, 0))
    return _mm(name, (m // tm, ns), dc, pl.BlockSpec((tm, nb), lambda i, s: (i, s)),
               w, pl.BlockSpec((None, k, nb), lambda i, s: (s, 0, 0)), ((1,), (1,)),
               (m, k), spec, out_dtypes, ns, epi, extras, spec)


def mm_nt_row(dc, w, name, out_dtypes=(F32,), epi=None, extras=()):
    m, n = dc.shape
    ns, kb, _ = w.shape
    tm = _tile_m(m)
    spec = pl.BlockSpec((tm, kb), lambda i, s: (i, s))
    return _mm(name, (m // tm, ns), dc, pl.BlockSpec((tm, n), lambda i, s: (i, 0)),
               w, pl.BlockSpec((None, kb, n), lambda i, s: (s, 0, 0)), ((1,), (1,)),
               (m, ns * kb), spec, out_dtypes, 0, epi, extras, spec)


def mm_tn(a, c, slab, ns, name, out_dtype=BF16):
    m, ka_all = a.shape
    nc_all = c.shape[1]
    ka = ka_all // ns if slab == "a" else ka_all
    nc = nc_all // ns if slab == "c" else nc_all
    tt = _divisor_tile(m, 256, 16)
    steps = m // tt

    def body(a_ref, c_ref, o_ref, acc_ref):
        t = pl.program_id(0)

        @pl.when(t == 0)
        def _():
            acc_ref[...] = jnp.zeros_like(acc_ref)

        for s in range(ns):
            a_s = a_ref[:, s * ka:(s + 1) * ka] if slab == "a" else a_ref[...]
            c_s = c_ref[:, s * nc:(s + 1) * nc] if slab == "c" else c_ref[...]
            acc_ref[s] += lax.dot_general(a_s.astype(BF16), c_s.astype(BF16), (((0,), (0,)), ((), ())),
                                          preferred_element_type=F32)

        @pl.when(t == steps - 1)
        def _():
            o_ref[...] = acc_ref[...].astype(o_ref.dtype)

    return pl.pallas_call(
        body, name=name, grid=(steps,),
        in_specs=[pl.BlockSpec((tt, ka_all), lambda t: (t, 0)), pl.BlockSpec((tt, nc_all), lambda t: (t, 0))],
        out_specs=pl.BlockSpec((ns, ka, nc), lambda t: (0, 0, 0)),
        out_shape=jax.ShapeDtypeStruct((ns, ka, nc), out_dtype),
        scratch_shapes=[pltpu.VMEM((ns, ka, nc), F32)],
        compiler_params=_cparams("arbitrary"),
    )(a, c)


def slab_sum(landed, name):
    shape = landed.shape[1:]
    total = rows_call(lambda g: jnp.sum(g.astype(F32), axis=0, keepdims=True),
                      [_as_rows(landed, 1)], [(1, shape[-1], F32)], name)[0]
    return total.reshape(shape)


def all_reduce_small(leaves, name):
    sizes = [int(a.size) for a in leaves]
    flat = jnp.concatenate([a.reshape(-1) for a in leaves])
    total = int(flat.size)
    padded = -(-total // (16 * LANES)) * (16 * LANES)
    flat = jnp.pad(flat, (0, padded - total)).reshape(padded // LANES, LANES)
    gathered = all_gather(flat, name + "_gather")
    summed = rows_call(lambda g: jnp.sum(g, axis=0, keepdims=True), [gathered],
                       [(1, LANES, F32)], name + "_sum")[0].reshape(-1)
    out, at = [], 0
    for a, n in zip(leaves, sizes):
        out.append(summed[at:at + n].reshape(a.shape))
        at += n
    return out


def adamw(w, g, m, v, name):
    c = w.shape[-1] if w.ndim else 1

    def fn(w_, g_, m_, v_):
        nm = ADAM_B1 * m_ + (1.0 - ADAM_B1) * g_
        nv = ADAM_B2 * v_ + (1.0 - ADAM_B2) * (g_ * g_)
        m_hat = nm / (1.0 - ADAM_B1 ** ADAM_STEP)
        v_hat = nv / (1.0 - ADAM_B2 ** ADAM_STEP)
        delta = -ADAM_LR * (m_hat / (jnp.sqrt(v_hat) + ADAM_EPS) + ADAM_WD * w_)
        return delta, nm, nv

    res = rows_call(fn, [_as_rows(t) for t in (w, g.astype(F32), m, v)], [(1, c, F32)] * 3, name)
    return tuple(r.reshape(w.shape) for r in res)


def _rowsum(v):
    return jnp.sum(v, axis=0, keepdims=True)


def _norm_mod(x, g, sh, sc):
    n = x * lax.rsqrt(jnp.mean(x * x, axis=-1, keepdims=True) + EPS)
    return (n * g) * (1.0 + sc) + sh


def _norm_mod_bwd(x, g, sc, dh, dres):
    r = lax.rsqrt(jnp.mean(x * x, axis=-1, keepdims=True) + EPS)
    n = x * r
    dy = dh * (1.0 + sc)
    dn = dy * g
    dx = r * (dn - n * jnp.mean(dn * n, axis=-1, keepdims=True))
    return dres + dx, _rowsum(dh), _rowsum(dh * (n * g)), _rowsum(dy * n)


def _head_mean(v):
    low = lax.broadcasted_iota(jnp.int32, (1, LANES), 1) < HEAD_DIM
    parts = []
    for p in range(v.shape[1] // LANES):
        blk = v[:, p * LANES:(p + 1) * LANES]
        s0 = jnp.sum(jnp.where(low, blk, 0.0), axis=-1, keepdims=True)
        s1 = jnp.sum(jnp.where(low, 0.0, blk), axis=-1, keepdims=True)
        parts.append(jnp.where(low, s0, s1))
    return jnp.concatenate(parts, axis=1) * (1.0 / HEAD_DIM)


def _head_norm(x, g):
    return x * lax.rsqrt(_head_mean(x * x) + EPS) * g


def _head_norm_bwd(x, g, dy):
    r = lax.rsqrt(_head_mean(x * x) + EPS)
    n = x * r
    dn = dy * g
    return r * (dn - n * _head_mean(dn * n)), _rowsum(dy * n)


GELU_C = math.sqrt(2.0 / math.pi)
GELU_A = 0.044715


def _gelu_grad(y):
    t = jnp.tanh(GELU_C * (y + GELU_A * y * y * y))
    return 0.5 * (1.0 + t) + 0.5 * y * (1.0 - t * t) * GELU_C * (1.0 + 3.0 * GELU_A * y * y)


def ada_fwd(c_all, w_cols, b_cols):
    def body(c_ref, w_ref, b_ref, o_ref):
        c = c_ref[...]
        s = (c * jax.nn.sigmoid(c)).astype(BF16)
        o_ref[...] = jnp.dot(s, w_ref[...].astype(BF16), preferred_element_type=F32) + b_ref[...]

    return pl.pallas_call(
        body, name="ada_fwd", out_shape=jax.ShapeDtypeStruct((c_all.shape[0], w_cols.shape[1]), F32),
        compiler_params=pltpu.CompilerParams(vmem_limit_bytes=VMEM_LIMIT_BYTES),
    )(c_all, w_cols, b_cols)


def ada_bwd(c_all, dm_cols, dm_all):
    def body(c_ref, d_ref, all_ref, dw_ref, db_ref):
        c = c_ref[...]
        s = (c * jax.nn.sigmoid(c)).astype(BF16)
        dw_ref[...] = lax.dot_general(s, d_ref[...].astype(BF16), (((0,), (0,)), ((), ())),
                                      preferred_element_type=F32)
        db_ref[...] = jnp.sum(all_ref[...], axis=0, keepdims=True)

    return pl.pallas_call(
        body, name="ada_bwd",
        out_shape=[jax.ShapeDtypeStruct((c_all.shape[1], dm_cols.shape[1]), F32),
                   jax.ShapeDtypeStruct((1, dm_all.shape[1]), F32)],
        compiler_params=pltpu.CompilerParams(vmem_limit_bytes=VMEM_LIMIT_BYTES),
    )(c_all, dm_cols, dm_all)


def _s5_discretise(lam_re, lam_im, log_dt, b_re, b_im):
    dt = jnp.exp(log_dt)
    mag = jnp.exp(lam_re * dt)
    ab_re = mag * jnp.cos(lam_im * dt)
    ab_im = mag * jnp.sin(lam_im * dt)
    den = lam_re * lam_re + lam_im * lam_im
    nr = ab_re - 1.0
    ni = ab_im
    f_re = (nr * lam_re + ni * lam_im) / den
    f_im = (ni * lam_re - nr * lam_im) / den
    bb_re = f_re * b_re - f_im * b_im
    bb_im = f_re * b_im + f_im * b_re
    return ab_re, ab_im, bb_re, bb_im


def s5_prep(lam_re, lam_im, log_dt, b_re, b_im):
    gp, h = b_re.shape

    def body(lr, li, ld, br, bi, o_ar, o_ai, o_br, o_bi):
        res = _s5_discretise(lr[...], li[...], ld[...], br[...], bi[...])
        for r, v in zip((o_ar, o_ai, o_br, o_bi), res):
            r[...] = v

    col, mat = jax.ShapeDtypeStruct((gp, 1), F32), jax.ShapeDtypeStruct((gp, h), F32)
    return pl.pallas_call(body, name="s5_prep", out_shape=[col, col, mat, mat])(lam_re, lam_im, log_dt, b_re, b_im)


def s5_prep_bwd(lam_re, lam_im, log_dt, b_re, b_im, d_ab_re, d_ab_im, d_bb_re, d_bb_im):
    gp, h = b_re.shape

    def body(lr, li, ld, br, bi, g_ar, g_ai, g_br, g_bi, o_lr, o_li, o_ld, o_br, o_bi):
        _, vjp = jax.vjp(_s5_discretise, lr[...], li[...], ld[...], br[...], bi[...])
        res = vjp((g_ar[...], g_ai[...], g_br[...], g_bi[...]))
        for r, v in zip((o_lr, o_li, o_ld, o_br, o_bi), res):
            r[...] = v

    col, mat = jax.ShapeDtypeStruct((gp, 1), F32), jax.ShapeDtypeStruct((gp, h), F32)
    return pl.pallas_call(body, name="s5_prep_bwd", out_shape=[col, col, col, mat, mat])(
        lam_re, lam_im, log_dt, b_re, b_im, d_ab_re, d_ab_im, d_bb_re, d_bb_im)


def _s5_chunk(seq):
    return _divisor_tile(seq, 256, 16)


def s5_fwd(u, bbd_re, bbd_im, cbd_re, cbd_im, ab_re, ab_im, dskip):
    bsz, seq, d = u.shape
    nb, cb, ns = bbd_re.shape
    lc = _s5_chunk(seq)

    def body(u_ref, bre_ref, bim_ref, cre_ref, cim_ref, ar_ref, ai_ref, d_ref, y_ref, sre_ref, sim_ref,
             carry_re, carry_im):
        t = pl.program_id(1)

        @pl.when(t == 0)
        def _():
            carry_re[...] = jnp.zeros_like(carry_re)
            carry_im[...] = jnp.zeros_like(carry_im)

        for b in range(bsz):
            ub = u_ref[b].astype(BF16)
            sre_ref[b] = jnp.dot(ub, bre_ref[...], preferred_element_type=F32)
            sim_ref[b] = jnp.dot(ub, bim_ref[...], preferred_element_type=F32)
        ar, ai = ar_ref[...], ai_ref[...]

        def step(i, carry):
            row = pl.ds(i, 1)
            out = []
            for b, (cr, ci) in enumerate(carry):
                nr = ar * cr - ai * ci + sre_ref[b, row, :]
                ni = ar * ci + ai * cr + sim_ref[b, row, :]
                sre_ref[b, row, :] = nr
                sim_ref[b, row, :] = ni
                out.append((nr, ni))
            return tuple(out)

        init = tuple((carry_re[b], carry_im[b]) for b in range(bsz))
        last = lax.fori_loop(0, lc, step, init, unroll=8)
        for b, (cr, ci) in enumerate(last):
            carry_re[b] = cr
            carry_im[b] = ci
            y = jnp.dot(sre_ref[b].astype(BF16), cre_ref[...], preferred_element_type=F32)
            y -= jnp.dot(sim_ref[b].astype(BF16), cim_ref[...], preferred_element_type=F32)
            y_ref[b] = y + d_ref[...] * u_ref[b]

    chan = pl.BlockSpec((bsz, lc, cb), lambda n, t: (0, t, n))
    state = pl.BlockSpec((bsz, lc, ns), lambda n, t: (0, t, n))
    par = lambda r, c: pl.BlockSpec((None, r, c), lambda n, t: (n, 0, 0))
    return pl.pallas_call(
        body, name="s5_fwd", grid=(nb, seq // lc),
        in_specs=[chan, par(cb, ns), par(cb, ns), par(ns, cb), par(ns, cb), par(1, ns), par(1, ns),
                  pl.BlockSpec((None, 1, cb), lambda n, t: (0, 0, n))],
        out_specs=[chan, state, state],
        out_shape=[jax.ShapeDtypeStruct((bsz, seq, d), F32),
                   jax.ShapeDtypeStruct((bsz, seq, nb * ns), F32),
                   jax.ShapeDtypeStruct((bsz, seq, nb * ns), F32)],
        scratch_shapes=[pltpu.VMEM((bsz, 1, ns), F32), pltpu.VMEM((bsz, 1, ns), F32)],
        compiler_params=_cparams("arbitrary", "arbitrary"),
    )(u, bbd_re, bbd_im, cbd_re, cbd_im, ab_re, ab_im, dskip)


def s5_bwd(dy, u, st_re, st_im, bbd_re, bbd_im, cbd_re, cbd_im, ab_re, ab_im, dskip):
    bsz, seq, d = u.shape
    nb, cb, ns = bbd_re.shape
    lc = _s5_chunk(seq)
    nc = seq // lc

    def body(dy_ref, u_ref, sre_ref, sim_ref, bre_ref, bim_ref, cre_ref, cim_ref, ar_ref, ai_ref, d_ref,
             du_ref, dbre_ref, dbim_ref, dcre_ref, dcim_ref, dar_ref, dai_ref, dd_ref,
             g_re, g_im, gs_re, gs_im, carry_re, carry_im):
        t = pl.program_id(1)

        @pl.when(t == 0)
        def _():
            for r in (dbre_ref, dbim_ref, dcre_ref, dcim_ref, dar_ref, dai_ref, dd_ref, carry_re, carry_im):
                r[...] = jnp.zeros_like(r)

        nt = (((1,), (1,)), ((), ()))
        tn = (((0,), (0,)), ((), ()))
        for b in range(bsz):
            dyb = dy_ref[b].astype(BF16)
            g_re[b] = lax.dot_general(dyb, cre_ref[...], nt, preferred_element_type=F32)
            g_im[b] = -lax.dot_general(dyb, cim_ref[...], nt, preferred_element_type=F32)
        ar, ai = ar_ref[...], ai_ref[...]

        def step(k, carry):
            row = pl.ds(lc - 1 - k, 1)
            out = []
            for b, (cr, ci) in enumerate(carry):
                gs_re[b, row, :] = cr
                gs_im[b, row, :] = ci
                nr = ar * cr + ai * ci + g_re[b, row, :]
                ni = ar * ci - ai * cr + g_im[b, row, :]
                g_re[b, row, :] = nr
                g_im[b, row, :] = ni
                out.append((nr, ni))
            return tuple(out)

        init = tuple((carry_re[b], carry_im[b]) for b in range(bsz))
        last = lax.fori_loop(0, lc, step, init, unroll=8)
        for b, (cr, ci) in enumerate(last):
            carry_re[b] = cr
            carry_im[b] = ci
            dyf, uf = dy_ref[b], u_ref[b]
            dyb, ub = dyf.astype(BF16), uf.astype(BF16)
            sr, si = sre_ref[b], sim_ref[b]
            hr, hi = gs_re[b], gs_im[b]
            dar_ref[...] += _rowsum(hr * sr + hi * si)
            dai_ref[...] += _rowsum(hi * sr - hr * si)
            gr, gi = g_re[b].astype(BF16), g_im[b].astype(BF16)
            du = lax.dot_general(gr, bre_ref[...], nt, preferred_element_type=F32)
            du += lax.dot_general(gi, bim_ref[...], nt, preferred_element_type=F32)
            du_ref[b] = du + d_ref[...] * dyf
            dbre_ref[...] += lax.dot_general(ub, gr, tn, preferred_element_type=F32)
            dbim_ref[...] += lax.dot_general(ub, gi, tn, preferred_element_type=F32)
            dcre_ref[...] += lax.dot_general(sr.astype(BF16), dyb, tn, preferred_element_type=F32)
            dcim_ref[...] -= lax.dot_general(si.astype(BF16), dyb, tn, preferred_element_type=F32)
            dd_ref[...] += _rowsum(dyf * uf)

    chan = pl.BlockSpec((bsz, lc, cb), lambda n, t: (0, nc - 1 - t, n))
    state = pl.BlockSpec((bsz, lc, ns), lambda n, t: (0, nc - 1 - t, n))
    par = lambda r, c: pl.BlockSpec((None, r, c), lambda n, t: (n, 0, 0))
    return pl.pallas_call(
        body, name="s5_bwd", grid=(nb, nc),
        in_specs=[chan, chan, state, state, par(cb, ns), par(cb, ns), par(ns, cb), par(ns, cb),
                  par(1, ns), par(1, ns), pl.BlockSpec((None, 1, cb), lambda n, t: (0, 0, n))],
        out_specs=[chan, par(cb, ns), par(cb, ns), par(ns, cb), par(ns, cb), par(1, ns), par(1, ns), par(1, cb)],
        out_shape=[jax.ShapeDtypeStruct((bsz, seq, d), F32),
                   jax.ShapeDtypeStruct((nb, cb, ns), F32), jax.ShapeDtypeStruct((nb, cb, ns), F32),
                   jax.ShapeDtypeStruct((nb, ns, cb), F32), jax.ShapeDtypeStruct((nb, ns, cb), F32),
                   jax.ShapeDtypeStruct((nb, 1, ns), F32), jax.ShapeDtypeStruct((nb, 1, ns), F32),
                   jax.ShapeDtypeStruct((nb, 1, cb), F32)],
        scratch_shapes=[pltpu.VMEM((bsz, lc, ns), F32)] * 4 + [pltpu.VMEM((bsz, 1, ns), F32)] * 2,
        compiler_params=_cparams("arbitrary", "arbitrary"),
    )(dy, u, st_re, st_im, bbd_re, bbd_im, cbd_re, cbd_im, ab_re, ab_im, dskip)


ATT_HEADS = 4
ATT_LANES = ATT_HEADS * HEAD_DIM
ATT_KEYS = 2 * ATT_BLOCK
ATT_SCALE = 1.0 / math.sqrt(HEAD_DIM)
_NT = (((1,), (1,)), ((), ()))
_TN = (((0,), (0,)), ((), ()))
_HEADS = [slice(h * HEAD_DIM, (h + 1) * HEAD_DIM) for h in range(ATT_HEADS)]
_HALF = [slice(0, ATT_BLOCK), slice(ATT_BLOCK, ATT_KEYS)]


def _log_sigmoids(z):
    sp = jnp.log(1.0 + jnp.exp(-jnp.abs(z)))
    ls = jnp.minimum(z, 0.0) - sp
    return ls, ls - z


def _sum_matrix(after, inclusive):
    j = lax.broadcasted_iota(jnp.int32, (ATT_KEYS, ATT_KEYS), 0) % ATT_BLOCK
    s = lax.broadcasted_iota(jnp.int32, (ATT_KEYS, ATT_KEYS), 1)
    if after:
        hit = (j >= s) if inclusive else (j > s)
    else:
        hit = (j <= s) if inclusive else (j < s)
    return jnp.where(jnp.logical_or(hit, s >= ATT_BLOCK), 1.0, 0.0).astype(BF16)


def _hi_lo(v):
    hi = v.astype(BF16)
    lo = (v - hi.astype(F32)).astype(BF16)
    return jnp.concatenate([hi, lo], axis=1)


def _strict_mask(i, j):
    t = i * ATT_BLOCK + lax.broadcasted_iota(jnp.int32, (ATT_BLOCK, ATT_KEYS), 0)
    s = j * ATT_KEYS + lax.broadcasted_iota(jnp.int32, (ATT_BLOCK, ATT_KEYS), 1)
    return s < t


def attention_fwd(q, k, v):
    bsz, seq, d = q.shape

    def body(q_ref, k_ref, v_ref, o_ref, tot_ref, z_buf, ls_buf, cs_buf, acc_buf, run_buf):
        i = pl.program_id(2)
        jd = i // 2
        sums = _sum_matrix(True, False)
        acc_buf[...] = jnp.zeros_like(acc_buf)
        run_buf[...] = jnp.zeros_like(run_buf)

        def block(j, masked):
            rows = pl.ds(pl.multiple_of(j * ATT_KEYS, ATT_KEYS), ATT_KEYS)
            strict = _strict_mask(i, j) if masked else None
            for h, ln in enumerate(_HEADS):
                z_buf[h] = lax.dot_general(q_ref[:, ln], k_ref[rows, ln], _NT, preferred_element_type=F32)
            for h in range(ATT_HEADS):
                for half, cols in enumerate(_HALF):
                    ls, lf = _log_sigmoids(z_buf[h, :, cols])
                    if masked:
                        lf = jnp.where(strict[:, cols], lf, 0.0)
                    ls_buf[h, :, cols] = ls
                    cs_buf[h, half] = jnp.dot(_hi_lo(lf), sums, preferred_element_type=F32)
            for h, ln in enumerate(_HEADS):
                run = run_buf[h]
                late, early = cs_buf[h, 1], cs_buf[h, 0]
                a1 = run + late[:, _HALF[0]]
                run = run + late[:, _HALF[1]]
                a0 = run + early[:, _HALF[0]]
                run_buf[h] = run + early[:, _HALF[1]]
                w = jnp.exp(ls_buf[h] + jnp.concatenate([a0, a1], axis=1))
                if masked:
                    w = jnp.where(strict, w, 0.0)
                acc_buf[h] += jnp.dot(w.astype(BF16), v_ref[rows, ln], preferred_element_type=F32)

        block(jd, True)

        def step(it, carry):
            block(jd - 1 - it, False)
            return carry

        lax.fori_loop(0, jd, step, 0)
        o_ref[...] = jnp.concatenate([acc_buf[h] for h in range(ATT_HEADS)], axis=1).astype(o_ref.dtype)
        tot_ref[...] = jnp.concatenate([run_buf[h, :, :HEAD_DIM] for h in range(ATT_HEADS)], axis=1)

    blk = pl.BlockSpec((None, ATT_BLOCK, ATT_LANES), lambda b, p, i: (b, i, p))
    full = pl.BlockSpec((None, seq, ATT_LANES), lambda b, p, i: (b, 0, p))
    tile = (ATT_HEADS, ATT_BLOCK, ATT_KEYS)
    return pl.pallas_call(
        body, name="attention_fwd", grid=(bsz, d // ATT_LANES, seq // ATT_BLOCK),
        in_specs=[blk, full, full], out_specs=[blk, blk],
        out_shape=[jax.ShapeDtypeStruct((bsz, seq, d), BF16), jax.ShapeDtypeStruct((bsz, seq, d), F32)],
        scratch_shapes=[pltpu.VMEM(tile, F32), pltpu.VMEM(tile, F32),
                        pltpu.VMEM((ATT_HEADS, 2, ATT_BLOCK, ATT_KEYS), F32),
                        pltpu.VMEM((ATT_HEADS, ATT_BLOCK, HEAD_DIM), F32),
                        pltpu.VMEM((ATT_HEADS, ATT_BLOCK, ATT_BLOCK), F32)],
        compiler_params=_cparams("arbitrary", "arbitrary", "arbitrary"),
    )(q, k, v)


def attention_bwd(q, k, v, tot, do):
    bsz, seq, d = q.shape

    def body(q_ref, k_ref, v_ref, tot_ref, do_ref, dq_ref, dk_ref, dv_ref,
             z_buf, dw_buf, ls_buf, e_buf, up_buf, bf_buf, w_buf, do_buf, dq_buf, tot_buf, run_buf, erun_buf):
        i = pl.program_id(2)
        jd = i // 2

        @pl.when(i == 0)
        def _():
            dk_ref[...] = jnp.zeros_like(dk_ref)
            dv_ref[...] = jnp.zeros_like(dv_ref)

        upto_incl, upto_excl = _sum_matrix(False, True), _sum_matrix(False, False)
        do_buf[...] = do_ref[...].astype(BF16)
        for h, ln in enumerate(_HEADS):
            tot_buf[h] = jnp.concatenate([tot_ref[:, ln], tot_ref[:, ln]], axis=1)
        dq_buf[...] = jnp.zeros_like(dq_buf)
        run_buf[...] = jnp.zeros_like(run_buf)
        erun_buf[...] = jnp.zeros_like(erun_buf)

        def block(j, masked):
            rows = pl.ds(pl.multiple_of(j * ATT_KEYS, ATT_KEYS), ATT_KEYS)
            strict = _strict_mask(i, j) if masked else None
            for h, ln in enumerate(_HEADS):
                z_buf[h] = lax.dot_general(q_ref[:, ln], k_ref[rows, ln], _NT, preferred_element_type=F32)
                dw_buf[h] = lax.dot_general(do_buf[:, ln], v_ref[rows, ln], _NT, preferred_element_type=F32)
            for h in range(ATT_HEADS):
                for half, cols in enumerate(_HALF):
                    ls, lf = _log_sigmoids(z_buf[h, :, cols])
                    if masked:
                        lf = jnp.where(strict[:, cols], lf, 0.0)
                    ls_buf[h, :, cols] = ls
                    up_buf[h, half] = jnp.dot(_hi_lo(lf), upto_incl, preferred_element_type=F32)
            for h in range(ATT_HEADS):
                run = run_buf[h]
                early, late = up_buf[h, 0], up_buf[h, 1]
                u0 = run + early[:, _HALF[0]]
                run = run + early[:, _HALF[1]]
                u1 = run + late[:, _HALF[0]]
                run_buf[h] = run + late[:, _HALF[1]]
                tot_h = tot_buf[h]
                after = jnp.concatenate([tot_h - u0, tot_h - u1], axis=1)
                w = jnp.exp(ls_buf[h] + after)
                if masked:
                    w = jnp.where(strict, w, 0.0)
                w_buf[h] = w.astype(BF16)
                e = dw_buf[h] * w
                e_buf[h] = e
                for half, cols in enumerate(_HALF):
                    bf_buf[h, half] = jnp.dot(_hi_lo(e[:, cols]), upto_excl, preferred_element_type=F32)
            dks, dvs = [], []
            for h, ln in enumerate(_HEADS):
                erun = erun_buf[h]
                early, late = bf_buf[h, 0], bf_buf[h, 1]
                b0 = erun + early[:, _HALF[0]]
                erun = erun + early[:, _HALF[1]]
                b1 = erun + late[:, _HALF[0]]
                erun_buf[h] = erun + late[:, _HALF[1]]
                e = e_buf[h]
                dz = e - jnp.exp(ls_buf[h]) * (e + jnp.concatenate([b0, b1], axis=1))
                if masked:
                    dz = jnp.where(strict, dz, 0.0)
                dz = dz.astype(BF16)
                dq_buf[h] += jnp.dot(dz, k_ref[rows, ln], preferred_element_type=F32)
                dks.append(lax.dot_general(dz, q_ref[:, ln], _TN, preferred_element_type=F32))
                dvs.append(lax.dot_general(w_buf[h], do_buf[:, ln], _TN, preferred_element_type=F32))
            dk_ref[rows, :] += jnp.concatenate(dks, axis=1)
            dv_ref[rows, :] += jnp.concatenate(dvs, axis=1)

        def step(j, carry):
            block(j, False)
            return carry

        lax.fori_loop(0, jd, step, 0)
        block(jd, True)
        dq_ref[...] = jnp.concatenate([dq_buf[h] for h in range(ATT_HEADS)], axis=1) * ATT_SCALE

    blk = pl.BlockSpec((None, ATT_BLOCK, ATT_LANES), lambda b, p, i: (b, i, p))
    full = pl.BlockSpec((None, seq, ATT_LANES), lambda b, p, i: (b, 0, p))
    shape = jax.ShapeDtypeStruct((bsz, seq, d), F32)
    tile = (ATT_HEADS, ATT_BLOCK, ATT_KEYS)
    pair = (ATT_HEADS, 2, ATT_BLOCK, ATT_KEYS)
    square = (ATT_HEADS, ATT_BLOCK, ATT_BLOCK)
    return pl.pallas_call(
        body, name="attention_bwd", grid=(bsz, d // ATT_LANES, seq // ATT_BLOCK),
        in_specs=[blk, full, full, blk, blk], out_specs=[blk, full, full], out_shape=[shape, shape, shape],
        scratch_shapes=[pltpu.VMEM(tile, F32), pltpu.VMEM(tile, F32), pltpu.VMEM(tile, F32), pltpu.VMEM(tile, F32),
                        pltpu.VMEM(pair, F32), pltpu.VMEM(pair, F32), pltpu.VMEM(tile, BF16),
                        pltpu.VMEM((ATT_BLOCK, ATT_LANES), BF16), pltpu.VMEM((ATT_HEADS, ATT_BLOCK, HEAD_DIM), F32),
                        pltpu.VMEM(square, F32), pltpu.VMEM(square, F32), pltpu.VMEM(square, F32)],
        compiler_params=_cparams("arbitrary", "arbitrary", "arbitrary"),
    )(q, k, v, tot, do)


def mlp_fwd(x, g, sh, sc, gate, w1_handle, w2_handle, tag):
    bsz, seq, d = x.shape
    t = bsz * seq
    h = act_call(_norm_mod, [x, g, sh, sc], [(d, BF16, "tile")], tag + "_norm")[0]
    w1 = exchange_wait(w1_handle, h, tag + "_w1_wait")
    act = mm_nn_col(h.reshape(t, d), w1, tag + "_up", (BF16,),
                    lambda acc: (jnp.square(jnp.maximum(acc, 0.0)),))
    w2 = exchange_wait(w2_handle, act, tag + "_w2_wait")
    ff = mm_nn_row(act, w2, tag + "_down").reshape(bsz, seq, d)
    out = act_call(lambda x_, f_, g_: x_ + g_ * f_, [x, ff, gate], [(d, F32, "tile")], tag + "_res")[0]
    return out, (h, act, ff), w1, w2


def mlp_bwd(dout, x, g, sc, gate, w1, w2, saved, tag):
    bsz, seq, d = x.shape
    t = bsz * seq
    ns = w1.shape[0]
    h, act, ff = saved
    dff, dgate = act_call(lambda do_, f_, g_: (g_ * do_, _rowsum(do_ * f_)), [dout, ff, gate],
                          [(d, BF16, "tile"), (d, F32, "seq")], tag + "_dres")
    dff = dff.reshape(t, d)
    dpre = mm_nt_row(dff, w2, tag + "_dact", (BF16,),
                     lambda acc, a_: (acc * (2.0 * jnp.sqrt(a_.astype(F32))),), (act,))
    dw2 = mm_tn(act, dff, "a", ns, tag + "_dw2")
    dw1 = mm_tn(h.reshape(t, d), dpre, "c", ns, tag + "_dw1")
    (dw1_handle, dw2_handle), token = exchange_start([(dw1, True), (dw2, True)], tag + "_dw_start")
    dh = mm_nt_col(dpre, w1, tag + "_dh").reshape(bsz, seq, d)
    dx, dsh, dsc, dg = act_call(_norm_mod_bwd, [x, g + token[0, 0], sc, dh, dout],
                                [(d, F32, "tile"), (d, F32, "seq"), (d, F32, "seq"), (d, F32, "all")],
                                tag + "_dnorm")
    return dx, dw1_handle, dw2_handle, (dsh, dsc, dgate, dg)


def _block_diag(m, rows_first):
    nb, k, r, c = m.shape
    eye = jnp.eye(k, dtype=m.dtype)
    return jnp.einsum("nkrc,kl->nkrlc", m, eye).reshape(nb, k * r, k * c)


def _block_diag_part(m, r, c):
    nb = m.shape[0]
    k = m.shape[1] // r
    return jnp.einsum("nkrlc,kl->nkrc", m.reshape(nb, k, r, k, c), jnp.eye(k, dtype=m.dtype))


def kernel(x, c, ada_w, ada_b, mix_norm_g, mlp_norm_g, mlp_w1, mlp_w2, s5_a_re, s5_a_im, s5_log_dt, s5_b_re, s5_b_im, s5_c_re, s5_c_im, s5_d, s5_w_glu, kv_ada_w, kv_ada_b, kv_norm_g, w_kv, k_norm_g, sb_w_q, q_norm_g, sb_w_o, loss_target, m_ada_w, m_ada_b, m_mix_norm_g, m_mlp_norm_g, m_mlp_w1, m_mlp_w2, m_s5_a_re, m_s5_a_im, m_s5_log_dt, m_s5_b_re, m_s5_b_im, m_s5_c_re, m_s5_c_im, m_s5_d, m_s5_w_glu, m_kv_ada_w, m_kv_ada_b, m_kv_norm_g, m_w_kv, m_k_norm_g, m_sb_w_q, m_q_norm_g, m_sb_w_o, v_ada_w, v_ada_b, v_mix_norm_g, v_mlp_norm_g, v_mlp_w1, v_mlp_w2, v_s5_a_re, v_s5_a_im, v_s5_log_dt, v_s5_b_re, v_s5_b_im, v_s5_c_re, v_s5_c_im, v_s5_d, v_s5_w_glu, v_kv_ada_w, v_kv_ada_b, v_kv_norm_g, v_w_kv, v_k_norm_g, v_sb_w_q, v_q_norm_g, v_sb_w_o):
    bsz, seq, d = x.shape
    t = bsz * seq
    n_groups = d // S5_GROUP
    nb = n_groups // S5_BLOCK_GROUPS
    gp = n_groups * S5_STATE
    dev = 4 * lax.axis_index("x") + 2 * lax.axis_index("y") + lax.axis_index("c")
    e_ada, e_kv = 6 * d, 2 * d
    n_ada, n_kv = e_ada // N_DEV, e_kv // N_DEV

    d_skip = all_gather(s5_d, "gather_skip").reshape(1, 1, d)
    c_all = all_gather(c, "gather_c").reshape(N_DEV * bsz, d)

    w_cols = jnp.concatenate([ada_w[0], ada_w[1], kv_ada_w], axis=1)
    b_cols = jnp.concatenate([
        lax.dynamic_slice_in_dim(ada_b[0], dev * n_ada, n_ada),
        lax.dynamic_slice_in_dim(ada_b[1], dev * n_ada, n_ada),
        lax.dynamic_slice_in_dim(kv_ada_b, dev * n_kv, n_kv)])[None, :]
    mod_cols = ada_fwd(c_all, w_cols, b_cols)
    mod_all = all_gather(mod_cols, "gather_mod")
    mod_mine = lax.dynamic_slice_in_dim(mod_all, dev * bsz, bsz, axis=1)
    mod_mine = jnp.transpose(mod_mine, (1, 0, 2))
    mods = []
    for i in range(2):
        full = mod_mine[:, :, i * n_ada:(i + 1) * n_ada].reshape(bsz, e_ada)
        mods.append([full[:, None, j * d:(j + 1) * d] for j in range(6)])
    kv_full = mod_mine[:, :, 2 * n_ada:].reshape(bsz, e_kv)
    kv_sh, kv_sc = kv_full[:, None, :d], kv_full[:, None, d:]

    par = lambda p: p.reshape(1, 1, -1)

    shards = [s5_w_glu[0], mlp_w1[0], mlp_w2[0], w_kv, sb_w_q[0], sb_w_o[0], mlp_w1[1], mlp_w2[1]]
    gathers, gather_token = exchange_start([(w.astype(BF16), False) for w in shards], "gather_start", after=[mod_all, d_skip])
    glu_handle, w1_0_handle, w2_0_handle, wkv_handle, wq_handle, wo_handle, w1_1_handle, w2_1_handle = gathers
    started = gather_token[0, 0]

    sh_a, sc_a, g_a, sh_m, sc_m, g_m = mods[0]
    lam_re, lam_im = s5_a_re.reshape(gp, 1), s5_a_im.reshape(gp, 1)
    log_dt = jnp.broadcast_to(s5_log_dt.reshape(n_groups, 1), (n_groups, S5_STATE)).reshape(gp, 1)
    b_re, b_im = s5_b_re.reshape(gp, S5_GROUP), s5_b_im.reshape(gp, S5_GROUP)
    ab_re, ab_im, bb_re, bb_im = s5_prep(lam_re, lam_im, log_dt, b_re, b_im)
    to_bbd = lambda m: _block_diag(jnp.swapaxes(m.reshape(nb, S5_BLOCK_GROUPS, S5_STATE, S5_GROUP), 2, 3), True)
    to_cbd = lambda m: _block_diag(jnp.swapaxes(m.reshape(nb, S5_BLOCK_GROUPS, S5_GROUP, S5_STATE), 2, 3), True)
    bbd_re, bbd_im = to_bbd(bb_re).astype(BF16), to_bbd(bb_im).astype(BF16)
    cbd_re, cbd_im = to_cbd(s5_c_re[0]).astype(BF16), to_cbd(s5_c_im[0]).astype(BF16)
    abr, abi = ab_re.reshape(nb, 1, -1), ab_im.reshape(nb, 1, -1)

    h0 = act_call(_norm_mod, [x, par(mix_norm_g[0]) + started, sh_a, sc_a], [(d, F32, "tile")], "mix0_norm")[0]
    y, st_re, st_im = s5_fwd(h0, bbd_re, bbd_im, cbd_re, cbd_im, abr, abi, d_skip)
    ge = act_call(lambda y_: jax.nn.gelu(y_), [y], [(d, BF16, "tile")], "gelu")[0]
    w_glu = exchange_wait(glu_handle, ge, "glu_w_wait")
    z = mm_nn_col(ge.reshape(t, d), w_glu, "glu_up").reshape(bsz, seq, 2 * d)
    x1 = act_call(lambda x_, z_, g_: x_ + g_ * (z_[:, :d] * jax.nn.sigmoid(z_[:, d:])), [x, z, g_a],
                  [(d, F32, "tile")], "glu_res")[0]
    x2, mlp0_saved, w1_0, w2_0 = mlp_fwd(x1, par(mlp_norm_g[0]), sh_m, sc_m, g_m, w1_0_handle, w2_0_handle, "mlp0")

    sh_a1, sc_a1, g_a1, sh_m1, sc_m1, g_m1 = mods[1]
    kg = par(jnp.tile(k_norm_g, d // HEAD_DIM))
    qg = par(jnp.tile(q_norm_g[0], d // HEAD_DIM))
    hkv = act_call(_norm_mod, [x2, par(kv_norm_g), kv_sh, kv_sc], [(d, BF16, "tile")], "kv_norm")[0]
    wkv = exchange_wait(wkv_handle, hkv, "kv_w_wait")
    kvf = mm_nn_col(hkv.reshape(t, d), wkv, "kv_proj").reshape(bsz, seq, 2 * d)
    k_h, v_h = act_call(lambda kv_, g_: (_head_norm(kv_[:, :d], g_), kv_[:, d:]), [kvf, kg],
                        [(d, BF16, "tile"), (d, BF16, "tile")], "k_norm")
    h1 = act_call(_norm_mod, [x2, par(mix_norm_g[1]), sh_a1, sc_a1], [(d, BF16, "tile")], "mix1_norm")[0]
    wq = exchange_wait(wq_handle, h1, "q_w_wait")
    q_raw = mm_nn_row(h1.reshape(t, d), wq, "q_proj").reshape(bsz, seq, d)
    q_h = act_call(lambda x_, g_: _head_norm(x_, g_) * ATT_SCALE, [q_raw, qg], [(d, BF16, "tile")], "q_norm")[0]
    o, att_tot = attention_fwd(q_h, k_h, v_h)
    wo = exchange_wait(wo_handle, o, "o_w_wait")
    mix1 = mm_nn_row(o.reshape(t, d), wo, "o_proj").reshape(bsz, seq, d)
    x3 = act_call(lambda x_, f_, g_: x_ + g_ * f_, [x2, mix1, g_a1], [(d, F32, "tile")], "att_res")[0]
    x4, mlp1_saved, w1_1, w2_1 = mlp_fwd(x3, par(mlp_norm_g[1]), sh_m1, sc_m1, g_m1, w1_1_handle, w2_1_handle, "mlp1")

    def loss_fn(y_, t_):
        diff = y_ - t_
        part = jnp.sum(0.5 * jnp.mean(diff * diff, axis=-1, keepdims=True), axis=0, keepdims=True)
        return jnp.broadcast_to(part, (1, LANES)), diff * (1.0 / d)

    loss_part, dx4 = act_call(loss_fn, [x4, loss_target], [(LANES, F32, "all"), (d, F32, "tile")], "loss")
    loss = lax.psum(loss_part[0, 0, 0], ("x", "y", "c"))

    dx3, dw1_1, dw2_1, (dsh_m1, dsc_m1, dg_m1, dgn_mlp1) = mlp_bwd(
        dx4, x3, par(mlp_norm_g[1]), sc_m1, g_m1, w1_1, w2_1, mlp1_saved, "mlp1")
    dmix1, dg_a1 = act_call(lambda do_, f_, g_: (g_ * do_, _rowsum(do_ * f_)), [dx3, mix1, g_a1],
                            [(d, BF16, "tile"), (d, F32, "seq")], "att_dres")
    dmix1 = dmix1.reshape(t, d)
    do = mm_nt_row(dmix1, wo, "o_dproj").reshape(bsz, seq, d)
    dwo = mm_tn(o.reshape(t, d), dmix1, "a", N_DEV, "o_dw")
    dq, dk, dv = attention_bwd(q_h, k_h, v_h, att_tot, do)
    dq_raw, dqg = act_call(_head_norm_bwd, [q_raw, qg, dq], [(d, BF16, "tile"), (d, F32, "all")], "q_dnorm")
    dq_raw = dq_raw.reshape(t, d)
    dh1 = mm_nt_row(dq_raw, wq, "q_dproj").reshape(bsz, seq, d)
    dwq = mm_tn(h1.reshape(t, d), dq_raw, "a", N_DEV, "q_dw")
    dx2, dsh_a1, dsc_a1, dgn_mix1 = act_call(
        _norm_mod_bwd, [x2, par(mix_norm_g[1]), sc_a1, dh1, dx3],
        [(d, F32, "tile"), (d, F32, "seq"), (d, F32, "seq"), (d, F32, "all")], "mix1_dnorm")

    def kv_bwd_fn(kv_, g_, dk_, dv_):
        dk_raw, dg_ = _head_norm_bwd(kv_[:, :d], g_, dk_)
        return jnp.concatenate([dk_raw, dv_], axis=1), dg_

    dkvf, dkg = act_call(kv_bwd_fn, [kvf, kg, dk, dv], [(2 * d, BF16, "tile"), (d, F32, "all")], "k_dnorm")
    dkvf = dkvf.reshape(t, 2 * d)
    dhkv = mm_nt_col(dkvf, wkv, "kv_dproj").reshape(bsz, seq, d)
    dwkv = mm_tn(hkv.reshape(t, d), dkvf, "c", N_DEV, "kv_dw")
    (dwo, dwq, dwkv), att_token = exchange_start([(dwo, True), (dwq, True), (dwkv, True)], "att_dw_start")
    dx2, dkv_sh, dkv_sc, dgn_kv = act_call(
        _norm_mod_bwd, [x2, par(kv_norm_g) + att_token[0, 0], kv_sc, dhkv, dx2],
        [(d, F32, "tile"), (d, F32, "seq"), (d, F32, "seq"), (d, F32, "all")], "kv_dnorm")

    dx1, dw1_0, dw2_0, (dsh_m0, dsc_m0, dg_m0, dgn_mlp0) = mlp_bwd(
        dx2, x1, par(mlp_norm_g[0]), sc_m, g_m, w1_0, w2_0, mlp0_saved, "mlp0")

    def glu_bwd_fn(do_, z_, g_):
        val, sig = z_[:, :d], jax.nn.sigmoid(z_[:, d:])
        dmix = g_ * do_
        dz = jnp.concatenate([dmix * sig, dmix * val * sig * (1.0 - sig)], axis=1)
        return dz, _rowsum(do_ * (val * sig))

    dz, dg_a0 = act_call(glu_bwd_fn, [dx1, z, g_a], [(2 * d, BF16, "tile"), (d, F32, "seq")], "glu_dres")
    dz = dz.reshape(t, 2 * d)
    dy = mm_nt_col(dz, w_glu, "glu_dup", (F32,), lambda acc, y_: (acc * _gelu_grad(y_),),
                   (y.reshape(t, d),)).reshape(bsz, seq, d)
    dwglu = mm_tn(ge.reshape(t, d), dz, "c", N_DEV, "glu_dw")
    (dwglu,), glu_token = exchange_start([(dwglu, True)], "glu_dw_start")
    du, dbbd_re, dbbd_im, dcbd_re, dcbd_im, dab_re, dab_im, dd_skip = s5_bwd(
        dy, h0, st_re, st_im, bbd_re, bbd_im, cbd_re, cbd_im, abr, abi, d_skip + glu_token[0, 0])
    dx0, dsh_a0, dsc_a0, dgn_mix0 = act_call(
        _norm_mod_bwd, [x, par(mix_norm_g[0]), sc_a, du, dx1],
        [(d, F32, "tile"), (d, F32, "seq"), (d, F32, "seq"), (d, F32, "all")], "mix0_dnorm")

    from_bbd = lambda m: jnp.swapaxes(_block_diag_part(m, S5_GROUP, S5_STATE), 2, 3).reshape(gp, S5_GROUP)
    d_c = lambda m: jnp.swapaxes(_block_diag_part(m, S5_STATE, S5_GROUP), 2, 3).reshape(
        1, n_groups, S5_GROUP, S5_STATE)
    d_lam_re, d_lam_im, d_log_dt, d_b_re, d_b_im = s5_prep_bwd(
        lam_re, lam_im, log_dt, b_re, b_im, dab_re.reshape(gp, 1), dab_im.reshape(gp, 1),
        from_bbd(dbbd_re), from_bbd(dbbd_im))

    small = all_reduce_small([
        jnp.stack([dgn_mix0.reshape(d), dgn_mix1.reshape(d)]),
        jnp.stack([dgn_mlp0.reshape(d), dgn_mlp1.reshape(d)]),
        d_lam_re.reshape(1, n_groups, S5_STATE), d_lam_im.reshape(1, n_groups, S5_STATE),
        d_log_dt.reshape(1, n_groups, S5_STATE).sum(axis=-1),
        d_b_re.reshape(s5_b_re.shape), d_b_im.reshape(s5_b_im.shape),
        d_c(dcbd_re), d_c(dcbd_im),
        dd_skip.reshape(1, d),
        dgn_kv.reshape(d),
        dkg.reshape(d // HEAD_DIM, HEAD_DIM).sum(axis=0),
        dqg.reshape(d // HEAD_DIM, HEAD_DIM).sum(axis=0)[None, :],
    ], "small_grads")
    (g_mix_norm, g_mlp_norm, g_a_re, g_a_im, g_log_dt, g_b_re, g_b_im, g_c_re, g_c_im,
     g_skip_full, g_kv_norm, g_k_norm, g_q_norm) = small
    g_s5_d = lax.dynamic_slice_in_dim(g_skip_full, dev * (d // N_DEV), d // N_DEV, axis=1)

    dm_mine = jnp.concatenate([
        dsh_a0, dsc_a0, dg_a0, dsh_m0, dsc_m0, dg_m0,
        dsh_a1, dsc_a1, dg_a1, dsh_m1, dsc_m1, dg_m1, dkv_sh, dkv_sc], axis=2).reshape(bsz, 2 * e_ada + e_kv)
    dm_all = all_gather(dm_mine, "gather_dmod").reshape(N_DEV * bsz, 2 * e_ada + e_kv)
    dm_cols = jnp.concatenate([
        lax.dynamic_slice_in_dim(dm_all, dev * n_ada, n_ada, axis=1),
        lax.dynamic_slice_in_dim(dm_all, e_ada + dev * n_ada, n_ada, axis=1),
        lax.dynamic_slice_in_dim(dm_all, 2 * e_ada + dev * n_kv, n_kv, axis=1)], axis=1)
    dw_cols, db_all = ada_bwd(c_all, dm_cols, dm_all)
    g_ada_w = jnp.stack([dw_cols[:, :n_ada], dw_cols[:, n_ada:2 * n_ada]])
    g_kv_ada_w = dw_cols[:, 2 * n_ada:]
    g_ada_b = db_all[0, :2 * e_ada].reshape(2, e_ada)
    g_kv_ada_b = db_all[0, 2 * e_ada:]

    landed = lambda handle, name: slab_sum(exchange_wait(handle, dx0, name + "_wait"), name + "_sum")
    g_w1 = jnp.stack([landed(dw1_0, "rs_w1_0"), landed(dw1_1, "rs_w1_1")])
    g_w2 = jnp.stack([landed(dw2_0, "rs_w2_0"), landed(dw2_1, "rs_w2_1")])
    g_glu = landed(dwglu, "rs_glu")[None]
    g_wkv = landed(dwkv, "rs_wkv")
    g_wq = landed(dwq, "rs_wq")[None]
    g_wo = landed(dwo, "rs_wo")[None]

    weights = [ada_w, ada_b, mix_norm_g, mlp_norm_g, mlp_w1, mlp_w2, s5_a_re, s5_a_im, s5_log_dt, s5_b_re,
               s5_b_im, s5_c_re, s5_c_im, s5_d, s5_w_glu, kv_ada_w, kv_ada_b, kv_norm_g, w_kv, k_norm_g,
               sb_w_q, q_norm_g, sb_w_o]
    grads = [g_ada_w, g_ada_b, g_mix_norm, g_mlp_norm, g_w1, g_w2, g_a_re, g_a_im, g_log_dt, g_b_re,
             g_b_im, g_c_re, g_c_im, g_s5_d, g_glu, g_kv_ada_w, g_kv_ada_b, g_kv_norm, g_wkv, g_k_norm,
             g_wq, g_q_norm, g_wo]
    ms = [m_ada_w, m_ada_b, m_mix_norm_g, m_mlp_norm_g, m_mlp_w1, m_mlp_w2, m_s5_a_re, m_s5_a_im, m_s5_log_dt,
          m_s5_b_re, m_s5_b_im, m_s5_c_re, m_s5_c_im, m_s5_d, m_s5_w_glu, m_kv_ada_w, m_kv_ada_b, m_kv_norm_g,
          m_w_kv, m_k_norm_g, m_sb_w_q, m_q_norm_g, m_sb_w_o]
    vs = [v_ada_w, v_ada_b, v_mix_norm_g, v_mlp_norm_g, v_mlp_w1, v_mlp_w2, v_s5_a_re, v_s5_a_im, v_s5_log_dt,
          v_s5_b_re, v_s5_b_im, v_s5_c_re, v_s5_c_im, v_s5_d, v_s5_w_glu, v_kv_ada_w, v_kv_ada_b, v_kv_norm_g,
          v_w_kv, v_k_norm_g, v_sb_w_q, v_q_norm_g, v_sb_w_o]
    grads = [g.reshape(w.shape) for g, w in zip(grads, weights)]
    deltas, new_ms, new_vs = [], [], []
    for i, (w, g, m, v) in enumerate(zip(weights, grads, ms, vs)):
        dl, nm, nv = adamw(w, g, m, v, f"adamw_{i}")
        deltas.append(dl)
        new_ms.append(nm)
        new_vs.append(nv)
    return (loss, dx0, *grads, *deltas, *new_ms, *new_vs)
```

```python
import functools
import math

import jax
import jax.numpy as jnp
from jax import lax
from jax.experimental import pallas as pl
from jax.experimental.pallas import tpu as pltpu

F32 = jnp.float32
BF16 = jnp.bfloat16

N_DEV = 8
N_CHIPS = 4
MESH = pl.DeviceIdType.MESH
ANY = pl.BlockSpec(memory_space=pl.ANY)

LANES = 128
VMEM_LIMIT_BYTES = 48 * 2 ** 20
TILE_BUDGET_BYTES = 4 * 2 ** 20

S5_GROUP = 16
S5_STATE = 64
S5_BLOCK_GROUPS = 16
HEAD_DIM = 64
ATT_BLOCK = 128
EPS = 1e-6

ADAM_LR = 0.001
ADAM_B1 = 0.9
ADAM_B2 = 0.999
ADAM_EPS = 1e-08
ADAM_WD = 0.01
ADAM_STEP = 10


def _cparams(*sem):
    return pltpu.CompilerParams(dimension_semantics=sem, vmem_limit_bytes=VMEM_LIMIT_BYTES)


def _divisor_tile(n, limit, mult):
    best = None
    for t in range(mult, min(n, limit) + 1, mult):
        if n % t == 0:
            best = t
    return best if best is not None else n


def _tile_m(m):
    return _divisor_tile(m, 2048 if m >= 4096 else 256, 16)


def all_gather(x, name):
    def body(x_ref, out_ref, send_sems, recv_sems, local_sem):
        ax, ay, ac = lax.axis_index("x"), lax.axis_index("y"), lax.axis_index("c")
        me, sibling = (ax, ay, ac), (ax, ay, 1 - ac)
        chips = [(1 - ax, ay), (ax, 1 - ay), (1 - ax, 1 - ay)]

        def slot(px, py, pc):
            return out_ref.at[4 * px + 2 * py + pc]

        def copy(k, block, to, src=None):
            return pltpu.make_async_remote_copy(
                src_ref=slot(*block) if src is None else src, dst_ref=slot(*block),
                send_sem=send_sems.at[k], recv_sem=recv_sems.at[k], device_id=to, device_id_type=MESH)

        mine = pltpu.make_async_copy(x_ref, slot(*me), local_sem)
        mine.start()
        first = [copy(0, me, sibling, src=x_ref)]
        first += [copy(1 + j, me, (*chip, ac), src=x_ref) for j, chip in enumerate(chips)]
        for cp in first:
            cp.start()
        passed = [copy(4 + j, (*chip, ac), sibling) for j, chip in enumerate(chips)]
        for j, chip in enumerate(chips):
            copy(1 + j, (*chip, ac), me).wait_recv()
            passed[j].start()
        copy(0, sibling, me).wait_recv()
        for j, chip in enumerate(chips):
            copy(4 + j, (*chip, 1 - ac), me).wait_recv()
        for cp in first + passed:
            cp.wait_send()
        mine.wait()

    return pl.pallas_call(
        body, name=name,
        out_shape=jax.ShapeDtypeStruct((N_DEV,) + x.shape, x.dtype),
        in_specs=[ANY], out_specs=ANY,
        scratch_shapes=[pltpu.SemaphoreType.DMA((7,)), pltpu.SemaphoreType.DMA((7,)), pltpu.SemaphoreType.DMA],
    )(x)


HBM = pl.BlockSpec(memory_space=pltpu.HBM)
SEM = pl.BlockSpec(memory_space=pltpu.SEMAPHORE)
N_PEERS = N_DEV - 1


def _peers():
    ax, ay, ac = lax.axis_index("x"), lax.axis_index("y"), lax.axis_index("c")
    flip = lambda v, bit: 1 - v if bit else v
    return [(flip(ax, k & 4), flip(ay, k & 2), flip(ac, k & 1)) for k in range(1, N_DEV)]


def _dev_index(pos):
    return 4 * pos[0] + 2 * pos[1] + pos[2]


def exchange_start(items, name, after=None):
    n = len(items)
    srcs = [a for a, _ in items]
    blocks = [a.shape[1:] if scatter else a.shape for a, scatter in items]
    extra = list(after or ())

    def body(*refs):
        src_refs, land_refs = refs[:n], refs[n:2 * n]
        outs = refs[2 * n + len(extra):]
        send_sems, recv_sems = outs[:n], outs[n:2 * n]
        token = outs[-1]
        me = _dev_index((lax.axis_index("x"), lax.axis_index("y"), lax.axis_index("c")))
        for w, (_, scatter) in enumerate(items):
            for k, peer in enumerate(_peers()):
                src = src_refs[w].at[_dev_index(peer)] if scatter else src_refs[w]
                pltpu.make_async_remote_copy(
                    src_ref=src, dst_ref=land_refs[w].at[me], send_sem=send_sems[w].at[k],
                    recv_sem=recv_sems[w].at[k], device_id=peer, device_id_type=MESH).start()
        token[...] = jnp.zeros_like(token)

    lands = [lax.empty((N_DEV,) + blk, a.dtype) for a, blk in zip(srcs, blocks)]
    res = pl.pallas_call(
        body, name=name,
        out_shape=([pltpu.SemaphoreType.DMA((N_PEERS,))] * (2 * n)
                   + [pltpu.HBM(a.shape, a.dtype) for a in srcs] + [pltpu.HBM(l.shape, l.dtype) for l in lands]
                   + [jax.ShapeDtypeStruct((8, LANES), F32)]),
        in_specs=[HBM] * (2 * n) + [ANY] * len(extra),
        out_specs=[SEM] * (2 * n) + [HBM] * (2 * n) + [pl.BlockSpec(memory_space=pltpu.VMEM)],
        input_output_aliases={i: 2 * n + i for i in range(2 * n)},
        compiler_params=pltpu.CompilerParams(has_side_effects=pltpu.SideEffectType.DATAFLOW_SIDE_EFFECTING),
    )(*[pltpu.with_memory_space_constraint(a, pltpu.HBM) for a in srcs + lands], *extra)
    handles = [(res[w], res[n + w], res[2 * n + w], res[3 * n + w], scatter) for w, (_, scatter) in enumerate(items)]
    return handles, res[-1]


def exchange_wait(handle, after, name):
    send_sem, recv_sem, src, land, scatter = handle

    def body(src_ref, land_ref, send_ref, recv_ref, after_ref, src_out, land_out):
        for k, peer in enumerate(_peers()):
            slot = _dev_index(peer)
            copy = pltpu.make_async_remote_copy(
                src_ref=src_ref.at[slot] if scatter else src_ref, dst_ref=land_ref.at[slot],
                send_sem=send_ref.at[k], recv_sem=recv_ref.at[k], device_id=peer, device_id_type=MESH)
            copy.wait_send()
            copy.wait_recv()

    src, landed = pl.pallas_call(
        body, name=name,
        out_shape=(pltpu.HBM(src.shape, src.dtype), pltpu.HBM(land.shape, land.dtype)),
        in_specs=[HBM, HBM, SEM, SEM, ANY], out_specs=(HBM, HBM), input_output_aliases={0: 0, 1: 1},
        compiler_params=pltpu.CompilerParams(has_side_effects=pltpu.SideEffectType.DATAFLOW_SIDE_EFFECTING),
    )(src, land, send_sem, recv_sem, after)
    dev = _dev_index((lax.axis_index("x"), lax.axis_index("y"), lax.axis_index("c")))
    own = lax.dynamic_index_in_dim(src, dev, axis=0, keepdims=True) if scatter else src[None]
    return lax.dynamic_update_slice_in_dim(landed, own, dev, axis=0)


def rows_call(fn, ins, outs, name):
    rows = ins[0].shape[1]
    per_row = sum(a.shape[0] * a.shape[2] * a.dtype.itemsize for a in ins)
    per_row += sum(l * c * jnp.dtype(dt).itemsize for l, c, dt in outs)
    tr = _divisor_tile(rows, max(16, TILE_BUDGET_BYTES // per_row), 16)
    n_in = len(ins)

    def body(*refs):
        vals = fn(*[r[...] for r in refs[:n_in]])
        if not isinstance(vals, (tuple, list)):
            vals = (vals,)
        for r, v in zip(refs[n_in:], vals):
            r[...] = v.astype(r.dtype)

    def spec(l, c):
        return pl.BlockSpec((l, tr, c), lambda i: (0, i, 0))

    res = pl.pallas_call(
        body, name=name, grid=(rows // tr,),
        in_specs=[spec(a.shape[0], a.shape[2]) for a in ins],
        out_specs=[spec(l, c) for l, c, _ in outs],
        out_shape=[jax.ShapeDtypeStruct((l, rows, c), dt) for l, c, dt in outs],
        compiler_params=_cparams("arbitrary"),
    )(*ins)
    return res


def _as_rows(a, lead=0):
    shape = a.shape
    l = int(math.prod(shape[:lead])) if lead else 1
    rest = shape[lead:]
    c = rest[-1] if rest else 1
    r = int(math.prod(rest[:-1])) if len(rest) > 1 else 1
    return a.reshape(l, r, c)


def act_call(fn, ins, outs, name):
    bsz, seq = ins[0].shape[0], ins[0].shape[1]
    per_row = sum(a.shape[2] * a.dtype.itemsize for a in ins if a.shape[1] == seq)
    per_row += sum(c * jnp.dtype(dt).itemsize for c, dt, kind in outs if kind == "tile")
    ts = _divisor_tile(seq, max(16, TILE_BUDGET_BYTES // per_row), 16)
    n_in = len(ins)

    def in_spec(a):
        c = a.shape[2]
        if a.shape[1] == seq:
            return pl.BlockSpec((None, ts, c), lambda b, s: (b, s, 0))
        if a.shape[0] == bsz:
            return pl.BlockSpec((None, 1, c), lambda b, s: (b, 0, 0))
        return pl.BlockSpec((None, 1, c), lambda b, s: (0, 0, 0))

    def out_spec(c, kind):
        if kind == "tile":
            return pl.BlockSpec((None, ts, c), lambda b, s: (b, s, 0))
        if kind == "seq":
            return pl.BlockSpec((None, 1, c), lambda b, s: (b, 0, 0))
        return pl.BlockSpec((None, 1, c), lambda b, s: (0, 0, 0))

    def out_shape(c, dt, kind):
        if kind == "tile":
            return jax.ShapeDtypeStruct((bsz, seq, c), dt)
        return jax.ShapeDtypeStruct((bsz if kind == "seq" else 1, 1, c), dt)

    def accumulate(ref, v, first):
        @pl.when(first)
        def _():
            ref[...] = jnp.zeros_like(ref)

        ref[...] += v.astype(ref.dtype)

    def body(*refs):
        b, s = pl.program_id(0), pl.program_id(1)
        vals = fn(*[r[...] for r in refs[:n_in]])
        if not isinstance(vals, (tuple, list)):
            vals = (vals,)
        for ref, v, (_, _, kind) in zip(refs[n_in:], vals, outs):
            if kind == "tile":
                ref[...] = v.astype(ref.dtype)
            elif kind == "seq":
                accumulate(ref, v, s == 0)
            else:
                accumulate(ref, v, jnp.logical_and(b == 0, s == 0))

    return pl.pallas_call(
        body, name=name, grid=(bsz, seq // ts),
        in_specs=[in_spec(a) for a in ins],
        out_specs=[out_spec(c, kind) for c, _, kind in outs],
        out_shape=[out_shape(*o) for o in outs],
        compiler_params=_cparams("arbitrary", "arbitrary"),
    )(*ins)


def _mm(name, grid, a, a_spec, b, b_spec, dims, out_shape, out_spec, out_dtypes, acc_steps,
        epi=None, extras=(), extra_spec=None):
    n_ex, n_out = len(extras), len(out_dtypes)
    tile = tuple(d for d in out_spec.block_shape if d is not None)

    def body(*refs):
        a_ref, b_ref = refs[0], refs[1]
        ex_refs = refs[2:2 + n_ex]
        o_refs = refs[2 + n_ex:2 + n_ex + n_out]
        p = lax.dot_general(a_ref[...].astype(BF16), b_ref[...].astype(BF16), (dims, ((), ())),
                            preferred_element_type=F32)

        def finish(acc):
            vals = epi(acc, *[r[...] for r in ex_refs]) if epi is not None else (acc,) * n_out
            for r, v in zip(o_refs, vals):
                r[...] = v.astype(r.dtype)

        if not acc_steps:
            finish(p)
        else:
            acc_ref = refs[-1]
            s = pl.program_id(1)

            @pl.when(s == 0)
            def _():
                acc_ref[...] = p

            @pl.when(s > 0)
            def _():
                acc_ref[...] += p

            @pl.when(s == acc_steps - 1)
            def _():
                finish(acc_ref[...])

    res = pl.pallas_call(
        body, name=name, grid=grid,
        in_specs=[a_spec, b_spec] + [extra_spec] * n_ex,
        out_specs=[out_spec] * n_out,
        out_shape=[jax.ShapeDtypeStruct(out_shape, dt) for dt in out_dtypes],
        scratch_shapes=[pltpu.VMEM(tile, F32)] if acc_steps else [],
        compiler_params=_cparams("arbitrary", "arbitrary"),
    )(a, b, *extras)
    return res if n_out > 1 else res[0]


def mm_nn_col(a, w, name, out_dtypes=(F32,), epi=None):
    m, k = a.shape
    ns, _, nb = w.shape
    tm = _tile_m(m)
    return _mm(name, (m // tm, ns), a, pl.BlockSpec((tm, k), lambda i, j: (i, 0)),
               w, pl.BlockSpec((None, k, nb), lambda i, j: (j, 0, 0)), ((1,), (0,)),
               (m, ns * nb), pl.BlockSpec((tm, nb), lambda i, j: (i, j)), out_dtypes, 0, epi)


def mm_nn_row(a, w, name, out_dtypes=(F32,)):
    m = a.shape[0]
    ns, kb, n = w.shape
    tm = _tile_m(m)
    return _mm(name, (m // tm, ns), a, pl.BlockSpec((tm, kb), lambda i, s: (i, s)),
               w, pl.BlockSpec((None, kb, n), lambda i, s: (s, 0, 0)), ((1,), (0,)),
               (m, n), pl.BlockSpec((tm, n), lambda i, s: (i, 0)), out_dtypes, ns)


def mm_nt_col(dc, w, name, out_dtypes=(F32,), epi=None, extras=()):
    m = dc.shape[0]
    ns, k, nb = w.shape
    tm = _tile_m(m) // 2 if extras else _tile_m(m)
    spec = pl.BlockSpec((tm, k), lambda i, s: (i, 0))
    return _mm(name, (m // tm, ns), dc, pl.BlockSpec((tm, nb), lambda i, s: (i, s)),
               w, pl.BlockSpec((None, k, nb), lambda i, s: (s, 0, 0)), ((1,), (1,)),
               (m, k), spec, out_dtypes, ns, epi, extras, spec)


def mm_nt_row(dc, w, name, out_dtypes=(F32,), epi=None, extras=()):
    m, n = dc.shape
    ns, kb, _ = w.shape
    tm = _tile_m(m)
    spec = pl.BlockSpec((tm, kb), lambda i, s: (i, s))
    return _mm(name, (m // tm, ns), dc, pl.BlockSpec((tm, n), lambda i, s: (i, 0)),
               w, pl.BlockSpec((None, kb, n), lambda i, s: (s, 0, 0)), ((1,), (1,)),
               (m, ns * kb), spec, out_dtypes, 0, epi, extras, spec)


def mm_tn(a, c, slab, ns, name, out_dtype=BF16):
    m, ka_all = a.shape
    nc_all = c.shape[1]
    ka = ka_all // ns if slab == "a" else ka_all
    nc = nc_all // ns if slab == "c" else nc_all
    tt = _divisor_tile(m, 256, 16)
    steps = m // tt

    def body(a_ref, c_ref, o_ref, acc_ref):
        t = pl.program_id(0)

        @pl.when(t == 0)
        def _():
            acc_ref[...] = jnp.zeros_like(acc_ref)

        for s in range(ns):
            a_s = a_ref[:, s * ka:(s + 1) * ka] if slab == "a" else a_ref[...]
            c_s = c_ref[:, s * nc:(s + 1) * nc] if slab == "c" else c_ref[...]
            acc_ref[s] += lax.dot_general(a_s.astype(BF16), c_s.astype(BF16), (((0,), (0,)), ((), ())),
                                          preferred_element_type=F32)

        @pl.when(t == steps - 1)
        def _():
            o_ref[...] = acc_ref[...].astype(o_ref.dtype)

    return pl.pallas_call(
        body, name=name, grid=(steps,),
        in_specs=[pl.BlockSpec((tt, ka_all), lambda t: (t, 0)), pl.BlockSpec((tt, nc_all), lambda t: (t, 0))],
        out_specs=pl.BlockSpec((ns, ka, nc), lambda t: (0, 0, 0)),
        out_shape=jax.ShapeDtypeStruct((ns, ka, nc), out_dtype),
        scratch_shapes=[pltpu.VMEM((ns, ka, nc), F32)],
        compiler_params=_cparams("arbitrary"),
    )(a, c)


def slab_sum(landed, name):
    shape = landed.shape[1:]
    total = rows_call(lambda g: jnp.sum(g.astype(F32), axis=0, keepdims=True),
                      [_as_rows(landed, 1)], [(1, shape[-1], F32)], name)[0]
    return total.reshape(shape)


def all_reduce_small(leaves, name):
    sizes = [int(a.size) for a in leaves]
    flat = jnp.concatenate([a.reshape(-1) for a in leaves])
    total = int(flat.size)
    padded = -(-total // (16 * LANES)) * (16 * LANES)
    flat = jnp.pad(flat, (0, padded - total)).reshape(padded // LANES, LANES)
    gathered = all_gather(flat, name + "_gather")
    summed = rows_call(lambda g: jnp.sum(g, axis=0, keepdims=True), [gathered],
                       [(1, LANES, F32)], name + "_sum")[0].reshape(-1)
    out, at = [], 0
    for a, n in zip(leaves, sizes):
        out.append(summed[at:at + n].reshape(a.shape))
        at += n
    return out


def adamw(w, g, m, v, name):
    c = w.shape[-1] if w.ndim else 1

    def fn(w_, g_, m_, v_):
        nm = ADAM_B1 * m_ + (1.0 - ADAM_B1) * g_
        nv = ADAM_B2 * v_ + (1.0 - ADAM_B2) * (g_ * g_)
        m_hat = nm / (1.0 - ADAM_B1 ** ADAM_STEP)
        v_hat = nv / (1.0 - ADAM_B2 ** ADAM_STEP)
        delta = -ADAM_LR * (m_hat / (jnp.sqrt(v_hat) + ADAM_EPS) + ADAM_WD * w_)
        return delta, nm, nv

    res = rows_call(fn, [_as_rows(t) for t in (w, g.astype(F32), m, v)], [(1, c, F32)] * 3, name)
    return tuple(r.reshape(w.shape) for r in res)


def _rowsum(v):
    return jnp.sum(v, axis=0, keepdims=True)


def _norm_mod(x, g, sh, sc):
    n = x * lax.rsqrt(jnp.mean(x * x, axis=-1, keepdims=True) + EPS)
    return (n * g) * (1.0 + sc) + sh


def _norm_mod_bwd(x, g, sc, dh, dres):
    r = lax.rsqrt(jnp.mean(x * x, axis=-1, keepdims=True) + EPS)
    n = x * r
    dy = dh * (1.0 + sc)
    dn = dy * g
    dx = r * (dn - n * jnp.mean(dn * n, axis=-1, keepdims=True))
    return dres + dx, _rowsum(dh), _rowsum(dh * (n * g)), _rowsum(dy * n)


def _head_mean(v):
    low = lax.broadcasted_iota(jnp.int32, (1, LANES), 1) < HEAD_DIM
    parts = []
    for p in range(v.shape[1] // LANES):
        blk = v[:, p * LANES:(p + 1) * LANES]
        s0 = jnp.sum(jnp.where(low, blk, 0.0), axis=-1, keepdims=True)
        s1 = jnp.sum(jnp.where(low, 0.0, blk), axis=-1, keepdims=True)
        parts.append(jnp.where(low, s0, s1))
    return jnp.concatenate(parts, axis=1) * (1.0 / HEAD_DIM)


def _head_norm(x, g):
    return x * lax.rsqrt(_head_mean(x * x) + EPS) * g


def _head_norm_bwd(x, g, dy):
    r = lax.rsqrt(_head_mean(x * x) + EPS)
    n = x * r
    dn = dy * g
    return r * (dn - n * _head_mean(dn * n)), _rowsum(dy * n)


GELU_C = math.sqrt(2.0 / math.pi)
GELU_A = 0.044715


def _gelu_grad(y):
    t = jnp.tanh(GELU_C * (y + GELU_A * y * y * y))
    return 0.5 * (1.0 + t) + 0.5 * y * (1.0 - t * t) * GELU_C * (1.0 + 3.0 * GELU_A * y * y)


def ada_fwd(c_all, w_cols, b_cols):
    def body(c_ref, w_ref, b_ref, o_ref):
        c = c_ref[...]
        s = (c * jax.nn.sigmoid(c)).astype(BF16)
        o_ref[...] = jnp.dot(s, w_ref[...].astype(BF16), preferred_element_type=F32) + b_ref[...]

    return pl.pallas_call(
        body, name="ada_fwd", out_shape=jax.ShapeDtypeStruct((c_all.shape[0], w_cols.shape[1]), F32),
        compiler_params=pltpu.CompilerParams(vmem_limit_bytes=VMEM_LIMIT_BYTES),
    )(c_all, w_cols, b_cols)


def ada_bwd(c_all, dm_cols, dm_all):
    def body(c_ref, d_ref, all_ref, dw_ref, db_ref):
        c = c_ref[...]
        s = (c * jax.nn.sigmoid(c)).astype(BF16)
        dw_ref[...] = lax.dot_general(s, d_ref[...].astype(BF16), (((0,), (0,)), ((), ())),
                                      preferred_element_type=F32)
        db_ref[...] = jnp.sum(all_ref[...], axis=0, keepdims=True)

    return pl.pallas_call(
        body, name="ada_bwd",
        out_shape=[jax.ShapeDtypeStruct((c_all.shape[1], dm_cols.shape[1]), F32),
                   jax.ShapeDtypeStruct((1, dm_all.shape[1]), F32)],
        compiler_params=pltpu.CompilerParams(vmem_limit_bytes=VMEM_LIMIT_BYTES),
    )(c_all, dm_cols, dm_all)


def _s5_discretise(lam_re, lam_im, log_dt, b_re, b_im):
    dt = jnp.exp(log_dt)
    mag = jnp.exp(lam_re * dt)
    ab_re = mag * jnp.cos(lam_im * dt)
    ab_im = mag * jnp.sin(lam_im * dt)
    den = lam_re * lam_re + lam_im * lam_im
    nr = ab_re - 1.0
    ni = ab_im
    f_re = (nr * lam_re + ni * lam_im) / den
    f_im = (ni * lam_re - nr * lam_im) / den
    bb_re = f_re * b_re - f_im * b_im
    bb_im = f_re * b_im + f_im * b_re
    return ab_re, ab_im, bb_re, bb_im


def s5_prep(lam_re, lam_im, log_dt, b_re, b_im):
    gp, h = b_re.shape

    def body(lr, li, ld, br, bi, o_ar, o_ai, o_br, o_bi):
        res = _s5_discretise(lr[...], li[...], ld[...], br[...], bi[...])
        for r, v in zip((o_ar, o_ai, o_br, o_bi), res):
            r[...] = v

    col, mat = jax.ShapeDtypeStruct((gp, 1), F32), jax.ShapeDtypeStruct((gp, h), F32)
    return pl.pallas_call(body, name="s5_prep", out_shape=[col, col, mat, mat])(lam_re, lam_im, log_dt, b_re, b_im)


def s5_prep_bwd(lam_re, lam_im, log_dt, b_re, b_im, d_ab_re, d_ab_im, d_bb_re, d_bb_im):
    gp, h = b_re.shape

    def body(lr, li, ld, br, bi, g_ar, g_ai, g_br, g_bi, o_lr, o_li, o_ld, o_br, o_bi):
        _, vjp = jax.vjp(_s5_discretise, lr[...], li[...], ld[...], br[...], bi[...])
        res = vjp((g_ar[...], g_ai[...], g_br[...], g_bi[...]))
        for r, v in zip((o_lr, o_li, o_ld, o_br, o_bi), res):
            r[...] = v

    col, mat = jax.ShapeDtypeStruct((gp, 1), F32), jax.ShapeDtypeStruct((gp, h), F32)
    return pl.pallas_call(body, name="s5_prep_bwd", out_shape=[col, col, col, mat, mat])(
        lam_re, lam_im, log_dt, b_re, b_im, d_ab_re, d_ab_im, d_bb_re, d_bb_im)


def _s5_chunk(seq):
    return _divisor_tile(seq, 256, 16)


def s5_fwd(u, bbd_re, bbd_im, cbd_re, cbd_im, ab_re, ab_im, dskip):
    bsz, seq, d = u.shape
    nb, cb, ns = bbd_re.shape
    lc = _s5_chunk(seq)

    def body(u_ref, bre_ref, bim_ref, cre_ref, cim_ref, ar_ref, ai_ref, d_ref, y_ref, sre_ref, sim_ref,
             carry_re, carry_im):
        t = pl.program_id(1)

        @pl.when(t == 0)
        def _():
            carry_re[...] = jnp.zeros_like(carry_re)
            carry_im[...] = jnp.zeros_like(carry_im)

        for b in range(bsz):
            ub = u_ref[b].astype(BF16)
            sre_ref[b] = jnp.dot(ub, bre_ref[...], preferred_element_type=F32)
            sim_ref[b] = jnp.dot(ub, bim_ref[...], preferred_element_type=F32)
        ar, ai = ar_ref[...], ai_ref[...]

        def step(i, carry):
            row = pl.ds(i, 1)
            out = []
            for b, (cr, ci) in enumerate(carry):
                nr = ar * cr - ai * ci + sre_ref[b, row, :]
                ni = ar * ci + ai * cr + sim_ref[b, row, :]
                sre_ref[b, row, :] = nr
                sim_ref[b, row, :] = ni
                out.append((nr, ni))
            return tuple(out)

        init = tuple((carry_re[b], carry_im[b]) for b in range(bsz))
        last = lax.fori_loop(0, lc, step, init, unroll=8)
        for b, (cr, ci) in enumerate(last):
            carry_re[b] = cr
            carry_im[b] = ci
            y = jnp.dot(sre_ref[b].astype(BF16), cre_ref[...], preferred_element_type=F32)
            y -= jnp.dot(sim_ref[b].astype(BF16), cim_ref[...], preferred_element_type=F32)
            y_ref[b] = y + d_ref[...] * u_ref[b]

    chan = pl.BlockSpec((bsz, lc, cb), lambda n, t: (0, t, n))
    state = pl.BlockSpec((bsz, lc, ns), lambda n, t: (0, t, n))
    par = lambda r, c: pl.BlockSpec((None, r, c), lambda n, t: (n, 0, 0))
    return pl.pallas_call(
        body, name="s5_fwd", grid=(nb, seq // lc),
        in_specs=[chan, par(cb, ns), par(cb, ns), par(ns, cb), par(ns, cb), par(1, ns), par(1, ns),
                  pl.BlockSpec((None, 1, cb), lambda n, t: (0, 0, n))],
        out_specs=[chan, state, state],
        out_shape=[jax.ShapeDtypeStruct((bsz, seq, d), F32),
                   jax.ShapeDtypeStruct((bsz, seq, nb * ns), F32),
                   jax.ShapeDtypeStruct((bsz, seq, nb * ns), F32)],
        scratch_shapes=[pltpu.VMEM((bsz, 1, ns), F32), pltpu.VMEM((bsz, 1, ns), F32)],
        compiler_params=_cparams("arbitrary", "arbitrary"),
    )(u, bbd_re, bbd_im, cbd_re, cbd_im, ab_re, ab_im, dskip)


def s5_bwd(dy, u, st_re, st_im, bbd_re, bbd_im, cbd_re, cbd_im, ab_re, ab_im, dskip):
    bsz, seq, d = u.shape
    nb, cb, ns = bbd_re.shape
    lc = _s5_chunk(seq)
    nc = seq // lc

    def body(dy_ref, u_ref, sre_ref, sim_ref, bre_ref, bim_ref, cre_ref, cim_ref, ar_ref, ai_ref, d_ref,
             du_ref, dbre_ref, dbim_ref, dcre_ref, dcim_ref, dar_ref, dai_ref, dd_ref,
             g_re, g_im, gs_re, gs_im, carry_re, carry_im):
        t = pl.program_id(1)

        @pl.when(t == 0)
        def _():
            for r in (dbre_ref, dbim_ref, dcre_ref, dcim_ref, dar_ref, dai_ref, dd_ref, carry_re, carry_im):
                r[...] = jnp.zeros_like(r)

        nt = (((1,), (1,)), ((), ()))
        tn = (((0,), (0,)), ((), ()))
        for b in range(bsz):
            dyb = dy_ref[b].astype(BF16)
            g_re[b] = lax.dot_general(dyb, cre_ref[...], nt, preferred_element_type=F32)
            g_im[b] = -lax.dot_general(dyb, cim_ref[...], nt, preferred_element_type=F32)
        ar, ai = ar_ref[...], ai_ref[...]

        def step(k, carry):
            row = pl.ds(lc - 1 - k, 1)
            out = []
            for b, (cr, ci) in enumerate(carry):
                gs_re[b, row, :] = cr
                gs_im[b, row, :] = ci
                nr = ar * cr + ai * ci + g_re[b, row, :]
                ni = ar * ci - ai * cr + g_im[b, row, :]
                g_re[b, row, :] = nr
                g_im[b, row, :] = ni
                out.append((nr, ni))
            return tuple(out)

        init = tuple((carry_re[b], carry_im[b]) for b in range(bsz))
        last = lax.fori_loop(0, lc, step, init, unroll=8)
        for b, (cr, ci) in enumerate(last):
            carry_re[b] = cr
            carry_im[b] = ci
            dyf, uf = dy_ref[b], u_ref[b]
            dyb, ub = dyf.astype(BF16), uf.astype(BF16)
            sr, si = sre_ref[b], sim_ref[b]
            hr, hi = gs_re[b], gs_im[b]
            dar_ref[...] += _rowsum(hr * sr + hi * si)
            dai_ref[...] += _rowsum(hi * sr - hr * si)
            gr, gi = g_re[b].astype(BF16), g_im[b].astype(BF16)
            du = lax.dot_general(gr, bre_ref[...], nt, preferred_element_type=F32)
            du += lax.dot_general(gi, bim_ref[...], nt, preferred_element_type=F32)
            du_ref[b] = du + d_ref[...] * dyf
            dbre_ref[...] += lax.dot_general(ub, gr, tn, preferred_element_type=F32)
            dbim_ref[...] += lax.dot_general(ub, gi, tn, preferred_element_type=F32)
            dcre_ref[...] += lax.dot_general(sr.astype(BF16), dyb, tn, preferred_element_type=F32)
            dcim_ref[...] -= lax.dot_general(si.astype(BF16), dyb, tn, preferred_element_type=F32)
            dd_ref[...] += _rowsum(dyf * uf)

    chan = pl.BlockSpec((bsz, lc, cb), lambda n, t: (0, nc - 1 - t, n))
    state = pl.BlockSpec((bsz, lc, ns), lambda n, t: (0, nc - 1 - t, n))
    par = lambda r, c: pl.BlockSpec((None, r, c), lambda n, t: (n, 0, 0))
    return pl.pallas_call(
        body, name="s5_bwd", grid=(nb, nc),
        in_specs=[chan, chan, state, state, par(cb, ns), par(cb, ns), par(ns, cb), par(ns, cb),
                  par(1, ns), par(1, ns), pl.BlockSpec((None, 1, cb), lambda n, t: (0, 0, n))],
        out_specs=[chan, par(cb, ns), par(cb, ns), par(ns, cb), par(ns, cb), par(1, ns), par(1, ns), par(1, cb)],
        out_shape=[jax.ShapeDtypeStruct((bsz, seq, d), F32),
                   jax.ShapeDtypeStruct((nb, cb, ns), F32), jax.ShapeDtypeStruct((nb, cb, ns), F32),
                   jax.ShapeDtypeStruct((nb, ns, cb), F32), jax.ShapeDtypeStruct((nb, ns, cb), F32),
                   jax.ShapeDtypeStruct((nb, 1, ns), F32), jax.ShapeDtypeStruct((nb, 1, ns), F32),
                   jax.ShapeDtypeStruct((nb, 1, cb), F32)],
        scratch_shapes=[pltpu.VMEM((bsz, lc, ns), F32)] * 4 + [pltpu.VMEM((bsz, 1, ns), F32)] * 2,
        compiler_params=_cparams("arbitrary", "arbitrary"),
    )(dy, u, st_re, st_im, bbd_re, bbd_im, cbd_re, cbd_im, ab_re, ab_im, dskip)


ATT_HEADS = 4
ATT_LANES = ATT_HEADS * HEAD_DIM
ATT_KEYS = 2 * ATT_BLOCK
ATT_Q = 256
ATT_SCALE = 1.0 / math.sqrt(HEAD_DIM)
_NT = (((1,), (1,)), ((), ()))
_TN = (((0,), (0,)), ((), ()))
_HEADS = [slice(h * HEAD_DIM, (h + 1) * HEAD_DIM) for h in range(ATT_HEADS)]
_HALF = [slice(0, ATT_BLOCK), slice(ATT_BLOCK, ATT_KEYS)]


def _log_sigmoids(z):
    sp = jnp.log(1.0 + jnp.exp(-jnp.abs(z)))
    ls = jnp.minimum(z, 0.0) - sp
    return ls, ls - z


def _sum_matrix(after, inclusive):
    j = lax.broadcasted_iota(jnp.int32, (ATT_KEYS, ATT_KEYS), 0) % ATT_BLOCK
    s = lax.broadcasted_iota(jnp.int32, (ATT_KEYS, ATT_KEYS), 1)
    if after:
        hit = (j >= s) if inclusive else (j > s)
    else:
        hit = (j <= s) if inclusive else (j < s)
    return jnp.where(jnp.logical_or(hit, s >= ATT_BLOCK), 1.0, 0.0).astype(BF16)


def _hi_lo(v):
    hi = v.astype(BF16)
    lo = (v - hi.astype(F32)).astype(BF16)
    return jnp.concatenate([hi, lo], axis=1)


def _strict_mask(i, j):
    t = i * ATT_Q + lax.broadcasted_iota(jnp.int32, (ATT_Q, ATT_KEYS), 0)
    s = j * ATT_KEYS + lax.broadcasted_iota(jnp.int32, (ATT_Q, ATT_KEYS), 1)
    return s < t


def attention_fwd(q, k, v):
    bsz, seq, d = q.shape

    def body(q_ref, k_ref, v_ref, o_ref, tot_ref, z_buf, ls_buf, cs_buf, acc_buf, run_buf):
        i = pl.program_id(2)
        jd = ((i + 1) * ATT_Q - 1) // ATT_KEYS
        sums = _sum_matrix(True, False)
        acc_buf[...] = jnp.zeros_like(acc_buf)
        run_buf[...] = jnp.zeros_like(run_buf)

        def block(j, masked):
            rows = pl.ds(pl.multiple_of(j * ATT_KEYS, ATT_KEYS), ATT_KEYS)
            strict = _strict_mask(i, j) if masked else None
            for h, ln in enumerate(_HEADS):
                z_buf[h] = lax.dot_general(q_ref[:, ln], k_ref[rows, ln], _NT, preferred_element_type=F32)
            for h in range(ATT_HEADS):
                for half, cols in enumerate(_HALF):
                    ls, lf = _log_sigmoids(z_buf[h, :, cols])
                    if masked:
                        lf = jnp.where(strict[:, cols], lf, 0.0)
                    ls_buf[h, :, cols] = ls
                    cs_buf[h, half] = jnp.dot(_hi_lo(lf), sums, preferred_element_type=F32)
            for h, ln in enumerate(_HEADS):
                run = run_buf[h]
                late, early = cs_buf[h, 1], cs_buf[h, 0]
                a1 = run + late[:, _HALF[0]]
                run = run + late[:, _HALF[1]]
                a0 = run + early[:, _HALF[0]]
                run_buf[h] = run + early[:, _HALF[1]]
                w = jnp.exp(ls_buf[h] + jnp.concatenate([a0, a1], axis=1))
                if masked:
                    w = jnp.where(strict, w, 0.0)
                acc_buf[h] += jnp.dot(w.astype(BF16), v_ref[rows, ln], preferred_element_type=F32)

        block(jd, True)

        def step(it, carry):
            block(jd - 1 - it, False)
            return carry

        lax.fori_loop(0, jd, step, 0)
        o_ref[...] = jnp.concatenate([acc_buf[h] for h in range(ATT_HEADS)], axis=1).astype(o_ref.dtype)
        tot_ref[...] = jnp.concatenate([run_buf[h, :, :HEAD_DIM] for h in range(ATT_HEADS)], axis=1)

    blk = pl.BlockSpec((None, ATT_Q, ATT_LANES), lambda b, p, i: (b, i, p))
    full = pl.BlockSpec((None, seq, ATT_LANES), lambda b, p, i: (b, 0, p))
    tile = (ATT_HEADS, ATT_Q, ATT_KEYS)
    return pl.pallas_call(
        body, name="attention_fwd", grid=(bsz, d // ATT_LANES, seq // ATT_Q),
        in_specs=[blk, full, full], out_specs=[blk, blk],
        out_shape=[jax.ShapeDtypeStruct((bsz, seq, d), BF16), jax.ShapeDtypeStruct((bsz, seq, d), F32)],
        scratch_shapes=[pltpu.VMEM(tile, F32), pltpu.VMEM(tile, F32),
                        pltpu.VMEM((ATT_HEADS, 2, ATT_Q, ATT_KEYS), F32),
                        pltpu.VMEM((ATT_HEADS, ATT_Q, HEAD_DIM), F32),
                        pltpu.VMEM((ATT_HEADS, ATT_Q, ATT_BLOCK), F32)],
        compiler_params=_cparams("arbitrary", "arbitrary", "arbitrary"),
    )(q, k, v)


def attention_bwd(q, k, v, tot, do):
    bsz, seq, d = q.shape

    def body(q_ref, k_ref, v_ref, tot_ref, do_ref, dq_ref, dk_ref, dv_ref,
             z_buf, dw_buf, ls_buf, e_buf, up_buf, bf_buf, w_buf, do_buf, dq_buf, tot_buf, run_buf, erun_buf):
        i = pl.program_id(2)
        jd = ((i + 1) * ATT_Q - 1) // ATT_KEYS

        @pl.when(i == 0)
        def _():
            dk_ref[...] = jnp.zeros_like(dk_ref)
            dv_ref[...] = jnp.zeros_like(dv_ref)

        upto_incl, upto_excl = _sum_matrix(False, True), _sum_matrix(False, False)
        do_buf[...] = do_ref[...].astype(BF16)
        for h, ln in enumerate(_HEADS):
            tot_buf[h] = jnp.concatenate([tot_ref[:, ln], tot_ref[:, ln]], axis=1)
        dq_buf[...] = jnp.zeros_like(dq_buf)
        run_buf[...] = jnp.zeros_like(run_buf)
        erun_buf[...] = jnp.zeros_like(erun_buf)

        def block(j, masked):
            rows = pl.ds(pl.multiple_of(j * ATT_KEYS, ATT_KEYS), ATT_KEYS)
            strict = _strict_mask(i, j) if masked else None
            for h, ln in enumerate(_HEADS):
                z_buf[h] = lax.dot_general(q_ref[:, ln], k_ref[rows, ln], _NT, preferred_element_type=F32)
                dw_buf[h] = lax.dot_general(do_buf[:, ln], v_ref[rows, ln], _NT, preferred_element_type=F32)
            for h in range(ATT_HEADS):
                for half, cols in enumerate(_HALF):
                    ls, lf = _log_sigmoids(z_buf[h, :, cols])
                    if masked:
                        lf = jnp.where(strict[:, cols], lf, 0.0)
                    ls_buf[h, :, cols] = ls
                    up_buf[h, half] = jnp.dot(_hi_lo(lf), upto_incl, preferred_element_type=F32)
            for h in range(ATT_HEADS):
                run = run_buf[h]
                early, late = up_buf[h, 0], up_buf[h, 1]
                u0 = run + early[:, _HALF[0]]
                run = run + early[:, _HALF[1]]
                u1 = run + late[:, _HALF[0]]
                run_buf[h] = run + late[:, _HALF[1]]
                tot_h = tot_buf[h]
                after = jnp.concatenate([tot_h - u0, tot_h - u1], axis=1)
                w = jnp.exp(ls_buf[h] + after)
                if masked:
                    w = jnp.where(strict, w, 0.0)
                w_buf[h] = w.astype(BF16)
                e = dw_buf[h] * w
                e_buf[h] = e
                for half, cols in enumerate(_HALF):
                    bf_buf[h, half] = jnp.dot(_hi_lo(e[:, cols]), upto_excl, preferred_element_type=F32)
            dks, dvs = [], []
            for h, ln in enumerate(_HEADS):
                erun = erun_buf[h]
                early, late = bf_buf[h, 0], bf_buf[h, 1]
                b0 = erun + early[:, _HALF[0]]
                erun = erun + early[:, _HALF[1]]
                b1 = erun + late[:, _HALF[0]]
                erun_buf[h] = erun + late[:, _HALF[1]]
                e = e_buf[h]
                dz = e - jnp.exp(ls_buf[h]) * (e + jnp.concatenate([b0, b1], axis=1))
                if masked:
                    dz = jnp.where(strict, dz, 0.0)
                dz = dz.astype(BF16)
                dq_buf[h] += jnp.dot(dz, k_ref[rows, ln], preferred_element_type=F32)
                dks.append(lax.dot_general(dz, q_ref[:, ln], _TN, preferred_element_type=F32))
                dvs.append(lax.dot_general(w_buf[h], do_buf[:, ln], _TN, preferred_element_type=F32))
            dk_ref[rows, :] += jnp.concatenate(dks, axis=1)
            dv_ref[rows, :] += jnp.concatenate(dvs, axis=1)

        def step(j, carry):
            block(j, False)
            return carry

        lax.fori_loop(0, jd, step, 0)
        block(jd, True)
        dq_ref[...] = jnp.concatenate([dq_buf[h] for h in range(ATT_HEADS)], axis=1) * ATT_SCALE

    blk = pl.BlockSpec((None, ATT_Q, ATT_LANES), lambda b, p, i: (b, i, p))
    full = pl.BlockSpec((None, seq, ATT_LANES), lambda b, p, i: (b, 0, p))
    shape = jax.ShapeDtypeStruct((bsz, seq, d), F32)
    tile = (ATT_HEADS, ATT_Q, ATT_KEYS)
    pair = (ATT_HEADS, 2, ATT_Q, ATT_KEYS)
    square = (ATT_HEADS, ATT_Q, ATT_BLOCK)
    return pl.pallas_call(
        body, name="attention_bwd", grid=(bsz, d // ATT_LANES, seq // ATT_Q),
        in_specs=[blk, full, full, blk, blk], out_specs=[blk, full, full], out_shape=[shape, shape, shape],
        scratch_shapes=[pltpu.VMEM(tile, F32), pltpu.VMEM(tile, F32), pltpu.VMEM(tile, F32), pltpu.VMEM(tile, F32),
                        pltpu.VMEM(pair, F32), pltpu.VMEM(pair, F32), pltpu.VMEM(tile, BF16),
                        pltpu.VMEM((ATT_Q, ATT_LANES), BF16), pltpu.VMEM((ATT_HEADS, ATT_Q, HEAD_DIM), F32),
                        pltpu.VMEM(square, F32), pltpu.VMEM(square, F32), pltpu.VMEM(square, F32)],
        compiler_params=_cparams("arbitrary", "arbitrary", "arbitrary"),
    )(q, k, v, tot, do)


def mlp_fwd(x, g, sh, sc, gate, w1_handle, w2_handle, tag):
    bsz, seq, d = x.shape
    t = bsz * seq
    h = act_call(_norm_mod, [x, g, sh, sc], [(d, BF16, "tile")], tag + "_norm")[0]
    w1 = exchange_wait(w1_handle, h, tag + "_w1_wait")
    act = mm_nn_col(h.reshape(t, d), w1, tag + "_up", (BF16,),
                    lambda acc: (jnp.square(jnp.maximum(acc, 0.0)),))
    w2 = exchange_wait(w2_handle, act, tag + "_w2_wait")
    ff = mm_nn_row(act, w2, tag + "_down").reshape(bsz, seq, d)
    out = act_call(lambda x_, f_, g_: x_ + g_ * f_, [x, ff, gate], [(d, F32, "tile")], tag + "_res")[0]
    return out, (h, act, ff), w1, w2


def mlp_bwd(dout, x, g, sc, gate, w1, w2, saved, tag):
    bsz, seq, d = x.shape
    t = bsz * seq
    ns = w1.shape[0]
    h, act, ff = saved
    dff, dgate = act_call(lambda do_, f_, g_: (g_ * do_, _rowsum(do_ * f_)), [dout, ff, gate],
                          [(d, BF16, "tile"), (d, F32, "seq")], tag + "_dres")
    dff = dff.reshape(t, d)
    dpre = mm_nt_row(dff, w2, tag + "_dact", (BF16,),
                     lambda acc, a_: (acc * (2.0 * jnp.sqrt(a_.astype(F32))),), (act,))
    dw2 = mm_tn(act, dff, "a", ns, tag + "_dw2")
    dw1 = mm_tn(h.reshape(t, d), dpre, "c", ns, tag + "_dw1")
    (dw1_handle, dw2_handle), token = exchange_start([(dw1, True), (dw2, True)], tag + "_dw_start")
    dh = mm_nt_col(dpre, w1, tag + "_dh").reshape(bsz, seq, d)
    dx, dsh, dsc, dg = act_call(_norm_mod_bwd, [x, g + token[0, 0], sc, dh, dout],
                                [(d, F32, "tile"), (d, F32, "seq"), (d, F32, "seq"), (d, F32, "all")],
                                tag + "_dnorm")
    return dx, dw1_handle, dw2_handle, (dsh, dsc, dgate, dg)


def _block_diag(m, rows_first):
    nb, k, r, c = m.shape
    eye = jnp.eye(k, dtype=m.dtype)
    return jnp.einsum("nkrc,kl->nkrlc", m, eye).reshape(nb, k * r, k * c)


def _block_diag_part(m, r, c):
    nb = m.shape[0]
    k = m.shape[1] // r
    return jnp.einsum("nkrlc,kl->nkrc", m.reshape(nb, k, r, k, c), jnp.eye(k, dtype=m.dtype))


def kernel(x, c, ada_w, ada_b, mix_norm_g, mlp_norm_g, mlp_w1, mlp_w2, s5_a_re, s5_a_im, s5_log_dt, s5_b_re, s5_b_im, s5_c_re, s5_c_im, s5_d, s5_w_glu, kv_ada_w, kv_ada_b, kv_norm_g, w_kv, k_norm_g, sb_w_q, q_norm_g, sb_w_o, loss_target, m_ada_w, m_ada_b, m_mix_norm_g, m_mlp_norm_g, m_mlp_w1, m_mlp_w2, m_s5_a_re, m_s5_a_im, m_s5_log_dt, m_s5_b_re, m_s5_b_im, m_s5_c_re, m_s5_c_im, m_s5_d, m_s5_w_glu, m_kv_ada_w, m_kv_ada_b, m_kv_norm_g, m_w_kv, m_k_norm_g, m_sb_w_q, m_q_norm_g, m_sb_w_o, v_ada_w, v_ada_b, v_mix_norm_g, v_mlp_norm_g, v_mlp_w1, v_mlp_w2, v_s5_a_re, v_s5_a_im, v_s5_log_dt, v_s5_b_re, v_s5_b_im, v_s5_c_re, v_s5_c_im, v_s5_d, v_s5_w_glu, v_kv_ada_w, v_kv_ada_b, v_kv_norm_g, v_w_kv, v_k_norm_g, v_sb_w_q, v_q_norm_g, v_sb_w_o):
    bsz, seq, d = x.shape
    t = bsz * seq
    n_groups = d // S5_GROUP
    nb = n_groups // S5_BLOCK_GROUPS
    gp = n_groups * S5_STATE
    dev = 4 * lax.axis_index("x") + 2 * lax.axis_index("y") + lax.axis_index("c")
    e_ada, e_kv = 6 * d, 2 * d
    n_ada, n_kv = e_ada // N_DEV, e_kv // N_DEV

    d_skip = all_gather(s5_d, "gather_skip").reshape(1, 1, d)
    c_all = all_gather(c, "gather_c").reshape(N_DEV * bsz, d)

    w_cols = jnp.concatenate([ada_w[0], ada_w[1], kv_ada_w], axis=1)
    b_cols = jnp.concatenate([
        lax.dynamic_slice_in_dim(ada_b[0], dev * n_ada, n_ada),
        lax.dynamic_slice_in_dim(ada_b[1], dev * n_ada, n_ada),
        lax.dynamic_slice_in_dim(kv_ada_b, dev * n_kv, n_kv)])[None, :]
    mod_cols = ada_fwd(c_all, w_cols, b_cols)
    mod_all = all_gather(mod_cols, "gather_mod")
    mod_mine = lax.dynamic_slice_in_dim(mod_all, dev * bsz, bsz, axis=1)
    mod_mine = jnp.transpose(mod_mine, (1, 0, 2))
    mods = []
    for i in range(2):
        full = mod_mine[:, :, i * n_ada:(i + 1) * n_ada].reshape(bsz, e_ada)
        mods.append([full[:, None, j * d:(j + 1) * d] for j in range(6)])
    kv_full = mod_mine[:, :, 2 * n_ada:].reshape(bsz, e_kv)
    kv_sh, kv_sc = kv_full[:, None, :d], kv_full[:, None, d:]

    par = lambda p: p.reshape(1, 1, -1)

    shards = [s5_w_glu[0], mlp_w1[0], mlp_w2[0], w_kv, sb_w_q[0], sb_w_o[0], mlp_w1[1], mlp_w2[1]]
    gathers, gather_token = exchange_start([(w.astype(BF16), False) for w in shards], "gather_start", after=[mod_all, d_skip])
    glu_handle, w1_0_handle, w2_0_handle, wkv_handle, wq_handle, wo_handle, w1_1_handle, w2_1_handle = gathers
    started = gather_token[0, 0]

    sh_a, sc_a, g_a, sh_m, sc_m, g_m = mods[0]
    lam_re, lam_im = s5_a_re.reshape(gp, 1), s5_a_im.reshape(gp, 1)
    log_dt = jnp.broadcast_to(s5_log_dt.reshape(n_groups, 1), (n_groups, S5_STATE)).reshape(gp, 1)
    b_re, b_im = s5_b_re.reshape(gp, S5_GROUP), s5_b_im.reshape(gp, S5_GROUP)
    ab_re, ab_im, bb_re, bb_im = s5_prep(lam_re, lam_im, log_dt, b_re, b_im)
    to_bbd = lambda m: _block_diag(jnp.swapaxes(m.reshape(nb, S5_BLOCK_GROUPS, S5_STATE, S5_GROUP), 2, 3), True)
    to_cbd = lambda m: _block_diag(jnp.swapaxes(m.reshape(nb, S5_BLOCK_GROUPS, S5_GROUP, S5_STATE), 2, 3), True)
    bbd_re, bbd_im = to_bbd(bb_re).astype(BF16), to_bbd(bb_im).astype(BF16)
    cbd_re, cbd_im = to_cbd(s5_c_re[0]).astype(BF16), to_cbd(s5_c_im[0]).astype(BF16)
    abr, abi = ab_re.reshape(nb, 1, -1), ab_im.reshape(nb, 1, -1)

    h0 = act_call(_norm_mod, [x, par(mix_norm_g[0]) + started, sh_a, sc_a], [(d, F32, "tile")], "mix0_norm")[0]
    y, st_re, st_im = s5_fwd(h0, bbd_re, bbd_im, cbd_re, cbd_im, abr, abi, d_skip)
    ge = act_call(lambda y_: jax.nn.gelu(y_), [y], [(d, BF16, "tile")], "gelu")[0]
    w_glu = exchange_wait(glu_handle, ge, "glu_w_wait")
    z = mm_nn_col(ge.reshape(t, d), w_glu, "glu_up").reshape(bsz, seq, 2 * d)
    x1 = act_call(lambda x_, z_, g_: x_ + g_ * (z_[:, :d] * jax.nn.sigmoid(z_[:, d:])), [x, z, g_a],
                  [(d, F32, "tile")], "glu_res")[0]
    x2, mlp0_saved, w1_0, w2_0 = mlp_fwd(x1, par(mlp_norm_g[0]), sh_m, sc_m, g_m, w1_0_handle, w2_0_handle, "mlp0")

    sh_a1, sc_a1, g_a1, sh_m1, sc_m1, g_m1 = mods[1]
    kg = par(jnp.tile(k_norm_g, d // HEAD_DIM))
    qg = par(jnp.tile(q_norm_g[0], d // HEAD_DIM))
    hkv = act_call(_norm_mod, [x2, par(kv_norm_g), kv_sh, kv_sc], [(d, BF16, "tile")], "kv_norm")[0]
    wkv = exchange_wait(wkv_handle, hkv, "kv_w_wait")
    kvf = mm_nn_col(hkv.reshape(t, d), wkv, "kv_proj").reshape(bsz, seq, 2 * d)
    k_h, v_h = act_call(lambda kv_, g_: (_head_norm(kv_[:, :d], g_), kv_[:, d:]), [kvf, kg],
                        [(d, BF16, "tile"), (d, BF16, "tile")], "k_norm")
    h1 = act_call(_norm_mod, [x2, par(mix_norm_g[1]), sh_a1, sc_a1], [(d, BF16, "tile")], "mix1_norm")[0]
    wq = exchange_wait(wq_handle, h1, "q_w_wait")
    q_raw = mm_nn_row(h1.reshape(t, d), wq, "q_proj").reshape(bsz, seq, d)
    q_h = act_call(lambda x_, g_: _head_norm(x_, g_) * ATT_SCALE, [q_raw, qg], [(d, BF16, "tile")], "q_norm")[0]
    o, att_tot = attention_fwd(q_h, k_h, v_h)
    wo = exchange_wait(wo_handle, o, "o_w_wait")
    mix1 = mm_nn_row(o.reshape(t, d), wo, "o_proj").reshape(bsz, seq, d)
    x3 = act_call(lambda x_, f_, g_: x_ + g_ * f_, [x2, mix1, g_a1], [(d, F32, "tile")], "att_res")[0]
    x4, mlp1_saved, w1_1, w2_1 = mlp_fwd(x3, par(mlp_norm_g[1]), sh_m1, sc_m1, g_m1, w1_1_handle, w2_1_handle, "mlp1")

    def loss_fn(y_, t_):
        diff = y_ - t_
        part = jnp.sum(0.5 * jnp.mean(diff * diff, axis=-1, keepdims=True), axis=0, keepdims=True)
        return jnp.broadcast_to(part, (1, LANES)), diff * (1.0 / d)

    loss_part, dx4 = act_call(loss_fn, [x4, loss_target], [(LANES, F32, "all"), (d, F32, "tile")], "loss")
    loss = lax.psum(loss_part[0, 0, 0], ("x", "y", "c"))

    dx3, dw1_1, dw2_1, (dsh_m1, dsc_m1, dg_m1, dgn_mlp1) = mlp_bwd(
        dx4, x3, par(mlp_norm_g[1]), sc_m1, g_m1, w1_1, w2_1, mlp1_saved, "mlp1")
    dmix1, dg_a1 = act_call(lambda do_, f_, g_: (g_ * do_, _rowsum(do_ * f_)), [dx3, mix1, g_a1],
                            [(d, BF16, "tile"), (d, F32, "seq")], "att_dres")
    dmix1 = dmix1.reshape(t, d)
    do = mm_nt_row(dmix1, wo, "o_dproj").reshape(bsz, seq, d)
    dwo = mm_tn(o.reshape(t, d), dmix1, "a", N_DEV, "o_dw")
    dq, dk, dv = attention_bwd(q_h, k_h, v_h, att_tot, do)
    dq_raw, dqg = act_call(_head_norm_bwd, [q_raw, qg, dq], [(d, BF16, "tile"), (d, F32, "all")], "q_dnorm")
    dq_raw = dq_raw.reshape(t, d)
    dh1 = mm_nt_row(dq_raw, wq, "q_dproj").reshape(bsz, seq, d)
    dwq = mm_tn(h1.reshape(t, d), dq_raw, "a", N_DEV, "q_dw")
    dx2, dsh_a1, dsc_a1, dgn_mix1 = act_call(
        _norm_mod_bwd, [x2, par(mix_norm_g[1]), sc_a1, dh1, dx3],
        [(d, F32, "tile"), (d, F32, "seq"), (d, F32, "seq"), (d, F32, "all")], "mix1_dnorm")

    def kv_bwd_fn(kv_, g_, dk_, dv_):
        dk_raw, dg_ = _head_norm_bwd(kv_[:, :d], g_, dk_)
        return jnp.concatenate([dk_raw, dv_], axis=1), dg_

    dkvf, dkg = act_call(kv_bwd_fn, [kvf, kg, dk, dv], [(2 * d, BF16, "tile"), (d, F32, "all")], "k_dnorm")
    dkvf = dkvf.reshape(t, 2 * d)
    dhkv = mm_nt_col(dkvf, wkv, "kv_dproj").reshape(bsz, seq, d)
    dwkv = mm_tn(hkv.reshape(t, d), dkvf, "c", N_DEV, "kv_dw")
    (dwo, dwq, dwkv), att_token = exchange_start([(dwo, True), (dwq, True), (dwkv, True)], "att_dw_start")
    dx2, dkv_sh, dkv_sc, dgn_kv = act_call(
        _norm_mod_bwd, [x2, par(kv_norm_g) + att_token[0, 0], kv_sc, dhkv, dx2],
        [(d, F32, "tile"), (d, F32, "seq"), (d, F32, "seq"), (d, F32, "all")], "kv_dnorm")

    dx1, dw1_0, dw2_0, (dsh_m0, dsc_m0, dg_m0, dgn_mlp0) = mlp_bwd(
        dx2, x1, par(mlp_norm_g[0]), sc_m, g_m, w1_0, w2_0, mlp0_saved, "mlp0")

    def glu_bwd_fn(do_, z_, g_):
        val, sig = z_[:, :d], jax.nn.sigmoid(z_[:, d:])
        dmix = g_ * do_
        dz = jnp.concatenate([dmix * sig, dmix * val * sig * (1.0 - sig)], axis=1)
        return dz, _rowsum(do_ * (val * sig))

    dz, dg_a0 = act_call(glu_bwd_fn, [dx1, z, g_a], [(2 * d, BF16, "tile"), (d, F32, "seq")], "glu_dres")
    dz = dz.reshape(t, 2 * d)
    dy = mm_nt_col(dz, w_glu, "glu_dup", (F32,), lambda acc, y_: (acc * _gelu_grad(y_),),
                   (y.reshape(t, d),)).reshape(bsz, seq, d)
    dwglu = mm_tn(ge.reshape(t, d), dz, "c", N_DEV, "glu_dw")
    (dwglu,), glu_token = exchange_start([(dwglu, True)], "glu_dw_start")
    du, dbbd_re, dbbd_im, dcbd_re, dcbd_im, dab_re, dab_im, dd_skip = s5_bwd(
        dy, h0, st_re, st_im, bbd_re, bbd_im, cbd_re, cbd_im, abr, abi, d_skip + glu_token[0, 0])
    dx0, dsh_a0, dsc_a0, dgn_mix0 = act_call(
        _norm_mod_bwd, [x, par(mix_norm_g[0]), sc_a, du, dx1],
        [(d, F32, "tile"), (d, F32, "seq"), (d, F32, "seq"), (d, F32, "all")], "mix0_dnorm")

    from_bbd = lambda m: jnp.swapaxes(_block_diag_part(m, S5_GROUP, S5_STATE), 2, 3).reshape(gp, S5_GROUP)
    d_c = lambda m: jnp.swapaxes(_block_diag_part(m, S5_STATE, S5_GROUP), 2, 3).reshape(
        1, n_groups, S5_GROUP, S5_STATE)
    d_lam_re, d_lam_im, d_log_dt, d_b_re, d_b_im = s5_prep_bwd(
        lam_re, lam_im, log_dt, b_re, b_im, dab_re.reshape(gp, 1), dab_im.reshape(gp, 1),
        from_bbd(dbbd_re), from_bbd(dbbd_im))

    small = all_reduce_small([
        jnp.stack([dgn_mix0.reshape(d), dgn_mix1.reshape(d)]),
        jnp.stack([dgn_mlp0.reshape(d), dgn_mlp1.reshape(d)]),
        d_lam_re.reshape(1, n_groups, S5_STATE), d_lam_im.reshape(1, n_groups, S5_STATE),
        d_log_dt.reshape(1, n_groups, S5_STATE).sum(axis=-1),
        d_b_re.reshape(s5_b_re.shape), d_b_im.reshape(s5_b_im.shape),
        d_c(dcbd_re), d_c(dcbd_im),
        dd_skip.reshape(1, d),
        dgn_kv.reshape(d),
        dkg.reshape(d // HEAD_DIM, HEAD_DIM).sum(axis=0),
        dqg.reshape(d // HEAD_DIM, HEAD_DIM).sum(axis=0)[None, :],
    ], "small_grads")
    (g_mix_norm, g_mlp_norm, g_a_re, g_a_im, g_log_dt, g_b_re, g_b_im, g_c_re, g_c_im,
     g_skip_full, g_kv_norm, g_k_norm, g_q_norm) = small
    g_s5_d = lax.dynamic_slice_in_dim(g_skip_full, dev * (d // N_DEV), d // N_DEV, axis=1)

    dm_mine = jnp.concatenate([
        dsh_a0, dsc_a0, dg_a0, dsh_m0, dsc_m0, dg_m0,
        dsh_a1, dsc_a1, dg_a1, dsh_m1, dsc_m1, dg_m1, dkv_sh, dkv_sc], axis=2).reshape(bsz, 2 * e_ada + e_kv)
    dm_all = all_gather(dm_mine, "gather_dmod").reshape(N_DEV * bsz, 2 * e_ada + e_kv)
    dm_cols = jnp.concatenate([
        lax.dynamic_slice_in_dim(dm_all, dev * n_ada, n_ada, axis=1),
        lax.dynamic_slice_in_dim(dm_all, e_ada + dev * n_ada, n_ada, axis=1),
        lax.dynamic_slice_in_dim(dm_all, 2 * e_ada + dev * n_kv, n_kv, axis=1)], axis=1)
    dw_cols, db_all = ada_bwd(c_all, dm_cols, dm_all)
    g_ada_w = jnp.stack([dw_cols[:, :n_ada], dw_cols[:, n_ada:2 * n_ada]])
    g_kv_ada_w = dw_cols[:, 2 * n_ada:]
    g_ada_b = db_all[0, :2 * e_ada].reshape(2, e_ada)
    g_kv_ada_b = db_all[0, 2 * e_ada:]

    landed = lambda handle, name: slab_sum(exchange_wait(handle, dx0, name + "_wait"), name + "_sum")
    g_w1 = jnp.stack([landed(dw1_0, "rs_w1_0"), landed(dw1_1, "rs_w1_1")])
    g_w2 = jnp.stack([landed(dw2_0, "rs_w2_0"), landed(dw2_1, "rs_w2_1")])
    g_glu = landed(dwglu, "rs_glu")[None]
    g_wkv = landed(dwkv, "rs_wkv")
    g_wq = landed(dwq, "rs_wq")[None]
    g_wo = landed(dwo, "rs_wo")[None]

    weights = [ada_w, ada_b, mix_norm_g, mlp_norm_g, mlp_w1, mlp_w2, s5_a_re, s5_a_im, s5_log_dt, s5_b_re,
               s5_b_im, s5_c_re, s5_c_im, s5_d, s5_w_glu, kv_ada_w, kv_ada_b, kv_norm_g, w_kv, k_norm_g,
               sb_w_q, q_norm_g, sb_w_o]
    grads = [g_ada_w, g_ada_b, g_mix_norm, g_mlp_norm, g_w1, g_w2, g_a_re, g_a_im, g_log_dt, g_b_re,
             g_b_im, g_c_re, g_c_im, g_s5_d, g_glu, g_kv_ada_w, g_kv_ada_b, g_kv_norm, g_wkv, g_k_norm,
             g_wq, g_q_norm, g_wo]
    ms = [m_ada_w, m_ada_b, m_mix_norm_g, m_mlp_norm_g, m_mlp_w1, m_mlp_w2, m_s5_a_re, m_s5_a_im, m_s5_log_dt,
          m_s5_b_re, m_s5_b_im, m_s5_c_re, m_s5_c_im, m_s5_d, m_s5_w_glu, m_kv_ada_w, m_kv_ada_b, m_kv_norm_g,
          m_w_kv, m_k_norm_g, m_sb_w_q, m_q_norm_g, m_sb_w_o]
    vs = [v_ada_w, v_ada_b, v_mix_norm_g, v_mlp_norm_g, v_mlp_w1, v_mlp_w2, v_s5_a_re, v_s5_a_im, v_s5_log_dt,
          v_s5_b_re, v_s5_b_im, v_s5_c_re, v_s5_c_im, v_s5_d, v_s5_w_glu, v_kv_ada_w, v_kv_ada_b, v_kv_norm_g,
          v_w_kv, v_k_norm_g, v_sb_w_q, v_q_norm_g, v_sb_w_o]
    grads = [g.reshape(w.shape) for g, w in zip(grads, weights)]
    deltas, new_ms, new_vs = [], [], []
    for i, (w, g, m, v) in enumerate(zip(weights, grads, ms, vs)):
        dl, nm, nv = adamw(w, g, m, v, f"adamw_{i}")
        deltas.append(dl)
        new_ms.append(nm)
        new_vs.append(nv)
    return (loss, dx0, *grads, *deltas, *new_ms, *new_vs)
```

```python
import functools
import math

import jax
import jax.numpy as jnp
from jax import lax
from jax.experimental import pallas as pl
from jax.experimental.pallas import tpu as pltpu

F32 = jnp.float32
BF16 = jnp.bfloat16

N_DEV = 8
N_CHIPS = 4
MESH = pl.DeviceIdType.MESH
ANY = pl.BlockSpec(memory_space=pl.ANY)

LANES = 128
VMEM_LIMIT_BYTES = 48 * 2 ** 20
TILE_BUDGET_BYTES = 4 * 2 ** 20

S5_GROUP = 16
S5_STATE = 64
S5_BLOCK_GROUPS = 16
HEAD_DIM = 64
ATT_BLOCK = 128
EPS = 1e-6

ADAM_LR = 0.001
ADAM_B1 = 0.9
ADAM_B2 = 0.999
ADAM_EPS = 1e-08
ADAM_WD = 0.01
ADAM_STEP = 10


def _cparams(*sem):
    return pltpu.CompilerParams(dimension_semantics=sem, vmem_limit_bytes=VMEM_LIMIT_BYTES)


def _divisor_tile(n, limit, mult):
    best = None
    for t in range(mult, min(n, limit) + 1, mult):
        if n % t == 0:
            best = t
    return best if best is not None else n


def _tile_m(m):
    return _divisor_tile(m, 2048 if m >= 4096 else 256, 16)


def all_gather(x, name):
    def body(x_ref, out_ref, send_sems, recv_sems, local_sem):
        ax, ay, ac = lax.axis_index("x"), lax.axis_index("y"), lax.axis_index("c")
        me, sibling = (ax, ay, ac), (ax, ay, 1 - ac)
        chips = [(1 - ax, ay), (ax, 1 - ay), (1 - ax, 1 - ay)]

        def slot(px, py, pc):
            return out_ref.at[4 * px + 2 * py + pc]

        def copy(k, block, to, src=None):
            return pltpu.make_async_remote_copy(
                src_ref=slot(*block) if src is None else src, dst_ref=slot(*block),
                send_sem=send_sems.at[k], recv_sem=recv_sems.at[k], device_id=to, device_id_type=MESH)

        mine = pltpu.make_async_copy(x_ref, slot(*me), local_sem)
        mine.start()
        first = [copy(0, me, sibling, src=x_ref)]
        first += [copy(1 + j, me, (*chip, ac), src=x_ref) for j, chip in enumerate(chips)]
        for cp in first:
            cp.start()
        passed = [copy(4 + j, (*chip, ac), sibling) for j, chip in enumerate(chips)]
        for j, chip in enumerate(chips):
            copy(1 + j, (*chip, ac), me).wait_recv()
            passed[j].start()
        copy(0, sibling, me).wait_recv()
        for j, chip in enumerate(chips):
            copy(4 + j, (*chip, 1 - ac), me).wait_recv()
        for cp in first + passed:
            cp.wait_send()
        mine.wait()

    return pl.pallas_call(
        body, name=name,
        out_shape=jax.ShapeDtypeStruct((N_DEV,) + x.shape, x.dtype),
        in_specs=[ANY], out_specs=ANY,
        scratch_shapes=[pltpu.SemaphoreType.DMA((7,)), pltpu.SemaphoreType.DMA((7,)), pltpu.SemaphoreType.DMA],
    )(x)


HBM = pl.BlockSpec(memory_space=pltpu.HBM)
SEM = pl.BlockSpec(memory_space=pltpu.SEMAPHORE)
N_PEERS = N_DEV - 1


def _peers():
    ax, ay, ac = lax.axis_index("x"), lax.axis_index("y"), lax.axis_index("c")
    flip = lambda v, bit: 1 - v if bit else v
    return [(flip(ax, k & 4), flip(ay, k & 2), flip(ac, k & 1)) for k in range(1, N_DEV)]


def _dev_index(pos):
    return 4 * pos[0] + 2 * pos[1] + pos[2]


def exchange_start(items, name, after=None):
    n = len(items)
    srcs = [a for a, _ in items]
    blocks = [a.shape[1:] if scatter else a.shape for a, scatter in items]
    extra = list(after or ())

    def body(*refs):
        src_refs, land_refs = refs[:n], refs[n:2 * n]
        outs = refs[2 * n + len(extra):]
        send_sems, recv_sems = outs[:n], outs[n:2 * n]
        token = outs[-1]
        me = _dev_index((lax.axis_index("x"), lax.axis_index("y"), lax.axis_index("c")))
        for w, (_, scatter) in enumerate(items):
            for k, peer in enumerate(_peers()):
                src = src_refs[w].at[_dev_index(peer)] if scatter else src_refs[w]
                pltpu.make_async_remote_copy(
                    src_ref=src, dst_ref=land_refs[w].at[me], send_sem=send_sems[w].at[k],
                    recv_sem=recv_sems[w].at[k], device_id=peer, device_id_type=MESH).start()
        token[...] = jnp.zeros_like(token)

    lands = [lax.empty((N_DEV,) + blk, a.dtype) for a, blk in zip(srcs, blocks)]
    res = pl.pallas_call(
        body, name=name,
        out_shape=([pltpu.SemaphoreType.DMA((N_PEERS,))] * (2 * n)
                   + [pltpu.HBM(a.shape, a.dtype) for a in srcs] + [pltpu.HBM(l.shape, l.dtype) for l in lands]
                   + [jax.ShapeDtypeStruct((8, LANES), F32)]),
        in_specs=[HBM] * (2 * n) + [ANY] * len(extra),
        out_specs=[SEM] * (2 * n) + [HBM] * (2 * n) + [pl.BlockSpec(memory_space=pltpu.VMEM)],
        input_output_aliases={i: 2 * n + i for i in range(2 * n)},
        compiler_params=pltpu.CompilerParams(has_side_effects=pltpu.SideEffectType.DATAFLOW_SIDE_EFFECTING),
    )(*[pltpu.with_memory_space_constraint(a, pltpu.HBM) for a in srcs + lands], *extra)
    handles = [(res[w], res[n + w], res[2 * n + w], res[3 * n + w], scatter) for w, (_, scatter) in enumerate(items)]
    return handles, res[-1]


def exchange_wait(handle, after, name):
    send_sem, recv_sem, src, land, scatter = handle

    def body(src_ref, land_ref, send_ref, recv_ref, after_ref, src_out, land_out):
        for k, peer in enumerate(_peers()):
            slot = _dev_index(peer)
            copy = pltpu.make_async_remote_copy(
                src_ref=src_ref.at[slot] if scatter else src_ref, dst_ref=land_ref.at[slot],
                send_sem=send_ref.at[k], recv_sem=recv_ref.at[k], device_id=peer, device_id_type=MESH)
            copy.wait_send()
            copy.wait_recv()

    src, landed = pl.pallas_call(
        body, name=name,
        out_shape=(pltpu.HBM(src.shape, src.dtype), pltpu.HBM(land.shape, land.dtype)),
        in_specs=[HBM, HBM, SEM, SEM, ANY], out_specs=(HBM, HBM), input_output_aliases={0: 0, 1: 1},
        compiler_params=pltpu.CompilerParams(has_side_effects=pltpu.SideEffectType.DATAFLOW_SIDE_EFFECTING),
    )(src, land, send_sem, recv_sem, after)
    dev = _dev_index((lax.axis_index("x"), lax.axis_index("y"), lax.axis_index("c")))
    own = lax.dynamic_index_in_dim(src, dev, axis=0, keepdims=True) if scatter else src[None]
    return lax.dynamic_update_slice_in_dim(landed, own, dev, axis=0)


def rows_call(fn, ins, outs, name):
    rows = ins[0].shape[1]
    per_row = sum(a.shape[0] * a.shape[2] * a.dtype.itemsize for a in ins)
    per_row += sum(l * c * jnp.dtype(dt).itemsize for l, c, dt in outs)
    tr = _divisor_tile(rows, max(16, TILE_BUDGET_BYTES // per_row), 16)
    n_in = len(ins)

    def body(*refs):
        vals = fn(*[r[...] for r in refs[:n_in]])
        if not isinstance(vals, (tuple, list)):
            vals = (vals,)
        for r, v in zip(refs[n_in:], vals):
            r[...] = v.astype(r.dtype)

    def spec(l, c):
        return pl.BlockSpec((l, tr, c), lambda i: (0, i, 0))

    res = pl.pallas_call(
        body, name=name, grid=(rows // tr,),
        in_specs=[spec(a.shape[0], a.shape[2]) for a in ins],
        out_specs=[spec(l, c) for l, c, _ in outs],
        out_shape=[jax.ShapeDtypeStruct((l, rows, c), dt) for l, c, dt in outs],
        compiler_params=_cparams("arbitrary"),
    )(*ins)
    return res


def _as_rows(a, lead=0):
    shape = a.shape
    l = int(math.prod(shape[:lead])) if lead else 1
    rest = shape[lead:]
    c = rest[-1] if rest else 1
    r = int(math.prod(rest[:-1])) if len(rest) > 1 else 1
    return a.reshape(l, r, c)


def act_call(fn, ins, outs, name):
    bsz, seq = ins[0].shape[0], ins[0].shape[1]
    per_row = sum(a.shape[2] * a.dtype.itemsize for a in ins if a.shape[1] == seq)
    per_row += sum(c * jnp.dtype(dt).itemsize for c, dt, kind in outs if kind == "tile")
    ts = _divisor_tile(seq, max(16, TILE_BUDGET_BYTES // per_row), 16)
    n_in = len(ins)

    def in_spec(a):
        c = a.shape[2]
        if a.shape[1] == seq:
            return pl.BlockSpec((None, ts, c), lambda b, s: (b, s, 0))
        if a.shape[0] == bsz:
            return pl.BlockSpec((None, 1, c), lambda b, s: (b, 0, 0))
        return pl.BlockSpec((None, 1, c), lambda b, s: (0, 0, 0))

    def out_spec(c, kind):
        if kind == "tile":
            return pl.BlockSpec((None, ts, c), lambda b, s: (b, s, 0))
        if kind == "seq":
            return pl.BlockSpec((None, 1, c), lambda b, s: (b, 0, 0))
        return pl.BlockSpec((None, 1, c), lambda b, s: (0, 0, 0))

    def out_shape(c, dt, kind):
        if kind == "tile":
            return jax.ShapeDtypeStruct((bsz, seq, c), dt)
        return jax.ShapeDtypeStruct((bsz if kind == "seq" else 1, 1, c), dt)

    def accumulate(ref, v, first):
        @pl.when(first)
        def _():
            ref[...] = jnp.zeros_like(ref)

        ref[...] += v.astype(ref.dtype)

    def body(*refs):
        b, s = pl.program_id(0), pl.program_id(1)
        vals = fn(*[r[...] for r in refs[:n_in]])
        if not isinstance(vals, (tuple, list)):
            vals = (vals,)
        for ref, v, (_, _, kind) in zip(refs[n_in:], vals, outs):
            if kind == "tile":
                ref[...] = v.astype(ref.dtype)
            elif kind == "seq":
                accumulate(ref, v, s == 0)
            else:
                accumulate(ref, v, jnp.logical_and(b == 0, s == 0))

    return pl.pallas_call(
        body, name=name, grid=(bsz, seq // ts),
        in_specs=[in_spec(a) for a in ins],
        out_specs=[out_spec(c, kind) for c, _, kind in outs],
        out_shape=[out_shape(*o) for o in outs],
        compiler_params=_cparams("arbitrary", "arbitrary"),
    )(*ins)


def _mm(name, grid, a, a_spec, b, b_spec, dims, out_shape, out_spec, out_dtypes, acc_steps,
        epi=None, extras=(), extra_spec=None):
    n_ex, n_out = len(extras), len(out_dtypes)
    tile = tuple(d for d in out_spec.block_shape if d is not None)

    def body(*refs):
        a_ref, b_ref = refs[0], refs[1]
        ex_refs = refs[2:2 + n_ex]
        o_refs = refs[2 + n_ex:2 + n_ex + n_out]
        p = lax.dot_general(a_ref[...].astype(BF16), b_ref[...].astype(BF16), (dims, ((), ())),
                            preferred_element_type=F32)

        def finish(acc):
            vals = epi(acc, *[r[...] for r in ex_refs]) if epi is not None else (acc,) * n_out
            for r, v in zip(o_refs, vals):
                r[...] = v.astype(r.dtype)

        if not acc_steps:
            finish(p)
        else:
            acc_ref = refs[-1]
            s = pl.program_id(1)

            @pl.when(s == 0)
            def _():
                acc_ref[...] = p

            @pl.when(s > 0)
            def _():
                acc_ref[...] += p

            @pl.when(s == acc_steps - 1)
            def _():
                finish(acc_ref[...])

    res = pl.pallas_call(
        body, name=name, grid=grid,
        in_specs=[a_spec, b_spec] + [extra_spec] * n_ex,
        out_specs=[out_spec] * n_out,
        out_shape=[jax.ShapeDtypeStruct(out_shape, dt) for dt in out_dtypes],
        scratch_shapes=[pltpu.VMEM(tile, F32)] if acc_steps else [],
        compiler_params=_cparams("arbitrary", "arbitrary"),
    )(a, b, *extras)
    return res if n_out > 1 else res[0]


def mm_nn_col(a, w, name, out_dtypes=(F32,), epi=None):
    m, k = a.shape
    ns, _, nb = w.shape
    tm = _tile_m(m)
    return _mm(name, (m // tm, ns), a, pl.BlockSpec((tm, k), lambda i, j: (i, 0)),
               w, pl.BlockSpec((None, k, nb), lambda i, j: (j, 0, 0)), ((1,), (0,)),
               (m, ns * nb), pl.BlockSpec((tm, nb), lambda i, j: (i, j)), out_dtypes, 0, epi)


def mm_nn_row(a, w, name, out_dtypes=(F32,)):
    m = a.shape[0]
    ns, kb, n = w.shape
    tm = _tile_m(m)
    return _mm(name, (m // tm, ns), a, pl.BlockSpec((tm, kb), lambda i, s: (i, s)),
               w, pl.BlockSpec((None, kb, n), lambda i, s: (s, 0, 0)), ((1,), (0,)),
               (m, n), pl.BlockSpec((tm, n), lambda i, s: (i, 0)), out_dtypes, ns)


def mm_nt_col(dc, w, name, out_dtypes=(F32,), epi=None, extras=()):
    m = dc.shape[0]
    ns, k, nb = w.shape
    tm = _tile_m(m) // 2 if extras else _tile_m(m)
    spec = pl.BlockSpec((tm, k), lambda i, s: (i, 0))
    return _mm(name, (m // tm, ns), dc, pl.BlockSpec((tm, nb), lambda i, s: (i, s)),
               w, pl.BlockSpec((None, k, nb), lambda i, s: (s, 0, 0)), ((1,), (1,)),
               (m, k), spec, out_dtypes, ns, epi, extras, spec)


def mm_nt_row(dc, w, name, out_dtypes=(F32,), epi=None, extras=()):
    m, n = dc.shape
    ns, kb, _ = w.shape
    tm = _tile_m(m)
    spec = pl.BlockSpec((tm, kb), lambda i, s: (i, s))
    return _mm(name, (m // tm, ns), dc, pl.BlockSpec((tm, n), lambda i, s: (i, 0)),
               w, pl.BlockSpec((None, kb, n), lambda i, s: (s, 0, 0)), ((1,), (1,)),
               (m, ns * kb), spec, out_dtypes, 0, epi, extras, spec)


def mm_tn(a, c, slab, ns, name, out_dtype=BF16):
    m, ka_all = a.shape
    nc_all = c.shape[1]
    ka = ka_all // ns if slab == "a" else ka_all
    nc = nc_all // ns if slab == "c" else nc_all
    tt = _divisor_tile(m, 256, 16)
    steps = m // tt

    def body(a_ref, c_ref, o_ref, acc_ref):
        t = pl.program_id(0)

        @pl.when(t == 0)
        def _():
            acc_ref[...] = jnp.zeros_like(acc_ref)

        for s in range(ns):
            a_s = a_ref[:, s * ka:(s + 1) * ka] if slab == "a" else a_ref[...]
            c_s = c_ref[:, s * nc:(s + 1) * nc] if slab == "c" else c_ref[...]
            acc_ref[s] += lax.dot_general(a_s.astype(BF16), c_s.astype(BF16), (((0,), (0,)), ((), ())),
                                          preferred_element_type=F32)

        @pl.when(t == steps - 1)
        def _():
            o_ref[...] = acc_ref[...].astype(o_ref.dtype)

    return pl.pallas_call(
        body, name=name, grid=(steps,),
        in_specs=[pl.BlockSpec((tt, ka_all), lambda t: (t, 0)), pl.BlockSpec((tt, nc_all), lambda t: (t, 0))],
        out_specs=pl.BlockSpec((ns, ka, nc), lambda t: (0, 0, 0)),
        out_shape=jax.ShapeDtypeStruct((ns, ka, nc), out_dtype),
        scratch_shapes=[pltpu.VMEM((ns, ka, nc), F32)],
        compiler_params=_cparams("arbitrary"),
    )(a, c)


def slab_sum(landed, name):
    shape = landed.shape[1:]
    total = rows_call(lambda g: jnp.sum(g.astype(F32), axis=0, keepdims=True),
                      [_as_rows(landed, 1)], [(1, shape[-1], F32)], name)[0]
    return total.reshape(shape)


def all_reduce_small(leaves, name):
    sizes = [int(a.size) for a in leaves]
    flat = jnp.concatenate([a.reshape(-1) for a in leaves])
    total = int(flat.size)
    padded = -(-total // (16 * LANES)) * (16 * LANES)
    flat = jnp.pad(flat, (0, padded - total)).reshape(padded // LANES, LANES)
    gathered = all_gather(flat, name + "_gather")
    summed = rows_call(lambda g: jnp.sum(g, axis=0, keepdims=True), [gathered],
                       [(1, LANES, F32)], name + "_sum")[0].reshape(-1)
    out, at = [], 0
    for a, n in zip(leaves, sizes):
        out.append(summed[at:at + n].reshape(a.shape))
        at += n
    return out


def adamw(w, g, m, v, name):
    c = w.shape[-1] if w.ndim else 1

    def fn(w_, g_, m_, v_):
        nm = ADAM_B1 * m_ + (1.0 - ADAM_B1) * g_
        nv = ADAM_B2 * v_ + (1.0 - ADAM_B2) * (g_ * g_)
        m_hat = nm / (1.0 - ADAM_B1 ** ADAM_STEP)
        v_hat = nv / (1.0 - ADAM_B2 ** ADAM_STEP)
        delta = -ADAM_LR * (m_hat / (jnp.sqrt(v_hat) + ADAM_EPS) + ADAM_WD * w_)
        return delta, nm, nv

    res = rows_call(fn, [_as_rows(t) for t in (w, g.astype(F32), m, v)], [(1, c, F32)] * 3, name)
    return tuple(r.reshape(w.shape) for r in res)


def _rowsum(v):
    return jnp.sum(v, axis=0, keepdims=True)


def _norm_mod(x, g, sh, sc):
    n = x * lax.rsqrt(jnp.mean(x * x, axis=-1, keepdims=True) + EPS)
    return (n * g) * (1.0 + sc) + sh


def _norm_mod_bwd(x, g, sc, dh, dres):
    r = lax.rsqrt(jnp.mean(x * x, axis=-1, keepdims=True) + EPS)
    n = x * r
    dy = dh * (1.0 + sc)
    dn = dy * g
    dx = r * (dn - n * jnp.mean(dn * n, axis=-1, keepdims=True))
    return dres + dx, _rowsum(dh), _rowsum(dh * (n * g)), _rowsum(dy * n)


def _head_mean(v):
    low = lax.broadcasted_iota(jnp.int32, (1, LANES), 1) < HEAD_DIM
    parts = []
    for p in range(v.shape[1] // LANES):
        blk = v[:, p * LANES:(p + 1) * LANES]
        s0 = jnp.sum(jnp.where(low, blk, 0.0), axis=-1, keepdims=True)
        s1 = jnp.sum(jnp.where(low, 0.0, blk), axis=-1, keepdims=True)
        parts.append(jnp.where(low, s0, s1))
    return jnp.concatenate(parts, axis=1) * (1.0 / HEAD_DIM)


def _head_norm(x, g):
    return x * lax.rsqrt(_head_mean(x * x) + EPS) * g


def _head_norm_bwd(x, g, dy):
    r = lax.rsqrt(_head_mean(x * x) + EPS)
    n = x * r
    dn = dy * g
    return r * (dn - n * _head_mean(dn * n)), _rowsum(dy * n)


GELU_C = math.sqrt(2.0 / math.pi)
GELU_A = 0.044715


def _gelu_grad(y):
    t = jnp.tanh(GELU_C * (y + GELU_A * y * y * y))
    return 0.5 * (1.0 + t) + 0.5 * y * (1.0 - t * t) * GELU_C * (1.0 + 3.0 * GELU_A * y * y)


def ada_fwd(c_all, w_cols, b_cols):
    def body(c_ref, w_ref, b_ref, o_ref):
        c = c_ref[...]
        s = (c * jax.nn.sigmoid(c)).astype(BF16)
        o_ref[...] = jnp.dot(s, w_ref[...].astype(BF16), preferred_element_type=F32) + b_ref[...]

    return pl.pallas_call(
        body, name="ada_fwd", out_shape=jax.ShapeDtypeStruct((c_all.shape[0], w_cols.shape[1]), F32),
        compiler_params=pltpu.CompilerParams(vmem_limit_bytes=VMEM_LIMIT_BYTES),
    )(c_all, w_cols, b_cols)


def ada_bwd(c_all, dm_cols, dm_all):
    def body(c_ref, d_ref, all_ref, dw_ref, db_ref):
        c = c_ref[...]
        s = (c * jax.nn.sigmoid(c)).astype(BF16)
        dw_ref[...] = lax.dot_general(s, d_ref[...].astype(BF16), (((0,), (0,)), ((), ())),
                                      preferred_element_type=F32)
        db_ref[...] = jnp.sum(all_ref[...], axis=0, keepdims=True)

    return pl.pallas_call(
        body, name="ada_bwd",
        out_shape=[jax.ShapeDtypeStruct((c_all.shape[1], dm_cols.shape[1]), F32),
                   jax.ShapeDtypeStruct((1, dm_all.shape[1]), F32)],
        compiler_params=pltpu.CompilerParams(vmem_limit_bytes=VMEM_LIMIT_BYTES),
    )(c_all, dm_cols, dm_all)


def _s5_discretise(lam_re, lam_im, log_dt, b_re, b_im):
    dt = jnp.exp(log_dt)
    mag = jnp.exp(lam_re * dt)
    ab_re = mag * jnp.cos(lam_im * dt)
    ab_im = mag * jnp.sin(lam_im * dt)
    den = lam_re * lam_re + lam_im * lam_im
    nr = ab_re - 1.0
    ni = ab_im
    f_re = (nr * lam_re + ni * lam_im) / den
    f_im = (ni * lam_re - nr * lam_im) / den
    bb_re = f_re * b_re - f_im * b_im
    bb_im = f_re * b_im + f_im * b_re
    return ab_re, ab_im, bb_re, bb_im


def s5_prep(lam_re, lam_im, log_dt, b_re, b_im):
    gp, h = b_re.shape

    def body(lr, li, ld, br, bi, o_ar, o_ai, o_br, o_bi):
        res = _s5_discretise(lr[...], li[...], ld[...], br[...], bi[...])
        for r, v in zip((o_ar, o_ai, o_br, o_bi), res):
            r[...] = v

    col, mat = jax.ShapeDtypeStruct((gp, 1), F32), jax.ShapeDtypeStruct((gp, h), F32)
    return pl.pallas_call(body, name="s5_prep", out_shape=[col, col, mat, mat])(lam_re, lam_im, log_dt, b_re, b_im)


def s5_prep_bwd(lam_re, lam_im, log_dt, b_re, b_im, d_ab_re, d_ab_im, d_bb_re, d_bb_im):
    gp, h = b_re.shape

    def body(lr, li, ld, br, bi, g_ar, g_ai, g_br, g_bi, o_lr, o_li, o_ld, o_br, o_bi):
        _, vjp = jax.vjp(_s5_discretise, lr[...], li[...], ld[...], br[...], bi[...])
        res = vjp((g_ar[...], g_ai[...], g_br[...], g_bi[...]))
        for r, v in zip((o_lr, o_li, o_ld, o_br, o_bi), res):
            r[...] = v

    col, mat = jax.ShapeDtypeStruct((gp, 1), F32), jax.ShapeDtypeStruct((gp, h), F32)
    return pl.pallas_call(body, name="s5_prep_bwd", out_shape=[col, col, col, mat, mat])(
        lam_re, lam_im, log_dt, b_re, b_im, d_ab_re, d_ab_im, d_bb_re, d_bb_im)


def _s5_chunk(seq):
    return _divisor_tile(seq, 256, 16)


def s5_fwd(u, bbd_re, bbd_im, cbd_re, cbd_im, ab_re, ab_im, dskip):
    bsz, seq, d = u.shape
    nb, cb, ns = bbd_re.shape
    lc = _s5_chunk(seq)

    def body(u_ref, bre_ref, bim_ref, cre_ref, cim_ref, ar_ref, ai_ref, d_ref, y_ref, sre_ref, sim_ref,
             carry_re, carry_im):
        t = pl.program_id(1)

        @pl.when(t == 0)
        def _():
            carry_re[...] = jnp.zeros_like(carry_re)
            carry_im[...] = jnp.zeros_like(carry_im)

        for b in range(bsz):
            ub = u_ref[b].astype(BF16)
            sre_ref[b] = jnp.dot(ub, bre_ref[...], preferred_element_type=F32)
            sim_ref[b] = jnp.dot(ub, bim_ref[...], preferred_element_type=F32)
        ar, ai = ar_ref[...], ai_ref[...]

        def step(i, carry):
            row = pl.ds(i, 1)
            out = []
            for b, (cr, ci) in enumerate(carry):
                nr = ar * cr - ai * ci + sre_ref[b, row, :]
                ni = ar * ci + ai * cr + sim_ref[b, row, :]
                sre_ref[b, row, :] = nr
                sim_ref[b, row, :] = ni
                out.append((nr, ni))
            return tuple(out)

        init = tuple((carry_re[b], carry_im[b]) for b in range(bsz))
        last = lax.fori_loop(0, lc, step, init, unroll=8)
        for b, (cr, ci) in enumerate(last):
            carry_re[b] = cr
            carry_im[b] = ci
            y = jnp.dot(sre_ref[b].astype(BF16), cre_ref[...], preferred_element_type=F32)
            y -= jnp.dot(sim_ref[b].astype(BF16), cim_ref[...], preferred_element_type=F32)
            y_ref[b] = y + d_ref[...] * u_ref[b]

    chan = pl.BlockSpec((bsz, lc, cb), lambda n, t: (0, t, n))
    state = pl.BlockSpec((bsz, lc, ns), lambda n, t: (0, t, n))
    par = lambda r, c: pl.BlockSpec((None, r, c), lambda n, t: (n, 0, 0))
    return pl.pallas_call(
        body, name="s5_fwd", grid=(nb, seq // lc),
        in_specs=[chan, par(cb, ns), par(cb, ns), par(ns, cb), par(ns, cb), par(1, ns), par(1, ns),
                  pl.BlockSpec((None, 1, cb), lambda n, t: (0, 0, n))],
        out_specs=[chan, state, state],
        out_shape=[jax.ShapeDtypeStruct((bsz, seq, d), F32),
                   jax.ShapeDtypeStruct((bsz, seq, nb * ns), F32),
                   jax.ShapeDtypeStruct((bsz, seq, nb * ns), F32)],
        scratch_shapes=[pltpu.VMEM((bsz, 1, ns), F32), pltpu.VMEM((bsz, 1, ns), F32)],
        compiler_params=_cparams("arbitrary", "arbitrary"),
    )(u, bbd_re, bbd_im, cbd_re, cbd_im, ab_re, ab_im, dskip)


def s5_bwd(dy, u, st_re, st_im, bbd_re, bbd_im, cbd_re, cbd_im, ab_re, ab_im, dskip):
    bsz, seq, d = u.shape
    nb, cb, ns = bbd_re.shape
    lc = _s5_chunk(seq)
    nc = seq // lc

    def body(dy_ref, u_ref, sre_ref, sim_ref, bre_ref, bim_ref, cre_ref, cim_ref, ar_ref, ai_ref, d_ref,
             du_ref, dbre_out, dbim_out, dcre_out, dcim_out, dar_ref, dai_ref, dd_ref,
             g_re, g_im, gs_re, gs_im, carry_re, carry_im, dbre_ref, dbim_ref, dcre_ref, dcim_ref):
        t = pl.program_id(1)

        @pl.when(t == 0)
        def _():
            for r in (dbre_ref, dbim_ref, dcre_ref, dcim_ref, dar_ref, dai_ref, dd_ref, carry_re, carry_im):
                r[...] = jnp.zeros_like(r)

        nt = (((1,), (1,)), ((), ()))
        tn = (((0,), (0,)), ((), ()))
        for b in range(bsz):
            dyb = dy_ref[b].astype(BF16)
            g_re[b] = lax.dot_general(dyb, cre_ref[...], nt, preferred_element_type=F32)
            g_im[b] = -lax.dot_general(dyb, cim_ref[...], nt, preferred_element_type=F32)
        ar, ai = ar_ref[...], ai_ref[...]

        def step(k, carry):
            row = pl.ds(lc - 1 - k, 1)
            out = []
            for b, (cr, ci) in enumerate(carry):
                gs_re[b, row, :] = cr
                gs_im[b, row, :] = ci
                nr = ar * cr + ai * ci + g_re[b, row, :]
                ni = ar * ci - ai * cr + g_im[b, row, :]
                g_re[b, row, :] = nr
                g_im[b, row, :] = ni
                out.append((nr, ni))
            return tuple(out)

        init = tuple((carry_re[b], carry_im[b]) for b in range(bsz))
        last = lax.fori_loop(0, lc, step, init, unroll=8)
        for b, (cr, ci) in enumerate(last):
            carry_re[b] = cr
            carry_im[b] = ci
            dyf, uf = dy_ref[b], u_ref[b]
            dyb, ub = dyf.astype(BF16), uf.astype(BF16)
            sr, si = sre_ref[b], sim_ref[b]
            hr, hi = gs_re[b], gs_im[b]
            dar_ref[...] += _rowsum(hr * sr + hi * si)
            dai_ref[...] += _rowsum(hi * sr - hr * si)
            gr, gi = g_re[b].astype(BF16), g_im[b].astype(BF16)
            du = lax.dot_general(gr, bre_ref[...], nt, preferred_element_type=F32)
            du += lax.dot_general(gi, bim_ref[...], nt, preferred_element_type=F32)
            du_ref[b] = du + d_ref[...] * dyf
            dbre_ref[...] += lax.dot_general(ub, gr, tn, preferred_element_type=F32)
            dbim_ref[...] += lax.dot_general(ub, gi, tn, preferred_element_type=F32)
            dcre_ref[...] += lax.dot_general(sr.astype(BF16), dyb, tn, preferred_element_type=F32)
            dcim_ref[...] -= lax.dot_general(si.astype(BF16), dyb, tn, preferred_element_type=F32)
            dd_ref[...] += _rowsum(dyf * uf)

        @pl.when(t == nc - 1)
        def _():
            for k in range(cb // S5_GROUP):
                chans = slice(k * S5_GROUP, (k + 1) * S5_GROUP)
                states = slice(k * S5_STATE, (k + 1) * S5_STATE)
                dbre_out[chans, :] = dbre_ref[chans, states]
                dbim_out[chans, :] = dbim_ref[chans, states]
                dcre_out[states, :] = dcre_ref[states, chans]
                dcim_out[states, :] = dcim_ref[states, chans]

    chan = pl.BlockSpec((bsz, lc, cb), lambda n, t: (0, nc - 1 - t, n))
    state = pl.BlockSpec((bsz, lc, ns), lambda n, t: (0, nc - 1 - t, n))
    par = lambda r, c: pl.BlockSpec((None, r, c), lambda n, t: (n, 0, 0))
    return pl.pallas_call(
        body, name="s5_bwd", grid=(nb, nc),
        in_specs=[chan, chan, state, state, par(cb, ns), par(cb, ns), par(ns, cb), par(ns, cb),
                  par(1, ns), par(1, ns), pl.BlockSpec((None, 1, cb), lambda n, t: (0, 0, n))],
        out_specs=[chan, par(cb, S5_STATE), par(cb, S5_STATE), par(ns, S5_GROUP), par(ns, S5_GROUP),
                   par(1, ns), par(1, ns), par(1, cb)],
        out_shape=[jax.ShapeDtypeStruct((bsz, seq, d), F32),
                   jax.ShapeDtypeStruct((nb, cb, S5_STATE), F32), jax.ShapeDtypeStruct((nb, cb, S5_STATE), F32),
                   jax.ShapeDtypeStruct((nb, ns, S5_GROUP), F32), jax.ShapeDtypeStruct((nb, ns, S5_GROUP), F32),
                   jax.ShapeDtypeStruct((nb, 1, ns), F32), jax.ShapeDtypeStruct((nb, 1, ns), F32),
                   jax.ShapeDtypeStruct((nb, 1, cb), F32)],
        scratch_shapes=([pltpu.VMEM((bsz, lc, ns), F32)] * 4 + [pltpu.VMEM((bsz, 1, ns), F32)] * 2
                        + [pltpu.VMEM((cb, ns), F32)] * 2 + [pltpu.VMEM((ns, cb), F32)] * 2),
        compiler_params=_cparams("arbitrary", "arbitrary"),
    )(dy, u, st_re, st_im, bbd_re, bbd_im, cbd_re, cbd_im, ab_re, ab_im, dskip)


ATT_HEADS = 4
ATT_HEADS_FWD = 8
ATT_LANES = ATT_HEADS * HEAD_DIM
ATT_KEYS = 2 * ATT_BLOCK
ATT_Q = 256
ATT_SCALE = 1.0 / math.sqrt(HEAD_DIM)
_NT = (((1,), (1,)), ((), ()))
_TN = (((0,), (0,)), ((), ()))
_HEADS = [slice(h * HEAD_DIM, (h + 1) * HEAD_DIM) for h in range(ATT_HEADS)]
_HALF = [slice(0, ATT_BLOCK), slice(ATT_BLOCK, ATT_KEYS)]


def _log_sigmoids(z):
    sp = jnp.log(1.0 + jnp.exp(-jnp.abs(z)))
    ls = jnp.minimum(z, 0.0) - sp
    return ls, ls - z


def _sum_matrix(after, inclusive):
    j = lax.broadcasted_iota(jnp.int32, (ATT_KEYS, ATT_KEYS), 0) % ATT_BLOCK
    s = lax.broadcasted_iota(jnp.int32, (ATT_KEYS, ATT_KEYS), 1)
    if after:
        hit = (j >= s) if inclusive else (j > s)
    else:
        hit = (j <= s) if inclusive else (j < s)
    return jnp.where(jnp.logical_or(hit, s >= ATT_BLOCK), 1.0, 0.0).astype(BF16)


def _hi_lo(v):
    hi = v.astype(BF16)
    lo = (v - hi.astype(F32)).astype(BF16)
    return jnp.concatenate([hi, lo], axis=1)


def _strict_mask(i, j):
    t = i * ATT_Q + lax.broadcasted_iota(jnp.int32, (ATT_Q, ATT_KEYS), 0)
    s = j * ATT_KEYS + lax.broadcasted_iota(jnp.int32, (ATT_Q, ATT_KEYS), 1)
    return s < t


def attention_fwd(q, k, v):
    bsz, seq, d = q.shape
    n_heads = ATT_HEADS_FWD if d % (ATT_HEADS_FWD * HEAD_DIM) == 0 else ATT_HEADS
    lanes = n_heads * HEAD_DIM
    heads = [slice(h * HEAD_DIM, (h + 1) * HEAD_DIM) for h in range(n_heads)]

    def body(q_ref, k_ref, v_ref, o_ref, tot_ref, z_buf, ls_buf, cs_buf, acc_buf, run_buf):
        i = pl.program_id(2)
        jd = ((i + 1) * ATT_Q - 1) // ATT_KEYS
        sums = _sum_matrix(True, False)
        acc_buf[...] = jnp.zeros_like(acc_buf)
        run_buf[...] = jnp.zeros_like(run_buf)

        def block(j, masked):
            rows = pl.ds(pl.multiple_of(j * ATT_KEYS, ATT_KEYS), ATT_KEYS)
            strict = _strict_mask(i, j) if masked else None
            for h, ln in enumerate(heads):
                z_buf[h] = lax.dot_general(q_ref[:, ln], k_ref[rows, ln], _NT, preferred_element_type=F32)
            for h in range(n_heads):
                for half, cols in enumerate(_HALF):
                    ls, lf = _log_sigmoids(z_buf[h, :, cols])
                    if masked:
                        lf = jnp.where(strict[:, cols], lf, 0.0)
                    ls_buf[h, :, cols] = ls
                    cs_buf[h, half] = jnp.dot(_hi_lo(lf), sums, preferred_element_type=F32)
            for h, ln in enumerate(heads):
                run = run_buf[h]
                late, early = cs_buf[h, 1], cs_buf[h, 0]
                a1 = run + late[:, _HALF[0]]
                run = run + late[:, _HALF[1]]
                a0 = run + early[:, _HALF[0]]
                run_buf[h] = run + early[:, _HALF[1]]
                w = jnp.exp(ls_buf[h] + jnp.concatenate([a0, a1], axis=1))
                if masked:
                    w = jnp.where(strict, w, 0.0)
                acc_buf[h] += jnp.dot(w.astype(BF16), v_ref[rows, ln], preferred_element_type=F32)

        block(jd, True)

        def step(it, carry):
            block(jd - 1 - it, False)
            return carry

        lax.fori_loop(0, jd, step, 0)
        o_ref[...] = jnp.concatenate([acc_buf[h] for h in range(n_heads)], axis=1).astype(o_ref.dtype)
        tot_ref[...] = jnp.concatenate([run_buf[h, :, :HEAD_DIM] for h in range(n_heads)], axis=1)

    blk = pl.BlockSpec((None, ATT_Q, lanes), lambda b, p, i: (b, i, p))
    full = pl.BlockSpec((None, seq, lanes), lambda b, p, i: (b, 0, p))
    tile = (n_heads, ATT_Q, ATT_KEYS)
    return pl.pallas_call(
        body, name="attention_fwd", grid=(bsz, d // lanes, seq // ATT_Q),
        in_specs=[blk, full, full], out_specs=[blk, blk],
        out_shape=[jax.ShapeDtypeStruct((bsz, seq, d), BF16), jax.ShapeDtypeStruct((bsz, seq, d), F32)],
        scratch_shapes=[pltpu.VMEM(tile, F32), pltpu.VMEM(tile, F32),
                        pltpu.VMEM((n_heads, 2, ATT_Q, ATT_KEYS), F32),
                        pltpu.VMEM((n_heads, ATT_Q, HEAD_DIM), F32),
                        pltpu.VMEM((n_heads, ATT_Q, ATT_BLOCK), F32)],
        compiler_params=_cparams("arbitrary", "arbitrary", "arbitrary"),
    )(q, k, v)


def attention_bwd(q, k, v, tot, do):
    bsz, seq, d = q.shape

    def body(q_ref, k_ref, v_ref, tot_ref, do_ref, dq_ref, dk_ref, dv_ref,
             z_buf, dw_buf, ls_buf, e_buf, up_buf, bf_buf, w_buf, do_buf, dq_buf, tot_buf, run_buf, erun_buf):
        i = pl.program_id(2)
        jd = ((i + 1) * ATT_Q - 1) // ATT_KEYS

        @pl.when(i == 0)
        def _():
            dk_ref[...] = jnp.zeros_like(dk_ref)
            dv_ref[...] = jnp.zeros_like(dv_ref)

        upto_incl, upto_excl = _sum_matrix(False, True), _sum_matrix(False, False)
        do_buf[...] = do_ref[...].astype(BF16)
        for h, ln in enumerate(_HEADS):
            tot_buf[h] = jnp.concatenate([tot_ref[:, ln], tot_ref[:, ln]], axis=1)
        dq_buf[...] = jnp.zeros_like(dq_buf)
        run_buf[...] = jnp.zeros_like(run_buf)
        erun_buf[...] = jnp.zeros_like(erun_buf)

        def block(j, masked):
            rows = pl.ds(pl.multiple_of(j * ATT_KEYS, ATT_KEYS), ATT_KEYS)
            strict = _strict_mask(i, j) if masked else None
            for h, ln in enumerate(_HEADS):
                z_buf[h] = lax.dot_general(q_ref[:, ln], k_ref[rows, ln], _NT, preferred_element_type=F32)
                dw_buf[h] = lax.dot_general(do_buf[:, ln], v_ref[rows, ln], _NT, preferred_element_type=F32)
            for h in range(ATT_HEADS):
                for half, cols in enumerate(_HALF):
                    ls, lf = _log_sigmoids(z_buf[h, :, cols])
                    if masked:
                        lf = jnp.where(strict[:, cols], lf, 0.0)
                    ls_buf[h, :, cols] = ls
                    up_buf[h, half] = jnp.dot(_hi_lo(lf), upto_incl, preferred_element_type=F32)
            for h in range(ATT_HEADS):
                run = run_buf[h]
                early, late = up_buf[h, 0], up_buf[h, 1]
                u0 = run + early[:, _HALF[0]]
                run = run + early[:, _HALF[1]]
                u1 = run + late[:, _HALF[0]]
                run_buf[h] = run + late[:, _HALF[1]]
                tot_h = tot_buf[h]
                after = jnp.concatenate([tot_h - u0, tot_h - u1], axis=1)
                w = jnp.exp(ls_buf[h] + after)
                if masked:
                    w = jnp.where(strict, w, 0.0)
                w_buf[h] = w.astype(BF16)
                e = dw_buf[h] * w
                e_buf[h] = e
                for half, cols in enumerate(_HALF):
                    bf_buf[h, half] = jnp.dot(_hi_lo(e[:, cols]), upto_excl, preferred_element_type=F32)
            dks, dvs = [], []
            for h, ln in enumerate(_HEADS):
                erun = erun_buf[h]
                early, late = bf_buf[h, 0], bf_buf[h, 1]
                b0 = erun + early[:, _HALF[0]]
                erun = erun + early[:, _HALF[1]]
                b1 = erun + late[:, _HALF[0]]
                erun_buf[h] = erun + late[:, _HALF[1]]
                e = e_buf[h]
                dz = e - jnp.exp(ls_buf[h]) * (e + jnp.concatenate([b0, b1], axis=1))
                if masked:
                    dz = jnp.where(strict, dz, 0.0)
                dz = dz.astype(BF16)
                dq_buf[h] += jnp.dot(dz, k_ref[rows, ln], preferred_element_type=F32)
                dks.append(lax.dot_general(dz, q_ref[:, ln], _TN, preferred_element_type=F32))
                dvs.append(lax.dot_general(w_buf[h], do_buf[:, ln], _TN, preferred_element_type=F32))
            dk_ref[rows, :] += jnp.concatenate(dks, axis=1)
            dv_ref[rows, :] += jnp.concatenate(dvs, axis=1)

        def step(j, carry):
            block(j, False)
            return carry

        lax.fori_loop(0, jd, step, 0)
        block(jd, True)
        dq_ref[...] = jnp.concatenate([dq_buf[h] for h in range(ATT_HEADS)], axis=1) * ATT_SCALE

    blk = pl.BlockSpec((None, ATT_Q, ATT_LANES), lambda b, p, i: (b, i, p))
    full = pl.BlockSpec((None, seq, ATT_LANES), lambda b, p, i: (b, 0, p))
    shape = jax.ShapeDtypeStruct((bsz, seq, d), F32)
    tile = (ATT_HEADS, ATT_Q, ATT_KEYS)
    pair = (ATT_HEADS, 2, ATT_Q, ATT_KEYS)
    square = (ATT_HEADS, ATT_Q, ATT_BLOCK)
    return pl.pallas_call(
        body, name="attention_bwd", grid=(bsz, d // ATT_LANES, seq // ATT_Q),
        in_specs=[blk, full, full, blk, blk], out_specs=[blk, full, full], out_shape=[shape, shape, shape],
        scratch_shapes=[pltpu.VMEM(tile, F32), pltpu.VMEM(tile, F32), pltpu.VMEM(tile, F32), pltpu.VMEM(tile, F32),
                        pltpu.VMEM(pair, F32), pltpu.VMEM(pair, F32), pltpu.VMEM(tile, BF16),
                        pltpu.VMEM((ATT_Q, ATT_LANES), BF16), pltpu.VMEM((ATT_HEADS, ATT_Q, HEAD_DIM), F32),
                        pltpu.VMEM(square, F32), pltpu.VMEM(square, F32), pltpu.VMEM(square, F32)],
        compiler_params=_cparams("arbitrary", "arbitrary", "arbitrary"),
    )(q, k, v, tot, do)


def _tile_fused(seq):
    return _divisor_tile(seq, 1024, 16)


def mlp_core_fwd(h, w1, w2, x, gate, name):
    bsz, seq, d = x.shape
    ns, _, fs = w1.shape
    t = bsz * seq
    tm = _tile_fused(seq)

    def body(h_ref, w1_ref, w2_ref, x_ref, g_ref, act_ref, ff_ref, out_ref, acc_ref):
        s = pl.program_id(1)
        pre = jnp.dot(h_ref[...], w1_ref[...], preferred_element_type=F32)
        act = jnp.square(jnp.maximum(pre, 0.0)).astype(BF16)
        act_ref[...] = act
        part = jnp.dot(act, w2_ref[...], preferred_element_type=F32)

        @pl.when(s == 0)
        def _():
            acc_ref[...] = part

        @pl.when(s > 0)
        def _():
            acc_ref[...] += part

        @pl.when(s == ns - 1)
        def _():
            ff = acc_ref[...]
            ff_ref[...] = ff
            out_ref[...] = x_ref[...] + g_ref[...] * ff

    rows = pl.BlockSpec((tm, d), lambda i, s: (i, 0))
    act, ff, out = pl.pallas_call(
        body, name=name, grid=(t // tm, ns),
        in_specs=[rows, pl.BlockSpec((None, d, fs), lambda i, s: (s, 0, 0)),
                  pl.BlockSpec((None, fs, d), lambda i, s: (s, 0, 0)), rows,
                  pl.BlockSpec((None, 1, d), lambda i, s: ((i * tm) // seq, 0, 0))],
        out_specs=[pl.BlockSpec((tm, fs), lambda i, s: (i, s)), rows, rows],
        out_shape=[jax.ShapeDtypeStruct((t, ns * fs), BF16), jax.ShapeDtypeStruct((t, d), F32),
                   jax.ShapeDtypeStruct((t, d), F32)],
        scratch_shapes=[pltpu.VMEM((tm, d), F32)],
        compiler_params=_cparams("arbitrary", "arbitrary"),
    )(h.reshape(t, d), w1, w2, x.reshape(t, d), gate)
    return act, ff.reshape(bsz, seq, d), out.reshape(bsz, seq, d)


def mlp_core_bwd(dff, act, w1, w2, name):
    t, d = dff.shape
    ns, _, fs = w1.shape
    tm = _tile_fused(t)

    def body(dff_ref, act_ref, w1_ref, w2_ref, dpre_ref, dh_ref, acc_ref):
        s = pl.program_id(1)
        dact = lax.dot_general(dff_ref[...], w2_ref[...], _NT, preferred_element_type=F32)
        dpre = (dact * (2.0 * jnp.sqrt(act_ref[...].astype(F32)))).astype(BF16)
        dpre_ref[...] = dpre
        part = lax.dot_general(dpre, w1_ref[...], _NT, preferred_element_type=F32)

        @pl.when(s == 0)
        def _():
            acc_ref[...] = part

        @pl.when(s > 0)
        def _():
            acc_ref[...] += part

        @pl.when(s == ns - 1)
        def _():
            dh_ref[...] = acc_ref[...]

    rows = pl.BlockSpec((tm, d), lambda i, s: (i, 0))
    slab = pl.BlockSpec((tm, fs), lambda i, s: (i, s))
    return pl.pallas_call(
        body, name=name, grid=(t // tm, ns),
        in_specs=[rows, slab, pl.BlockSpec((None, d, fs), lambda i, s: (s, 0, 0)),
                  pl.BlockSpec((None, fs, d), lambda i, s: (s, 0, 0))],
        out_specs=[slab, rows],
        out_shape=[jax.ShapeDtypeStruct((t, ns * fs), BF16), jax.ShapeDtypeStruct((t, d), F32)],
        scratch_shapes=[pltpu.VMEM((tm, d), F32)],
        compiler_params=_cparams("arbitrary", "arbitrary"),
    )(dff, act, w1, w2)


def mlp_fwd(x, g, sh, sc, gate, w1_handle, w2_handle, tag):
    bsz, seq, d = x.shape
    h = act_call(_norm_mod, [x, g, sh, sc], [(d, BF16, "tile")], tag + "_norm")[0]
    w1 = exchange_wait(w1_handle, h, tag + "_w1_wait")
    w2 = exchange_wait(w2_handle, h, tag + "_w2_wait")
    act, ff, out = mlp_core_fwd(h, w1, w2, x, gate, tag + "_core")
    return out, (h, act, ff), w1, w2


def mlp_bwd(dout, x, g, sc, gate, w1, w2, saved, tag):
    bsz, seq, d = x.shape
    t = bsz * seq
    ns = w1.shape[0]
    h, act, ff = saved
    dff, dgate = act_call(lambda do_, f_, g_: (g_ * do_, _rowsum(do_ * f_)), [dout, ff, gate],
                          [(d, BF16, "tile"), (d, F32, "seq")], tag + "_dres")
    dff = dff.reshape(t, d)
    dpre, dh = mlp_core_bwd(dff, act, w1, w2, tag + "_dcore")
    dw2 = mm_tn(act, dff, "a", ns, tag + "_dw2")
    dw1 = mm_tn(h.reshape(t, d), dpre, "c", ns, tag + "_dw1")
    (dw1_handle, dw2_handle), token = exchange_start([(dw1, True), (dw2, True)], tag + "_dw_start")
    dx, dsh, dsc, dg = act_call(_norm_mod_bwd, [x, g + token[0, 0], sc, dh.reshape(bsz, seq, d), dout],
                                [(d, F32, "tile"), (d, F32, "seq"), (d, F32, "seq"), (d, F32, "all")],
                                tag + "_dnorm")
    return dx, dw1_handle, dw2_handle, (dsh, dsc, dgate, dg)


def _block_diag(m, rows_first):
    nb, k, r, c = m.shape
    eye = jnp.eye(k, dtype=m.dtype)
    return jnp.einsum("nkrc,kl->nkrlc", m, eye).reshape(nb, k * r, k * c)


def kernel(x, c, ada_w, ada_b, mix_norm_g, mlp_norm_g, mlp_w1, mlp_w2, s5_a_re, s5_a_im, s5_log_dt, s5_b_re, s5_b_im, s5_c_re, s5_c_im, s5_d, s5_w_glu, kv_ada_w, kv_ada_b, kv_norm_g, w_kv, k_norm_g, sb_w_q, q_norm_g, sb_w_o, loss_target, m_ada_w, m_ada_b, m_mix_norm_g, m_mlp_norm_g, m_mlp_w1, m_mlp_w2, m_s5_a_re, m_s5_a_im, m_s5_log_dt, m_s5_b_re, m_s5_b_im, m_s5_c_re, m_s5_c_im, m_s5_d, m_s5_w_glu, m_kv_ada_w, m_kv_ada_b, m_kv_norm_g, m_w_kv, m_k_norm_g, m_sb_w_q, m_q_norm_g, m_sb_w_o, v_ada_w, v_ada_b, v_mix_norm_g, v_mlp_norm_g, v_mlp_w1, v_mlp_w2, v_s5_a_re, v_s5_a_im, v_s5_log_dt, v_s5_b_re, v_s5_b_im, v_s5_c_re, v_s5_c_im, v_s5_d, v_s5_w_glu, v_kv_ada_w, v_kv_ada_b, v_kv_norm_g, v_w_kv, v_k_norm_g, v_sb_w_q, v_q_norm_g, v_sb_w_o):
    bsz, seq, d = x.shape
    t = bsz * seq
    n_groups = d // S5_GROUP
    nb = n_groups // S5_BLOCK_GROUPS
    gp = n_groups * S5_STATE
    dev = 4 * lax.axis_index("x") + 2 * lax.axis_index("y") + lax.axis_index("c")
    e_ada, e_kv = 6 * d, 2 * d
    n_ada, n_kv = e_ada // N_DEV, e_kv // N_DEV

    d_skip = all_gather(s5_d, "gather_skip").reshape(1, 1, d)
    c_all = all_gather(c, "gather_c").reshape(N_DEV * bsz, d)

    w_cols = jnp.concatenate([ada_w[0], ada_w[1], kv_ada_w], axis=1)
    b_cols = jnp.concatenate([
        lax.dynamic_slice_in_dim(ada_b[0], dev * n_ada, n_ada),
        lax.dynamic_slice_in_dim(ada_b[1], dev * n_ada, n_ada),
        lax.dynamic_slice_in_dim(kv_ada_b, dev * n_kv, n_kv)])[None, :]
    mod_cols = ada_fwd(c_all, w_cols, b_cols)
    mod_all = all_gather(mod_cols, "gather_mod")
    mod_mine = lax.dynamic_slice_in_dim(mod_all, dev * bsz, bsz, axis=1)
    mod_mine = jnp.transpose(mod_mine, (1, 0, 2))
    mods = []
    for i in range(2):
        full = mod_mine[:, :, i * n_ada:(i + 1) * n_ada].reshape(bsz, e_ada)
        mods.append([full[:, None, j * d:(j + 1) * d] for j in range(6)])
    kv_full = mod_mine[:, :, 2 * n_ada:].reshape(bsz, e_kv)
    kv_sh, kv_sc = kv_full[:, None, :d], kv_full[:, None, d:]

    par = lambda p: p.reshape(1, 1, -1)

    shards = [s5_w_glu[0], mlp_w1[0], mlp_w2[0], w_kv, sb_w_q[0], sb_w_o[0], mlp_w1[1], mlp_w2[1]]
    gathers, gather_token = exchange_start([(w.astype(BF16), False) for w in shards], "gather_start", after=[mod_all, d_skip])
    glu_handle, w1_0_handle, w2_0_handle, wkv_handle, wq_handle, wo_handle, w1_1_handle, w2_1_handle = gathers
    started = gather_token[0, 0]

    sh_a, sc_a, g_a, sh_m, sc_m, g_m = mods[0]
    lam_re, lam_im = s5_a_re.reshape(gp, 1), s5_a_im.reshape(gp, 1)
    log_dt = jnp.broadcast_to(s5_log_dt.reshape(n_groups, 1), (n_groups, S5_STATE)).reshape(gp, 1)
    b_re, b_im = s5_b_re.reshape(gp, S5_GROUP), s5_b_im.reshape(gp, S5_GROUP)
    ab_re, ab_im, bb_re, bb_im = s5_prep(lam_re, lam_im, log_dt, b_re, b_im)
    to_bbd = lambda m: _block_diag(jnp.swapaxes(m.reshape(nb, S5_BLOCK_GROUPS, S5_STATE, S5_GROUP), 2, 3), True)
    to_cbd = lambda m: _block_diag(jnp.swapaxes(m.reshape(nb, S5_BLOCK_GROUPS, S5_GROUP, S5_STATE), 2, 3), True)
    bbd_re, bbd_im = to_bbd(bb_re).astype(BF16), to_bbd(bb_im).astype(BF16)
    cbd_re, cbd_im = to_cbd(s5_c_re[0]).astype(BF16), to_cbd(s5_c_im[0]).astype(BF16)
    abr, abi = ab_re.reshape(nb, 1, -1), ab_im.reshape(nb, 1, -1)

    h0 = act_call(_norm_mod, [x, par(mix_norm_g[0]) + started, sh_a, sc_a], [(d, F32, "tile")], "mix0_norm")[0]
    y, st_re, st_im = s5_fwd(h0, bbd_re, bbd_im, cbd_re, cbd_im, abr, abi, d_skip)
    ge = act_call(lambda y_: jax.nn.gelu(y_), [y], [(d, BF16, "tile")], "gelu")[0]
    w_glu = exchange_wait(glu_handle, ge, "glu_w_wait")
    z = mm_nn_col(ge.reshape(t, d), w_glu, "glu_up").reshape(bsz, seq, 2 * d)
    x1 = act_call(lambda x_, z_, g_: x_ + g_ * (z_[:, :d] * jax.nn.sigmoid(z_[:, d:])), [x, z, g_a],
                  [(d, F32, "tile")], "glu_res")[0]
    x2, mlp0_saved, w1_0, w2_0 = mlp_fwd(x1, par(mlp_norm_g[0]), sh_m, sc_m, g_m, w1_0_handle, w2_0_handle, "mlp0")

    sh_a1, sc_a1, g_a1, sh_m1, sc_m1, g_m1 = mods[1]
    kg = par(jnp.tile(k_norm_g, d // HEAD_DIM))
    qg = par(jnp.tile(q_norm_g[0], d // HEAD_DIM))
    hkv = act_call(_norm_mod, [x2, par(kv_norm_g), kv_sh, kv_sc], [(d, BF16, "tile")], "kv_norm")[0]
    wkv = exchange_wait(wkv_handle, hkv, "kv_w_wait")
    kvf = mm_nn_col(hkv.reshape(t, d), wkv, "kv_proj").reshape(bsz, seq, 2 * d)
    k_h, v_h = act_call(lambda kv_, g_: (_head_norm(kv_[:, :d], g_), kv_[:, d:]), [kvf, kg],
                        [(d, BF16, "tile"), (d, BF16, "tile")], "k_norm")
    h1 = act_call(_norm_mod, [x2, par(mix_norm_g[1]), sh_a1, sc_a1], [(d, BF16, "tile")], "mix1_norm")[0]
    wq = exchange_wait(wq_handle, h1, "q_w_wait")
    whole = lambda w: w.reshape(1, d, d)
    q_raw = mm_nn_row(h1.reshape(t, d), whole(wq), "q_proj").reshape(bsz, seq, d)
    q_h = act_call(lambda x_, g_: _head_norm(x_, g_) * ATT_SCALE, [q_raw, qg], [(d, BF16, "tile")], "q_norm")[0]
    o, att_tot = attention_fwd(q_h, k_h, v_h)
    wo = exchange_wait(wo_handle, o, "o_w_wait")
    mix1 = mm_nn_row(o.reshape(t, d), whole(wo), "o_proj").reshape(bsz, seq, d)
    x3 = act_call(lambda x_, f_, g_: x_ + g_ * f_, [x2, mix1, g_a1], [(d, F32, "tile")], "att_res")[0]
    x4, mlp1_saved, w1_1, w2_1 = mlp_fwd(x3, par(mlp_norm_g[1]), sh_m1, sc_m1, g_m1, w1_1_handle, w2_1_handle, "mlp1")

    def loss_fn(y_, t_):
        diff = y_ - t_
        part = jnp.sum(0.5 * jnp.mean(diff * diff, axis=-1, keepdims=True), axis=0, keepdims=True)
        return jnp.broadcast_to(part, (1, LANES)), diff * (1.0 / d)

    loss_part, dx4 = act_call(loss_fn, [x4, loss_target], [(LANES, F32, "all"), (d, F32, "tile")], "loss")
    loss = lax.psum(loss_part[0, 0, 0], ("x", "y", "c"))

    dx3, dw1_1, dw2_1, (dsh_m1, dsc_m1, dg_m1, dgn_mlp1) = mlp_bwd(
        dx4, x3, par(mlp_norm_g[1]), sc_m1, g_m1, w1_1, w2_1, mlp1_saved, "mlp1")
    dmix1, dg_a1 = act_call(lambda do_, f_, g_: (g_ * do_, _rowsum(do_ * f_)), [dx3, mix1, g_a1],
                            [(d, BF16, "tile"), (d, F32, "seq")], "att_dres")
    dmix1 = dmix1.reshape(t, d)
    do = mm_nt_row(dmix1, whole(wo), "o_dproj").reshape(bsz, seq, d)
    dwo = mm_tn(o.reshape(t, d), dmix1, "a", N_DEV, "o_dw")
    dq, dk, dv = attention_bwd(q_h, k_h, v_h, att_tot, do)
    dq_raw, dqg = act_call(_head_norm_bwd, [q_raw, qg, dq], [(d, BF16, "tile"), (d, F32, "all")], "q_dnorm")
    dq_raw = dq_raw.reshape(t, d)
    dh1 = mm_nt_row(dq_raw, whole(wq), "q_dproj").reshape(bsz, seq, d)
    dwq = mm_tn(h1.reshape(t, d), dq_raw, "a", N_DEV, "q_dw")
    dx2, dsh_a1, dsc_a1, dgn_mix1 = act_call(
        _norm_mod_bwd, [x2, par(mix_norm_g[1]), sc_a1, dh1, dx3],
        [(d, F32, "tile"), (d, F32, "seq"), (d, F32, "seq"), (d, F32, "all")], "mix1_dnorm")

    def kv_bwd_fn(kv_, g_, dk_, dv_):
        dk_raw, dg_ = _head_norm_bwd(kv_[:, :d], g_, dk_)
        return jnp.concatenate([dk_raw, dv_], axis=1), dg_

    dkvf, dkg = act_call(kv_bwd_fn, [kvf, kg, dk, dv], [(2 * d, BF16, "tile"), (d, F32, "all")], "k_dnorm")
    dkvf = dkvf.reshape(t, 2 * d)
    dhkv = mm_nt_col(dkvf, wkv, "kv_dproj").reshape(bsz, seq, d)
    dwkv = mm_tn(hkv.reshape(t, d), dkvf, "c", N_DEV, "kv_dw")
    (dwo, dwq, dwkv), att_token = exchange_start([(dwo, True), (dwq, True), (dwkv, True)], "att_dw_start")
    dx2, dkv_sh, dkv_sc, dgn_kv = act_call(
        _norm_mod_bwd, [x2, par(kv_norm_g) + att_token[0, 0], kv_sc, dhkv, dx2],
        [(d, F32, "tile"), (d, F32, "seq"), (d, F32, "seq"), (d, F32, "all")], "kv_dnorm")

    dx1, dw1_0, dw2_0, (dsh_m0, dsc_m0, dg_m0, dgn_mlp0) = mlp_bwd(
        dx2, x1, par(mlp_norm_g[0]), sc_m, g_m, w1_0, w2_0, mlp0_saved, "mlp0")

    def glu_bwd_fn(do_, z_, g_):
        val, sig = z_[:, :d], jax.nn.sigmoid(z_[:, d:])
        dmix = g_ * do_
        dz = jnp.concatenate([dmix * sig, dmix * val * sig * (1.0 - sig)], axis=1)
        return dz, _rowsum(do_ * (val * sig))

    dz, dg_a0 = act_call(glu_bwd_fn, [dx1, z, g_a], [(2 * d, BF16, "tile"), (d, F32, "seq")], "glu_dres")
    dz = dz.reshape(t, 2 * d)
    dy = mm_nt_col(dz, w_glu, "glu_dup", (F32,), lambda acc, y_: (acc * _gelu_grad(y_),),
                   (y.reshape(t, d),)).reshape(bsz, seq, d)
    dwglu = mm_tn(ge.reshape(t, d), dz, "c", N_DEV, "glu_dw")
    (dwglu,), glu_token = exchange_start([(dwglu, True)], "glu_dw_start")
    du, dbbd_re, dbbd_im, dcbd_re, dcbd_im, dab_re, dab_im, dd_skip = s5_bwd(
        dy, h0, st_re, st_im, bbd_re, bbd_im, cbd_re, cbd_im, abr, abi, d_skip + glu_token[0, 0])
    dx0, dsh_a0, dsc_a0, dgn_mix0 = act_call(
        _norm_mod_bwd, [x, par(mix_norm_g[0]), sc_a, du, dx1],
        [(d, F32, "tile"), (d, F32, "seq"), (d, F32, "seq"), (d, F32, "all")], "mix0_dnorm")

    from_bbd = lambda m: jnp.swapaxes(m.reshape(nb, S5_BLOCK_GROUPS, S5_GROUP, S5_STATE), 2, 3).reshape(gp, S5_GROUP)
    d_c = lambda m: jnp.swapaxes(m.reshape(nb, S5_BLOCK_GROUPS, S5_STATE, S5_GROUP), 2, 3).reshape(
        1, n_groups, S5_GROUP, S5_STATE)
    d_lam_re, d_lam_im, d_log_dt, d_b_re, d_b_im = s5_prep_bwd(
        lam_re, lam_im, log_dt, b_re, b_im, dab_re.reshape(gp, 1), dab_im.reshape(gp, 1),
        from_bbd(dbbd_re), from_bbd(dbbd_im))

    small = all_reduce_small([
        jnp.stack([dgn_mix0.reshape(d), dgn_mix1.reshape(d)]),
        jnp.stack([dgn_mlp0.reshape(d), dgn_mlp1.reshape(d)]),
        d_lam_re.reshape(1, n_groups, S5_STATE), d_lam_im.reshape(1, n_groups, S5_STATE),
        d_log_dt.reshape(1, n_groups, S5_STATE).sum(axis=-1),
        d_b_re.reshape(s5_b_re.shape), d_b_im.reshape(s5_b_im.shape),
        d_c(dcbd_re), d_c(dcbd_im),
        dd_skip.reshape(1, d),
        dgn_kv.reshape(d),
        dkg.reshape(d // HEAD_DIM, HEAD_DIM).sum(axis=0),
        dqg.reshape(d // HEAD_DIM, HEAD_DIM).sum(axis=0)[None, :],
    ], "small_grads")
    (g_mix_norm, g_mlp_norm, g_a_re, g_a_im, g_log_dt, g_b_re, g_b_im, g_c_re, g_c_im,
     g_skip_full, g_kv_norm, g_k_norm, g_q_norm) = small
    g_s5_d = lax.dynamic_slice_in_dim(g_skip_full, dev * (d // N_DEV), d // N_DEV, axis=1)

    dm_mine = jnp.concatenate([
        dsh_a0, dsc_a0, dg_a0, dsh_m0, dsc_m0, dg_m0,
        dsh_a1, dsc_a1, dg_a1, dsh_m1, dsc_m1, dg_m1, dkv_sh, dkv_sc], axis=2).reshape(bsz, 2 * e_ada + e_kv)
    dm_all = all_gather(dm_mine, "gather_dmod").reshape(N_DEV * bsz, 2 * e_ada + e_kv)
    dm_cols = jnp.concatenate([
        lax.dynamic_slice_in_dim(dm_all, dev * n_ada, n_ada, axis=1),
        lax.dynamic_slice_in_dim(dm_all, e_ada + dev * n_ada, n_ada, axis=1),
        lax.dynamic_slice_in_dim(dm_all, 2 * e_ada + dev * n_kv, n_kv, axis=1)], axis=1)
    dw_cols, db_all = ada_bwd(c_all, dm_cols, dm_all)
    g_ada_w = jnp.stack([dw_cols[:, :n_ada], dw_cols[:, n_ada:2 * n_ada]])
    g_kv_ada_w = dw_cols[:, 2 * n_ada:]
    g_ada_b = db_all[0, :2 * e_ada].reshape(2, e_ada)
    g_kv_ada_b = db_all[0, 2 * e_ada:]

    landed = lambda handle, name: slab_sum(exchange_wait(handle, dx0, name + "_wait"), name + "_sum")
    g_w1 = jnp.stack([landed(dw1_0, "rs_w1_0"), landed(dw1_1, "rs_w1_1")])
    g_w2 = jnp.stack([landed(dw2_0, "rs_w2_0"), landed(dw2_1, "rs_w2_1")])
    g_glu = landed(dwglu, "rs_glu")[None]
    g_wkv = landed(dwkv, "rs_wkv")
    g_wq = landed(dwq, "rs_wq")[None]
    g_wo = landed(dwo, "rs_wo")[None]

    weights = [ada_w, ada_b, mix_norm_g, mlp_norm_g, mlp_w1, mlp_w2, s5_a_re, s5_a_im, s5_log_dt, s5_b_re,
               s5_b_im, s5_c_re, s5_c_im, s5_d, s5_w_glu, kv_ada_w, kv_ada_b, kv_norm_g, w_kv, k_norm_g,
               sb_w_q, q_norm_g, sb_w_o]
    grads = [g_ada_w, g_ada_b, g_mix_norm, g_mlp_norm, g_w1, g_w2, g_a_re, g_a_im, g_log_dt, g_b_re,
             g_b_im, g_c_re, g_c_im, g_s5_d, g_glu, g_kv_ada_w, g_kv_ada_b, g_kv_norm, g_wkv, g_k_norm,
             g_wq, g_q_norm, g_wo]
    ms = [m_ada_w, m_ada_b, m_mix_norm_g, m_mlp_norm_g, m_mlp_w1, m_mlp_w2, m_s5_a_re, m_s5_a_im, m_s5_log_dt,
          m_s5_b_re, m_s5_b_im, m_s5_c_re, m_s5_c_im, m_s5_d, m_s5_w_glu, m_kv_ada_w, m_kv_ada_b, m_kv_norm_g,
          m_w_kv, m_k_norm_g, m_sb_w_q, m_q_norm_g, m_sb_w_o]
    vs = [v_ada_w, v_ada_b, v_mix_norm_g, v_mlp_norm_g, v_mlp_w1, v_mlp_w2, v_s5_a_re, v_s5_a_im, v_s5_log_dt,
          v_s5_b_re, v_s5_b_im, v_s5_c_re, v_s5_c_im, v_s5_d, v_s5_w_glu, v_kv_ada_w, v_kv_ada_b, v_kv_norm_g,
          v_w_kv, v_k_norm_g, v_sb_w_q, v_q_norm_g, v_sb_w_o]
    grads = [g.reshape(w.shape) for g, w in zip(grads, weights)]
    deltas, new_ms, new_vs = [], [], []
    for i, (w, g, m, v) in enumerate(zip(weights, grads, ms, vs)):
        dl, nm, nv = adamw(w, g, m, v, f"adamw_{i}")
        deltas.append(dl)
        new_ms.append(nm)
        new_vs.append(nv)
    return (loss, dx0, *grads, *deltas, *new_ms, *new_vs)
```

```python
import functools
import math

import jax
import jax.numpy as jnp
from jax import lax
from jax.experimental import pallas as pl
from jax.experimental.pallas import tpu as pltpu

F32 = jnp.float32
BF16 = jnp.bfloat16

N_DEV = 8
N_CHIPS = 4
MESH = pl.DeviceIdType.MESH
ANY = pl.BlockSpec(memory_space=pl.ANY)

LANES = 128
VMEM_LIMIT_BYTES = 48 * 2 ** 20
TILE_BUDGET_BYTES = 4 * 2 ** 20

S5_GROUP = 16
S5_STATE = 64
S5_BLOCK_GROUPS = 16
HEAD_DIM = 64
ATT_BLOCK = 128
EPS = 1e-6

ADAM_LR = 0.001
ADAM_B1 = 0.9
ADAM_B2 = 0.999
ADAM_EPS = 1e-08
ADAM_WD = 0.01
ADAM_STEP = 10


def _cparams(*sem):
    return pltpu.CompilerParams(dimension_semantics=sem, vmem_limit_bytes=VMEM_LIMIT_BYTES)


def _divisor_tile(n, limit, mult):
    best = None
    for t in range(mult, min(n, limit) + 1, mult):
        if n % t == 0:
            best = t
    return best if best is not None else n


def _tile_m(m):
    return _divisor_tile(m, 2048 if m >= 4096 else 256, 16)


def all_gather(x, name):
    def body(x_ref, out_ref, send_sems, recv_sems, local_sem):
        ax, ay, ac = lax.axis_index("x"), lax.axis_index("y"), lax.axis_index("c")
        me, sibling = (ax, ay, ac), (ax, ay, 1 - ac)
        chips = [(1 - ax, ay), (ax, 1 - ay), (1 - ax, 1 - ay)]

        def slot(px, py, pc):
            return out_ref.at[4 * px + 2 * py + pc]

        def copy(k, block, to, src=None):
            return pltpu.make_async_remote_copy(
                src_ref=slot(*block) if src is None else src, dst_ref=slot(*block),
                send_sem=send_sems.at[k], recv_sem=recv_sems.at[k], device_id=to, device_id_type=MESH)

        mine = pltpu.make_async_copy(x_ref, slot(*me), local_sem)
        mine.start()
        first = [copy(0, me, sibling, src=x_ref)]
        first += [copy(1 + j, me, (*chip, ac), src=x_ref) for j, chip in enumerate(chips)]
        for cp in first:
            cp.start()
        passed = [copy(4 + j, (*chip, ac), sibling) for j, chip in enumerate(chips)]
        for j, chip in enumerate(chips):
            copy(1 + j, (*chip, ac), me).wait_recv()
            passed[j].start()
        copy(0, sibling, me).wait_recv()
        for j, chip in enumerate(chips):
            copy(4 + j, (*chip, 1 - ac), me).wait_recv()
        for cp in first + passed:
            cp.wait_send()
        mine.wait()

    return pl.pallas_call(
        body, name=name,
        out_shape=jax.ShapeDtypeStruct((N_DEV,) + x.shape, x.dtype),
        in_specs=[ANY], out_specs=ANY,
        scratch_shapes=[pltpu.SemaphoreType.DMA((7,)), pltpu.SemaphoreType.DMA((7,)), pltpu.SemaphoreType.DMA],
    )(x)


HBM = pl.BlockSpec(memory_space=pltpu.HBM)
SEM = pl.BlockSpec(memory_space=pltpu.SEMAPHORE)
N_PEERS = N_DEV - 1


def _peers():
    ax, ay, ac = lax.axis_index("x"), lax.axis_index("y"), lax.axis_index("c")
    flip = lambda v, bit: 1 - v if bit else v
    return [(flip(ax, k & 4), flip(ay, k & 2), flip(ac, k & 1)) for k in range(1, N_DEV)]


def _dev_index(pos):
    return 4 * pos[0] + 2 * pos[1] + pos[2]


def exchange_start(items, name, after=None):
    n = len(items)
    srcs = [a for a, _ in items]
    blocks = [a.shape[1:] if scatter else a.shape for a, scatter in items]
    extra = list(after or ())

    def body(*refs):
        src_refs, land_refs = refs[:n], refs[n:2 * n]
        outs = refs[2 * n + len(extra):]
        send_sems, recv_sems = outs[:n], outs[n:2 * n]
        token = outs[-1]
        me = _dev_index((lax.axis_index("x"), lax.axis_index("y"), lax.axis_index("c")))
        for w, (_, scatter) in enumerate(items):
            for k, peer in enumerate(_peers()):
                src = src_refs[w].at[_dev_index(peer)] if scatter else src_refs[w]
                pltpu.make_async_remote_copy(
                    src_ref=src, dst_ref=land_refs[w].at[me], send_sem=send_sems[w].at[k],
                    recv_sem=recv_sems[w].at[k], device_id=peer, device_id_type=MESH).start()
        token[...] = jnp.zeros_like(token)

    lands = [lax.empty((N_DEV,) + blk, a.dtype) for a, blk in zip(srcs, blocks)]
    res = pl.pallas_call(
        body, name=name,
        out_shape=([pltpu.SemaphoreType.DMA((N_PEERS,))] * (2 * n)
                   + [pltpu.HBM(a.shape, a.dtype) for a in srcs] + [pltpu.HBM(l.shape, l.dtype) for l in lands]
                   + [jax.ShapeDtypeStruct((8, LANES), F32)]),
        in_specs=[HBM] * (2 * n) + [ANY] * len(extra),
        out_specs=[SEM] * (2 * n) + [HBM] * (2 * n) + [pl.BlockSpec(memory_space=pltpu.VMEM)],
        input_output_aliases={i: 2 * n + i for i in range(2 * n)},
        compiler_params=pltpu.CompilerParams(has_side_effects=pltpu.SideEffectType.DATAFLOW_SIDE_EFFECTING),
    )(*[pltpu.with_memory_space_constraint(a, pltpu.HBM) for a in srcs + lands], *extra)
    handles = [(res[w], res[n + w], res[2 * n + w], res[3 * n + w], scatter) for w, (_, scatter) in enumerate(items)]
    return handles, res[-1]


def exchange_wait(handle, after, name):
    send_sem, recv_sem, src, land, scatter = handle

    def body(src_ref, land_ref, send_ref, recv_ref, after_ref, src_out, land_out):
        for k, peer in enumerate(_peers()):
            slot = _dev_index(peer)
            copy = pltpu.make_async_remote_copy(
                src_ref=src_ref.at[slot] if scatter else src_ref, dst_ref=land_ref.at[slot],
                send_sem=send_ref.at[k], recv_sem=recv_ref.at[k], device_id=peer, device_id_type=MESH)
            copy.wait_send()
            copy.wait_recv()

    src, landed = pl.pallas_call(
        body, name=name,
        out_shape=(pltpu.HBM(src.shape, src.dtype), pltpu.HBM(land.shape, land.dtype)),
        in_specs=[HBM, HBM, SEM, SEM, ANY], out_specs=(HBM, HBM), input_output_aliases={0: 0, 1: 1},
        compiler_params=pltpu.CompilerParams(has_side_effects=pltpu.SideEffectType.DATAFLOW_SIDE_EFFECTING),
    )(src, land, send_sem, recv_sem, after)
    dev = _dev_index((lax.axis_index("x"), lax.axis_index("y"), lax.axis_index("c")))
    own = lax.dynamic_index_in_dim(src, dev, axis=0, keepdims=True) if scatter else src[None]
    return lax.dynamic_update_slice_in_dim(landed, own, dev, axis=0)


def rows_call(fn, ins, outs, name):
    rows = ins[0].shape[1]
    per_row = sum(a.shape[0] * a.shape[2] * a.dtype.itemsize for a in ins)
    per_row += sum(l * c * jnp.dtype(dt).itemsize for l, c, dt in outs)
    tr = _divisor_tile(rows, max(16, TILE_BUDGET_BYTES // per_row), 16)
    n_in = len(ins)

    def body(*refs):
        vals = fn(*[r[...] for r in refs[:n_in]])
        if not isinstance(vals, (tuple, list)):
            vals = (vals,)
        for r, v in zip(refs[n_in:], vals):
            r[...] = v.astype(r.dtype)

    def spec(l, c):
        return pl.BlockSpec((l, tr, c), lambda i: (0, i, 0))

    res = pl.pallas_call(
        body, name=name, grid=(rows // tr,),
        in_specs=[spec(a.shape[0], a.shape[2]) for a in ins],
        out_specs=[spec(l, c) for l, c, _ in outs],
        out_shape=[jax.ShapeDtypeStruct((l, rows, c), dt) for l, c, dt in outs],
        compiler_params=_cparams("arbitrary"),
    )(*ins)
    return res


def _as_rows(a, lead=0):
    shape = a.shape
    l = int(math.prod(shape[:lead])) if lead else 1
    rest = shape[lead:]
    c = rest[-1] if rest else 1
    r = int(math.prod(rest[:-1])) if len(rest) > 1 else 1
    return a.reshape(l, r, c)


def act_call(fn, ins, outs, name):
    bsz, seq = ins[0].shape[0], ins[0].shape[1]
    per_row = sum(a.shape[2] * a.dtype.itemsize for a in ins if a.shape[1] == seq)
    per_row += sum(c * jnp.dtype(dt).itemsize for c, dt, kind in outs if kind == "tile")
    ts = _divisor_tile(seq, max(16, TILE_BUDGET_BYTES // per_row), 16)
    n_in = len(ins)

    def in_spec(a):
        c = a.shape[2]
        if a.shape[1] == seq:
            return pl.BlockSpec((None, ts, c), lambda b, s: (b, s, 0))
        if a.shape[0] == bsz:
            return pl.BlockSpec((None, 1, c), lambda b, s: (b, 0, 0))
        return pl.BlockSpec((None, 1, c), lambda b, s: (0, 0, 0))

    def out_spec(c, kind):
        if kind == "tile":
            return pl.BlockSpec((None, ts, c), lambda b, s: (b, s, 0))
        if kind == "seq":
            return pl.BlockSpec((None, 1, c), lambda b, s: (b, 0, 0))
        return pl.BlockSpec((None, 1, c), lambda b, s: (0, 0, 0))

    def out_shape(c, dt, kind):
        if kind == "tile":
            return jax.ShapeDtypeStruct((bsz, seq, c), dt)
        return jax.ShapeDtypeStruct((bsz if kind == "seq" else 1, 1, c), dt)

    def accumulate(ref, v, first):
        @pl.when(first)
        def _():
            ref[...] = jnp.zeros_like(ref)

        ref[...] += v.astype(ref.dtype)

    def body(*refs):
        b, s = pl.program_id(0), pl.program_id(1)
        vals = fn(*[r[...] for r in refs[:n_in]])
        if not isinstance(vals, (tuple, list)):
            vals = (vals,)
        for ref, v, (_, _, kind) in zip(refs[n_in:], vals, outs):
            if kind == "tile":
                ref[...] = v.astype(ref.dtype)
            elif kind == "seq":
                accumulate(ref, v, s == 0)
            else:
                accumulate(ref, v, jnp.logical_and(b == 0, s == 0))

    return pl.pallas_call(
        body, name=name, grid=(bsz, seq // ts),
        in_specs=[in_spec(a) for a in ins],
        out_specs=[out_spec(c, kind) for c, _, kind in outs],
        out_shape=[out_shape(*o) for o in outs],
        compiler_params=_cparams("arbitrary", "arbitrary"),
    )(*ins)


def _mm(name, grid, a, a_spec, b, b_spec, dims, out_shape, out_spec, out_dtypes, acc_steps,
        epi=None, extras=(), extra_specs=()):
    n_ex, n_out = len(extras), len(out_dtypes)
    tile = tuple(d for d in out_spec.block_shape if d is not None)

    def body(*refs):
        a_ref, b_ref = refs[0], refs[1]
        ex_refs = refs[2:2 + n_ex]
        o_refs = refs[2 + n_ex:2 + n_ex + n_out]
        p = lax.dot_general(a_ref[...].astype(BF16), b_ref[...].astype(BF16), (dims, ((), ())),
                            preferred_element_type=F32)

        def finish(acc):
            vals = epi(acc, *[r[...] for r in ex_refs]) if epi is not None else (acc,) * n_out
            for r, v in zip(o_refs, vals):
                r[...] = v.astype(r.dtype)

        if not acc_steps:
            finish(p)
        else:
            acc_ref = refs[-1]
            s = pl.program_id(1)

            @pl.when(s == 0)
            def _():
                acc_ref[...] = p

            @pl.when(s > 0)
            def _():
                acc_ref[...] += p

            @pl.when(s == acc_steps - 1)
            def _():
                finish(acc_ref[...])

    res = pl.pallas_call(
        body, name=name, grid=grid,
        in_specs=[a_spec, b_spec] + list(extra_specs),
        out_specs=[out_spec] * n_out,
        out_shape=[jax.ShapeDtypeStruct(out_shape, dt) for dt in out_dtypes],
        scratch_shapes=[pltpu.VMEM(tile, F32)] if acc_steps else [],
        compiler_params=_cparams("arbitrary", "arbitrary"),
    )(a, b, *extras)
    return res if n_out > 1 else res[0]


def mm_nn_col(a, w, name, out_dtypes=(F32,), epi=None, slabs=None, col_params=()):
    m, k = a.shape
    _, _, nb = w.shape
    first, count = slabs if slabs is not None else (0, w.shape[0])
    tm = _tile_m(m)
    return _mm(name, (m // tm, count), a, pl.BlockSpec((tm, k), lambda i, j: (i, 0)),
               w, pl.BlockSpec((None, k, nb), lambda i, j: (first + j, 0, 0)), ((1,), (0,)),
               (m, count * nb), pl.BlockSpec((tm, nb), lambda i, j: (i, j)), out_dtypes, 0, epi,
               col_params, [pl.BlockSpec((1, nb), lambda i, j: (0, j))] * len(col_params))


def mm_nn_row(a, w, name, out_dtypes=(F32,), epi=None, extras=(), extra_specs=None, tm=None):
    m = a.shape[0]
    ns, kb, n = w.shape
    tm = tm or _tile_m(m)
    return _mm(name, (m // tm, ns), a, pl.BlockSpec((tm, kb), lambda i, s: (i, s)),
               w, pl.BlockSpec((None, kb, n), lambda i, s: (s, 0, 0)), ((1,), (0,)),
               (m, n), pl.BlockSpec((tm, n), lambda i, s: (i, 0)), out_dtypes, ns, epi,
               extras, extra_specs(tm) if extras else ())


def mm_nt_col(dc, w, name, out_dtypes=(F32,), epi=None, extras=()):
    m = dc.shape[0]
    ns, k, nb = w.shape
    tm = _tile_m(m) // 2 if extras else _tile_m(m)
    spec = pl.BlockSpec((tm, k), lambda i, s: (i, 0))
    return _mm(name, (m // tm, ns), dc, pl.BlockSpec((tm, nb), lambda i, s: (i, s)),
               w, pl.BlockSpec((None, k, nb), lambda i, s: (s, 0, 0)), ((1,), (1,)),
               (m, k), spec, out_dtypes, ns, epi, extras, [spec] * len(extras))


def mm_nt_row(dc, w, name, out_dtypes=(F32,), epi=None, extras=()):
    m, n = dc.shape
    ns, kb, _ = w.shape
    tm = _tile_m(m)
    spec = pl.BlockSpec((tm, kb), lambda i, s: (i, s))
    return _mm(name, (m // tm, ns), dc, pl.BlockSpec((tm, n), lambda i, s: (i, 0)),
               w, pl.BlockSpec((None, kb, n), lambda i, s: (s, 0, 0)), ((1,), (1,)),
               (m, ns * kb), spec, out_dtypes, 0, epi, extras, [spec] * len(extras))


def mm_tn(a, c, slab, ns, name, out_dtype=BF16):
    m, ka_all = a.shape
    nc_all = c.shape[1]
    ka = ka_all // ns if slab == "a" else ka_all
    nc = nc_all // ns if slab == "c" else nc_all
    tt = _divisor_tile(m, 256, 16)
    steps = m // tt

    def body(a_ref, c_ref, o_ref, acc_ref):
        t = pl.program_id(0)

        @pl.when(t == 0)
        def _():
            acc_ref[...] = jnp.zeros_like(acc_ref)

        for s in range(ns):
            a_s = a_ref[:, s * ka:(s + 1) * ka] if slab == "a" else a_ref[...]
            c_s = c_ref[:, s * nc:(s + 1) * nc] if slab == "c" else c_ref[...]
            acc_ref[s] += lax.dot_general(a_s.astype(BF16), c_s.astype(BF16), (((0,), (0,)), ((), ())),
                                          preferred_element_type=F32)

        @pl.when(t == steps - 1)
        def _():
            o_ref[...] = acc_ref[...].astype(o_ref.dtype)

    return pl.pallas_call(
        body, name=name, grid=(steps,),
        in_specs=[pl.BlockSpec((tt, ka_all), lambda t: (t, 0)), pl.BlockSpec((tt, nc_all), lambda t: (t, 0))],
        out_specs=pl.BlockSpec((ns, ka, nc), lambda t: (0, 0, 0)),
        out_shape=jax.ShapeDtypeStruct((ns, ka, nc), out_dtype),
        scratch_shapes=[pltpu.VMEM((ns, ka, nc), F32)],
        compiler_params=_cparams("arbitrary"),
    )(a, c)


def slab_sum(landed, name):
    shape = landed.shape[1:]
    total = rows_call(lambda g: jnp.sum(g.astype(F32), axis=0, keepdims=True),
                      [_as_rows(landed, 1)], [(1, shape[-1], F32)], name)[0]
    return total.reshape(shape)


def all_reduce_small(leaves, name):
    sizes = [int(a.size) for a in leaves]
    flat = jnp.concatenate([a.reshape(-1) for a in leaves])
    total = int(flat.size)
    padded = -(-total // (16 * LANES)) * (16 * LANES)
    flat = jnp.pad(flat, (0, padded - total)).reshape(padded // LANES, LANES)
    gathered = all_gather(flat, name + "_gather")
    summed = rows_call(lambda g: jnp.sum(g, axis=0, keepdims=True), [gathered],
                       [(1, LANES, F32)], name + "_sum")[0].reshape(-1)
    out, at = [], 0
    for a, n in zip(leaves, sizes):
        out.append(summed[at:at + n].reshape(a.shape))
        at += n
    return out


def adamw(w, g, m, v, name):
    c = w.shape[-1] if w.ndim else 1

    def fn(w_, g_, m_, v_):
        nm = ADAM_B1 * m_ + (1.0 - ADAM_B1) * g_
        nv = ADAM_B2 * v_ + (1.0 - ADAM_B2) * (g_ * g_)
        m_hat = nm / (1.0 - ADAM_B1 ** ADAM_STEP)
        v_hat = nv / (1.0 - ADAM_B2 ** ADAM_STEP)
        delta = -ADAM_LR * (m_hat / (jnp.sqrt(v_hat) + ADAM_EPS) + ADAM_WD * w_)
        return delta, nm, nv

    res = rows_call(fn, [_as_rows(t) for t in (w, g.astype(F32), m, v)], [(1, c, F32)] * 3, name)
    return tuple(r.reshape(w.shape) for r in res)


def _rowsum(v):
    return jnp.sum(v, axis=0, keepdims=True)


def _norm_mod(x, g, sh, sc):
    n = x * lax.rsqrt(jnp.mean(x * x, axis=-1, keepdims=True) + EPS)
    return (n * g) * (1.0 + sc) + sh


def _norm_mod_bwd(x, g, sc, dh, dres):
    r = lax.rsqrt(jnp.mean(x * x, axis=-1, keepdims=True) + EPS)
    n = x * r
    dy = dh * (1.0 + sc)
    dn = dy * g
    dx = r * (dn - n * jnp.mean(dn * n, axis=-1, keepdims=True))
    return dres + dx, _rowsum(dh), _rowsum(dh * (n * g)), _rowsum(dy * n)


def _head_mean(v):
    low = lax.broadcasted_iota(jnp.int32, (1, LANES), 1) < HEAD_DIM
    parts = []
    for p in range(v.shape[1] // LANES):
        blk = v[:, p * LANES:(p + 1) * LANES]
        s0 = jnp.sum(jnp.where(low, blk, 0.0), axis=-1, keepdims=True)
        s1 = jnp.sum(jnp.where(low, 0.0, blk), axis=-1, keepdims=True)
        parts.append(jnp.where(low, s0, s1))
    return jnp.concatenate(parts, axis=1) * (1.0 / HEAD_DIM)


def _head_norm(x, g):
    return x * lax.rsqrt(_head_mean(x * x) + EPS) * g


def _head_norm_bwd(x, g, dy):
    r = lax.rsqrt(_head_mean(x * x) + EPS)
    n = x * r
    dn = dy * g
    return r * (dn - n * _head_mean(dn * n)), _rowsum(dy * n)


GELU_C = math.sqrt(2.0 / math.pi)
GELU_A = 0.044715


def _gelu_grad(y):
    t = jnp.tanh(GELU_C * (y + GELU_A * y * y * y))
    return 0.5 * (1.0 + t) + 0.5 * y * (1.0 - t * t) * GELU_C * (1.0 + 3.0 * GELU_A * y * y)


def ada_fwd(c_all, w_cols, b_cols):
    def body(c_ref, w_ref, b_ref, o_ref):
        c = c_ref[...]
        s = (c * jax.nn.sigmoid(c)).astype(BF16)
        o_ref[...] = jnp.dot(s, w_ref[...].astype(BF16), preferred_element_type=F32) + b_ref[...]

    return pl.pallas_call(
        body, name="ada_fwd", out_shape=jax.ShapeDtypeStruct((c_all.shape[0], w_cols.shape[1]), F32),
        compiler_params=pltpu.CompilerParams(vmem_limit_bytes=VMEM_LIMIT_BYTES),
    )(c_all, w_cols, b_cols)


def ada_bwd(c_all, dm_cols, dm_all):
    def body(c_ref, d_ref, all_ref, dw_ref, db_ref):
        c = c_ref[...]
        s = (c * jax.nn.sigmoid(c)).astype(BF16)
        dw_ref[...] = lax.dot_general(s, d_ref[...].astype(BF16), (((0,), (0,)), ((), ())),
                                      preferred_element_type=F32)
        db_ref[...] = jnp.sum(all_ref[...], axis=0, keepdims=True)

    return pl.pallas_call(
        body, name="ada_bwd",
        out_shape=[jax.ShapeDtypeStruct((c_all.shape[1], dm_cols.shape[1]), F32),
                   jax.ShapeDtypeStruct((1, dm_all.shape[1]), F32)],
        compiler_params=pltpu.CompilerParams(vmem_limit_bytes=VMEM_LIMIT_BYTES),
    )(c_all, dm_cols, dm_all)


def _s5_discretise(lam_re, lam_im, log_dt, b_re, b_im):
    dt = jnp.exp(log_dt)
    mag = jnp.exp(lam_re * dt)
    ab_re = mag * jnp.cos(lam_im * dt)
    ab_im = mag * jnp.sin(lam_im * dt)
    den = lam_re * lam_re + lam_im * lam_im
    nr = ab_re - 1.0
    ni = ab_im
    f_re = (nr * lam_re + ni * lam_im) / den
    f_im = (ni * lam_re - nr * lam_im) / den
    bb_re = f_re * b_re - f_im * b_im
    bb_im = f_re * b_im + f_im * b_re
    return ab_re, ab_im, bb_re, bb_im


def s5_prep(lam_re, lam_im, log_dt, b_re, b_im):
    gp, h = b_re.shape

    def body(lr, li, ld, br, bi, o_ar, o_ai, o_br, o_bi):
        res = _s5_discretise(lr[...], li[...], ld[...], br[...], bi[...])
        for r, v in zip((o_ar, o_ai, o_br, o_bi), res):
            r[...] = v

    col, mat = jax.ShapeDtypeStruct((gp, 1), F32), jax.ShapeDtypeStruct((gp, h), F32)
    return pl.pallas_call(body, name="s5_prep", out_shape=[col, col, mat, mat])(lam_re, lam_im, log_dt, b_re, b_im)


def s5_prep_bwd(lam_re, lam_im, log_dt, b_re, b_im, d_ab_re, d_ab_im, d_bb_re, d_bb_im):
    gp, h = b_re.shape

    def body(lr, li, ld, br, bi, g_ar, g_ai, g_br, g_bi, o_lr, o_li, o_ld, o_br, o_bi):
        _, vjp = jax.vjp(_s5_discretise, lr[...], li[...], ld[...], br[...], bi[...])
        res = vjp((g_ar[...], g_ai[...], g_br[...], g_bi[...]))
        for r, v in zip((o_lr, o_li, o_ld, o_br, o_bi), res):
            r[...] = v

    col, mat = jax.ShapeDtypeStruct((gp, 1), F32), jax.ShapeDtypeStruct((gp, h), F32)
    return pl.pallas_call(body, name="s5_prep_bwd", out_shape=[col, col, col, mat, mat])(
        lam_re, lam_im, log_dt, b_re, b_im, d_ab_re, d_ab_im, d_bb_re, d_bb_im)


def _s5_chunk(seq):
    return _divisor_tile(seq, 256, 16)


def s5_fwd(u, bbd_re, bbd_im, cbd_re, cbd_im, ab_re, ab_im, dskip):
    bsz, seq, d = u.shape
    nb, cb, ns = bbd_re.shape
    lc = _s5_chunk(seq)

    def body(u_ref, bre_ref, bim_ref, cre_ref, cim_ref, ar_ref, ai_ref, d_ref, y_ref, ge_ref, sre_ref, sim_ref,
             carry_re, carry_im):
        t = pl.program_id(1)

        @pl.when(t == 0)
        def _():
            carry_re[...] = jnp.zeros_like(carry_re)
            carry_im[...] = jnp.zeros_like(carry_im)

        for b in range(bsz):
            ub = u_ref[b].astype(BF16)
            sre_ref[b] = jnp.dot(ub, bre_ref[...].astype(BF16), preferred_element_type=F32)
            sim_ref[b] = jnp.dot(ub, bim_ref[...].astype(BF16), preferred_element_type=F32)
        ar, ai = ar_ref[...], ai_ref[...]

        def step(i, carry):
            row = pl.ds(i, 1)
            out = []
            for b, (cr, ci) in enumerate(carry):
                nr = ar * cr - ai * ci + sre_ref[b, row, :]
                ni = ar * ci + ai * cr + sim_ref[b, row, :]
                sre_ref[b, row, :] = nr
                sim_ref[b, row, :] = ni
                out.append((nr, ni))
            return tuple(out)

        init = tuple((carry_re[b], carry_im[b]) for b in range(bsz))
        last = lax.fori_loop(0, lc, step, init, unroll=8)
        for b, (cr, ci) in enumerate(last):
            carry_re[b] = cr
            carry_im[b] = ci
            y = jnp.dot(sre_ref[b].astype(BF16), cre_ref[...].astype(BF16), preferred_element_type=F32)
            y -= jnp.dot(sim_ref[b].astype(BF16), cim_ref[...].astype(BF16), preferred_element_type=F32)
            y = y + d_ref[...] * u_ref[b]
            y_ref[b] = y
            ge_ref[b] = jax.nn.gelu(y).astype(BF16)

    chan = pl.BlockSpec((bsz, lc, cb), lambda n, t: (0, t, n))
    state = pl.BlockSpec((bsz, lc, ns), lambda n, t: (0, t, n))
    par = lambda r, c: pl.BlockSpec((None, r, c), lambda n, t: (n, 0, 0))
    return pl.pallas_call(
        body, name="s5_fwd", grid=(nb, seq // lc),
        in_specs=[chan, par(cb, ns), par(cb, ns), par(ns, cb), par(ns, cb), par(1, ns), par(1, ns),
                  pl.BlockSpec((None, 1, cb), lambda n, t: (0, 0, n))],
        out_specs=[chan, chan, state, state],
        out_shape=[jax.ShapeDtypeStruct((bsz, seq, d), F32), jax.ShapeDtypeStruct((bsz, seq, d), BF16),
                   jax.ShapeDtypeStruct((bsz, seq, nb * ns), F32),
                   jax.ShapeDtypeStruct((bsz, seq, nb * ns), F32)],
        scratch_shapes=[pltpu.VMEM((bsz, 1, ns), F32), pltpu.VMEM((bsz, 1, ns), F32)],
        compiler_params=_cparams("arbitrary", "arbitrary"),
    )(u, bbd_re, bbd_im, cbd_re, cbd_im, ab_re, ab_im, dskip)


def s5_bwd(dy, u, st_re, st_im, bbd_re, bbd_im, cbd_re, cbd_im, ab_re, ab_im, dskip):
    bsz, seq, d = u.shape
    nb, cb, ns = bbd_re.shape
    lc = _s5_chunk(seq)
    nc = seq // lc

    def body(dy_ref, u_ref, sre_ref, sim_ref, bre_ref, bim_ref, cre_ref, cim_ref, ar_ref, ai_ref, d_ref,
             du_ref, dbre_out, dbim_out, dcre_out, dcim_out, dar_ref, dai_ref, dd_ref,
             g_re, g_im, gs_re, gs_im, carry_re, carry_im, dbre_ref, dbim_ref, dcre_ref, dcim_ref):
        t = pl.program_id(1)

        @pl.when(t == 0)
        def _():
            for r in (dbre_ref, dbim_ref, dcre_ref, dcim_ref, dar_ref, dai_ref, dd_ref, carry_re, carry_im):
                r[...] = jnp.zeros_like(r)

        nt = (((1,), (1,)), ((), ()))
        tn = (((0,), (0,)), ((), ()))
        for b in range(bsz):
            dyb = dy_ref[b].astype(BF16)
            g_re[b] = lax.dot_general(dyb, cre_ref[...].astype(BF16), nt, preferred_element_type=F32)
            g_im[b] = -lax.dot_general(dyb, cim_ref[...].astype(BF16), nt, preferred_element_type=F32)
        ar, ai = ar_ref[...], ai_ref[...]

        def step(k, carry):
            row = pl.ds(lc - 1 - k, 1)
            out = []
            for b, (cr, ci) in enumerate(carry):
                gs_re[b, row, :] = cr
                gs_im[b, row, :] = ci
                nr = ar * cr + ai * ci + g_re[b, row, :]
                ni = ar * ci - ai * cr + g_im[b, row, :]
                g_re[b, row, :] = nr
                g_im[b, row, :] = ni
                out.append((nr, ni))
            return tuple(out)

        init = tuple((carry_re[b], carry_im[b]) for b in range(bsz))
        last = lax.fori_loop(0, lc, step, init, unroll=8)
        for b, (cr, ci) in enumerate(last):
            carry_re[b] = cr
            carry_im[b] = ci
            dyf, uf = dy_ref[b], u_ref[b]
            dyb, ub = dyf.astype(BF16), uf.astype(BF16)
            sr, si = sre_ref[b], sim_ref[b]
            hr, hi = gs_re[b], gs_im[b]
            dar_ref[...] += _rowsum(hr * sr + hi * si)
            dai_ref[...] += _rowsum(hi * sr - hr * si)
            gr, gi = g_re[b].astype(BF16), g_im[b].astype(BF16)
            du = lax.dot_general(gr, bre_ref[...].astype(BF16), nt, preferred_element_type=F32)
            du += lax.dot_general(gi, bim_ref[...].astype(BF16), nt, preferred_element_type=F32)
            du_ref[b] = du + d_ref[...] * dyf
            dbre_ref[...] += lax.dot_general(ub, gr, tn, preferred_element_type=F32)
            dbim_ref[...] += lax.dot_general(ub, gi, tn, preferred_element_type=F32)
            dcre_ref[...] += lax.dot_general(sr.astype(BF16), dyb, tn, preferred_element_type=F32)
            dcim_ref[...] -= lax.dot_general(si.astype(BF16), dyb, tn, preferred_element_type=F32)
            dd_ref[...] += _rowsum(dyf * uf)

        @pl.when(t == nc - 1)
        def _():
            for k in range(cb // S5_GROUP):
                chans = slice(k * S5_GROUP, (k + 1) * S5_GROUP)
                states = slice(k * S5_STATE, (k + 1) * S5_STATE)
                dbre_out[chans, :] = dbre_ref[chans, states]
                dbim_out[chans, :] = dbim_ref[chans, states]
                dcre_out[states, :] = dcre_ref[states, chans]
                dcim_out[states, :] = dcim_ref[states, chans]

    chan = pl.BlockSpec((bsz, lc, cb), lambda n, t: (0, nc - 1 - t, n))
    state = pl.BlockSpec((bsz, lc, ns), lambda n, t: (0, nc - 1 - t, n))
    par = lambda r, c: pl.BlockSpec((None, r, c), lambda n, t: (n, 0, 0))
    return pl.pallas_call(
        body, name="s5_bwd", grid=(nb, nc),
        in_specs=[chan, chan, state, state, par(cb, ns), par(cb, ns), par(ns, cb), par(ns, cb),
                  par(1, ns), par(1, ns), pl.BlockSpec((None, 1, cb), lambda n, t: (0, 0, n))],
        out_specs=[chan, par(cb, S5_STATE), par(cb, S5_STATE), par(ns, S5_GROUP), par(ns, S5_GROUP),
                   par(1, ns), par(1, ns), par(1, cb)],
        out_shape=[jax.ShapeDtypeStruct((bsz, seq, d), F32),
                   jax.ShapeDtypeStruct((nb, cb, S5_STATE), F32), jax.ShapeDtypeStruct((nb, cb, S5_STATE), F32),
                   jax.ShapeDtypeStruct((nb, ns, S5_GROUP), F32), jax.ShapeDtypeStruct((nb, ns, S5_GROUP), F32),
                   jax.ShapeDtypeStruct((nb, 1, ns), F32), jax.ShapeDtypeStruct((nb, 1, ns), F32),
                   jax.ShapeDtypeStruct((nb, 1, cb), F32)],
        scratch_shapes=([pltpu.VMEM((bsz, lc, ns), F32)] * 4 + [pltpu.VMEM((bsz, 1, ns), F32)] * 2
                        + [pltpu.VMEM((cb, ns), F32)] * 2 + [pltpu.VMEM((ns, cb), F32)] * 2),
        compiler_params=_cparams("arbitrary", "arbitrary"),
    )(dy, u, st_re, st_im, bbd_re, bbd_im, cbd_re, cbd_im, ab_re, ab_im, dskip)


ATT_HEADS = 8
ATT_HEADS_FWD = 8
ATT_LANES = ATT_HEADS * HEAD_DIM
ATT_KEYS = 2 * ATT_BLOCK
ATT_Q = 256
ATT_SCALE = 1.0 / math.sqrt(HEAD_DIM)
_NT = (((1,), (1,)), ((), ()))
_TN = (((0,), (0,)), ((), ()))
_HEADS = [slice(h * HEAD_DIM, (h + 1) * HEAD_DIM) for h in range(ATT_HEADS)]
_HALF = [slice(0, ATT_BLOCK), slice(ATT_BLOCK, ATT_KEYS)]


def _log_sigmoids(z):
    sp = jnp.log(1.0 + jnp.exp(-jnp.abs(z)))
    ls = jnp.minimum(z, 0.0) - sp
    return ls, ls - z


def _sum_matrix(after, inclusive):
    j = lax.broadcasted_iota(jnp.int32, (ATT_KEYS, ATT_KEYS), 0) % ATT_BLOCK
    s = lax.broadcasted_iota(jnp.int32, (ATT_KEYS, ATT_KEYS), 1)
    if after:
        hit = (j >= s) if inclusive else (j > s)
    else:
        hit = (j <= s) if inclusive else (j < s)
    return jnp.where(jnp.logical_or(hit, s >= ATT_BLOCK), 1.0, 0.0).astype(BF16)


def _hi_lo(v):
    hi = v.astype(BF16)
    lo = (v - hi.astype(F32)).astype(BF16)
    return jnp.concatenate([hi, lo], axis=1)


def _strict_mask(i, j):
    t = i * ATT_Q + lax.broadcasted_iota(jnp.int32, (ATT_Q, ATT_KEYS), 0)
    s = j * ATT_KEYS + lax.broadcasted_iota(jnp.int32, (ATT_Q, ATT_KEYS), 1)
    return s < t


def attention_fwd(q, k, v):
    bsz, seq, d = q.shape
    n_heads = ATT_HEADS_FWD if d % (ATT_HEADS_FWD * HEAD_DIM) == 0 else ATT_HEADS
    lanes = n_heads * HEAD_DIM
    heads = [slice(h * HEAD_DIM, (h + 1) * HEAD_DIM) for h in range(n_heads)]

    def body(q_ref, k_ref, v_ref, o_ref, tot_ref, z_buf, ls_buf, cs_buf, acc_buf, run_buf):
        i = pl.program_id(2)
        jd = ((i + 1) * ATT_Q - 1) // ATT_KEYS
        sums = _sum_matrix(True, False)
        acc_buf[...] = jnp.zeros_like(acc_buf)
        run_buf[...] = jnp.zeros_like(run_buf)

        def block(j, masked):
            rows = pl.ds(pl.multiple_of(j * ATT_KEYS, ATT_KEYS), ATT_KEYS)
            strict = _strict_mask(i, j) if masked else None
            for h, ln in enumerate(heads):
                z_buf[h] = lax.dot_general(q_ref[:, ln], k_ref[rows, ln], _NT, preferred_element_type=F32)
            for h in range(n_heads):
                for half, cols in enumerate(_HALF):
                    ls, lf = _log_sigmoids(z_buf[h, :, cols])
                    if masked:
                        lf = jnp.where(strict[:, cols], lf, 0.0)
                    ls_buf[h, :, cols] = ls
                    cs_buf[h, half] = jnp.dot(_hi_lo(lf), sums, preferred_element_type=F32)
            for h, ln in enumerate(heads):
                run = run_buf[h]
                late, early = cs_buf[h, 1], cs_buf[h, 0]
                a1 = run + late[:, _HALF[0]]
                run = run + late[:, _HALF[1]]
                a0 = run + early[:, _HALF[0]]
                run_buf[h] = run + early[:, _HALF[1]]
                w = jnp.exp(ls_buf[h] + jnp.concatenate([a0, a1], axis=1))
                if masked:
                    w = jnp.where(strict, w, 0.0)
                acc_buf[h] += jnp.dot(w.astype(BF16), v_ref[rows, ln], preferred_element_type=F32)

        block(jd, True)

        def step(it, carry):
            block(jd - 1 - it, False)
            return carry

        lax.fori_loop(0, jd, step, 0)
        o_ref[...] = jnp.concatenate([acc_buf[h] for h in range(n_heads)], axis=1).astype(o_ref.dtype)
        tot_ref[...] = jnp.concatenate([run_buf[h, :, :HEAD_DIM] for h in range(n_heads)], axis=1)

    blk = pl.BlockSpec((None, ATT_Q, lanes), lambda b, p, i: (b, i, p))
    full = pl.BlockSpec((None, seq, lanes), lambda b, p, i: (b, 0, p))
    tile = (n_heads, ATT_Q, ATT_KEYS)
    return pl.pallas_call(
        body, name="attention_fwd", grid=(bsz, d // lanes, seq // ATT_Q),
        in_specs=[blk, full, full], out_specs=[blk, blk],
        out_shape=[jax.ShapeDtypeStruct((bsz, seq, d), BF16), jax.ShapeDtypeStruct((bsz, seq, d), F32)],
        scratch_shapes=[pltpu.VMEM(tile, F32), pltpu.VMEM(tile, F32),
                        pltpu.VMEM((n_heads, 2, ATT_Q, ATT_KEYS), F32),
                        pltpu.VMEM((n_heads, ATT_Q, HEAD_DIM), F32),
                        pltpu.VMEM((n_heads, ATT_Q, ATT_BLOCK), F32)],
        compiler_params=_cparams("arbitrary", "arbitrary", "arbitrary"),
    )(q, k, v)


def attention_bwd(q, k, v, tot, do):
    bsz, seq, d = q.shape

    def body(q_ref, k_ref, v_ref, tot_ref, do_ref, dq_ref, dk_ref, dv_ref,
             z_buf, dw_buf, ls_buf, e_buf, up_buf, bf_buf, w_buf, do_buf, dq_buf, tot_buf, run_buf, erun_buf):
        i = pl.program_id(2)
        jd = ((i + 1) * ATT_Q - 1) // ATT_KEYS

        @pl.when(i == 0)
        def _():
            dk_ref[...] = jnp.zeros_like(dk_ref)
            dv_ref[...] = jnp.zeros_like(dv_ref)

        upto_incl, upto_excl = _sum_matrix(False, True), _sum_matrix(False, False)
        do_buf[...] = do_ref[...].astype(BF16)
        for h, ln in enumerate(_HEADS):
            tot_buf[h] = jnp.concatenate([tot_ref[:, ln], tot_ref[:, ln]], axis=1)
        dq_buf[...] = jnp.zeros_like(dq_buf)
        run_buf[...] = jnp.zeros_like(run_buf)
        erun_buf[...] = jnp.zeros_like(erun_buf)

        def block(j, masked):
            rows = pl.ds(pl.multiple_of(j * ATT_KEYS, ATT_KEYS), ATT_KEYS)
            strict = _strict_mask(i, j) if masked else None
            for h, ln in enumerate(_HEADS):
                z_buf[h] = lax.dot_general(q_ref[:, ln], k_ref[rows, ln], _NT, preferred_element_type=F32)
                dw_buf[h] = lax.dot_general(do_buf[:, ln], v_ref[rows, ln], _NT, preferred_element_type=F32)
            for h in range(ATT_HEADS):
                for half, cols in enumerate(_HALF):
                    ls, lf = _log_sigmoids(z_buf[h, :, cols])
                    if masked:
                        lf = jnp.where(strict[:, cols], lf, 0.0)
                    ls_buf[h, :, cols] = ls
                    up_buf[h, half] = jnp.dot(_hi_lo(lf), upto_incl, preferred_element_type=F32)
            for h in range(ATT_HEADS):
                run = run_buf[h]
                early, late = up_buf[h, 0], up_buf[h, 1]
                u0 = run + early[:, _HALF[0]]
                run = run + early[:, _HALF[1]]
                u1 = run + late[:, _HALF[0]]
                run_buf[h] = run + late[:, _HALF[1]]
                tot_h = tot_buf[h]
                after = jnp.concatenate([tot_h - u0, tot_h - u1], axis=1)
                w = jnp.exp(ls_buf[h] + after)
                if masked:
                    w = jnp.where(strict, w, 0.0)
                w_buf[h] = w.astype(BF16)
                e = dw_buf[h] * w
                e_buf[h] = e
                for half, cols in enumerate(_HALF):
                    bf_buf[h, half] = jnp.dot(_hi_lo(e[:, cols]), upto_excl, preferred_element_type=F32)
            dks, dvs = [], []
            for h, ln in enumerate(_HEADS):
                erun = erun_buf[h]
                early, late = bf_buf[h, 0], bf_buf[h, 1]
                b0 = erun + early[:, _HALF[0]]
                erun = erun + early[:, _HALF[1]]
                b1 = erun + late[:, _HALF[0]]
                erun_buf[h] = erun + late[:, _HALF[1]]
                e = e_buf[h]
                dz = e - jnp.exp(ls_buf[h]) * (e + jnp.concatenate([b0, b1], axis=1))
                if masked:
                    dz = jnp.where(strict, dz, 0.0)
                dz = dz.astype(BF16)
                dq_buf[h] += jnp.dot(dz, k_ref[rows, ln], preferred_element_type=F32)
                dks.append(lax.dot_general(dz, q_ref[:, ln], _TN, preferred_element_type=F32))
                dvs.append(lax.dot_general(w_buf[h], do_buf[:, ln], _TN, preferred_element_type=F32))
            dk_ref[rows, :] += jnp.concatenate(dks, axis=1)
            dv_ref[rows, :] += jnp.concatenate(dvs, axis=1)

        def step(j, carry):
            block(j, False)
            return carry

        lax.fori_loop(0, jd, step, 0)
        block(jd, True)
        dq_ref[...] = jnp.concatenate([dq_buf[h] for h in range(ATT_HEADS)], axis=1) * ATT_SCALE

    blk = pl.BlockSpec((None, ATT_Q, ATT_LANES), lambda b, p, i: (b, i, p))
    full = pl.BlockSpec((None, seq, ATT_LANES), lambda b, p, i: (b, 0, p), pipeline_mode=pl.Buffered(1))
    shape = jax.ShapeDtypeStruct((bsz, seq, d), F32)
    tile = (ATT_HEADS, ATT_Q, ATT_KEYS)
    pair = (ATT_HEADS, 2, ATT_Q, ATT_KEYS)
    square = (ATT_HEADS, ATT_Q, ATT_BLOCK)
    return pl.pallas_call(
        body, name="attention_bwd", grid=(bsz, d // ATT_LANES, seq // ATT_Q),
        in_specs=[blk, full, full, blk, blk], out_specs=[blk, full, full], out_shape=[shape, shape, shape],
        scratch_shapes=[pltpu.VMEM(tile, F32), pltpu.VMEM(tile, F32), pltpu.VMEM(tile, F32), pltpu.VMEM(tile, F32),
                        pltpu.VMEM(pair, F32), pltpu.VMEM(pair, F32), pltpu.VMEM(tile, BF16),
                        pltpu.VMEM((ATT_Q, ATT_LANES), BF16), pltpu.VMEM((ATT_HEADS, ATT_Q, HEAD_DIM), F32),
                        pltpu.VMEM(square, F32), pltpu.VMEM(square, F32), pltpu.VMEM(square, F32)],
        compiler_params=_cparams("arbitrary", "arbitrary", "arbitrary"),
    )(q, k, v, tot, do)


def _tile_fused(seq):
    return _divisor_tile(seq, 1024, 16)


MLP_SLABS = 1


def mlp_core_fwd(h, w1, w2, x, gate, name):
    bsz, seq, d = x.shape
    ns, _, fs = w1.shape
    t = bsz * seq
    tm = _tile_fused(seq)
    g = MLP_SLABS if ns % MLP_SLABS == 0 else 1
    steps = ns // g

    def body(h_ref, w1_ref, w2_ref, x_ref, g_ref, act_ref, ff_ref, out_ref, acc_ref):
        s = pl.program_id(1)
        hb = h_ref[...]
        part = None
        for k in range(g):
            pre = jnp.dot(hb, w1_ref[k], preferred_element_type=F32)
            act = jnp.square(jnp.maximum(pre, 0.0)).astype(BF16)
            act_ref[:, k * fs:(k + 1) * fs] = act
            p = jnp.dot(act, w2_ref[k], preferred_element_type=F32)
            part = p if part is None else part + p

        @pl.when(s == 0)
        def _():
            acc_ref[...] = part

        @pl.when(s > 0)
        def _():
            acc_ref[...] += part

        @pl.when(s == steps - 1)
        def _():
            ff = acc_ref[...]
            ff_ref[...] = ff
            out_ref[...] = x_ref[...] + g_ref[...] * ff

    rows = pl.BlockSpec((tm, d), lambda i, s: (i, 0))
    act, ff, out = pl.pallas_call(
        body, name=name, grid=(t // tm, steps),
        in_specs=[rows, pl.BlockSpec((g, d, fs), lambda i, s: (s, 0, 0)),
                  pl.BlockSpec((g, fs, d), lambda i, s: (s, 0, 0)), rows,
                  pl.BlockSpec((None, 1, d), lambda i, s: ((i * tm) // seq, 0, 0))],
        out_specs=[pl.BlockSpec((tm, g * fs), lambda i, s: (i, s)), rows, rows],
        out_shape=[jax.ShapeDtypeStruct((t, ns * fs), BF16), jax.ShapeDtypeStruct((t, d), F32),
                   jax.ShapeDtypeStruct((t, d), F32)],
        scratch_shapes=[pltpu.VMEM((tm, d), F32)],
        compiler_params=_cparams("arbitrary", "arbitrary"),
    )(h.reshape(t, d), w1, w2, x.reshape(t, d), gate)
    return act, ff.reshape(bsz, seq, d), out.reshape(bsz, seq, d)


def mlp_core_bwd(dff, act, w1, w2, name):
    t, d = dff.shape
    ns, _, fs = w1.shape
    tm = _tile_fused(t)
    g = MLP_SLABS if ns % MLP_SLABS == 0 else 1
    steps = ns // g

    def body(dff_ref, act_ref, w1_ref, w2_ref, dpre_ref, dh_ref, acc_ref):
        s = pl.program_id(1)
        db = dff_ref[...]
        part = None
        for k in range(g):
            cols = slice(k * fs, (k + 1) * fs)
            dact = lax.dot_general(db, w2_ref[k], _NT, preferred_element_type=F32)
            dpre = (dact * (2.0 * jnp.sqrt(act_ref[:, cols].astype(F32)))).astype(BF16)
            dpre_ref[:, cols] = dpre
            p = lax.dot_general(dpre, w1_ref[k], _NT, preferred_element_type=F32)
            part = p if part is None else part + p

        @pl.when(s == 0)
        def _():
            acc_ref[...] = part

        @pl.when(s > 0)
        def _():
            acc_ref[...] += part

        @pl.when(s == steps - 1)
        def _():
            dh_ref[...] = acc_ref[...]

    rows = pl.BlockSpec((tm, d), lambda i, s: (i, 0))
    slab = pl.BlockSpec((tm, g * fs), lambda i, s: (i, s))
    return pl.pallas_call(
        body, name=name, grid=(t // tm, steps),
        in_specs=[rows, slab, pl.BlockSpec((g, d, fs), lambda i, s: (s, 0, 0)),
                  pl.BlockSpec((g, fs, d), lambda i, s: (s, 0, 0))],
        out_specs=[slab, rows],
        out_shape=[jax.ShapeDtypeStruct((t, ns * fs), BF16), jax.ShapeDtypeStruct((t, d), F32)],
        scratch_shapes=[pltpu.VMEM((tm, d), F32)],
        compiler_params=_cparams("arbitrary", "arbitrary"),
    )(dff, act, w1, w2)


def mlp_fwd(x, g, sh, sc, gate, w1_handle, w2_handle, tag):
    bsz, seq, d = x.shape
    h = act_call(_norm_mod, [x, g, sh, sc], [(d, BF16, "tile")], tag + "_norm")[0]
    w1 = exchange_wait(w1_handle, h, tag + "_w1_wait")
    w2 = exchange_wait(w2_handle, h, tag + "_w2_wait")
    act, ff, out = mlp_core_fwd(h, w1, w2, x, gate, tag + "_core")
    return out, (h, act, ff), w1, w2


def _gate_bwd(dx, f, gate):
    return gate * dx, _rowsum(dx * f)


def _gate_bwd_outs(d):
    return [(d, BF16, "tile"), (d, F32, "seq")]


def _norm_bwd_outs(d):
    return [(d, F32, "tile"), (d, F32, "seq"), (d, F32, "seq"), (d, F32, "all")]


def _norm_then_gate_bwd(x, g, sc, dh, dres, f, gate):
    res = _norm_mod_bwd(x, g, sc, dh, dres)
    return (*res, *_gate_bwd(res[0], f, gate))


def mlp_bwd(dout, dff, x, g, sc, w1, w2, saved, tag, branch=None):
    bsz, seq, d = x.shape
    t = bsz * seq
    ns = w1.shape[0]
    h, act, _ = saved
    dff = dff.reshape(t, d)
    dpre, dh = mlp_core_bwd(dff, act, w1, w2, tag + "_dcore")
    dw2 = mm_tn(act, dff, "a", ns, tag + "_dw2")
    dw1 = mm_tn(h.reshape(t, d), dpre, "c", ns, tag + "_dw1")
    (dw1_handle, dw2_handle), token = exchange_start([(dw1, True), (dw2, True)], tag + "_dw_start")
    ins = [x, g + token[0, 0], sc, dh.reshape(bsz, seq, d), dout]
    if branch is None:
        dx, dsh, dsc, dg = act_call(_norm_mod_bwd, ins, _norm_bwd_outs(d), tag + "_dnorm")
        into_branch = None
    else:
        dx, dsh, dsc, dg, *into_branch = act_call(_norm_then_gate_bwd, ins + list(branch),
                                                  _norm_bwd_outs(d) + _gate_bwd_outs(d), tag + "_dnorm")
    return dx, dw1_handle, dw2_handle, (dsh, dsc, dg), into_branch


def _block_diag(m, rows_first):
    nb, k, r, c = m.shape
    eye = jnp.eye(k, dtype=m.dtype)
    return jnp.einsum("nkrc,kl->nkrlc", m, eye).reshape(nb, k * r, k * c)


def kernel(x, c, ada_w, ada_b, mix_norm_g, mlp_norm_g, mlp_w1, mlp_w2, s5_a_re, s5_a_im, s5_log_dt, s5_b_re, s5_b_im, s5_c_re, s5_c_im, s5_d, s5_w_glu, kv_ada_w, kv_ada_b, kv_norm_g, w_kv, k_norm_g, sb_w_q, q_norm_g, sb_w_o, loss_target, m_ada_w, m_ada_b, m_mix_norm_g, m_mlp_norm_g, m_mlp_w1, m_mlp_w2, m_s5_a_re, m_s5_a_im, m_s5_log_dt, m_s5_b_re, m_s5_b_im, m_s5_c_re, m_s5_c_im, m_s5_d, m_s5_w_glu, m_kv_ada_w, m_kv_ada_b, m_kv_norm_g, m_w_kv, m_k_norm_g, m_sb_w_q, m_q_norm_g, m_sb_w_o, v_ada_w, v_ada_b, v_mix_norm_g, v_mlp_norm_g, v_mlp_w1, v_mlp_w2, v_s5_a_re, v_s5_a_im, v_s5_log_dt, v_s5_b_re, v_s5_b_im, v_s5_c_re, v_s5_c_im, v_s5_d, v_s5_w_glu, v_kv_ada_w, v_kv_ada_b, v_kv_norm_g, v_w_kv, v_k_norm_g, v_sb_w_q, v_q_norm_g, v_sb_w_o):
    bsz, seq, d = x.shape
    t = bsz * seq
    n_groups = d // S5_GROUP
    nb = n_groups // S5_BLOCK_GROUPS
    gp = n_groups * S5_STATE
    dev = 4 * lax.axis_index("x") + 2 * lax.axis_index("y") + lax.axis_index("c")
    e_ada, e_kv = 6 * d, 2 * d
    n_ada, n_kv = e_ada // N_DEV, e_kv // N_DEV

    d_skip = all_gather(s5_d, "gather_skip").reshape(1, 1, d)
    c_all = all_gather(c, "gather_c").reshape(N_DEV * bsz, d)

    w_cols = jnp.concatenate([ada_w[0], ada_w[1], kv_ada_w], axis=1)
    b_cols = jnp.concatenate([
        lax.dynamic_slice_in_dim(ada_b[0], dev * n_ada, n_ada),
        lax.dynamic_slice_in_dim(ada_b[1], dev * n_ada, n_ada),
        lax.dynamic_slice_in_dim(kv_ada_b, dev * n_kv, n_kv)])[None, :]
    mod_cols = ada_fwd(c_all, w_cols, b_cols)
    mod_all = all_gather(mod_cols, "gather_mod")
    mod_mine = lax.dynamic_slice_in_dim(mod_all, dev * bsz, bsz, axis=1)
    mod_mine = jnp.transpose(mod_mine, (1, 0, 2))
    mods = []
    for i in range(2):
        full = mod_mine[:, :, i * n_ada:(i + 1) * n_ada].reshape(bsz, e_ada)
        mods.append([full[:, None, j * d:(j + 1) * d] for j in range(6)])
    kv_full = mod_mine[:, :, 2 * n_ada:].reshape(bsz, e_kv)
    kv_sh, kv_sc = kv_full[:, None, :d], kv_full[:, None, d:]

    par = lambda p: p.reshape(1, 1, -1)

    shards = [s5_w_glu[0], mlp_w1[0], mlp_w2[0], w_kv, sb_w_q[0], sb_w_o[0], mlp_w1[1], mlp_w2[1]]
    gathers, gather_token = exchange_start([(w.astype(BF16), False) for w in shards], "gather_start", after=[mod_all, d_skip])
    glu_handle, w1_0_handle, w2_0_handle, wkv_handle, wq_handle, wo_handle, w1_1_handle, w2_1_handle = gathers
    started = gather_token[0, 0]

    sh_a, sc_a, g_a, sh_m, sc_m, g_m = mods[0]
    lam_re, lam_im = s5_a_re.reshape(gp, 1), s5_a_im.reshape(gp, 1)
    log_dt = jnp.broadcast_to(s5_log_dt.reshape(n_groups, 1), (n_groups, S5_STATE)).reshape(gp, 1)
    b_re, b_im = s5_b_re.reshape(gp, S5_GROUP), s5_b_im.reshape(gp, S5_GROUP)
    ab_re, ab_im, bb_re, bb_im = s5_prep(lam_re, lam_im, log_dt, b_re, b_im)
    to_bbd = lambda m: _block_diag(jnp.swapaxes(m.reshape(nb, S5_BLOCK_GROUPS, S5_STATE, S5_GROUP), 2, 3), True)
    to_cbd = lambda m: _block_diag(jnp.swapaxes(m.reshape(nb, S5_BLOCK_GROUPS, S5_GROUP, S5_STATE), 2, 3), True)
    bbd_re, bbd_im = to_bbd(bb_re), to_bbd(bb_im)
    cbd_re, cbd_im = to_cbd(s5_c_re[0]), to_cbd(s5_c_im[0])
    abr, abi = ab_re.reshape(nb, 1, -1), ab_im.reshape(nb, 1, -1)

    h0 = act_call(_norm_mod, [x, par(mix_norm_g[0]) + started, sh_a, sc_a], [(d, F32, "tile")], "mix0_norm")[0]
    y, ge, st_re, st_im = s5_fwd(h0, bbd_re, bbd_im, cbd_re, cbd_im, abr, abi, d_skip)
    w_glu = exchange_wait(glu_handle, ge, "glu_w_wait")
    z = mm_nn_col(ge.reshape(t, d), w_glu, "glu_up").reshape(bsz, seq, 2 * d)
    x1 = act_call(lambda x_, z_, g_: x_ + g_ * (z_[:, :d] * jax.nn.sigmoid(z_[:, d:])), [x, z, g_a],
                  [(d, F32, "tile")], "glu_res")[0]
    x2, mlp0_saved, w1_0, w2_0 = mlp_fwd(x1, par(mlp_norm_g[0]), sh_m, sc_m, g_m, w1_0_handle, w2_0_handle, "mlp0")

    sh_a1, sc_a1, g_a1, sh_m1, sc_m1, g_m1 = mods[1]
    kg = par(jnp.tile(k_norm_g, d // HEAD_DIM))
    qg = par(jnp.tile(q_norm_g[0], d // HEAD_DIM))
    hkv = act_call(_norm_mod, [x2, par(kv_norm_g), kv_sh, kv_sc], [(d, BF16, "tile")], "kv_norm")[0]
    wkv = exchange_wait(wkv_handle, hkv, "kv_w_wait")
    half = N_DEV // 2
    k_raw, k_h = mm_nn_col(hkv.reshape(t, d), wkv, "k_proj", (F32, BF16),
                           lambda acc, g_: (acc, _head_norm(acc, g_)), (0, half), (kg.reshape(1, d),))
    v_h = mm_nn_col(hkv.reshape(t, d), wkv, "v_proj", (BF16,), None, (half, half))
    k_raw, k_h, v_h = (a.reshape(bsz, seq, d) for a in (k_raw, k_h, v_h))
    h1 = act_call(_norm_mod, [x2, par(mix_norm_g[1]), sh_a1, sc_a1], [(d, BF16, "tile")], "mix1_norm")[0]
    wq = exchange_wait(wq_handle, h1, "q_w_wait")
    whole = lambda w: w.reshape(1, d, d)
    tm_epi = _tile_fused(seq)
    vec = lambda tm: [pl.BlockSpec((1, d), lambda i, s: (0, 0))]
    q_raw, q_h = mm_nn_row(h1.reshape(t, d), whole(wq), "q_proj", (F32, BF16),
                           lambda acc, g_: (acc, _head_norm(acc, g_) * ATT_SCALE), (qg.reshape(1, d),), vec, tm_epi)
    q_raw, q_h = q_raw.reshape(bsz, seq, d), q_h.reshape(bsz, seq, d)
    o, att_tot = attention_fwd(q_h, k_h, v_h)
    wo = exchange_wait(wo_handle, o, "o_w_wait")
    res_specs = lambda tm: [pl.BlockSpec((tm, d), lambda i, s: (i, 0)),
                            pl.BlockSpec((None, 1, d), lambda i, s: ((i * tm) // seq, 0, 0))]
    mix1, x3 = mm_nn_row(o.reshape(t, d), whole(wo), "o_proj", (F32, F32),
                         lambda acc, x_, g_: (acc, x_ + g_ * acc), (x2.reshape(t, d), g_a1), res_specs, tm_epi)
    mix1, x3 = mix1.reshape(bsz, seq, d), x3.reshape(bsz, seq, d)
    x4, mlp1_saved, w1_1, w2_1 = mlp_fwd(x3, par(mlp_norm_g[1]), sh_m1, sc_m1, g_m1, w1_1_handle, w2_1_handle, "mlp1")

    def loss_fn(y_, t_, f_, g_):
        diff = y_ - t_
        part = jnp.sum(0.5 * jnp.mean(diff * diff, axis=-1, keepdims=True), axis=0, keepdims=True)
        dy_ = diff * (1.0 / d)
        return (jnp.broadcast_to(part, (1, LANES)), dy_, *_gate_bwd(dy_, f_, g_))

    loss_part, dx4, dff1, dg_m1 = act_call(
        loss_fn, [x4, loss_target, mlp1_saved[2], g_m1],
        [(LANES, F32, "all"), (d, F32, "tile")] + _gate_bwd_outs(d), "loss")

    dx3, dw1_1, dw2_1, (dsh_m1, dsc_m1, dgn_mlp1), (dmix1, dg_a1) = mlp_bwd(
        dx4, dff1, x3, par(mlp_norm_g[1]), sc_m1, w1_1, w2_1, mlp1_saved, "mlp1", (mix1, g_a1))
    dmix1 = dmix1.reshape(t, d)
    do = mm_nt_row(dmix1, whole(wo), "o_dproj").reshape(bsz, seq, d)
    dwo = mm_tn(o.reshape(t, d), dmix1, "a", N_DEV, "o_dw")
    dq, dk, dv = attention_bwd(q_h, k_h, v_h, att_tot, do)
    dq_raw, dqg = act_call(_head_norm_bwd, [q_raw, qg, dq], [(d, BF16, "tile"), (d, F32, "all")], "q_dnorm")
    dq_raw = dq_raw.reshape(t, d)
    dh1 = mm_nt_row(dq_raw, whole(wq), "q_dproj").reshape(bsz, seq, d)
    dwq = mm_tn(h1.reshape(t, d), dq_raw, "a", N_DEV, "q_dw")
    dx2, dsh_a1, dsc_a1, dgn_mix1 = act_call(
        _norm_mod_bwd, [x2, par(mix_norm_g[1]), sc_a1, dh1, dx3],
        [(d, F32, "tile"), (d, F32, "seq"), (d, F32, "seq"), (d, F32, "all")], "mix1_dnorm")

    def kv_bwd_fn(k_, g_, dk_, dv_):
        dk_raw, dg_ = _head_norm_bwd(k_, g_, dk_)
        return jnp.concatenate([dk_raw, dv_], axis=1), dg_

    dkvf, dkg = act_call(kv_bwd_fn, [k_raw, kg, dk, dv], [(2 * d, BF16, "tile"), (d, F32, "all")], "k_dnorm")
    dkvf = dkvf.reshape(t, 2 * d)
    dhkv = mm_nt_col(dkvf, wkv, "kv_dproj").reshape(bsz, seq, d)
    dwkv = mm_tn(hkv.reshape(t, d), dkvf, "c", N_DEV, "kv_dw")
    (dwo, dwq, dwkv), att_token = exchange_start([(dwo, True), (dwq, True), (dwkv, True)], "att_dw_start")
    dx2, dkv_sh, dkv_sc, dgn_kv, dff0, dg_m0 = act_call(
        _norm_then_gate_bwd, [x2, par(kv_norm_g) + att_token[0, 0], kv_sc, dhkv, dx2, mlp0_saved[2], g_m],
        _norm_bwd_outs(d) + _gate_bwd_outs(d), "kv_dnorm")

    dx1, dw1_0, dw2_0, (dsh_m0, dsc_m0, dgn_mlp0), _ = mlp_bwd(
        dx2, dff0, x1, par(mlp_norm_g[0]), sc_m, w1_0, w2_0, mlp0_saved, "mlp0")

    def glu_bwd_fn(do_, z_, g_):
        val, sig = z_[:, :d], jax.nn.sigmoid(z_[:, d:])
        dmix = g_ * do_
        dz = jnp.concatenate([dmix * sig, dmix * val * sig * (1.0 - sig)], axis=1)
        return dz, _rowsum(do_ * (val * sig))

    dz, dg_a0 = act_call(glu_bwd_fn, [dx1, z, g_a], [(2 * d, BF16, "tile"), (d, F32, "seq")], "glu_dres")
    dz = dz.reshape(t, 2 * d)
    dy = mm_nt_col(dz, w_glu, "glu_dup", (F32,), lambda acc, y_: (acc * _gelu_grad(y_),),
                   (y.reshape(t, d),)).reshape(bsz, seq, d)
    dwglu = mm_tn(ge.reshape(t, d), dz, "c", N_DEV, "glu_dw")
    (dwglu,), glu_token = exchange_start([(dwglu, True)], "glu_dw_start")
    du, dbbd_re, dbbd_im, dcbd_re, dcbd_im, dab_re, dab_im, dd_skip = s5_bwd(
        dy, h0, st_re, st_im, bbd_re, bbd_im, cbd_re, cbd_im, abr, abi, d_skip + glu_token[0, 0])
    dx0, dsh_a0, dsc_a0, dgn_mix0 = act_call(
        _norm_mod_bwd, [x, par(mix_norm_g[0]), sc_a, du, dx1],
        [(d, F32, "tile"), (d, F32, "seq"), (d, F32, "seq"), (d, F32, "all")], "mix0_dnorm")

    from_bbd = lambda m: jnp.swapaxes(m.reshape(nb, S5_BLOCK_GROUPS, S5_GROUP, S5_STATE), 2, 3).reshape(gp, S5_GROUP)
    d_c = lambda m: jnp.swapaxes(m.reshape(nb, S5_BLOCK_GROUPS, S5_STATE, S5_GROUP), 2, 3).reshape(
        1, n_groups, S5_GROUP, S5_STATE)
    d_lam_re, d_lam_im, d_log_dt, d_b_re, d_b_im = s5_prep_bwd(
        lam_re, lam_im, log_dt, b_re, b_im, dab_re.reshape(gp, 1), dab_im.reshape(gp, 1),
        from_bbd(dbbd_re), from_bbd(dbbd_im))

    small = all_reduce_small([
        jnp.stack([dgn_mix0.reshape(d), dgn_mix1.reshape(d)]),
        jnp.stack([dgn_mlp0.reshape(d), dgn_mlp1.reshape(d)]),
        d_lam_re.reshape(1, n_groups, S5_STATE), d_lam_im.reshape(1, n_groups, S5_STATE),
        d_log_dt.reshape(1, n_groups, S5_STATE).sum(axis=-1),
        d_b_re.reshape(s5_b_re.shape), d_b_im.reshape(s5_b_im.shape),
        d_c(dcbd_re), d_c(dcbd_im),
        dd_skip.reshape(1, d),
        dgn_kv.reshape(d),
        dkg.reshape(d // HEAD_DIM, HEAD_DIM).sum(axis=0),
        dqg.reshape(d // HEAD_DIM, HEAD_DIM).sum(axis=0)[None, :],
        loss_part[0, 0, :1],
    ], "small_grads")
    (g_mix_norm, g_mlp_norm, g_a_re, g_a_im, g_log_dt, g_b_re, g_b_im, g_c_re, g_c_im,
     g_skip_full, g_kv_norm, g_k_norm, g_q_norm, loss_all) = small
    loss = loss_all[0]
    g_s5_d = lax.dynamic_slice_in_dim(g_skip_full, dev * (d // N_DEV), d // N_DEV, axis=1)

    dm_mine = jnp.concatenate([
        dsh_a0, dsc_a0, dg_a0, dsh_m0, dsc_m0, dg_m0,
        dsh_a1, dsc_a1, dg_a1, dsh_m1, dsc_m1, dg_m1, dkv_sh, dkv_sc], axis=2).reshape(bsz, 2 * e_ada + e_kv)
    dm_all = all_gather(dm_mine, "gather_dmod").reshape(N_DEV * bsz, 2 * e_ada + e_kv)
    dm_cols = jnp.concatenate([
        lax.dynamic_slice_in_dim(dm_all, dev * n_ada, n_ada, axis=1),
        lax.dynamic_slice_in_dim(dm_all, e_ada + dev * n_ada, n_ada, axis=1),
        lax.dynamic_slice_in_dim(dm_all, 2 * e_ada + dev * n_kv, n_kv, axis=1)], axis=1)
    dw_cols, db_all = ada_bwd(c_all, dm_cols, dm_all)
    g_ada_w = jnp.stack([dw_cols[:, :n_ada], dw_cols[:, n_ada:2 * n_ada]])
    g_kv_ada_w = dw_cols[:, 2 * n_ada:]
    g_ada_b = db_all[0, :2 * e_ada].reshape(2, e_ada)
    g_kv_ada_b = db_all[0, 2 * e_ada:]

    landed = lambda handle, name: slab_sum(exchange_wait(handle, dx0, name + "_wait"), name + "_sum")
    g_w1 = jnp.stack([landed(dw1_0, "rs_w1_0"), landed(dw1_1, "rs_w1_1")])
    g_w2 = jnp.stack([landed(dw2_0, "rs_w2_0"), landed(dw2_1, "rs_w2_1")])
    g_glu = landed(dwglu, "rs_glu")[None]
    g_wkv = landed(dwkv, "rs_wkv")
    g_wq = landed(dwq, "rs_wq")[None]
    g_wo = landed(dwo, "rs_wo")[None]

    weights = [ada_w, ada_b, mix_norm_g, mlp_norm_g, mlp_w1, mlp_w2, s5_a_re, s5_a_im, s5_log_dt, s5_b_re,
               s5_b_im, s5_c_re, s5_c_im, s5_d, s5_w_glu, kv_ada_w, kv_ada_b, kv_norm_g, w_kv, k_norm_g,
               sb_w_q, q_norm_g, sb_w_o]
    grads = [g_ada_w, g_ada_b, g_mix_norm, g_mlp_norm, g_w1, g_w2, g_a_re, g_a_im, g_log_dt, g_b_re,
             g_b_im, g_c_re, g_c_im, g_s5_d, g_glu, g_kv_ada_w, g_kv_ada_b, g_kv_norm, g_wkv, g_k_norm,
             g_wq, g_q_norm, g_wo]
    ms = [m_ada_w, m_ada_b, m_mix_norm_g, m_mlp_norm_g, m_mlp_w1, m_mlp_w2, m_s5_a_re, m_s5_a_im, m_s5_log_dt,
          m_s5_b_re, m_s5_b_im, m_s5_c_re, m_s5_c_im, m_s5_d, m_s5_w_glu, m_kv_ada_w, m_kv_ada_b, m_kv_norm_g,
          m_w_kv, m_k_norm_g, m_sb_w_q, m_q_norm_g, m_sb_w_o]
    vs = [v_ada_w, v_ada_b, v_mix_norm_g, v_mlp_norm_g, v_mlp_w1, v_mlp_w2, v_s5_a_re, v_s5_a_im, v_s5_log_dt,
          v_s5_b_re, v_s5_b_im, v_s5_c_re, v_s5_c_im, v_s5_d, v_s5_w_glu, v_kv_ada_w, v_kv_ada_b, v_kv_norm_g,
          v_w_kv, v_k_norm_g, v_sb_w_q, v_q_norm_g, v_sb_w_o]
    grads = [g.reshape(w.shape) for g, w in zip(grads, weights)]
    deltas, new_ms, new_vs = [], [], []
    for i, (w, g, m, v) in enumerate(zip(weights, grads, ms, vs)):
        dl, nm, nv = adamw(w, g, m, v, f"adamw_{i}")
        deltas.append(dl)
        new_ms.append(nm)
        new_vs.append(nv)
    return (loss, dx0, *grads, *deltas, *new_ms, *new_vs)
```

```python
import functools
import math

import jax
import jax.numpy as jnp
from jax import lax
from jax.experimental import pallas as pl
from jax.experimental.pallas import tpu as pltpu

F32 = jnp.float32
BF16 = jnp.bfloat16

N_DEV = 8
N_CHIPS = 4
MESH = pl.DeviceIdType.MESH
ANY = pl.BlockSpec(memory_space=pl.ANY)

LANES = 128
VMEM_LIMIT_BYTES = 48 * 2 ** 20
TILE_BUDGET_BYTES = 4 * 2 ** 20

S5_GROUP = 16
S5_STATE = 64
S5_BLOCK_GROUPS = 16
HEAD_DIM = 64
ATT_BLOCK = 128
EPS = 1e-6

ADAM_LR = 0.001
ADAM_B1 = 0.9
ADAM_B2 = 0.999
ADAM_EPS = 1e-08
ADAM_WD = 0.01
ADAM_STEP = 10


def _cparams(*sem):
    return pltpu.CompilerParams(dimension_semantics=sem, vmem_limit_bytes=VMEM_LIMIT_BYTES)


def _divisor_tile(n, limit, mult):
    best = None
    for t in range(mult, min(n, limit) + 1, mult):
        if n % t == 0:
            best = t
    return best if best is not None else n


def _tile_m(m):
    return _divisor_tile(m, 2048 if m >= 4096 else 256, 16)


def all_gather(x, name):
    def body(x_ref, out_ref, send_sems, recv_sems, local_sem):
        ax, ay, ac = lax.axis_index("x"), lax.axis_index("y"), lax.axis_index("c")
        me, sibling = (ax, ay, ac), (ax, ay, 1 - ac)
        chips = [(1 - ax, ay), (ax, 1 - ay), (1 - ax, 1 - ay)]

        def slot(px, py, pc):
            return out_ref.at[4 * px + 2 * py + pc]

        def copy(k, block, to, src=None):
            return pltpu.make_async_remote_copy(
                src_ref=slot(*block) if src is None else src, dst_ref=slot(*block),
                send_sem=send_sems.at[k], recv_sem=recv_sems.at[k], device_id=to, device_id_type=MESH)

        mine = pltpu.make_async_copy(x_ref, slot(*me), local_sem)
        mine.start()
        first = [copy(0, me, sibling, src=x_ref)]
        first += [copy(1 + j, me, (*chip, ac), src=x_ref) for j, chip in enumerate(chips)]
        for cp in first:
            cp.start()
        passed = [copy(4 + j, (*chip, ac), sibling) for j, chip in enumerate(chips)]
        for j, chip in enumerate(chips):
            copy(1 + j, (*chip, ac), me).wait_recv()
            passed[j].start()
        copy(0, sibling, me).wait_recv()
        for j, chip in enumerate(chips):
            copy(4 + j, (*chip, 1 - ac), me).wait_recv()
        for cp in first + passed:
            cp.wait_send()
        mine.wait()

    return pl.pallas_call(
        body, name=name,
        out_shape=jax.ShapeDtypeStruct((N_DEV,) + x.shape, x.dtype),
        in_specs=[ANY], out_specs=ANY,
        scratch_shapes=[pltpu.SemaphoreType.DMA((7,)), pltpu.SemaphoreType.DMA((7,)), pltpu.SemaphoreType.DMA],
    )(x)


HBM = pl.BlockSpec(memory_space=pltpu.HBM)
SEM = pl.BlockSpec(memory_space=pltpu.SEMAPHORE)
N_PEERS = N_DEV - 1


def _peers():
    ax, ay, ac = lax.axis_index("x"), lax.axis_index("y"), lax.axis_index("c")
    flip = lambda v, bit: 1 - v if bit else v
    return [(flip(ax, k & 4), flip(ay, k & 2), flip(ac, k & 1)) for k in range(1, N_DEV)]


def _dev_index(pos):
    return 4 * pos[0] + 2 * pos[1] + pos[2]


def exchange_start(items, name, after=None):
    n = len(items)
    srcs = [a for a, _ in items]
    blocks = [a.shape[1:] if scatter else a.shape for a, scatter in items]
    extra = list(after or ())

    def body(*refs):
        src_refs, land_refs = refs[:n], refs[n:2 * n]
        outs = refs[2 * n + len(extra):]
        send_sems, recv_sems = outs[:n], outs[n:2 * n]
        token = outs[-1]
        me = _dev_index((lax.axis_index("x"), lax.axis_index("y"), lax.axis_index("c")))
        for w, (_, scatter) in enumerate(items):
            for k, peer in enumerate(_peers()):
                src = src_refs[w].at[_dev_index(peer)] if scatter else src_refs[w]
                pltpu.make_async_remote_copy(
                    src_ref=src, dst_ref=land_refs[w].at[me], send_sem=send_sems[w].at[k],
                    recv_sem=recv_sems[w].at[k], device_id=peer, device_id_type=MESH).start()
        token[...] = jnp.zeros_like(token)

    lands = [lax.empty((N_DEV,) + blk, a.dtype) for a, blk in zip(srcs, blocks)]
    res = pl.pallas_call(
        body, name=name,
        out_shape=([pltpu.SemaphoreType.DMA((N_PEERS,))] * (2 * n)
                   + [pltpu.HBM(a.shape, a.dtype) for a in srcs] + [pltpu.HBM(l.shape, l.dtype) for l in lands]
                   + [jax.ShapeDtypeStruct((8, LANES), F32)]),
        in_specs=[HBM] * (2 * n) + [ANY] * len(extra),
        out_specs=[SEM] * (2 * n) + [HBM] * (2 * n) + [pl.BlockSpec(memory_space=pltpu.VMEM)],
        input_output_aliases={i: 2 * n + i for i in range(2 * n)},
        compiler_params=pltpu.CompilerParams(has_side_effects=pltpu.SideEffectType.DATAFLOW_SIDE_EFFECTING),
    )(*[pltpu.with_memory_space_constraint(a, pltpu.HBM) for a in srcs + lands], *extra)
    handles = [(res[w], res[n + w], res[2 * n + w], res[3 * n + w], scatter) for w, (_, scatter) in enumerate(items)]
    return handles, res[-1]


def exchange_wait(handle, after, name):
    send_sem, recv_sem, src, land, scatter = handle

    def body(src_ref, land_ref, send_ref, recv_ref, after_ref, src_out, land_out):
        for k, peer in enumerate(_peers()):
            slot = _dev_index(peer)
            copy = pltpu.make_async_remote_copy(
                src_ref=src_ref.at[slot] if scatter else src_ref, dst_ref=land_ref.at[slot],
                send_sem=send_ref.at[k], recv_sem=recv_ref.at[k], device_id=peer, device_id_type=MESH)
            copy.wait_send()
            copy.wait_recv()

    src, landed = pl.pallas_call(
        body, name=name,
        out_shape=(pltpu.HBM(src.shape, src.dtype), pltpu.HBM(land.shape, land.dtype)),
        in_specs=[HBM, HBM, SEM, SEM, ANY], out_specs=(HBM, HBM), input_output_aliases={0: 0, 1: 1},
        compiler_params=pltpu.CompilerParams(has_side_effects=pltpu.SideEffectType.DATAFLOW_SIDE_EFFECTING),
    )(src, land, send_sem, recv_sem, after)
    dev = _dev_index((lax.axis_index("x"), lax.axis_index("y"), lax.axis_index("c")))
    own = lax.dynamic_index_in_dim(src, dev, axis=0, keepdims=True) if scatter else src[None]
    return lax.dynamic_update_slice_in_dim(landed, own, dev, axis=0)


def rows_call(fn, ins, outs, name):
    rows = ins[0].shape[1]
    per_row = sum(a.shape[0] * a.shape[2] * a.dtype.itemsize for a in ins)
    per_row += sum(l * c * jnp.dtype(dt).itemsize for l, c, dt in outs)
    tr = _divisor_tile(rows, max(16, TILE_BUDGET_BYTES // per_row), 16)
    n_in = len(ins)

    def body(*refs):
        vals = fn(*[r[...] for r in refs[:n_in]])
        if not isinstance(vals, (tuple, list)):
            vals = (vals,)
        for r, v in zip(refs[n_in:], vals):
            r[...] = v.astype(r.dtype)

    def spec(l, c):
        return pl.BlockSpec((l, tr, c), lambda i: (0, i, 0))

    res = pl.pallas_call(
        body, name=name, grid=(rows // tr,),
        in_specs=[spec(a.shape[0], a.shape[2]) for a in ins],
        out_specs=[spec(l, c) for l, c, _ in outs],
        out_shape=[jax.ShapeDtypeStruct((l, rows, c), dt) for l, c, dt in outs],
        compiler_params=_cparams("arbitrary"),
    )(*ins)
    return res


def _as_rows(a, lead=0):
    shape = a.shape
    l = int(math.prod(shape[:lead])) if lead else 1
    rest = shape[lead:]
    c = rest[-1] if rest else 1
    r = int(math.prod(rest[:-1])) if len(rest) > 1 else 1
    return a.reshape(l, r, c)


def act_call(fn, ins, outs, name):
    bsz, seq = ins[0].shape[0], ins[0].shape[1]
    per_row = sum(a.shape[2] * a.dtype.itemsize for a in ins if a.shape[1] == seq)
    per_row += sum(c * jnp.dtype(dt).itemsize for c, dt, kind in outs if kind == "tile")
    ts = _divisor_tile(seq, max(16, TILE_BUDGET_BYTES // per_row), 16)
    n_in = len(ins)

    def in_spec(a):
        c = a.shape[2]
        if a.shape[1] == seq:
            return pl.BlockSpec((None, ts, c), lambda b, s: (b, s, 0))
        if a.shape[0] == bsz:
            return pl.BlockSpec((None, 1, c), lambda b, s: (b, 0, 0))
        return pl.BlockSpec((None, 1, c), lambda b, s: (0, 0, 0))

    def out_spec(c, kind):
        if kind == "tile":
            return pl.BlockSpec((None, ts, c), lambda b, s: (b, s, 0))
        if kind == "seq":
            return pl.BlockSpec((None, 1, c), lambda b, s: (b, 0, 0))
        return pl.BlockSpec((None, 1, c), lambda b, s: (0, 0, 0))

    def out_shape(c, dt, kind):
        if kind == "tile":
            return jax.ShapeDtypeStruct((bsz, seq, c), dt)
        return jax.ShapeDtypeStruct((bsz if kind == "seq" else 1, 1, c), dt)

    def accumulate(ref, v, first):
        @pl.when(first)
        def _():
            ref[...] = jnp.zeros_like(ref)

        ref[...] += v.astype(ref.dtype)

    def body(*refs):
        b, s = pl.program_id(0), pl.program_id(1)
        vals = fn(*[r[...] for r in refs[:n_in]])
        if not isinstance(vals, (tuple, list)):
            vals = (vals,)
        for ref, v, (_, _, kind) in zip(refs[n_in:], vals, outs):
            if kind == "tile":
                ref[...] = v.astype(ref.dtype)
            elif kind == "seq":
                accumulate(ref, v, s == 0)
            else:
                accumulate(ref, v, jnp.logical_and(b == 0, s == 0))

    return pl.pallas_call(
        body, name=name, grid=(bsz, seq // ts),
        in_specs=[in_spec(a) for a in ins],
        out_specs=[out_spec(c, kind) for c, _, kind in outs],
        out_shape=[out_shape(*o) for o in outs],
        compiler_params=_cparams("arbitrary", "arbitrary"),
    )(*ins)


def _mm(name, grid, a, a_spec, b, b_spec, dims, out_shape, out_spec, out_dtypes, acc_steps,
        epi=None, extras=(), extra_specs=()):
    n_ex, n_out = len(extras), len(out_dtypes)
    tile = tuple(d for d in out_spec.block_shape if d is not None)

    def body(*refs):
        a_ref, b_ref = refs[0], refs[1]
        ex_refs = refs[2:2 + n_ex]
        o_refs = refs[2 + n_ex:2 + n_ex + n_out]
        p = lax.dot_general(a_ref[...].astype(BF16), b_ref[...].astype(BF16), (dims, ((), ())),
                            preferred_element_type=F32)

        def finish(acc):
            vals = epi(acc, *[r[...] for r in ex_refs]) if epi is not None else (acc,) * n_out
            for r, v in zip(o_refs, vals):
                r[...] = v.astype(r.dtype)

        if not acc_steps:
            finish(p)
        else:
            acc_ref = refs[-1]
            s = pl.program_id(1)

            @pl.when(s == 0)
            def _():
                acc_ref[...] = p

            @pl.when(s > 0)
            def _():
                acc_ref[...] += p

            @pl.when(s == acc_steps - 1)
            def _():
                finish(acc_ref[...])

    res = pl.pallas_call(
        body, name=name, grid=grid,
        in_specs=[a_spec, b_spec] + list(extra_specs),
        out_specs=[out_spec] * n_out,
        out_shape=[jax.ShapeDtypeStruct(out_shape, dt) for dt in out_dtypes],
        scratch_shapes=[pltpu.VMEM(tile, F32)] if acc_steps else [],
        compiler_params=_cparams("arbitrary", "arbitrary"),
    )(a, b, *extras)
    return res if n_out > 1 else res[0]


def mm_nn_col(a, w, name, out_dtypes=(F32,), epi=None, slabs=None, col_params=()):
    m, k = a.shape
    _, _, nb = w.shape
    first, count = slabs if slabs is not None else (0, w.shape[0])
    tm = _tile_m(m)
    return _mm(name, (m // tm, count), a, pl.BlockSpec((tm, k), lambda i, j: (i, 0)),
               w, pl.BlockSpec((None, k, nb), lambda i, j: (first + j, 0, 0)), ((1,), (0,)),
               (m, count * nb), pl.BlockSpec((tm, nb), lambda i, j: (i, j)), out_dtypes, 0, epi,
               col_params, [pl.BlockSpec((1, nb), lambda i, j: (0, j))] * len(col_params))


def mm_nn_row(a, w, name, out_dtypes=(F32,), epi=None, extras=(), extra_specs=None, tm=None):
    m = a.shape[0]
    ns, kb, n = w.shape
    tm = tm or _tile_m(m)
    return _mm(name, (m // tm, ns), a, pl.BlockSpec((tm, kb), lambda i, s: (i, s)),
               w, pl.BlockSpec((None, kb, n), lambda i, s: (s, 0, 0)), ((1,), (0,)),
               (m, n), pl.BlockSpec((tm, n), lambda i, s: (i, 0)), out_dtypes, ns, epi,
               extras, extra_specs(tm) if extras else ())


def mm_nt_col(dc, w, name, out_dtypes=(F32,), epi=None, extras=()):
    m = dc.shape[0]
    ns, k, nb = w.shape
    tm = _tile_m(m) // 2 if extras else _tile_m(m)
    spec = pl.BlockSpec((tm, k), lambda i, s: (i, 0))
    return _mm(name, (m // tm, ns), dc, pl.BlockSpec((tm, nb), lambda i, s: (i, s)),
               w, pl.BlockSpec((None, k, nb), lambda i, s: (s, 0, 0)), ((1,), (1,)),
               (m, k), spec, out_dtypes, ns, epi, extras, [spec] * len(extras))


def mm_nt_row(dc, w, name, out_dtypes=(F32,), epi=None, extras=()):
    m, n = dc.shape
    ns, kb, _ = w.shape
    tm = _tile_m(m)
    spec = pl.BlockSpec((tm, kb), lambda i, s: (i, s))
    return _mm(name, (m // tm, ns), dc, pl.BlockSpec((tm, n), lambda i, s: (i, 0)),
               w, pl.BlockSpec((None, kb, n), lambda i, s: (s, 0, 0)), ((1,), (1,)),
               (m, ns * kb), spec, out_dtypes, 0, epi, extras, [spec] * len(extras))


def mm_tn(a, c, slab, ns, name, out_dtype=BF16):
    m, ka_all = a.shape
    nc_all = c.shape[1]
    ka = ka_all // ns if slab == "a" else ka_all
    nc = nc_all // ns if slab == "c" else nc_all
    tt = _divisor_tile(m, 256, 16)
    steps = m // tt

    def body(a_ref, c_ref, o_ref, acc_ref):
        t = pl.program_id(0)

        @pl.when(t == 0)
        def _():
            acc_ref[...] = jnp.zeros_like(acc_ref)

        for s in range(ns):
            a_s = a_ref[:, s * ka:(s + 1) * ka] if slab == "a" else a_ref[...]
            c_s = c_ref[:, s * nc:(s + 1) * nc] if slab == "c" else c_ref[...]
            acc_ref[s] += lax.dot_general(a_s.astype(BF16), c_s.astype(BF16), (((0,), (0,)), ((), ())),
                                          preferred_element_type=F32)

        @pl.when(t == steps - 1)
        def _():
            o_ref[...] = acc_ref[...].astype(o_ref.dtype)

    return pl.pallas_call(
        body, name=name, grid=(steps,),
        in_specs=[pl.BlockSpec((tt, ka_all), lambda t: (t, 0)), pl.BlockSpec((tt, nc_all), lambda t: (t, 0))],
        out_specs=pl.BlockSpec((ns, ka, nc), lambda t: (0, 0, 0)),
        out_shape=jax.ShapeDtypeStruct((ns, ka, nc), out_dtype),
        scratch_shapes=[pltpu.VMEM((ns, ka, nc), F32)],
        compiler_params=_cparams("arbitrary"),
    )(a, c)


def slab_sum(landed, name):
    shape = landed.shape[1:]
    total = rows_call(lambda g: jnp.sum(g.astype(F32), axis=0, keepdims=True),
                      [_as_rows(landed, 1)], [(1, shape[-1], F32)], name)[0]
    return total.reshape(shape)


def all_reduce_small(leaves, name):
    sizes = [int(a.size) for a in leaves]
    flat = jnp.concatenate([a.reshape(-1) for a in leaves])
    total = int(flat.size)
    chunk = N_DEV * 16 * LANES
    padded = -(-total // chunk) * chunk
    parts = jnp.pad(flat, (0, padded - total)).reshape(N_DEV, padded // (N_DEV * LANES), LANES)
    (scatter,), _ = exchange_start([(parts, True)], name + "_scatter")
    landed = exchange_wait(scatter, parts, name + "_scatter_wait")
    mine = rows_call(lambda g: jnp.sum(g, axis=0, keepdims=True), [landed], [(1, LANES, F32)], name + "_sum")[0][0]
    (gather,), _ = exchange_start([(mine, False)], name + "_gather")
    summed = exchange_wait(gather, mine, name + "_gather_wait").reshape(-1)
    out, at = [], 0
    for a, n in zip(leaves, sizes):
        out.append(summed[at:at + n].reshape(a.shape))
        at += n
    return out


def adamw(w, g, m, v, name):
    c = w.shape[-1] if w.ndim else 1

    def fn(w_, g_, m_, v_):
        nm = ADAM_B1 * m_ + (1.0 - ADAM_B1) * g_
        nv = ADAM_B2 * v_ + (1.0 - ADAM_B2) * (g_ * g_)
        m_hat = nm / (1.0 - ADAM_B1 ** ADAM_STEP)
        v_hat = nv / (1.0 - ADAM_B2 ** ADAM_STEP)
        delta = -ADAM_LR * (m_hat / (jnp.sqrt(v_hat) + ADAM_EPS) + ADAM_WD * w_)
        return delta, nm, nv

    res = rows_call(fn, [_as_rows(t) for t in (w, g.astype(F32), m, v)], [(1, c, F32)] * 3, name)
    return tuple(r.reshape(w.shape) for r in res)


def _rowsum(v):
    return jnp.sum(v, axis=0, keepdims=True)


def _norm_mod(x, g, sh, sc):
    n = x * lax.rsqrt(jnp.mean(x * x, axis=-1, keepdims=True) + EPS)
    return (n * g) * (1.0 + sc) + sh


def _norm_mod_bwd(x, g, sc, dh, dres):
    r = lax.rsqrt(jnp.mean(x * x, axis=-1, keepdims=True) + EPS)
    n = x * r
    dy = dh * (1.0 + sc)
    dn = dy * g
    dx = r * (dn - n * jnp.mean(dn * n, axis=-1, keepdims=True))
    return dres + dx, _rowsum(dh), _rowsum(dh * (n * g)), _rowsum(dy * n)


def _head_mean(v):
    low = lax.broadcasted_iota(jnp.int32, (1, LANES), 1) < HEAD_DIM
    parts = []
    for p in range(v.shape[1] // LANES):
        blk = v[:, p * LANES:(p + 1) * LANES]
        s0 = jnp.sum(jnp.where(low, blk, 0.0), axis=-1, keepdims=True)
        s1 = jnp.sum(jnp.where(low, 0.0, blk), axis=-1, keepdims=True)
        parts.append(jnp.where(low, s0, s1))
    return jnp.concatenate(parts, axis=1) * (1.0 / HEAD_DIM)


def _head_norm(x, g):
    return x * lax.rsqrt(_head_mean(x * x) + EPS) * g


def _head_norm_bwd(x, g, dy):
    r = lax.rsqrt(_head_mean(x * x) + EPS)
    n = x * r
    dn = dy * g
    return r * (dn - n * _head_mean(dn * n)), _rowsum(dy * n)


GELU_C = math.sqrt(2.0 / math.pi)
GELU_A = 0.044715


def _gelu_grad(y):
    t = jnp.tanh(GELU_C * (y + GELU_A * y * y * y))
    return 0.5 * (1.0 + t) + 0.5 * y * (1.0 - t * t) * GELU_C * (1.0 + 3.0 * GELU_A * y * y)


def ada_fwd(c_all, w_cols, b_cols):
    def body(c_ref, w_ref, b_ref, o_ref):
        c = c_ref[...]
        s = (c * jax.nn.sigmoid(c)).astype(BF16)
        o_ref[...] = jnp.dot(s, w_ref[...].astype(BF16), preferred_element_type=F32) + b_ref[...]

    return pl.pallas_call(
        body, name="ada_fwd", out_shape=jax.ShapeDtypeStruct((c_all.shape[0], w_cols.shape[1]), F32),
        compiler_params=pltpu.CompilerParams(vmem_limit_bytes=VMEM_LIMIT_BYTES),
    )(c_all, w_cols, b_cols)


def ada_bwd(c_all, dm_cols, dm_all):
    def body(c_ref, d_ref, all_ref, dw_ref, db_ref):
        c = c_ref[...]
        s = (c * jax.nn.sigmoid(c)).astype(BF16)
        dw_ref[...] = lax.dot_general(s, d_ref[...].astype(BF16), (((0,), (0,)), ((), ())),
                                      preferred_element_type=F32)
        db_ref[...] = jnp.sum(all_ref[...], axis=0, keepdims=True)

    return pl.pallas_call(
        body, name="ada_bwd",
        out_shape=[jax.ShapeDtypeStruct((c_all.shape[1], dm_cols.shape[1]), F32),
                   jax.ShapeDtypeStruct((1, dm_all.shape[1]), F32)],
        compiler_params=pltpu.CompilerParams(vmem_limit_bytes=VMEM_LIMIT_BYTES),
    )(c_all, dm_cols, dm_all)


def _s5_discretise(lam_re, lam_im, log_dt, b_re, b_im):
    dt = jnp.exp(log_dt)
    mag = jnp.exp(lam_re * dt)
    ab_re = mag * jnp.cos(lam_im * dt)
    ab_im = mag * jnp.sin(lam_im * dt)
    den = lam_re * lam_re + lam_im * lam_im
    nr = ab_re - 1.0
    ni = ab_im
    f_re = (nr * lam_re + ni * lam_im) / den
    f_im = (ni * lam_re - nr * lam_im) / den
    bb_re = f_re * b_re - f_im * b_im
    bb_im = f_re * b_im + f_im * b_re
    return ab_re, ab_im, bb_re, bb_im


def s5_prep(lam_re, lam_im, log_dt, b_re, b_im):
    gp, h = b_re.shape

    def body(lr, li, ld, br, bi, o_ar, o_ai, o_br, o_bi):
        res = _s5_discretise(lr[...], li[...], ld[...], br[...], bi[...])
        for r, v in zip((o_ar, o_ai, o_br, o_bi), res):
            r[...] = v

    col, mat = jax.ShapeDtypeStruct((gp, 1), F32), jax.ShapeDtypeStruct((gp, h), F32)
    return pl.pallas_call(body, name="s5_prep", out_shape=[col, col, mat, mat])(lam_re, lam_im, log_dt, b_re, b_im)


def s5_prep_bwd(lam_re, lam_im, log_dt, b_re, b_im, d_ab_re, d_ab_im, d_bb_re, d_bb_im):
    gp, h = b_re.shape

    def body(lr, li, ld, br, bi, g_ar, g_ai, g_br, g_bi, o_lr, o_li, o_ld, o_br, o_bi):
        _, vjp = jax.vjp(_s5_discretise, lr[...], li[...], ld[...], br[...], bi[...])
        res = vjp((g_ar[...], g_ai[...], g_br[...], g_bi[...]))
        for r, v in zip((o_lr, o_li, o_ld, o_br, o_bi), res):
            r[...] = v

    col, mat = jax.ShapeDtypeStruct((gp, 1), F32), jax.ShapeDtypeStruct((gp, h), F32)
    return pl.pallas_call(body, name="s5_prep_bwd", out_shape=[col, col, col, mat, mat])(
        lam_re, lam_im, log_dt, b_re, b_im, d_ab_re, d_ab_im, d_bb_re, d_bb_im)


def _s5_chunk(seq):
    return _divisor_tile(seq, 256, 16)


def s5_fwd(u, bbd_re, bbd_im, cbd_re, cbd_im, ab_re, ab_im, dskip):
    bsz, seq, d = u.shape
    nb, cb, ns = bbd_re.shape
    lc = _s5_chunk(seq)

    def body(u_ref, bre_ref, bim_ref, cre_ref, cim_ref, ar_ref, ai_ref, d_ref, y_ref, ge_ref, sre_ref, sim_ref,
             carry_re, carry_im):
        t = pl.program_id(1)

        @pl.when(t == 0)
        def _():
            carry_re[...] = jnp.zeros_like(carry_re)
            carry_im[...] = jnp.zeros_like(carry_im)

        for b in range(bsz):
            ub = u_ref[b].astype(BF16)
            sre_ref[b] = jnp.dot(ub, bre_ref[...].astype(BF16), preferred_element_type=F32)
            sim_ref[b] = jnp.dot(ub, bim_ref[...].astype(BF16), preferred_element_type=F32)
        ar, ai = ar_ref[...], ai_ref[...]

        def step(i, carry):
            row = pl.ds(i, 1)
            out = []
            for b, (cr, ci) in enumerate(carry):
                nr = ar * cr - ai * ci + sre_ref[b, row, :]
                ni = ar * ci + ai * cr + sim_ref[b, row, :]
                sre_ref[b, row, :] = nr
                sim_ref[b, row, :] = ni
                out.append((nr, ni))
            return tuple(out)

        init = tuple((carry_re[b], carry_im[b]) for b in range(bsz))
        last = lax.fori_loop(0, lc, step, init, unroll=8)
        for b, (cr, ci) in enumerate(last):
            carry_re[b] = cr
            carry_im[b] = ci
            y = jnp.dot(sre_ref[b].astype(BF16), cre_ref[...].astype(BF16), preferred_element_type=F32)
            y -= jnp.dot(sim_ref[b].astype(BF16), cim_ref[...].astype(BF16), preferred_element_type=F32)
            y = y + d_ref[...] * u_ref[b]
            y_ref[b] = y
            ge_ref[b] = jax.nn.gelu(y).astype(BF16)

    chan = pl.BlockSpec((bsz, lc, cb), lambda n, t: (0, t, n))
    state = pl.BlockSpec((bsz, lc, ns), lambda n, t: (0, t, n))
    par = lambda r, c: pl.BlockSpec((None, r, c), lambda n, t: (n, 0, 0))
    return pl.pallas_call(
        body, name="s5_fwd", grid=(nb, seq // lc),
        in_specs=[chan, par(cb, ns), par(cb, ns), par(ns, cb), par(ns, cb), par(1, ns), par(1, ns),
                  pl.BlockSpec((None, 1, cb), lambda n, t: (0, 0, n))],
        out_specs=[chan, chan, state, state],
        out_shape=[jax.ShapeDtypeStruct((bsz, seq, d), F32), jax.ShapeDtypeStruct((bsz, seq, d), BF16),
                   jax.ShapeDtypeStruct((bsz, seq, nb * ns), F32),
                   jax.ShapeDtypeStruct((bsz, seq, nb * ns), F32)],
        scratch_shapes=[pltpu.VMEM((bsz, 1, ns), F32), pltpu.VMEM((bsz, 1, ns), F32)],
        compiler_params=_cparams("arbitrary", "arbitrary"),
    )(u, bbd_re, bbd_im, cbd_re, cbd_im, ab_re, ab_im, dskip)


def s5_bwd(dy, u, st_re, st_im, bbd_re, bbd_im, cbd_re, cbd_im, ab_re, ab_im, dskip):
    bsz, seq, d = u.shape
    nb, cb, ns = bbd_re.shape
    lc = _s5_chunk(seq)
    nc = seq // lc

    def body(dy_ref, u_ref, sre_ref, sim_ref, bre_ref, bim_ref, cre_ref, cim_ref, ar_ref, ai_ref, d_ref,
             du_ref, dbre_out, dbim_out, dcre_out, dcim_out, dar_ref, dai_ref, dd_ref,
             g_re, g_im, gs_re, gs_im, carry_re, carry_im, dbre_ref, dbim_ref, dcre_ref, dcim_ref):
        t = pl.program_id(1)

        @pl.when(t == 0)
        def _():
            for r in (dbre_ref, dbim_ref, dcre_ref, dcim_ref, dar_ref, dai_ref, dd_ref, carry_re, carry_im):
                r[...] = jnp.zeros_like(r)

        nt = (((1,), (1,)), ((), ()))
        tn = (((0,), (0,)), ((), ()))
        for b in range(bsz):
            dyb = dy_ref[b].astype(BF16)
            g_re[b] = lax.dot_general(dyb, cre_ref[...].astype(BF16), nt, preferred_element_type=F32)
            g_im[b] = -lax.dot_general(dyb, cim_ref[...].astype(BF16), nt, preferred_element_type=F32)
        ar, ai = ar_ref[...], ai_ref[...]

        def step(k, carry):
            row = pl.ds(lc - 1 - k, 1)
            out = []
            for b, (cr, ci) in enumerate(carry):
                gs_re[b, row, :] = cr
                gs_im[b, row, :] = ci
                nr = ar * cr + ai * ci + g_re[b, row, :]
                ni = ar * ci - ai * cr + g_im[b, row, :]
                g_re[b, row, :] = nr
                g_im[b, row, :] = ni
                out.append((nr, ni))
            return tuple(out)

        init = tuple((carry_re[b], carry_im[b]) for b in range(bsz))
        last = lax.fori_loop(0, lc, step, init, unroll=8)
        for b, (cr, ci) in enumerate(last):
            carry_re[b] = cr
            carry_im[b] = ci
            dyf, uf = dy_ref[b], u_ref[b]
            dyb, ub = dyf.astype(BF16), uf.astype(BF16)
            sr, si = sre_ref[b], sim_ref[b]
            hr, hi = gs_re[b], gs_im[b]
            dar_ref[...] += _rowsum(hr * sr + hi * si)
            dai_ref[...] += _rowsum(hi * sr - hr * si)
            gr, gi = g_re[b].astype(BF16), g_im[b].astype(BF16)
            du = lax.dot_general(gr, bre_ref[...].astype(BF16), nt, preferred_element_type=F32)
            du += lax.dot_general(gi, bim_ref[...].astype(BF16), nt, preferred_element_type=F32)
            du_ref[b] = du + d_ref[...] * dyf
            dbre_ref[...] += lax.dot_general(ub, gr, tn, preferred_element_type=F32)
            dbim_ref[...] += lax.dot_general(ub, gi, tn, preferred_element_type=F32)
            dcre_ref[...] += lax.dot_general(sr.astype(BF16), dyb, tn, preferred_element_type=F32)
            dcim_ref[...] -= lax.dot_general(si.astype(BF16), dyb, tn, preferred_element_type=F32)
            dd_ref[...] += _rowsum(dyf * uf)

        @pl.when(t == nc - 1)
        def _():
            for k in range(cb // S5_GROUP):
                chans = slice(k * S5_GROUP, (k + 1) * S5_GROUP)
                states = slice(k * S5_STATE, (k + 1) * S5_STATE)
                dbre_out[chans, :] = dbre_ref[chans, states]
                dbim_out[chans, :] = dbim_ref[chans, states]
                dcre_out[states, :] = dcre_ref[states, chans]
                dcim_out[states, :] = dcim_ref[states, chans]

    chan = pl.BlockSpec((bsz, lc, cb), lambda n, t: (0, nc - 1 - t, n))
    state = pl.BlockSpec((bsz, lc, ns), lambda n, t: (0, nc - 1 - t, n))
    par = lambda r, c: pl.BlockSpec((None, r, c), lambda n, t: (n, 0, 0))
    return pl.pallas_call(
        body, name="s5_bwd", grid=(nb, nc),
        in_specs=[chan, chan, state, state, par(cb, ns), par(cb, ns), par(ns, cb), par(ns, cb),
                  par(1, ns), par(1, ns), pl.BlockSpec((None, 1, cb), lambda n, t: (0, 0, n))],
        out_specs=[chan, par(cb, S5_STATE), par(cb, S5_STATE), par(ns, S5_GROUP), par(ns, S5_GROUP),
                   par(1, ns), par(1, ns), par(1, cb)],
        out_shape=[jax.ShapeDtypeStruct((bsz, seq, d), F32),
                   jax.ShapeDtypeStruct((nb, cb, S5_STATE), F32), jax.ShapeDtypeStruct((nb, cb, S5_STATE), F32),
                   jax.ShapeDtypeStruct((nb, ns, S5_GROUP), F32), jax.ShapeDtypeStruct((nb, ns, S5_GROUP), F32),
                   jax.ShapeDtypeStruct((nb, 1, ns), F32), jax.ShapeDtypeStruct((nb, 1, ns), F32),
                   jax.ShapeDtypeStruct((nb, 1, cb), F32)],
        scratch_shapes=([pltpu.VMEM((bsz, lc, ns), F32)] * 4 + [pltpu.VMEM((bsz, 1, ns), F32)] * 2
                        + [pltpu.VMEM((cb, ns), F32)] * 2 + [pltpu.VMEM((ns, cb), F32)] * 2),
        compiler_params=_cparams("arbitrary", "arbitrary"),
    )(dy, u, st_re, st_im, bbd_re, bbd_im, cbd_re, cbd_im, ab_re, ab_im, dskip)


ATT_HEADS = 8
ATT_HEADS_FWD = 8
ATT_LANES = ATT_HEADS * HEAD_DIM
ATT_KEYS = 2 * ATT_BLOCK
ATT_Q = 256
ATT_SCALE = 1.0 / math.sqrt(HEAD_DIM)
_NT = (((1,), (1,)), ((), ()))
_TN = (((0,), (0,)), ((), ()))
_HEADS = [slice(h * HEAD_DIM, (h + 1) * HEAD_DIM) for h in range(ATT_HEADS)]
_HALF = [slice(0, ATT_BLOCK), slice(ATT_BLOCK, ATT_KEYS)]


def _log_sigmoids(z):
    sp = jnp.log(1.0 + jnp.exp(-jnp.abs(z)))
    ls = jnp.minimum(z, 0.0) - sp
    return ls, ls - z


def _sum_matrix(after, inclusive):
    j = lax.broadcasted_iota(jnp.int32, (ATT_KEYS, ATT_KEYS), 0) % ATT_BLOCK
    s = lax.broadcasted_iota(jnp.int32, (ATT_KEYS, ATT_KEYS), 1)
    if after:
        hit = (j >= s) if inclusive else (j > s)
    else:
        hit = (j <= s) if inclusive else (j < s)
    return jnp.where(jnp.logical_or(hit, s >= ATT_BLOCK), 1.0, 0.0).astype(BF16)


def _hi_lo(v):
    hi = v.astype(BF16)
    lo = (v - hi.astype(F32)).astype(BF16)
    return jnp.concatenate([hi, lo], axis=1)


def _strict_mask(i, j):
    t = i * ATT_Q + lax.broadcasted_iota(jnp.int32, (ATT_Q, ATT_KEYS), 0)
    s = j * ATT_KEYS + lax.broadcasted_iota(jnp.int32, (ATT_Q, ATT_KEYS), 1)
    return s < t


def attention_fwd(q, k, v):
    bsz, seq, d = q.shape
    n_heads = ATT_HEADS_FWD if d % (ATT_HEADS_FWD * HEAD_DIM) == 0 else ATT_HEADS
    lanes = n_heads * HEAD_DIM
    heads = [slice(h * HEAD_DIM, (h + 1) * HEAD_DIM) for h in range(n_heads)]

    def body(q_ref, k_ref, v_ref, o_ref, tot_ref, z_buf, ls_buf, cs_buf, acc_buf, run_buf):
        i = pl.program_id(2)
        jd = ((i + 1) * ATT_Q - 1) // ATT_KEYS
        sums = _sum_matrix(True, False)
        acc_buf[...] = jnp.zeros_like(acc_buf)
        run_buf[...] = jnp.zeros_like(run_buf)

        def block(j, masked):
            rows = pl.ds(pl.multiple_of(j * ATT_KEYS, ATT_KEYS), ATT_KEYS)
            strict = _strict_mask(i, j) if masked else None
            for h, ln in enumerate(heads):
                z_buf[h] = lax.dot_general(q_ref[:, ln], k_ref[rows, ln], _NT, preferred_element_type=F32)
            for h in range(n_heads):
                for half, cols in enumerate(_HALF):
                    ls, lf = _log_sigmoids(z_buf[h, :, cols])
                    if masked:
                        lf = jnp.where(strict[:, cols], lf, 0.0)
                    ls_buf[h, :, cols] = ls
                    cs_buf[h, half] = jnp.dot(_hi_lo(lf), sums, preferred_element_type=F32)
            for h, ln in enumerate(heads):
                run = run_buf[h]
                late, early = cs_buf[h, 1], cs_buf[h, 0]
                a1 = run + late[:, _HALF[0]]
                run = run + late[:, _HALF[1]]
                a0 = run + early[:, _HALF[0]]
                run_buf[h] = run + early[:, _HALF[1]]
                w = jnp.exp(ls_buf[h] + jnp.concatenate([a0, a1], axis=1))
                if masked:
                    w = jnp.where(strict, w, 0.0)
                acc_buf[h] += jnp.dot(w.astype(BF16), v_ref[rows, ln], preferred_element_type=F32)

        block(jd, True)

        def step(it, carry):
            block(jd - 1 - it, False)
            return carry

        lax.fori_loop(0, jd, step, 0)
        o_ref[...] = jnp.concatenate([acc_buf[h] for h in range(n_heads)], axis=1).astype(o_ref.dtype)
        tot_ref[...] = jnp.concatenate([run_buf[h, :, :HEAD_DIM] for h in range(n_heads)], axis=1)

    blk = pl.BlockSpec((None, ATT_Q, lanes), lambda b, p, i: (b, i, p))
    full = pl.BlockSpec((None, seq, lanes), lambda b, p, i: (b, 0, p))
    tile = (n_heads, ATT_Q, ATT_KEYS)
    return pl.pallas_call(
        body, name="attention_fwd", grid=(bsz, d // lanes, seq // ATT_Q),
        in_specs=[blk, full, full], out_specs=[blk, blk],
        out_shape=[jax.ShapeDtypeStruct((bsz, seq, d), BF16), jax.ShapeDtypeStruct((bsz, seq, d), F32)],
        scratch_shapes=[pltpu.VMEM(tile, F32), pltpu.VMEM(tile, F32),
                        pltpu.VMEM((n_heads, 2, ATT_Q, ATT_KEYS), F32),
                        pltpu.VMEM((n_heads, ATT_Q, HEAD_DIM), F32),
                        pltpu.VMEM((n_heads, ATT_Q, ATT_BLOCK), F32)],
        compiler_params=_cparams("arbitrary", "arbitrary", "arbitrary"),
    )(q, k, v)


def attention_bwd(q, k, v, tot, do):
    bsz, seq, d = q.shape

    def body(q_ref, k_ref, v_ref, tot_ref, do_ref, dq_ref, dk_ref, dv_ref,
             z_buf, dw_buf, ls_buf, e_buf, up_buf, bf_buf, w_buf, do_buf, dq_buf, tot_buf, run_buf, erun_buf):
        i = pl.program_id(2)
        jd = ((i + 1) * ATT_Q - 1) // ATT_KEYS

        @pl.when(i == 0)
        def _():
            dk_ref[...] = jnp.zeros_like(dk_ref)
            dv_ref[...] = jnp.zeros_like(dv_ref)

        upto_incl, upto_excl = _sum_matrix(False, True), _sum_matrix(False, False)
        do_buf[...] = do_ref[...].astype(BF16)
        for h, ln in enumerate(_HEADS):
            tot_buf[h] = jnp.concatenate([tot_ref[:, ln], tot_ref[:, ln]], axis=1)
        dq_buf[...] = jnp.zeros_like(dq_buf)
        run_buf[...] = jnp.zeros_like(run_buf)
        erun_buf[...] = jnp.zeros_like(erun_buf)

        def block(j, masked):
            rows = pl.ds(pl.multiple_of(j * ATT_KEYS, ATT_KEYS), ATT_KEYS)
            strict = _strict_mask(i, j) if masked else None
            for h, ln in enumerate(_HEADS):
                z_buf[h] = lax.dot_general(q_ref[:, ln], k_ref[rows, ln], _NT, preferred_element_type=F32)
                dw_buf[h] = lax.dot_general(do_buf[:, ln], v_ref[rows, ln], _NT, preferred_element_type=F32)
            for h in range(ATT_HEADS):
                for half, cols in enumerate(_HALF):
                    ls, lf = _log_sigmoids(z_buf[h, :, cols])
                    if masked:
                        lf = jnp.where(strict[:, cols], lf, 0.0)
                    ls_buf[h, :, cols] = ls
                    up_buf[h, half] = jnp.dot(_hi_lo(lf), upto_incl, preferred_element_type=F32)
            for h in range(ATT_HEADS):
                run = run_buf[h]
                early, late = up_buf[h, 0], up_buf[h, 1]
                u0 = run + early[:, _HALF[0]]
                run = run + early[:, _HALF[1]]
                u1 = run + late[:, _HALF[0]]
                run_buf[h] = run + late[:, _HALF[1]]
                tot_h = tot_buf[h]
                after = jnp.concatenate([tot_h - u0, tot_h - u1], axis=1)
                w = jnp.exp(ls_buf[h] + after)
                if masked:
                    w = jnp.where(strict, w, 0.0)
                w_buf[h] = w.astype(BF16)
                e = dw_buf[h] * w
                e_buf[h] = e
                for half, cols in enumerate(_HALF):
                    bf_buf[h, half] = jnp.dot(_hi_lo(e[:, cols]), upto_excl, preferred_element_type=F32)
            dks, dvs = [], []
            for h, ln in enumerate(_HEADS):
                erun = erun_buf[h]
                early, late = bf_buf[h, 0], bf_buf[h, 1]
                b0 = erun + early[:, _HALF[0]]
                erun = erun + early[:, _HALF[1]]
                b1 = erun + late[:, _HALF[0]]
                erun_buf[h] = erun + late[:, _HALF[1]]
                e = e_buf[h]
                dz = e - jnp.exp(ls_buf[h]) * (e + jnp.concatenate([b0, b1], axis=1))
                if masked:
                    dz = jnp.where(strict, dz, 0.0)
                dz = dz.astype(BF16)
                dq_buf[h] += jnp.dot(dz, k_ref[rows, ln], preferred_element_type=F32)
                dks.append(lax.dot_general(dz, q_ref[:, ln], _TN, preferred_element_type=F32))
                dvs.append(lax.dot_general(w_buf[h], do_buf[:, ln], _TN, preferred_element_type=F32))
            dk_ref[rows, :] += jnp.concatenate(dks, axis=1)
            dv_ref[rows, :] += jnp.concatenate(dvs, axis=1)

        def step(j, carry):
            block(j, False)
            return carry

        lax.fori_loop(0, jd, step, 0)
        block(jd, True)
        dq_ref[...] = jnp.concatenate([dq_buf[h] for h in range(ATT_HEADS)], axis=1) * ATT_SCALE

    blk = pl.BlockSpec((None, ATT_Q, ATT_LANES), lambda b, p, i: (b, i, p))
    full = pl.BlockSpec((None, seq, ATT_LANES), lambda b, p, i: (b, 0, p), pipeline_mode=pl.Buffered(1))
    shape = jax.ShapeDtypeStruct((bsz, seq, d), F32)
    tile = (ATT_HEADS, ATT_Q, ATT_KEYS)
    pair = (ATT_HEADS, 2, ATT_Q, ATT_KEYS)
    square = (ATT_HEADS, ATT_Q, ATT_BLOCK)
    return pl.pallas_call(
        body, name="attention_bwd", grid=(bsz, d // ATT_LANES, seq // ATT_Q),
        in_specs=[blk, full, full, blk, blk], out_specs=[blk, full, full], out_shape=[shape, shape, shape],
        scratch_shapes=[pltpu.VMEM(tile, F32), pltpu.VMEM(tile, F32), pltpu.VMEM(tile, F32), pltpu.VMEM(tile, F32),
                        pltpu.VMEM(pair, F32), pltpu.VMEM(pair, F32), pltpu.VMEM(tile, BF16),
                        pltpu.VMEM((ATT_Q, ATT_LANES), BF16), pltpu.VMEM((ATT_HEADS, ATT_Q, HEAD_DIM), F32),
                        pltpu.VMEM(square, F32), pltpu.VMEM(square, F32), pltpu.VMEM(square, F32)],
        compiler_params=_cparams("arbitrary", "arbitrary", "arbitrary"),
    )(q, k, v, tot, do)


def _tile_fused(seq):
    return _divisor_tile(seq, 1024, 16)


MLP_SLABS = 1


def mlp_core_fwd(h, w1, w2, x, gate, name):
    bsz, seq, d = x.shape
    ns, _, fs = w1.shape
    t = bsz * seq
    tm = _tile_fused(seq)
    g = MLP_SLABS if ns % MLP_SLABS == 0 else 1
    steps = ns // g

    def body(h_ref, w1_ref, w2_ref, x_ref, g_ref, act_ref, ff_ref, out_ref, acc_ref):
        s = pl.program_id(1)
        hb = h_ref[...]
        part = None
        for k in range(g):
            pre = jnp.dot(hb, w1_ref[k], preferred_element_type=F32)
            act = jnp.square(jnp.maximum(pre, 0.0)).astype(BF16)
            act_ref[:, k * fs:(k + 1) * fs] = act
            p = jnp.dot(act, w2_ref[k], preferred_element_type=F32)
            part = p if part is None else part + p

        @pl.when(s == 0)
        def _():
            acc_ref[...] = part

        @pl.when(s > 0)
        def _():
            acc_ref[...] += part

        @pl.when(s == steps - 1)
        def _():
            ff = acc_ref[...]
            ff_ref[...] = ff
            out_ref[...] = x_ref[...] + g_ref[...] * ff

    rows = pl.BlockSpec((tm, d), lambda i, s: (i, 0))
    act, ff, out = pl.pallas_call(
        body, name=name, grid=(t // tm, steps),
        in_specs=[rows, pl.BlockSpec((g, d, fs), lambda i, s: (s, 0, 0)),
                  pl.BlockSpec((g, fs, d), lambda i, s: (s, 0, 0)), rows,
                  pl.BlockSpec((None, 1, d), lambda i, s: ((i * tm) // seq, 0, 0))],
        out_specs=[pl.BlockSpec((tm, g * fs), lambda i, s: (i, s)), rows, rows],
        out_shape=[jax.ShapeDtypeStruct((t, ns * fs), BF16), jax.ShapeDtypeStruct((t, d), F32),
                   jax.ShapeDtypeStruct((t, d), F32)],
        scratch_shapes=[pltpu.VMEM((tm, d), F32)],
        compiler_params=_cparams("arbitrary", "arbitrary"),
    )(h.reshape(t, d), w1, w2, x.reshape(t, d), gate)
    return act, ff.reshape(bsz, seq, d), out.reshape(bsz, seq, d)


def mlp_core_bwd(dff, act, w1, w2, name):
    t, d = dff.shape
    ns, _, fs = w1.shape
    tm = _tile_fused(t)
    g = MLP_SLABS if ns % MLP_SLABS == 0 else 1
    steps = ns // g

    def body(dff_ref, act_ref, w1_ref, w2_ref, dpre_ref, dh_ref, acc_ref):
        s = pl.program_id(1)
        db = dff_ref[...]
        part = None
        for k in range(g):
            cols = slice(k * fs, (k + 1) * fs)
            dact = lax.dot_general(db, w2_ref[k], _NT, preferred_element_type=F32)
            dpre = (dact * (2.0 * jnp.sqrt(act_ref[:, cols].astype(F32)))).astype(BF16)
            dpre_ref[:, cols] = dpre
            p = lax.dot_general(dpre, w1_ref[k], _NT, preferred_element_type=F32)
            part = p if part is None else part + p

        @pl.when(s == 0)
        def _():
            acc_ref[...] = part

        @pl.when(s > 0)
        def _():
            acc_ref[...] += part

        @pl.when(s == steps - 1)
        def _():
            dh_ref[...] = acc_ref[...]

    rows = pl.BlockSpec((tm, d), lambda i, s: (i, 0))
    slab = pl.BlockSpec((tm, g * fs), lambda i, s: (i, s))
    return pl.pallas_call(
        body, name=name, grid=(t // tm, steps),
        in_specs=[rows, slab, pl.BlockSpec((g, d, fs), lambda i, s: (s, 0, 0)),
                  pl.BlockSpec((g, fs, d), lambda i, s: (s, 0, 0))],
        out_specs=[slab, rows],
        out_shape=[jax.ShapeDtypeStruct((t, ns * fs), BF16), jax.ShapeDtypeStruct((t, d), F32)],
        scratch_shapes=[pltpu.VMEM((tm, d), F32)],
        compiler_params=_cparams("arbitrary", "arbitrary"),
    )(dff, act, w1, w2)


def mlp_fwd(x, g, sh, sc, gate, w1_handle, w2_handle, tag):
    bsz, seq, d = x.shape
    h = act_call(_norm_mod, [x, g, sh, sc], [(d, BF16, "tile")], tag + "_norm")[0]
    w1 = exchange_wait(w1_handle, h, tag + "_w1_wait")
    w2 = exchange_wait(w2_handle, h, tag + "_w2_wait")
    act, ff, out = mlp_core_fwd(h, w1, w2, x, gate, tag + "_core")
    return out, (h, act, ff), w1, w2


def glu_fwd(ge, w, x, gate, name):
    bsz, seq, d = x.shape
    ns, _, nb = w.shape
    half = ns // 2
    t = bsz * seq
    tm = _tile_fused(seq)

    def body(a_ref, wv_ref, wg_ref, x_ref, g_ref, val_ref, gt_ref, out_ref):
        a = a_ref[...]
        val = jnp.dot(a, wv_ref[...], preferred_element_type=F32)
        gt = jnp.dot(a, wg_ref[...], preferred_element_type=F32)
        val_ref[...] = val
        gt_ref[...] = gt
        out_ref[...] = x_ref[...] + g_ref[...] * (val * jax.nn.sigmoid(gt))

    cols = pl.BlockSpec((tm, nb), lambda i, j: (i, j))
    res = pl.pallas_call(
        body, name=name, grid=(t // tm, half),
        in_specs=[pl.BlockSpec((tm, d), lambda i, j: (i, 0)),
                  pl.BlockSpec((None, d, nb), lambda i, j: (j, 0, 0)),
                  pl.BlockSpec((None, d, nb), lambda i, j: (half + j, 0, 0)), cols,
                  pl.BlockSpec((None, 1, nb), lambda i, j: ((i * tm) // seq, 0, j))],
        out_specs=[cols, cols, cols],
        out_shape=[jax.ShapeDtypeStruct((t, d), F32)] * 3,
        compiler_params=_cparams("arbitrary", "arbitrary"),
    )(ge.reshape(t, d), w, w, x.reshape(t, d), gate)
    return tuple(r.reshape(bsz, seq, d) for r in res)


def _gate_bwd(dx, f, gate):
    return gate * dx, _rowsum(dx * f)


def _gate_bwd_outs(d):
    return [(d, BF16, "tile"), (d, F32, "seq")]


def _norm_bwd_outs(d):
    return [(d, F32, "tile"), (d, F32, "seq"), (d, F32, "seq"), (d, F32, "all")]


def _norm_then_gate_bwd(x, g, sc, dh, dres, f, gate):
    res = _norm_mod_bwd(x, g, sc, dh, dres)
    return (*res, *_gate_bwd(res[0], f, gate))


def mlp_bwd(dout, dff, x, g, sc, w1, w2, saved, tag, branch=None):
    bsz, seq, d = x.shape
    t = bsz * seq
    ns = w1.shape[0]
    h, act, _ = saved
    dff = dff.reshape(t, d)
    dpre, dh = mlp_core_bwd(dff, act, w1, w2, tag + "_dcore")
    dw2 = mm_tn(act, dff, "a", ns, tag + "_dw2")
    dw1 = mm_tn(h.reshape(t, d), dpre, "c", ns, tag + "_dw1")
    (dw1_handle, dw2_handle), token = exchange_start([(dw1, True), (dw2, True)], tag + "_dw_start")
    ins = [x, g + token[0, 0], sc, dh.reshape(bsz, seq, d), dout]
    if branch is None:
        dx, dsh, dsc, dg = act_call(_norm_mod_bwd, ins, _norm_bwd_outs(d), tag + "_dnorm")
        into_branch = None
    else:
        dx, dsh, dsc, dg, *into_branch = act_call(_norm_then_gate_bwd, ins + list(branch),
                                                  _norm_bwd_outs(d) + _gate_bwd_outs(d), tag + "_dnorm")
    return dx, dw1_handle, dw2_handle, (dsh, dsc, dg), into_branch


def _block_diag(m):
    _, rows, c = m.shape
    k = S5_BLOCK_GROUPS
    row_group = lax.broadcasted_iota(jnp.int32, (rows, k * c), 0) // (rows // k)
    col_group = lax.broadcasted_iota(jnp.int32, (rows, k * c), 1) // c
    return jnp.where(row_group == col_group, jnp.tile(m, (1, 1, k)), 0.0)


def kernel(x, c, ada_w, ada_b, mix_norm_g, mlp_norm_g, mlp_w1, mlp_w2, s5_a_re, s5_a_im, s5_log_dt, s5_b_re, s5_b_im, s5_c_re, s5_c_im, s5_d, s5_w_glu, kv_ada_w, kv_ada_b, kv_norm_g, w_kv, k_norm_g, sb_w_q, q_norm_g, sb_w_o, loss_target, m_ada_w, m_ada_b, m_mix_norm_g, m_mlp_norm_g, m_mlp_w1, m_mlp_w2, m_s5_a_re, m_s5_a_im, m_s5_log_dt, m_s5_b_re, m_s5_b_im, m_s5_c_re, m_s5_c_im, m_s5_d, m_s5_w_glu, m_kv_ada_w, m_kv_ada_b, m_kv_norm_g, m_w_kv, m_k_norm_g, m_sb_w_q, m_q_norm_g, m_sb_w_o, v_ada_w, v_ada_b, v_mix_norm_g, v_mlp_norm_g, v_mlp_w1, v_mlp_w2, v_s5_a_re, v_s5_a_im, v_s5_log_dt, v_s5_b_re, v_s5_b_im, v_s5_c_re, v_s5_c_im, v_s5_d, v_s5_w_glu, v_kv_ada_w, v_kv_ada_b, v_kv_norm_g, v_w_kv, v_k_norm_g, v_sb_w_q, v_q_norm_g, v_sb_w_o):
    bsz, seq, d = x.shape
    t = bsz * seq
    n_groups = d // S5_GROUP
    nb = n_groups // S5_BLOCK_GROUPS
    gp = n_groups * S5_STATE
    dev = 4 * lax.axis_index("x") + 2 * lax.axis_index("y") + lax.axis_index("c")
    e_ada, e_kv = 6 * d, 2 * d
    n_ada, n_kv = e_ada // N_DEV, e_kv // N_DEV

    d_skip = all_gather(s5_d, "gather_skip").reshape(1, 1, d)
    c_all = all_gather(c, "gather_c").reshape(N_DEV * bsz, d)

    w_cols = jnp.concatenate([ada_w[0], ada_w[1], kv_ada_w], axis=1)
    b_cols = jnp.concatenate([
        lax.dynamic_slice_in_dim(ada_b[0], dev * n_ada, n_ada),
        lax.dynamic_slice_in_dim(ada_b[1], dev * n_ada, n_ada),
        lax.dynamic_slice_in_dim(kv_ada_b, dev * n_kv, n_kv)])[None, :]
    mod_cols = ada_fwd(c_all, w_cols, b_cols)
    mod_all = all_gather(mod_cols, "gather_mod")
    mod_mine = lax.dynamic_slice_in_dim(mod_all, dev * bsz, bsz, axis=1)
    mod_mine = jnp.transpose(mod_mine, (1, 0, 2))
    mods = []
    for i in range(2):
        full = mod_mine[:, :, i * n_ada:(i + 1) * n_ada].reshape(bsz, e_ada)
        mods.append([full[:, None, j * d:(j + 1) * d] for j in range(6)])
    kv_full = mod_mine[:, :, 2 * n_ada:].reshape(bsz, e_kv)
    kv_sh, kv_sc = kv_full[:, None, :d], kv_full[:, None, d:]

    par = lambda p: p.reshape(1, 1, -1)

    shards = [s5_w_glu[0], mlp_w1[0], mlp_w2[0], w_kv, sb_w_q[0], sb_w_o[0], mlp_w1[1], mlp_w2[1]]
    gathers, gather_token = exchange_start([(w.astype(BF16), False) for w in shards], "gather_start", after=[mod_all, d_skip])
    glu_handle, w1_0_handle, w2_0_handle, wkv_handle, wq_handle, wo_handle, w1_1_handle, w2_1_handle = gathers
    started = gather_token[0, 0]

    sh_a, sc_a, g_a, sh_m, sc_m, g_m = mods[0]
    lam_re, lam_im = s5_a_re.reshape(gp, 1), s5_a_im.reshape(gp, 1)
    log_dt = jnp.broadcast_to(s5_log_dt.reshape(n_groups, 1), (n_groups, S5_STATE)).reshape(gp, 1)
    b_re, b_im = s5_b_re.reshape(gp, S5_GROUP), s5_b_im.reshape(gp, S5_GROUP)
    ab_re, ab_im, bb_re, bb_im = s5_prep(lam_re, lam_im, log_dt, b_re, b_im)
    to_bbd = lambda m: _block_diag(
        jnp.swapaxes(m.reshape(nb, S5_BLOCK_GROUPS, S5_STATE, S5_GROUP), 2, 3).reshape(nb, -1, S5_STATE))
    to_cbd = lambda m: _block_diag(
        jnp.swapaxes(m.reshape(nb, S5_BLOCK_GROUPS, S5_GROUP, S5_STATE), 2, 3).reshape(nb, -1, S5_GROUP))
    bbd_re, bbd_im = to_bbd(bb_re), to_bbd(bb_im)
    cbd_re, cbd_im = to_cbd(s5_c_re[0]), to_cbd(s5_c_im[0])
    abr, abi = ab_re.reshape(nb, 1, -1), ab_im.reshape(nb, 1, -1)

    h0 = act_call(_norm_mod, [x, par(mix_norm_g[0]) + started, sh_a, sc_a], [(d, F32, "tile")], "mix0_norm")[0]
    y, ge, st_re, st_im = s5_fwd(h0, bbd_re, bbd_im, cbd_re, cbd_im, abr, abi, d_skip)
    w_glu = exchange_wait(glu_handle, ge, "glu_w_wait")
    z_val, z_gate, x1 = glu_fwd(ge, w_glu, x, g_a, "glu_up")
    x2, mlp0_saved, w1_0, w2_0 = mlp_fwd(x1, par(mlp_norm_g[0]), sh_m, sc_m, g_m, w1_0_handle, w2_0_handle, "mlp0")

    sh_a1, sc_a1, g_a1, sh_m1, sc_m1, g_m1 = mods[1]
    kg = par(jnp.tile(k_norm_g, d // HEAD_DIM))
    qg = par(jnp.tile(q_norm_g[0], d // HEAD_DIM))
    hkv = act_call(_norm_mod, [x2, par(kv_norm_g), kv_sh, kv_sc], [(d, BF16, "tile")], "kv_norm")[0]
    wkv = exchange_wait(wkv_handle, hkv, "kv_w_wait")
    half = N_DEV // 2
    k_raw, k_h = mm_nn_col(hkv.reshape(t, d), wkv, "k_proj", (F32, BF16),
                           lambda acc, g_: (acc, _head_norm(acc, g_)), (0, half), (kg.reshape(1, d),))
    v_h = mm_nn_col(hkv.reshape(t, d), wkv, "v_proj", (BF16,), None, (half, half))
    k_raw, k_h, v_h = (a.reshape(bsz, seq, d) for a in (k_raw, k_h, v_h))
    h1 = act_call(_norm_mod, [x2, par(mix_norm_g[1]), sh_a1, sc_a1], [(d, BF16, "tile")], "mix1_norm")[0]
    wq = exchange_wait(wq_handle, h1, "q_w_wait")
    whole = lambda w: w.reshape(1, d, d)
    tm_epi = _tile_fused(seq)
    vec = lambda tm: [pl.BlockSpec((1, d), lambda i, s: (0, 0))]
    q_raw, q_h = mm_nn_row(h1.reshape(t, d), whole(wq), "q_proj", (F32, BF16),
                           lambda acc, g_: (acc, _head_norm(acc, g_) * ATT_SCALE), (qg.reshape(1, d),), vec, tm_epi)
    q_raw, q_h = q_raw.reshape(bsz, seq, d), q_h.reshape(bsz, seq, d)
    o, att_tot = attention_fwd(q_h, k_h, v_h)
    wo = exchange_wait(wo_handle, o, "o_w_wait")
    res_specs = lambda tm: [pl.BlockSpec((tm, d), lambda i, s: (i, 0)),
                            pl.BlockSpec((None, 1, d), lambda i, s: ((i * tm) // seq, 0, 0))]
    mix1, x3 = mm_nn_row(o.reshape(t, d), whole(wo), "o_proj", (F32, F32),
                         lambda acc, x_, g_: (acc, x_ + g_ * acc), (x2.reshape(t, d), g_a1), res_specs, tm_epi)
    mix1, x3 = mix1.reshape(bsz, seq, d), x3.reshape(bsz, seq, d)
    x4, mlp1_saved, w1_1, w2_1 = mlp_fwd(x3, par(mlp_norm_g[1]), sh_m1, sc_m1, g_m1, w1_1_handle, w2_1_handle, "mlp1")

    def loss_fn(y_, t_, f_, g_):
        diff = y_ - t_
        part = jnp.sum(0.5 * jnp.mean(diff * diff, axis=-1, keepdims=True), axis=0, keepdims=True)
        dy_ = diff * (1.0 / d)
        return (jnp.broadcast_to(part, (1, LANES)), dy_, *_gate_bwd(dy_, f_, g_))

    loss_part, dx4, dff1, dg_m1 = act_call(
        loss_fn, [x4, loss_target, mlp1_saved[2], g_m1],
        [(LANES, F32, "all"), (d, F32, "tile")] + _gate_bwd_outs(d), "loss")

    dx3, dw1_1, dw2_1, (dsh_m1, dsc_m1, dgn_mlp1), (dmix1, dg_a1) = mlp_bwd(
        dx4, dff1, x3, par(mlp_norm_g[1]), sc_m1, w1_1, w2_1, mlp1_saved, "mlp1", (mix1, g_a1))
    dmix1 = dmix1.reshape(t, d)
    do = mm_nt_row(dmix1, whole(wo), "o_dproj").reshape(bsz, seq, d)
    dwo = mm_tn(o.reshape(t, d), dmix1, "a", N_DEV, "o_dw")
    dq, dk, dv = attention_bwd(q_h, k_h, v_h, att_tot, do)
    dq_raw, dqg = act_call(_head_norm_bwd, [q_raw, qg, dq], [(d, BF16, "tile"), (d, F32, "all")], "q_dnorm")
    dq_raw = dq_raw.reshape(t, d)
    dh1 = mm_nt_row(dq_raw, whole(wq), "q_dproj").reshape(bsz, seq, d)
    dwq = mm_tn(h1.reshape(t, d), dq_raw, "a", N_DEV, "q_dw")
    dx2, dsh_a1, dsc_a1, dgn_mix1 = act_call(
        _norm_mod_bwd, [x2, par(mix_norm_g[1]), sc_a1, dh1, dx3],
        [(d, F32, "tile"), (d, F32, "seq"), (d, F32, "seq"), (d, F32, "all")], "mix1_dnorm")

    def kv_bwd_fn(k_, g_, dk_, dv_):
        dk_raw, dg_ = _head_norm_bwd(k_, g_, dk_)
        return jnp.concatenate([dk_raw, dv_], axis=1), dg_

    dkvf, dkg = act_call(kv_bwd_fn, [k_raw, kg, dk, dv], [(2 * d, BF16, "tile"), (d, F32, "all")], "k_dnorm")
    dkvf = dkvf.reshape(t, 2 * d)
    dhkv = mm_nt_col(dkvf, wkv, "kv_dproj").reshape(bsz, seq, d)
    dwkv = mm_tn(hkv.reshape(t, d), dkvf, "c", N_DEV, "kv_dw")
    (dwo, dwq, dwkv), att_token = exchange_start([(dwo, True), (dwq, True), (dwkv, True)], "att_dw_start")
    dx2, dkv_sh, dkv_sc, dgn_kv, dff0, dg_m0 = act_call(
        _norm_then_gate_bwd, [x2, par(kv_norm_g) + att_token[0, 0], kv_sc, dhkv, dx2, mlp0_saved[2], g_m],
        _norm_bwd_outs(d) + _gate_bwd_outs(d), "kv_dnorm")

    dx1, dw1_0, dw2_0, (dsh_m0, dsc_m0, dgn_mlp0), _ = mlp_bwd(
        dx2, dff0, x1, par(mlp_norm_g[0]), sc_m, w1_0, w2_0, mlp0_saved, "mlp0")

    def glu_bwd_fn(do_, val, gt, g_):
        sig = jax.nn.sigmoid(gt)
        dmix = g_ * do_
        dz = jnp.concatenate([dmix * sig, dmix * val * sig * (1.0 - sig)], axis=1)
        return dz, _rowsum(do_ * (val * sig))

    dz, dg_a0 = act_call(glu_bwd_fn, [dx1, z_val, z_gate, g_a], [(2 * d, BF16, "tile"), (d, F32, "seq")], "glu_dres")
    dz = dz.reshape(t, 2 * d)
    dy = mm_nt_col(dz, w_glu, "glu_dup", (F32,), lambda acc, y_: (acc * _gelu_grad(y_),),
                   (y.reshape(t, d),)).reshape(bsz, seq, d)
    dwglu = mm_tn(ge.reshape(t, d), dz, "c", N_DEV, "glu_dw")
    (dwglu,), glu_token = exchange_start([(dwglu, True)], "glu_dw_start")
    du, dbbd_re, dbbd_im, dcbd_re, dcbd_im, dab_re, dab_im, dd_skip = s5_bwd(
        dy, h0, st_re, st_im, bbd_re, bbd_im, cbd_re, cbd_im, abr, abi, d_skip + glu_token[0, 0])
    dx0, dsh_a0, dsc_a0, dgn_mix0 = act_call(
        _norm_mod_bwd, [x, par(mix_norm_g[0]), sc_a, du, dx1],
        [(d, F32, "tile"), (d, F32, "seq"), (d, F32, "seq"), (d, F32, "all")], "mix0_dnorm")

    from_bbd = lambda m: jnp.swapaxes(m.reshape(nb, S5_BLOCK_GROUPS, S5_GROUP, S5_STATE), 2, 3).reshape(gp, S5_GROUP)
    d_c = lambda m: jnp.swapaxes(m.reshape(nb, S5_BLOCK_GROUPS, S5_STATE, S5_GROUP), 2, 3).reshape(
        1, n_groups, S5_GROUP, S5_STATE)
    d_lam_re, d_lam_im, d_log_dt, d_b_re, d_b_im = s5_prep_bwd(
        lam_re, lam_im, log_dt, b_re, b_im, dab_re.reshape(gp, 1), dab_im.reshape(gp, 1),
        from_bbd(dbbd_re), from_bbd(dbbd_im))

    small = all_reduce_small([
        jnp.stack([dgn_mix0.reshape(d), dgn_mix1.reshape(d)]),
        jnp.stack([dgn_mlp0.reshape(d), dgn_mlp1.reshape(d)]),
        d_lam_re.reshape(1, n_groups, S5_STATE), d_lam_im.reshape(1, n_groups, S5_STATE),
        d_log_dt.reshape(1, n_groups, S5_STATE).sum(axis=-1),
        d_b_re.reshape(s5_b_re.shape), d_b_im.reshape(s5_b_im.shape),
        d_c(dcbd_re), d_c(dcbd_im),
        dd_skip.reshape(1, d),
        dgn_kv.reshape(d),
        dkg.reshape(d // HEAD_DIM, HEAD_DIM).sum(axis=0),
        dqg.reshape(d // HEAD_DIM, HEAD_DIM).sum(axis=0)[None, :],
        loss_part[0, 0, :1],
    ], "small_grads")
    (g_mix_norm, g_mlp_norm, g_a_re, g_a_im, g_log_dt, g_b_re, g_b_im, g_c_re, g_c_im,
     g_skip_full, g_kv_norm, g_k_norm, g_q_norm, loss_all) = small
    loss = loss_all[0]
    g_s5_d = lax.dynamic_slice_in_dim(g_skip_full, dev * (d // N_DEV), d // N_DEV, axis=1)

    dm_mine = jnp.concatenate([
        dsh_a0, dsc_a0, dg_a0, dsh_m0, dsc_m0, dg_m0,
        dsh_a1, dsc_a1, dg_a1, dsh_m1, dsc_m1, dg_m1, dkv_sh, dkv_sc], axis=2).reshape(bsz, 2 * e_ada + e_kv)
    dm_all = all_gather(dm_mine, "gather_dmod").reshape(N_DEV * bsz, 2 * e_ada + e_kv)
    dm_cols = jnp.concatenate([
        lax.dynamic_slice_in_dim(dm_all, dev * n_ada, n_ada, axis=1),
        lax.dynamic_slice_in_dim(dm_all, e_ada + dev * n_ada, n_ada, axis=1),
        lax.dynamic_slice_in_dim(dm_all, 2 * e_ada + dev * n_kv, n_kv, axis=1)], axis=1)
    dw_cols, db_all = ada_bwd(c_all, dm_cols, dm_all)
    g_ada_w = jnp.stack([dw_cols[:, :n_ada], dw_cols[:, n_ada:2 * n_ada]])
    g_kv_ada_w = dw_cols[:, 2 * n_ada:]
    g_ada_b = db_all[0, :2 * e_ada].reshape(2, e_ada)
    g_kv_ada_b = db_all[0, 2 * e_ada:]

    landed = lambda handle, name: slab_sum(exchange_wait(handle, dx0, name + "_wait"), name + "_sum")
    g_w1 = jnp.stack([landed(dw1_0, "rs_w1_0"), landed(dw1_1, "rs_w1_1")])
    g_w2 = jnp.stack([landed(dw2_0, "rs_w2_0"), landed(dw2_1, "rs_w2_1")])
    g_glu = landed(dwglu, "rs_glu")[None]
    g_wkv = landed(dwkv, "rs_wkv")
    g_wq = landed(dwq, "rs_wq")[None]
    g_wo = landed(dwo, "rs_wo")[None]

    weights = [ada_w, ada_b, mix_norm_g, mlp_norm_g, mlp_w1, mlp_w2, s5_a_re, s5_a_im, s5_log_dt, s5_b_re,
               s5_b_im, s5_c_re, s5_c_im, s5_d, s5_w_glu, kv_ada_w, kv_ada_b, kv_norm_g, w_kv, k_norm_g,
               sb_w_q, q_norm_g, sb_w_o]
    grads = [g_ada_w, g_ada_b, g_mix_norm, g_mlp_norm, g_w1, g_w2, g_a_re, g_a_im, g_log_dt, g_b_re,
             g_b_im, g_c_re, g_c_im, g_s5_d, g_glu, g_kv_ada_w, g_kv_ada_b, g_kv_norm, g_wkv, g_k_norm,
             g_wq, g_q_norm, g_wo]
    ms = [m_ada_w, m_ada_b, m_mix_norm_g, m_mlp_norm_g, m_mlp_w1, m_mlp_w2, m_s5_a_re, m_s5_a_im, m_s5_log_dt,
          m_s5_b_re, m_s5_b_im, m_s5_c_re, m_s5_c_im, m_s5_d, m_s5_w_glu, m_kv_ada_w, m_kv_ada_b, m_kv_norm_g,
          m_w_kv, m_k_norm_g, m_sb_w_q, m_q_norm_g, m_sb_w_o]
    vs = [v_ada_w, v_ada_b, v_mix_norm_g, v_mlp_norm_g, v_mlp_w1, v_mlp_w2, v_s5_a_re, v_s5_a_im, v_s5_log_dt,
          v_s5_b_re, v_s5_b_im, v_s5_c_re, v_s5_c_im, v_s5_d, v_s5_w_glu, v_kv_ada_w, v_kv_ada_b, v_kv_norm_g,
          v_w_kv, v_k_norm_g, v_sb_w_q, v_q_norm_g, v_sb_w_o]
    grads = [g.reshape(w.shape) for g, w in zip(grads, weights)]
    deltas, new_ms, new_vs = [], [], []
    for i, (w, g, m, v) in enumerate(zip(weights, grads, ms, vs)):
        dl, nm, nv = adamw(w, g, m, v, f"adamw_{i}")
        deltas.append(dl)
        new_ms.append(nm)
        new_vs.append(nv)
    return (loss, dx0, *grads, *deltas, *new_ms, *new_vs)
```

```python
import functools
import math

import jax
import jax.numpy as jnp
from jax import lax
from jax.experimental import pallas as pl
from jax.experimental.pallas import tpu as pltpu

F32 = jnp.float32
BF16 = jnp.bfloat16

N_DEV = 8
N_CHIPS = 4
MESH = pl.DeviceIdType.MESH
ANY = pl.BlockSpec(memory_space=pl.ANY)

LANES = 128
VMEM_LIMIT_BYTES = 48 * 2 ** 20
TILE_BUDGET_BYTES = 4 * 2 ** 20

S5_GROUP = 16
S5_STATE = 64
S5_BLOCK_GROUPS = 16
HEAD_DIM = 64
ATT_BLOCK = 128
EPS = 1e-6

ADAM_LR = 0.001
ADAM_B1 = 0.9
ADAM_B2 = 0.999
ADAM_EPS = 1e-08
ADAM_WD = 0.01
ADAM_STEP = 10


def _cparams(*sem):
    return pltpu.CompilerParams(dimension_semantics=sem, vmem_limit_bytes=VMEM_LIMIT_BYTES)


def _divisor_tile(n, limit, mult):
    best = None
    for t in range(mult, min(n, limit) + 1, mult):
        if n % t == 0:
            best = t
    return best if best is not None else n


def _tile_m(m):
    return _divisor_tile(m, 2048 if m >= 4096 else 256, 16)


def all_gather(x, name):
    def body(x_ref, out_ref, send_sems, recv_sems, local_sem):
        ax, ay, ac = lax.axis_index("x"), lax.axis_index("y"), lax.axis_index("c")
        me, sibling = (ax, ay, ac), (ax, ay, 1 - ac)
        chips = [(1 - ax, ay), (ax, 1 - ay), (1 - ax, 1 - ay)]

        def slot(px, py, pc):
            return out_ref.at[4 * px + 2 * py + pc]

        def copy(k, block, to, src=None):
            return pltpu.make_async_remote_copy(
                src_ref=slot(*block) if src is None else src, dst_ref=slot(*block),
                send_sem=send_sems.at[k], recv_sem=recv_sems.at[k], device_id=to, device_id_type=MESH)

        mine = pltpu.make_async_copy(x_ref, slot(*me), local_sem)
        mine.start()
        first = [copy(0, me, sibling, src=x_ref)]
        first += [copy(1 + j, me, (*chip, ac), src=x_ref) for j, chip in enumerate(chips)]
        for cp in first:
            cp.start()
        passed = [copy(4 + j, (*chip, ac), sibling) for j, chip in enumerate(chips)]
        for j, chip in enumerate(chips):
            copy(1 + j, (*chip, ac), me).wait_recv()
            passed[j].start()
        copy(0, sibling, me).wait_recv()
        for j, chip in enumerate(chips):
            copy(4 + j, (*chip, 1 - ac), me).wait_recv()
        for cp in first + passed:
            cp.wait_send()
        mine.wait()

    return pl.pallas_call(
        body, name=name,
        out_shape=jax.ShapeDtypeStruct((N_DEV,) + x.shape, x.dtype),
        in_specs=[ANY], out_specs=ANY,
        scratch_shapes=[pltpu.SemaphoreType.DMA((7,)), pltpu.SemaphoreType.DMA((7,)), pltpu.SemaphoreType.DMA],
    )(x)


HBM = pl.BlockSpec(memory_space=pltpu.HBM)
SEM = pl.BlockSpec(memory_space=pltpu.SEMAPHORE)
N_PEERS = N_DEV - 1


def _peers():
    ax, ay, ac = lax.axis_index("x"), lax.axis_index("y"), lax.axis_index("c")
    flip = lambda v, bit: 1 - v if bit else v
    return [(flip(ax, k & 4), flip(ay, k & 2), flip(ac, k & 1)) for k in range(1, N_DEV)]


def _dev_index(pos):
    return 4 * pos[0] + 2 * pos[1] + pos[2]


def exchange_start(items, name, after=None):
    n = len(items)
    srcs = [a for a, _ in items]
    blocks = [a.shape[1:] if scatter else a.shape for a, scatter in items]
    extra = list(after or ())

    def body(*refs):
        src_refs, land_refs = refs[:n], refs[n:2 * n]
        outs = refs[2 * n + len(extra):]
        send_sems, recv_sems = outs[:n], outs[n:2 * n]
        token = outs[-1]
        me = _dev_index((lax.axis_index("x"), lax.axis_index("y"), lax.axis_index("c")))
        for w, (_, scatter) in enumerate(items):
            for k, peer in enumerate(_peers()):
                src = src_refs[w].at[_dev_index(peer)] if scatter else src_refs[w]
                pltpu.make_async_remote_copy(
                    src_ref=src, dst_ref=land_refs[w].at[me], send_sem=send_sems[w].at[k],
                    recv_sem=recv_sems[w].at[k], device_id=peer, device_id_type=MESH).start()
        token[...] = jnp.zeros_like(token)

    lands = [lax.empty((N_DEV,) + blk, a.dtype) for a, blk in zip(srcs, blocks)]
    res = pl.pallas_call(
        body, name=name,
        out_shape=([pltpu.SemaphoreType.DMA((N_PEERS,))] * (2 * n)
                   + [pltpu.HBM(a.shape, a.dtype) for a in srcs] + [pltpu.HBM(l.shape, l.dtype) for l in lands]
                   + [jax.ShapeDtypeStruct((8, LANES), F32)]),
        in_specs=[HBM] * (2 * n) + [ANY] * len(extra),
        out_specs=[SEM] * (2 * n) + [HBM] * (2 * n) + [pl.BlockSpec(memory_space=pltpu.VMEM)],
        input_output_aliases={i: 2 * n + i for i in range(2 * n)},
        compiler_params=pltpu.CompilerParams(has_side_effects=pltpu.SideEffectType.DATAFLOW_SIDE_EFFECTING),
    )(*[pltpu.with_memory_space_constraint(a, pltpu.HBM) for a in srcs + lands], *extra)
    handles = [(res[w], res[n + w], res[2 * n + w], res[3 * n + w], scatter) for w, (_, scatter) in enumerate(items)]
    return handles, res[-1]


def exchange_wait(handle, after, name):
    send_sem, recv_sem, src, land, scatter = handle

    def body(src_ref, land_ref, send_ref, recv_ref, after_ref, src_out, land_out):
        for k, peer in enumerate(_peers()):
            slot = _dev_index(peer)
            copy = pltpu.make_async_remote_copy(
                src_ref=src_ref.at[slot] if scatter else src_ref, dst_ref=land_ref.at[slot],
                send_sem=send_ref.at[k], recv_sem=recv_ref.at[k], device_id=peer, device_id_type=MESH)
            copy.wait_send()
            copy.wait_recv()

    src, landed = pl.pallas_call(
        body, name=name,
        out_shape=(pltpu.HBM(src.shape, src.dtype), pltpu.HBM(land.shape, land.dtype)),
        in_specs=[HBM, HBM, SEM, SEM, ANY], out_specs=(HBM, HBM), input_output_aliases={0: 0, 1: 1},
        compiler_params=pltpu.CompilerParams(has_side_effects=pltpu.SideEffectType.DATAFLOW_SIDE_EFFECTING),
    )(src, land, send_sem, recv_sem, after)
    dev = _dev_index((lax.axis_index("x"), lax.axis_index("y"), lax.axis_index("c")))
    own = lax.dynamic_index_in_dim(src, dev, axis=0, keepdims=True) if scatter else src[None]
    return lax.dynamic_update_slice_in_dim(landed, own, dev, axis=0)


def rows_call(fn, ins, outs, name):
    rows = ins[0].shape[1]
    per_row = sum(a.shape[0] * a.shape[2] * a.dtype.itemsize for a in ins)
    per_row += sum(l * c * jnp.dtype(dt).itemsize for l, c, dt in outs)
    tr = _divisor_tile(rows, max(16, TILE_BUDGET_BYTES // per_row), 16)
    n_in = len(ins)

    def body(*refs):
        vals = fn(*[r[...] for r in refs[:n_in]])
        if not isinstance(vals, (tuple, list)):
            vals = (vals,)
        for r, v in zip(refs[n_in:], vals):
            r[...] = v.astype(r.dtype)

    def spec(l, c):
        return pl.BlockSpec((l, tr, c), lambda i: (0, i, 0))

    res = pl.pallas_call(
        body, name=name, grid=(rows // tr,),
        in_specs=[spec(a.shape[0], a.shape[2]) for a in ins],
        out_specs=[spec(l, c) for l, c, _ in outs],
        out_shape=[jax.ShapeDtypeStruct((l, rows, c), dt) for l, c, dt in outs],
        compiler_params=_cparams("arbitrary"),
    )(*ins)
    return res


def _as_rows(a, lead=0):
    shape = a.shape
    l = int(math.prod(shape[:lead])) if lead else 1
    rest = shape[lead:]
    c = rest[-1] if rest else 1
    r = int(math.prod(rest[:-1])) if len(rest) > 1 else 1
    return a.reshape(l, r, c)


def act_call(fn, ins, outs, name):
    bsz, seq = ins[0].shape[0], ins[0].shape[1]
    per_row = sum(a.shape[2] * a.dtype.itemsize for a in ins if a.shape[1] == seq)
    per_row += sum(c * jnp.dtype(dt).itemsize for c, dt, kind in outs if kind == "tile")
    ts = _divisor_tile(seq, max(16, TILE_BUDGET_BYTES // per_row), 16)
    n_in = len(ins)

    def in_spec(a):
        c = a.shape[2]
        if a.shape[1] == seq:
            return pl.BlockSpec((None, ts, c), lambda b, s: (b, s, 0))
        if a.shape[0] == bsz:
            return pl.BlockSpec((None, 1, c), lambda b, s: (b, 0, 0))
        return pl.BlockSpec((None, 1, c), lambda b, s: (0, 0, 0))

    def out_spec(c, kind):
        if kind == "tile":
            return pl.BlockSpec((None, ts, c), lambda b, s: (b, s, 0))
        if kind == "seq":
            return pl.BlockSpec((None, 1, c), lambda b, s: (b, 0, 0))
        return pl.BlockSpec((None, 1, c), lambda b, s: (0, 0, 0))

    def out_shape(c, dt, kind):
        if kind == "tile":
            return jax.ShapeDtypeStruct((bsz, seq, c), dt)
        return jax.ShapeDtypeStruct((bsz if kind == "seq" else 1, 1, c), dt)

    def accumulate(ref, v, first):
        @pl.when(first)
        def _():
            ref[...] = jnp.zeros_like(ref)

        ref[...] += v.astype(ref.dtype)

    def body(*refs):
        b, s = pl.program_id(0), pl.program_id(1)
        vals = fn(*[r[...] for r in refs[:n_in]])
        if not isinstance(vals, (tuple, list)):
            vals = (vals,)
        for ref, v, (_, _, kind) in zip(refs[n_in:], vals, outs):
            if kind == "tile":
                ref[...] = v.astype(ref.dtype)
            elif kind == "seq":
                accumulate(ref, v, s == 0)
            else:
                accumulate(ref, v, jnp.logical_and(b == 0, s == 0))

    return pl.pallas_call(
        body, name=name, grid=(bsz, seq // ts),
        in_specs=[in_spec(a) for a in ins],
        out_specs=[out_spec(c, kind) for c, _, kind in outs],
        out_shape=[out_shape(*o) for o in outs],
        compiler_params=_cparams("arbitrary", "arbitrary"),
    )(*ins)


def _mm(name, grid, a, a_spec, b, b_spec, dims, out_shape, out_spec, out_dtypes, acc_steps,
        epi=None, extras=(), extra_specs=()):
    n_ex, n_out = len(extras), len(out_dtypes)
    tile = tuple(d for d in out_spec.block_shape if d is not None)

    def body(*refs):
        a_ref, b_ref = refs[0], refs[1]
        ex_refs = refs[2:2 + n_ex]
        o_refs = refs[2 + n_ex:2 + n_ex + n_out]
        p = lax.dot_general(a_ref[...].astype(BF16), b_ref[...].astype(BF16), (dims, ((), ())),
                            preferred_element_type=F32)

        def finish(acc):
            vals = epi(acc, *[r[...] for r in ex_refs]) if epi is not None else (acc,) * n_out
            for r, v in zip(o_refs, vals):
                r[...] = v.astype(r.dtype)

        if not acc_steps:
            finish(p)
        else:
            acc_ref = refs[-1]
            s = pl.program_id(1)

            @pl.when(s == 0)
            def _():
                acc_ref[...] = p

            @pl.when(s > 0)
            def _():
                acc_ref[...] += p

            @pl.when(s == acc_steps - 1)
            def _():
                finish(acc_ref[...])

    res = pl.pallas_call(
        body, name=name, grid=grid,
        in_specs=[a_spec, b_spec] + list(extra_specs),
        out_specs=[out_spec] * n_out,
        out_shape=[jax.ShapeDtypeStruct(out_shape, dt) for dt in out_dtypes],
        scratch_shapes=[pltpu.VMEM(tile, F32)] if acc_steps else [],
        compiler_params=_cparams("arbitrary", "arbitrary"),
    )(a, b, *extras)
    return res if n_out > 1 else res[0]


def mm_nn_col(a, w, name, out_dtypes=(F32,), epi=None, slabs=None, col_params=()):
    m, k = a.shape
    _, _, nb = w.shape
    first, count = slabs if slabs is not None else (0, w.shape[0])
    tm = _tile_m(m)
    return _mm(name, (m // tm, count), a, pl.BlockSpec((tm, k), lambda i, j: (i, 0)),
               w, pl.BlockSpec((None, k, nb), lambda i, j: (first + j, 0, 0)), ((1,), (0,)),
               (m, count * nb), pl.BlockSpec((tm, nb), lambda i, j: (i, j)), out_dtypes, 0, epi,
               col_params, [pl.BlockSpec((1, nb), lambda i, j: (0, j))] * len(col_params))


def mm_nn_row(a, w, name, out_dtypes=(F32,), epi=None, extras=(), extra_specs=None, tm=None):
    m = a.shape[0]
    ns, kb, n = w.shape
    tm = tm or _tile_m(m)
    return _mm(name, (m // tm, ns), a, pl.BlockSpec((tm, kb), lambda i, s: (i, s)),
               w, pl.BlockSpec((None, kb, n), lambda i, s: (s, 0, 0)), ((1,), (0,)),
               (m, n), pl.BlockSpec((tm, n), lambda i, s: (i, 0)), out_dtypes, ns, epi,
               extras, extra_specs(tm) if extras else ())


def mm_nt_col(dc, w, name, out_dtypes=(F32,), epi=None, extras=()):
    m = dc.shape[0]
    ns, k, nb = w.shape
    tm = _divisor_tile(m, 512, 16)
    n_ex, n_out = len(extras), len(out_dtypes)

    def body(dc_ref, w_ref, *rest):
        ex_refs, o_refs = rest[:n_ex], rest[n_ex:]
        w_all = jnp.concatenate([w_ref[s] for s in range(ns)], axis=1)
        acc = lax.dot_general(dc_ref[...].astype(BF16), w_all, (((1,), (1,)), ((), ())), preferred_element_type=F32)
        vals = epi(acc, *[r[...] for r in ex_refs]) if epi is not None else (acc,) * n_out
        for r, v in zip(o_refs, vals):
            r[...] = v.astype(r.dtype)

    rows = pl.BlockSpec((tm, k), lambda i: (i, 0))
    res = pl.pallas_call(
        body, name=name, grid=(m // tm,),
        in_specs=[pl.BlockSpec((tm, ns * nb), lambda i: (i, 0)), pl.BlockSpec((ns, k, nb), lambda i: (0, 0, 0))]
        + [rows] * n_ex,
        out_specs=[rows] * n_out,
        out_shape=[jax.ShapeDtypeStruct((m, k), dt) for dt in out_dtypes],
        compiler_params=_cparams("arbitrary"),
    )(dc, w, *extras)
    return res if n_out > 1 else res[0]


def mm_nt_row(dc, w, name, out_dtypes=(F32,), epi=None, extras=()):
    m, n = dc.shape
    ns, kb, _ = w.shape
    tm = _tile_m(m)
    spec = pl.BlockSpec((tm, kb), lambda i, s: (i, s))
    return _mm(name, (m // tm, ns), dc, pl.BlockSpec((tm, n), lambda i, s: (i, 0)),
               w, pl.BlockSpec((None, kb, n), lambda i, s: (s, 0, 0)), ((1,), (1,)),
               (m, ns * kb), spec, out_dtypes, 0, epi, extras, [spec] * len(extras))


def mm_tn(a, c, slab, ns, name, out_dtype=BF16):
    m, ka_all = a.shape
    nc_all = c.shape[1]
    ka = ka_all // ns if slab == "a" else ka_all
    nc = nc_all // ns if slab == "c" else nc_all
    tt = _divisor_tile(m, 256, 16)
    steps = m // tt

    def body(a_ref, c_ref, o_ref, acc_ref):
        t = pl.program_id(0)

        @pl.when(t == 0)
        def _():
            acc_ref[...] = jnp.zeros_like(acc_ref)

        for s in range(ns):
            a_s = a_ref[:, s * ka:(s + 1) * ka] if slab == "a" else a_ref[...]
            c_s = c_ref[:, s * nc:(s + 1) * nc] if slab == "c" else c_ref[...]
            acc_ref[s] += lax.dot_general(a_s.astype(BF16), c_s.astype(BF16), (((0,), (0,)), ((), ())),
                                          preferred_element_type=F32)

        @pl.when(t == steps - 1)
        def _():
            o_ref[...] = acc_ref[...].astype(o_ref.dtype)

    return pl.pallas_call(
        body, name=name, grid=(steps,),
        in_specs=[pl.BlockSpec((tt, ka_all), lambda t: (t, 0)), pl.BlockSpec((tt, nc_all), lambda t: (t, 0))],
        out_specs=pl.BlockSpec((ns, ka, nc), lambda t: (0, 0, 0)),
        out_shape=jax.ShapeDtypeStruct((ns, ka, nc), out_dtype),
        scratch_shapes=[pltpu.VMEM((ns, ka, nc), F32)],
        compiler_params=_cparams("arbitrary"),
    )(a, c)


def slab_sum(landed, name):
    shape = landed.shape[1:]
    total = rows_call(lambda g: jnp.sum(g.astype(F32), axis=0, keepdims=True),
                      [_as_rows(landed, 1)], [(1, shape[-1], F32)], name)[0]
    return total.reshape(shape)


def all_reduce_small(leaves, name):
    sizes = [int(a.size) for a in leaves]
    flat = jnp.concatenate([a.reshape(-1) for a in leaves])
    total = int(flat.size)
    chunk = N_DEV * 16 * LANES
    padded = -(-total // chunk) * chunk
    parts = jnp.pad(flat, (0, padded - total)).reshape(N_DEV, padded // (N_DEV * LANES), LANES)
    (scatter,), _ = exchange_start([(parts, True)], name + "_scatter")
    landed = exchange_wait(scatter, parts, name + "_scatter_wait")
    mine = rows_call(lambda g: jnp.sum(g, axis=0, keepdims=True), [landed], [(1, LANES, F32)], name + "_sum")[0][0]
    (gather,), _ = exchange_start([(mine, False)], name + "_gather")
    summed = exchange_wait(gather, mine, name + "_gather_wait").reshape(-1)
    out, at = [], 0
    for a, n in zip(leaves, sizes):
        out.append(summed[at:at + n].reshape(a.shape))
        at += n
    return out


def adamw(w, g, m, v, name):
    c = w.shape[-1] if w.ndim else 1

    def fn(w_, g_, m_, v_):
        nm = ADAM_B1 * m_ + (1.0 - ADAM_B1) * g_
        nv = ADAM_B2 * v_ + (1.0 - ADAM_B2) * (g_ * g_)
        m_hat = nm / (1.0 - ADAM_B1 ** ADAM_STEP)
        v_hat = nv / (1.0 - ADAM_B2 ** ADAM_STEP)
        delta = -ADAM_LR * (m_hat / (jnp.sqrt(v_hat) + ADAM_EPS) + ADAM_WD * w_)
        return delta, nm, nv

    res = rows_call(fn, [_as_rows(t) for t in (w, g.astype(F32), m, v)], [(1, c, F32)] * 3, name)
    return tuple(r.reshape(w.shape) for r in res)


def _rowsum(v):
    return jnp.sum(v, axis=0, keepdims=True)


def _norm_mod(x, g, sh, sc):
    n = x * lax.rsqrt(jnp.mean(x * x, axis=-1, keepdims=True) + EPS)
    return (n * g) * (1.0 + sc) + sh


def _norm_mod_bwd(x, g, sc, dh, dres):
    r = lax.rsqrt(jnp.mean(x * x, axis=-1, keepdims=True) + EPS)
    n = x * r
    dy = dh * (1.0 + sc)
    dn = dy * g
    dx = r * (dn - n * jnp.mean(dn * n, axis=-1, keepdims=True))
    return dres + dx, _rowsum(dh), _rowsum(dh * (n * g)), _rowsum(dy * n)


def _head_mean(v):
    low = lax.broadcasted_iota(jnp.int32, (1, LANES), 1) < HEAD_DIM
    parts = []
    for p in range(v.shape[1] // LANES):
        blk = v[:, p * LANES:(p + 1) * LANES]
        s0 = jnp.sum(jnp.where(low, blk, 0.0), axis=-1, keepdims=True)
        s1 = jnp.sum(jnp.where(low, 0.0, blk), axis=-1, keepdims=True)
        parts.append(jnp.where(low, s0, s1))
    return jnp.concatenate(parts, axis=1) * (1.0 / HEAD_DIM)


def _head_norm(x, g):
    return x * lax.rsqrt(_head_mean(x * x) + EPS) * g


def _head_norm_bwd(x, g, dy):
    r = lax.rsqrt(_head_mean(x * x) + EPS)
    n = x * r
    dn = dy * g
    return r * (dn - n * _head_mean(dn * n)), _rowsum(dy * n)


GELU_C = math.sqrt(2.0 / math.pi)
GELU_A = 0.044715


def _gelu_grad(y):
    t = jnp.tanh(GELU_C * (y + GELU_A * y * y * y))
    return 0.5 * (1.0 + t) + 0.5 * y * (1.0 - t * t) * GELU_C * (1.0 + 3.0 * GELU_A * y * y)


def ada_fwd(c_all, w_cols, b_cols):
    def body(c_ref, w_ref, b_ref, o_ref):
        c = c_ref[...]
        s = (c * jax.nn.sigmoid(c)).astype(BF16)
        o_ref[...] = jnp.dot(s, w_ref[...].astype(BF16), preferred_element_type=F32) + b_ref[...]

    return pl.pallas_call(
        body, name="ada_fwd", out_shape=jax.ShapeDtypeStruct((c_all.shape[0], w_cols.shape[1]), F32),
        compiler_params=pltpu.CompilerParams(vmem_limit_bytes=VMEM_LIMIT_BYTES),
    )(c_all, w_cols, b_cols)


def ada_bwd(c_all, dm_cols, dm_all):
    def body(c_ref, d_ref, all_ref, dw_ref, db_ref):
        c = c_ref[...]
        s = (c * jax.nn.sigmoid(c)).astype(BF16)
        dw_ref[...] = lax.dot_general(s, d_ref[...].astype(BF16), (((0,), (0,)), ((), ())),
                                      preferred_element_type=F32)
        db_ref[...] = jnp.sum(all_ref[...], axis=0, keepdims=True)

    return pl.pallas_call(
        body, name="ada_bwd",
        out_shape=[jax.ShapeDtypeStruct((c_all.shape[1], dm_cols.shape[1]), F32),
                   jax.ShapeDtypeStruct((1, dm_all.shape[1]), F32)],
        compiler_params=pltpu.CompilerParams(vmem_limit_bytes=VMEM_LIMIT_BYTES),
    )(c_all, dm_cols, dm_all)


def _s5_discretise(lam_re, lam_im, log_dt, b_re, b_im):
    dt = jnp.exp(log_dt)
    mag = jnp.exp(lam_re * dt)
    ab_re = mag * jnp.cos(lam_im * dt)
    ab_im = mag * jnp.sin(lam_im * dt)
    den = lam_re * lam_re + lam_im * lam_im
    nr = ab_re - 1.0
    ni = ab_im
    f_re = (nr * lam_re + ni * lam_im) / den
    f_im = (ni * lam_re - nr * lam_im) / den
    bb_re = f_re * b_re - f_im * b_im
    bb_im = f_re * b_im + f_im * b_re
    return ab_re, ab_im, bb_re, bb_im


def s5_prep(lam_re, lam_im, log_dt, b_re, b_im):
    gp, h = b_re.shape

    def body(lr, li, ld, br, bi, o_ar, o_ai, o_br, o_bi):
        res = _s5_discretise(lr[...], li[...], ld[...], br[...], bi[...])
        for r, v in zip((o_ar, o_ai, o_br, o_bi), res):
            r[...] = v

    col, mat = jax.ShapeDtypeStruct((gp, 1), F32), jax.ShapeDtypeStruct((gp, h), F32)
    return pl.pallas_call(body, name="s5_prep", out_shape=[col, col, mat, mat])(lam_re, lam_im, log_dt, b_re, b_im)


def s5_prep_bwd(lam_re, lam_im, log_dt, b_re, b_im, d_ab_re, d_ab_im, d_bb_re, d_bb_im):
    gp, h = b_re.shape

    def body(lr, li, ld, br, bi, g_ar, g_ai, g_br, g_bi, o_lr, o_li, o_ld, o_br, o_bi):
        _, vjp = jax.vjp(_s5_discretise, lr[...], li[...], ld[...], br[...], bi[...])
        res = vjp((g_ar[...], g_ai[...], g_br[...], g_bi[...]))
        for r, v in zip((o_lr, o_li, o_ld, o_br, o_bi), res):
            r[...] = v

    col, mat = jax.ShapeDtypeStruct((gp, 1), F32), jax.ShapeDtypeStruct((gp, h), F32)
    return pl.pallas_call(body, name="s5_prep_bwd", out_shape=[col, col, col, mat, mat])(
        lam_re, lam_im, log_dt, b_re, b_im, d_ab_re, d_ab_im, d_bb_re, d_bb_im)


def _s5_chunk(seq):
    return _divisor_tile(seq, 256, 16)


def s5_fwd(u, bbd_re, bbd_im, cbd_re, cbd_im, ab_re, ab_im, dskip):
    bsz, seq, d = u.shape
    nb, cb, ns = bbd_re.shape
    lc = _s5_chunk(seq)

    def body(u_ref, bre_ref, bim_ref, cre_ref, cim_ref, ar_ref, ai_ref, d_ref, y_ref, ge_ref, sre_ref, sim_ref,
             carry_re, carry_im):
        t = pl.program_id(1)

        @pl.when(t == 0)
        def _():
            carry_re[...] = jnp.zeros_like(carry_re)
            carry_im[...] = jnp.zeros_like(carry_im)

        for b in range(bsz):
            ub = u_ref[b].astype(BF16)
            sre_ref[b] = jnp.dot(ub, bre_ref[...].astype(BF16), preferred_element_type=F32)
            sim_ref[b] = jnp.dot(ub, bim_ref[...].astype(BF16), preferred_element_type=F32)
        ar, ai = ar_ref[...], ai_ref[...]

        def step(i, carry):
            row = pl.ds(i, 1)
            out = []
            for b, (cr, ci) in enumerate(carry):
                nr = ar * cr - ai * ci + sre_ref[b, row, :]
                ni = ar * ci + ai * cr + sim_ref[b, row, :]
                sre_ref[b, row, :] = nr
                sim_ref[b, row, :] = ni
                out.append((nr, ni))
            return tuple(out)

        init = tuple((carry_re[b], carry_im[b]) for b in range(bsz))
        last = lax.fori_loop(0, lc, step, init, unroll=8)
        for b, (cr, ci) in enumerate(last):
            carry_re[b] = cr
            carry_im[b] = ci
            y = jnp.dot(sre_ref[b].astype(BF16), cre_ref[...].astype(BF16), preferred_element_type=F32)
            y -= jnp.dot(sim_ref[b].astype(BF16), cim_ref[...].astype(BF16), preferred_element_type=F32)
            y = y + d_ref[...] * u_ref[b]
            y_ref[b] = y
            ge_ref[b] = jax.nn.gelu(y).astype(BF16)

    chan = pl.BlockSpec((bsz, lc, cb), lambda n, t: (0, t, n))
    state = pl.BlockSpec((bsz, lc, ns), lambda n, t: (0, t, n))
    par = lambda r, c: pl.BlockSpec((None, r, c), lambda n, t: (n, 0, 0))
    return pl.pallas_call(
        body, name="s5_fwd", grid=(nb, seq // lc),
        in_specs=[chan, par(cb, ns), par(cb, ns), par(ns, cb), par(ns, cb), par(1, ns), par(1, ns),
                  pl.BlockSpec((None, 1, cb), lambda n, t: (0, 0, n))],
        out_specs=[chan, chan, state, state],
        out_shape=[jax.ShapeDtypeStruct((bsz, seq, d), F32), jax.ShapeDtypeStruct((bsz, seq, d), BF16),
                   jax.ShapeDtypeStruct((bsz, seq, nb * ns), F32),
                   jax.ShapeDtypeStruct((bsz, seq, nb * ns), F32)],
        scratch_shapes=[pltpu.VMEM((bsz, 1, ns), F32), pltpu.VMEM((bsz, 1, ns), F32)],
        compiler_params=_cparams("arbitrary", "arbitrary"),
    )(u, bbd_re, bbd_im, cbd_re, cbd_im, ab_re, ab_im, dskip)


def s5_bwd(dy, u, st_re, st_im, bbd_re, bbd_im, cbd_re, cbd_im, ab_re, ab_im, dskip):
    bsz, seq, d = u.shape
    nb, cb, ns = bbd_re.shape
    lc = _s5_chunk(seq)
    nc = seq // lc

    def body(dy_ref, u_ref, sre_ref, sim_ref, bre_ref, bim_ref, cre_ref, cim_ref, ar_ref, ai_ref, d_ref,
             du_ref, dbre_out, dbim_out, dcre_out, dcim_out, dar_ref, dai_ref, dd_ref,
             g_re, g_im, gs_re, gs_im, carry_re, carry_im, dbre_ref, dbim_ref, dcre_ref, dcim_ref):
        t = pl.program_id(1)

        @pl.when(t == 0)
        def _():
            for r in (dbre_ref, dbim_ref, dcre_ref, dcim_ref, dar_ref, dai_ref, dd_ref, carry_re, carry_im):
                r[...] = jnp.zeros_like(r)

        nt = (((1,), (1,)), ((), ()))
        tn = (((0,), (0,)), ((), ()))
        for b in range(bsz):
            dyb = dy_ref[b].astype(BF16)
            g_re[b] = lax.dot_general(dyb, cre_ref[...].astype(BF16), nt, preferred_element_type=F32)
            g_im[b] = -lax.dot_general(dyb, cim_ref[...].astype(BF16), nt, preferred_element_type=F32)
        ar, ai = ar_ref[...], ai_ref[...]

        def step(k, carry):
            row = pl.ds(lc - 1 - k, 1)
            out = []
            for b, (cr, ci) in enumerate(carry):
                gs_re[b, row, :] = cr
                gs_im[b, row, :] = ci
                nr = ar * cr + ai * ci + g_re[b, row, :]
                ni = ar * ci - ai * cr + g_im[b, row, :]
                g_re[b, row, :] = nr
                g_im[b, row, :] = ni
                out.append((nr, ni))
            return tuple(out)

        init = tuple((carry_re[b], carry_im[b]) for b in range(bsz))
        last = lax.fori_loop(0, lc, step, init, unroll=8)
        for b, (cr, ci) in enumerate(last):
            carry_re[b] = cr
            carry_im[b] = ci
            dyf, uf = dy_ref[b], u_ref[b]
            dyb, ub = dyf.astype(BF16), uf.astype(BF16)
            sr, si = sre_ref[b], sim_ref[b]
            hr, hi = gs_re[b], gs_im[b]
            dar_ref[...] += _rowsum(hr * sr + hi * si)
            dai_ref[...] += _rowsum(hi * sr - hr * si)
            gr, gi = g_re[b].astype(BF16), g_im[b].astype(BF16)
            du = lax.dot_general(gr, bre_ref[...].astype(BF16), nt, preferred_element_type=F32)
            du += lax.dot_general(gi, bim_ref[...].astype(BF16), nt, preferred_element_type=F32)
            du_ref[b] = du + d_ref[...] * dyf
            dbre_ref[...] += lax.dot_general(ub, gr, tn, preferred_element_type=F32)
            dbim_ref[...] += lax.dot_general(ub, gi, tn, preferred_element_type=F32)
            dcre_ref[...] += lax.dot_general(sr.astype(BF16), dyb, tn, preferred_element_type=F32)
            dcim_ref[...] -= lax.dot_general(si.astype(BF16), dyb, tn, preferred_element_type=F32)
            dd_ref[...] += _rowsum(dyf * uf)

        @pl.when(t == nc - 1)
        def _():
            for k in range(cb // S5_GROUP):
                chans = slice(k * S5_GROUP, (k + 1) * S5_GROUP)
                states = slice(k * S5_STATE, (k + 1) * S5_STATE)
                dbre_out[chans, :] = dbre_ref[chans, states]
                dbim_out[chans, :] = dbim_ref[chans, states]
                dcre_out[states, :] = dcre_ref[states, chans]
                dcim_out[states, :] = dcim_ref[states, chans]

    chan = pl.BlockSpec((bsz, lc, cb), lambda n, t: (0, nc - 1 - t, n))
    state = pl.BlockSpec((bsz, lc, ns), lambda n, t: (0, nc - 1 - t, n))
    par = lambda r, c: pl.BlockSpec((None, r, c), lambda n, t: (n, 0, 0))
    return pl.pallas_call(
        body, name="s5_bwd", grid=(nb, nc),
        in_specs=[chan, chan, state, state, par(cb, ns), par(cb, ns), par(ns, cb), par(ns, cb),
                  par(1, ns), par(1, ns), pl.BlockSpec((None, 1, cb), lambda n, t: (0, 0, n))],
        out_specs=[chan, par(cb, S5_STATE), par(cb, S5_STATE), par(ns, S5_GROUP), par(ns, S5_GROUP),
                   par(1, ns), par(1, ns), par(1, cb)],
        out_shape=[jax.ShapeDtypeStruct((bsz, seq, d), F32),
                   jax.ShapeDtypeStruct((nb, cb, S5_STATE), F32), jax.ShapeDtypeStruct((nb, cb, S5_STATE), F32),
                   jax.ShapeDtypeStruct((nb, ns, S5_GROUP), F32), jax.ShapeDtypeStruct((nb, ns, S5_GROUP), F32),
                   jax.ShapeDtypeStruct((nb, 1, ns), F32), jax.ShapeDtypeStruct((nb, 1, ns), F32),
                   jax.ShapeDtypeStruct((nb, 1, cb), F32)],
        scratch_shapes=([pltpu.VMEM((bsz, lc, ns), F32)] * 4 + [pltpu.VMEM((bsz, 1, ns), F32)] * 2
                        + [pltpu.VMEM((cb, ns), F32)] * 2 + [pltpu.VMEM((ns, cb), F32)] * 2),
        compiler_params=_cparams("arbitrary", "arbitrary"),
    )(dy, u, st_re, st_im, bbd_re, bbd_im, cbd_re, cbd_im, ab_re, ab_im, dskip)


ATT_HEADS = 8
ATT_HEADS_FWD = 8
ATT_LANES = ATT_HEADS * HEAD_DIM
ATT_KEYS = 2 * ATT_BLOCK
ATT_Q = 256
ATT_SCALE = 1.0 / math.sqrt(HEAD_DIM)
_NT = (((1,), (1,)), ((), ()))
_TN = (((0,), (0,)), ((), ()))
_HEADS = [slice(h * HEAD_DIM, (h + 1) * HEAD_DIM) for h in range(ATT_HEADS)]
_HALF = [slice(0, ATT_BLOCK), slice(ATT_BLOCK, ATT_KEYS)]


def _log_sigmoids(z):
    sp = jnp.log(1.0 + jnp.exp(-jnp.abs(z)))
    ls = jnp.minimum(z, 0.0) - sp
    return ls, ls - z


def _sum_matrix(after, inclusive):
    j = lax.broadcasted_iota(jnp.int32, (ATT_KEYS, ATT_KEYS), 0) % ATT_BLOCK
    s = lax.broadcasted_iota(jnp.int32, (ATT_KEYS, ATT_KEYS), 1)
    if after:
        hit = (j >= s) if inclusive else (j > s)
    else:
        hit = (j <= s) if inclusive else (j < s)
    return jnp.where(jnp.logical_or(hit, s >= ATT_BLOCK), 1.0, 0.0).astype(BF16)


def _hi_lo(v):
    hi = v.astype(BF16)
    lo = (v - hi.astype(F32)).astype(BF16)
    return jnp.concatenate([hi, lo], axis=1)


def _strict_mask(i, j):
    t = i * ATT_Q + lax.broadcasted_iota(jnp.int32, (ATT_Q, ATT_KEYS), 0)
    s = j * ATT_KEYS + lax.broadcasted_iota(jnp.int32, (ATT_Q, ATT_KEYS), 1)
    return s < t


def attention_fwd(q, k, v):
    bsz, seq, d = q.shape
    n_heads = ATT_HEADS_FWD if d % (ATT_HEADS_FWD * HEAD_DIM) == 0 else ATT_HEADS
    lanes = n_heads * HEAD_DIM
    heads = [slice(h * HEAD_DIM, (h + 1) * HEAD_DIM) for h in range(n_heads)]

    def body(q_ref, k_ref, v_ref, o_ref, tot_ref, z_buf, ls_buf, cs_buf, acc_buf, run_buf):
        i = pl.program_id(2)
        jd = ((i + 1) * ATT_Q - 1) // ATT_KEYS
        sums = _sum_matrix(True, False)
        acc_buf[...] = jnp.zeros_like(acc_buf)
        run_buf[...] = jnp.zeros_like(run_buf)

        def block(j, masked):
            rows = pl.ds(pl.multiple_of(j * ATT_KEYS, ATT_KEYS), ATT_KEYS)
            strict = _strict_mask(i, j) if masked else None
            for h, ln in enumerate(heads):
                z_buf[h] = lax.dot_general(q_ref[:, ln], k_ref[rows, ln], _NT, preferred_element_type=F32)
            for h in range(n_heads):
                for half, cols in enumerate(_HALF):
                    ls, lf = _log_sigmoids(z_buf[h, :, cols])
                    if masked:
                        lf = jnp.where(strict[:, cols], lf, 0.0)
                    ls_buf[h, :, cols] = ls
                    cs_buf[h, half] = jnp.dot(_hi_lo(lf), sums, preferred_element_type=F32)
            for h, ln in enumerate(heads):
                run = run_buf[h]
                late, early = cs_buf[h, 1], cs_buf[h, 0]
                a1 = run + late[:, _HALF[0]]
                run = run + late[:, _HALF[1]]
                a0 = run + early[:, _HALF[0]]
                run_buf[h] = run + early[:, _HALF[1]]
                w = jnp.exp(ls_buf[h] + jnp.concatenate([a0, a1], axis=1))
                if masked:
                    w = jnp.where(strict, w, 0.0)
                acc_buf[h] += jnp.dot(w.astype(BF16), v_ref[rows, ln], preferred_element_type=F32)

        block(jd, True)

        def step(it, carry):
            block(jd - 1 - it, False)
            return carry

        lax.fori_loop(0, jd, step, 0)
        o_ref[...] = jnp.concatenate([acc_buf[h] for h in range(n_heads)], axis=1).astype(o_ref.dtype)
        tot_ref[...] = jnp.concatenate([run_buf[h, :, :HEAD_DIM] for h in range(n_heads)], axis=1)

    blk = pl.BlockSpec((None, ATT_Q, lanes), lambda b, p, i: (b, i, p))
    full = pl.BlockSpec((None, seq, lanes), lambda b, p, i: (b, 0, p))
    tile = (n_heads, ATT_Q, ATT_KEYS)
    return pl.pallas_call(
        body, name="attention_fwd", grid=(bsz, d // lanes, seq // ATT_Q),
        in_specs=[blk, full, full], out_specs=[blk, blk],
        out_shape=[jax.ShapeDtypeStruct((bsz, seq, d), BF16), jax.ShapeDtypeStruct((bsz, seq, d), F32)],
        scratch_shapes=[pltpu.VMEM(tile, F32), pltpu.VMEM(tile, F32),
                        pltpu.VMEM((n_heads, 2, ATT_Q, ATT_KEYS), F32),
                        pltpu.VMEM((n_heads, ATT_Q, HEAD_DIM), F32),
                        pltpu.VMEM((n_heads, ATT_Q, ATT_BLOCK), F32)],
        compiler_params=_cparams("arbitrary", "arbitrary", "arbitrary"),
    )(q, k, v)


def attention_bwd(q, k, v, tot, do):
    bsz, seq, d = q.shape

    def body(q_ref, k_ref, v_ref, tot_ref, do_ref, dq_ref, dk_ref, dv_ref,
             z_buf, dw_buf, ls_buf, e_buf, up_buf, bf_buf, w_buf, do_buf, dq_buf, tot_buf, run_buf, erun_buf):
        i = pl.program_id(2)
        jd = ((i + 1) * ATT_Q - 1) // ATT_KEYS

        @pl.when(i == 0)
        def _():
            dk_ref[...] = jnp.zeros_like(dk_ref)
            dv_ref[...] = jnp.zeros_like(dv_ref)

        upto_incl, upto_excl = _sum_matrix(False, True), _sum_matrix(False, False)
        do_buf[...] = do_ref[...].astype(BF16)
        for h, ln in enumerate(_HEADS):
            tot_buf[h] = jnp.concatenate([tot_ref[:, ln], tot_ref[:, ln]], axis=1)
        dq_buf[...] = jnp.zeros_like(dq_buf)
        run_buf[...] = jnp.zeros_like(run_buf)
        erun_buf[...] = jnp.zeros_like(erun_buf)

        def block(j, masked):
            rows = pl.ds(pl.multiple_of(j * ATT_KEYS, ATT_KEYS), ATT_KEYS)
            strict = _strict_mask(i, j) if masked else None
            for h, ln in enumerate(_HEADS):
                z_buf[h] = lax.dot_general(q_ref[:, ln], k_ref[rows, ln], _NT, preferred_element_type=F32)
                dw_buf[h] = lax.dot_general(do_buf[:, ln], v_ref[rows, ln], _NT, preferred_element_type=F32)
            for h in range(ATT_HEADS):
                for half, cols in enumerate(_HALF):
                    ls, lf = _log_sigmoids(z_buf[h, :, cols])
                    if masked:
                        lf = jnp.where(strict[:, cols], lf, 0.0)
                    ls_buf[h, :, cols] = ls
                    up_buf[h, half] = jnp.dot(_hi_lo(lf), upto_incl, preferred_element_type=F32)
            for h in range(ATT_HEADS):
                run = run_buf[h]
                early, late = up_buf[h, 0], up_buf[h, 1]
                u0 = run + early[:, _HALF[0]]
                run = run + early[:, _HALF[1]]
                u1 = run + late[:, _HALF[0]]
                run_buf[h] = run + late[:, _HALF[1]]
                tot_h = tot_buf[h]
                after = jnp.concatenate([tot_h - u0, tot_h - u1], axis=1)
                w = jnp.exp(ls_buf[h] + after)
                if masked:
                    w = jnp.where(strict, w, 0.0)
                w_buf[h] = w.astype(BF16)
                e = dw_buf[h] * w
                e_buf[h] = e
                for half, cols in enumerate(_HALF):
                    bf_buf[h, half] = jnp.dot(_hi_lo(e[:, cols]), upto_excl, preferred_element_type=F32)
            dks, dvs = [], []
            for h, ln in enumerate(_HEADS):
                erun = erun_buf[h]
                early, late = bf_buf[h, 0], bf_buf[h, 1]
                b0 = erun + early[:, _HALF[0]]
                erun = erun + early[:, _HALF[1]]
                b1 = erun + late[:, _HALF[0]]
                erun_buf[h] = erun + late[:, _HALF[1]]
                e = e_buf[h]
                dz = e - jnp.exp(ls_buf[h]) * (e + jnp.concatenate([b0, b1], axis=1))
                if masked:
                    dz = jnp.where(strict, dz, 0.0)
                dz = dz.astype(BF16)
                dq_buf[h] += jnp.dot(dz, k_ref[rows, ln], preferred_element_type=F32)
                dks.append(lax.dot_general(dz, q_ref[:, ln], _TN, preferred_element_type=F32))
                dvs.append(lax.dot_general(w_buf[h], do_buf[:, ln], _TN, preferred_element_type=F32))
            dk_ref[rows, :] += jnp.concatenate(dks, axis=1)
            dv_ref[rows, :] += jnp.concatenate(dvs, axis=1)

        def step(j, carry):
            block(j, False)
            return carry

        lax.fori_loop(0, jd, step, 0)
        block(jd, True)
        dq_ref[...] = jnp.concatenate([dq_buf[h] for h in range(ATT_HEADS)], axis=1) * ATT_SCALE

    blk = pl.BlockSpec((None, ATT_Q, ATT_LANES), lambda b, p, i: (b, i, p))
    full = pl.BlockSpec((None, seq, ATT_LANES), lambda b, p, i: (b, 0, p), pipeline_mode=pl.Buffered(1))
    shape = jax.ShapeDtypeStruct((bsz, seq, d), F32)
    tile = (ATT_HEADS, ATT_Q, ATT_KEYS)
    pair = (ATT_HEADS, 2, ATT_Q, ATT_KEYS)
    square = (ATT_HEADS, ATT_Q, ATT_BLOCK)
    return pl.pallas_call(
        body, name="attention_bwd", grid=(bsz, d // ATT_LANES, seq // ATT_Q),
        in_specs=[blk, full, full, blk, blk], out_specs=[blk, full, full], out_shape=[shape, shape, shape],
        scratch_shapes=[pltpu.VMEM(tile, F32), pltpu.VMEM(tile, F32), pltpu.VMEM(tile, F32), pltpu.VMEM(tile, F32),
                        pltpu.VMEM(pair, F32), pltpu.VMEM(pair, F32), pltpu.VMEM(tile, BF16),
                        pltpu.VMEM((ATT_Q, ATT_LANES), BF16), pltpu.VMEM((ATT_HEADS, ATT_Q, HEAD_DIM), F32),
                        pltpu.VMEM(square, F32), pltpu.VMEM(square, F32), pltpu.VMEM(square, F32)],
        compiler_params=_cparams("arbitrary", "arbitrary", "arbitrary"),
    )(q, k, v, tot, do)


def _tile_fused(seq):
    return _divisor_tile(seq, 1024, 16)


MLP_SLABS = 1


def mlp_core_fwd(x, norm_g, sh, sc, gate, w1, w2, name):
    bsz, seq, d = x.shape
    ns, _, fs = w1.shape
    t = bsz * seq
    tm = _tile_fused(seq)
    g = MLP_SLABS if ns % MLP_SLABS == 0 else 1
    steps = ns // g

    def body(x_ref, w1_ref, w2_ref, ng_ref, sh_ref, sc_ref, g_ref, h_ref, act_ref, ff_ref, out_ref, acc_ref, h_buf):
        s = pl.program_id(1)

        @pl.when(s == 0)
        def _():
            h = _norm_mod(x_ref[...], ng_ref[...], sh_ref[...], sc_ref[...]).astype(BF16)
            h_buf[...] = h
            h_ref[...] = h

        hb = h_buf[...]
        part = None
        for k in range(g):
            pre = jnp.dot(hb, w1_ref[k], preferred_element_type=F32)
            act = jnp.square(jnp.maximum(pre, 0.0)).astype(BF16)
            act_ref[:, k * fs:(k + 1) * fs] = act
            p = jnp.dot(act, w2_ref[k], preferred_element_type=F32)
            part = p if part is None else part + p

        @pl.when(s == 0)
        def _():
            acc_ref[...] = part

        @pl.when(s > 0)
        def _():
            acc_ref[...] += part

        @pl.when(s == steps - 1)
        def _():
            ff = acc_ref[...]
            ff_ref[...] = ff
            out_ref[...] = x_ref[...] + g_ref[...] * ff

    rows = pl.BlockSpec((tm, d), lambda i, s: (i, 0))
    per_seq = pl.BlockSpec((None, 1, d), lambda i, s: ((i * tm) // seq, 0, 0))
    h, act, ff, out = pl.pallas_call(
        body, name=name, grid=(t // tm, steps),
        in_specs=[rows, pl.BlockSpec((g, d, fs), lambda i, s: (s, 0, 0)),
                  pl.BlockSpec((g, fs, d), lambda i, s: (s, 0, 0)),
                  pl.BlockSpec((None, 1, d), lambda i, s: (0, 0, 0)), per_seq, per_seq, per_seq],
        out_specs=[rows, pl.BlockSpec((tm, g * fs), lambda i, s: (i, s)), rows, rows],
        out_shape=[jax.ShapeDtypeStruct((t, d), BF16), jax.ShapeDtypeStruct((t, ns * fs), BF16),
                   jax.ShapeDtypeStruct((t, d), F32), jax.ShapeDtypeStruct((t, d), F32)],
        scratch_shapes=[pltpu.VMEM((tm, d), F32), pltpu.VMEM((tm, d), BF16)],
        compiler_params=_cparams("arbitrary", "arbitrary"),
    )(x.reshape(t, d), w1, w2, norm_g, sh, sc, gate)
    return h, act, ff.reshape(bsz, seq, d), out.reshape(bsz, seq, d)


def mlp_core_bwd(dff, act, w1, w2, name):
    t, d = dff.shape
    ns, _, fs = w1.shape
    tm = _tile_fused(t)
    g = MLP_SLABS if ns % MLP_SLABS == 0 else 1
    steps = ns // g

    def body(dff_ref, act_ref, w1_ref, w2_ref, dpre_ref, dh_ref, acc_ref):
        s = pl.program_id(1)
        db = dff_ref[...]
        part = None
        for k in range(g):
            cols = slice(k * fs, (k + 1) * fs)
            dact = lax.dot_general(db, w2_ref[k], _NT, preferred_element_type=F32)
            dpre = (dact * (2.0 * jnp.sqrt(act_ref[:, cols].astype(F32)))).astype(BF16)
            dpre_ref[:, cols] = dpre
            p = lax.dot_general(dpre, w1_ref[k], _NT, preferred_element_type=F32)
            part = p if part is None else part + p

        @pl.when(s == 0)
        def _():
            acc_ref[...] = part

        @pl.when(s > 0)
        def _():
            acc_ref[...] += part

        @pl.when(s == steps - 1)
        def _():
            dh_ref[...] = acc_ref[...]

    rows = pl.BlockSpec((tm, d), lambda i, s: (i, 0))
    slab = pl.BlockSpec((tm, g * fs), lambda i, s: (i, s))
    return pl.pallas_call(
        body, name=name, grid=(t // tm, steps),
        in_specs=[rows, slab, pl.BlockSpec((g, d, fs), lambda i, s: (s, 0, 0)),
                  pl.BlockSpec((g, fs, d), lambda i, s: (s, 0, 0))],
        out_specs=[slab, rows],
        out_shape=[jax.ShapeDtypeStruct((t, ns * fs), BF16), jax.ShapeDtypeStruct((t, d), F32)],
        scratch_shapes=[pltpu.VMEM((tm, d), F32)],
        compiler_params=_cparams("arbitrary", "arbitrary"),
    )(dff, act, w1, w2)


def mlp_fwd(x, g, sh, sc, gate, w1_handle, w2_handle, tag):
    w1 = exchange_wait(w1_handle, x, tag + "_w1_wait")
    w2 = exchange_wait(w2_handle, x, tag + "_w2_wait")
    h, act, ff, out = mlp_core_fwd(x, g, sh, sc, gate, w1, w2, tag + "_core")
    return out, (h, act, ff), w1, w2


def glu_fwd(ge, w, x, gate, name):
    bsz, seq, d = x.shape
    ns, _, nb = w.shape
    half = ns // 2
    t = bsz * seq
    tm = _tile_fused(seq)

    def body(a_ref, wv_ref, wg_ref, x_ref, g_ref, val_ref, gt_ref, out_ref):
        a = a_ref[...]
        val = jnp.dot(a, wv_ref[...], preferred_element_type=F32)
        gt = jnp.dot(a, wg_ref[...], preferred_element_type=F32)
        val_ref[...] = val
        gt_ref[...] = gt
        out_ref[...] = x_ref[...] + g_ref[...] * (val * jax.nn.sigmoid(gt))

    cols = pl.BlockSpec((tm, nb), lambda i, j: (i, j))
    res = pl.pallas_call(
        body, name=name, grid=(t // tm, half),
        in_specs=[pl.BlockSpec((tm, d), lambda i, j: (i, 0)),
                  pl.BlockSpec((None, d, nb), lambda i, j: (j, 0, 0)),
                  pl.BlockSpec((None, d, nb), lambda i, j: (half + j, 0, 0)), cols,
                  pl.BlockSpec((None, 1, nb), lambda i, j: ((i * tm) // seq, 0, j))],
        out_specs=[cols, cols, cols],
        out_shape=[jax.ShapeDtypeStruct((t, d), F32)] * 3,
        compiler_params=_cparams("arbitrary", "arbitrary"),
    )(ge.reshape(t, d), w, w, x.reshape(t, d), gate)
    return tuple(r.reshape(bsz, seq, d) for r in res)


def _gate_bwd(dx, f, gate):
    return gate * dx, _rowsum(dx * f)


def _gate_bwd_outs(d):
    return [(d, BF16, "tile"), (d, F32, "seq")]


def _norm_bwd_outs(d):
    return [(d, F32, "tile"), (d, F32, "seq"), (d, F32, "seq"), (d, F32, "all")]


def _norm_then_gate_bwd(x, g, sc, dh, dres, f, gate):
    res = _norm_mod_bwd(x, g, sc, dh, dres)
    return (*res, *_gate_bwd(res[0], f, gate))


def mlp_bwd(dout, dff, x, g, sc, w1, w2, saved, tag, branch=None):
    bsz, seq, d = x.shape
    t = bsz * seq
    ns = w1.shape[0]
    h, act, _ = saved
    dff = dff.reshape(t, d)
    dpre, dh = mlp_core_bwd(dff, act, w1, w2, tag + "_dcore")
    dw2 = mm_tn(act, dff, "a", ns, tag + "_dw2")
    dw1 = mm_tn(h.reshape(t, d), dpre, "c", ns, tag + "_dw1")
    (dw1_handle, dw2_handle), token = exchange_start([(dw1, True), (dw2, True)], tag + "_dw_start")
    ins = [x, g + token[0, 0], sc, dh.reshape(bsz, seq, d), dout]
    if branch is None:
        dx, dsh, dsc, dg = act_call(_norm_mod_bwd, ins, _norm_bwd_outs(d), tag + "_dnorm")
        into_branch = None
    else:
        dx, dsh, dsc, dg, *into_branch = act_call(_norm_then_gate_bwd, ins + list(branch),
                                                  _norm_bwd_outs(d) + _gate_bwd_outs(d), tag + "_dnorm")
    return dx, dw1_handle, dw2_handle, (dsh, dsc, dg), into_branch


def _block_diag(m):
    _, rows, c = m.shape
    k = S5_BLOCK_GROUPS
    row_group = lax.broadcasted_iota(jnp.int32, (rows, k * c), 0) // (rows // k)
    col_group = lax.broadcasted_iota(jnp.int32, (rows, k * c), 1) // c
    return jnp.where(row_group == col_group, jnp.tile(m, (1, 1, k)), 0.0)


def kernel(x, c, ada_w, ada_b, mix_norm_g, mlp_norm_g, mlp_w1, mlp_w2, s5_a_re, s5_a_im, s5_log_dt, s5_b_re, s5_b_im, s5_c_re, s5_c_im, s5_d, s5_w_glu, kv_ada_w, kv_ada_b, kv_norm_g, w_kv, k_norm_g, sb_w_q, q_norm_g, sb_w_o, loss_target, m_ada_w, m_ada_b, m_mix_norm_g, m_mlp_norm_g, m_mlp_w1, m_mlp_w2, m_s5_a_re, m_s5_a_im, m_s5_log_dt, m_s5_b_re, m_s5_b_im, m_s5_c_re, m_s5_c_im, m_s5_d, m_s5_w_glu, m_kv_ada_w, m_kv_ada_b, m_kv_norm_g, m_w_kv, m_k_norm_g, m_sb_w_q, m_q_norm_g, m_sb_w_o, v_ada_w, v_ada_b, v_mix_norm_g, v_mlp_norm_g, v_mlp_w1, v_mlp_w2, v_s5_a_re, v_s5_a_im, v_s5_log_dt, v_s5_b_re, v_s5_b_im, v_s5_c_re, v_s5_c_im, v_s5_d, v_s5_w_glu, v_kv_ada_w, v_kv_ada_b, v_kv_norm_g, v_w_kv, v_k_norm_g, v_sb_w_q, v_q_norm_g, v_sb_w_o):
    bsz, seq, d = x.shape
    t = bsz * seq
    n_groups = d // S5_GROUP
    nb = n_groups // S5_BLOCK_GROUPS
    gp = n_groups * S5_STATE
    dev = 4 * lax.axis_index("x") + 2 * lax.axis_index("y") + lax.axis_index("c")
    e_ada, e_kv = 6 * d, 2 * d
    n_ada, n_kv = e_ada // N_DEV, e_kv // N_DEV

    d_skip = all_gather(s5_d, "gather_skip").reshape(1, 1, d)
    c_all = all_gather(c, "gather_c").reshape(N_DEV * bsz, d)

    w_cols = jnp.concatenate([ada_w[0], ada_w[1], kv_ada_w], axis=1)
    b_cols = jnp.concatenate([
        lax.dynamic_slice_in_dim(ada_b[0], dev * n_ada, n_ada),
        lax.dynamic_slice_in_dim(ada_b[1], dev * n_ada, n_ada),
        lax.dynamic_slice_in_dim(kv_ada_b, dev * n_kv, n_kv)])[None, :]
    mod_cols = ada_fwd(c_all, w_cols, b_cols)
    mod_all = all_gather(mod_cols, "gather_mod")
    mod_mine = lax.dynamic_slice_in_dim(mod_all, dev * bsz, bsz, axis=1)
    mod_mine = jnp.transpose(mod_mine, (1, 0, 2))
    mods = []
    for i in range(2):
        full = mod_mine[:, :, i * n_ada:(i + 1) * n_ada].reshape(bsz, e_ada)
        mods.append([full[:, None, j * d:(j + 1) * d] for j in range(6)])
    kv_full = mod_mine[:, :, 2 * n_ada:].reshape(bsz, e_kv)
    kv_sh, kv_sc = kv_full[:, None, :d], kv_full[:, None, d:]

    par = lambda p: p.reshape(1, 1, -1)

    shards = [s5_w_glu[0], mlp_w1[0], mlp_w2[0], w_kv, sb_w_q[0], sb_w_o[0], mlp_w1[1], mlp_w2[1]]
    gathers, gather_token = exchange_start([(w.astype(BF16), False) for w in shards], "gather_start", after=[mod_all, d_skip])
    glu_handle, w1_0_handle, w2_0_handle, wkv_handle, wq_handle, wo_handle, w1_1_handle, w2_1_handle = gathers
    started = gather_token[0, 0]

    sh_a, sc_a, g_a, sh_m, sc_m, g_m = mods[0]
    lam_re, lam_im = s5_a_re.reshape(gp, 1), s5_a_im.reshape(gp, 1)
    log_dt = jnp.broadcast_to(s5_log_dt.reshape(n_groups, 1), (n_groups, S5_STATE)).reshape(gp, 1)
    b_re, b_im = s5_b_re.reshape(gp, S5_GROUP), s5_b_im.reshape(gp, S5_GROUP)
    ab_re, ab_im, bb_re, bb_im = s5_prep(lam_re, lam_im, log_dt, b_re, b_im)
    to_bbd = lambda m: _block_diag(
        jnp.swapaxes(m.reshape(nb, S5_BLOCK_GROUPS, S5_STATE, S5_GROUP), 2, 3).reshape(nb, -1, S5_STATE))
    to_cbd = lambda m: _block_diag(
        jnp.swapaxes(m.reshape(nb, S5_BLOCK_GROUPS, S5_GROUP, S5_STATE), 2, 3).reshape(nb, -1, S5_GROUP))
    bbd_re, bbd_im = to_bbd(bb_re), to_bbd(bb_im)
    cbd_re, cbd_im = to_cbd(s5_c_re[0]), to_cbd(s5_c_im[0])
    abr, abi = ab_re.reshape(nb, 1, -1), ab_im.reshape(nb, 1, -1)

    h0 = act_call(_norm_mod, [x, par(mix_norm_g[0]) + started, sh_a, sc_a], [(d, F32, "tile")], "mix0_norm")[0]
    y, ge, st_re, st_im = s5_fwd(h0, bbd_re, bbd_im, cbd_re, cbd_im, abr, abi, d_skip)
    w_glu = exchange_wait(glu_handle, ge, "glu_w_wait")
    z_val, z_gate, x1 = glu_fwd(ge, w_glu, x, g_a, "glu_up")
    x2, mlp0_saved, w1_0, w2_0 = mlp_fwd(x1, par(mlp_norm_g[0]), sh_m, sc_m, g_m, w1_0_handle, w2_0_handle, "mlp0")

    sh_a1, sc_a1, g_a1, sh_m1, sc_m1, g_m1 = mods[1]
    kg = par(jnp.tile(k_norm_g, d // HEAD_DIM))
    qg = par(jnp.tile(q_norm_g[0], d // HEAD_DIM))
    hkv = act_call(_norm_mod, [x2, par(kv_norm_g), kv_sh, kv_sc], [(d, BF16, "tile")], "kv_norm")[0]
    wkv = exchange_wait(wkv_handle, hkv, "kv_w_wait")
    half = N_DEV // 2
    k_raw, k_h = mm_nn_col(hkv.reshape(t, d), wkv, "k_proj", (F32, BF16),
                           lambda acc, g_: (acc, _head_norm(acc, g_)), (0, half), (kg.reshape(1, d),))
    v_h = mm_nn_col(hkv.reshape(t, d), wkv, "v_proj", (BF16,), None, (half, half))
    k_raw, k_h, v_h = (a.reshape(bsz, seq, d) for a in (k_raw, k_h, v_h))
    h1 = act_call(_norm_mod, [x2, par(mix_norm_g[1]), sh_a1, sc_a1], [(d, BF16, "tile")], "mix1_norm")[0]
    wq = exchange_wait(wq_handle, h1, "q_w_wait")
    whole = lambda w: w.reshape(1, d, d)
    tm_epi = _tile_fused(seq)
    vec = lambda tm: [pl.BlockSpec((1, d), lambda i, s: (0, 0))]
    q_raw, q_h = mm_nn_row(h1.reshape(t, d), whole(wq), "q_proj", (F32, BF16),
                           lambda acc, g_: (acc, _head_norm(acc, g_) * ATT_SCALE), (qg.reshape(1, d),), vec, tm_epi)
    q_raw, q_h = q_raw.reshape(bsz, seq, d), q_h.reshape(bsz, seq, d)
    o, att_tot = attention_fwd(q_h, k_h, v_h)
    wo = exchange_wait(wo_handle, o, "o_w_wait")
    res_specs = lambda tm: [pl.BlockSpec((tm, d), lambda i, s: (i, 0)),
                            pl.BlockSpec((None, 1, d), lambda i, s: ((i * tm) // seq, 0, 0))]
    mix1, x3 = mm_nn_row(o.reshape(t, d), whole(wo), "o_proj", (F32, F32),
                         lambda acc, x_, g_: (acc, x_ + g_ * acc), (x2.reshape(t, d), g_a1), res_specs, tm_epi)
    mix1, x3 = mix1.reshape(bsz, seq, d), x3.reshape(bsz, seq, d)
    x4, mlp1_saved, w1_1, w2_1 = mlp_fwd(x3, par(mlp_norm_g[1]), sh_m1, sc_m1, g_m1, w1_1_handle, w2_1_handle, "mlp1")

    def loss_fn(y_, t_, f_, g_):
        diff = y_ - t_
        part = jnp.sum(0.5 * jnp.mean(diff * diff, axis=-1, keepdims=True), axis=0, keepdims=True)
        dy_ = diff * (1.0 / d)
        return (jnp.broadcast_to(part, (1, LANES)), dy_, *_gate_bwd(dy_, f_, g_))

    loss_part, dx4, dff1, dg_m1 = act_call(
        loss_fn, [x4, loss_target, mlp1_saved[2], g_m1],
        [(LANES, F32, "all"), (d, F32, "tile")] + _gate_bwd_outs(d), "loss")

    dx3, dw1_1, dw2_1, (dsh_m1, dsc_m1, dgn_mlp1), (dmix1, dg_a1) = mlp_bwd(
        dx4, dff1, x3, par(mlp_norm_g[1]), sc_m1, w1_1, w2_1, mlp1_saved, "mlp1", (mix1, g_a1))
    dmix1 = dmix1.reshape(t, d)
    do = mm_nt_row(dmix1, whole(wo), "o_dproj").reshape(bsz, seq, d)
    dwo = mm_tn(o.reshape(t, d), dmix1, "a", N_DEV, "o_dw")
    dq, dk, dv = attention_bwd(q_h, k_h, v_h, att_tot, do)
    dq_raw, dqg = act_call(_head_norm_bwd, [q_raw, qg, dq], [(d, BF16, "tile"), (d, F32, "all")], "q_dnorm")
    dq_raw = dq_raw.reshape(t, d)
    dh1 = mm_nt_row(dq_raw, whole(wq), "q_dproj").reshape(bsz, seq, d)
    dwq = mm_tn(h1.reshape(t, d), dq_raw, "a", N_DEV, "q_dw")
    dx2, dsh_a1, dsc_a1, dgn_mix1 = act_call(
        _norm_mod_bwd, [x2, par(mix_norm_g[1]), sc_a1, dh1, dx3],
        [(d, F32, "tile"), (d, F32, "seq"), (d, F32, "seq"), (d, F32, "all")], "mix1_dnorm")

    def kv_bwd_fn(k_, g_, dk_, dv_):
        dk_raw, dg_ = _head_norm_bwd(k_, g_, dk_)
        return jnp.concatenate([dk_raw, dv_], axis=1), dg_

    dkvf, dkg = act_call(kv_bwd_fn, [k_raw, kg, dk, dv], [(2 * d, BF16, "tile"), (d, F32, "all")], "k_dnorm")
    dkvf = dkvf.reshape(t, 2 * d)
    dhkv = mm_nt_col(dkvf, wkv, "kv_dproj").reshape(bsz, seq, d)
    dwkv = mm_tn(hkv.reshape(t, d), dkvf, "c", N_DEV, "kv_dw")
    (dwo, dwq, dwkv), att_token = exchange_start([(dwo, True), (dwq, True), (dwkv, True)], "att_dw_start")
    dx2, dkv_sh, dkv_sc, dgn_kv, dff0, dg_m0 = act_call(
        _norm_then_gate_bwd, [x2, par(kv_norm_g) + att_token[0, 0], kv_sc, dhkv, dx2, mlp0_saved[2], g_m],
        _norm_bwd_outs(d) + _gate_bwd_outs(d), "kv_dnorm")

    dx1, dw1_0, dw2_0, (dsh_m0, dsc_m0, dgn_mlp0), _ = mlp_bwd(
        dx2, dff0, x1, par(mlp_norm_g[0]), sc_m, w1_0, w2_0, mlp0_saved, "mlp0")

    def glu_bwd_fn(do_, val, gt, g_):
        sig = jax.nn.sigmoid(gt)
        dmix = g_ * do_
        dz = jnp.concatenate([dmix * sig, dmix * val * sig * (1.0 - sig)], axis=1)
        return dz, _rowsum(do_ * (val * sig))

    dz, dg_a0 = act_call(glu_bwd_fn, [dx1, z_val, z_gate, g_a], [(2 * d, BF16, "tile"), (d, F32, "seq")], "glu_dres")
    dz = dz.reshape(t, 2 * d)
    dy = mm_nt_col(dz, w_glu, "glu_dup", (F32,), lambda acc, y_: (acc * _gelu_grad(y_),),
                   (y.reshape(t, d),)).reshape(bsz, seq, d)
    dwglu = mm_tn(ge.reshape(t, d), dz, "c", N_DEV, "glu_dw")
    (dwglu,), glu_token = exchange_start([(dwglu, True)], "glu_dw_start")
    du, dbbd_re, dbbd_im, dcbd_re, dcbd_im, dab_re, dab_im, dd_skip = s5_bwd(
        dy, h0, st_re, st_im, bbd_re, bbd_im, cbd_re, cbd_im, abr, abi, d_skip + glu_token[0, 0])
    dx0, dsh_a0, dsc_a0, dgn_mix0 = act_call(
        _norm_mod_bwd, [x, par(mix_norm_g[0]), sc_a, du, dx1],
        [(d, F32, "tile"), (d, F32, "seq"), (d, F32, "seq"), (d, F32, "all")], "mix0_dnorm")

    from_bbd = lambda m: jnp.swapaxes(m.reshape(nb, S5_BLOCK_GROUPS, S5_GROUP, S5_STATE), 2, 3).reshape(gp, S5_GROUP)
    d_c = lambda m: jnp.swapaxes(m.reshape(nb, S5_BLOCK_GROUPS, S5_STATE, S5_GROUP), 2, 3).reshape(
        1, n_groups, S5_GROUP, S5_STATE)
    d_lam_re, d_lam_im, d_log_dt, d_b_re, d_b_im = s5_prep_bwd(
        lam_re, lam_im, log_dt, b_re, b_im, dab_re.reshape(gp, 1), dab_im.reshape(gp, 1),
        from_bbd(dbbd_re), from_bbd(dbbd_im))

    small = all_reduce_small([
        jnp.stack([dgn_mix0.reshape(d), dgn_mix1.reshape(d)]),
        jnp.stack([dgn_mlp0.reshape(d), dgn_mlp1.reshape(d)]),
        d_lam_re.reshape(1, n_groups, S5_STATE), d_lam_im.reshape(1, n_groups, S5_STATE),
        d_log_dt.reshape(1, n_groups, S5_STATE).sum(axis=-1),
        d_b_re.reshape(s5_b_re.shape), d_b_im.reshape(s5_b_im.shape),
        d_c(dcbd_re), d_c(dcbd_im),
        dd_skip.reshape(1, d),
        dgn_kv.reshape(d),
        dkg.reshape(d // HEAD_DIM, HEAD_DIM).sum(axis=0),
        dqg.reshape(d // HEAD_DIM, HEAD_DIM).sum(axis=0)[None, :],
        loss_part[0, 0, :1],
    ], "small_grads")
    (g_mix_norm, g_mlp_norm, g_a_re, g_a_im, g_log_dt, g_b_re, g_b_im, g_c_re, g_c_im,
     g_skip_full, g_kv_norm, g_k_norm, g_q_norm, loss_all) = small
    loss = loss_all[0]
    g_s5_d = lax.dynamic_slice_in_dim(g_skip_full, dev * (d // N_DEV), d // N_DEV, axis=1)

    dm_mine = jnp.concatenate([
        dsh_a0, dsc_a0, dg_a0, dsh_m0, dsc_m0, dg_m0,
        dsh_a1, dsc_a1, dg_a1, dsh_m1, dsc_m1, dg_m1, dkv_sh, dkv_sc], axis=2).reshape(bsz, 2 * e_ada + e_kv)
    dm_all = all_gather(dm_mine, "gather_dmod").reshape(N_DEV * bsz, 2 * e_ada + e_kv)
    dm_cols = jnp.concatenate([
        lax.dynamic_slice_in_dim(dm_all, dev * n_ada, n_ada, axis=1),
        lax.dynamic_slice_in_dim(dm_all, e_ada + dev * n_ada, n_ada, axis=1),
        lax.dynamic_slice_in_dim(dm_all, 2 * e_ada + dev * n_kv, n_kv, axis=1)], axis=1)
    dw_cols, db_all = ada_bwd(c_all, dm_cols, dm_all)
    g_ada_w = jnp.stack([dw_cols[:, :n_ada], dw_cols[:, n_ada:2 * n_ada]])
    g_kv_ada_w = dw_cols[:, 2 * n_ada:]
    g_ada_b = db_all[0, :2 * e_ada].reshape(2, e_ada)
    g_kv_ada_b = db_all[0, 2 * e_ada:]

    landed = lambda handle, name: slab_sum(exchange_wait(handle, dx0, name + "_wait"), name + "_sum")
    g_w1 = jnp.stack([landed(dw1_0, "rs_w1_0"), landed(dw1_1, "rs_w1_1")])
    g_w2 = jnp.stack([landed(dw2_0, "rs_w2_0"), landed(dw2_1, "rs_w2_1")])
    g_glu = landed(dwglu, "rs_glu")[None]
    g_wkv = landed(dwkv, "rs_wkv")
    g_wq = landed(dwq, "rs_wq")[None]
    g_wo = landed(dwo, "rs_wo")[None]

    weights = [ada_w, ada_b, mix_norm_g, mlp_norm_g, mlp_w1, mlp_w2, s5_a_re, s5_a_im, s5_log_dt, s5_b_re,
               s5_b_im, s5_c_re, s5_c_im, s5_d, s5_w_glu, kv_ada_w, kv_ada_b, kv_norm_g, w_kv, k_norm_g,
               sb_w_q, q_norm_g, sb_w_o]
    grads = [g_ada_w, g_ada_b, g_mix_norm, g_mlp_norm, g_w1, g_w2, g_a_re, g_a_im, g_log_dt, g_b_re,
             g_b_im, g_c_re, g_c_im, g_s5_d, g_glu, g_kv_ada_w, g_kv_ada_b, g_kv_norm, g_wkv, g_k_norm,
             g_wq, g_q_norm, g_wo]
    ms = [m_ada_w, m_ada_b, m_mix_norm_g, m_mlp_norm_g, m_mlp_w1, m_mlp_w2, m_s5_a_re, m_s5_a_im, m_s5_log_dt,
          m_s5_b_re, m_s5_b_im, m_s5_c_re, m_s5_c_im, m_s5_d, m_s5_w_glu, m_kv_ada_w, m_kv_ada_b, m_kv_norm_g,
          m_w_kv, m_k_norm_g, m_sb_w_q, m_q_norm_g, m_sb_w_o]
    vs = [v_ada_w, v_ada_b, v_mix_norm_g, v_mlp_norm_g, v_mlp_w1, v_mlp_w2, v_s5_a_re, v_s5_a_im, v_s5_log_dt,
          v_s5_b_re, v_s5_b_im, v_s5_c_re, v_s5_c_im, v_s5_d, v_s5_w_glu, v_kv_ada_w, v_kv_ada_b, v_kv_norm_g,
          v_w_kv, v_k_norm_g, v_sb_w_q, v_q_norm_g, v_sb_w_o]
    grads = [g.reshape(w.shape) for g, w in zip(grads, weights)]
    deltas, new_ms, new_vs = [], [], []
    for i, (w, g, m, v) in enumerate(zip(weights, grads, ms, vs)):
        dl, nm, nv = adamw(w, g, m, v, f"adamw_{i}")
        deltas.append(dl)
        new_ms.append(nm)
        new_vs.append(nv)
    return (loss, dx0, *grads, *deltas, *new_ms, *new_vs)
```

```python
import functools
import math

import jax
import jax.numpy as jnp
from jax import lax
from jax.experimental import pallas as pl
from jax.experimental.pallas import tpu as pltpu

F32 = jnp.float32
BF16 = jnp.bfloat16

N_DEV = 8
N_CHIPS = 4
MESH = pl.DeviceIdType.MESH
ANY = pl.BlockSpec(memory_space=pl.ANY)

LANES = 128
VMEM_LIMIT_BYTES = 48 * 2 ** 20
TILE_BUDGET_BYTES = 4 * 2 ** 20

S5_GROUP = 16
S5_STATE = 64
S5_BLOCK_GROUPS = 16
HEAD_DIM = 64
ATT_BLOCK = 128
EPS = 1e-6

ADAM_LR = 0.001
ADAM_B1 = 0.9
ADAM_B2 = 0.999
ADAM_EPS = 1e-08
ADAM_WD = 0.01
ADAM_STEP = 10


def _cparams(*sem):
    return pltpu.CompilerParams(dimension_semantics=sem, vmem_limit_bytes=VMEM_LIMIT_BYTES)


def _divisor_tile(n, limit, mult):
    best = None
    for t in range(mult, min(n, limit) + 1, mult):
        if n % t == 0:
            best = t
    return best if best is not None else n


def _tile_m(m):
    return _divisor_tile(m, 2048 if m >= 4096 else 256, 16)


def all_gather(x, name):
    def body(x_ref, out_ref, send_sems, recv_sems, local_sem):
        ax, ay, ac = lax.axis_index("x"), lax.axis_index("y"), lax.axis_index("c")
        me, sibling = (ax, ay, ac), (ax, ay, 1 - ac)
        chips = [(1 - ax, ay), (ax, 1 - ay), (1 - ax, 1 - ay)]

        def slot(px, py, pc):
            return out_ref.at[4 * px + 2 * py + pc]

        def copy(k, block, to, src=None):
            return pltpu.make_async_remote_copy(
                src_ref=slot(*block) if src is None else src, dst_ref=slot(*block),
                send_sem=send_sems.at[k], recv_sem=recv_sems.at[k], device_id=to, device_id_type=MESH)

        mine = pltpu.make_async_copy(x_ref, slot(*me), local_sem)
        mine.start()
        first = [copy(0, me, sibling, src=x_ref)]
        first += [copy(1 + j, me, (*chip, ac), src=x_ref) for j, chip in enumerate(chips)]
        for cp in first:
            cp.start()
        passed = [copy(4 + j, (*chip, ac), sibling) for j, chip in enumerate(chips)]
        for j, chip in enumerate(chips):
            copy(1 + j, (*chip, ac), me).wait_recv()
            passed[j].start()
        copy(0, sibling, me).wait_recv()
        for j, chip in enumerate(chips):
            copy(4 + j, (*chip, 1 - ac), me).wait_recv()
        for cp in first + passed:
            cp.wait_send()
        mine.wait()

    return pl.pallas_call(
        body, name=name,
        out_shape=jax.ShapeDtypeStruct((N_DEV,) + x.shape, x.dtype),
        in_specs=[ANY], out_specs=ANY,
        scratch_shapes=[pltpu.SemaphoreType.DMA((7,)), pltpu.SemaphoreType.DMA((7,)), pltpu.SemaphoreType.DMA],
    )(x)


HBM = pl.BlockSpec(memory_space=pltpu.HBM)
SEM = pl.BlockSpec(memory_space=pltpu.SEMAPHORE)
N_PEERS = N_DEV - 1


def _peers():
    ax, ay, ac = lax.axis_index("x"), lax.axis_index("y"), lax.axis_index("c")
    flip = lambda v, bit: 1 - v if bit else v
    return [(flip(ax, k & 4), flip(ay, k & 2), flip(ac, k & 1)) for k in range(1, N_DEV)]


def _dev_index(pos):
    return 4 * pos[0] + 2 * pos[1] + pos[2]


def exchange_start(items, name, after=None):
    n = len(items)
    srcs = [a for a, _ in items]
    blocks = [a.shape[1:] if scatter else a.shape for a, scatter in items]
    extra = list(after or ())

    def body(*refs):
        src_refs, land_refs = refs[:n], refs[n:2 * n]
        outs = refs[2 * n + len(extra):]
        send_sems, recv_sems = outs[:n], outs[n:2 * n]
        token = outs[-1]
        me = _dev_index((lax.axis_index("x"), lax.axis_index("y"), lax.axis_index("c")))
        for w, (_, scatter) in enumerate(items):
            for k, peer in enumerate(_peers()):
                src = src_refs[w].at[_dev_index(peer)] if scatter else src_refs[w]
                pltpu.make_async_remote_copy(
                    src_ref=src, dst_ref=land_refs[w].at[me], send_sem=send_sems[w].at[k],
                    recv_sem=recv_sems[w].at[k], device_id=peer, device_id_type=MESH).start()
        token[...] = jnp.zeros_like(token)

    lands = [lax.empty((N_DEV,) + blk, a.dtype) for a, blk in zip(srcs, blocks)]
    res = pl.pallas_call(
        body, name=name,
        out_shape=([pltpu.SemaphoreType.DMA((N_PEERS,))] * (2 * n)
                   + [pltpu.HBM(a.shape, a.dtype) for a in srcs] + [pltpu.HBM(l.shape, l.dtype) for l in lands]
                   + [jax.ShapeDtypeStruct((8, LANES), F32)]),
        in_specs=[HBM] * (2 * n) + [ANY] * len(extra),
        out_specs=[SEM] * (2 * n) + [HBM] * (2 * n) + [pl.BlockSpec(memory_space=pltpu.VMEM)],
        input_output_aliases={i: 2 * n + i for i in range(2 * n)},
        compiler_params=pltpu.CompilerParams(has_side_effects=pltpu.SideEffectType.DATAFLOW_SIDE_EFFECTING),
    )(*[pltpu.with_memory_space_constraint(a, pltpu.HBM) for a in srcs + lands], *extra)
    handles = [(res[w], res[n + w], res[2 * n + w], res[3 * n + w], scatter) for w, (_, scatter) in enumerate(items)]
    return handles, res[-1]


def exchange_wait(handle, after, name):
    send_sem, recv_sem, src, land, scatter = handle

    def body(src_ref, land_ref, send_ref, recv_ref, after_ref, src_out, land_out):
        for k, peer in enumerate(_peers()):
            slot = _dev_index(peer)
            copy = pltpu.make_async_remote_copy(
                src_ref=src_ref.at[slot] if scatter else src_ref, dst_ref=land_ref.at[slot],
                send_sem=send_ref.at[k], recv_sem=recv_ref.at[k], device_id=peer, device_id_type=MESH)
            copy.wait_send()
            copy.wait_recv()

    src, landed = pl.pallas_call(
        body, name=name,
        out_shape=(pltpu.HBM(src.shape, src.dtype), pltpu.HBM(land.shape, land.dtype)),
        in_specs=[HBM, HBM, SEM, SEM, ANY], out_specs=(HBM, HBM), input_output_aliases={0: 0, 1: 1},
        compiler_params=pltpu.CompilerParams(has_side_effects=pltpu.SideEffectType.DATAFLOW_SIDE_EFFECTING),
    )(src, land, send_sem, recv_sem, after)
    dev = _dev_index((lax.axis_index("x"), lax.axis_index("y"), lax.axis_index("c")))
    own = lax.dynamic_index_in_dim(src, dev, axis=0, keepdims=True) if scatter else src[None]
    return lax.dynamic_update_slice_in_dim(landed, own, dev, axis=0)


def rows_call(fn, ins, outs, name):
    rows = ins[0].shape[1]
    per_row = sum(a.shape[0] * a.shape[2] * a.dtype.itemsize for a in ins)
    per_row += sum(l * c * jnp.dtype(dt).itemsize for l, c, dt in outs)
    tr = _divisor_tile(rows, max(16, TILE_BUDGET_BYTES // per_row), 16)
    n_in = len(ins)

    def body(*refs):
        vals = fn(*[r[...] for r in refs[:n_in]])
        if not isinstance(vals, (tuple, list)):
            vals = (vals,)
        for r, v in zip(refs[n_in:], vals):
            r[...] = v.astype(r.dtype)

    def spec(l, c):
        return pl.BlockSpec((l, tr, c), lambda i: (0, i, 0))

    res = pl.pallas_call(
        body, name=name, grid=(rows // tr,),
        in_specs=[spec(a.shape[0], a.shape[2]) for a in ins],
        out_specs=[spec(l, c) for l, c, _ in outs],
        out_shape=[jax.ShapeDtypeStruct((l, rows, c), dt) for l, c, dt in outs],
        compiler_params=_cparams("arbitrary"),
    )(*ins)
    return res


def _as_rows(a, lead=0):
    shape = a.shape
    l = int(math.prod(shape[:lead])) if lead else 1
    rest = shape[lead:]
    c = rest[-1] if rest else 1
    r = int(math.prod(rest[:-1])) if len(rest) > 1 else 1
    return a.reshape(l, r, c)


def act_call(fn, ins, outs, name):
    bsz, seq = ins[0].shape[0], ins[0].shape[1]
    per_row = sum(a.shape[2] * a.dtype.itemsize for a in ins if a.shape[1] == seq)
    per_row += sum(c * jnp.dtype(dt).itemsize for c, dt, kind in outs if kind == "tile")
    ts = _divisor_tile(seq, max(16, TILE_BUDGET_BYTES // per_row), 16)
    n_in = len(ins)

    def in_spec(a):
        c = a.shape[2]
        if a.shape[1] == seq:
            return pl.BlockSpec((None, ts, c), lambda b, s: (b, s, 0))
        if a.shape[0] == bsz:
            return pl.BlockSpec((None, 1, c), lambda b, s: (b, 0, 0))
        return pl.BlockSpec((None, 1, c), lambda b, s: (0, 0, 0))

    def out_spec(c, kind):
        if kind == "tile":
            return pl.BlockSpec((None, ts, c), lambda b, s: (b, s, 0))
        if kind == "seq":
            return pl.BlockSpec((None, 1, c), lambda b, s: (b, 0, 0))
        return pl.BlockSpec((None, 1, c), lambda b, s: (0, 0, 0))

    def out_shape(c, dt, kind):
        if kind == "tile":
            return jax.ShapeDtypeStruct((bsz, seq, c), dt)
        return jax.ShapeDtypeStruct((bsz if kind == "seq" else 1, 1, c), dt)

    def accumulate(ref, v, first):
        @pl.when(first)
        def _():
            ref[...] = jnp.zeros_like(ref)

        ref[...] += v.astype(ref.dtype)

    def body(*refs):
        b, s = pl.program_id(0), pl.program_id(1)
        vals = fn(*[r[...] for r in refs[:n_in]])
        if not isinstance(vals, (tuple, list)):
            vals = (vals,)
        for ref, v, (_, _, kind) in zip(refs[n_in:], vals, outs):
            if kind == "tile":
                ref[...] = v.astype(ref.dtype)
            elif kind == "seq":
                accumulate(ref, v, s == 0)
            else:
                accumulate(ref, v, jnp.logical_and(b == 0, s == 0))

    return pl.pallas_call(
        body, name=name, grid=(bsz, seq // ts),
        in_specs=[in_spec(a) for a in ins],
        out_specs=[out_spec(c, kind) for c, _, kind in outs],
        out_shape=[out_shape(*o) for o in outs],
        compiler_params=_cparams("arbitrary", "arbitrary"),
    )(*ins)


def _mm(name, grid, a, a_spec, b, b_spec, dims, out_shape, out_spec, out_dtypes, acc_steps,
        epi=None, extras=(), extra_specs=()):
    n_ex, n_out = len(extras), len(out_dtypes)
    tile = tuple(d for d in out_spec.block_shape if d is not None)

    def body(*refs):
        a_ref, b_ref = refs[0], refs[1]
        ex_refs = refs[2:2 + n_ex]
        o_refs = refs[2 + n_ex:2 + n_ex + n_out]
        def product():
            return lax.dot_general(a_ref[...].astype(BF16), b_ref[...].astype(BF16), (dims, ((), ())),
                                   preferred_element_type=F32)

        def finish(acc):
            vals = epi(acc, *[r[...] for r in ex_refs]) if epi is not None else (acc,) * n_out
            for r, v in zip(o_refs, vals):
                r[...] = v.astype(r.dtype)

        if not acc_steps:
            finish(product())
        else:
            acc_ref = refs[-1]
            s = pl.program_id(1)

            @pl.when(s == 0)
            def _():
                acc_ref[...] = jnp.zeros_like(acc_ref)

            acc_ref[...] += product()

            @pl.when(s == acc_steps - 1)
            def _():
                finish(acc_ref[...])

    res = pl.pallas_call(
        body, name=name, grid=grid,
        in_specs=[a_spec, b_spec] + list(extra_specs),
        out_specs=[out_spec] * n_out,
        out_shape=[jax.ShapeDtypeStruct(out_shape, dt) for dt in out_dtypes],
        scratch_shapes=[pltpu.VMEM(tile, F32)] if acc_steps else [],
        compiler_params=_cparams("arbitrary", "arbitrary"),
    )(a, b, *extras)
    return res if n_out > 1 else res[0]


def mm_nn_col(a, w, name, out_dtypes=(F32,), epi=None, slabs=None, col_params=()):
    m, k = a.shape
    _, _, nb = w.shape
    first, count = slabs if slabs is not None else (0, w.shape[0])
    tm = _tile_m(m)
    return _mm(name, (m // tm, count), a, pl.BlockSpec((tm, k), lambda i, j: (i, 0)),
               w, pl.BlockSpec((None, k, nb), lambda i, j: (first + j, 0, 0)), ((1,), (0,)),
               (m, count * nb), pl.BlockSpec((tm, nb), lambda i, j: (i, j)), out_dtypes, 0, epi,
               col_params, [pl.BlockSpec((1, nb), lambda i, j: (0, j))] * len(col_params))


def mm_nn_row(a, w, name, out_dtypes=(F32,), epi=None, extras=(), extra_specs=None, tm=None):
    m = a.shape[0]
    ns, kb, n = w.shape
    tm = tm or _tile_m(m)
    return _mm(name, (m // tm, ns), a, pl.BlockSpec((tm, kb), lambda i, s: (i, s)),
               w, pl.BlockSpec((None, kb, n), lambda i, s: (s, 0, 0)), ((1,), (0,)),
               (m, n), pl.BlockSpec((tm, n), lambda i, s: (i, 0)), out_dtypes, ns if ns > 1 else 0, epi,
               extras, extra_specs(tm) if extras else ())


def mm_nt_col(dc, w, name, out_dtypes=(F32,), epi=None, extras=()):
    m = dc.shape[0]
    ns, k, nb = w.shape
    tm = _divisor_tile(m, 512, 16)
    n_ex, n_out = len(extras), len(out_dtypes)

    def body(dc_ref, w_ref, *rest):
        ex_refs, o_refs = rest[:n_ex], rest[n_ex:]
        w_all = jnp.concatenate([w_ref[s] for s in range(ns)], axis=1)
        acc = lax.dot_general(dc_ref[...].astype(BF16), w_all, (((1,), (1,)), ((), ())), preferred_element_type=F32)
        vals = epi(acc, *[r[...] for r in ex_refs]) if epi is not None else (acc,) * n_out
        for r, v in zip(o_refs, vals):
            r[...] = v.astype(r.dtype)

    rows = pl.BlockSpec((tm, k), lambda i: (i, 0))
    res = pl.pallas_call(
        body, name=name, grid=(m // tm,),
        in_specs=[pl.BlockSpec((tm, ns * nb), lambda i: (i, 0)), pl.BlockSpec((ns, k, nb), lambda i: (0, 0, 0))]
        + [rows] * n_ex,
        out_specs=[rows] * n_out,
        out_shape=[jax.ShapeDtypeStruct((m, k), dt) for dt in out_dtypes],
        compiler_params=_cparams("arbitrary"),
    )(dc, w, *extras)
    return res if n_out > 1 else res[0]


def mm_nt_row(dc, w, name, out_dtypes=(F32,), epi=None, extras=()):
    m, n = dc.shape
    ns, kb, _ = w.shape
    tm = _tile_m(m)
    spec = pl.BlockSpec((tm, kb), lambda i, s: (i, s))
    return _mm(name, (m // tm, ns), dc, pl.BlockSpec((tm, n), lambda i, s: (i, 0)),
               w, pl.BlockSpec((None, kb, n), lambda i, s: (s, 0, 0)), ((1,), (1,)),
               (m, ns * kb), spec, out_dtypes, 0, epi, extras, [spec] * len(extras))


def mm_tn(a, c, slab, ns, name, out_dtype=BF16):
    m, ka_all = a.shape
    nc_all = c.shape[1]
    ka = ka_all // ns if slab == "a" else ka_all
    nc = nc_all // ns if slab == "c" else nc_all
    tt = _divisor_tile(m, 1024, 16)
    steps = m // tt

    def body(a_ref, c_ref, o_ref, acc_ref):
        t = pl.program_id(0)

        @pl.when(t == 0)
        def _():
            acc_ref[...] = jnp.zeros_like(acc_ref)

        for s in range(ns):
            a_s = a_ref[:, s * ka:(s + 1) * ka] if slab == "a" else a_ref[...]
            c_s = c_ref[:, s * nc:(s + 1) * nc] if slab == "c" else c_ref[...]
            acc_ref[s] += lax.dot_general(a_s.astype(BF16), c_s.astype(BF16), (((0,), (0,)), ((), ())),
                                          preferred_element_type=F32)

        @pl.when(t == steps - 1)
        def _():
            o_ref[...] = acc_ref[...].astype(o_ref.dtype)

    return pl.pallas_call(
        body, name=name, grid=(steps,),
        in_specs=[pl.BlockSpec((tt, ka_all), lambda t: (t, 0)), pl.BlockSpec((tt, nc_all), lambda t: (t, 0))],
        out_specs=pl.BlockSpec((ns, ka, nc), lambda t: (0, 0, 0), pipeline_mode=pl.Buffered(1)),
        out_shape=jax.ShapeDtypeStruct((ns, ka, nc), out_dtype),
        scratch_shapes=[pltpu.VMEM((ns, ka, nc), F32)],
        compiler_params=_cparams("arbitrary"),
    )(a, c)


def slab_sum(landed, name):
    shape = landed.shape[1:]
    total = rows_call(lambda g: jnp.sum(g.astype(F32), axis=0, keepdims=True),
                      [_as_rows(landed, 1)], [(1, shape[-1], F32)], name)[0]
    return total.reshape(shape)


def all_reduce_small(leaves, name):
    sizes = [int(a.size) for a in leaves]
    flat = jnp.concatenate([a.reshape(-1) for a in leaves])
    total = int(flat.size)
    chunk = N_DEV * 16 * LANES
    padded = -(-total // chunk) * chunk
    parts = jnp.pad(flat, (0, padded - total)).reshape(N_DEV, padded // (N_DEV * LANES), LANES)
    (scatter,), _ = exchange_start([(parts, True)], name + "_scatter")
    landed = exchange_wait(scatter, parts, name + "_scatter_wait")
    mine = rows_call(lambda g: jnp.sum(g, axis=0, keepdims=True), [landed], [(1, LANES, F32)], name + "_sum")[0][0]
    (gather,), _ = exchange_start([(mine, False)], name + "_gather")
    summed = exchange_wait(gather, mine, name + "_gather_wait").reshape(-1)
    out, at = [], 0
    for a, n in zip(leaves, sizes):
        out.append(summed[at:at + n].reshape(a.shape))
        at += n
    return out


def adamw(w, g, m, v, name):
    c = w.shape[-1] if w.ndim else 1

    def fn(w_, g_, m_, v_):
        nm = ADAM_B1 * m_ + (1.0 - ADAM_B1) * g_
        nv = ADAM_B2 * v_ + (1.0 - ADAM_B2) * (g_ * g_)
        m_hat = nm / (1.0 - ADAM_B1 ** ADAM_STEP)
        v_hat = nv / (1.0 - ADAM_B2 ** ADAM_STEP)
        delta = -ADAM_LR * (m_hat / (jnp.sqrt(v_hat) + ADAM_EPS) + ADAM_WD * w_)
        return delta, nm, nv

    res = rows_call(fn, [_as_rows(t) for t in (w, g.astype(F32), m, v)], [(1, c, F32)] * 3, name)
    return tuple(r.reshape(w.shape) for r in res)


def _rowsum(v):
    return jnp.sum(v, axis=0, keepdims=True)


def _norm_mod(x, g, sh, sc):
    n = x * lax.rsqrt(jnp.mean(x * x, axis=-1, keepdims=True) + EPS)
    return (n * g) * (1.0 + sc) + sh


def _norm_mod_bwd(x, g, sc, dh, dres):
    r = lax.rsqrt(jnp.mean(x * x, axis=-1, keepdims=True) + EPS)
    n = x * r
    dy = dh * (1.0 + sc)
    dn = dy * g
    dx = r * (dn - n * jnp.mean(dn * n, axis=-1, keepdims=True))
    return dres + dx, _rowsum(dh), _rowsum(dh * (n * g)), _rowsum(dy * n)


def _head_mean(v):
    low = lax.broadcasted_iota(jnp.int32, (1, LANES), 1) < HEAD_DIM
    parts = []
    for p in range(v.shape[1] // LANES):
        blk = v[:, p * LANES:(p + 1) * LANES]
        s0 = jnp.sum(jnp.where(low, blk, 0.0), axis=-1, keepdims=True)
        s1 = jnp.sum(jnp.where(low, 0.0, blk), axis=-1, keepdims=True)
        parts.append(jnp.where(low, s0, s1))
    return jnp.concatenate(parts, axis=1) * (1.0 / HEAD_DIM)


def _head_norm(x, g):
    return x * lax.rsqrt(_head_mean(x * x) + EPS) * g


def _head_norm_bwd(x, g, dy):
    r = lax.rsqrt(_head_mean(x * x) + EPS)
    n = x * r
    dn = dy * g
    return r * (dn - n * _head_mean(dn * n)), _rowsum(dy * n)


GELU_C = math.sqrt(2.0 / math.pi)
GELU_A = 0.044715


def _gelu_grad(y):
    t = jnp.tanh(GELU_C * (y + GELU_A * y * y * y))
    return 0.5 * (1.0 + t) + 0.5 * y * (1.0 - t * t) * GELU_C * (1.0 + 3.0 * GELU_A * y * y)


def ada_fwd(c_all, w_cols, b_cols):
    def body(c_ref, w_ref, b_ref, o_ref):
        c = c_ref[...]
        s = (c * jax.nn.sigmoid(c)).astype(BF16)
        o_ref[...] = jnp.dot(s, w_ref[...].astype(BF16), preferred_element_type=F32) + b_ref[...]

    return pl.pallas_call(
        body, name="ada_fwd", out_shape=jax.ShapeDtypeStruct((c_all.shape[0], w_cols.shape[1]), F32),
        compiler_params=pltpu.CompilerParams(vmem_limit_bytes=VMEM_LIMIT_BYTES),
    )(c_all, w_cols, b_cols)


def ada_bwd(c_all, dm_cols, dm_all):
    def body(c_ref, d_ref, all_ref, dw_ref, db_ref):
        c = c_ref[...]
        s = (c * jax.nn.sigmoid(c)).astype(BF16)
        dw_ref[...] = lax.dot_general(s, d_ref[...].astype(BF16), (((0,), (0,)), ((), ())),
                                      preferred_element_type=F32)
        db_ref[...] = jnp.sum(all_ref[...], axis=0, keepdims=True)

    return pl.pallas_call(
        body, name="ada_bwd",
        out_shape=[jax.ShapeDtypeStruct((c_all.shape[1], dm_cols.shape[1]), F32),
                   jax.ShapeDtypeStruct((1, dm_all.shape[1]), F32)],
        compiler_params=pltpu.CompilerParams(vmem_limit_bytes=VMEM_LIMIT_BYTES),
    )(c_all, dm_cols, dm_all)


def _s5_discretise(lam_re, lam_im, log_dt, b_re, b_im):
    dt = jnp.exp(log_dt)
    mag = jnp.exp(lam_re * dt)
    ab_re = mag * jnp.cos(lam_im * dt)
    ab_im = mag * jnp.sin(lam_im * dt)
    den = lam_re * lam_re + lam_im * lam_im
    nr = ab_re - 1.0
    ni = ab_im
    f_re = (nr * lam_re + ni * lam_im) / den
    f_im = (ni * lam_re - nr * lam_im) / den
    bb_re = f_re * b_re - f_im * b_im
    bb_im = f_re * b_im + f_im * b_re
    return ab_re, ab_im, bb_re, bb_im


def s5_prep(lam_re, lam_im, log_dt, b_re, b_im):
    gp, h = b_re.shape

    def body(lr, li, ld, br, bi, o_ar, o_ai, o_br, o_bi):
        res = _s5_discretise(lr[...], li[...], ld[...], br[...], bi[...])
        for r, v in zip((o_ar, o_ai, o_br, o_bi), res):
            r[...] = v

    col, mat = jax.ShapeDtypeStruct((gp, 1), F32), jax.ShapeDtypeStruct((gp, h), F32)
    return pl.pallas_call(body, name="s5_prep", out_shape=[col, col, mat, mat])(lam_re, lam_im, log_dt, b_re, b_im)


def s5_prep_bwd(lam_re, lam_im, log_dt, b_re, b_im, d_ab_re, d_ab_im, d_bb_re, d_bb_im):
    gp, h = b_re.shape

    def body(lr, li, ld, br, bi, g_ar, g_ai, g_br, g_bi, o_lr, o_li, o_ld, o_br, o_bi):
        _, vjp = jax.vjp(_s5_discretise, lr[...], li[...], ld[...], br[...], bi[...])
        res = vjp((g_ar[...], g_ai[...], g_br[...], g_bi[...]))
        for r, v in zip((o_lr, o_li, o_ld, o_br, o_bi), res):
            r[...] = v

    col, mat = jax.ShapeDtypeStruct((gp, 1), F32), jax.ShapeDtypeStruct((gp, h), F32)
    return pl.pallas_call(body, name="s5_prep_bwd", out_shape=[col, col, col, mat, mat])(
        lam_re, lam_im, log_dt, b_re, b_im, d_ab_re, d_ab_im, d_bb_re, d_bb_im)


def _s5_chunk(seq):
    return _divisor_tile(seq, 256, 16)


def s5_fwd(u, bbd_re, bbd_im, cbd_re, cbd_im, ab_re, ab_im, dskip):
    bsz, seq, d = u.shape
    nb, cb, ns = bbd_re.shape
    lc = _s5_chunk(seq)

    def body(u_ref, bre_ref, bim_ref, cre_ref, cim_ref, ar_ref, ai_ref, d_ref, y_ref, ge_ref, sre_ref, sim_ref,
             carry_re, carry_im):
        t = pl.program_id(1)

        @pl.when(t == 0)
        def _():
            carry_re[...] = jnp.zeros_like(carry_re)
            carry_im[...] = jnp.zeros_like(carry_im)

        for b in range(bsz):
            ub = u_ref[b].astype(BF16)
            sre_ref[b] = jnp.dot(ub, bre_ref[...].astype(BF16), preferred_element_type=F32)
            sim_ref[b] = jnp.dot(ub, bim_ref[...].astype(BF16), preferred_element_type=F32)
        ar, ai = ar_ref[...], ai_ref[...]

        def step(i, carry):
            row = pl.ds(i, 1)
            out = []
            for b, (cr, ci) in enumerate(carry):
                nr = ar * cr - ai * ci + sre_ref[b, row, :]
                ni = ar * ci + ai * cr + sim_ref[b, row, :]
                sre_ref[b, row, :] = nr
                sim_ref[b, row, :] = ni
                out.append((nr, ni))
            return tuple(out)

        init = tuple((carry_re[b], carry_im[b]) for b in range(bsz))
        last = lax.fori_loop(0, lc, step, init, unroll=8)
        for b, (cr, ci) in enumerate(last):
            carry_re[b] = cr
            carry_im[b] = ci
            y = jnp.dot(sre_ref[b].astype(BF16), cre_ref[...].astype(BF16), preferred_element_type=F32)
            y -= jnp.dot(sim_ref[b].astype(BF16), cim_ref[...].astype(BF16), preferred_element_type=F32)
            y = y + d_ref[...] * u_ref[b]
            y_ref[b] = y
            ge_ref[b] = jax.nn.gelu(y).astype(BF16)

    chan = pl.BlockSpec((bsz, lc, cb), lambda n, t: (0, t, n))
    state = pl.BlockSpec((bsz, lc, ns), lambda n, t: (0, t, n))
    par = lambda r, c: pl.BlockSpec((None, r, c), lambda n, t: (n, 0, 0))
    return pl.pallas_call(
        body, name="s5_fwd", grid=(nb, seq // lc),
        in_specs=[chan, par(cb, ns), par(cb, ns), par(ns, cb), par(ns, cb), par(1, ns), par(1, ns),
                  pl.BlockSpec((None, 1, cb), lambda n, t: (0, 0, n))],
        out_specs=[chan, chan, state, state],
        out_shape=[jax.ShapeDtypeStruct((bsz, seq, d), F32), jax.ShapeDtypeStruct((bsz, seq, d), BF16),
                   jax.ShapeDtypeStruct((bsz, seq, nb * ns), F32),
                   jax.ShapeDtypeStruct((bsz, seq, nb * ns), F32)],
        scratch_shapes=[pltpu.VMEM((bsz, 1, ns), F32), pltpu.VMEM((bsz, 1, ns), F32)],
        compiler_params=_cparams("arbitrary", "arbitrary"),
    )(u, bbd_re, bbd_im, cbd_re, cbd_im, ab_re, ab_im, dskip)


def s5_bwd(dy, u, st_re, st_im, bbd_re, bbd_im, cbd_re, cbd_im, ab_re, ab_im, dskip):
    bsz, seq, d = u.shape
    nb, cb, ns = bbd_re.shape
    lc = _s5_chunk(seq)
    nc = seq // lc

    def body(dy_ref, u_ref, sre_ref, sim_ref, bre_ref, bim_ref, cre_ref, cim_ref, ar_ref, ai_ref, d_ref,
             du_ref, dbre_out, dbim_out, dcre_out, dcim_out, dar_ref, dai_ref, dd_ref,
             g_re, g_im, gs_re, gs_im, carry_re, carry_im, dbre_ref, dbim_ref, dcre_ref, dcim_ref):
        t = pl.program_id(1)

        @pl.when(t == 0)
        def _():
            for r in (dbre_ref, dbim_ref, dcre_ref, dcim_ref, dar_ref, dai_ref, dd_ref, carry_re, carry_im):
                r[...] = jnp.zeros_like(r)

        nt = (((1,), (1,)), ((), ()))
        tn = (((0,), (0,)), ((), ()))
        for b in range(bsz):
            dyb = dy_ref[b].astype(BF16)
            g_re[b] = lax.dot_general(dyb, cre_ref[...].astype(BF16), nt, preferred_element_type=F32)
            g_im[b] = -lax.dot_general(dyb, cim_ref[...].astype(BF16), nt, preferred_element_type=F32)
        ar, ai = ar_ref[...], ai_ref[...]

        def step(k, carry):
            row = pl.ds(lc - 1 - k, 1)
            out = []
            for b, (cr, ci) in enumerate(carry):
                gs_re[b, row, :] = cr
                gs_im[b, row, :] = ci
                nr = ar * cr + ai * ci + g_re[b, row, :]
                ni = ar * ci - ai * cr + g_im[b, row, :]
                g_re[b, row, :] = nr
                g_im[b, row, :] = ni
                out.append((nr, ni))
            return tuple(out)

        init = tuple((carry_re[b], carry_im[b]) for b in range(bsz))
        last = lax.fori_loop(0, lc, step, init, unroll=8)
        for b, (cr, ci) in enumerate(last):
            carry_re[b] = cr
            carry_im[b] = ci
            dyf, uf = dy_ref[b], u_ref[b]
            dyb, ub = dyf.astype(BF16), uf.astype(BF16)
            sr, si = sre_ref[b], sim_ref[b]
            hr, hi = gs_re[b], gs_im[b]
            dar_ref[...] += _rowsum(hr * sr + hi * si)
            dai_ref[...] += _rowsum(hi * sr - hr * si)
            gr, gi = g_re[b].astype(BF16), g_im[b].astype(BF16)
            du = lax.dot_general(gr, bre_ref[...].astype(BF16), nt, preferred_element_type=F32)
            du += lax.dot_general(gi, bim_ref[...].astype(BF16), nt, preferred_element_type=F32)
            du_ref[b] = du + d_ref[...] * dyf
            dbre_ref[...] += lax.dot_general(ub, gr, tn, preferred_element_type=F32)
            dbim_ref[...] += lax.dot_general(ub, gi, tn, preferred_element_type=F32)
            dcre_ref[...] += lax.dot_general(sr.astype(BF16), dyb, tn, preferred_element_type=F32)
            dcim_ref[...] -= lax.dot_general(si.astype(BF16), dyb, tn, preferred_element_type=F32)
            dd_ref[...] += _rowsum(dyf * uf)

        @pl.when(t == nc - 1)
        def _():
            for k in range(cb // S5_GROUP):
                chans = slice(k * S5_GROUP, (k + 1) * S5_GROUP)
                states = slice(k * S5_STATE, (k + 1) * S5_STATE)
                dbre_out[chans, :] = dbre_ref[chans, states]
                dbim_out[chans, :] = dbim_ref[chans, states]
                dcre_out[states, :] = dcre_ref[states, chans]
                dcim_out[states, :] = dcim_ref[states, chans]

    chan = pl.BlockSpec((bsz, lc, cb), lambda n, t: (0, nc - 1 - t, n))
    state = pl.BlockSpec((bsz, lc, ns), lambda n, t: (0, nc - 1 - t, n))
    par = lambda r, c: pl.BlockSpec((None, r, c), lambda n, t: (n, 0, 0))
    return pl.pallas_call(
        body, name="s5_bwd", grid=(nb, nc),
        in_specs=[chan, chan, state, state, par(cb, ns), par(cb, ns), par(ns, cb), par(ns, cb),
                  par(1, ns), par(1, ns), pl.BlockSpec((None, 1, cb), lambda n, t: (0, 0, n))],
        out_specs=[chan, par(cb, S5_STATE), par(cb, S5_STATE), par(ns, S5_GROUP), par(ns, S5_GROUP),
                   par(1, ns), par(1, ns), par(1, cb)],
        out_shape=[jax.ShapeDtypeStruct((bsz, seq, d), F32),
                   jax.ShapeDtypeStruct((nb, cb, S5_STATE), F32), jax.ShapeDtypeStruct((nb, cb, S5_STATE), F32),
                   jax.ShapeDtypeStruct((nb, ns, S5_GROUP), F32), jax.ShapeDtypeStruct((nb, ns, S5_GROUP), F32),
                   jax.ShapeDtypeStruct((nb, 1, ns), F32), jax.ShapeDtypeStruct((nb, 1, ns), F32),
                   jax.ShapeDtypeStruct((nb, 1, cb), F32)],
        scratch_shapes=([pltpu.VMEM((bsz, lc, ns), F32)] * 4 + [pltpu.VMEM((bsz, 1, ns), F32)] * 2
                        + [pltpu.VMEM((cb, ns), F32)] * 2 + [pltpu.VMEM((ns, cb), F32)] * 2),
        compiler_params=_cparams("arbitrary", "arbitrary"),
    )(dy, u, st_re, st_im, bbd_re, bbd_im, cbd_re, cbd_im, ab_re, ab_im, dskip)


ATT_HEADS = 8
ATT_HEADS_FWD = 8
ATT_LANES = ATT_HEADS * HEAD_DIM
ATT_KEYS = 2 * ATT_BLOCK
ATT_Q = 256
ATT_SCALE = 1.0 / math.sqrt(HEAD_DIM)
_NT = (((1,), (1,)), ((), ()))
_TN = (((0,), (0,)), ((), ()))
_HEADS = [slice(h * HEAD_DIM, (h + 1) * HEAD_DIM) for h in range(ATT_HEADS)]
_HALF = [slice(0, ATT_BLOCK), slice(ATT_BLOCK, ATT_KEYS)]


def _log_sigmoids(z):
    sp = jnp.log(1.0 + jnp.exp(-jnp.abs(z)))
    ls = jnp.minimum(z, 0.0) - sp
    return ls, ls - z


def _sum_matrix(after, inclusive):
    j = lax.broadcasted_iota(jnp.int32, (ATT_KEYS, ATT_KEYS), 0) % ATT_BLOCK
    s = lax.broadcasted_iota(jnp.int32, (ATT_KEYS, ATT_KEYS), 1)
    if after:
        hit = (j >= s) if inclusive else (j > s)
    else:
        hit = (j <= s) if inclusive else (j < s)
    return jnp.where(jnp.logical_or(hit, s >= ATT_BLOCK), 1.0, 0.0).astype(BF16)


def _hi_lo(v):
    hi = v.astype(BF16)
    lo = (v - hi.astype(F32)).astype(BF16)
    return jnp.concatenate([hi, lo], axis=1)


def _strict_mask(i, j):
    t = i * ATT_Q + lax.broadcasted_iota(jnp.int32, (ATT_Q, ATT_KEYS), 0)
    s = j * ATT_KEYS + lax.broadcasted_iota(jnp.int32, (ATT_Q, ATT_KEYS), 1)
    return s < t


def attention_fwd(q, k, v):
    bsz, seq, d = q.shape
    n_heads = ATT_HEADS_FWD if d % (ATT_HEADS_FWD * HEAD_DIM) == 0 else ATT_HEADS
    lanes = n_heads * HEAD_DIM
    heads = [slice(h * HEAD_DIM, (h + 1) * HEAD_DIM) for h in range(n_heads)]

    def body(q_ref, k_ref, v_ref, o_ref, tot_ref, z_buf, ls_buf, cs_buf, acc_buf, run_buf):
        i = pl.program_id(2)
        jd = ((i + 1) * ATT_Q - 1) // ATT_KEYS
        sums = _sum_matrix(True, False)
        acc_buf[...] = jnp.zeros_like(acc_buf)
        run_buf[...] = jnp.zeros_like(run_buf)

        def block(j, masked):
            rows = pl.ds(pl.multiple_of(j * ATT_KEYS, ATT_KEYS), ATT_KEYS)
            strict = _strict_mask(i, j) if masked else None
            for h, ln in enumerate(heads):
                z_buf[h] = lax.dot_general(q_ref[:, ln], k_ref[rows, ln], _NT, preferred_element_type=F32)
            for h in range(n_heads):
                for half, cols in enumerate(_HALF):
                    ls, lf = _log_sigmoids(z_buf[h, :, cols])
                    if masked:
                        lf = jnp.where(strict[:, cols], lf, 0.0)
                    ls_buf[h, :, cols] = ls
                    cs_buf[h, half] = jnp.dot(_hi_lo(lf), sums, preferred_element_type=F32)
            for h, ln in enumerate(heads):
                run = run_buf[h]
                late, early = cs_buf[h, 1], cs_buf[h, 0]
                a1 = run + late[:, _HALF[0]]
                run = run + late[:, _HALF[1]]
                a0 = run + early[:, _HALF[0]]
                run_buf[h] = run + early[:, _HALF[1]]
                w = jnp.exp(ls_buf[h] + jnp.concatenate([a0, a1], axis=1))
                if masked:
                    w = jnp.where(strict, w, 0.0)
                acc_buf[h] += jnp.dot(w.astype(BF16), v_ref[rows, ln], preferred_element_type=F32)

        block(jd, True)

        def step(it, carry):
            block(jd - 1 - it, False)
            return carry

        lax.fori_loop(0, jd, step, 0)
        o_ref[...] = jnp.concatenate([acc_buf[h] for h in range(n_heads)], axis=1).astype(o_ref.dtype)
        tot_ref[...] = jnp.concatenate([run_buf[h, :, :HEAD_DIM] for h in range(n_heads)], axis=1)

    blk = pl.BlockSpec((None, ATT_Q, lanes), lambda b, p, i: (b, i, p))
    full = pl.BlockSpec((None, seq, lanes), lambda b, p, i: (b, 0, p))
    tile = (n_heads, ATT_Q, ATT_KEYS)
    return pl.pallas_call(
        body, name="attention_fwd", grid=(bsz, d // lanes, seq // ATT_Q),
        in_specs=[blk, full, full], out_specs=[blk, blk],
        out_shape=[jax.ShapeDtypeStruct((bsz, seq, d), BF16), jax.ShapeDtypeStruct((bsz, seq, d), F32)],
        scratch_shapes=[pltpu.VMEM(tile, F32), pltpu.VMEM(tile, F32),
                        pltpu.VMEM((n_heads, 2, ATT_Q, ATT_KEYS), F32),
                        pltpu.VMEM((n_heads, ATT_Q, HEAD_DIM), F32),
                        pltpu.VMEM((n_heads, ATT_Q, ATT_BLOCK), F32)],
        compiler_params=_cparams("arbitrary", "arbitrary", "arbitrary"),
    )(q, k, v)


def attention_bwd(q, k, v, tot, do):
    bsz, seq, d = q.shape

    def body(q_ref, k_ref, v_ref, tot_ref, do_ref, dq_ref, dk_ref, dv_ref,
             z_buf, dw_buf, ls_buf, e_buf, up_buf, bf_buf, w_buf, do_buf, dq_buf, tot_buf, run_buf, erun_buf):
        i = pl.program_id(2)
        jd = ((i + 1) * ATT_Q - 1) // ATT_KEYS

        @pl.when(i == 0)
        def _():
            dk_ref[...] = jnp.zeros_like(dk_ref)
            dv_ref[...] = jnp.zeros_like(dv_ref)

        upto_incl, upto_excl = _sum_matrix(False, True), _sum_matrix(False, False)
        do_buf[...] = do_ref[...].astype(BF16)
        for h, ln in enumerate(_HEADS):
            tot_buf[h] = jnp.concatenate([tot_ref[:, ln], tot_ref[:, ln]], axis=1)
        dq_buf[...] = jnp.zeros_like(dq_buf)
        run_buf[...] = jnp.zeros_like(run_buf)
        erun_buf[...] = jnp.zeros_like(erun_buf)

        def block(j, masked):
            rows = pl.ds(pl.multiple_of(j * ATT_KEYS, ATT_KEYS), ATT_KEYS)
            strict = _strict_mask(i, j) if masked else None
            for h, ln in enumerate(_HEADS):
                z_buf[h] = lax.dot_general(q_ref[:, ln], k_ref[rows, ln], _NT, preferred_element_type=F32)
                dw_buf[h] = lax.dot_general(do_buf[:, ln], v_ref[rows, ln], _NT, preferred_element_type=F32)
            for h in range(ATT_HEADS):
                for half, cols in enumerate(_HALF):
                    ls, lf = _log_sigmoids(z_buf[h, :, cols])
                    if masked:
                        lf = jnp.where(strict[:, cols], lf, 0.0)
                    ls_buf[h, :, cols] = ls
                    up_buf[h, half] = jnp.dot(_hi_lo(lf), upto_incl, preferred_element_type=F32)
            for h in range(ATT_HEADS):
                run = run_buf[h]
                early, late = up_buf[h, 0], up_buf[h, 1]
                u0 = run + early[:, _HALF[0]]
                run = run + early[:, _HALF[1]]
                u1 = run + late[:, _HALF[0]]
                run_buf[h] = run + late[:, _HALF[1]]
                tot_h = tot_buf[h]
                after = jnp.concatenate([tot_h - u0, tot_h - u1], axis=1)
                w = jnp.exp(ls_buf[h] + after)
                if masked:
                    w = jnp.where(strict, w, 0.0)
                w_buf[h] = w.astype(BF16)
                e = dw_buf[h] * w
                e_buf[h] = e
                for half, cols in enumerate(_HALF):
                    bf_buf[h, half] = jnp.dot(_hi_lo(e[:, cols]), upto_excl, preferred_element_type=F32)
            dks, dvs = [], []
            for h, ln in enumerate(_HEADS):
                erun = erun_buf[h]
                early, late = bf_buf[h, 0], bf_buf[h, 1]
                b0 = erun + early[:, _HALF[0]]
                erun = erun + early[:, _HALF[1]]
                b1 = erun + late[:, _HALF[0]]
                erun_buf[h] = erun + late[:, _HALF[1]]
                e = e_buf[h]
                dz = e - jnp.exp(ls_buf[h]) * (e + jnp.concatenate([b0, b1], axis=1))
                if masked:
                    dz = jnp.where(strict, dz, 0.0)
                dz = dz.astype(BF16)
                dq_buf[h] += jnp.dot(dz, k_ref[rows, ln], preferred_element_type=F32)
                dks.append(lax.dot_general(dz, q_ref[:, ln], _TN, preferred_element_type=F32))
                dvs.append(lax.dot_general(w_buf[h], do_buf[:, ln], _TN, preferred_element_type=F32))
            dk_ref[rows, :] += jnp.concatenate(dks, axis=1)
            dv_ref[rows, :] += jnp.concatenate(dvs, axis=1)

        def step(j, carry):
            block(j, False)
            return carry

        lax.fori_loop(0, jd, step, 0)
        block(jd, True)
        dq_ref[...] = jnp.concatenate([dq_buf[h] for h in range(ATT_HEADS)], axis=1) * ATT_SCALE

    blk = pl.BlockSpec((None, ATT_Q, ATT_LANES), lambda b, p, i: (b, i, p))
    full = pl.BlockSpec((None, seq, ATT_LANES), lambda b, p, i: (b, 0, p), pipeline_mode=pl.Buffered(1))
    shape = jax.ShapeDtypeStruct((bsz, seq, d), F32)
    tile = (ATT_HEADS, ATT_Q, ATT_KEYS)
    pair = (ATT_HEADS, 2, ATT_Q, ATT_KEYS)
    square = (ATT_HEADS, ATT_Q, ATT_BLOCK)
    return pl.pallas_call(
        body, name="attention_bwd", grid=(bsz, d // ATT_LANES, seq // ATT_Q),
        in_specs=[blk, full, full, blk, blk], out_specs=[blk, full, full], out_shape=[shape, shape, shape],
        scratch_shapes=[pltpu.VMEM(tile, F32), pltpu.VMEM(tile, F32), pltpu.VMEM(tile, F32), pltpu.VMEM(tile, F32),
                        pltpu.VMEM(pair, F32), pltpu.VMEM(pair, F32), pltpu.VMEM(tile, BF16),
                        pltpu.VMEM((ATT_Q, ATT_LANES), BF16), pltpu.VMEM((ATT_HEADS, ATT_Q, HEAD_DIM), F32),
                        pltpu.VMEM(square, F32), pltpu.VMEM(square, F32), pltpu.VMEM(square, F32)],
        compiler_params=_cparams("arbitrary", "arbitrary", "arbitrary"),
    )(q, k, v, tot, do)


def _tile_fused(seq):
    return _divisor_tile(seq, 1024, 16)


MLP_SLABS = 1


def mlp_core_fwd(x, norm_g, sh, sc, gate, w1, w2, name):
    bsz, seq, d = x.shape
    ns, _, fs = w1.shape
    t = bsz * seq
    tm = _tile_fused(seq)
    g = MLP_SLABS if ns % MLP_SLABS == 0 else 1
    steps = ns // g

    def body(x_ref, w1_ref, w2_ref, ng_ref, sh_ref, sc_ref, g_ref, h_ref, act_ref, ff_ref, out_ref, acc_ref, h_buf):
        s = pl.program_id(1)

        @pl.when(s == 0)
        def _():
            h = _norm_mod(x_ref[...], ng_ref[...], sh_ref[...], sc_ref[...]).astype(BF16)
            h_buf[...] = h
            h_ref[...] = h
            acc_ref[...] = jnp.zeros_like(acc_ref)

        hb = h_buf[...]
        for k in range(g):
            pre = jnp.dot(hb, w1_ref[k], preferred_element_type=F32)
            act = jnp.square(jnp.maximum(pre, 0.0)).astype(BF16)
            act_ref[:, k * fs:(k + 1) * fs] = act
            acc_ref[...] += jnp.dot(act, w2_ref[k], preferred_element_type=F32)

        @pl.when(s == steps - 1)
        def _():
            ff = acc_ref[...]
            ff_ref[...] = ff
            out_ref[...] = x_ref[...] + g_ref[...] * ff

    rows = pl.BlockSpec((tm, d), lambda i, s: (i, 0))
    per_seq = pl.BlockSpec((None, 1, d), lambda i, s: ((i * tm) // seq, 0, 0))
    h, act, ff, out = pl.pallas_call(
        body, name=name, grid=(t // tm, steps),
        in_specs=[rows, pl.BlockSpec((g, d, fs), lambda i, s: (s, 0, 0)),
                  pl.BlockSpec((g, fs, d), lambda i, s: (s, 0, 0)),
                  pl.BlockSpec((None, 1, d), lambda i, s: (0, 0, 0)), per_seq, per_seq, per_seq],
        out_specs=[rows, pl.BlockSpec((tm, g * fs), lambda i, s: (i, s)), rows, rows],
        out_shape=[jax.ShapeDtypeStruct((t, d), BF16), jax.ShapeDtypeStruct((t, ns * fs), BF16),
                   jax.ShapeDtypeStruct((t, d), F32), jax.ShapeDtypeStruct((t, d), F32)],
        scratch_shapes=[pltpu.VMEM((tm, d), F32), pltpu.VMEM((tm, d), BF16)],
        compiler_params=_cparams("arbitrary", "arbitrary"),
    )(x.reshape(t, d), w1, w2, norm_g, sh, sc, gate)
    return h, act, ff.reshape(bsz, seq, d), out.reshape(bsz, seq, d)


def mlp_core_bwd(dff, act, w1, w2, name):
    t, d = dff.shape
    ns, _, fs = w1.shape
    tm = _tile_fused(t)
    g = MLP_SLABS if ns % MLP_SLABS == 0 else 1
    steps = ns // g

    def body(dff_ref, act_ref, w1_ref, w2_ref, dpre_ref, dh_ref, acc_ref):
        s = pl.program_id(1)
        db = dff_ref[...]

        @pl.when(s == 0)
        def _():
            acc_ref[...] = jnp.zeros_like(acc_ref)

        for k in range(g):
            cols = slice(k * fs, (k + 1) * fs)
            dact = lax.dot_general(db, w2_ref[k], _NT, preferred_element_type=F32)
            dpre = (dact * (2.0 * jnp.sqrt(act_ref[:, cols].astype(F32)))).astype(BF16)
            dpre_ref[:, cols] = dpre
            acc_ref[...] += lax.dot_general(dpre, w1_ref[k], _NT, preferred_element_type=F32)

        @pl.when(s == steps - 1)
        def _():
            dh_ref[...] = acc_ref[...]

    rows = pl.BlockSpec((tm, d), lambda i, s: (i, 0))
    slab = pl.BlockSpec((tm, g * fs), lambda i, s: (i, s))
    return pl.pallas_call(
        body, name=name, grid=(t // tm, steps),
        in_specs=[rows, slab, pl.BlockSpec((g, d, fs), lambda i, s: (s, 0, 0)),
                  pl.BlockSpec((g, fs, d), lambda i, s: (s, 0, 0))],
        out_specs=[slab, rows],
        out_shape=[jax.ShapeDtypeStruct((t, ns * fs), BF16), jax.ShapeDtypeStruct((t, d), F32)],
        scratch_shapes=[pltpu.VMEM((tm, d), F32)],
        compiler_params=_cparams("arbitrary", "arbitrary"),
    )(dff, act, w1, w2)


def mlp_fwd(x, g, sh, sc, gate, w1_handle, w2_handle, tag):
    w1 = exchange_wait(w1_handle, x, tag + "_w1_wait")
    w2 = exchange_wait(w2_handle, x, tag + "_w2_wait")
    h, act, ff, out = mlp_core_fwd(x, g, sh, sc, gate, w1, w2, tag + "_core")
    return out, (h, act, ff), w1, w2


def glu_fwd(ge, w, x, gate, name):
    bsz, seq, d = x.shape
    ns, _, nb = w.shape
    half = ns // 2
    t = bsz * seq
    tm = _tile_fused(seq)

    def body(a_ref, wv_ref, wg_ref, x_ref, g_ref, val_ref, gt_ref, out_ref):
        a = a_ref[...]
        val = jnp.dot(a, wv_ref[...], preferred_element_type=F32)
        gt = jnp.dot(a, wg_ref[...], preferred_element_type=F32)
        val_ref[...] = val
        gt_ref[...] = gt
        out_ref[...] = x_ref[...] + g_ref[...] * (val * jax.nn.sigmoid(gt))

    cols = pl.BlockSpec((tm, nb), lambda i, j: (i, j))
    res = pl.pallas_call(
        body, name=name, grid=(t // tm, half),
        in_specs=[pl.BlockSpec((tm, d), lambda i, j: (i, 0)),
                  pl.BlockSpec((None, d, nb), lambda i, j: (j, 0, 0)),
                  pl.BlockSpec((None, d, nb), lambda i, j: (half + j, 0, 0)), cols,
                  pl.BlockSpec((None, 1, nb), lambda i, j: ((i * tm) // seq, 0, j))],
        out_specs=[cols, cols, cols],
        out_shape=[jax.ShapeDtypeStruct((t, d), F32)] * 3,
        compiler_params=_cparams("arbitrary", "arbitrary"),
    )(ge.reshape(t, d), w, w, x.reshape(t, d), gate)
    return tuple(r.reshape(bsz, seq, d) for r in res)


def _gate_bwd(dx, f, gate):
    return gate * dx, _rowsum(dx * f)


def _gate_bwd_outs(d):
    return [(d, BF16, "tile"), (d, F32, "seq")]


def _norm_bwd_outs(d):
    return [(d, F32, "tile"), (d, F32, "seq"), (d, F32, "seq"), (d, F32, "all")]


def _norm_then_gate_bwd(x, g, sc, dh, dres, f, gate):
    res = _norm_mod_bwd(x, g, sc, dh, dres)
    return (*res, *_gate_bwd(res[0], f, gate))


def mlp_bwd(dout, dff, x, g, sc, w1, w2, saved, tag, branch=None):
    bsz, seq, d = x.shape
    t = bsz * seq
    ns = w1.shape[0]
    h, act, _ = saved
    dff = dff.reshape(t, d)
    dpre, dh = mlp_core_bwd(dff, act, w1, w2, tag + "_dcore")
    dw2 = mm_tn(act, dff, "a", ns, tag + "_dw2")
    dw1 = mm_tn(h.reshape(t, d), dpre, "c", ns, tag + "_dw1")
    (dw1_handle, dw2_handle), token = exchange_start([(dw1, True), (dw2, True)], tag + "_dw_start")
    ins = [x, g + token[0, 0], sc, dh.reshape(bsz, seq, d), dout]
    if branch is None:
        dx, dsh, dsc, dg = act_call(_norm_mod_bwd, ins, _norm_bwd_outs(d), tag + "_dnorm")
        into_branch = None
    else:
        dx, dsh, dsc, dg, *into_branch = act_call(_norm_then_gate_bwd, ins + list(branch),
                                                  _norm_bwd_outs(d) + _gate_bwd_outs(d), tag + "_dnorm")
    return dx, dw1_handle, dw2_handle, (dsh, dsc, dg), into_branch


def _block_diag(m):
    _, rows, c = m.shape
    k = S5_BLOCK_GROUPS
    row_group = lax.broadcasted_iota(jnp.int32, (rows, k * c), 0) // (rows // k)
    col_group = lax.broadcasted_iota(jnp.int32, (rows, k * c), 1) // c
    return jnp.where(row_group == col_group, jnp.tile(m, (1, 1, k)), 0.0)


def kernel(x, c, ada_w, ada_b, mix_norm_g, mlp_norm_g, mlp_w1, mlp_w2, s5_a_re, s5_a_im, s5_log_dt, s5_b_re, s5_b_im, s5_c_re, s5_c_im, s5_d, s5_w_glu, kv_ada_w, kv_ada_b, kv_norm_g, w_kv, k_norm_g, sb_w_q, q_norm_g, sb_w_o, loss_target, m_ada_w, m_ada_b, m_mix_norm_g, m_mlp_norm_g, m_mlp_w1, m_mlp_w2, m_s5_a_re, m_s5_a_im, m_s5_log_dt, m_s5_b_re, m_s5_b_im, m_s5_c_re, m_s5_c_im, m_s5_d, m_s5_w_glu, m_kv_ada_w, m_kv_ada_b, m_kv_norm_g, m_w_kv, m_k_norm_g, m_sb_w_q, m_q_norm_g, m_sb_w_o, v_ada_w, v_ada_b, v_mix_norm_g, v_mlp_norm_g, v_mlp_w1, v_mlp_w2, v_s5_a_re, v_s5_a_im, v_s5_log_dt, v_s5_b_re, v_s5_b_im, v_s5_c_re, v_s5_c_im, v_s5_d, v_s5_w_glu, v_kv_ada_w, v_kv_ada_b, v_kv_norm_g, v_w_kv, v_k_norm_g, v_sb_w_q, v_q_norm_g, v_sb_w_o):
    bsz, seq, d = x.shape
    t = bsz * seq
    n_groups = d // S5_GROUP
    nb = n_groups // S5_BLOCK_GROUPS
    gp = n_groups * S5_STATE
    dev = 4 * lax.axis_index("x") + 2 * lax.axis_index("y") + lax.axis_index("c")
    e_ada, e_kv = 6 * d, 2 * d
    n_ada, n_kv = e_ada // N_DEV, e_kv // N_DEV

    d_skip = all_gather(s5_d, "gather_skip").reshape(1, 1, d)
    c_all = all_gather(c, "gather_c").reshape(N_DEV * bsz, d)

    w_cols = jnp.concatenate([ada_w[0], ada_w[1], kv_ada_w], axis=1)
    b_cols = jnp.concatenate([
        lax.dynamic_slice_in_dim(ada_b[0], dev * n_ada, n_ada),
        lax.dynamic_slice_in_dim(ada_b[1], dev * n_ada, n_ada),
        lax.dynamic_slice_in_dim(kv_ada_b, dev * n_kv, n_kv)])[None, :]
    mod_cols = ada_fwd(c_all, w_cols, b_cols)
    mod_all = all_gather(mod_cols, "gather_mod")
    mod_mine = lax.dynamic_slice_in_dim(mod_all, dev * bsz, bsz, axis=1)
    mod_mine = jnp.transpose(mod_mine, (1, 0, 2))
    mods = []
    for i in range(2):
        full = mod_mine[:, :, i * n_ada:(i + 1) * n_ada].reshape(bsz, e_ada)
        mods.append([full[:, None, j * d:(j + 1) * d] for j in range(6)])
    kv_full = mod_mine[:, :, 2 * n_ada:].reshape(bsz, e_kv)
    kv_sh, kv_sc = kv_full[:, None, :d], kv_full[:, None, d:]

    par = lambda p: p.reshape(1, 1, -1)

    shards = [s5_w_glu[0], mlp_w1[0], mlp_w2[0], w_kv, sb_w_q[0], sb_w_o[0], mlp_w1[1], mlp_w2[1]]
    gathers, gather_token = exchange_start([(w.astype(BF16), False) for w in shards], "gather_start", after=[mod_all, d_skip])
    glu_handle, w1_0_handle, w2_0_handle, wkv_handle, wq_handle, wo_handle, w1_1_handle, w2_1_handle = gathers
    started = gather_token[0, 0]

    sh_a, sc_a, g_a, sh_m, sc_m, g_m = mods[0]
    lam_re, lam_im = s5_a_re.reshape(gp, 1), s5_a_im.reshape(gp, 1)
    log_dt = jnp.broadcast_to(s5_log_dt.reshape(n_groups, 1), (n_groups, S5_STATE)).reshape(gp, 1)
    b_re, b_im = s5_b_re.reshape(gp, S5_GROUP), s5_b_im.reshape(gp, S5_GROUP)
    ab_re, ab_im, bb_re, bb_im = s5_prep(lam_re, lam_im, log_dt, b_re, b_im)
    to_bbd = lambda m: _block_diag(
        jnp.swapaxes(m.reshape(nb, S5_BLOCK_GROUPS, S5_STATE, S5_GROUP), 2, 3).reshape(nb, -1, S5_STATE))
    to_cbd = lambda m: _block_diag(
        jnp.swapaxes(m.reshape(nb, S5_BLOCK_GROUPS, S5_GROUP, S5_STATE), 2, 3).reshape(nb, -1, S5_GROUP))
    bbd_re, bbd_im = to_bbd(bb_re), to_bbd(bb_im)
    cbd_re, cbd_im = to_cbd(s5_c_re[0]), to_cbd(s5_c_im[0])
    abr, abi = ab_re.reshape(nb, 1, -1), ab_im.reshape(nb, 1, -1)

    h0 = act_call(_norm_mod, [x, par(mix_norm_g[0]) + started, sh_a, sc_a], [(d, F32, "tile")], "mix0_norm")[0]
    y, ge, st_re, st_im = s5_fwd(h0, bbd_re, bbd_im, cbd_re, cbd_im, abr, abi, d_skip)
    w_glu = exchange_wait(glu_handle, ge, "glu_w_wait")
    z_val, z_gate, x1 = glu_fwd(ge, w_glu, x, g_a, "glu_up")
    x2, mlp0_saved, w1_0, w2_0 = mlp_fwd(x1, par(mlp_norm_g[0]), sh_m, sc_m, g_m, w1_0_handle, w2_0_handle, "mlp0")

    sh_a1, sc_a1, g_a1, sh_m1, sc_m1, g_m1 = mods[1]
    kg = par(jnp.tile(k_norm_g, d // HEAD_DIM))
    qg = par(jnp.tile(q_norm_g[0], d // HEAD_DIM))
    hkv = act_call(_norm_mod, [x2, par(kv_norm_g), kv_sh, kv_sc], [(d, BF16, "tile")], "kv_norm")[0]
    wkv = exchange_wait(wkv_handle, hkv, "kv_w_wait")
    half = N_DEV // 2
    k_raw, k_h = mm_nn_col(hkv.reshape(t, d), wkv, "k_proj", (F32, BF16),
                           lambda acc, g_: (acc, _head_norm(acc, g_)), (0, half), (kg.reshape(1, d),))
    v_h = mm_nn_col(hkv.reshape(t, d), wkv, "v_proj", (BF16,), None, (half, half))
    k_raw, k_h, v_h = (a.reshape(bsz, seq, d) for a in (k_raw, k_h, v_h))
    h1 = act_call(_norm_mod, [x2, par(mix_norm_g[1]), sh_a1, sc_a1], [(d, BF16, "tile")], "mix1_norm")[0]
    wq = exchange_wait(wq_handle, h1, "q_w_wait")
    whole = lambda w: w.reshape(1, d, d)
    tm_epi = _tile_fused(seq)
    vec = lambda tm: [pl.BlockSpec((1, d), lambda i, s: (0, 0))]
    q_raw, q_h = mm_nn_row(h1.reshape(t, d), whole(wq), "q_proj", (F32, BF16),
                           lambda acc, g_: (acc, _head_norm(acc, g_) * ATT_SCALE), (qg.reshape(1, d),), vec, tm_epi)
    q_raw, q_h = q_raw.reshape(bsz, seq, d), q_h.reshape(bsz, seq, d)
    o, att_tot = attention_fwd(q_h, k_h, v_h)
    wo = exchange_wait(wo_handle, o, "o_w_wait")
    res_specs = lambda tm: [pl.BlockSpec((tm, d), lambda i, s: (i, 0)),
                            pl.BlockSpec((None, 1, d), lambda i, s: ((i * tm) // seq, 0, 0))]
    mix1, x3 = mm_nn_row(o.reshape(t, d), whole(wo), "o_proj", (F32, F32),
                         lambda acc, x_, g_: (acc, x_ + g_ * acc), (x2.reshape(t, d), g_a1), res_specs, tm_epi)
    mix1, x3 = mix1.reshape(bsz, seq, d), x3.reshape(bsz, seq, d)
    x4, mlp1_saved, w1_1, w2_1 = mlp_fwd(x3, par(mlp_norm_g[1]), sh_m1, sc_m1, g_m1, w1_1_handle, w2_1_handle, "mlp1")

    def loss_fn(y_, t_, f_, g_):
        diff = y_ - t_
        part = jnp.sum(0.5 * jnp.mean(diff * diff, axis=-1, keepdims=True), axis=0, keepdims=True)
        dy_ = diff * (1.0 / d)
        return (jnp.broadcast_to(part, (1, LANES)), dy_, *_gate_bwd(dy_, f_, g_))

    loss_part, dx4, dff1, dg_m1 = act_call(
        loss_fn, [x4, loss_target, mlp1_saved[2], g_m1],
        [(LANES, F32, "all"), (d, F32, "tile")] + _gate_bwd_outs(d), "loss")

    dx3, dw1_1, dw2_1, (dsh_m1, dsc_m1, dgn_mlp1), (dmix1, dg_a1) = mlp_bwd(
        dx4, dff1, x3, par(mlp_norm_g[1]), sc_m1, w1_1, w2_1, mlp1_saved, "mlp1", (mix1, g_a1))
    dmix1 = dmix1.reshape(t, d)
    do = mm_nt_row(dmix1, whole(wo), "o_dproj").reshape(bsz, seq, d)
    dwo = mm_tn(o.reshape(t, d), dmix1, "a", N_DEV, "o_dw")
    dq, dk, dv = attention_bwd(q_h, k_h, v_h, att_tot, do)
    dq_raw, dqg = act_call(_head_norm_bwd, [q_raw, qg, dq], [(d, BF16, "tile"), (d, F32, "all")], "q_dnorm")
    dq_raw = dq_raw.reshape(t, d)
    dh1 = mm_nt_row(dq_raw, whole(wq), "q_dproj").reshape(bsz, seq, d)
    dwq = mm_tn(h1.reshape(t, d), dq_raw, "a", N_DEV, "q_dw")
    dx2, dsh_a1, dsc_a1, dgn_mix1 = act_call(
        _norm_mod_bwd, [x2, par(mix_norm_g[1]), sc_a1, dh1, dx3],
        [(d, F32, "tile"), (d, F32, "seq"), (d, F32, "seq"), (d, F32, "all")], "mix1_dnorm")

    def kv_bwd_fn(k_, g_, dk_, dv_):
        dk_raw, dg_ = _head_norm_bwd(k_, g_, dk_)
        return jnp.concatenate([dk_raw, dv_], axis=1), dg_

    dkvf, dkg = act_call(kv_bwd_fn, [k_raw, kg, dk, dv], [(2 * d, BF16, "tile"), (d, F32, "all")], "k_dnorm")
    dkvf = dkvf.reshape(t, 2 * d)
    dhkv = mm_nt_col(dkvf, wkv, "kv_dproj").reshape(bsz, seq, d)
    dwkv = mm_tn(hkv.reshape(t, d), dkvf, "c", N_DEV, "kv_dw")
    (dwo, dwq, dwkv), att_token = exchange_start([(dwo, True), (dwq, True), (dwkv, True)], "att_dw_start")
    dx2, dkv_sh, dkv_sc, dgn_kv, dff0, dg_m0 = act_call(
        _norm_then_gate_bwd, [x2, par(kv_norm_g) + att_token[0, 0], kv_sc, dhkv, dx2, mlp0_saved[2], g_m],
        _norm_bwd_outs(d) + _gate_bwd_outs(d), "kv_dnorm")

    dx1, dw1_0, dw2_0, (dsh_m0, dsc_m0, dgn_mlp0), _ = mlp_bwd(
        dx2, dff0, x1, par(mlp_norm_g[0]), sc_m, w1_0, w2_0, mlp0_saved, "mlp0")

    def glu_bwd_fn(do_, val, gt, g_):
        sig = jax.nn.sigmoid(gt)
        dmix = g_ * do_
        dz = jnp.concatenate([dmix * sig, dmix * val * sig * (1.0 - sig)], axis=1)
        return dz, _rowsum(do_ * (val * sig))

    dz, dg_a0 = act_call(glu_bwd_fn, [dx1, z_val, z_gate, g_a], [(2 * d, BF16, "tile"), (d, F32, "seq")], "glu_dres")
    dz = dz.reshape(t, 2 * d)
    dy = mm_nt_col(dz, w_glu, "glu_dup", (F32,), lambda acc, y_: (acc * _gelu_grad(y_),),
                   (y.reshape(t, d),)).reshape(bsz, seq, d)
    dwglu = mm_tn(ge.reshape(t, d), dz, "c", N_DEV, "glu_dw")
    (dwglu,), glu_token = exchange_start([(dwglu, True)], "glu_dw_start")
    du, dbbd_re, dbbd_im, dcbd_re, dcbd_im, dab_re, dab_im, dd_skip = s5_bwd(
        dy, h0, st_re, st_im, bbd_re, bbd_im, cbd_re, cbd_im, abr, abi, d_skip + glu_token[0, 0])
    dx0, dsh_a0, dsc_a0, dgn_mix0 = act_call(
        _norm_mod_bwd, [x, par(mix_norm_g[0]), sc_a, du, dx1],
        [(d, F32, "tile"), (d, F32, "seq"), (d, F32, "seq"), (d, F32, "all")], "mix0_dnorm")

    from_bbd = lambda m: jnp.swapaxes(m.reshape(nb, S5_BLOCK_GROUPS, S5_GROUP, S5_STATE), 2, 3).reshape(gp, S5_GROUP)
    d_c = lambda m: jnp.swapaxes(m.reshape(nb, S5_BLOCK_GROUPS, S5_STATE, S5_GROUP), 2, 3).reshape(
        1, n_groups, S5_GROUP, S5_STATE)
    d_lam_re, d_lam_im, d_log_dt, d_b_re, d_b_im = s5_prep_bwd(
        lam_re, lam_im, log_dt, b_re, b_im, dab_re.reshape(gp, 1), dab_im.reshape(gp, 1),
        from_bbd(dbbd_re), from_bbd(dbbd_im))

    small = all_reduce_small([
        jnp.stack([dgn_mix0.reshape(d), dgn_mix1.reshape(d)]),
        jnp.stack([dgn_mlp0.reshape(d), dgn_mlp1.reshape(d)]),
        d_lam_re.reshape(1, n_groups, S5_STATE), d_lam_im.reshape(1, n_groups, S5_STATE),
        d_log_dt.reshape(1, n_groups, S5_STATE).sum(axis=-1),
        d_b_re.reshape(s5_b_re.shape), d_b_im.reshape(s5_b_im.shape),
        d_c(dcbd_re), d_c(dcbd_im),
        dd_skip.reshape(1, d),
        dgn_kv.reshape(d),
        dkg.reshape(d // HEAD_DIM, HEAD_DIM).sum(axis=0),
        dqg.reshape(d // HEAD_DIM, HEAD_DIM).sum(axis=0)[None, :],
        loss_part[0, 0, :1],
    ], "small_grads")
    (g_mix_norm, g_mlp_norm, g_a_re, g_a_im, g_log_dt, g_b_re, g_b_im, g_c_re, g_c_im,
     g_skip_full, g_kv_norm, g_k_norm, g_q_norm, loss_all) = small
    loss = loss_all[0]
    g_s5_d = lax.dynamic_slice_in_dim(g_skip_full, dev * (d // N_DEV), d // N_DEV, axis=1)

    dm_mine = jnp.concatenate([
        dsh_a0, dsc_a0, dg_a0, dsh_m0, dsc_m0, dg_m0,
        dsh_a1, dsc_a1, dg_a1, dsh_m1, dsc_m1, dg_m1, dkv_sh, dkv_sc], axis=2).reshape(bsz, 2 * e_ada + e_kv)
    dm_all = all_gather(dm_mine, "gather_dmod").reshape(N_DEV * bsz, 2 * e_ada + e_kv)
    dm_cols = jnp.concatenate([
        lax.dynamic_slice_in_dim(dm_all, dev * n_ada, n_ada, axis=1),
        lax.dynamic_slice_in_dim(dm_all, e_ada + dev * n_ada, n_ada, axis=1),
        lax.dynamic_slice_in_dim(dm_all, 2 * e_ada + dev * n_kv, n_kv, axis=1)], axis=1)
    dw_cols, db_all = ada_bwd(c_all, dm_cols, dm_all)
    g_ada_w = jnp.stack([dw_cols[:, :n_ada], dw_cols[:, n_ada:2 * n_ada]])
    g_kv_ada_w = dw_cols[:, 2 * n_ada:]
    g_ada_b = db_all[0, :2 * e_ada].reshape(2, e_ada)
    g_kv_ada_b = db_all[0, 2 * e_ada:]

    landed = lambda handle, name: slab_sum(exchange_wait(handle, dx0, name + "_wait"), name + "_sum")
    g_w1 = jnp.stack([landed(dw1_0, "rs_w1_0"), landed(dw1_1, "rs_w1_1")])
    g_w2 = jnp.stack([landed(dw2_0, "rs_w2_0"), landed(dw2_1, "rs_w2_1")])
    g_glu = landed(dwglu, "rs_glu")[None]
    g_wkv = landed(dwkv, "rs_wkv")
    g_wq = landed(dwq, "rs_wq")[None]
    g_wo = landed(dwo, "rs_wo")[None]

    weights = [ada_w, ada_b, mix_norm_g, mlp_norm_g, mlp_w1, mlp_w2, s5_a_re, s5_a_im, s5_log_dt, s5_b_re,
               s5_b_im, s5_c_re, s5_c_im, s5_d, s5_w_glu, kv_ada_w, kv_ada_b, kv_norm_g, w_kv, k_norm_g,
               sb_w_q, q_norm_g, sb_w_o]
    grads = [g_ada_w, g_ada_b, g_mix_norm, g_mlp_norm, g_w1, g_w2, g_a_re, g_a_im, g_log_dt, g_b_re,
             g_b_im, g_c_re, g_c_im, g_s5_d, g_glu, g_kv_ada_w, g_kv_ada_b, g_kv_norm, g_wkv, g_k_norm,
             g_wq, g_q_norm, g_wo]
    ms = [m_ada_w, m_ada_b, m_mix_norm_g, m_mlp_norm_g, m_mlp_w1, m_mlp_w2, m_s5_a_re, m_s5_a_im, m_s5_log_dt,
          m_s5_b_re, m_s5_b_im, m_s5_c_re, m_s5_c_im, m_s5_d, m_s5_w_glu, m_kv_ada_w, m_kv_ada_b, m_kv_norm_g,
          m_w_kv, m_k_norm_g, m_sb_w_q, m_q_norm_g, m_sb_w_o]
    vs = [v_ada_w, v_ada_b, v_mix_norm_g, v_mlp_norm_g, v_mlp_w1, v_mlp_w2, v_s5_a_re, v_s5_a_im, v_s5_log_dt,
          v_s5_b_re, v_s5_b_im, v_s5_c_re, v_s5_c_im, v_s5_d, v_s5_w_glu, v_kv_ada_w, v_kv_ada_b, v_kv_norm_g,
          v_w_kv, v_k_norm_g, v_sb_w_q, v_q_norm_g, v_sb_w_o]
    grads = [g.reshape(w.shape) for g, w in zip(grads, weights)]
    deltas, new_ms, new_vs = [], [], []
    for i, (w, g, m, v) in enumerate(zip(weights, grads, ms, vs)):
        dl, nm, nv = adamw(w, g, m, v, f"adamw_{i}")
        deltas.append(dl)
        new_ms.append(nm)
        new_vs.append(nv)
    return (loss, dx0, *grads, *deltas, *new_ms, *new_vs)
```

```python
import functools
import math

import jax
import jax.numpy as jnp
from jax import lax
from jax.experimental import pallas as pl
from jax.experimental.pallas import tpu as pltpu

F32 = jnp.float32
BF16 = jnp.bfloat16

N_DEV = 8
MESH = pl.DeviceIdType.MESH
ANY = pl.BlockSpec(memory_space=pl.ANY)

LANES = 128
VMEM_LIMIT_BYTES = 48 * 2 ** 20
TILE_BUDGET_BYTES = 4 * 2 ** 20

S5_GROUP = 16
S5_STATE = 64
S5_BLOCK_GROUPS = 16
HEAD_DIM = 64
ATT_BLOCK = 128
EPS = 1e-6

ADAM_LR = 0.001
ADAM_B1 = 0.9
ADAM_B2 = 0.999
ADAM_EPS = 1e-08
ADAM_WD = 0.01
ADAM_STEP = 10


def _cparams(*sem):
    return pltpu.CompilerParams(dimension_semantics=sem, vmem_limit_bytes=VMEM_LIMIT_BYTES)


def _divisor_tile(n, limit, mult):
    best = None
    for t in range(mult, min(n, limit) + 1, mult):
        if n % t == 0:
            best = t
    return best if best is not None else n


def _tile_m(m):
    return _divisor_tile(m, 2048 if m >= 4096 else 256, 16)


def all_gather(x, name):
    def body(x_ref, out_ref, send_sems, recv_sems, local_sem):
        ax, ay, ac = lax.axis_index("x"), lax.axis_index("y"), lax.axis_index("c")
        me, sibling = (ax, ay, ac), (ax, ay, 1 - ac)
        chips = [(1 - ax, ay), (ax, 1 - ay), (1 - ax, 1 - ay)]

        def slot(px, py, pc):
            return out_ref.at[4 * px + 2 * py + pc]

        def copy(k, block, to, src=None):
            return pltpu.make_async_remote_copy(
                src_ref=slot(*block) if src is None else src, dst_ref=slot(*block),
                send_sem=send_sems.at[k], recv_sem=recv_sems.at[k], device_id=to, device_id_type=MESH)

        mine = pltpu.make_async_copy(x_ref, slot(*me), local_sem)
        mine.start()
        first = [copy(0, me, sibling, src=x_ref)]
        first += [copy(1 + j, me, (*chip, ac), src=x_ref) for j, chip in enumerate(chips)]
        for cp in first:
            cp.start()
        passed = [copy(4 + j, (*chip, ac), sibling) for j, chip in enumerate(chips)]
        for j, chip in enumerate(chips):
            copy(1 + j, (*chip, ac), me).wait_recv()
            passed[j].start()
        copy(0, sibling, me).wait_recv()
        for j, chip in enumerate(chips):
            copy(4 + j, (*chip, 1 - ac), me).wait_recv()
        for cp in first + passed:
            cp.wait_send()
        mine.wait()

    return pl.pallas_call(
        body, name=name,
        out_shape=jax.ShapeDtypeStruct((N_DEV,) + x.shape, x.dtype),
        in_specs=[ANY], out_specs=ANY,
        scratch_shapes=[pltpu.SemaphoreType.DMA((7,)), pltpu.SemaphoreType.DMA((7,)), pltpu.SemaphoreType.DMA],
    )(x)


HBM = pl.BlockSpec(memory_space=pltpu.HBM)
SEM = pl.BlockSpec(memory_space=pltpu.SEMAPHORE)
N_PEERS = N_DEV - 1


def _peers():
    ax, ay, ac = lax.axis_index("x"), lax.axis_index("y"), lax.axis_index("c")
    flip = lambda v, bit: 1 - v if bit else v
    return [(flip(ax, k & 4), flip(ay, k & 2), flip(ac, k & 1)) for k in range(1, N_DEV)]


def _dev_index(pos):
    return 4 * pos[0] + 2 * pos[1] + pos[2]


def exchange_start(items, name, after=None):
    n = len(items)
    srcs = [a for a, _ in items]
    blocks = [a.shape[1:] if scatter else a.shape for a, scatter in items]
    extra = list(after or ())

    def body(*refs):
        src_refs, land_refs = refs[:n], refs[n:2 * n]
        outs = refs[2 * n + len(extra):]
        send_sems, recv_sems = outs[:n], outs[n:2 * n]
        token = outs[-1]
        me = _dev_index((lax.axis_index("x"), lax.axis_index("y"), lax.axis_index("c")))
        for w, (_, scatter) in enumerate(items):
            for k, peer in enumerate(_peers()):
                src = src_refs[w].at[_dev_index(peer)] if scatter else src_refs[w]
                pltpu.make_async_remote_copy(
                    src_ref=src, dst_ref=land_refs[w].at[me], send_sem=send_sems[w].at[k],
                    recv_sem=recv_sems[w].at[k], device_id=peer, device_id_type=MESH).start()
        token[...] = jnp.zeros_like(token)

    lands = [lax.empty((N_DEV,) + blk, a.dtype) for a, blk in zip(srcs, blocks)]
    res = pl.pallas_call(
        body, name=name,
        out_shape=([pltpu.SemaphoreType.DMA((N_PEERS,))] * (2 * n)
                   + [pltpu.HBM(a.shape, a.dtype) for a in srcs] + [pltpu.HBM(l.shape, l.dtype) for l in lands]
                   + [jax.ShapeDtypeStruct((8, LANES), F32)]),
        in_specs=[HBM] * (2 * n) + [ANY] * len(extra),
        out_specs=[SEM] * (2 * n) + [HBM] * (2 * n) + [pl.BlockSpec(memory_space=pltpu.VMEM)],
        input_output_aliases={i: 2 * n + i for i in range(2 * n)},
        compiler_params=pltpu.CompilerParams(has_side_effects=pltpu.SideEffectType.DATAFLOW_SIDE_EFFECTING),
    )(*[pltpu.with_memory_space_constraint(a, pltpu.HBM) for a in srcs + lands], *extra)
    handles = [(res[w], res[n + w], res[2 * n + w], res[3 * n + w], scatter) for w, (_, scatter) in enumerate(items)]
    return handles, res[-1]


def exchange_wait(handle, after, name):
    send_sem, recv_sem, src, land, scatter = handle

    def body(src_ref, land_ref, send_ref, recv_ref, after_ref, src_out, land_out):
        for k, peer in enumerate(_peers()):
            slot = _dev_index(peer)
            copy = pltpu.make_async_remote_copy(
                src_ref=src_ref.at[slot] if scatter else src_ref, dst_ref=land_ref.at[slot],
                send_sem=send_ref.at[k], recv_sem=recv_ref.at[k], device_id=peer, device_id_type=MESH)
            copy.wait_send()
            copy.wait_recv()

    src, landed = pl.pallas_call(
        body, name=name,
        out_shape=(pltpu.HBM(src.shape, src.dtype), pltpu.HBM(land.shape, land.dtype)),
        in_specs=[HBM, HBM, SEM, SEM, ANY], out_specs=(HBM, HBM), input_output_aliases={0: 0, 1: 1},
        compiler_params=pltpu.CompilerParams(has_side_effects=pltpu.SideEffectType.DATAFLOW_SIDE_EFFECTING),
    )(src, land, send_sem, recv_sem, after)
    dev = _dev_index((lax.axis_index("x"), lax.axis_index("y"), lax.axis_index("c")))
    own = lax.dynamic_index_in_dim(src, dev, axis=0, keepdims=True) if scatter else src[None]
    return lax.dynamic_update_slice_in_dim(landed, own, dev, axis=0)


def rows_call(fn, ins, outs, name):
    rows = ins[0].shape[1]
    per_row = sum(a.shape[0] * a.shape[2] * a.dtype.itemsize for a in ins)
    per_row += sum(l * c * jnp.dtype(dt).itemsize for l, c, dt in outs)
    tr = _divisor_tile(rows, max(16, TILE_BUDGET_BYTES // per_row), 16)
    n_in = len(ins)

    def body(*refs):
        vals = fn(*[r[...] for r in refs[:n_in]])
        if not isinstance(vals, (tuple, list)):
            vals = (vals,)
        for r, v in zip(refs[n_in:], vals):
            r[...] = v.astype(r.dtype)

    def spec(l, c):
        return pl.BlockSpec((l, tr, c), lambda i: (0, i, 0))

    res = pl.pallas_call(
        body, name=name, grid=(rows // tr,),
        in_specs=[spec(a.shape[0], a.shape[2]) for a in ins],
        out_specs=[spec(l, c) for l, c, _ in outs],
        out_shape=[jax.ShapeDtypeStruct((l, rows, c), dt) for l, c, dt in outs],
        compiler_params=_cparams("arbitrary"),
    )(*ins)
    return res


def _as_rows(a, lead=0):
    shape = a.shape
    l = int(math.prod(shape[:lead])) if lead else 1
    rest = shape[lead:]
    c = rest[-1] if rest else 1
    r = int(math.prod(rest[:-1])) if len(rest) > 1 else 1
    return a.reshape(l, r, c)


def act_call(fn, ins, outs, name):
    bsz, seq = ins[0].shape[0], ins[0].shape[1]
    per_row = sum(a.shape[2] * a.dtype.itemsize for a in ins if a.shape[1] == seq)
    per_row += sum(c * jnp.dtype(dt).itemsize for c, dt, kind in outs if kind == "tile")
    ts = _divisor_tile(seq, max(16, TILE_BUDGET_BYTES // per_row), 16)
    n_in = len(ins)

    def in_spec(a):
        c = a.shape[2]
        if a.shape[1] == seq:
            return pl.BlockSpec((None, ts, c), lambda b, s: (b, s, 0))
        if a.shape[0] == bsz:
            return pl.BlockSpec((None, 1, c), lambda b, s: (b, 0, 0))
        return pl.BlockSpec((None, 1, c), lambda b, s: (0, 0, 0))

    def out_spec(c, kind):
        if kind == "tile":
            return pl.BlockSpec((None, ts, c), lambda b, s: (b, s, 0))
        if kind == "seq":
            return pl.BlockSpec((None, 1, c), lambda b, s: (b, 0, 0))
        return pl.BlockSpec((None, 1, c), lambda b, s: (0, 0, 0))

    def out_shape(c, dt, kind):
        if kind == "tile":
            return jax.ShapeDtypeStruct((bsz, seq, c), dt)
        return jax.ShapeDtypeStruct((bsz if kind == "seq" else 1, 1, c), dt)

    def accumulate(ref, v, first):
        @pl.when(first)
        def _():
            ref[...] = jnp.zeros_like(ref)

        ref[...] += v.astype(ref.dtype)

    def body(*refs):
        b, s = pl.program_id(0), pl.program_id(1)
        vals = fn(*[r[...] for r in refs[:n_in]])
        if not isinstance(vals, (tuple, list)):
            vals = (vals,)
        for ref, v, (_, _, kind) in zip(refs[n_in:], vals, outs):
            if kind == "tile":
                ref[...] = v.astype(ref.dtype)
            elif kind == "seq":
                accumulate(ref, v, s == 0)
            else:
                accumulate(ref, v, jnp.logical_and(b == 0, s == 0))

    return pl.pallas_call(
        body, name=name, grid=(bsz, seq // ts),
        in_specs=[in_spec(a) for a in ins],
        out_specs=[out_spec(c, kind) for c, _, kind in outs],
        out_shape=[out_shape(*o) for o in outs],
        compiler_params=_cparams("arbitrary", "arbitrary"),
    )(*ins)


def _mm(name, grid, a, a_spec, b, b_spec, dims, out_shape, out_spec, out_dtypes, acc_steps,
        epi=None, extras=(), extra_specs=()):
    n_ex, n_out = len(extras), len(out_dtypes)
    tile = tuple(d for d in out_spec.block_shape if d is not None)

    def body(*refs):
        a_ref, b_ref = refs[0], refs[1]
        ex_refs = refs[2:2 + n_ex]
        o_refs = refs[2 + n_ex:2 + n_ex + n_out]
        def product():
            return lax.dot_general(a_ref[...].astype(BF16), b_ref[...].astype(BF16), (dims, ((), ())),
                                   preferred_element_type=F32)

        def finish(acc):
            vals = epi(acc, *[r[...] for r in ex_refs]) if epi is not None else (acc,) * n_out
            for r, v in zip(o_refs, vals):
                r[...] = v.astype(r.dtype)

        if not acc_steps:
            finish(product())
        else:
            acc_ref = refs[-1]
            s = pl.program_id(1)

            @pl.when(s == 0)
            def _():
                acc_ref[...] = jnp.zeros_like(acc_ref)

            acc_ref[...] += product()

            @pl.when(s == acc_steps - 1)
            def _():
                finish(acc_ref[...])

    res = pl.pallas_call(
        body, name=name, grid=grid,
        in_specs=[a_spec, b_spec] + list(extra_specs),
        out_specs=[out_spec] * n_out,
        out_shape=[jax.ShapeDtypeStruct(out_shape, dt) for dt in out_dtypes],
        scratch_shapes=[pltpu.VMEM(tile, F32)] if acc_steps else [],
        compiler_params=_cparams("arbitrary", "arbitrary"),
    )(a, b, *extras)
    return res if n_out > 1 else res[0]


def mm_nn_col(a, w, name, out_dtypes=(F32,), epi=None, slabs=None, col_params=()):
    m, k = a.shape
    _, _, nb = w.shape
    first, count = slabs if slabs is not None else (0, w.shape[0])
    tm = _tile_m(m)
    return _mm(name, (m // tm, count), a, pl.BlockSpec((tm, k), lambda i, j: (i, 0)),
               w, pl.BlockSpec((None, k, nb), lambda i, j: (first + j, 0, 0)), ((1,), (0,)),
               (m, count * nb), pl.BlockSpec((tm, nb), lambda i, j: (i, j)), out_dtypes, 0, epi,
               col_params, [pl.BlockSpec((1, nb), lambda i, j: (0, j))] * len(col_params))


def mm_nn_row(a, w, name, out_dtypes=(F32,), epi=None, extras=(), extra_specs=None, tm=None):
    m = a.shape[0]
    ns, kb, n = w.shape
    tm = tm or _tile_m(m)
    return _mm(name, (m // tm, ns), a, pl.BlockSpec((tm, kb), lambda i, s: (i, s)),
               w, pl.BlockSpec((None, kb, n), lambda i, s: (s, 0, 0)), ((1,), (0,)),
               (m, n), pl.BlockSpec((tm, n), lambda i, s: (i, 0)), out_dtypes, ns if ns > 1 else 0, epi,
               extras, extra_specs(tm) if extras else ())


def mm_nt_col(dc, w, name, out_dtypes=(F32,), epi=None, extras=()):
    m = dc.shape[0]
    ns, k, nb = w.shape
    tm = _divisor_tile(m, 512, 16)
    n_ex, n_out = len(extras), len(out_dtypes)

    def body(dc_ref, w_ref, *rest):
        ex_refs, o_refs = rest[:n_ex], rest[n_ex:]
        w_all = jnp.concatenate([w_ref[s] for s in range(ns)], axis=1)
        acc = lax.dot_general(dc_ref[...].astype(BF16), w_all, (((1,), (1,)), ((), ())), preferred_element_type=F32)
        vals = epi(acc, *[r[...] for r in ex_refs]) if epi is not None else (acc,) * n_out
        for r, v in zip(o_refs, vals):
            r[...] = v.astype(r.dtype)

    rows = pl.BlockSpec((tm, k), lambda i: (i, 0))
    res = pl.pallas_call(
        body, name=name, grid=(m // tm,),
        in_specs=[pl.BlockSpec((tm, ns * nb), lambda i: (i, 0)), pl.BlockSpec((ns, k, nb), lambda i: (0, 0, 0))]
        + [rows] * n_ex,
        out_specs=[rows] * n_out,
        out_shape=[jax.ShapeDtypeStruct((m, k), dt) for dt in out_dtypes],
        compiler_params=_cparams("arbitrary"),
    )(dc, w, *extras)
    return res if n_out > 1 else res[0]


def mm_nt_row(dc, w, name, out_dtypes=(F32,), epi=None, extras=()):
    m, n = dc.shape
    ns, kb, _ = w.shape
    tm = _tile_m(m)
    spec = pl.BlockSpec((tm, kb), lambda i, s: (i, s))
    return _mm(name, (m // tm, ns), dc, pl.BlockSpec((tm, n), lambda i, s: (i, 0)),
               w, pl.BlockSpec((None, kb, n), lambda i, s: (s, 0, 0)), ((1,), (1,)),
               (m, ns * kb), spec, out_dtypes, 0, epi, extras, [spec] * len(extras))


def mm_tn(a, c, slab, ns, name, out_dtype=BF16):
    m, ka_all = a.shape
    nc_all = c.shape[1]
    ka = ka_all // ns if slab == "a" else ka_all
    nc = nc_all // ns if slab == "c" else nc_all
    tt = _divisor_tile(m, 1024, 16)
    steps = m // tt

    def body(a_ref, c_ref, o_ref, acc_ref):
        t = pl.program_id(0)

        @pl.when(t == 0)
        def _():
            acc_ref[...] = jnp.zeros_like(acc_ref)

        for s in range(ns):
            a_s = a_ref[:, s * ka:(s + 1) * ka] if slab == "a" else a_ref[...]
            c_s = c_ref[:, s * nc:(s + 1) * nc] if slab == "c" else c_ref[...]
            acc_ref[s] += lax.dot_general(a_s.astype(BF16), c_s.astype(BF16), (((0,), (0,)), ((), ())),
                                          preferred_element_type=F32)

        @pl.when(t == steps - 1)
        def _():
            o_ref[...] = acc_ref[...].astype(o_ref.dtype)

    return pl.pallas_call(
        body, name=name, grid=(steps,),
        in_specs=[pl.BlockSpec((tt, ka_all), lambda t: (t, 0)), pl.BlockSpec((tt, nc_all), lambda t: (t, 0))],
        out_specs=pl.BlockSpec((ns, ka, nc), lambda t: (0, 0, 0), pipeline_mode=pl.Buffered(1)),
        out_shape=jax.ShapeDtypeStruct((ns, ka, nc), out_dtype),
        scratch_shapes=[pltpu.VMEM((ns, ka, nc), F32)],
        compiler_params=_cparams("arbitrary"),
    )(a, c)


def slab_sum(landed, name):
    shape = landed.shape[1:]
    total = rows_call(lambda g: jnp.sum(g.astype(F32), axis=0, keepdims=True),
                      [_as_rows(landed, 1)], [(1, shape[-1], F32)], name)[0]
    return total.reshape(shape)


def all_reduce_small(leaves, name):
    sizes = [int(a.size) for a in leaves]
    flat = jnp.concatenate([a.reshape(-1) for a in leaves])
    total = int(flat.size)
    chunk = N_DEV * 16 * LANES
    padded = -(-total // chunk) * chunk
    parts = jnp.pad(flat, (0, padded - total)).reshape(N_DEV, padded // (N_DEV * LANES), LANES)
    (scatter,), _ = exchange_start([(parts, True)], name + "_scatter")
    landed = exchange_wait(scatter, parts, name + "_scatter_wait")
    mine = rows_call(lambda g: jnp.sum(g, axis=0, keepdims=True), [landed], [(1, LANES, F32)], name + "_sum")[0][0]
    (gather,), _ = exchange_start([(mine, False)], name + "_gather")
    summed = exchange_wait(gather, mine, name + "_gather_wait").reshape(-1)
    out, at = [], 0
    for a, n in zip(leaves, sizes):
        out.append(summed[at:at + n].reshape(a.shape))
        at += n
    return out


def adamw(w, g, m, v, name):
    c = w.shape[-1] if w.ndim else 1

    def fn(w_, g_, m_, v_):
        nm = ADAM_B1 * m_ + (1.0 - ADAM_B1) * g_
        nv = ADAM_B2 * v_ + (1.0 - ADAM_B2) * (g_ * g_)
        m_hat = nm / (1.0 - ADAM_B1 ** ADAM_STEP)
        v_hat = nv / (1.0 - ADAM_B2 ** ADAM_STEP)
        delta = -ADAM_LR * (m_hat / (jnp.sqrt(v_hat) + ADAM_EPS) + ADAM_WD * w_)
        return delta, nm, nv

    res = rows_call(fn, [_as_rows(t) for t in (w, g.astype(F32), m, v)], [(1, c, F32)] * 3, name)
    return tuple(r.reshape(w.shape) for r in res)


def _rowsum(v):
    return jnp.sum(v, axis=0, keepdims=True)


def _norm_mod(x, g, sh, sc):
    n = x * lax.rsqrt(jnp.mean(x * x, axis=-1, keepdims=True) + EPS)
    return (n * g) * (1.0 + sc) + sh


def _norm_mod_bwd(x, g, sc, dh, dres):
    r = lax.rsqrt(jnp.mean(x * x, axis=-1, keepdims=True) + EPS)
    n = x * r
    dy = dh * (1.0 + sc)
    dn = dy * g
    dx = r * (dn - n * jnp.mean(dn * n, axis=-1, keepdims=True))
    return dres + dx, _rowsum(dh), _rowsum(dh * (n * g)), _rowsum(dy * n)


def _head_mean(v):
    low = lax.broadcasted_iota(jnp.int32, (1, LANES), 1) < HEAD_DIM
    parts = []
    for p in range(v.shape[1] // LANES):
        blk = v[:, p * LANES:(p + 1) * LANES]
        s0 = jnp.sum(jnp.where(low, blk, 0.0), axis=-1, keepdims=True)
        s1 = jnp.sum(jnp.where(low, 0.0, blk), axis=-1, keepdims=True)
        parts.append(jnp.where(low, s0, s1))
    return jnp.concatenate(parts, axis=1) * (1.0 / HEAD_DIM)


def _head_norm(x, g):
    return x * lax.rsqrt(_head_mean(x * x) + EPS) * g


def _head_norm_bwd(x, g, dy):
    r = lax.rsqrt(_head_mean(x * x) + EPS)
    n = x * r
    dn = dy * g
    return r * (dn - n * _head_mean(dn * n)), _rowsum(dy * n)


GELU_C = math.sqrt(2.0 / math.pi)
GELU_A = 0.044715


def _gelu_grad(y):
    t = jnp.tanh(GELU_C * (y + GELU_A * y * y * y))
    return 0.5 * (1.0 + t) + 0.5 * y * (1.0 - t * t) * GELU_C * (1.0 + 3.0 * GELU_A * y * y)


def ada_fwd(c_all, w_cols, b_cols):
    def body(c_ref, w_ref, b_ref, o_ref):
        c = c_ref[...]
        s = (c * jax.nn.sigmoid(c)).astype(BF16)
        o_ref[...] = jnp.dot(s, w_ref[...].astype(BF16), preferred_element_type=F32) + b_ref[...]

    return pl.pallas_call(
        body, name="ada_fwd", out_shape=jax.ShapeDtypeStruct((c_all.shape[0], w_cols.shape[1]), F32),
        compiler_params=pltpu.CompilerParams(vmem_limit_bytes=VMEM_LIMIT_BYTES),
    )(c_all, w_cols, b_cols)


def ada_bwd(c_all, dm_cols, dm_all):
    def body(c_ref, d_ref, all_ref, dw_ref, db_ref):
        c = c_ref[...]
        s = (c * jax.nn.sigmoid(c)).astype(BF16)
        dw_ref[...] = lax.dot_general(s, d_ref[...].astype(BF16), (((0,), (0,)), ((), ())),
                                      preferred_element_type=F32)
        db_ref[...] = jnp.sum(all_ref[...], axis=0, keepdims=True)

    return pl.pallas_call(
        body, name="ada_bwd",
        out_shape=[jax.ShapeDtypeStruct((c_all.shape[1], dm_cols.shape[1]), F32),
                   jax.ShapeDtypeStruct((1, dm_all.shape[1]), F32)],
        compiler_params=pltpu.CompilerParams(vmem_limit_bytes=VMEM_LIMIT_BYTES),
    )(c_all, dm_cols, dm_all)


def _s5_discretise(lam_re, lam_im, log_dt, b_re, b_im):
    dt = jnp.exp(log_dt)
    mag = jnp.exp(lam_re * dt)
    ab_re = mag * jnp.cos(lam_im * dt)
    ab_im = mag * jnp.sin(lam_im * dt)
    den = lam_re * lam_re + lam_im * lam_im
    nr = ab_re - 1.0
    ni = ab_im
    f_re = (nr * lam_re + ni * lam_im) / den
    f_im = (ni * lam_re - nr * lam_im) / den
    bb_re = f_re * b_re - f_im * b_im
    bb_im = f_re * b_im + f_im * b_re
    return ab_re, ab_im, bb_re, bb_im


def s5_prep(lam_re, lam_im, log_dt, b_re, b_im):
    h, gp = b_re.shape

    def body(lr, li, ld, br, bi, o_ar, o_ai, o_br, o_bi):
        res = _s5_discretise(lr[...], li[...], ld[...], br[...], bi[...])
        for r, v in zip((o_ar, o_ai, o_br, o_bi), res):
            r[...] = v

    col, mat = jax.ShapeDtypeStruct((1, gp), F32), jax.ShapeDtypeStruct((h, gp), F32)
    return pl.pallas_call(body, name="s5_prep", out_shape=[col, col, mat, mat])(lam_re, lam_im, log_dt, b_re, b_im)


def s5_prep_bwd(lam_re, lam_im, log_dt, b_re, b_im, d_ab_re, d_ab_im, d_bb_re, d_bb_im):
    h, gp = b_re.shape

    def body(lr, li, ld, br, bi, g_ar, g_ai, g_br, g_bi, o_lr, o_li, o_ld, o_br, o_bi):
        _, vjp = jax.vjp(_s5_discretise, lr[...], li[...], ld[...], br[...], bi[...])
        res = vjp((g_ar[...], g_ai[...], g_br[...], g_bi[...]))
        for r, v in zip((o_lr, o_li, o_ld, o_br, o_bi), res):
            r[...] = v

    col, mat = jax.ShapeDtypeStruct((1, gp), F32), jax.ShapeDtypeStruct((h, gp), F32)
    return pl.pallas_call(body, name="s5_prep_bwd", out_shape=[col, col, col, mat, mat])(
        lam_re, lam_im, log_dt, b_re, b_im, d_ab_re, d_ab_im, d_bb_re, d_bb_im)


def _s5_chunk(seq):
    return _divisor_tile(seq, 256, 16)


def s5_fwd(u, bbd_re, bbd_im, cbd_re, cbd_im, ab_re, ab_im, dskip):
    bsz, seq, d = u.shape
    nb, cb, ns = bbd_re.shape
    lc = _s5_chunk(seq)

    def body(u_ref, bre_ref, bim_ref, cre_ref, cim_ref, ar_ref, ai_ref, d_ref, y_ref, ge_ref, sre_ref, sim_ref,
             carry_re, carry_im):
        t = pl.program_id(1)

        @pl.when(t == 0)
        def _():
            carry_re[...] = jnp.zeros_like(carry_re)
            carry_im[...] = jnp.zeros_like(carry_im)

        for b in range(bsz):
            ub = u_ref[b].astype(BF16)
            sre_ref[b] = jnp.dot(ub, bre_ref[...].astype(BF16), preferred_element_type=F32)
            sim_ref[b] = jnp.dot(ub, bim_ref[...].astype(BF16), preferred_element_type=F32)
        ar, ai = ar_ref[...], ai_ref[...]

        def step(i, carry):
            row = pl.ds(i, 1)
            out = []
            for b, (cr, ci) in enumerate(carry):
                nr = ar * cr - ai * ci + sre_ref[b, row, :]
                ni = ar * ci + ai * cr + sim_ref[b, row, :]
                sre_ref[b, row, :] = nr
                sim_ref[b, row, :] = ni
                out.append((nr, ni))
            return tuple(out)

        init = tuple((carry_re[b], carry_im[b]) for b in range(bsz))
        last = lax.fori_loop(0, lc, step, init, unroll=8)
        for b, (cr, ci) in enumerate(last):
            carry_re[b] = cr
            carry_im[b] = ci
            y = jnp.dot(sre_ref[b].astype(BF16), cre_ref[...].astype(BF16), preferred_element_type=F32)
            y -= jnp.dot(sim_ref[b].astype(BF16), cim_ref[...].astype(BF16), preferred_element_type=F32)
            y = y + d_ref[...] * u_ref[b]
            y_ref[b] = y
            ge_ref[b] = jax.nn.gelu(y).astype(BF16)

    chan = pl.BlockSpec((bsz, lc, cb), lambda n, t: (0, t, n))
    state = pl.BlockSpec((bsz, lc, ns), lambda n, t: (0, t, n))
    par = lambda r, c: pl.BlockSpec((None, r, c), lambda n, t: (n, 0, 0))
    return pl.pallas_call(
        body, name="s5_fwd", grid=(nb, seq // lc),
        in_specs=[chan, par(cb, ns), par(cb, ns), par(ns, cb), par(ns, cb), par(1, ns), par(1, ns),
                  pl.BlockSpec((None, 1, cb), lambda n, t: (0, 0, n))],
        out_specs=[chan, chan, state, state],
        out_shape=[jax.ShapeDtypeStruct((bsz, seq, d), F32), jax.ShapeDtypeStruct((bsz, seq, d), BF16),
                   jax.ShapeDtypeStruct((bsz, seq, nb * ns), F32),
                   jax.ShapeDtypeStruct((bsz, seq, nb * ns), F32)],
        scratch_shapes=[pltpu.VMEM((bsz, 1, ns), F32), pltpu.VMEM((bsz, 1, ns), F32)],
        compiler_params=_cparams("arbitrary", "arbitrary"),
    )(u, bbd_re, bbd_im, cbd_re, cbd_im, ab_re, ab_im, dskip)


def s5_bwd(dy, u, st_re, st_im, bbd_re, bbd_im, cbd_re, cbd_im, ab_re, ab_im, dskip):
    bsz, seq, d = u.shape
    nb, cb, ns = bbd_re.shape
    lc = _s5_chunk(seq)
    nc = seq // lc

    def body(dy_ref, u_ref, sre_ref, sim_ref, bre_ref, bim_ref, cre_ref, cim_ref, ar_ref, ai_ref, d_ref,
             du_ref, dbre_out, dbim_out, dcre_out, dcim_out, dar_ref, dai_ref, dd_ref,
             g_re, g_im, gs_re, gs_im, carry_re, carry_im, dbre_ref, dbim_ref, dcre_ref, dcim_ref):
        t = pl.program_id(1)

        @pl.when(t == 0)
        def _():
            for r in (dbre_ref, dbim_ref, dcre_ref, dcim_ref, dar_ref, dai_ref, dd_ref, carry_re, carry_im):
                r[...] = jnp.zeros_like(r)

        nt = (((1,), (1,)), ((), ()))
        tn = (((0,), (0,)), ((), ()))
        for b in range(bsz):
            dyb = dy_ref[b].astype(BF16)
            g_re[b] = lax.dot_general(dyb, cre_ref[...].astype(BF16), nt, preferred_element_type=F32)
            g_im[b] = -lax.dot_general(dyb, cim_ref[...].astype(BF16), nt, preferred_element_type=F32)
        ar, ai = ar_ref[...], ai_ref[...]

        def step(k, carry):
            row = pl.ds(lc - 1 - k, 1)
            out = []
            for b, (cr, ci) in enumerate(carry):
                gs_re[b, row, :] = cr
                gs_im[b, row, :] = ci
                nr = ar * cr + ai * ci + g_re[b, row, :]
                ni = ar * ci - ai * cr + g_im[b, row, :]
                g_re[b, row, :] = nr
                g_im[b, row, :] = ni
                out.append((nr, ni))
            return tuple(out)

        init = tuple((carry_re[b], carry_im[b]) for b in range(bsz))
        last = lax.fori_loop(0, lc, step, init, unroll=8)
        for b, (cr, ci) in enumerate(last):
            carry_re[b] = cr
            carry_im[b] = ci
            dyf, uf = dy_ref[b], u_ref[b]
            dyb, ub = dyf.astype(BF16), uf.astype(BF16)
            sr, si = sre_ref[b], sim_ref[b]
            hr, hi = gs_re[b], gs_im[b]
            dar_ref[...] += _rowsum(hr * sr + hi * si)
            dai_ref[...] += _rowsum(hi * sr - hr * si)
            gr, gi = g_re[b].astype(BF16), g_im[b].astype(BF16)
            du = lax.dot_general(gr, bre_ref[...].astype(BF16), nt, preferred_element_type=F32)
            du += lax.dot_general(gi, bim_ref[...].astype(BF16), nt, preferred_element_type=F32)
            du_ref[b] = du + d_ref[...] * dyf
            dbre_ref[...] += lax.dot_general(ub, gr, tn, preferred_element_type=F32)
            dbim_ref[...] += lax.dot_general(ub, gi, tn, preferred_element_type=F32)
            dcre_ref[...] += lax.dot_general(sr.astype(BF16), dyb, tn, preferred_element_type=F32)
            dcim_ref[...] -= lax.dot_general(si.astype(BF16), dyb, tn, preferred_element_type=F32)
            dd_ref[...] += _rowsum(dyf * uf)

        @pl.when(t == nc - 1)
        def _():
            for k in range(cb // S5_GROUP):
                chans = slice(k * S5_GROUP, (k + 1) * S5_GROUP)
                states = slice(k * S5_STATE, (k + 1) * S5_STATE)
                dbre_out[chans, :] = dbre_ref[chans, states]
                dbim_out[chans, :] = dbim_ref[chans, states]
                dcre_out[states, :] = dcre_ref[states, chans]
                dcim_out[states, :] = dcim_ref[states, chans]

    chan = pl.BlockSpec((bsz, lc, cb), lambda n, t: (0, nc - 1 - t, n))
    state = pl.BlockSpec((bsz, lc, ns), lambda n, t: (0, nc - 1 - t, n))
    par = lambda r, c: pl.BlockSpec((None, r, c), lambda n, t: (n, 0, 0))
    return pl.pallas_call(
        body, name="s5_bwd", grid=(nb, nc),
        in_specs=[chan, chan, state, state, par(cb, ns), par(cb, ns), par(ns, cb), par(ns, cb),
                  par(1, ns), par(1, ns), pl.BlockSpec((None, 1, cb), lambda n, t: (0, 0, n))],
        out_specs=[chan, par(cb, S5_STATE), par(cb, S5_STATE), par(ns, S5_GROUP), par(ns, S5_GROUP),
                   par(1, ns), par(1, ns), par(1, cb)],
        out_shape=[jax.ShapeDtypeStruct((bsz, seq, d), F32),
                   jax.ShapeDtypeStruct((nb, cb, S5_STATE), F32), jax.ShapeDtypeStruct((nb, cb, S5_STATE), F32),
                   jax.ShapeDtypeStruct((nb, ns, S5_GROUP), F32), jax.ShapeDtypeStruct((nb, ns, S5_GROUP), F32),
                   jax.ShapeDtypeStruct((nb, 1, ns), F32), jax.ShapeDtypeStruct((nb, 1, ns), F32),
                   jax.ShapeDtypeStruct((nb, 1, cb), F32)],
        scratch_shapes=([pltpu.VMEM((bsz, lc, ns), F32)] * 4 + [pltpu.VMEM((bsz, 1, ns), F32)] * 2
                        + [pltpu.VMEM((cb, ns), F32)] * 2 + [pltpu.VMEM((ns, cb), F32)] * 2),
        compiler_params=_cparams("arbitrary", "arbitrary"),
    )(dy, u, st_re, st_im, bbd_re, bbd_im, cbd_re, cbd_im, ab_re, ab_im, dskip)


ATT_HEADS = 8
ATT_LANES = ATT_HEADS * HEAD_DIM
ATT_KEYS = 2 * ATT_BLOCK
ATT_Q = 256
ATT_SCALE = 1.0 / math.sqrt(HEAD_DIM)
_NT = (((1,), (1,)), ((), ()))
_TN = (((0,), (0,)), ((), ()))
_HEADS = [slice(h * HEAD_DIM, (h + 1) * HEAD_DIM) for h in range(ATT_HEADS)]
_HALF = [slice(0, ATT_BLOCK), slice(ATT_BLOCK, ATT_KEYS)]


def _log_sigmoids(z):
    sp = jnp.log(1.0 + jnp.exp(-jnp.abs(z)))
    ls = jnp.minimum(z, 0.0) - sp
    return ls, ls - z


def _sum_matrix(after, inclusive):
    j = lax.broadcasted_iota(jnp.int32, (ATT_KEYS, ATT_KEYS), 0) % ATT_BLOCK
    s = lax.broadcasted_iota(jnp.int32, (ATT_KEYS, ATT_KEYS), 1)
    if after:
        hit = (j >= s) if inclusive else (j > s)
    else:
        hit = (j <= s) if inclusive else (j < s)
    return jnp.where(jnp.logical_or(hit, s >= ATT_BLOCK), 1.0, 0.0).astype(BF16)


def _hi_lo(v):
    hi = v.astype(BF16)
    lo = (v - hi.astype(F32)).astype(BF16)
    return jnp.concatenate([hi, lo], axis=1)


def _strict_mask(i, j):
    t = i * ATT_Q + lax.broadcasted_iota(jnp.int32, (ATT_Q, ATT_KEYS), 0)
    s = j * ATT_KEYS + lax.broadcasted_iota(jnp.int32, (ATT_Q, ATT_KEYS), 1)
    return s < t


def attention_fwd(q, k, v):
    bsz, seq, d = q.shape
    n_heads = ATT_HEADS
    lanes = n_heads * HEAD_DIM
    heads = [slice(h * HEAD_DIM, (h + 1) * HEAD_DIM) for h in range(n_heads)]

    def body(q_ref, k_ref, v_ref, o_ref, tot_ref, z_buf, ls_buf, cs_buf, acc_buf, run_buf):
        i = pl.program_id(2)
        jd = ((i + 1) * ATT_Q - 1) // ATT_KEYS
        sums = _sum_matrix(True, False)
        acc_buf[...] = jnp.zeros_like(acc_buf)
        run_buf[...] = jnp.zeros_like(run_buf)

        def block(j, masked):
            rows = pl.ds(pl.multiple_of(j * ATT_KEYS, ATT_KEYS), ATT_KEYS)
            strict = _strict_mask(i, j) if masked else None
            for h, ln in enumerate(heads):
                z_buf[h] = lax.dot_general(q_ref[:, ln], k_ref[rows, ln], _NT, preferred_element_type=F32)
            for h in range(n_heads):
                for half, cols in enumerate(_HALF):
                    ls, lf = _log_sigmoids(z_buf[h, :, cols])
                    if masked:
                        lf = jnp.where(strict[:, cols], lf, 0.0)
                    ls_buf[h, :, cols] = ls
                    cs_buf[h, half] = jnp.dot(_hi_lo(lf), sums, preferred_element_type=F32)
            for h, ln in enumerate(heads):
                run = run_buf[h]
                late, early = cs_buf[h, 1], cs_buf[h, 0]
                a1 = run + late[:, _HALF[0]]
                run = run + late[:, _HALF[1]]
                a0 = run + early[:, _HALF[0]]
                run_buf[h] = run + early[:, _HALF[1]]
                w = jnp.exp(ls_buf[h] + jnp.concatenate([a0, a1], axis=1))
                if masked:
                    w = jnp.where(strict, w, 0.0)
                acc_buf[h] += jnp.dot(w.astype(BF16), v_ref[rows, ln], preferred_element_type=F32)

        block(jd, True)

        def step(it, carry):
            block(jd - 1 - it, False)
            return carry

        lax.fori_loop(0, jd, step, 0)
        o_ref[...] = jnp.concatenate([acc_buf[h] for h in range(n_heads)], axis=1).astype(o_ref.dtype)
        tot_ref[...] = jnp.concatenate([run_buf[h, :, :HEAD_DIM] for h in range(n_heads)], axis=1)

    blk = pl.BlockSpec((None, ATT_Q, lanes), lambda b, p, i: (b, i, p))
    full = pl.BlockSpec((None, seq, lanes), lambda b, p, i: (b, 0, p))
    tile = (n_heads, ATT_Q, ATT_KEYS)
    return pl.pallas_call(
        body, name="attention_fwd", grid=(bsz, d // lanes, seq // ATT_Q),
        in_specs=[blk, full, full], out_specs=[blk, blk],
        out_shape=[jax.ShapeDtypeStruct((bsz, seq, d), BF16), jax.ShapeDtypeStruct((bsz, seq, d), F32)],
        scratch_shapes=[pltpu.VMEM(tile, F32), pltpu.VMEM(tile, F32),
                        pltpu.VMEM((n_heads, 2, ATT_Q, ATT_KEYS), F32),
                        pltpu.VMEM((n_heads, ATT_Q, HEAD_DIM), F32),
                        pltpu.VMEM((n_heads, ATT_Q, ATT_BLOCK), F32)],
        compiler_params=_cparams("arbitrary", "arbitrary", "arbitrary"),
    )(q, k, v)


def attention_bwd(q, k, v, tot, do):
    bsz, seq, d = q.shape

    def body(q_ref, k_ref, v_ref, tot_ref, do_ref, dq_ref, dk_ref, dv_ref,
             z_buf, dw_buf, ls_buf, e_buf, up_buf, bf_buf, w_buf, do_buf, dq_buf, tot_buf, run_buf, erun_buf):
        i = pl.program_id(2)
        jd = ((i + 1) * ATT_Q - 1) // ATT_KEYS

        @pl.when(i == 0)
        def _():
            dk_ref[...] = jnp.zeros_like(dk_ref)
            dv_ref[...] = jnp.zeros_like(dv_ref)

        upto_incl, upto_excl = _sum_matrix(False, True), _sum_matrix(False, False)
        do_buf[...] = do_ref[...].astype(BF16)
        for h, ln in enumerate(_HEADS):
            tot_buf[h] = jnp.concatenate([tot_ref[:, ln], tot_ref[:, ln]], axis=1)
        dq_buf[...] = jnp.zeros_like(dq_buf)
        run_buf[...] = jnp.zeros_like(run_buf)
        erun_buf[...] = jnp.zeros_like(erun_buf)

        def block(j, masked):
            rows = pl.ds(pl.multiple_of(j * ATT_KEYS, ATT_KEYS), ATT_KEYS)
            strict = _strict_mask(i, j) if masked else None
            for h, ln in enumerate(_HEADS):
                z_buf[h] = lax.dot_general(q_ref[:, ln], k_ref[rows, ln], _NT, preferred_element_type=F32)
                dw_buf[h] = lax.dot_general(do_buf[:, ln], v_ref[rows, ln], _NT, preferred_element_type=F32)
            for h in range(ATT_HEADS):
                for half, cols in enumerate(_HALF):
                    ls, lf = _log_sigmoids(z_buf[h, :, cols])
                    if masked:
                        lf = jnp.where(strict[:, cols], lf, 0.0)
                    ls_buf[h, :, cols] = ls
                    up_buf[h, half] = jnp.dot(_hi_lo(lf), upto_incl, preferred_element_type=F32)
            for h in range(ATT_HEADS):
                run = run_buf[h]
                early, late = up_buf[h, 0], up_buf[h, 1]
                u0 = run + early[:, _HALF[0]]
                run = run + early[:, _HALF[1]]
                u1 = run + late[:, _HALF[0]]
                run_buf[h] = run + late[:, _HALF[1]]
                tot_h = tot_buf[h]
                after = jnp.concatenate([tot_h - u0, tot_h - u1], axis=1)
                w = jnp.exp(ls_buf[h] + after)
                if masked:
                    w = jnp.where(strict, w, 0.0)
                w_buf[h] = w.astype(BF16)
                e = dw_buf[h] * w
                e_buf[h] = e
                for half, cols in enumerate(_HALF):
                    bf_buf[h, half] = jnp.dot(_hi_lo(e[:, cols]), upto_excl, preferred_element_type=F32)
            dks, dvs = [], []
            for h, ln in enumerate(_HEADS):
                erun = erun_buf[h]
                early, late = bf_buf[h, 0], bf_buf[h, 1]
                b0 = erun + early[:, _HALF[0]]
                erun = erun + early[:, _HALF[1]]
                b1 = erun + late[:, _HALF[0]]
                erun_buf[h] = erun + late[:, _HALF[1]]
                e = e_buf[h]
                dz = e - jnp.exp(ls_buf[h]) * (e + jnp.concatenate([b0, b1], axis=1))
                if masked:
                    dz = jnp.where(strict, dz, 0.0)
                dz = dz.astype(BF16)
                dq_buf[h] += jnp.dot(dz, k_ref[rows, ln], preferred_element_type=F32)
                dks.append(lax.dot_general(dz, q_ref[:, ln], _TN, preferred_element_type=F32))
                dvs.append(lax.dot_general(w_buf[h], do_buf[:, ln], _TN, preferred_element_type=F32))
            dk_ref[rows, :] += jnp.concatenate(dks, axis=1)
            dv_ref[rows, :] += jnp.concatenate(dvs, axis=1)

        def step(j, carry):
            block(j, False)
            return carry

        lax.fori_loop(0, jd, step, 0)
        block(jd, True)
        dq_ref[...] = jnp.concatenate([dq_buf[h] for h in range(ATT_HEADS)], axis=1) * ATT_SCALE

    blk = pl.BlockSpec((None, ATT_Q, ATT_LANES), lambda b, p, i: (b, i, p))
    full = pl.BlockSpec((None, seq, ATT_LANES), lambda b, p, i: (b, 0, p), pipeline_mode=pl.Buffered(1))
    shape = jax.ShapeDtypeStruct((bsz, seq, d), F32)
    tile = (ATT_HEADS, ATT_Q, ATT_KEYS)
    pair = (ATT_HEADS, 2, ATT_Q, ATT_KEYS)
    square = (ATT_HEADS, ATT_Q, ATT_BLOCK)
    return pl.pallas_call(
        body, name="attention_bwd", grid=(bsz, d // ATT_LANES, seq // ATT_Q),
        in_specs=[blk, full, full, blk, blk], out_specs=[blk, full, full], out_shape=[shape, shape, shape],
        scratch_shapes=[pltpu.VMEM(tile, F32), pltpu.VMEM(tile, F32), pltpu.VMEM(tile, F32), pltpu.VMEM(tile, F32),
                        pltpu.VMEM(pair, F32), pltpu.VMEM(pair, F32), pltpu.VMEM(tile, BF16),
                        pltpu.VMEM((ATT_Q, ATT_LANES), BF16), pltpu.VMEM((ATT_HEADS, ATT_Q, HEAD_DIM), F32),
                        pltpu.VMEM(square, F32), pltpu.VMEM(square, F32), pltpu.VMEM(square, F32)],
        compiler_params=_cparams("arbitrary", "arbitrary", "arbitrary"),
    )(q, k, v, tot, do)


def _tile_fused(seq):
    return _divisor_tile(seq, 1024, 16)


MLP_SLABS = 1


def mlp_core_fwd(x, norm_g, sh, sc, gate, w1, w2, name):
    bsz, seq, d = x.shape
    ns, _, fs = w1.shape
    t = bsz * seq
    tm = _tile_fused(seq)
    g = MLP_SLABS if ns % MLP_SLABS == 0 else 1
    steps = ns // g

    def body(x_ref, w1_ref, w2_ref, ng_ref, sh_ref, sc_ref, g_ref, h_ref, act_ref, ff_ref, out_ref, acc_ref, h_buf):
        s = pl.program_id(1)

        @pl.when(s == 0)
        def _():
            h = _norm_mod(x_ref[...], ng_ref[...], sh_ref[...], sc_ref[...]).astype(BF16)
            h_buf[...] = h
            h_ref[...] = h
            acc_ref[...] = jnp.zeros_like(acc_ref)

        hb = h_buf[...]
        for k in range(g):
            pre = jnp.dot(hb, w1_ref[k], preferred_element_type=F32)
            act = jnp.square(jnp.maximum(pre, 0.0)).astype(BF16)
            act_ref[:, k * fs:(k + 1) * fs] = act
            acc_ref[...] += jnp.dot(act, w2_ref[k], preferred_element_type=F32)

        @pl.when(s == steps - 1)
        def _():
            ff = acc_ref[...]
            ff_ref[...] = ff
            out_ref[...] = x_ref[...] + g_ref[...] * ff

    rows = pl.BlockSpec((tm, d), lambda i, s: (i, 0))
    per_seq = pl.BlockSpec((None, 1, d), lambda i, s: ((i * tm) // seq, 0, 0))
    h, act, ff, out = pl.pallas_call(
        body, name=name, grid=(t // tm, steps),
        in_specs=[rows, pl.BlockSpec((g, d, fs), lambda i, s: (s, 0, 0)),
                  pl.BlockSpec((g, fs, d), lambda i, s: (s, 0, 0)),
                  pl.BlockSpec((None, 1, d), lambda i, s: (0, 0, 0)), per_seq, per_seq, per_seq],
        out_specs=[rows, pl.BlockSpec((tm, g * fs), lambda i, s: (i, s)), rows, rows],
        out_shape=[jax.ShapeDtypeStruct((t, d), BF16), jax.ShapeDtypeStruct((t, ns * fs), BF16),
                   jax.ShapeDtypeStruct((t, d), F32), jax.ShapeDtypeStruct((t, d), F32)],
        scratch_shapes=[pltpu.VMEM((tm, d), F32), pltpu.VMEM((tm, d), BF16)],
        compiler_params=_cparams("arbitrary", "arbitrary"),
    )(x.reshape(t, d), w1, w2, norm_g, sh, sc, gate)
    return h, act, ff.reshape(bsz, seq, d), out.reshape(bsz, seq, d)


def mlp_core_bwd(dff, act, w1, w2, name):
    t, d = dff.shape
    ns, _, fs = w1.shape
    tm = _tile_fused(t)
    g = MLP_SLABS if ns % MLP_SLABS == 0 else 1
    steps = ns // g

    def body(dff_ref, act_ref, w1_ref, w2_ref, dpre_ref, dh_ref, acc_ref):
        s = pl.program_id(1)
        db = dff_ref[...]

        @pl.when(s == 0)
        def _():
            acc_ref[...] = jnp.zeros_like(acc_ref)

        for k in range(g):
            cols = slice(k * fs, (k + 1) * fs)
            dact = lax.dot_general(db, w2_ref[k], _NT, preferred_element_type=F32)
            dpre = (dact * (2.0 * jnp.sqrt(act_ref[:, cols].astype(F32)))).astype(BF16)
            dpre_ref[:, cols] = dpre
            acc_ref[...] += lax.dot_general(dpre, w1_ref[k], _NT, preferred_element_type=F32)

        @pl.when(s == steps - 1)
        def _():
            dh_ref[...] = acc_ref[...]

    rows = pl.BlockSpec((tm, d), lambda i, s: (i, 0))
    slab = pl.BlockSpec((tm, g * fs), lambda i, s: (i, s))
    return pl.pallas_call(
        body, name=name, grid=(t // tm, steps),
        in_specs=[rows, slab, pl.BlockSpec((g, d, fs), lambda i, s: (s, 0, 0)),
                  pl.BlockSpec((g, fs, d), lambda i, s: (s, 0, 0))],
        out_specs=[slab, rows],
        out_shape=[jax.ShapeDtypeStruct((t, ns * fs), BF16), jax.ShapeDtypeStruct((t, d), F32)],
        scratch_shapes=[pltpu.VMEM((tm, d), F32)],
        compiler_params=_cparams("arbitrary", "arbitrary"),
    )(dff, act, w1, w2)


def mlp_fwd(x, g, sh, sc, gate, w1_handle, w2_handle, tag):
    w1 = exchange_wait(w1_handle, x, tag + "_w1_wait")
    w2 = exchange_wait(w2_handle, x, tag + "_w2_wait")
    h, act, ff, out = mlp_core_fwd(x, g, sh, sc, gate, w1, w2, tag + "_core")
    return out, (h, act, ff), w1, w2


def glu_fwd(ge, w, x, gate, name):
    bsz, seq, d = x.shape
    ns, _, nb = w.shape
    half = ns // 2
    t = bsz * seq
    tm = _tile_fused(seq)

    def body(a_ref, wv_ref, wg_ref, x_ref, g_ref, val_ref, gt_ref, out_ref):
        a = a_ref[...]
        val = jnp.dot(a, wv_ref[...], preferred_element_type=F32)
        gt = jnp.dot(a, wg_ref[...], preferred_element_type=F32)
        val_ref[...] = val
        gt_ref[...] = gt
        out_ref[...] = x_ref[...] + g_ref[...] * (val * jax.nn.sigmoid(gt))

    cols = pl.BlockSpec((tm, nb), lambda i, j: (i, j))
    res = pl.pallas_call(
        body, name=name, grid=(t // tm, half),
        in_specs=[pl.BlockSpec((tm, d), lambda i, j: (i, 0)),
                  pl.BlockSpec((None, d, nb), lambda i, j: (j, 0, 0)),
                  pl.BlockSpec((None, d, nb), lambda i, j: (half + j, 0, 0)), cols,
                  pl.BlockSpec((None, 1, nb), lambda i, j: ((i * tm) // seq, 0, j))],
        out_specs=[cols, cols, cols],
        out_shape=[jax.ShapeDtypeStruct((t, d), F32)] * 3,
        compiler_params=_cparams("arbitrary", "arbitrary"),
    )(ge.reshape(t, d), w, w, x.reshape(t, d), gate)
    return tuple(r.reshape(bsz, seq, d) for r in res)


def _gate_bwd(dx, f, gate):
    return gate * dx, _rowsum(dx * f)


def _gate_bwd_outs(d):
    return [(d, BF16, "tile"), (d, F32, "seq")]


def _norm_bwd_outs(d):
    return [(d, F32, "tile"), (d, F32, "seq"), (d, F32, "seq"), (d, F32, "all")]


def _norm_then_gate_bwd(x, g, sc, dh, dres, f, gate):
    res = _norm_mod_bwd(x, g, sc, dh, dres)
    return (*res, *_gate_bwd(res[0], f, gate))


def mlp_bwd(dout, dff, x, g, sc, w1, w2, saved, tag, branch=None):
    bsz, seq, d = x.shape
    t = bsz * seq
    ns = w1.shape[0]
    h, act, _ = saved
    dff = dff.reshape(t, d)
    dpre, dh = mlp_core_bwd(dff, act, w1, w2, tag + "_dcore")
    dw2 = mm_tn(act, dff, "a", ns, tag + "_dw2")
    dw1 = mm_tn(h.reshape(t, d), dpre, "c", ns, tag + "_dw1")
    (dw1_handle, dw2_handle), token = exchange_start([(dw1, True), (dw2, True)], tag + "_dw_start")
    ins = [x, g + token[0, 0], sc, dh.reshape(bsz, seq, d), dout]
    if branch is None:
        dx, dsh, dsc, dg = act_call(_norm_mod_bwd, ins, _norm_bwd_outs(d), tag + "_dnorm")
        into_branch = None
    else:
        dx, dsh, dsc, dg, *into_branch = act_call(_norm_then_gate_bwd, ins + list(branch),
                                                  _norm_bwd_outs(d) + _gate_bwd_outs(d), tag + "_dnorm")
    return dx, dw1_handle, dw2_handle, (dsh, dsc, dg), into_branch


def _block_diag(m):
    _, rows, c = m.shape
    k = S5_BLOCK_GROUPS
    row_group = lax.broadcasted_iota(jnp.int32, (rows, k * c), 0) // (rows // k)
    col_group = lax.broadcasted_iota(jnp.int32, (rows, k * c), 1) // c
    return jnp.where(row_group == col_group, jnp.tile(m, (1, 1, k)), 0.0)


def kernel(x, c, ada_w, ada_b, mix_norm_g, mlp_norm_g, mlp_w1, mlp_w2, s5_a_re, s5_a_im, s5_log_dt, s5_b_re, s5_b_im, s5_c_re, s5_c_im, s5_d, s5_w_glu, kv_ada_w, kv_ada_b, kv_norm_g, w_kv, k_norm_g, sb_w_q, q_norm_g, sb_w_o, loss_target, m_ada_w, m_ada_b, m_mix_norm_g, m_mlp_norm_g, m_mlp_w1, m_mlp_w2, m_s5_a_re, m_s5_a_im, m_s5_log_dt, m_s5_b_re, m_s5_b_im, m_s5_c_re, m_s5_c_im, m_s5_d, m_s5_w_glu, m_kv_ada_w, m_kv_ada_b, m_kv_norm_g, m_w_kv, m_k_norm_g, m_sb_w_q, m_q_norm_g, m_sb_w_o, v_ada_w, v_ada_b, v_mix_norm_g, v_mlp_norm_g, v_mlp_w1, v_mlp_w2, v_s5_a_re, v_s5_a_im, v_s5_log_dt, v_s5_b_re, v_s5_b_im, v_s5_c_re, v_s5_c_im, v_s5_d, v_s5_w_glu, v_kv_ada_w, v_kv_ada_b, v_kv_norm_g, v_w_kv, v_k_norm_g, v_sb_w_q, v_q_norm_g, v_sb_w_o):
    bsz, seq, d = x.shape
    t = bsz * seq
    n_groups = d // S5_GROUP
    nb = n_groups // S5_BLOCK_GROUPS
    gp = n_groups * S5_STATE
    dev = 4 * lax.axis_index("x") + 2 * lax.axis_index("y") + lax.axis_index("c")
    e_ada, e_kv = 6 * d, 2 * d
    n_ada, n_kv = e_ada // N_DEV, e_kv // N_DEV

    d_skip = all_gather(s5_d, "gather_skip").reshape(1, 1, d)
    c_all = all_gather(c, "gather_c").reshape(N_DEV * bsz, d)

    w_cols = jnp.concatenate([ada_w[0], ada_w[1], kv_ada_w], axis=1)
    b_cols = jnp.concatenate([
        lax.dynamic_slice_in_dim(ada_b[0], dev * n_ada, n_ada),
        lax.dynamic_slice_in_dim(ada_b[1], dev * n_ada, n_ada),
        lax.dynamic_slice_in_dim(kv_ada_b, dev * n_kv, n_kv)])[None, :]
    mod_cols = ada_fwd(c_all, w_cols, b_cols)
    mod_all = all_gather(mod_cols, "gather_mod")
    mod_mine = lax.dynamic_slice_in_dim(mod_all, dev * bsz, bsz, axis=1)
    mod_mine = jnp.transpose(mod_mine, (1, 0, 2))
    mods = []
    for i in range(2):
        full = mod_mine[:, :, i * n_ada:(i + 1) * n_ada].reshape(bsz, e_ada)
        mods.append([full[:, None, j * d:(j + 1) * d] for j in range(6)])
    kv_full = mod_mine[:, :, 2 * n_ada:].reshape(bsz, e_kv)
    kv_sh, kv_sc = kv_full[:, None, :d], kv_full[:, None, d:]

    par = lambda p: p.reshape(1, 1, -1)

    shards = [s5_w_glu[0], mlp_w1[0], mlp_w2[0], w_kv, sb_w_q[0], sb_w_o[0], mlp_w1[1], mlp_w2[1]]
    gathers, gather_token = exchange_start([(w.astype(BF16), False) for w in shards], "gather_start", after=[mod_all, d_skip])
    glu_handle, w1_0_handle, w2_0_handle, wkv_handle, wq_handle, wo_handle, w1_1_handle, w2_1_handle = gathers
    started = gather_token[0, 0]

    sh_a, sc_a, g_a, sh_m, sc_m, g_m = mods[0]
    lam_re, lam_im = s5_a_re.reshape(1, gp), s5_a_im.reshape(1, gp)
    log_dt = jnp.broadcast_to(s5_log_dt.reshape(n_groups, 1), (n_groups, S5_STATE)).reshape(1, gp)
    b_re, b_im = s5_b_re.reshape(gp, S5_GROUP).T, s5_b_im.reshape(gp, S5_GROUP).T
    ab_re, ab_im, bb_re, bb_im = s5_prep(lam_re, lam_im, log_dt, b_re, b_im)
    to_bbd = lambda m: _block_diag(jnp.transpose(
        m.reshape(S5_GROUP, nb, S5_BLOCK_GROUPS, S5_STATE), (1, 2, 0, 3)).reshape(nb, -1, S5_STATE))
    to_cbd = lambda m: _block_diag(
        jnp.swapaxes(m.reshape(nb, S5_BLOCK_GROUPS, S5_GROUP, S5_STATE), 2, 3).reshape(nb, -1, S5_GROUP))
    bbd_re, bbd_im = to_bbd(bb_re), to_bbd(bb_im)
    cbd_re, cbd_im = to_cbd(s5_c_re[0]), to_cbd(s5_c_im[0])
    abr, abi = ab_re.reshape(nb, 1, -1), ab_im.reshape(nb, 1, -1)

    h0 = act_call(_norm_mod, [x, par(mix_norm_g[0]) + started, sh_a, sc_a], [(d, F32, "tile")], "mix0_norm")[0]
    y, ge, st_re, st_im = s5_fwd(h0, bbd_re, bbd_im, cbd_re, cbd_im, abr, abi, d_skip)
    w_glu = exchange_wait(glu_handle, ge, "glu_w_wait")
    z_val, z_gate, x1 = glu_fwd(ge, w_glu, x, g_a, "glu_up")
    x2, mlp0_saved, w1_0, w2_0 = mlp_fwd(x1, par(mlp_norm_g[0]), sh_m, sc_m, g_m, w1_0_handle, w2_0_handle, "mlp0")

    sh_a1, sc_a1, g_a1, sh_m1, sc_m1, g_m1 = mods[1]
    kg = par(jnp.tile(k_norm_g, d // HEAD_DIM))
    qg = par(jnp.tile(q_norm_g[0], d // HEAD_DIM))
    hkv = act_call(_norm_mod, [x2, par(kv_norm_g), kv_sh, kv_sc], [(d, BF16, "tile")], "kv_norm")[0]
    wkv = exchange_wait(wkv_handle, hkv, "kv_w_wait")
    half = N_DEV // 2
    k_raw, k_h = mm_nn_col(hkv.reshape(t, d), wkv, "k_proj", (F32, BF16),
                           lambda acc, g_: (acc, _head_norm(acc, g_)), (0, half), (kg.reshape(1, d),))
    v_h = mm_nn_col(hkv.reshape(t, d), wkv, "v_proj", (BF16,), None, (half, half))
    k_raw, k_h, v_h = (a.reshape(bsz, seq, d) for a in (k_raw, k_h, v_h))
    h1 = act_call(_norm_mod, [x2, par(mix_norm_g[1]), sh_a1, sc_a1], [(d, BF16, "tile")], "mix1_norm")[0]
    wq = exchange_wait(wq_handle, h1, "q_w_wait")
    whole = lambda w: w.reshape(1, d, d)
    tm_epi = _tile_fused(seq)
    vec = lambda tm: [pl.BlockSpec((1, d), lambda i, s: (0, 0))]
    q_raw, q_h = mm_nn_row(h1.reshape(t, d), whole(wq), "q_proj", (F32, BF16),
                           lambda acc, g_: (acc, _head_norm(acc, g_) * ATT_SCALE), (qg.reshape(1, d),), vec, tm_epi)
    q_raw, q_h = q_raw.reshape(bsz, seq, d), q_h.reshape(bsz, seq, d)
    o, att_tot = attention_fwd(q_h, k_h, v_h)
    wo = exchange_wait(wo_handle, o, "o_w_wait")
    res_specs = lambda tm: [pl.BlockSpec((tm, d), lambda i, s: (i, 0)),
                            pl.BlockSpec((None, 1, d), lambda i, s: ((i * tm) // seq, 0, 0))]
    mix1, x3 = mm_nn_row(o.reshape(t, d), whole(wo), "o_proj", (F32, F32),
                         lambda acc, x_, g_: (acc, x_ + g_ * acc), (x2.reshape(t, d), g_a1), res_specs, tm_epi)
    mix1, x3 = mix1.reshape(bsz, seq, d), x3.reshape(bsz, seq, d)
    x4, mlp1_saved, w1_1, w2_1 = mlp_fwd(x3, par(mlp_norm_g[1]), sh_m1, sc_m1, g_m1, w1_1_handle, w2_1_handle, "mlp1")

    def loss_fn(y_, t_, f_, g_):
        diff = y_ - t_
        part = jnp.sum(0.5 * jnp.mean(diff * diff, axis=-1, keepdims=True), axis=0, keepdims=True)
        dy_ = diff * (1.0 / d)
        return (jnp.broadcast_to(part, (1, LANES)), dy_, *_gate_bwd(dy_, f_, g_))

    loss_part, dx4, dff1, dg_m1 = act_call(
        loss_fn, [x4, loss_target, mlp1_saved[2], g_m1],
        [(LANES, F32, "all"), (d, F32, "tile")] + _gate_bwd_outs(d), "loss")

    dx3, dw1_1, dw2_1, (dsh_m1, dsc_m1, dgn_mlp1), (dmix1, dg_a1) = mlp_bwd(
        dx4, dff1, x3, par(mlp_norm_g[1]), sc_m1, w1_1, w2_1, mlp1_saved, "mlp1", (mix1, g_a1))
    dmix1 = dmix1.reshape(t, d)
    do = mm_nt_row(dmix1, whole(wo), "o_dproj").reshape(bsz, seq, d)
    dwo = mm_tn(o.reshape(t, d), dmix1, "a", N_DEV, "o_dw")
    dq, dk, dv = attention_bwd(q_h, k_h, v_h, att_tot, do)
    dq_raw, dqg = act_call(_head_norm_bwd, [q_raw, qg, dq], [(d, BF16, "tile"), (d, F32, "all")], "q_dnorm")
    dq_raw = dq_raw.reshape(t, d)
    dh1 = mm_nt_row(dq_raw, whole(wq), "q_dproj").reshape(bsz, seq, d)
    dwq = mm_tn(h1.reshape(t, d), dq_raw, "a", N_DEV, "q_dw")
    dx2, dsh_a1, dsc_a1, dgn_mix1 = act_call(
        _norm_mod_bwd, [x2, par(mix_norm_g[1]), sc_a1, dh1, dx3],
        [(d, F32, "tile"), (d, F32, "seq"), (d, F32, "seq"), (d, F32, "all")], "mix1_dnorm")

    def kv_bwd_fn(k_, g_, dk_, dv_):
        dk_raw, dg_ = _head_norm_bwd(k_, g_, dk_)
        return jnp.concatenate([dk_raw, dv_], axis=1), dg_

    dkvf, dkg = act_call(kv_bwd_fn, [k_raw, kg, dk, dv], [(2 * d, BF16, "tile"), (d, F32, "all")], "k_dnorm")
    dkvf = dkvf.reshape(t, 2 * d)
    dhkv = mm_nt_col(dkvf, wkv, "kv_dproj").reshape(bsz, seq, d)
    dwkv = mm_tn(hkv.reshape(t, d), dkvf, "c", N_DEV, "kv_dw")
    (dwo, dwq, dwkv), att_token = exchange_start([(dwo, True), (dwq, True), (dwkv, True)], "att_dw_start")
    dx2, dkv_sh, dkv_sc, dgn_kv, dff0, dg_m0 = act_call(
        _norm_then_gate_bwd, [x2, par(kv_norm_g) + att_token[0, 0], kv_sc, dhkv, dx2, mlp0_saved[2], g_m],
        _norm_bwd_outs(d) + _gate_bwd_outs(d), "kv_dnorm")

    dx1, dw1_0, dw2_0, (dsh_m0, dsc_m0, dgn_mlp0), _ = mlp_bwd(
        dx2, dff0, x1, par(mlp_norm_g[0]), sc_m, w1_0, w2_0, mlp0_saved, "mlp0")

    def glu_bwd_fn(do_, val, gt, g_):
        sig = jax.nn.sigmoid(gt)
        dmix = g_ * do_
        dz = jnp.concatenate([dmix * sig, dmix * val * sig * (1.0 - sig)], axis=1)
        return dz, _rowsum(do_ * (val * sig))

    dz, dg_a0 = act_call(glu_bwd_fn, [dx1, z_val, z_gate, g_a], [(2 * d, BF16, "tile"), (d, F32, "seq")], "glu_dres")
    dz = dz.reshape(t, 2 * d)
    dy = mm_nt_col(dz, w_glu, "glu_dup", (F32,), lambda acc, y_: (acc * _gelu_grad(y_),),
                   (y.reshape(t, d),)).reshape(bsz, seq, d)
    dwglu = mm_tn(ge.reshape(t, d), dz, "c", N_DEV, "glu_dw")
    (dwglu,), glu_token = exchange_start([(dwglu, True)], "glu_dw_start")
    du, dbbd_re, dbbd_im, dcbd_re, dcbd_im, dab_re, dab_im, dd_skip = s5_bwd(
        dy, h0, st_re, st_im, bbd_re, bbd_im, cbd_re, cbd_im, abr, abi, d_skip + glu_token[0, 0])
    dx0, dsh_a0, dsc_a0, dgn_mix0 = act_call(
        _norm_mod_bwd, [x, par(mix_norm_g[0]), sc_a, du, dx1],
        [(d, F32, "tile"), (d, F32, "seq"), (d, F32, "seq"), (d, F32, "all")], "mix0_dnorm")

    from_bbd = lambda m: jnp.transpose(
        m.reshape(nb, S5_BLOCK_GROUPS, S5_GROUP, S5_STATE), (2, 0, 1, 3)).reshape(S5_GROUP, gp)
    d_c = lambda m: jnp.swapaxes(m.reshape(nb, S5_BLOCK_GROUPS, S5_STATE, S5_GROUP), 2, 3).reshape(
        1, n_groups, S5_GROUP, S5_STATE)
    d_lam_re, d_lam_im, d_log_dt, d_b_re, d_b_im = s5_prep_bwd(
        lam_re, lam_im, log_dt, b_re, b_im, dab_re.reshape(1, gp), dab_im.reshape(1, gp),
        from_bbd(dbbd_re), from_bbd(dbbd_im))

    small = all_reduce_small([
        jnp.stack([dgn_mix0.reshape(d), dgn_mix1.reshape(d)]),
        jnp.stack([dgn_mlp0.reshape(d), dgn_mlp1.reshape(d)]),
        d_lam_re.reshape(1, n_groups, S5_STATE), d_lam_im.reshape(1, n_groups, S5_STATE),
        d_log_dt.reshape(1, n_groups, S5_STATE).sum(axis=-1),
        d_b_re.T.reshape(s5_b_re.shape), d_b_im.T.reshape(s5_b_im.shape),
        d_c(dcbd_re), d_c(dcbd_im),
        dd_skip.reshape(1, d),
        dgn_kv.reshape(d),
        dkg.reshape(d // HEAD_DIM, HEAD_DIM).sum(axis=0),
        dqg.reshape(d // HEAD_DIM, HEAD_DIM).sum(axis=0)[None, :],
        loss_part[0, 0, :1],
    ], "small_grads")
    (g_mix_norm, g_mlp_norm, g_a_re, g_a_im, g_log_dt, g_b_re, g_b_im, g_c_re, g_c_im,
     g_skip_full, g_kv_norm, g_k_norm, g_q_norm, loss_all) = small
    loss = loss_all[0]
    g_s5_d = lax.dynamic_slice_in_dim(g_skip_full, dev * (d // N_DEV), d // N_DEV, axis=1)

    dm_mine = jnp.concatenate([
        dsh_a0, dsc_a0, dg_a0, dsh_m0, dsc_m0, dg_m0,
        dsh_a1, dsc_a1, dg_a1, dsh_m1, dsc_m1, dg_m1, dkv_sh, dkv_sc], axis=2).reshape(bsz, 2 * e_ada + e_kv)
    dm_all = all_gather(dm_mine, "gather_dmod").reshape(N_DEV * bsz, 2 * e_ada + e_kv)
    dm_cols = jnp.concatenate([
        lax.dynamic_slice_in_dim(dm_all, dev * n_ada, n_ada, axis=1),
        lax.dynamic_slice_in_dim(dm_all, e_ada + dev * n_ada, n_ada, axis=1),
        lax.dynamic_slice_in_dim(dm_all, 2 * e_ada + dev * n_kv, n_kv, axis=1)], axis=1)
    dw_cols, db_all = ada_bwd(c_all, dm_cols, dm_all)
    g_ada_w = jnp.stack([dw_cols[:, :n_ada], dw_cols[:, n_ada:2 * n_ada]])
    g_kv_ada_w = dw_cols[:, 2 * n_ada:]
    g_ada_b = db_all[0, :2 * e_ada].reshape(2, e_ada)
    g_kv_ada_b = db_all[0, 2 * e_ada:]

    landed = lambda handle, name: slab_sum(exchange_wait(handle, dx0, name + "_wait"), name + "_sum")
    g_w1 = jnp.stack([landed(dw1_0, "rs_w1_0"), landed(dw1_1, "rs_w1_1")])
    g_w2 = jnp.stack([landed(dw2_0, "rs_w2_0"), landed(dw2_1, "rs_w2_1")])
    g_glu = landed(dwglu, "rs_glu")[None]
    g_wkv = landed(dwkv, "rs_wkv")
    g_wq = landed(dwq, "rs_wq")[None]
    g_wo = landed(dwo, "rs_wo")[None]

    weights = [ada_w, ada_b, mix_norm_g, mlp_norm_g, mlp_w1, mlp_w2, s5_a_re, s5_a_im, s5_log_dt, s5_b_re,
               s5_b_im, s5_c_re, s5_c_im, s5_d, s5_w_glu, kv_ada_w, kv_ada_b, kv_norm_g, w_kv, k_norm_g,
               sb_w_q, q_norm_g, sb_w_o]
    grads = [g_ada_w, g_ada_b, g_mix_norm, g_mlp_norm, g_w1, g_w2, g_a_re, g_a_im, g_log_dt, g_b_re,
             g_b_im, g_c_re, g_c_im, g_s5_d, g_glu, g_kv_ada_w, g_kv_ada_b, g_kv_norm, g_wkv, g_k_norm,
             g_wq, g_q_norm, g_wo]
    ms = [m_ada_w, m_ada_b, m_mix_norm_g, m_mlp_norm_g, m_mlp_w1, m_mlp_w2, m_s5_a_re, m_s5_a_im, m_s5_log_dt,
          m_s5_b_re, m_s5_b_im, m_s5_c_re, m_s5_c_im, m_s5_d, m_s5_w_glu, m_kv_ada_w, m_kv_ada_b, m_kv_norm_g,
          m_w_kv, m_k_norm_g, m_sb_w_q, m_q_norm_g, m_sb_w_o]
    vs = [v_ada_w, v_ada_b, v_mix_norm_g, v_mlp_norm_g, v_mlp_w1, v_mlp_w2, v_s5_a_re, v_s5_a_im, v_s5_log_dt,
          v_s5_b_re, v_s5_b_im, v_s5_c_re, v_s5_c_im, v_s5_d, v_s5_w_glu, v_kv_ada_w, v_kv_ada_b, v_kv_norm_g,
          v_w_kv, v_k_norm_g, v_sb_w_q, v_q_norm_g, v_sb_w_o]
    grads = [g.reshape(w.shape) for g, w in zip(grads, weights)]
    deltas, new_ms, new_vs = [], [], []
    for i, (w, g, m, v) in enumerate(zip(weights, grads, ms, vs)):
        dl, nm, nv = adamw(w, g, m, v, f"adamw_{i}")
        deltas.append(dl)
        new_ms.append(nm)
        new_vs.append(nv)
    return (loss, dx0, *grads, *deltas, *new_ms, *new_vs)
```

```python
import functools
import math

import jax
import jax.numpy as jnp
from jax import lax
from jax.experimental import pallas as pl
from jax.experimental.pallas import tpu as pltpu

F32 = jnp.float32
BF16 = jnp.bfloat16

N_DEV = 8
MESH = pl.DeviceIdType.MESH
ANY = pl.BlockSpec(memory_space=pl.ANY)

LANES = 128
VMEM_LIMIT_BYTES = 48 * 2 ** 20
TILE_BUDGET_BYTES = 4 * 2 ** 20

S5_GROUP = 16
S5_STATE = 64
S5_BLOCK_GROUPS = 16
HEAD_DIM = 64
ATT_BLOCK = 128
EPS = 1e-6

ADAM_LR = 0.001
ADAM_B1 = 0.9
ADAM_B2 = 0.999
ADAM_EPS = 1e-08
ADAM_WD = 0.01
ADAM_STEP = 10


def _cparams(*sem):
    return pltpu.CompilerParams(dimension_semantics=sem, vmem_limit_bytes=VMEM_LIMIT_BYTES)


def _divisor_tile(n, limit, mult):
    best = None
    for t in range(mult, min(n, limit) + 1, mult):
        if n % t == 0:
            best = t
    return best if best is not None else n


def _tile_m(m):
    return _divisor_tile(m, 2048 if m >= 4096 else 256, 16)


def all_gather(x, name):
    def body(x_ref, out_ref, send_sems, recv_sems, local_sem):
        ax, ay, ac = lax.axis_index("x"), lax.axis_index("y"), lax.axis_index("c")
        me, sibling = (ax, ay, ac), (ax, ay, 1 - ac)
        chips = [(1 - ax, ay), (ax, 1 - ay), (1 - ax, 1 - ay)]

        def slot(px, py, pc):
            return out_ref.at[4 * px + 2 * py + pc]

        def copy(k, block, to, src=None):
            return pltpu.make_async_remote_copy(
                src_ref=slot(*block) if src is None else src, dst_ref=slot(*block),
                send_sem=send_sems.at[k], recv_sem=recv_sems.at[k], device_id=to, device_id_type=MESH)

        mine = pltpu.make_async_copy(x_ref, slot(*me), local_sem)
        mine.start()
        first = [copy(0, me, sibling, src=x_ref)]
        first += [copy(1 + j, me, (*chip, ac), src=x_ref) for j, chip in enumerate(chips)]
        for cp in first:
            cp.start()
        passed = [copy(4 + j, (*chip, ac), sibling) for j, chip in enumerate(chips)]
        for j, chip in enumerate(chips):
            copy(1 + j, (*chip, ac), me).wait_recv()
            passed[j].start()
        copy(0, sibling, me).wait_recv()
        for j, chip in enumerate(chips):
            copy(4 + j, (*chip, 1 - ac), me).wait_recv()
        for cp in first + passed:
            cp.wait_send()
        mine.wait()

    return pl.pallas_call(
        body, name=name,
        out_shape=jax.ShapeDtypeStruct((N_DEV,) + x.shape, x.dtype),
        in_specs=[ANY], out_specs=ANY,
        scratch_shapes=[pltpu.SemaphoreType.DMA((7,)), pltpu.SemaphoreType.DMA((7,)), pltpu.SemaphoreType.DMA],
    )(x)


HBM = pl.BlockSpec(memory_space=pltpu.HBM)
SEM = pl.BlockSpec(memory_space=pltpu.SEMAPHORE)
N_PEERS = N_DEV - 1


def _peers():
    ax, ay, ac = lax.axis_index("x"), lax.axis_index("y"), lax.axis_index("c")
    flip = lambda v, bit: 1 - v if bit else v
    return [(flip(ax, k & 4), flip(ay, k & 2), flip(ac, k & 1)) for k in range(1, N_DEV)]


def _dev_index(pos):
    return 4 * pos[0] + 2 * pos[1] + pos[2]


def exchange_start(items, name, after=None):
    n = len(items)
    srcs = [a for a, _ in items]
    blocks = [a.shape[1:] if scatter else a.shape for a, scatter in items]
    extra = list(after or ())

    def body(*refs):
        src_refs, land_refs = refs[:n], refs[n:2 * n]
        outs = refs[2 * n + len(extra):]
        send_sems, recv_sems = outs[:n], outs[n:2 * n]
        token = outs[-1]
        me = _dev_index((lax.axis_index("x"), lax.axis_index("y"), lax.axis_index("c")))
        for w, (_, scatter) in enumerate(items):
            for k, peer in enumerate(_peers()):
                src = src_refs[w].at[_dev_index(peer)] if scatter else src_refs[w]
                pltpu.make_async_remote_copy(
                    src_ref=src, dst_ref=land_refs[w].at[me], send_sem=send_sems[w].at[k],
                    recv_sem=recv_sems[w].at[k], device_id=peer, device_id_type=MESH).start()
        token[...] = jnp.zeros_like(token)

    lands = [lax.empty((N_DEV,) + blk, a.dtype) for a, blk in zip(srcs, blocks)]
    res = pl.pallas_call(
        body, name=name,
        out_shape=([pltpu.SemaphoreType.DMA((N_PEERS,))] * (2 * n)
                   + [pltpu.HBM(a.shape, a.dtype) for a in srcs] + [pltpu.HBM(l.shape, l.dtype) for l in lands]
                   + [jax.ShapeDtypeStruct((8, LANES), F32)]),
        in_specs=[HBM] * (2 * n) + [ANY] * len(extra),
        out_specs=[SEM] * (2 * n) + [HBM] * (2 * n) + [pl.BlockSpec(memory_space=pltpu.VMEM)],
        input_output_aliases={i: 2 * n + i for i in range(2 * n)},
        compiler_params=pltpu.CompilerParams(has_side_effects=pltpu.SideEffectType.DATAFLOW_SIDE_EFFECTING),
    )(*[pltpu.with_memory_space_constraint(a, pltpu.HBM) for a in srcs + lands], *extra)
    handles = [(res[w], res[n + w], res[2 * n + w], res[3 * n + w], scatter) for w, (_, scatter) in enumerate(items)]
    return handles, res[-1]


def exchange_wait(handle, after, name):
    send_sem, recv_sem, src, land, scatter = handle

    def body(src_ref, land_ref, send_ref, recv_ref, after_ref, src_out, land_out):
        for k, peer in enumerate(_peers()):
            slot = _dev_index(peer)
            copy = pltpu.make_async_remote_copy(
                src_ref=src_ref.at[slot] if scatter else src_ref, dst_ref=land_ref.at[slot],
                send_sem=send_ref.at[k], recv_sem=recv_ref.at[k], device_id=peer, device_id_type=MESH)
            copy.wait_send()
            copy.wait_recv()

    src, landed = pl.pallas_call(
        body, name=name,
        out_shape=(pltpu.HBM(src.shape, src.dtype), pltpu.HBM(land.shape, land.dtype)),
        in_specs=[HBM, HBM, SEM, SEM, ANY], out_specs=(HBM, HBM), input_output_aliases={0: 0, 1: 1},
        compiler_params=pltpu.CompilerParams(has_side_effects=pltpu.SideEffectType.DATAFLOW_SIDE_EFFECTING),
    )(src, land, send_sem, recv_sem, after)
    dev = _dev_index((lax.axis_index("x"), lax.axis_index("y"), lax.axis_index("c")))
    own = lax.dynamic_index_in_dim(src, dev, axis=0, keepdims=True) if scatter else src[None]
    return lax.dynamic_update_slice_in_dim(landed, own, dev, axis=0)


def rows_call(fn, ins, outs, name):
    rows = ins[0].shape[1]
    per_row = sum(a.shape[0] * a.shape[2] * a.dtype.itemsize for a in ins)
    per_row += sum(l * c * jnp.dtype(dt).itemsize for l, c, dt in outs)
    tr = _divisor_tile(rows, max(16, TILE_BUDGET_BYTES // per_row), 16)
    n_in = len(ins)

    def body(*refs):
        vals = fn(*[r[...] for r in refs[:n_in]])
        if not isinstance(vals, (tuple, list)):
            vals = (vals,)
        for r, v in zip(refs[n_in:], vals):
            r[...] = v.astype(r.dtype)

    def spec(l, c):
        return pl.BlockSpec((l, tr, c), lambda i: (0, i, 0))

    res = pl.pallas_call(
        body, name=name, grid=(rows // tr,),
        in_specs=[spec(a.shape[0], a.shape[2]) for a in ins],
        out_specs=[spec(l, c) for l, c, _ in outs],
        out_shape=[jax.ShapeDtypeStruct((l, rows, c), dt) for l, c, dt in outs],
        compiler_params=_cparams("arbitrary"),
    )(*ins)
    return res


def _as_rows(a, lead=0):
    shape = a.shape
    l = int(math.prod(shape[:lead])) if lead else 1
    rest = shape[lead:]
    c = rest[-1] if rest else 1
    r = int(math.prod(rest[:-1])) if len(rest) > 1 else 1
    return a.reshape(l, r, c)


def act_call(fn, ins, outs, name):
    bsz, seq = ins[0].shape[0], ins[0].shape[1]
    per_row = sum(a.shape[2] * a.dtype.itemsize for a in ins if a.shape[1] == seq)
    per_row += sum(c * jnp.dtype(dt).itemsize for c, dt, kind in outs if kind == "tile")
    ts = _divisor_tile(seq, max(16, TILE_BUDGET_BYTES // per_row), 16)
    n_in = len(ins)

    def in_spec(a):
        c = a.shape[2]
        if a.shape[1] == seq:
            return pl.BlockSpec((None, ts, c), lambda b, s: (b, s, 0))
        if a.shape[0] == bsz:
            return pl.BlockSpec((None, 1, c), lambda b, s: (b, 0, 0))
        return pl.BlockSpec((None, 1, c), lambda b, s: (0, 0, 0))

    def out_spec(c, kind):
        if kind == "tile":
            return pl.BlockSpec((None, ts, c), lambda b, s: (b, s, 0))
        if kind == "seq":
            return pl.BlockSpec((None, 1, c), lambda b, s: (b, 0, 0))
        return pl.BlockSpec((None, 1, c), lambda b, s: (0, 0, 0))

    def out_shape(c, dt, kind):
        if kind == "tile":
            return jax.ShapeDtypeStruct((bsz, seq, c), dt)
        return jax.ShapeDtypeStruct((bsz if kind == "seq" else 1, 1, c), dt)

    def accumulate(ref, v, first):
        @pl.when(first)
        def _():
            ref[...] = jnp.zeros_like(ref)

        ref[...] += v.astype(ref.dtype)

    def body(*refs):
        b, s = pl.program_id(0), pl.program_id(1)
        vals = fn(*[r[...] for r in refs[:n_in]])
        if not isinstance(vals, (tuple, list)):
            vals = (vals,)
        for ref, v, (_, _, kind) in zip(refs[n_in:], vals, outs):
            if kind == "tile":
                ref[...] = v.astype(ref.dtype)
            elif kind == "seq":
                accumulate(ref, v, s == 0)
            else:
                accumulate(ref, v, jnp.logical_and(b == 0, s == 0))

    return pl.pallas_call(
        body, name=name, grid=(bsz, seq // ts),
        in_specs=[in_spec(a) for a in ins],
        out_specs=[out_spec(c, kind) for c, _, kind in outs],
        out_shape=[out_shape(*o) for o in outs],
        compiler_params=_cparams("arbitrary", "arbitrary"),
    )(*ins)


def _mm(name, grid, a, a_spec, b, b_spec, dims, out_shape, out_spec, out_dtypes, acc_steps,
        epi=None, extras=(), extra_specs=()):
    n_ex, n_out = len(extras), len(out_dtypes)
    tile = tuple(d for d in out_spec.block_shape if d is not None)

    def body(*refs):
        a_ref, b_ref = refs[0], refs[1]
        ex_refs = refs[2:2 + n_ex]
        o_refs = refs[2 + n_ex:2 + n_ex + n_out]
        def product():
            return lax.dot_general(a_ref[...].astype(BF16), b_ref[...].astype(BF16), (dims, ((), ())),
                                   preferred_element_type=F32)

        def finish(acc):
            vals = epi(acc, *[r[...] for r in ex_refs]) if epi is not None else (acc,) * n_out
            for r, v in zip(o_refs, vals):
                r[...] = v.astype(r.dtype)

        if not acc_steps:
            finish(product())
        else:
            acc_ref = refs[-1]
            s = pl.program_id(1)

            @pl.when(s == 0)
            def _():
                acc_ref[...] = jnp.zeros_like(acc_ref)

            acc_ref[...] += product()

            @pl.when(s == acc_steps - 1)
            def _():
                finish(acc_ref[...])

    res = pl.pallas_call(
        body, name=name, grid=grid,
        in_specs=[a_spec, b_spec] + list(extra_specs),
        out_specs=[out_spec] * n_out,
        out_shape=[jax.ShapeDtypeStruct(out_shape, dt) for dt in out_dtypes],
        scratch_shapes=[pltpu.VMEM(tile, F32)] if acc_steps else [],
        compiler_params=_cparams("arbitrary", "arbitrary"),
    )(a, b, *extras)
    return res if n_out > 1 else res[0]


def mm_nn_col(a, w, name, out_dtypes=(F32,), epi=None, slabs=None, col_params=()):
    m, k = a.shape
    _, _, nb = w.shape
    first, count = slabs if slabs is not None else (0, w.shape[0])
    tm = _tile_m(m)
    return _mm(name, (m // tm, count), a, pl.BlockSpec((tm, k), lambda i, j: (i, 0)),
               w, pl.BlockSpec((None, k, nb), lambda i, j: (first + j, 0, 0)), ((1,), (0,)),
               (m, count * nb), pl.BlockSpec((tm, nb), lambda i, j: (i, j)), out_dtypes, 0, epi,
               col_params, [pl.BlockSpec((1, nb), lambda i, j: (0, j))] * len(col_params))


def mm_nn_row(a, w, name, out_dtypes=(F32,), epi=None, extras=(), extra_specs=None, tm=None):
    m = a.shape[0]
    ns, kb, n = w.shape
    tm = tm or _tile_m(m)
    return _mm(name, (m // tm, ns), a, pl.BlockSpec((tm, kb), lambda i, s: (i, s)),
               w, pl.BlockSpec((None, kb, n), lambda i, s: (s, 0, 0)), ((1,), (0,)),
               (m, n), pl.BlockSpec((tm, n), lambda i, s: (i, 0)), out_dtypes, ns if ns > 1 else 0, epi,
               extras, extra_specs(tm) if extras else ())


def mm_nt_col(dc, w, name, out_dtypes=(F32,), epi=None, extras=()):
    m = dc.shape[0]
    ns, k, nb = w.shape
    tm = _divisor_tile(m, 512, 16)
    n_ex, n_out = len(extras), len(out_dtypes)

    def body(dc_ref, w_ref, *rest):
        ex_refs, o_refs = rest[:n_ex], rest[n_ex:]
        w_all = jnp.concatenate([w_ref[s] for s in range(ns)], axis=1)
        acc = lax.dot_general(dc_ref[...].astype(BF16), w_all, (((1,), (1,)), ((), ())), preferred_element_type=F32)
        vals = epi(acc, *[r[...] for r in ex_refs]) if epi is not None else (acc,) * n_out
        for r, v in zip(o_refs, vals):
            r[...] = v.astype(r.dtype)

    rows = pl.BlockSpec((tm, k), lambda i: (i, 0))
    res = pl.pallas_call(
        body, name=name, grid=(m // tm,),
        in_specs=[pl.BlockSpec((tm, ns * nb), lambda i: (i, 0)), pl.BlockSpec((ns, k, nb), lambda i: (0, 0, 0))]
        + [rows] * n_ex,
        out_specs=[rows] * n_out,
        out_shape=[jax.ShapeDtypeStruct((m, k), dt) for dt in out_dtypes],
        compiler_params=_cparams("arbitrary"),
    )(dc, w, *extras)
    return res if n_out > 1 else res[0]


def mm_nt_row(dc, w, name, out_dtypes=(F32,), epi=None, extras=()):
    m, n = dc.shape
    ns, kb, _ = w.shape
    tm = _tile_m(m)
    spec = pl.BlockSpec((tm, kb), lambda i, s: (i, s))
    return _mm(name, (m // tm, ns), dc, pl.BlockSpec((tm, n), lambda i, s: (i, 0)),
               w, pl.BlockSpec((None, kb, n), lambda i, s: (s, 0, 0)), ((1,), (1,)),
               (m, ns * kb), spec, out_dtypes, 0, epi, extras, [spec] * len(extras))


def mm_tn(a, c, slab, ns, name, out_dtype=BF16):
    m, ka_all = a.shape
    nc_all = c.shape[1]
    ka = ka_all // ns if slab == "a" else ka_all
    nc = nc_all // ns if slab == "c" else nc_all
    tt = _divisor_tile(m, 1024, 16)
    steps = m // tt

    def body(a_ref, c_ref, o_ref, acc_ref):
        t = pl.program_id(0)

        @pl.when(t == 0)
        def _():
            acc_ref[...] = jnp.zeros_like(acc_ref)

        for s in range(ns):
            a_s = a_ref[:, s * ka:(s + 1) * ka] if slab == "a" else a_ref[...]
            c_s = c_ref[:, s * nc:(s + 1) * nc] if slab == "c" else c_ref[...]
            acc_ref[s] += lax.dot_general(a_s.astype(BF16), c_s.astype(BF16), (((0,), (0,)), ((), ())),
                                          preferred_element_type=F32)

        @pl.when(t == steps - 1)
        def _():
            o_ref[...] = acc_ref[...].astype(o_ref.dtype)

    return pl.pallas_call(
        body, name=name, grid=(steps,),
        in_specs=[pl.BlockSpec((tt, ka_all), lambda t: (t, 0)), pl.BlockSpec((tt, nc_all), lambda t: (t, 0))],
        out_specs=pl.BlockSpec((ns, ka, nc), lambda t: (0, 0, 0), pipeline_mode=pl.Buffered(1)),
        out_shape=jax.ShapeDtypeStruct((ns, ka, nc), out_dtype),
        scratch_shapes=[pltpu.VMEM((ns, ka, nc), F32)],
        compiler_params=_cparams("arbitrary"),
    )(a, c)


def slab_sum(landed, name):
    shape = landed.shape[1:]
    total = rows_call(lambda g: jnp.sum(g.astype(F32), axis=0, keepdims=True),
                      [_as_rows(landed, 1)], [(1, shape[-1], F32)], name)[0]
    return total.reshape(shape)


def all_reduce_small(leaves, name):
    sizes = [int(a.size) for a in leaves]
    flat = jnp.concatenate([a.reshape(-1) for a in leaves])
    total = int(flat.size)
    chunk = N_DEV * 16 * LANES
    padded = -(-total // chunk) * chunk
    parts = jnp.pad(flat, (0, padded - total)).reshape(N_DEV, padded // (N_DEV * LANES), LANES)
    (scatter,), _ = exchange_start([(parts, True)], name + "_scatter")
    landed = exchange_wait(scatter, parts, name + "_scatter_wait")
    mine = rows_call(lambda g: jnp.sum(g, axis=0, keepdims=True), [landed], [(1, LANES, F32)], name + "_sum")[0][0]
    (gather,), _ = exchange_start([(mine, False)], name + "_gather")
    summed = exchange_wait(gather, mine, name + "_gather_wait").reshape(-1)
    out, at = [], 0
    for a, n in zip(leaves, sizes):
        out.append(summed[at:at + n].reshape(a.shape))
        at += n
    return out


def adamw(w, g, m, v, name):
    narrow = w.ndim > 0 and w.shape[-1] < LANES and w.size % LANES == 0
    view = (lambda a: a.reshape(1, -1, LANES)) if narrow else _as_rows
    c = LANES if narrow else (w.shape[-1] if w.ndim else 1)

    def fn(w_, g_, m_, v_):
        nm = ADAM_B1 * m_ + (1.0 - ADAM_B1) * g_
        nv = ADAM_B2 * v_ + (1.0 - ADAM_B2) * (g_ * g_)
        m_hat = nm / (1.0 - ADAM_B1 ** ADAM_STEP)
        v_hat = nv / (1.0 - ADAM_B2 ** ADAM_STEP)
        delta = -ADAM_LR * (m_hat / (jnp.sqrt(v_hat) + ADAM_EPS) + ADAM_WD * w_)
        return delta, nm, nv

    res = rows_call(fn, [view(t) for t in (w, g.astype(F32), m, v)], [(1, c, F32)] * 3, name)
    return tuple(r.reshape(w.shape) for r in res)


def _rowsum(v):
    return jnp.sum(v, axis=0, keepdims=True)


def _norm_mod(x, g, sh, sc):
    n = x * lax.rsqrt(jnp.mean(x * x, axis=-1, keepdims=True) + EPS)
    return (n * g) * (1.0 + sc) + sh


def _norm_mod_bwd(x, g, sc, dh, dres):
    r = lax.rsqrt(jnp.mean(x * x, axis=-1, keepdims=True) + EPS)
    n = x * r
    dy = dh * (1.0 + sc)
    dn = dy * g
    dx = r * (dn - n * jnp.mean(dn * n, axis=-1, keepdims=True))
    return dres + dx, _rowsum(dh), _rowsum(dh * (n * g)), _rowsum(dy * n)


def _head_mean(v):
    low = lax.broadcasted_iota(jnp.int32, (1, LANES), 1) < HEAD_DIM
    parts = []
    for p in range(v.shape[1] // LANES):
        blk = v[:, p * LANES:(p + 1) * LANES]
        s0 = jnp.sum(jnp.where(low, blk, 0.0), axis=-1, keepdims=True)
        s1 = jnp.sum(jnp.where(low, 0.0, blk), axis=-1, keepdims=True)
        parts.append(jnp.where(low, s0, s1))
    return jnp.concatenate(parts, axis=1) * (1.0 / HEAD_DIM)


def _head_norm(x, g):
    return x * lax.rsqrt(_head_mean(x * x) + EPS) * g


def _head_norm_bwd(x, g, dy):
    r = lax.rsqrt(_head_mean(x * x) + EPS)
    n = x * r
    dn = dy * g
    return r * (dn - n * _head_mean(dn * n)), _rowsum(dy * n)


GELU_C = math.sqrt(2.0 / math.pi)
GELU_A = 0.044715


def _gelu_grad(y):
    t = jnp.tanh(GELU_C * (y + GELU_A * y * y * y))
    return 0.5 * (1.0 + t) + 0.5 * y * (1.0 - t * t) * GELU_C * (1.0 + 3.0 * GELU_A * y * y)


def ada_fwd(c_all, w_cols, b_cols):
    def body(c_ref, w_ref, b_ref, o_ref):
        c = c_ref[...]
        s = (c * jax.nn.sigmoid(c)).astype(BF16)
        o_ref[...] = jnp.dot(s, w_ref[...].astype(BF16), preferred_element_type=F32) + b_ref[...]

    return pl.pallas_call(
        body, name="ada_fwd", out_shape=jax.ShapeDtypeStruct((c_all.shape[0], w_cols.shape[1]), F32),
        compiler_params=pltpu.CompilerParams(vmem_limit_bytes=VMEM_LIMIT_BYTES),
    )(c_all, w_cols, b_cols)


def ada_bwd(c_all, dm_cols, dm_all):
    def body(c_ref, d_ref, all_ref, dw_ref, db_ref):
        c = c_ref[...]
        s = (c * jax.nn.sigmoid(c)).astype(BF16)
        dw_ref[...] = lax.dot_general(s, d_ref[...].astype(BF16), (((0,), (0,)), ((), ())),
                                      preferred_element_type=F32)
        db_ref[...] = jnp.sum(all_ref[...], axis=0, keepdims=True)

    return pl.pallas_call(
        body, name="ada_bwd",
        out_shape=[jax.ShapeDtypeStruct((c_all.shape[1], dm_cols.shape[1]), F32),
                   jax.ShapeDtypeStruct((1, dm_all.shape[1]), F32)],
        compiler_params=pltpu.CompilerParams(vmem_limit_bytes=VMEM_LIMIT_BYTES),
    )(c_all, dm_cols, dm_all)


def _s5_discretise(lam_re, lam_im, log_dt, b_re, b_im):
    dt = jnp.exp(log_dt)
    mag = jnp.exp(lam_re * dt)
    ab_re = mag * jnp.cos(lam_im * dt)
    ab_im = mag * jnp.sin(lam_im * dt)
    den = lam_re * lam_re + lam_im * lam_im
    nr = ab_re - 1.0
    ni = ab_im
    f_re = (nr * lam_re + ni * lam_im) / den
    f_im = (ni * lam_re - nr * lam_im) / den
    bb_re = f_re * b_re - f_im * b_im
    bb_im = f_re * b_im + f_im * b_re
    return ab_re, ab_im, bb_re, bb_im


def s5_prep(lam_re, lam_im, log_dt, b_re, b_im):
    h, gp = b_re.shape

    def body(lr, li, ld, br, bi, o_ar, o_ai, o_br, o_bi):
        res = _s5_discretise(lr[...], li[...], ld[...], br[...], bi[...])
        for r, v in zip((o_ar, o_ai, o_br, o_bi), res):
            r[...] = v

    col, mat = jax.ShapeDtypeStruct((1, gp), F32), jax.ShapeDtypeStruct((h, gp), F32)
    return pl.pallas_call(body, name="s5_prep", out_shape=[col, col, mat, mat])(lam_re, lam_im, log_dt, b_re, b_im)


def s5_prep_bwd(lam_re, lam_im, log_dt, b_re, b_im, d_ab_re, d_ab_im, d_bb_re, d_bb_im):
    h, gp = b_re.shape

    def body(lr, li, ld, br, bi, g_ar, g_ai, g_br, g_bi, o_lr, o_li, o_ld, o_br, o_bi):
        _, vjp = jax.vjp(_s5_discretise, lr[...], li[...], ld[...], br[...], bi[...])
        res = vjp((g_ar[...], g_ai[...], g_br[...], g_bi[...]))
        for r, v in zip((o_lr, o_li, o_ld, o_br, o_bi), res):
            r[...] = v

    col, mat = jax.ShapeDtypeStruct((1, gp), F32), jax.ShapeDtypeStruct((h, gp), F32)
    return pl.pallas_call(body, name="s5_prep_bwd", out_shape=[col, col, col, mat, mat])(
        lam_re, lam_im, log_dt, b_re, b_im, d_ab_re, d_ab_im, d_bb_re, d_bb_im)


def _s5_chunk(seq):
    return _divisor_tile(seq, 256, 16)


def s5_fwd(u, bbd_re, bbd_im, cbd_re, cbd_im, ab_re, ab_im, dskip):
    bsz, seq, d = u.shape
    nb, cb, ns = bbd_re.shape
    lc = _s5_chunk(seq)

    def body(u_ref, bre_ref, bim_ref, cre_ref, cim_ref, ar_ref, ai_ref, d_ref, y_ref, ge_ref, sre_ref, sim_ref,
             carry_re, carry_im):
        t = pl.program_id(1)

        @pl.when(t == 0)
        def _():
            carry_re[...] = jnp.zeros_like(carry_re)
            carry_im[...] = jnp.zeros_like(carry_im)

        for b in range(bsz):
            ub = u_ref[b].astype(BF16)
            sre_ref[b] = jnp.dot(ub, bre_ref[...].astype(BF16), preferred_element_type=F32)
            sim_ref[b] = jnp.dot(ub, bim_ref[...].astype(BF16), preferred_element_type=F32)
        ar, ai = ar_ref[...], ai_ref[...]

        def step(i, carry):
            row = pl.ds(i, 1)
            out = []
            for b, (cr, ci) in enumerate(carry):
                nr = ar * cr - ai * ci + sre_ref[b, row, :]
                ni = ar * ci + ai * cr + sim_ref[b, row, :]
                sre_ref[b, row, :] = nr
                sim_ref[b, row, :] = ni
                out.append((nr, ni))
            return tuple(out)

        init = tuple((carry_re[b], carry_im[b]) for b in range(bsz))
        last = lax.fori_loop(0, lc, step, init, unroll=8)
        for b, (cr, ci) in enumerate(last):
            carry_re[b] = cr
            carry_im[b] = ci
            y = jnp.dot(sre_ref[b].astype(BF16), cre_ref[...].astype(BF16), preferred_element_type=F32)
            y -= jnp.dot(sim_ref[b].astype(BF16), cim_ref[...].astype(BF16), preferred_element_type=F32)
            y = y + d_ref[...] * u_ref[b]
            y_ref[b] = y
            ge_ref[b] = jax.nn.gelu(y).astype(BF16)

    chan = pl.BlockSpec((bsz, lc, cb), lambda n, t: (0, t, n))
    state = pl.BlockSpec((bsz, lc, ns), lambda n, t: (0, t, n))
    par = lambda r, c: pl.BlockSpec((None, r, c), lambda n, t: (n, 0, 0))
    return pl.pallas_call(
        body, name="s5_fwd", grid=(nb, seq // lc),
        in_specs=[chan, par(cb, ns), par(cb, ns), par(ns, cb), par(ns, cb), par(1, ns), par(1, ns),
                  pl.BlockSpec((None, 1, cb), lambda n, t: (0, 0, n))],
        out_specs=[chan, chan, state, state],
        out_shape=[jax.ShapeDtypeStruct((bsz, seq, d), F32), jax.ShapeDtypeStruct((bsz, seq, d), BF16),
                   jax.ShapeDtypeStruct((bsz, seq, nb * ns), F32),
                   jax.ShapeDtypeStruct((bsz, seq, nb * ns), F32)],
        scratch_shapes=[pltpu.VMEM((bsz, 1, ns), F32), pltpu.VMEM((bsz, 1, ns), F32)],
        compiler_params=_cparams("arbitrary", "arbitrary"),
    )(u, bbd_re, bbd_im, cbd_re, cbd_im, ab_re, ab_im, dskip)


def s5_bwd(dy, u, st_re, st_im, bbd_re, bbd_im, cbd_re, cbd_im, ab_re, ab_im, dskip):
    bsz, seq, d = u.shape
    nb, cb, ns = bbd_re.shape
    lc = _s5_chunk(seq)
    nc = seq // lc

    def body(dy_ref, u_ref, sre_ref, sim_ref, bre_ref, bim_ref, cre_ref, cim_ref, ar_ref, ai_ref, d_ref,
             du_ref, dbre_out, dbim_out, dcre_out, dcim_out, dar_ref, dai_ref, dd_ref,
             g_re, g_im, gs_re, gs_im, carry_re, carry_im, dbre_ref, dbim_ref, dcre_ref, dcim_ref):
        t = pl.program_id(1)

        @pl.when(t == 0)
        def _():
            for r in (dbre_ref, dbim_ref, dcre_ref, dcim_ref, dar_ref, dai_ref, dd_ref, carry_re, carry_im):
                r[...] = jnp.zeros_like(r)

        nt = (((1,), (1,)), ((), ()))
        tn = (((0,), (0,)), ((), ()))
        for b in range(bsz):
            dyb = dy_ref[b].astype(BF16)
            g_re[b] = lax.dot_general(dyb, cre_ref[...].astype(BF16), nt, preferred_element_type=F32)
            g_im[b] = -lax.dot_general(dyb, cim_ref[...].astype(BF16), nt, preferred_element_type=F32)
        ar, ai = ar_ref[...], ai_ref[...]

        def step(k, carry):
            row = pl.ds(lc - 1 - k, 1)
            out = []
            for b, (cr, ci) in enumerate(carry):
                gs_re[b, row, :] = cr
                gs_im[b, row, :] = ci
                nr = ar * cr + ai * ci + g_re[b, row, :]
                ni = ar * ci - ai * cr + g_im[b, row, :]
                g_re[b, row, :] = nr
                g_im[b, row, :] = ni
                out.append((nr, ni))
            return tuple(out)

        init = tuple((carry_re[b], carry_im[b]) for b in range(bsz))
        last = lax.fori_loop(0, lc, step, init, unroll=8)
        for b, (cr, ci) in enumerate(last):
            carry_re[b] = cr
            carry_im[b] = ci
            dyf, uf = dy_ref[b], u_ref[b]
            dyb, ub = dyf.astype(BF16), uf.astype(BF16)
            sr, si = sre_ref[b], sim_ref[b]
            hr, hi = gs_re[b], gs_im[b]
            dar_ref[...] += _rowsum(hr * sr + hi * si)
            dai_ref[...] += _rowsum(hi * sr - hr * si)
            gr, gi = g_re[b].astype(BF16), g_im[b].astype(BF16)
            du = lax.dot_general(gr, bre_ref[...].astype(BF16), nt, preferred_element_type=F32)
            du += lax.dot_general(gi, bim_ref[...].astype(BF16), nt, preferred_element_type=F32)
            du_ref[b] = du + d_ref[...] * dyf
            dbre_ref[...] += lax.dot_general(ub, gr, tn, preferred_element_type=F32)
            dbim_ref[...] += lax.dot_general(ub, gi, tn, preferred_element_type=F32)
            dcre_ref[...] += lax.dot_general(sr.astype(BF16), dyb, tn, preferred_element_type=F32)
            dcim_ref[...] -= lax.dot_general(si.astype(BF16), dyb, tn, preferred_element_type=F32)
            dd_ref[...] += _rowsum(dyf * uf)

        @pl.when(t == nc - 1)
        def _():
            for k in range(cb // S5_GROUP):
                chans = slice(k * S5_GROUP, (k + 1) * S5_GROUP)
                states = slice(k * S5_STATE, (k + 1) * S5_STATE)
                dbre_out[chans, :] = dbre_ref[chans, states]
                dbim_out[chans, :] = dbim_ref[chans, states]
                dcre_out[states, :] = dcre_ref[states, chans]
                dcim_out[states, :] = dcim_ref[states, chans]

    chan = pl.BlockSpec((bsz, lc, cb), lambda n, t: (0, nc - 1 - t, n))
    state = pl.BlockSpec((bsz, lc, ns), lambda n, t: (0, nc - 1 - t, n))
    par = lambda r, c: pl.BlockSpec((None, r, c), lambda n, t: (n, 0, 0))
    return pl.pallas_call(
        body, name="s5_bwd", grid=(nb, nc),
        in_specs=[chan, chan, state, state, par(cb, ns), par(cb, ns), par(ns, cb), par(ns, cb),
                  par(1, ns), par(1, ns), pl.BlockSpec((None, 1, cb), lambda n, t: (0, 0, n))],
        out_specs=[chan, par(cb, S5_STATE), par(cb, S5_STATE), par(ns, S5_GROUP), par(ns, S5_GROUP),
                   par(1, ns), par(1, ns), par(1, cb)],
        out_shape=[jax.ShapeDtypeStruct((bsz, seq, d), F32),
                   jax.ShapeDtypeStruct((nb, cb, S5_STATE), F32), jax.ShapeDtypeStruct((nb, cb, S5_STATE), F32),
                   jax.ShapeDtypeStruct((nb, ns, S5_GROUP), F32), jax.ShapeDtypeStruct((nb, ns, S5_GROUP), F32),
                   jax.ShapeDtypeStruct((nb, 1, ns), F32), jax.ShapeDtypeStruct((nb, 1, ns), F32),
                   jax.ShapeDtypeStruct((nb, 1, cb), F32)],
        scratch_shapes=([pltpu.VMEM((bsz, lc, ns), F32)] * 4 + [pltpu.VMEM((bsz, 1, ns), F32)] * 2
                        + [pltpu.VMEM((cb, ns), F32)] * 2 + [pltpu.VMEM((ns, cb), F32)] * 2),
        compiler_params=_cparams("arbitrary", "arbitrary"),
    )(dy, u, st_re, st_im, bbd_re, bbd_im, cbd_re, cbd_im, ab_re, ab_im, dskip)


ATT_HEADS = 8
ATT_LANES = ATT_HEADS * HEAD_DIM
ATT_KEYS = 2 * ATT_BLOCK
ATT_Q = 256
ATT_SCALE = 1.0 / math.sqrt(HEAD_DIM)
_NT = (((1,), (1,)), ((), ()))
_TN = (((0,), (0,)), ((), ()))
_HEADS = [slice(h * HEAD_DIM, (h + 1) * HEAD_DIM) for h in range(ATT_HEADS)]
_HALF = [slice(0, ATT_BLOCK), slice(ATT_BLOCK, ATT_KEYS)]


def _log_sigmoids(z):
    sp = jnp.log(1.0 + jnp.exp(-jnp.abs(z)))
    ls = jnp.minimum(z, 0.0) - sp
    return ls, ls - z


def _sum_matrix(after, inclusive):
    j = lax.broadcasted_iota(jnp.int32, (ATT_KEYS, ATT_KEYS), 0) % ATT_BLOCK
    s = lax.broadcasted_iota(jnp.int32, (ATT_KEYS, ATT_KEYS), 1)
    if after:
        hit = (j >= s) if inclusive else (j > s)
    else:
        hit = (j <= s) if inclusive else (j < s)
    return jnp.where(jnp.logical_or(hit, s >= ATT_BLOCK), 1.0, 0.0).astype(BF16)


def _hi_lo(v):
    hi = v.astype(BF16)
    lo = (v - hi.astype(F32)).astype(BF16)
    return jnp.concatenate([hi, lo], axis=1)


def _strict_mask(i, j):
    t = i * ATT_Q + lax.broadcasted_iota(jnp.int32, (ATT_Q, ATT_KEYS), 0)
    s = j * ATT_KEYS + lax.broadcasted_iota(jnp.int32, (ATT_Q, ATT_KEYS), 1)
    return s < t


def attention_fwd(q, k, v):
    bsz, seq, d = q.shape
    n_heads = ATT_HEADS
    lanes = n_heads * HEAD_DIM
    heads = [slice(h * HEAD_DIM, (h + 1) * HEAD_DIM) for h in range(n_heads)]

    def body(q_ref, k_ref, v_ref, o_ref, tot_ref, z_buf, ls_buf, cs_buf, acc_buf, run_buf):
        i = pl.program_id(2)
        jd = ((i + 1) * ATT_Q - 1) // ATT_KEYS
        sums = _sum_matrix(True, False)
        acc_buf[...] = jnp.zeros_like(acc_buf)
        run_buf[...] = jnp.zeros_like(run_buf)

        def block(j, masked):
            rows = pl.ds(pl.multiple_of(j * ATT_KEYS, ATT_KEYS), ATT_KEYS)
            strict = _strict_mask(i, j) if masked else None
            for h, ln in enumerate(heads):
                z_buf[h] = lax.dot_general(q_ref[:, ln], k_ref[rows, ln], _NT, preferred_element_type=F32)
            for h in range(n_heads):
                for half, cols in enumerate(_HALF):
                    ls, lf = _log_sigmoids(z_buf[h, :, cols])
                    if masked:
                        lf = jnp.where(strict[:, cols], lf, 0.0)
                    ls_buf[h, :, cols] = ls
                    cs_buf[h, half] = jnp.dot(_hi_lo(lf), sums, preferred_element_type=F32)
            for h, ln in enumerate(heads):
                run = run_buf[h]
                late, early = cs_buf[h, 1], cs_buf[h, 0]
                a1 = run + late[:, _HALF[0]]
                run = run + late[:, _HALF[1]]
                a0 = run + early[:, _HALF[0]]
                run_buf[h] = run + early[:, _HALF[1]]
                w = jnp.exp(ls_buf[h] + jnp.concatenate([a0, a1], axis=1))
                if masked:
                    w = jnp.where(strict, w, 0.0)
                acc_buf[h] += jnp.dot(w.astype(BF16), v_ref[rows, ln], preferred_element_type=F32)

        block(jd, True)

        def step(it, carry):
            block(jd - 1 - it, False)
            return carry

        lax.fori_loop(0, jd, step, 0)
        o_ref[...] = jnp.concatenate([acc_buf[h] for h in range(n_heads)], axis=1).astype(o_ref.dtype)
        tot_ref[...] = jnp.concatenate([run_buf[h, :, :HEAD_DIM] for h in range(n_heads)], axis=1)

    blk = pl.BlockSpec((None, ATT_Q, lanes), lambda b, p, i: (b, i, p))
    full = pl.BlockSpec((None, seq, lanes), lambda b, p, i: (b, 0, p))
    tile = (n_heads, ATT_Q, ATT_KEYS)
    return pl.pallas_call(
        body, name="attention_fwd", grid=(bsz, d // lanes, seq // ATT_Q),
        in_specs=[blk, full, full], out_specs=[blk, blk],
        out_shape=[jax.ShapeDtypeStruct((bsz, seq, d), BF16), jax.ShapeDtypeStruct((bsz, seq, d), F32)],
        scratch_shapes=[pltpu.VMEM(tile, F32), pltpu.VMEM(tile, F32),
                        pltpu.VMEM((n_heads, 2, ATT_Q, ATT_KEYS), F32),
                        pltpu.VMEM((n_heads, ATT_Q, HEAD_DIM), F32),
                        pltpu.VMEM((n_heads, ATT_Q, ATT_BLOCK), F32)],
        compiler_params=_cparams("arbitrary", "arbitrary", "arbitrary"),
    )(q, k, v)


def attention_bwd(q, k, v, tot, do):
    bsz, seq, d = q.shape

    def body(q_ref, k_ref, v_ref, tot_ref, do_ref, dq_ref, dk_ref, dv_ref,
             z_buf, dw_buf, ls_buf, e_buf, up_buf, bf_buf, w_buf, do_buf, dq_buf, tot_buf, run_buf, erun_buf):
        i = pl.program_id(2)
        jd = ((i + 1) * ATT_Q - 1) // ATT_KEYS

        @pl.when(i == 0)
        def _():
            dk_ref[...] = jnp.zeros_like(dk_ref)
            dv_ref[...] = jnp.zeros_like(dv_ref)

        upto_incl, upto_excl = _sum_matrix(False, True), _sum_matrix(False, False)
        do_buf[...] = do_ref[...].astype(BF16)
        for h, ln in enumerate(_HEADS):
            tot_buf[h] = jnp.concatenate([tot_ref[:, ln], tot_ref[:, ln]], axis=1)
        dq_buf[...] = jnp.zeros_like(dq_buf)
        run_buf[...] = jnp.zeros_like(run_buf)
        erun_buf[...] = jnp.zeros_like(erun_buf)

        def block(j, masked):
            rows = pl.ds(pl.multiple_of(j * ATT_KEYS, ATT_KEYS), ATT_KEYS)
            strict = _strict_mask(i, j) if masked else None
            for h, ln in enumerate(_HEADS):
                z_buf[h] = lax.dot_general(q_ref[:, ln], k_ref[rows, ln], _NT, preferred_element_type=F32)
                dw_buf[h] = lax.dot_general(do_buf[:, ln], v_ref[rows, ln], _NT, preferred_element_type=F32)
            for h in range(ATT_HEADS):
                for half, cols in enumerate(_HALF):
                    ls, lf = _log_sigmoids(z_buf[h, :, cols])
                    if masked:
                        lf = jnp.where(strict[:, cols], lf, 0.0)
                    ls_buf[h, :, cols] = ls
                    up_buf[h, half] = jnp.dot(_hi_lo(lf), upto_incl, preferred_element_type=F32)
            for h in range(ATT_HEADS):
                run = run_buf[h]
                early, late = up_buf[h, 0], up_buf[h, 1]
                u0 = run + early[:, _HALF[0]]
                run = run + early[:, _HALF[1]]
                u1 = run + late[:, _HALF[0]]
                run_buf[h] = run + late[:, _HALF[1]]
                tot_h = tot_buf[h]
                after = jnp.concatenate([tot_h - u0, tot_h - u1], axis=1)
                w = jnp.exp(ls_buf[h] + after)
                if masked:
                    w = jnp.where(strict, w, 0.0)
                w_buf[h] = w.astype(BF16)
                e = dw_buf[h] * w
                e_buf[h] = e
                for half, cols in enumerate(_HALF):
                    bf_buf[h, half] = jnp.dot(_hi_lo(e[:, cols]), upto_excl, preferred_element_type=F32)
            dks, dvs = [], []
            for h, ln in enumerate(_HEADS):
                erun = erun_buf[h]
                early, late = bf_buf[h, 0], bf_buf[h, 1]
                b0 = erun + early[:, _HALF[0]]
                erun = erun + early[:, _HALF[1]]
                b1 = erun + late[:, _HALF[0]]
                erun_buf[h] = erun + late[:, _HALF[1]]
                e = e_buf[h]
                dz = e - jnp.exp(ls_buf[h]) * (e + jnp.concatenate([b0, b1], axis=1))
                if masked:
                    dz = jnp.where(strict, dz, 0.0)
                dz = dz.astype(BF16)
                dq_buf[h] += jnp.dot(dz, k_ref[rows, ln], preferred_element_type=F32)
                dks.append(lax.dot_general(dz, q_ref[:, ln], _TN, preferred_element_type=F32))
                dvs.append(lax.dot_general(w_buf[h], do_buf[:, ln], _TN, preferred_element_type=F32))
            dk_ref[rows, :] += jnp.concatenate(dks, axis=1)
            dv_ref[rows, :] += jnp.concatenate(dvs, axis=1)

        def step(j, carry):
            block(j, False)
            return carry

        lax.fori_loop(0, jd, step, 0)
        block(jd, True)
        dq_ref[...] = jnp.concatenate([dq_buf[h] for h in range(ATT_HEADS)], axis=1) * ATT_SCALE

    blk = pl.BlockSpec((None, ATT_Q, ATT_LANES), lambda b, p, i: (b, i, p))
    full = pl.BlockSpec((None, seq, ATT_LANES), lambda b, p, i: (b, 0, p), pipeline_mode=pl.Buffered(1))
    shape = jax.ShapeDtypeStruct((bsz, seq, d), F32)
    tile = (ATT_HEADS, ATT_Q, ATT_KEYS)
    pair = (ATT_HEADS, 2, ATT_Q, ATT_KEYS)
    square = (ATT_HEADS, ATT_Q, ATT_BLOCK)
    return pl.pallas_call(
        body, name="attention_bwd", grid=(bsz, d // ATT_LANES, seq // ATT_Q),
        in_specs=[blk, full, full, blk, blk], out_specs=[blk, full, full], out_shape=[shape, shape, shape],
        scratch_shapes=[pltpu.VMEM(tile, F32), pltpu.VMEM(tile, F32), pltpu.VMEM(tile, F32), pltpu.VMEM(tile, F32),
                        pltpu.VMEM(pair, F32), pltpu.VMEM(pair, F32), pltpu.VMEM(tile, BF16),
                        pltpu.VMEM((ATT_Q, ATT_LANES), BF16), pltpu.VMEM((ATT_HEADS, ATT_Q, HEAD_DIM), F32),
                        pltpu.VMEM(square, F32), pltpu.VMEM(square, F32), pltpu.VMEM(square, F32)],
        compiler_params=_cparams("arbitrary", "arbitrary", "arbitrary"),
    )(q, k, v, tot, do)


def _tile_fused(seq):
    return _divisor_tile(seq, 1024, 16)


MLP_SLABS = 1


def mlp_core_fwd(x, norm_g, sh, sc, gate, w1, w2, name):
    bsz, seq, d = x.shape
    ns, _, fs = w1.shape
    t = bsz * seq
    tm = _tile_fused(seq)
    g = MLP_SLABS if ns % MLP_SLABS == 0 else 1
    steps = ns // g

    def body(x_ref, w1_ref, w2_ref, ng_ref, sh_ref, sc_ref, g_ref, h_ref, act_ref, ff_ref, out_ref, acc_ref, h_buf):
        s = pl.program_id(1)

        @pl.when(s == 0)
        def _():
            h = _norm_mod(x_ref[...], ng_ref[...], sh_ref[...], sc_ref[...]).astype(BF16)
            h_buf[...] = h
            h_ref[...] = h
            acc_ref[...] = jnp.zeros_like(acc_ref)

        hb = h_buf[...]
        for k in range(g):
            pre = jnp.dot(hb, w1_ref[k], preferred_element_type=F32)
            act = jnp.square(jnp.maximum(pre, 0.0)).astype(BF16)
            act_ref[:, k * fs:(k + 1) * fs] = act
            acc_ref[...] += jnp.dot(act, w2_ref[k], preferred_element_type=F32)

        @pl.when(s == steps - 1)
        def _():
            ff = acc_ref[...]
            ff_ref[...] = ff
            out_ref[...] = x_ref[...] + g_ref[...] * ff

    rows = pl.BlockSpec((tm, d), lambda i, s: (i, 0))
    per_seq = pl.BlockSpec((None, 1, d), lambda i, s: ((i * tm) // seq, 0, 0))
    h, act, ff, out = pl.pallas_call(
        body, name=name, grid=(t // tm, steps),
        in_specs=[rows, pl.BlockSpec((g, d, fs), lambda i, s: (s, 0, 0)),
                  pl.BlockSpec((g, fs, d), lambda i, s: (s, 0, 0)),
                  pl.BlockSpec((None, 1, d), lambda i, s: (0, 0, 0)), per_seq, per_seq, per_seq],
        out_specs=[rows, pl.BlockSpec((tm, g * fs), lambda i, s: (i, s)), rows, rows],
        out_shape=[jax.ShapeDtypeStruct((t, d), BF16), jax.ShapeDtypeStruct((t, ns * fs), BF16),
                   jax.ShapeDtypeStruct((t, d), F32), jax.ShapeDtypeStruct((t, d), F32)],
        scratch_shapes=[pltpu.VMEM((tm, d), F32), pltpu.VMEM((tm, d), BF16)],
        compiler_params=_cparams("arbitrary", "arbitrary"),
    )(x.reshape(t, d), w1, w2, norm_g, sh, sc, gate)
    return h, act, ff.reshape(bsz, seq, d), out.reshape(bsz, seq, d)


def mlp_core_bwd(dff, act, w1, w2, name):
    t, d = dff.shape
    ns, _, fs = w1.shape
    tm = _tile_fused(t)
    g = MLP_SLABS if ns % MLP_SLABS == 0 else 1
    steps = ns // g

    def body(dff_ref, act_ref, w1_ref, w2_ref, dpre_ref, dh_ref, acc_ref):
        s = pl.program_id(1)
        db = dff_ref[...]

        @pl.when(s == 0)
        def _():
            acc_ref[...] = jnp.zeros_like(acc_ref)

        for k in range(g):
            cols = slice(k * fs, (k + 1) * fs)
            dact = lax.dot_general(db, w2_ref[k], _NT, preferred_element_type=F32)
            dpre = (dact * (2.0 * jnp.sqrt(act_ref[:, cols].astype(F32)))).astype(BF16)
            dpre_ref[:, cols] = dpre
            acc_ref[...] += lax.dot_general(dpre, w1_ref[k], _NT, preferred_element_type=F32)

        @pl.when(s == steps - 1)
        def _():
            dh_ref[...] = acc_ref[...]

    rows = pl.BlockSpec((tm, d), lambda i, s: (i, 0))
    slab = pl.BlockSpec((tm, g * fs), lambda i, s: (i, s))
    return pl.pallas_call(
        body, name=name, grid=(t // tm, steps),
        in_specs=[rows, slab, pl.BlockSpec((g, d, fs), lambda i, s: (s, 0, 0)),
                  pl.BlockSpec((g, fs, d), lambda i, s: (s, 0, 0))],
        out_specs=[slab, rows],
        out_shape=[jax.ShapeDtypeStruct((t, ns * fs), BF16), jax.ShapeDtypeStruct((t, d), F32)],
        scratch_shapes=[pltpu.VMEM((tm, d), F32)],
        compiler_params=_cparams("arbitrary", "arbitrary"),
    )(dff, act, w1, w2)


def mlp_fwd(x, g, sh, sc, gate, w1_handle, w2_handle, tag):
    w1 = exchange_wait(w1_handle, x, tag + "_w1_wait")
    w2 = exchange_wait(w2_handle, x, tag + "_w2_wait")
    h, act, ff, out = mlp_core_fwd(x, g, sh, sc, gate, w1, w2, tag + "_core")
    return out, (h, act, ff), w1, w2


def glu_fwd(ge, w, x, gate, name):
    bsz, seq, d = x.shape
    ns, _, nb = w.shape
    half = ns // 2
    t = bsz * seq
    tm = _tile_fused(seq)

    def body(a_ref, wv_ref, wg_ref, x_ref, g_ref, val_ref, gt_ref, out_ref):
        a = a_ref[...]
        val = jnp.dot(a, wv_ref[...], preferred_element_type=F32)
        gt = jnp.dot(a, wg_ref[...], preferred_element_type=F32)
        val_ref[...] = val
        gt_ref[...] = gt
        out_ref[...] = x_ref[...] + g_ref[...] * (val * jax.nn.sigmoid(gt))

    cols = pl.BlockSpec((tm, nb), lambda i, j: (i, j))
    res = pl.pallas_call(
        body, name=name, grid=(t // tm, half),
        in_specs=[pl.BlockSpec((tm, d), lambda i, j: (i, 0)),
                  pl.BlockSpec((None, d, nb), lambda i, j: (j, 0, 0)),
                  pl.BlockSpec((None, d, nb), lambda i, j: (half + j, 0, 0)), cols,
                  pl.BlockSpec((None, 1, nb), lambda i, j: ((i * tm) // seq, 0, j))],
        out_specs=[cols, cols, cols],
        out_shape=[jax.ShapeDtypeStruct((t, d), F32)] * 3,
        compiler_params=_cparams("arbitrary", "arbitrary"),
    )(ge.reshape(t, d), w, w, x.reshape(t, d), gate)
    return tuple(r.reshape(bsz, seq, d) for r in res)


def _gate_bwd(dx, f, gate):
    return gate * dx, _rowsum(dx * f)


def _gate_bwd_outs(d):
    return [(d, BF16, "tile"), (d, F32, "seq")]


def _norm_bwd_outs(d):
    return [(d, F32, "tile"), (d, F32, "seq"), (d, F32, "seq"), (d, F32, "all")]


def _norm_then_gate_bwd(x, g, sc, dh, dres, f, gate):
    res = _norm_mod_bwd(x, g, sc, dh, dres)
    return (*res, *_gate_bwd(res[0], f, gate))


def mlp_bwd(dout, dff, x, g, sc, w1, w2, saved, tag, branch=None):
    bsz, seq, d = x.shape
    t = bsz * seq
    ns = w1.shape[0]
    h, act, _ = saved
    dff = dff.reshape(t, d)
    dpre, dh = mlp_core_bwd(dff, act, w1, w2, tag + "_dcore")
    dw2 = mm_tn(act, dff, "a", ns, tag + "_dw2")
    dw1 = mm_tn(h.reshape(t, d), dpre, "c", ns, tag + "_dw1")
    (dw1_handle, dw2_handle), token = exchange_start([(dw1, True), (dw2, True)], tag + "_dw_start")
    ins = [x, g + token[0, 0], sc, dh.reshape(bsz, seq, d), dout]
    if branch is None:
        dx, dsh, dsc, dg = act_call(_norm_mod_bwd, ins, _norm_bwd_outs(d), tag + "_dnorm")
        into_branch = None
    else:
        dx, dsh, dsc, dg, *into_branch = act_call(_norm_then_gate_bwd, ins + list(branch),
                                                  _norm_bwd_outs(d) + _gate_bwd_outs(d), tag + "_dnorm")
    return dx, dw1_handle, dw2_handle, (dsh, dsc, dg), into_branch


def _block_diag(m):
    _, rows, c = m.shape
    k = S5_BLOCK_GROUPS
    row_group = lax.broadcasted_iota(jnp.int32, (rows, k * c), 0) // (rows // k)
    col_group = lax.broadcasted_iota(jnp.int32, (rows, k * c), 1) // c
    return jnp.where(row_group == col_group, jnp.tile(m, (1, 1, k)), 0.0)


def kernel(x, c, ada_w, ada_b, mix_norm_g, mlp_norm_g, mlp_w1, mlp_w2, s5_a_re, s5_a_im, s5_log_dt, s5_b_re, s5_b_im, s5_c_re, s5_c_im, s5_d, s5_w_glu, kv_ada_w, kv_ada_b, kv_norm_g, w_kv, k_norm_g, sb_w_q, q_norm_g, sb_w_o, loss_target, m_ada_w, m_ada_b, m_mix_norm_g, m_mlp_norm_g, m_mlp_w1, m_mlp_w2, m_s5_a_re, m_s5_a_im, m_s5_log_dt, m_s5_b_re, m_s5_b_im, m_s5_c_re, m_s5_c_im, m_s5_d, m_s5_w_glu, m_kv_ada_w, m_kv_ada_b, m_kv_norm_g, m_w_kv, m_k_norm_g, m_sb_w_q, m_q_norm_g, m_sb_w_o, v_ada_w, v_ada_b, v_mix_norm_g, v_mlp_norm_g, v_mlp_w1, v_mlp_w2, v_s5_a_re, v_s5_a_im, v_s5_log_dt, v_s5_b_re, v_s5_b_im, v_s5_c_re, v_s5_c_im, v_s5_d, v_s5_w_glu, v_kv_ada_w, v_kv_ada_b, v_kv_norm_g, v_w_kv, v_k_norm_g, v_sb_w_q, v_q_norm_g, v_sb_w_o):
    bsz, seq, d = x.shape
    t = bsz * seq
    n_groups = d // S5_GROUP
    nb = n_groups // S5_BLOCK_GROUPS
    gp = n_groups * S5_STATE
    dev = 4 * lax.axis_index("x") + 2 * lax.axis_index("y") + lax.axis_index("c")
    e_ada, e_kv = 6 * d, 2 * d
    n_ada, n_kv = e_ada // N_DEV, e_kv // N_DEV

    both = all_gather(jnp.concatenate([c.reshape(-1), s5_d.reshape(-1)])[None, :], "gather_c_skip")
    c_all = both[:, 0, :bsz * d].reshape(N_DEV * bsz, d)
    d_skip = both[:, 0, bsz * d:].reshape(1, 1, d)

    w_cols = jnp.concatenate([ada_w[0], ada_w[1], kv_ada_w], axis=1)
    b_cols = jnp.concatenate([
        lax.dynamic_slice_in_dim(ada_b[0], dev * n_ada, n_ada),
        lax.dynamic_slice_in_dim(ada_b[1], dev * n_ada, n_ada),
        lax.dynamic_slice_in_dim(kv_ada_b, dev * n_kv, n_kv)])[None, :]
    mod_cols = ada_fwd(c_all, w_cols, b_cols)
    mod_all = all_gather(mod_cols, "gather_mod")
    mod_mine = lax.dynamic_slice_in_dim(mod_all, dev * bsz, bsz, axis=1)
    mod_mine = jnp.transpose(mod_mine, (1, 0, 2))
    mods = []
    for i in range(2):
        full = mod_mine[:, :, i * n_ada:(i + 1) * n_ada].reshape(bsz, e_ada)
        mods.append([full[:, None, j * d:(j + 1) * d] for j in range(6)])
    kv_full = mod_mine[:, :, 2 * n_ada:].reshape(bsz, e_kv)
    kv_sh, kv_sc = kv_full[:, None, :d], kv_full[:, None, d:]

    par = lambda p: p.reshape(1, 1, -1)

    shards = [s5_w_glu[0], mlp_w1[0], mlp_w2[0], w_kv, sb_w_q[0], sb_w_o[0], mlp_w1[1], mlp_w2[1]]
    gathers, gather_token = exchange_start([(w.astype(BF16), False) for w in shards], "gather_start", after=[mod_all, d_skip])
    glu_handle, w1_0_handle, w2_0_handle, wkv_handle, wq_handle, wo_handle, w1_1_handle, w2_1_handle = gathers
    started = gather_token[0, 0]

    sh_a, sc_a, g_a, sh_m, sc_m, g_m = mods[0]
    lam_re, lam_im = s5_a_re.reshape(1, gp), s5_a_im.reshape(1, gp)
    log_dt = jnp.broadcast_to(s5_log_dt.reshape(n_groups, 1), (n_groups, S5_STATE)).reshape(1, gp)
    b_re, b_im = s5_b_re.reshape(gp, S5_GROUP).T, s5_b_im.reshape(gp, S5_GROUP).T
    ab_re, ab_im, bb_re, bb_im = s5_prep(lam_re, lam_im, log_dt, b_re, b_im)
    to_bbd = lambda m: _block_diag(jnp.transpose(
        m.reshape(S5_GROUP, nb, S5_BLOCK_GROUPS, S5_STATE), (1, 2, 0, 3)).reshape(nb, -1, S5_STATE))
    to_cbd = lambda m: _block_diag(
        jnp.swapaxes(m.reshape(nb, S5_BLOCK_GROUPS, S5_GROUP, S5_STATE), 2, 3).reshape(nb, -1, S5_GROUP))
    bbd_re, bbd_im = to_bbd(bb_re), to_bbd(bb_im)
    cbd_re, cbd_im = to_cbd(s5_c_re[0]), to_cbd(s5_c_im[0])
    abr, abi = ab_re.reshape(nb, 1, -1), ab_im.reshape(nb, 1, -1)

    h0 = act_call(_norm_mod, [x, par(mix_norm_g[0]) + started, sh_a, sc_a], [(d, F32, "tile")], "mix0_norm")[0]
    y, ge, st_re, st_im = s5_fwd(h0, bbd_re, bbd_im, cbd_re, cbd_im, abr, abi, d_skip)
    w_glu = exchange_wait(glu_handle, ge, "glu_w_wait")
    z_val, z_gate, x1 = glu_fwd(ge, w_glu, x, g_a, "glu_up")
    x2, mlp0_saved, w1_0, w2_0 = mlp_fwd(x1, par(mlp_norm_g[0]), sh_m, sc_m, g_m, w1_0_handle, w2_0_handle, "mlp0")

    sh_a1, sc_a1, g_a1, sh_m1, sc_m1, g_m1 = mods[1]
    kg = par(jnp.tile(k_norm_g, d // HEAD_DIM))
    qg = par(jnp.tile(q_norm_g[0], d // HEAD_DIM))
    hkv = act_call(_norm_mod, [x2, par(kv_norm_g), kv_sh, kv_sc], [(d, BF16, "tile")], "kv_norm")[0]
    wkv = exchange_wait(wkv_handle, hkv, "kv_w_wait")
    half = N_DEV // 2
    k_raw, k_h = mm_nn_col(hkv.reshape(t, d), wkv, "k_proj", (F32, BF16),
                           lambda acc, g_: (acc, _head_norm(acc, g_)), (0, half), (kg.reshape(1, d),))
    v_h = mm_nn_col(hkv.reshape(t, d), wkv, "v_proj", (BF16,), None, (half, half))
    k_raw, k_h, v_h = (a.reshape(bsz, seq, d) for a in (k_raw, k_h, v_h))
    h1 = act_call(_norm_mod, [x2, par(mix_norm_g[1]), sh_a1, sc_a1], [(d, BF16, "tile")], "mix1_norm")[0]
    wq = exchange_wait(wq_handle, h1, "q_w_wait")
    whole = lambda w: w.reshape(1, d, d)
    tm_epi = _tile_fused(seq)
    vec = lambda tm: [pl.BlockSpec((1, d), lambda i, s: (0, 0))]
    q_raw, q_h = mm_nn_row(h1.reshape(t, d), whole(wq), "q_proj", (F32, BF16),
                           lambda acc, g_: (acc, _head_norm(acc, g_) * ATT_SCALE), (qg.reshape(1, d),), vec, tm_epi)
    q_raw, q_h = q_raw.reshape(bsz, seq, d), q_h.reshape(bsz, seq, d)
    o, att_tot = attention_fwd(q_h, k_h, v_h)
    wo = exchange_wait(wo_handle, o, "o_w_wait")
    res_specs = lambda tm: [pl.BlockSpec((tm, d), lambda i, s: (i, 0)),
                            pl.BlockSpec((None, 1, d), lambda i, s: ((i * tm) // seq, 0, 0))]
    mix1, x3 = mm_nn_row(o.reshape(t, d), whole(wo), "o_proj", (F32, F32),
                         lambda acc, x_, g_: (acc, x_ + g_ * acc), (x2.reshape(t, d), g_a1), res_specs, tm_epi)
    mix1, x3 = mix1.reshape(bsz, seq, d), x3.reshape(bsz, seq, d)
    x4, mlp1_saved, w1_1, w2_1 = mlp_fwd(x3, par(mlp_norm_g[1]), sh_m1, sc_m1, g_m1, w1_1_handle, w2_1_handle, "mlp1")

    def loss_fn(y_, t_, f_, g_):
        diff = y_ - t_
        part = jnp.sum(0.5 * jnp.mean(diff * diff, axis=-1, keepdims=True), axis=0, keepdims=True)
        dy_ = diff * (1.0 / d)
        return (jnp.broadcast_to(part, (1, LANES)), dy_, *_gate_bwd(dy_, f_, g_))

    loss_part, dx4, dff1, dg_m1 = act_call(
        loss_fn, [x4, loss_target, mlp1_saved[2], g_m1],
        [(LANES, F32, "all"), (d, F32, "tile")] + _gate_bwd_outs(d), "loss")

    dx3, dw1_1, dw2_1, (dsh_m1, dsc_m1, dgn_mlp1), (dmix1, dg_a1) = mlp_bwd(
        dx4, dff1, x3, par(mlp_norm_g[1]), sc_m1, w1_1, w2_1, mlp1_saved, "mlp1", (mix1, g_a1))
    dmix1 = dmix1.reshape(t, d)
    do = mm_nt_row(dmix1, whole(wo), "o_dproj").reshape(bsz, seq, d)
    dwo = mm_tn(o.reshape(t, d), dmix1, "a", N_DEV, "o_dw")
    dq, dk, dv = attention_bwd(q_h, k_h, v_h, att_tot, do)
    dq_raw, dqg = act_call(_head_norm_bwd, [q_raw, qg, dq], [(d, BF16, "tile"), (d, F32, "all")], "q_dnorm")
    dq_raw = dq_raw.reshape(t, d)
    dh1 = mm_nt_row(dq_raw, whole(wq), "q_dproj").reshape(bsz, seq, d)
    dwq = mm_tn(h1.reshape(t, d), dq_raw, "a", N_DEV, "q_dw")
    dx2, dsh_a1, dsc_a1, dgn_mix1 = act_call(
        _norm_mod_bwd, [x2, par(mix_norm_g[1]), sc_a1, dh1, dx3],
        [(d, F32, "tile"), (d, F32, "seq"), (d, F32, "seq"), (d, F32, "all")], "mix1_dnorm")

    def kv_bwd_fn(k_, g_, dk_, dv_):
        dk_raw, dg_ = _head_norm_bwd(k_, g_, dk_)
        return jnp.concatenate([dk_raw, dv_], axis=1), dg_

    dkvf, dkg = act_call(kv_bwd_fn, [k_raw, kg, dk, dv], [(2 * d, BF16, "tile"), (d, F32, "all")], "k_dnorm")
    dkvf = dkvf.reshape(t, 2 * d)
    dhkv = mm_nt_col(dkvf, wkv, "kv_dproj").reshape(bsz, seq, d)
    dwkv = mm_tn(hkv.reshape(t, d), dkvf, "c", N_DEV, "kv_dw")
    (dwo, dwq, dwkv), att_token = exchange_start([(dwo, True), (dwq, True), (dwkv, True)], "att_dw_start")
    dx2, dkv_sh, dkv_sc, dgn_kv, dff0, dg_m0 = act_call(
        _norm_then_gate_bwd, [x2, par(kv_norm_g) + att_token[0, 0], kv_sc, dhkv, dx2, mlp0_saved[2], g_m],
        _norm_bwd_outs(d) + _gate_bwd_outs(d), "kv_dnorm")

    dx1, dw1_0, dw2_0, (dsh_m0, dsc_m0, dgn_mlp0), _ = mlp_bwd(
        dx2, dff0, x1, par(mlp_norm_g[0]), sc_m, w1_0, w2_0, mlp0_saved, "mlp0")

    def glu_bwd_fn(do_, val, gt, g_):
        sig = jax.nn.sigmoid(gt)
        dmix = g_ * do_
        dz = jnp.concatenate([dmix * sig, dmix * val * sig * (1.0 - sig)], axis=1)
        return dz, _rowsum(do_ * (val * sig))

    dz, dg_a0 = act_call(glu_bwd_fn, [dx1, z_val, z_gate, g_a], [(2 * d, BF16, "tile"), (d, F32, "seq")], "glu_dres")
    dz = dz.reshape(t, 2 * d)
    dy = mm_nt_col(dz, w_glu, "glu_dup", (F32,), lambda acc, y_: (acc * _gelu_grad(y_),),
                   (y.reshape(t, d),)).reshape(bsz, seq, d)
    dwglu = mm_tn(ge.reshape(t, d), dz, "c", N_DEV, "glu_dw")
    (dwglu,), glu_token = exchange_start([(dwglu, True)], "glu_dw_start")
    du, dbbd_re, dbbd_im, dcbd_re, dcbd_im, dab_re, dab_im, dd_skip = s5_bwd(
        dy, h0, st_re, st_im, bbd_re, bbd_im, cbd_re, cbd_im, abr, abi, d_skip + glu_token[0, 0])
    dx0, dsh_a0, dsc_a0, dgn_mix0 = act_call(
        _norm_mod_bwd, [x, par(mix_norm_g[0]), sc_a, du, dx1],
        [(d, F32, "tile"), (d, F32, "seq"), (d, F32, "seq"), (d, F32, "all")], "mix0_dnorm")

    from_bbd = lambda m: jnp.transpose(
        m.reshape(nb, S5_BLOCK_GROUPS, S5_GROUP, S5_STATE), (2, 0, 1, 3)).reshape(S5_GROUP, gp)
    d_c = lambda m: jnp.swapaxes(m.reshape(nb, S5_BLOCK_GROUPS, S5_STATE, S5_GROUP), 2, 3).reshape(
        1, n_groups, S5_GROUP, S5_STATE)
    d_lam_re, d_lam_im, d_log_dt, d_b_re, d_b_im = s5_prep_bwd(
        lam_re, lam_im, log_dt, b_re, b_im, dab_re.reshape(1, gp), dab_im.reshape(1, gp),
        from_bbd(dbbd_re), from_bbd(dbbd_im))

    small = all_reduce_small([
        jnp.stack([dgn_mix0.reshape(d), dgn_mix1.reshape(d)]),
        jnp.stack([dgn_mlp0.reshape(d), dgn_mlp1.reshape(d)]),
        d_lam_re.reshape(1, n_groups, S5_STATE), d_lam_im.reshape(1, n_groups, S5_STATE),
        d_log_dt.reshape(1, n_groups, S5_STATE).sum(axis=-1),
        d_b_re.T.reshape(s5_b_re.shape), d_b_im.T.reshape(s5_b_im.shape),
        d_c(dcbd_re), d_c(dcbd_im),
        dd_skip.reshape(1, d),
        dgn_kv.reshape(d),
        dkg.reshape(d // HEAD_DIM, HEAD_DIM).sum(axis=0),
        dqg.reshape(d // HEAD_DIM, HEAD_DIM).sum(axis=0)[None, :],
        loss_part[0, 0, :1],
    ], "small_grads")
    (g_mix_norm, g_mlp_norm, g_a_re, g_a_im, g_log_dt, g_b_re, g_b_im, g_c_re, g_c_im,
     g_skip_full, g_kv_norm, g_k_norm, g_q_norm, loss_all) = small
    loss = loss_all[0]
    g_s5_d = lax.dynamic_slice_in_dim(g_skip_full, dev * (d // N_DEV), d // N_DEV, axis=1)

    dm_mine = jnp.concatenate([
        dsh_a0, dsc_a0, dg_a0, dsh_m0, dsc_m0, dg_m0,
        dsh_a1, dsc_a1, dg_a1, dsh_m1, dsc_m1, dg_m1, dkv_sh, dkv_sc], axis=2).reshape(bsz, 2 * e_ada + e_kv)
    dm_all = all_gather(dm_mine, "gather_dmod").reshape(N_DEV * bsz, 2 * e_ada + e_kv)
    dm_cols = jnp.concatenate([
        lax.dynamic_slice_in_dim(dm_all, dev * n_ada, n_ada, axis=1),
        lax.dynamic_slice_in_dim(dm_all, e_ada + dev * n_ada, n_ada, axis=1),
        lax.dynamic_slice_in_dim(dm_all, 2 * e_ada + dev * n_kv, n_kv, axis=1)], axis=1)
    dw_cols, db_all = ada_bwd(c_all, dm_cols, dm_all)
    g_ada_w = jnp.stack([dw_cols[:, :n_ada], dw_cols[:, n_ada:2 * n_ada]])
    g_kv_ada_w = dw_cols[:, 2 * n_ada:]
    g_ada_b = db_all[0, :2 * e_ada].reshape(2, e_ada)
    g_kv_ada_b = db_all[0, 2 * e_ada:]

    landed = lambda handle, name: slab_sum(exchange_wait(handle, dx0, name + "_wait"), name + "_sum")
    g_w1 = jnp.stack([landed(dw1_0, "rs_w1_0"), landed(dw1_1, "rs_w1_1")])
    g_w2 = jnp.stack([landed(dw2_0, "rs_w2_0"), landed(dw2_1, "rs_w2_1")])
    g_glu = landed(dwglu, "rs_glu")[None]
    g_wkv = landed(dwkv, "rs_wkv")
    g_wq = landed(dwq, "rs_wq")[None]
    g_wo = landed(dwo, "rs_wo")[None]

    weights = [ada_w, ada_b, mix_norm_g, mlp_norm_g, mlp_w1, mlp_w2, s5_a_re, s5_a_im, s5_log_dt, s5_b_re,
               s5_b_im, s5_c_re, s5_c_im, s5_d, s5_w_glu, kv_ada_w, kv_ada_b, kv_norm_g, w_kv, k_norm_g,
               sb_w_q, q_norm_g, sb_w_o]
    grads = [g_ada_w, g_ada_b, g_mix_norm, g_mlp_norm, g_w1, g_w2, g_a_re, g_a_im, g_log_dt, g_b_re,
             g_b_im, g_c_re, g_c_im, g_s5_d, g_glu, g_kv_ada_w, g_kv_ada_b, g_kv_norm, g_wkv, g_k_norm,
             g_wq, g_q_norm, g_wo]
    ms = [m_ada_w, m_ada_b, m_mix_norm_g, m_mlp_norm_g, m_mlp_w1, m_mlp_w2, m_s5_a_re, m_s5_a_im, m_s5_log_dt,
          m_s5_b_re, m_s5_b_im, m_s5_c_re, m_s5_c_im, m_s5_d, m_s5_w_glu, m_kv_ada_w, m_kv_ada_b, m_kv_norm_g,
          m_w_kv, m_k_norm_g, m_sb_w_q, m_q_norm_g, m_sb_w_o]
    vs = [v_ada_w, v_ada_b, v_mix_norm_g, v_mlp_norm_g, v_mlp_w1, v_mlp_w2, v_s5_a_re, v_s5_a_im, v_s5_log_dt,
          v_s5_b_re, v_s5_b_im, v_s5_c_re, v_s5_c_im, v_s5_d, v_s5_w_glu, v_kv_ada_w, v_kv_ada_b, v_kv_norm_g,
          v_w_kv, v_k_norm_g, v_sb_w_q, v_q_norm_g, v_sb_w_o]
    grads = [g.reshape(w.shape) for g, w in zip(grads, weights)]
    deltas, new_ms, new_vs = [], [], []
    for i, (w, g, m, v) in enumerate(zip(weights, grads, ms, vs)):
        dl, nm, nv = adamw(w, g, m, v, f"adamw_{i}")
        deltas.append(dl)
        new_ms.append(nm)
        new_vs.append(nv)
    return (loss, dx0, *grads, *deltas, *new_ms, *new_vs)
```
